```python
import jax, jax.numpy as jnp
from jax import lax
import numpy as np

D_MODEL = 2048
BATCH = 8
SEQ = 8192
DEPTH = 2

GRID_W = 64
CTX_LEN = 256

HEAD_DIM = 128
N_Q_HEADS = 8
N_KV_HEADS = 2
Q_PER_KV = N_Q_HEADS // N_KV_HEADS
ATTN_W = N_Q_HEADS * HEAD_DIM
KV_W = N_KV_HEADS * HEAD_DIM
Q_BLOCK = 128
ATTN_SCALE = HEAD_DIM ** -0.5
ROPE_THETA = 10000.0
AXIS_DIM = HEAD_DIM // 2
N_FREQ = AXIS_DIM // 2

CONV_CH = 1024
CONV_WIDTH = 31

CHUNK = 128
SGU_W = D_MODEL
SGU_GROUPS = 8
SGU_GW = SGU_W // SGU_GROUPS

EV_SPLITS = [KV_W, 2 * KV_W, 2 * KV_W + ATTN_W, 2 * KV_W + 2 * ATTN_W,
             2 * KV_W + 2 * ATTN_W + 2 * CONV_CH]
EV_IN = 2 * KV_W + 2 * ATTN_W + 3 * CONV_CH
EV_MIX = ATTN_W + CONV_CH
OD_IN = 3 * SGU_W

N_EVEN = (DEPTH + 1) // 2
N_ODD = DEPTH // 2
EPS = 1e-6

kernel_name = "hybrid_gqa_conformer_sgu_prefix_block"


def rmsnorm(x, g):
    xf = x.astype(jnp.float32)
    y = xf * lax.rsqrt(jnp.mean(xf * xf, axis=-1, keepdims=True) + EPS)
    return (y * g.astype(jnp.float32)).astype(x.dtype)


def layernorm(x, g, b):
    xf = x.astype(jnp.float32)
    mu = jnp.mean(xf, axis=-1, keepdims=True)
    var = jnp.mean(jnp.square(xf - mu), axis=-1, keepdims=True)
    y = (xf - mu) * lax.rsqrt(var + EPS)
    return (y * g.astype(jnp.float32) + b.astype(jnp.float32)).astype(x.dtype)


def modulate(h, shift, scale):
    return h * (1.0 + scale) + shift


def axial_rope_tables(n):
    rows = n // GRID_W
    row = jnp.repeat(jnp.arange(rows, dtype=jnp.float32), GRID_W)
    col = jnp.tile(jnp.arange(GRID_W, dtype=jnp.float32), rows)
    inv = jnp.power(ROPE_THETA, jnp.arange(N_FREQ, dtype=jnp.float32) * (-2.0 / AXIS_DIM))
    ang = jnp.concatenate([row[:, None] * inv, col[:, None] * inv], axis=-1)
    return jnp.cos(ang), jnp.sin(ang)


def apply_rope(x, cos, sin):
    shp = x.shape
    xf = x.astype(jnp.float32).reshape(shp[:-1] + (HEAD_DIM // 2, 2))
    x1, x2 = xf[..., 0], xf[..., 1]
    cs, sn = cos[None, :, None, :], sin[None, :, None, :]
    out = jnp.stack([x1 * cs - x2 * sn, x1 * sn + x2 * cs], axis=-1)
    return out.reshape(shp).astype(x.dtype)


def latent_attention(q, k_lat, v_lat, k_ctx, v_ctx):
    bsz, n = q.shape[:2]
    k_all = jnp.concatenate([k_lat, k_ctx], axis=1)
    v_all = jnp.concatenate([v_lat, v_ctx], axis=1)
    nb = n // Q_BLOCK
    qb = q.reshape(bsz, nb, Q_BLOCK, N_KV_HEADS, Q_PER_KV, HEAD_DIM).transpose(1, 0, 2, 3, 4, 5)

    def block(qi):
        s = jnp.einsum('bqhgd,bkhd->bhgqk', qi, k_all).astype(jnp.float32) * ATTN_SCALE
        p = jax.nn.softmax(s, axis=-1)
        return jnp.einsum('bhgqk,bkhd->bqhgd', p.astype(v_all.dtype), v_all)

    out = lax.map(block, qb)
    return out.transpose(1, 0, 2, 3, 4, 5).reshape(bsz, n, ATTN_W)


def context_attention(q, k, v):
    bsz, l = q.shape[:2]
    qg = q.reshape(bsz, l, N_KV_HEADS, Q_PER_KV, HEAD_DIM)
    s = jnp.einsum('bqhgd,bkhd->bhgqk', qg, k).astype(jnp.float32) * ATTN_SCALE
    p = jax.nn.softmax(s, axis=-1)
    o = jnp.einsum('bhgqk,bkhd->bqhgd', p.astype(v.dtype), v)
    return o.reshape(bsz, l, ATTN_W)


def conformer_conv(pair, dw_w, dw_b, ln_g, ln_b):
    a, b = jnp.split(pair, 2, axis=-1)
    y = a * jax.nn.sigmoid(b)
    y = lax.conv_general_dilated(
        y, dw_w[:, None, :], window_strides=(1,),
        padding=[(CONV_WIDTH // 2, CONV_WIDTH // 2)],
        dimension_numbers=('NWC', 'WIO', 'NWC'),
        feature_group_count=CONV_CH) + dw_b
    return jax.nn.silu(layernorm(y, ln_g, ln_b))


def spatial_gating(uv, ln_g, ln_b, ws, bs):
    bsz, n = uv.shape[:2]
    u, v = jnp.split(jax.nn.gelu(uv), 2, axis=-1)
    v = layernorm(v, ln_g, ln_b).reshape(bsz, n // CHUNK, CHUNK, SGU_GROUPS, SGU_GW)
    mixed = jnp.einsum('gij,bcjgd->bcigd', ws, v) + bs.T[None, None, :, :, None]
    return u * mixed.reshape(bsz, n, SGU_W)


def _fwd_setup_inputs(seed: int = 0) -> dict:
    key = jax.random.key(seed)
    ks = iter(jax.random.split(key, 32))

    def nrm(shape, scale):
        return jax.random.normal(next(ks), shape, jnp.float32) * scale

    d = D_MODEL
    return {
        "x": nrm((BATCH, SEQ, d), 1.0),
        "c": nrm((BATCH, d), 1.0),
        "ctx": nrm((BATCH, CTX_LEN, d), 1.0),
        "c_ctx": nrm((d,), 1.0),
        "ada_w": nrm((DEPTH, d, 3 * d), 0.5 * d ** -0.5),
        "ada_b": nrm((DEPTH, 3 * d), 0.02),
        "norm_g": 1.0 + nrm((DEPTH, d), 0.02),
        "ev_w_in": nrm((N_EVEN, d, EV_IN), d ** -0.5),
        "ev_q_norm": 1.0 + nrm((N_EVEN, HEAD_DIM), 0.02),
        "ev_k_norm": 1.0 + nrm((N_EVEN, HEAD_DIM), 0.02),
        "ev_dw_w": nrm((N_EVEN, CONV_WIDTH, CONV_CH), CONV_WIDTH ** -0.5),
        "ev_dw_b": nrm((N_EVEN, CONV_CH), 0.02),
        "ev_ln_g": 1.0 + nrm((N_EVEN, CONV_CH), 0.02),
        "ev_ln_b": nrm((N_EVEN, CONV_CH), 0.02),
        "ev_w_out": nrm((N_EVEN, EV_MIX, d), EV_MIX ** -0.5),
        "od_w_in": nrm((N_ODD, d, OD_IN), d ** -0.5),
        "od_ln_g": 1.0 + nrm((N_ODD, SGU_W), 0.02),
        "od_ln_b": nrm((N_ODD, SGU_W), 0.02),
        "od_ws": nrm((N_ODD, SGU_GROUPS, CHUNK, CHUNK), CHUNK ** -0.5),
        "od_bs": 1.0 + nrm((N_ODD, SGU_GROUPS, CHUNK), 0.02),
        "od_w_out": nrm((N_ODD, SGU_W, d), SGU_W ** -0.5),
        "final_g": 1.0 + nrm((d,), 0.02),
    }


def _fwd_reference(x, c, ctx, c_ctx, ada_w, ada_b, norm_g, ev_w_in, ev_q_norm, ev_k_norm,
              ev_dw_w, ev_dw_b, ev_ln_g, ev_ln_b, ev_w_out, od_w_in, od_ln_g, od_ln_b,
              od_ws, od_bs, od_w_out, final_g):
    bsz, n, _ = x.shape
    cos, sin = axial_rope_tables(n)
    sc = jax.nn.silu(c)
    scc = jax.nn.silu(c_ctx)
    xc = ctx
    lc = ctx.shape[1]
    for layer in range(DEPTH):
        ctx_needed = any(j % 2 == 0 for j in range(layer + 1, DEPTH))
        is_even = layer % 2 == 0
        mod = sc @ ada_w[layer] + ada_b[layer]
        shift, scale, gate = jnp.split(mod[:, None, :], 3, axis=-1)
        h = modulate(rmsnorm(x, norm_g[layer]), shift, scale)
        if is_even or ctx_needed:
            n_mod = 3 * D_MODEL if ctx_needed else 2 * D_MODEL
            mod_c = (scc @ ada_w[layer][:, :n_mod] + ada_b[layer][:n_mod])[None, None, :]
            hc = modulate(rmsnorm(xc, norm_g[layer]),
                          mod_c[..., :D_MODEL], mod_c[..., D_MODEL:2 * D_MODEL])
        if is_even:
            e = layer // 2
            w_in = ev_w_in[e]
            k, v, q, za, glu, zb = jnp.split(h @ w_in, EV_SPLITS, axis=-1)
            q = apply_rope(rmsnorm(q.reshape(bsz, n, N_Q_HEADS, HEAD_DIM), ev_q_norm[e]), cos, sin)
            k = apply_rope(rmsnorm(k.reshape(bsz, n, N_KV_HEADS, HEAD_DIM), ev_k_norm[e]), cos, sin)
            v = v.reshape(bsz, n, N_KV_HEADS, HEAD_DIM)
            if ctx_needed:
                kc, vc, qc, zac, gluc, zbc = jnp.split(hc @ w_in, EV_SPLITS, axis=-1)
            else:
                kc, vc = jnp.split(hc @ w_in[:, :2 * KV_W], 2, axis=-1)
            kc = rmsnorm(kc.reshape(bsz, lc, N_KV_HEADS, HEAD_DIM), ev_k_norm[e])
            vc = vc.reshape(bsz, lc, N_KV_HEADS, HEAD_DIM)
            attn = latent_attention(q, k, v, kc, vc)
            conv = conformer_conv(glu, ev_dw_w[e], ev_dw_b[e], ev_ln_g[e], ev_ln_b[e])
            mix = jnp.concatenate([attn * jax.nn.silu(za), conv * jax.nn.silu(zb)], axis=-1)
            x_new = x + gate * (mix @ ev_w_out[e])
            if ctx_needed:
                qc = rmsnorm(qc.reshape(bsz, lc, N_Q_HEADS, HEAD_DIM), ev_q_norm[e])
                attn_c = context_attention(qc, kc, vc)
                conv_c = conformer_conv(gluc, ev_dw_w[e], ev_dw_b[e], ev_ln_g[e], ev_ln_b[e])
                mix_c = jnp.concatenate([attn_c * jax.nn.silu(zac), conv_c * jax.nn.silu(zbc)], axis=-1)
                xc = xc + mod_c[..., 2 * D_MODEL:] * (mix_c @ ev_w_out[e])
            x = x_new
        else:
            o = layer // 2
            p = h @ od_w_in[o]
            mixed = spatial_gating(p[..., :2 * SGU_W], od_ln_g[o], od_ln_b[o], od_ws[o], od_bs[o])
            mixed = mixed * jax.nn.silu(p[..., 2 * SGU_W:])
            x_new = x + gate * (mixed @ od_w_out[o])
            if ctx_needed:
                pc = hc @ od_w_in[o]
                mixed_c = spatial_gating(pc[..., :2 * SGU_W], od_ln_g[o], od_ln_b[o], od_ws[o], od_bs[o])
                mixed_c = mixed_c * jax.nn.silu(pc[..., 2 * SGU_W:])
                xc = xc + mod_c[..., 2 * D_MODEL:] * (mixed_c @ od_w_out[o])
            x = x_new
    return rmsnorm(x, final_g)


import jax as _jax
import jax.numpy as _jnp

TWIN_FORMAT = 'train_step'
FWD_PARAMS = ['x', 'c', 'ctx', 'c_ctx', 'ada_w', 'ada_b', 'norm_g', 'ev_w_in', 'ev_q_norm', 'ev_k_norm', 'ev_dw_w', 'ev_dw_b', 'ev_ln_g', 'ev_ln_b', 'ev_w_out', 'od_w_in', 'od_ln_g', 'od_ln_b', 'od_ws', 'od_bs', 'od_w_out', 'final_g']
TWIN_WEIGHTS = ['c_ctx', 'ada_w', 'ada_b', 'norm_g', 'ev_w_in', 'ev_q_norm', 'ev_k_norm', 'ev_dw_w', 'ev_dw_b', 'ev_ln_g', 'ev_ln_b', 'ev_w_out', 'od_w_in', 'od_ln_g', 'od_ln_b', 'od_ws', 'od_bs', 'od_w_out', 'final_g']
TWIN_DIFF_INPUT = 'x'
TWIN_INPUTS = ['x', 'c', 'ctx', 'c_ctx', 'ada_w', 'ada_b', 'norm_g', 'ev_w_in', 'ev_q_norm', 'ev_k_norm', 'ev_dw_w', 'ev_dw_b', 'ev_ln_g', 'ev_ln_b', 'ev_w_out', 'od_w_in', 'od_ln_g', 'od_ln_b', 'od_ws', 'od_bs', 'od_w_out', 'final_g', 'loss_target', 'm_c_ctx', 'm_ada_w', 'm_ada_b', 'm_norm_g', 'm_ev_w_in', 'm_ev_q_norm', 'm_ev_k_norm', 'm_ev_dw_w', 'm_ev_dw_b', 'm_ev_ln_g', 'm_ev_ln_b', 'm_ev_w_out', 'm_od_w_in', 'm_od_ln_g', 'm_od_ln_b', 'm_od_ws', 'm_od_bs', 'm_od_w_out', 'm_final_g', 'v_c_ctx', 'v_ada_w', 'v_ada_b', 'v_norm_g', 'v_ev_w_in', 'v_ev_q_norm', 'v_ev_k_norm', 'v_ev_dw_w', 'v_ev_dw_b', 'v_ev_ln_g', 'v_ev_ln_b', 'v_ev_w_out', 'v_od_w_in', 'v_od_ln_g', 'v_od_ln_b', 'v_od_ws', 'v_od_bs', 'v_od_w_out', 'v_final_g']
TWIN_OUTPUTS = ['loss', 'grad_x', 'grad_c_ctx', 'grad_ada_w', 'grad_ada_b', 'grad_norm_g', 'grad_ev_w_in', 'grad_ev_q_norm', 'grad_ev_k_norm', 'grad_ev_dw_w', 'grad_ev_dw_b', 'grad_ev_ln_g', 'grad_ev_ln_b', 'grad_ev_w_out', 'grad_od_w_in', 'grad_od_ln_g', 'grad_od_ln_b', 'grad_od_ws', 'grad_od_bs', 'grad_od_w_out', 'grad_final_g', 'delta_c_ctx', 'delta_ada_w', 'delta_ada_b', 'delta_norm_g', 'delta_ev_w_in', 'delta_ev_q_norm', 'delta_ev_k_norm', 'delta_ev_dw_w', 'delta_ev_dw_b', 'delta_ev_ln_g', 'delta_ev_ln_b', 'delta_ev_w_out', 'delta_od_w_in', 'delta_od_ln_g', 'delta_od_ln_b', 'delta_od_ws', 'delta_od_bs', 'delta_od_w_out', 'delta_final_g', 'new_m_c_ctx', 'new_m_ada_w', 'new_m_ada_b', 'new_m_norm_g', 'new_m_ev_w_in', 'new_m_ev_q_norm', 'new_m_ev_k_norm', 'new_m_ev_dw_w', 'new_m_ev_dw_b', 'new_m_ev_ln_g', 'new_m_ev_ln_b', 'new_m_ev_w_out', 'new_m_od_w_in', 'new_m_od_ln_g', 'new_m_od_ln_b', 'new_m_od_ws', 'new_m_od_bs', 'new_m_od_w_out', 'new_m_final_g', 'new_v_c_ctx', 'new_v_ada_w', 'new_v_ada_b', 'new_v_norm_g', 'new_v_ev_w_in', 'new_v_ev_q_norm', 'new_v_ev_k_norm', 'new_v_ev_dw_w', 'new_v_ev_dw_b', 'new_v_ev_ln_g', 'new_v_ev_ln_b', 'new_v_ev_w_out', 'new_v_od_w_in', 'new_v_od_ln_g', 'new_v_od_ln_b', 'new_v_od_ws', 'new_v_od_bs', 'new_v_od_w_out', 'new_v_final_g']
TWIN_LEAF_KINDS = {'loss': 'loss', 'grad_x': 'grad_x', 'grad_c_ctx': 'grad_w', 'grad_ada_w': 'grad_w', 'grad_ada_b': 'grad_w', 'grad_norm_g': 'grad_w', 'grad_ev_w_in': 'grad_w', 'grad_ev_q_norm': 'grad_w', 'grad_ev_k_norm': 'grad_w', 'grad_ev_dw_w': 'grad_w', 'grad_ev_dw_b': 'grad_w', 'grad_ev_ln_g': 'grad_w', 'grad_ev_ln_b': 'grad_w', 'grad_ev_w_out': 'grad_w', 'grad_od_w_in': 'grad_w', 'grad_od_ln_g': 'grad_w', 'grad_od_ln_b': 'grad_w', 'grad_od_ws': 'grad_w', 'grad_od_bs': 'grad_w', 'grad_od_w_out': 'grad_w', 'grad_final_g': 'grad_w', 'delta_c_ctx': 'delta_w', 'delta_ada_w': 'delta_w', 'delta_ada_b': 'delta_w', 'delta_norm_g': 'delta_w', 'delta_ev_w_in': 'delta_w', 'delta_ev_q_norm': 'delta_w', 'delta_ev_k_norm': 'delta_w', 'delta_ev_dw_w': 'delta_w', 'delta_ev_dw_b': 'delta_w', 'delta_ev_ln_g': 'delta_w', 'delta_ev_ln_b': 'delta_w', 'delta_ev_w_out': 'delta_w', 'delta_od_w_in': 'delta_w', 'delta_od_ln_g': 'delta_w', 'delta_od_ln_b': 'delta_w', 'delta_od_ws': 'delta_w', 'delta_od_bs': 'delta_w', 'delta_od_w_out': 'delta_w', 'delta_final_g': 'delta_w', 'new_m_c_ctx': 'new_m', 'new_m_ada_w': 'new_m', 'new_m_ada_b': 'new_m', 'new_m_norm_g': 'new_m', 'new_m_ev_w_in': 'new_m', 'new_m_ev_q_norm': 'new_m', 'new_m_ev_k_norm': 'new_m', 'new_m_ev_dw_w': 'new_m', 'new_m_ev_dw_b': 'new_m', 'new_m_ev_ln_g': 'new_m', 'new_m_ev_ln_b': 'new_m', 'new_m_ev_w_out': 'new_m', 'new_m_od_w_in': 'new_m', 'new_m_od_ln_g': 'new_m', 'new_m_od_ln_b': 'new_m', 'new_m_od_ws': 'new_m', 'new_m_od_bs': 'new_m', 'new_m_od_w_out': 'new_m', 'new_m_final_g': 'new_m', 'new_v_c_ctx': 'new_v', 'new_v_ada_w': 'new_v', 'new_v_ada_b': 'new_v', 'new_v_norm_g': 'new_v', 'new_v_ev_w_in': 'new_v', 'new_v_ev_q_norm': 'new_v', 'new_v_ev_k_norm': 'new_v', 'new_v_ev_dw_w': 'new_v', 'new_v_ev_dw_b': 'new_v', 'new_v_ev_ln_g': 'new_v', 'new_v_ev_ln_b': 'new_v', 'new_v_ev_w_out': 'new_v', 'new_v_od_w_in': 'new_v', 'new_v_od_ln_g': 'new_v', 'new_v_od_ln_b': 'new_v', 'new_v_od_ws': 'new_v', 'new_v_od_bs': 'new_v', 'new_v_od_w_out': 'new_v', 'new_v_final_g': 'new_v'}


def _forward(args):
    return _fwd_reference(*[args[k] for k in FWD_PARAMS])


def _output_shape():
    def fwd():
        inp = _fwd_setup_inputs(0)
        return _fwd_reference(*[inp[k] for k in FWD_PARAMS])
    out = _jax.eval_shape(fwd)
    return out.shape, out.dtype

N_MICROBATCH = 1
ADAM_LR = 0.001
ADAM_B1 = 0.9
ADAM_B2 = 0.999
ADAM_EPS = 1e-08
ADAM_WD = 0.01
ADAM_STEP = 10
PER_EXAMPLE_BATCH_AXIS = {'x': 0, 'c': 0, 'ctx': 0, 'loss_target': 0}
SHARED_INPUTS = []
_WEIGHT_DTYPES = {'c_ctx': _jnp.float32, 'ada_w': _jnp.float32, 'ada_b': _jnp.float32, 'norm_g': _jnp.float32, 'ev_w_in': _jnp.float32, 'ev_q_norm': _jnp.float32, 'ev_k_norm': _jnp.float32, 'ev_dw_w': _jnp.float32, 'ev_dw_b': _jnp.float32, 'ev_ln_g': _jnp.float32, 'ev_ln_b': _jnp.float32, 'ev_w_out': _jnp.float32, 'od_w_in': _jnp.float32, 'od_ln_g': _jnp.float32, 'od_ln_b': _jnp.float32, 'od_ws': _jnp.float32, 'od_bs': _jnp.float32, 'od_w_out': _jnp.float32, 'final_g': _jnp.float32}
MOMENT_SCALE = {'c_ctx': 2.891037e-03, 'ada_w': 2.797834e-02, 'ada_b': 4.735873e-02, 'norm_g': 3.207323e-02, 'ev_w_in': 1.102274e-02, 'ev_q_norm': 6.117946e-03, 'ev_k_norm': 6.009314e-03, 'ev_dw_w': 1.586577e-02, 'ev_dw_b': 2.868988e-02, 'ev_ln_g': 1.887055e-02, 'ev_ln_b': 1.664912e-02, 'ev_w_out': 1.210737e-02, 'od_w_in': 2.342886e-02, 'od_ln_g': 1.717574e-02, 'od_ln_b': 1.744522e-02, 'od_ws': 2.419382e-02, 'od_bs': 2.510256e-02, 'od_w_out': 2.418769e-02, 'final_g': 3.198811e+01}


def _to_microbatches(a, axis):
    t = _jnp.moveaxis(a, axis, 0)
    t = t.reshape((N_MICROBATCH, t.shape[0] // N_MICROBATCH) + t.shape[1:])
    return _jnp.moveaxis(t, 1, axis + 1)


def setup_inputs(seed: int = 0) -> dict:
    inp = _fwd_setup_inputs(seed)
    key = _jax.random.fold_in(_jax.random.key(seed), 7919)
    shape, _ = _output_shape()
    out = dict(inp)
    out["loss_target"] = _jax.random.normal(_jax.random.fold_in(key, 0), shape, _jnp.float32)
    for i, name in enumerate(TWIN_WEIGHTS):
        w = inp[name].astype(_jnp.float32)
        if MOMENT_SCALE is None:
            s = _jnp.sqrt(_jnp.mean(_jnp.square(w)) + 1e-30)
        else:
            s = MOMENT_SCALE[name]
        km, kv = _jax.random.split(_jax.random.fold_in(key, i + 1))
        out[name] = w
        out["m_" + name] = s * _jax.random.normal(km, w.shape, _jnp.float32)
        out["v_" + name] = (s * s) * _jax.random.uniform(kv, w.shape, _jnp.float32, 0.5, 1.5)
    if N_MICROBATCH > 1:
        for name, axis in PER_EXAMPLE_BATCH_AXIS.items():
            out[name] = _to_microbatches(out[name], axis)
    return {'x': out['x'], 'c': out['c'], 'ctx': out['ctx'], 'c_ctx': out['c_ctx'], 'ada_w': out['ada_w'], 'ada_b': out['ada_b'], 'norm_g': out['norm_g'], 'ev_w_in': out['ev_w_in'], 'ev_q_norm': out['ev_q_norm'], 'ev_k_norm': out['ev_k_norm'], 'ev_dw_w': out['ev_dw_w'], 'ev_dw_b': out['ev_dw_b'], 'ev_ln_g': out['ev_ln_g'], 'ev_ln_b': out['ev_ln_b'], 'ev_w_out': out['ev_w_out'], 'od_w_in': out['od_w_in'], 'od_ln_g': out['od_ln_g'], 'od_ln_b': out['od_ln_b'], 'od_ws': out['od_ws'], 'od_bs': out['od_bs'], 'od_w_out': out['od_w_out'], 'final_g': out['final_g'], 'loss_target': out['loss_target'], 'm_c_ctx': out['m_c_ctx'], 'm_ada_w': out['m_ada_w'], 'm_ada_b': out['m_ada_b'], 'm_norm_g': out['m_norm_g'], 'm_ev_w_in': out['m_ev_w_in'], 'm_ev_q_norm': out['m_ev_q_norm'], 'm_ev_k_norm': out['m_ev_k_norm'], 'm_ev_dw_w': out['m_ev_dw_w'], 'm_ev_dw_b': out['m_ev_dw_b'], 'm_ev_ln_g': out['m_ev_ln_g'], 'm_ev_ln_b': out['m_ev_ln_b'], 'm_ev_w_out': out['m_ev_w_out'], 'm_od_w_in': out['m_od_w_in'], 'm_od_ln_g': out['m_od_ln_g'], 'm_od_ln_b': out['m_od_ln_b'], 'm_od_ws': out['m_od_ws'], 'm_od_bs': out['m_od_bs'], 'm_od_w_out': out['m_od_w_out'], 'm_final_g': out['m_final_g'], 'v_c_ctx': out['v_c_ctx'], 'v_ada_w': out['v_ada_w'], 'v_ada_b': out['v_ada_b'], 'v_norm_g': out['v_norm_g'], 'v_ev_w_in': out['v_ev_w_in'], 'v_ev_q_norm': out['v_ev_q_norm'], 'v_ev_k_norm': out['v_ev_k_norm'], 'v_ev_dw_w': out['v_ev_dw_w'], 'v_ev_dw_b': out['v_ev_dw_b'], 'v_ev_ln_g': out['v_ev_ln_g'], 'v_ev_ln_b': out['v_ev_ln_b'], 'v_ev_w_out': out['v_ev_w_out'], 'v_od_w_in': out['v_od_w_in'], 'v_od_ln_g': out['v_od_ln_g'], 'v_od_ln_b': out['v_od_ln_b'], 'v_od_ws': out['v_od_ws'], 'v_od_bs': out['v_od_bs'], 'v_od_w_out': out['v_od_w_out'], 'v_final_g': out['v_final_g']}


def _loss(weights, diff, rest, loss_target):
    with _jax.named_scope("forward"):
        args = {**rest, TWIN_DIFF_INPUT: diff, **{k: w.astype(_WEIGHT_DTYPES[k]) for k, w in weights.items()}}
        y = _forward(args)
    with _jax.named_scope("loss_head"):
        err = _jnp.square(y.astype(_jnp.float32) - loss_target)
        return 0.5 * _jnp.sum(_jnp.mean(err, axis=-1)) if err.ndim else 0.5 * err


def _adamw(w, g, m, v):
    m = ADAM_B1 * m + (1.0 - ADAM_B1) * g
    v = ADAM_B2 * v + (1.0 - ADAM_B2) * _jnp.square(g)
    m_hat = m / (1.0 - ADAM_B1 ** ADAM_STEP)
    v_hat = v / (1.0 - ADAM_B2 ** ADAM_STEP)
    delta = -ADAM_LR * (m_hat / (_jnp.sqrt(v_hat) + ADAM_EPS) + ADAM_WD * w)
    return delta, m, v


def reference(x, c, ctx, c_ctx, ada_w, ada_b, norm_g, ev_w_in, ev_q_norm, ev_k_norm, ev_dw_w, ev_dw_b, ev_ln_g, ev_ln_b, ev_w_out, od_w_in, od_ln_g, od_ln_b, od_ws, od_bs, od_w_out, final_g, loss_target, m_c_ctx, m_ada_w, m_ada_b, m_norm_g, m_ev_w_in, m_ev_q_norm, m_ev_k_norm, m_ev_dw_w, m_ev_dw_b, m_ev_ln_g, m_ev_ln_b, m_ev_w_out, m_od_w_in, m_od_ln_g, m_od_ln_b, m_od_ws, m_od_bs, m_od_w_out, m_final_g, v_c_ctx, v_ada_w, v_ada_b, v_norm_g, v_ev_w_in, v_ev_q_norm, v_ev_k_norm, v_ev_dw_w, v_ev_dw_b, v_ev_ln_g, v_ev_ln_b, v_ev_w_out, v_od_w_in, v_od_ln_g, v_od_ln_b, v_od_ws, v_od_bs, v_od_w_out, v_final_g):
    given = dict(x=x, c=c, ctx=ctx, c_ctx=c_ctx, ada_w=ada_w, ada_b=ada_b, norm_g=norm_g, ev_w_in=ev_w_in, ev_q_norm=ev_q_norm, ev_k_norm=ev_k_norm, ev_dw_w=ev_dw_w, ev_dw_b=ev_dw_b, ev_ln_g=ev_ln_g, ev_ln_b=ev_ln_b, ev_w_out=ev_w_out, od_w_in=od_w_in, od_ln_g=od_ln_g, od_ln_b=od_ln_b, od_ws=od_ws, od_bs=od_bs, od_w_out=od_w_out, final_g=final_g, loss_target=loss_target, m_c_ctx=m_c_ctx, m_ada_w=m_ada_w, m_ada_b=m_ada_b, m_norm_g=m_norm_g, m_ev_w_in=m_ev_w_in, m_ev_q_norm=m_ev_q_norm, m_ev_k_norm=m_ev_k_norm, m_ev_dw_w=m_ev_dw_w, m_ev_dw_b=m_ev_dw_b, m_ev_ln_g=m_ev_ln_g, m_ev_ln_b=m_ev_ln_b, m_ev_w_out=m_ev_w_out, m_od_w_in=m_od_w_in, m_od_ln_g=m_od_ln_g, m_od_ln_b=m_od_ln_b, m_od_ws=m_od_ws, m_od_bs=m_od_bs, m_od_w_out=m_od_w_out, m_final_g=m_final_g, v_c_ctx=v_c_ctx, v_ada_w=v_ada_w, v_ada_b=v_ada_b, v_norm_g=v_norm_g, v_ev_w_in=v_ev_w_in, v_ev_q_norm=v_ev_q_norm, v_ev_k_norm=v_ev_k_norm, v_ev_dw_w=v_ev_dw_w, v_ev_dw_b=v_ev_dw_b, v_ev_ln_g=v_ev_ln_g, v_ev_ln_b=v_ev_ln_b, v_ev_w_out=v_ev_w_out, v_od_w_in=v_od_w_in, v_od_ln_g=v_od_ln_g, v_od_ln_b=v_od_ln_b, v_od_ws=v_od_ws, v_od_bs=v_od_bs, v_od_w_out=v_od_w_out, v_final_g=v_final_g)
    weights = {n: given[n] for n in TWIN_WEIGHTS}
    shared = {n: given[n] for n in SHARED_INPUTS}
    per_example = {n: given[n] for n in ['x', 'c', 'ctx']}
    grad_fn = _jax.value_and_grad(_loss, argnums=(0, 1))

    def one_microbatch(ex, loss_target):
        ex = dict(ex)
        diff = ex.pop(TWIN_DIFF_INPUT)
        return grad_fn(weights, diff, {**shared, **ex}, loss_target)

    if N_MICROBATCH == 1:
        loss, (grad_w, grad_x) = one_microbatch(per_example, given["loss_target"])
    else:
        def body(carry, xs):
            loss_sum, grad_sum = carry
            l_k, (gw_k, gx_k) = one_microbatch(xs[0], xs[1])
            with _jax.named_scope("update"):
                return (loss_sum + l_k, _jax.tree.map(_jnp.add, grad_sum, gw_k)), gx_k

        init = (_jnp.zeros((), _jnp.float32), _jax.tree.map(_jnp.zeros_like, weights))
        (loss, grad_w), grad_x = _jax.lax.scan(body, init, (per_example, given["loss_target"]))
    with _jax.named_scope("update"):
        delta_w, new_m, new_v = {}, {}, {}
        for n in TWIN_WEIGHTS:
            delta_w[n], new_m[n], new_v[n] = _adamw(weights[n], grad_w[n], given["m_" + n], given["v_" + n])
    return (loss, grad_x, *[grad_w[n] for n in TWIN_WEIGHTS], *[delta_w[n] for n in TWIN_WEIGHTS],
            *[new_m[n] for n in TWIN_WEIGHTS], *[new_v[n] for n in TWIN_WEIGHTS])
```

```python
import math

import jax
import jax.numpy as jnp
from jax import lax
from jax.experimental import pallas as pl
from jax.experimental.pallas import tpu as pltpu

F32 = jnp.float32
BF16 = jnp.bfloat16
EPS = 1e-6
GRID_W = 64
ROPE_THETA = 10000.0
HEAD_DIM = 128
N_KV_HEADS = 2
CONV_HALO = 16
LANES = 128
N_DEV = 8
N_CHIP = 4
MOD_ROWS = 16
ADAM_LR, ADAM_B1, ADAM_B2, ADAM_EPS, ADAM_WD, ADAM_STEP = 0.001, 0.9, 0.999, 1e-08, 0.01, 10
VMEM_LIMIT = 56 * 1024 * 1024
MESH = pl.DeviceIdType.MESH
ANY = pl.BlockSpec(memory_space=pl.ANY)
VMEM_SPEC = pl.BlockSpec(memory_space=pltpu.VMEM)
CHIP_DELTAS = ((1, 0), (0, 1), (1, 1))


def _cp(sem=None):
    return pltpu.CompilerParams(dimension_semantics=sem, vmem_limit_bytes=VMEM_LIMIT)


def _pick(n, cands):
    for c in cands:
        if n % c == 0:
            return c
    raise ValueError(f"no tile for {n} in {cands}")


def _sigmoid(x):
    return 1.0 / (1.0 + jnp.exp(-x))


def _silu(x):
    return x * _sigmoid(x)


def _dsilu(x):
    s = _sigmoid(x)
    return s * (1.0 + x * (1.0 - s))


_GELU_C = math.sqrt(2.0 / math.pi)


def _gelu(x):
    return 0.5 * x * (1.0 + jnp.tanh(_GELU_C * (x + 0.044715 * x * x * x)))


def _dgelu(x):
    t = jnp.tanh(_GELU_C * (x + 0.044715 * x * x * x))
    return 0.5 * (1.0 + t) + 0.5 * x * (1.0 - t * t) * _GELU_C * (1.0 + 3.0 * 0.044715 * x * x)


def _vec(d):
    return pl.BlockSpec((1, d), lambda *_: (0, 0))


def _cat(refs):
    return refs[0][...] if len(refs) == 1 else jnp.concatenate([r[...] for r in refs], axis=1)


def _col_specs(rows, off, width, cb, row_map):
    assert off % cb == 0 and width % cb == 0
    return [pl.BlockSpec((rows, cb), (lambda *g, _c=off // cb + t: (row_map(*g), _c))) for t in range(width // cb)]


def _my_pos():
    return lax.axis_index("x"), lax.axis_index("y"), lax.axis_index("c")


def _allgather_small(x, name):
    r, c = x.shape

    def body(x_ref, out_ref, send_sems, recv_sems, local_sem):
        mx, my, mc = _my_pos()
        me = 4 * mx + 2 * my + mc
        mine = pltpu.make_async_copy(x_ref, out_ref.at[me], local_sem)
        mine.start()
        deltas = [(dx, dy, dc) for dx in (0, 1) for dy in (0, 1) for dc in (0, 1) if (dx, dy, dc) != (0, 0, 0)]
        sends = []
        for k, (dx, dy, dc) in enumerate(deltas):
            px, py, pc = (mx + dx) % 2, (my + dy) % 2, (mc + dc) % 2
            cp = pltpu.make_async_remote_copy(
                src_ref=x_ref, dst_ref=out_ref.at[me], send_sem=send_sems.at[k], recv_sem=recv_sems.at[k],
                device_id=(px, py, pc), device_id_type=MESH)
            cp.start()
            sends.append(cp)
        for k, (dx, dy, dc) in enumerate(deltas):
            px, py, pc = (mx + dx) % 2, (my + dy) % 2, (mc + dc) % 2
            peer = 4 * px + 2 * py + pc
            pltpu.make_async_remote_copy(
                src_ref=x_ref, dst_ref=out_ref.at[peer], send_sem=send_sems.at[k], recv_sem=recv_sems.at[k],
                device_id=(px, py, pc), device_id_type=MESH).wait_recv()
        for cp in sends:
            cp.wait_send()
        mine.wait()

    return pl.pallas_call(
        body, name=name,
        out_shape=jax.ShapeDtypeStruct((N_DEV, r, c), x.dtype),
        in_specs=[VMEM_SPEC], out_specs=VMEM_SPEC,
        scratch_shapes=[pltpu.SemaphoreType.DMA((N_DEV - 1,)), pltpu.SemaphoreType.DMA((N_DEV - 1,)),
                        pltpu.SemaphoreType.DMA],
        compiler_params=pltpu.CompilerParams(vmem_limit_bytes=VMEM_LIMIT),
    )(x)


class _Sharded:
    def __init__(self, full_shape, by_cols):
        self.full = full_shape
        self.by_cols = by_cols
        rows, cols = full_shape
        if by_cols:
            self.shard = (rows, cols // N_CHIP)
        else:
            self.shard = (rows // N_CHIP, cols)
        self.half = (self.shard[0] // 2, self.shard[1])
        self.halves = (rows // 2, cols)

    def in_full(self, ref, s, h):
        hr = self.half[0]
        if self.by_cols:
            return ref.at[pl.ds(h * hr, hr), pl.ds(s * self.shard[1], self.shard[1])]
        return ref.at[pl.ds(s * self.shard[0] + h * hr, hr), :]

    def shard_in_full(self, ref, s):
        if self.by_cols:
            return ref.at[:, pl.ds(s * self.shard[1], self.shard[1])]
        return ref.at[pl.ds(s * self.shard[0], self.shard[0]), :]

    def in_shard(self, ref, h):
        hr = self.half[0]
        return ref.at[pl.ds(h * hr, hr), :]

    def in_halves(self, ref, s):
        hr = self.half[0]
        if self.by_cols:
            return ref.at[:, pl.ds(s * self.shard[1], self.shard[1])]
        return ref.at[pl.ds(s * hr, hr), :]


def _gather_weights(shards, layouts):
    n = len(shards)

    def body(*refs):
        ins, outs = refs[:n], refs[n:2 * n]
        send_sems, recv_sems, local_sems = refs[2 * n:]
        mx, my, mc = _my_pos()
        chip = 2 * mx + my
        sibling = (mx, my, 1 - mc)
        locals_, sends = [], []
        for a, lay in enumerate(layouts):
            cp = pltpu.make_async_copy(ins[a], lay.shard_in_full(outs[a], chip), local_sems.at[a])
            cp.start()
            locals_.append(cp)
            for j, (dx, dy) in enumerate(CHIP_DELTAS):
                px, py = (mx + dx) % 2, (my + dy) % 2
                cp = pltpu.make_async_remote_copy(
                    src_ref=lay.in_shard(ins[a], mc), dst_ref=lay.in_full(outs[a], chip, mc),
                    send_sem=send_sems.at[a, j], recv_sem=recv_sems.at[a, j],
                    device_id=(px, py, mc), device_id_type=MESH)
                cp.start()
                sends.append(cp)
        for a, lay in enumerate(layouts):
            for j, (dx, dy) in enumerate(CHIP_DELTAS):
                px, py = (mx + dx) % 2, (my + dy) % 2
                other = 2 * px + py
                region = lay.in_full(outs[a], other, mc)
                pltpu.make_async_remote_copy(
                    src_ref=region, dst_ref=region, send_sem=send_sems.at[a, j], recv_sem=recv_sems.at[a, j],
                    device_id=(px, py, mc), device_id_type=MESH).wait_recv()
                cp = pltpu.make_async_remote_copy(
                    src_ref=region, dst_ref=region, send_sem=send_sems.at[a, 3 + j], recv_sem=recv_sems.at[a, 3 + j],
                    device_id=sibling, device_id_type=MESH)
                cp.start()
                sends.append(cp)
        for a, lay in enumerate(layouts):
            for j, (dx, dy) in enumerate(CHIP_DELTAS):
                other = 2 * ((mx + dx) % 2) + (my + dy) % 2
                region = lay.in_full(outs[a], other, 1 - mc)
                pltpu.make_async_remote_copy(
                    src_ref=region, dst_ref=region, send_sem=send_sems.at[a, 3 + j], recv_sem=recv_sems.at[a, 3 + j],
                    device_id=sibling, device_id_type=MESH).wait_recv()
        for cp in sends:
            cp.wait_send()
        for cp in locals_:
            cp.wait()

    return pl.pallas_call(
        body, name="gather_weights",
        out_shape=[jax.ShapeDtypeStruct(lay.full, BF16) for lay in layouts],
        in_specs=[ANY] * n, out_specs=[ANY] * n,
        scratch_shapes=[pltpu.SemaphoreType.DMA((n, 6)), pltpu.SemaphoreType.DMA((n, 6)),
                        pltpu.SemaphoreType.DMA((n,))],
    )(*shards)


def _pair_exchange(grads, layouts):
    n = len(grads)

    def body(*refs):
        ins, mine_out, theirs_out = refs[:n], refs[n:2 * n], refs[2 * n:3 * n]
        send_sems, recv_sems, local_sems = refs[3 * n:]
        mx, my, mc = _my_pos()
        sibling = (mx, my, 1 - mc)
        pending = []
        for a, lay in enumerate(layouts):
            for s in range(N_CHIP):
                cp = pltpu.make_async_copy(lay.in_full(ins[a], s, mc), lay.in_halves(mine_out[a], s),
                                           local_sems.at[a, s])
                cp.start()
                pending.append(cp)
                cp = pltpu.make_async_remote_copy(
                    src_ref=lay.in_full(ins[a], s, 1 - mc), dst_ref=lay.in_halves(theirs_out[a], s),
                    send_sem=send_sems.at[a, s], recv_sem=recv_sems.at[a, s],
                    device_id=sibling, device_id_type=MESH)
                cp.start()
                pending.append(cp)
        for cp in pending:
            cp.wait()

    out_shape = [jax.ShapeDtypeStruct(lay.halves, F32) for lay in layouts] * 2
    return pl.pallas_call(
        body, name="grad_pair_exchange",
        out_shape=out_shape, in_specs=[ANY] * n, out_specs=[ANY] * (2 * n),
        scratch_shapes=[pltpu.SemaphoreType.DMA((n, N_CHIP)), pltpu.SemaphoreType.DMA((n, N_CHIP)),
                        pltpu.SemaphoreType.DMA((n, N_CHIP))],
    )(*grads)


def _chip_scatter(pair_sums, layouts):
    n = len(pair_sums)

    def body(*refs):
        ins, outs = refs[:n], refs[n:2 * n]
        send_sems, recv_sems, local_sems = refs[2 * n:]
        mx, my, mc = _my_pos()
        chip = 2 * mx + my
        pending, sends = [], []
        for a, lay in enumerate(layouts):
            cp = pltpu.make_async_copy(lay.in_halves(ins[a], chip), outs[a].at[chip], local_sems.at[a])
            cp.start()
            pending.append(cp)
            for j, (dx, dy) in enumerate(CHIP_DELTAS):
                px, py = (mx + dx) % 2, (my + dy) % 2
                other = 2 * px + py
                cp = pltpu.make_async_remote_copy(
                    src_ref=lay.in_halves(ins[a], other), dst_ref=outs[a].at[chip],
                    send_sem=send_sems.at[a, j], recv_sem=recv_sems.at[a, j],
                    device_id=(px, py, mc), device_id_type=MESH)
                cp.start()
                sends.append(cp)
        for a, lay in enumerate(layouts):
            for j, (dx, dy) in enumerate(CHIP_DELTAS):
                px, py = (mx + dx) % 2, (my + dy) % 2
                other = 2 * px + py
                pltpu.make_async_remote_copy(
                    src_ref=outs[a].at[other], dst_ref=outs[a].at[other],
                    send_sem=send_sems.at[a, j], recv_sem=recv_sems.at[a, j],
                    device_id=(px, py, mc), device_id_type=MESH).wait_recv()
        for cp in sends:
            cp.wait_send()
        for cp in pending:
            cp.wait()

    return pl.pallas_call(
        body, name="grad_chip_scatter",
        out_shape=[jax.ShapeDtypeStruct((N_CHIP,) + lay.half, F32) for lay in layouts],
        in_specs=[ANY] * n, out_specs=[ANY] * n,
        scratch_shapes=[pltpu.SemaphoreType.DMA((n, 3)), pltpu.SemaphoreType.DMA((n, 3)),
                        pltpu.SemaphoreType.DMA((n,))],
    )(*pair_sums)


def _pair_share(regions, layouts):
    n = len(regions)

    def body(*refs):
        ins, outs = refs[:n], refs[n:2 * n]
        send_sems, recv_sems, local_sems = refs[2 * n:]
        mx, my, mc = _my_pos()
        sibling = (mx, my, 1 - mc)
        pending = []
        for a, lay in enumerate(layouts):
            cp = pltpu.make_async_copy(ins[a], lay.in_shard(outs[a], mc), local_sems.at[a])
            cp.start()
            pending.append(cp)
            cp = pltpu.make_async_remote_copy(
                src_ref=ins[a], dst_ref=lay.in_shard(outs[a], mc),
                send_sem=send_sems.at[a], recv_sem=recv_sems.at[a], device_id=sibling, device_id_type=MESH)
            cp.start()
            pending.append(cp)
        for a, lay in enumerate(layouts):
            pltpu.make_async_remote_copy(
                src_ref=ins[a], dst_ref=lay.in_shard(outs[a], 1 - mc),
                send_sem=send_sems.at[a], recv_sem=recv_sems.at[a], device_id=sibling, device_id_type=MESH).wait_recv()
        for a in range(n):
            pending[2 * a + 1].wait_send()
            pending[2 * a].wait()

    return pl.pallas_call(
        body, name="grad_pair_share",
        out_shape=[jax.ShapeDtypeStruct(lay.shard, F32) for lay in layouts],
        in_specs=[ANY] * n, out_specs=[ANY] * n,
        scratch_shapes=[pltpu.SemaphoreType.DMA((n,)), pltpu.SemaphoreType.DMA((n,)), pltpu.SemaphoreType.DMA((n,))],
    )(*regions)


def _add2(a, b, name):
    r, c = a.shape
    tr = _pick(r, (256, 128, 64, 32, 16, 8))

    def body(a_ref, b_ref, o_ref):
        o_ref[...] = a_ref[...] + b_ref[...]

    spec = pl.BlockSpec((tr, c), lambda i: (i, 0))
    return pl.pallas_call(body, name=name, grid=(r // tr,), in_specs=[spec, spec], out_specs=spec,
                          out_shape=jax.ShapeDtypeStruct((r, c), F32), compiler_params=_cp(("parallel",)))(a, b)


def _sum_slots(x, name):
    s, r, c = x.shape
    tr = _pick(r, (256, 128, 64, 32, 16, 8))

    def body(x_ref, o_ref):
        acc = x_ref[0]
        for k in range(1, s):
            acc = acc + x_ref[k]
        o_ref[...] = acc

    return pl.pallas_call(body, name=name, grid=(r // tr,),
                          in_specs=[pl.BlockSpec((s, tr, c), lambda i: (0, i, 0))],
                          out_specs=pl.BlockSpec((tr, c), lambda i: (i, 0)),
                          out_shape=jax.ShapeDtypeStruct((r, c), F32), compiler_params=_cp(("parallel",)))(x)


def _adamw(w, g, m, v, name):
    r, c = w.shape
    tr = _pick(r, (256, 128, 64, 32, 16, 8))
    bc1 = 1.0 - ADAM_B1 ** ADAM_STEP
    bc2 = 1.0 - ADAM_B2 ** ADAM_STEP

    def body(w_ref, g_ref, m_ref, v_ref, d_ref, nm_ref, nv_ref):
        gv = g_ref[...]
        nm = ADAM_B1 * m_ref[...] + (1.0 - ADAM_B1) * gv
        nv = ADAM_B2 * v_ref[...] + (1.0 - ADAM_B2) * (gv * gv)
        d_ref[...] = -ADAM_LR * ((nm / bc1) / (jnp.sqrt(nv / bc2) + ADAM_EPS) + ADAM_WD * w_ref[...])
        nm_ref[...] = nm
        nv_ref[...] = nv

    spec = pl.BlockSpec((tr, c), lambda i: (i, 0))
    shp = jax.ShapeDtypeStruct((r, c), F32)
    return pl.pallas_call(body, name=name, grid=(r // tr,), in_specs=[spec] * 4, out_specs=[spec] * 3,
                          out_shape=[shp] * 3, compiler_params=_cp(("parallel",)))(w, g, m, v)


def _cctx_grad(parts, c_ctx2d):
    def body(p_ref, c_ref, o_ref):
        tot = ((p_ref[0] + p_ref[2]) + p_ref[4]) + p_ref[6]
        o_ref[...] = tot * _dsilu(c_ref[...])

    return pl.pallas_call(body, name="cctx_grad", in_specs=[VMEM_SPEC, VMEM_SPEC], out_specs=VMEM_SPEC,
                          out_shape=jax.ShapeDtypeStruct(c_ctx2d.shape, F32))(parts, c_ctx2d)


def _mod_fwd(c_rows_t, ada_w, ada_b_shard):
    nl, d, w = ada_w.shape
    td = _pick(d, (256, 128))
    nd = d // td

    def body(ct_ref, w_ref, b_ref, o_ref):
        i = pl.program_id(1)

        @pl.when(i == 0)
        def _():
            o_ref[0] = jnp.broadcast_to(b_ref[0], (MOD_ROWS, w))

        st = _silu(ct_ref[...])
        wv = w_ref[0]
        rows = [jnp.sum(st[:, r:r + 1] * wv, axis=0, keepdims=True) for r in range(MOD_ROWS)]
        o_ref[0] += jnp.concatenate(rows, axis=0)

    return pl.pallas_call(
        body, name="mod_fwd", grid=(nl, nd),
        in_specs=[pl.BlockSpec((td, MOD_ROWS), lambda l, i: (i, 0)),
                  pl.BlockSpec((1, td, w), lambda l, i: (l, i, 0)),
                  pl.BlockSpec((1, 1, w), lambda l, i: (l, 0, 0))],
        out_specs=pl.BlockSpec((1, MOD_ROWS, w), lambda l, i: (l, 0, 0)),
        out_shape=jax.ShapeDtypeStruct((nl, MOD_ROWS, w), F32),
        compiler_params=_cp(("parallel", "arbitrary")),
    )(c_rows_t, ada_w, ada_b_shard)


def _mod_bwd(c_rows_t, dmod, ada_w):
    nl, d, w = ada_w.shape
    td = _pick(d, (256, 128))
    ctx_row = N_DEV

    def body(ct_ref, dm_ref, w_ref, gw_ref, ds_ref):
        st = _silu(ct_ref[...])
        dm = dm_ref[0]
        acc = st[:, 0:1] * dm[0:1, :]
        for r in range(1, ctx_row + 1):
            acc = acc + st[:, r:r + 1] * dm[r:r + 1, :]
        gw_ref[0] = acc
        ds_ref[0] = jnp.sum(w_ref[0] * dm[ctx_row:ctx_row + 1, :], axis=1, keepdims=True)

    return pl.pallas_call(
        body, name="mod_bwd", grid=(nl, d // td),
        in_specs=[pl.BlockSpec((td, MOD_ROWS), lambda l, i: (i, 0)),
                  pl.BlockSpec((1, MOD_ROWS, w), lambda l, i: (l, 0, 0)),
                  pl.BlockSpec((1, td, w), lambda l, i: (l, i, 0))],
        out_specs=[pl.BlockSpec((1, td, w), lambda l, i: (l, i, 0)),
                   pl.BlockSpec((1, td, 1), lambda l, i: (l, i, 0))],
        out_shape=[jax.ShapeDtypeStruct((nl, d, w), F32), jax.ShapeDtypeStruct((nl, d, 1), F32)],
        compiler_params=_cp(("parallel", "parallel")),
    )(c_rows_t, dmod, ada_w)


def _adaln_fwd(x, g, shift, scale, tr, name):
    r, d = x.shape

    def body(x_ref, g_ref, sh_ref, sc_ref, o_ref):
        xv = x_ref[...]
        rs = lax.rsqrt(jnp.mean(xv * xv, axis=-1, keepdims=True) + EPS)
        o_ref[...] = ((xv * rs * g_ref[...]) * (1.0 + sc_ref[...]) + sh_ref[...]).astype(BF16)

    spec = pl.BlockSpec((tr, d), lambda i: (i, 0))
    return pl.pallas_call(body, name=name, grid=(r // tr,), in_specs=[spec, _vec(d), _vec(d), _vec(d)],
                          out_specs=spec, out_shape=jax.ShapeDtypeStruct((r, d), BF16),
                          compiler_params=_cp(("parallel",)))(x, g, shift, scale)


def _adaln_bwd(xin, dh, row0, g, scale, dg_init, tr, name, dres=None, o_prev=None, gate_prev=None):
    r, d = xin.shape
    assert row0 % tr == 0
    rb0 = row0 // tr
    want_dx = dres is not None
    want_prev = o_prev is not None
    assert want_dx or not want_prev

    def body(*refs):
        it = iter(refs)
        x_ref, dh_ref, g_ref, sc_ref, dgi_ref = next(it), next(it), next(it), next(it), next(it)
        dres_ref = next(it) if want_dx else None
        o_ref, gp_ref = (next(it), next(it)) if want_prev else (None, None)
        dx_ref = next(it) if want_dx else None
        do_ref = next(it) if want_prev else None
        dsh_ref, dsc_ref, dg_ref = next(it), next(it), next(it)
        dgp_ref = next(it) if want_prev else None
        i = pl.program_id(0)

        @pl.when(i == 0)
        def _():
            dsh_ref[...] = jnp.zeros_like(dsh_ref)
            dsc_ref[...] = jnp.zeros_like(dsc_ref)
            dg_ref[...] = dgi_ref[...]
            if want_prev:
                dgp_ref[...] = jnp.zeros_like(dgp_ref)

        xv = x_ref[...]
        dhv = dh_ref[...].astype(F32)
        gv = g_ref[...]
        rs = lax.rsqrt(jnp.mean(xv * xv, axis=-1, keepdims=True) + EPS)
        xn = xv * rs
        dsh_ref[...] += jnp.sum(dhv, axis=0, keepdims=True)
        dsc_ref[...] += jnp.sum(dhv * (xn * gv), axis=0, keepdims=True)
        dr = dhv * (1.0 + sc_ref[...])
        dg_ref[...] += jnp.sum(dr * xn, axis=0, keepdims=True)
        if want_dx:
            gy = dr * gv
            dx = dres_ref[...] + rs * (gy - xn * jnp.mean(gy * xn, axis=-1, keepdims=True))
            dx_ref[...] = dx
            if want_prev:
                do_ref[...] = (gp_ref[...] * dx).astype(BF16)
                dgp_ref[...] += jnp.sum(dx * o_ref[...], axis=0, keepdims=True)

    row = pl.BlockSpec((tr, d), lambda i: (i, 0))
    in_specs = [row, pl.BlockSpec((tr, d), lambda i: (rb0 + i, 0)), _vec(d), _vec(d), _vec(d)]
    args = [xin, dh, g, scale, dg_init]
    out_specs, out_shape, names = [], [], []
    if want_dx:
        in_specs.append(row)
        args.append(dres)
    if want_prev:
        in_specs += [row, _vec(d)]
        args += [o_prev, gate_prev]
    if want_dx:
        out_specs.append(row)
        out_shape.append(jax.ShapeDtypeStruct((r, d), F32))
        names.append("dx")
    if want_prev:
        out_specs.append(row)
        out_shape.append(jax.ShapeDtypeStruct((r, d), BF16))
        names.append("do_prev")
    for nm in ("dshift", "dscale", "dg") + (("dgate_prev",) if want_prev else ()):
        out_specs.append(_vec(d))
        out_shape.append(jax.ShapeDtypeStruct((1, d), F32))
        names.append(nm)
    outs = pl.pallas_call(body, name=name, grid=(r // tr,), in_specs=in_specs, out_specs=out_specs,
                          out_shape=out_shape, compiler_params=_cp(("arbitrary",)))(*args)
    return dict(zip(names, outs))


def _mm(a, b, *, name, tm, tn, tk, ta=False, tb=False, out_dtype=F32, res=None, gate=None):
    if ta:
        kd, m = a.shape
    else:
        m, kd = a.shape
    if tb:
        n, kd2 = b.shape
    else:
        kd2, n = b.shape
    assert kd == kd2 and m % tm == 0 and n % tn == 0 and kd % tk == 0, (a.shape, b.shape, tm, tn, tk)
    nk = kd // tk
    dn = (((0 if ta else 1,), (1 if tb else 0,)), ((), ()))
    with_res = res is not None

    def body(*refs):
        if with_res:
            a_ref, b_ref, r_ref, g_ref, o_ref, o2_ref, acc_ref = refs
        else:
            a_ref, b_ref, o_ref, acc_ref = refs
        k = pl.program_id(2)

        @pl.when(k == 0)
        def _():
            acc_ref[...] = jnp.zeros_like(acc_ref)

        acc_ref[...] += lax.dot_general(a_ref[...], b_ref[...], dn, preferred_element_type=F32)

        @pl.when(k == nk - 1)
        def _():
            acc = acc_ref[...]
            o_ref[...] = acc.astype(o_ref.dtype)
            if with_res:
                o2_ref[...] = r_ref[...] + g_ref[...] * acc

    a_spec = pl.BlockSpec((tk, tm), lambda i, j, k: (k, i)) if ta else pl.BlockSpec((tm, tk), lambda i, j, k: (i, k))
    b_spec = pl.BlockSpec((tn, tk), lambda i, j, k: (j, k)) if tb else pl.BlockSpec((tk, tn), lambda i, j, k: (k, j))
    o_spec = pl.BlockSpec((tm, tn), lambda i, j, k: (i, j))
    in_specs, args = [a_spec, b_spec], [a, b]
    out_specs, out_shape = [o_spec], [jax.ShapeDtypeStruct((m, n), out_dtype)]
    if with_res:
        in_specs += [o_spec, pl.BlockSpec((1, tn), lambda i, j, k: (0, j))]
        args += [res, gate]
        out_specs.append(o_spec)
        out_shape.append(jax.ShapeDtypeStruct((m, n), F32))
    outs = pl.pallas_call(body, name=name, grid=(m // tm, n // tn, nk), in_specs=in_specs, out_specs=out_specs,
                          out_shape=out_shape, scratch_shapes=[pltpu.VMEM((tm, tn), F32)],
                          compiler_params=_cp(("parallel", "parallel", "arbitrary")))(*args)
    return outs if with_res else outs[0]


def _swap_pairs(x):
    lane = lax.broadcasted_iota(jnp.int32, x.shape, 1)
    return jnp.where(lane % 2 == 0, pltpu.roll(x, HEAD_DIM - 1, 1), pltpu.roll(x, 1, 1))


def _qk_prep(p0, cos_i, sin_s, q_norm, k_norm, dims, tr):
    rows = p0.shape[0]
    kvw, aw, cb = dims["kv_w"], dims["attn_w"], dims["cb"]
    nkv, nq = kvw // HEAD_DIM, aw // HEAD_DIM
    scale = HEAD_DIM ** -0.5
    n_kv_specs, n_q_specs = (2 * kvw) // cb, aw // cb

    def body(*refs):
        kv_refs = refs[:n_kv_specs]
        q_refs = refs[n_kv_specs:n_kv_specs + n_q_specs]
        cos_ref, sin_ref, qn_ref, kn_ref, qo_ref, ko_ref, vo_ref = refs[n_kv_specs + n_q_specs:]
        kv = _cat(kv_refs)
        qv = _cat(q_refs)
        cs, sn = cos_ref[...], sin_ref[...]

        def norm_rope(xh, gvec):
            rs = lax.rsqrt(jnp.mean(xh * xh, axis=-1, keepdims=True) + EPS)
            xn = xh * rs * gvec
            return xn * cs + _swap_pairs(xn) * sn

        for h in range(nkv):
            sl = slice(h * HEAD_DIM, (h + 1) * HEAD_DIM)
            ko_ref[:, sl] = norm_rope(kv[:, sl], kn_ref[...]).astype(BF16)
        vo_ref[...] = kv[:, kvw:].astype(BF16)
        for h in range(nq):
            sl = slice(h * HEAD_DIM, (h + 1) * HEAD_DIM)
            qo_ref[:, sl] = (norm_rope(qv[:, sl], qn_ref[...]) * scale).astype(BF16)

    rm = lambda i: i
    in_specs = (_col_specs(tr, 0, 2 * kvw, cb, rm) + _col_specs(tr, 2 * kvw, aw, cb, rm)
                + [pl.BlockSpec((tr, HEAD_DIM), lambda i: (i, 0))] * 2 + [_vec(HEAD_DIM)] * 2)
    args = [p0] * (n_kv_specs + n_q_specs) + [cos_i, sin_s, q_norm, k_norm]
    return pl.pallas_call(
        body, name="qk_prep", grid=(rows // tr,), in_specs=in_specs,
        out_specs=[pl.BlockSpec((tr, aw), lambda i: (i, 0)), pl.BlockSpec((tr, kvw), lambda i: (i, 0)),
                   pl.BlockSpec((tr, kvw), lambda i: (i, 0))],
        out_shape=[jax.ShapeDtypeStruct((rows, aw), BF16), jax.ShapeDtypeStruct((rows, kvw), BF16),
                   jax.ShapeDtypeStruct((rows, kvw), BF16)],
        compiler_params=_cp(("parallel",)))(*args)


def _qk_prep_bwd(dq_hat, dk_hat, dv, p0, cos_i, sin_s, q_norm, k_norm, dims, tr, n_lat):
    rows = p0.shape[0]
    kvw, aw, cb = dims["kv_w"], dims["attn_w"], dims["cb"]
    nkv, nq = kvw // HEAD_DIM, aw // HEAD_DIM
    scale = HEAD_DIM ** -0.5
    n_kv_specs, n_q_specs = (2 * kvw) // cb, aw // cb
    lat_tiles = n_lat // tr

    def body(*refs):
        kv_refs = refs[:n_kv_specs]
        q_refs = refs[n_kv_specs:n_kv_specs + n_q_specs]
        (dq_ref, dk_ref, dv_ref, cos_ref, sin_ref, qn_ref, kn_ref,
         out_ref, dqn_ref, dkn_ref) = refs[n_kv_specs + n_q_specs:]
        i = pl.program_id(0)

        @pl.when(i == 0)
        def _():
            dqn_ref[...] = jnp.zeros_like(dqn_ref)
            dkn_ref[...] = jnp.zeros_like(dkn_ref)

        kv = _cat(kv_refs)
        qv = _cat(q_refs)
        cs, sn = cos_ref[...], sin_ref[...]
        is_lat = (i < lat_tiles).astype(F32)

        def head_bwd(xh, dhat, gvec):
            dn = dhat * cs + _swap_pairs(dhat * sn)
            rs = lax.rsqrt(jnp.mean(xh * xh, axis=-1, keepdims=True) + EPS)
            xn = xh * rs
            gy = dn * gvec
            dx = rs * (gy - xn * jnp.mean(gy * xn, axis=-1, keepdims=True))
            return dx, jnp.sum(dn * xn, axis=0, keepdims=True)

        dkn = jnp.zeros((1, HEAD_DIM), F32)
        for h in range(nkv):
            sl = slice(h * HEAD_DIM, (h + 1) * HEAD_DIM)
            dx, dgv = head_bwd(kv[:, sl], dk_ref[:, sl], kn_ref[...])
            out_ref[:, sl] = dx.astype(BF16)
            dkn = dkn + dgv
        dkn_ref[...] += dkn
        out_ref[:, kvw:2 * kvw] = dv_ref[...].astype(BF16)
        dqn = jnp.zeros((1, HEAD_DIM), F32)
        for h in range(nq):
            sl = slice(h * HEAD_DIM, (h + 1) * HEAD_DIM)
            dx, dgv = head_bwd(qv[:, sl], dq_ref[:, sl] * (scale * is_lat), qn_ref[...])
            out_ref[:, 2 * kvw + h * HEAD_DIM:2 * kvw + (h + 1) * HEAD_DIM] = dx.astype(BF16)
            dqn = dqn + dgv
        dqn_ref[...] += dqn

    rm = lambda i: i
    wout = 2 * kvw + aw
    in_specs = (_col_specs(tr, 0, 2 * kvw, cb, rm) + _col_specs(tr, 2 * kvw, aw, cb, rm)
                + [pl.BlockSpec((tr, aw), lambda i: (jnp.minimum(i, lat_tiles - 1), 0)),
                   pl.BlockSpec((tr, kvw), lambda i: (i, 0)), pl.BlockSpec((tr, kvw), lambda i: (i, 0)),
                   pl.BlockSpec((tr, HEAD_DIM), lambda i: (i, 0)), pl.BlockSpec((tr, HEAD_DIM), lambda i: (i, 0)),
                   _vec(HEAD_DIM), _vec(HEAD_DIM)])
    args = [p0] * (n_kv_specs + n_q_specs) + [dq_hat, dk_hat, dv, cos_i, sin_s, q_norm, k_norm]
    return pl.pallas_call(
        body, name="qk_prep_bwd", grid=(rows // tr,), in_specs=in_specs,
        out_specs=[pl.BlockSpec((tr, wout), lambda i: (i, 0)), _vec(HEAD_DIM), _vec(HEAD_DIM)],
        out_shape=[jax.ShapeDtypeStruct((rows, wout), BF16), jax.ShapeDtypeStruct((1, HEAD_DIM), F32),
                   jax.ShapeDtypeStruct((1, HEAD_DIM), F32)],
        compiler_params=_cp(("arbitrary",)))(*args)


def _stack_heads(x, g):
    return jnp.concatenate([x[:, h * HEAD_DIM:(h + 1) * HEAD_DIM] for h in range(g)], axis=0)


def _flash_fwd(q_hat, k_all, v_all, n_lat, dims, tq, tk):
    kvw, aw = dims["kv_w"], dims["attn_w"]
    nkv = kvw // HEAD_DIM
    g = aw // kvw
    gw = g * HEAD_DIM
    n_keys = k_all.shape[0]
    ni, nj = n_lat // tq, n_keys // tk
    dn_nt = (((1,), (1,)), ((), ()))

    def body(q_ref, k_ref, v_ref, o_ref, lse_ref, m_scr, l_scr, acc_scr):
        j = pl.program_id(2)

        @pl.when(j == 0)
        def _():
            m_scr[...] = jnp.full_like(m_scr, -1e30)
            l_scr[...] = jnp.zeros_like(l_scr)
            acc_scr[...] = jnp.zeros_like(acc_scr)

        qs = _stack_heads(q_ref[...], g)
        s = lax.dot_general(qs, k_ref[...], dn_nt, preferred_element_type=F32)
        m_old = m_scr[...]
        m_new = jnp.maximum(m_old, jnp.max(s, axis=-1, keepdims=True))
        alpha = jnp.exp(m_old - m_new)
        p = jnp.exp(s - m_new)
        l_scr[...] = alpha * l_scr[...] + jnp.sum(p, axis=-1, keepdims=True)
        acc_scr[...] = alpha * acc_scr[...] + jnp.dot(p.astype(BF16), v_ref[...], preferred_element_type=F32)
        m_scr[...] = m_new

        @pl.when(j == nj - 1)
        def _():
            o = acc_scr[...] / l_scr[...]
            for h in range(g):
                o_ref[:, h * HEAD_DIM:(h + 1) * HEAD_DIM] = o[h * tq:(h + 1) * tq]
            lse_ref[...] = m_scr[...] + jnp.log(l_scr[...])

    return pl.pallas_call(
        body, name="flash_fwd", grid=(nkv, ni, nj),
        in_specs=[pl.BlockSpec((tq, gw), lambda h, i, j: (i, h)),
                  pl.BlockSpec((tk, HEAD_DIM), lambda h, i, j: (j, h)),
                  pl.BlockSpec((tk, HEAD_DIM), lambda h, i, j: (j, h))],
        out_specs=[pl.BlockSpec((tq, gw), lambda h, i, j: (i, h)),
                   pl.BlockSpec((g * tq, 1), lambda h, i, j: (h * ni + i, 0))],
        out_shape=[jax.ShapeDtypeStruct((n_lat, aw), F32), jax.ShapeDtypeStruct((nkv * ni * g * tq, 1), F32)],
        scratch_shapes=[pltpu.VMEM((g * tq, 1), F32), pltpu.VMEM((g * tq, 1), F32),
                        pltpu.VMEM((g * tq, HEAD_DIM), F32)],
        compiler_params=_cp(("parallel", "parallel", "arbitrary")))(q_hat, k_all, v_all)


def _flash_bwd(q_hat, k_all, v_all, do, o, lse, n_lat, dims, tq, tk):
    kvw, aw = dims["kv_w"], dims["attn_w"]
    nkv = kvw // HEAD_DIM
    g = aw // kvw
    gw = g * HEAD_DIM
    n_keys = k_all.shape[0]
    ni, nj = n_lat // tq, n_keys // tk
    dn_nt = (((1,), (1,)), ((), ()))
    dn_tn = (((0,), (0,)), ((), ()))

    def body(q_ref, k_ref, v_ref, do_ref, o_ref, lse_ref, dq_ref, dk_ref, dv_ref, dq_scr, delta_scr):
        i, j = pl.program_id(1), pl.program_id(2)

        @pl.when(jnp.logical_and(i == 0, j == 0))
        def _():
            dk_ref[...] = jnp.zeros_like(dk_ref)
            dv_ref[...] = jnp.zeros_like(dv_ref)

        dos = _stack_heads(do_ref[...], g)

        @pl.when(j == 0)
        def _():
            dq_scr[...] = jnp.zeros_like(dq_scr)
            delta_scr[...] = jnp.sum(dos.astype(F32) * _stack_heads(o_ref[...], g), axis=-1, keepdims=True)

        qs = _stack_heads(q_ref[...], g)
        kb, vb = k_ref[...], v_ref[...]
        s = lax.dot_general(qs, kb, dn_nt, preferred_element_type=F32)
        p = jnp.exp(s - lse_ref[...])
        dp = lax.dot_general(dos, vb, dn_nt, preferred_element_type=F32)
        ds = (p * (dp - delta_scr[...])).astype(BF16)
        rows = pl.ds(pl.multiple_of(j * tk, tk), tk)
        dv_ref[rows, :] += lax.dot_general(p.astype(BF16), dos, dn_tn, preferred_element_type=F32)
        dk_ref[rows, :] += lax.dot_general(ds, qs, dn_tn, preferred_element_type=F32)
        dq_scr[...] += jnp.dot(ds, kb, preferred_element_type=F32)

        @pl.when(j == nj - 1)
        def _():
            dq = dq_scr[...]
            for h in range(g):
                dq_ref[:, h * HEAD_DIM:(h + 1) * HEAD_DIM] = dq[h * tq:(h + 1) * tq]

    qspec = pl.BlockSpec((tq, gw), lambda h, i, j: (i, h))
    kspec = pl.BlockSpec((tk, HEAD_DIM), lambda h, i, j: (j, h))
    full_k = pl.BlockSpec((n_keys, HEAD_DIM), lambda h, i, j: (0, h))
    return pl.pallas_call(
        body, name="flash_bwd", grid=(nkv, ni, nj),
        in_specs=[qspec, kspec, kspec, qspec, qspec, pl.BlockSpec((g * tq, 1), lambda h, i, j: (h * ni + i, 0))],
        out_specs=[qspec, full_k, full_k],
        out_shape=[jax.ShapeDtypeStruct((n_lat, aw), F32), jax.ShapeDtypeStruct((n_keys, kvw), F32),
                   jax.ShapeDtypeStruct((n_keys, kvw), F32)],
        scratch_shapes=[pltpu.VMEM((g * tq, HEAD_DIM), F32), pltpu.VMEM((g * tq, 1), F32)],
        compiler_params=_cp(("parallel", "arbitrary", "arbitrary")))(q_hat, k_all, v_all, do, o, lse)


def _halo_maps(tr, n_tiles):
    per = tr // CONV_HALO
    prev = lambda i: jnp.maximum(i * per - 1, 0)
    nxt = lambda i: (i + 1) * per
    return prev, nxt


def _mix_fwd(attn, p0, dw_w, dw_b, ln_g, ln_b, dims, tr):
    n_lat, aw = attn.shape
    cc, cb, cw = dims["conv_ch"], dims["cb"], dims["conv_w"]
    off = dims["off"]
    n_tiles = n_lat // tr
    pad = cw // 2
    na, nc = aw // cb, cc // cb
    prev_map, next_map = _halo_maps(tr, n_tiles)

    def body(*refs):
        it = iter(refs)
        attn_ref = next(it)
        za = [next(it) for _ in range(na)]
        a_c = [next(it) for _ in range(nc)]
        b_c = [next(it) for _ in range(nc)]
        zb = [next(it) for _ in range(nc)]
        a_p = [next(it) for _ in range(nc)]
        b_p = [next(it) for _ in range(nc)]
        a_n = [next(it) for _ in range(nc)]
        b_n = [next(it) for _ in range(nc)]
        w_ref, db_ref, g_ref, bb_ref, mix_ref, yc_ref, ypad = (next(it) for _ in range(7))
        i = pl.program_id(0)
        ypad[pl.ds(0, CONV_HALO), :] = _cat(a_p) * _sigmoid(_cat(b_p)) * (i > 0).astype(F32)
        ypad[pl.ds(CONV_HALO, tr), :] = _cat(a_c) * _sigmoid(_cat(b_c))
        ypad[pl.ds(CONV_HALO + tr, CONV_HALO), :] = _cat(a_n) * _sigmoid(_cat(b_n)) * (i < n_tiles - 1).astype(F32)
        acc = jnp.broadcast_to(db_ref[...], (tr, cc))
        for k in range(cw):
            acc = acc + ypad[pl.ds(CONV_HALO - pad + k, tr), :] * w_ref[pl.ds(k, 1), :]
        yc_ref[...] = acc
        mu = jnp.mean(acc, axis=-1, keepdims=True)
        xc = acc - mu
        rs = lax.rsqrt(jnp.mean(xc * xc, axis=-1, keepdims=True) + EPS)
        nv = xc * rs * g_ref[...] + bb_ref[...]
        mix_ref[:, :aw] = (attn_ref[...] * _silu(_cat(za))).astype(BF16)
        mix_ref[:, aw:] = (_silu(nv) * _silu(_cat(zb))).astype(BF16)

    rm = lambda i: i
    in_specs = ([pl.BlockSpec((tr, aw), lambda i: (i, 0))]
                + _col_specs(tr, off["za"], aw, cb, rm) + _col_specs(tr, off["a"], cc, cb, rm)
                + _col_specs(tr, off["b"], cc, cb, rm) + _col_specs(tr, off["zb"], cc, cb, rm)
                + _col_specs(CONV_HALO, off["a"], cc, cb, prev_map) + _col_specs(CONV_HALO, off["b"], cc, cb, prev_map)
                + _col_specs(CONV_HALO, off["a"], cc, cb, next_map) + _col_specs(CONV_HALO, off["b"], cc, cb, next_map)
                + [pl.BlockSpec(dw_w.shape, lambda i: (0, 0)), _vec(cc), _vec(cc), _vec(cc)])
    args = [attn] + [p0] * (na + 7 * nc) + [dw_w, dw_b, ln_g, ln_b]
    return pl.pallas_call(
        body, name="mix_fwd", grid=(n_tiles,), in_specs=in_specs,
        out_specs=[pl.BlockSpec((tr, aw + cc), lambda i: (i, 0)), pl.BlockSpec((tr, cc), lambda i: (i, 0))],
        out_shape=[jax.ShapeDtypeStruct((n_lat, aw + cc), BF16), jax.ShapeDtypeStruct((n_lat, cc), F32)],
        scratch_shapes=[pltpu.VMEM((tr + 2 * CONV_HALO, cc), F32)],
        compiler_params=_cp(("parallel",)))(*args)


def _mix_bwd_pointwise(dmix, attn, p0, yc, ln_g, ln_b, dims, tr):
    n_lat, aw = attn.shape
    cc, cb, off = dims["conv_ch"], dims["cb"], dims["off"]
    na, nc = aw // cb, cc // cb

    def body(*refs):
        it = iter(refs)
        dmix_ref, attn_ref = next(it), next(it)
        za = [next(it) for _ in range(na)]
        zb = [next(it) for _ in range(nc)]
        yc_ref, g_ref, bb_ref = next(it), next(it), next(it)
        dattn_ref, dza_ref, dzb_ref, dyc_ref, dg_ref, dbb_ref, ddb_ref = (next(it) for _ in range(7))
        i = pl.program_id(0)

        @pl.when(i == 0)
        def _():
            dg_ref[...] = jnp.zeros_like(dg_ref)
            dbb_ref[...] = jnp.zeros_like(dbb_ref)
            ddb_ref[...] = jnp.zeros_like(ddb_ref)

        dm = dmix_ref[...]
        dma, dmb = dm[:, :aw], dm[:, aw:]
        zav, zbv = _cat(za), _cat(zb)
        dattn_ref[...] = (dma * _silu(zav)).astype(BF16)
        dza_ref[...] = (dma * attn_ref[...] * _dsilu(zav)).astype(BF16)
        ycv = yc_ref[...]
        mu = jnp.mean(ycv, axis=-1, keepdims=True)
        xc = ycv - mu
        rs = lax.rsqrt(jnp.mean(xc * xc, axis=-1, keepdims=True) + EPS)
        xh = xc * rs
        nv = xh * g_ref[...] + bb_ref[...]
        dzb_ref[...] = (dmb * _silu(nv) * _dsilu(zbv)).astype(BF16)
        dn = dmb * _silu(zbv) * _dsilu(nv)
        dg_ref[...] += jnp.sum(dn * xh, axis=0, keepdims=True)
        dbb_ref[...] += jnp.sum(dn, axis=0, keepdims=True)
        dxh = dn * g_ref[...]
        dyc = rs * (dxh - jnp.mean(dxh, axis=-1, keepdims=True) - xh * jnp.mean(dxh * xh, axis=-1, keepdims=True))
        dyc_ref[...] = dyc
        ddb_ref[...] += jnp.sum(dyc, axis=0, keepdims=True)

    rm = lambda i: i
    row = lambda w: pl.BlockSpec((tr, w), lambda i: (i, 0))
    in_specs = ([row(aw + cc), row(aw)] + _col_specs(tr, off["za"], aw, cb, rm)
                + _col_specs(tr, off["zb"], cc, cb, rm) + [row(cc), _vec(cc), _vec(cc)])
    args = [dmix, attn] + [p0] * (na + nc) + [yc, ln_g, ln_b]
    return pl.pallas_call(
        body, name="mix_bwd_pointwise", grid=(n_lat // tr,), in_specs=in_specs,
        out_specs=[row(aw), row(aw), row(cc), row(cc), _vec(cc), _vec(cc), _vec(cc)],
        out_shape=[jax.ShapeDtypeStruct((n_lat, aw), BF16), jax.ShapeDtypeStruct((n_lat, aw), BF16),
                   jax.ShapeDtypeStruct((n_lat, cc), BF16), jax.ShapeDtypeStruct((n_lat, cc), F32)]
                  + [jax.ShapeDtypeStruct((1, cc), F32)] * 3,
        compiler_params=_cp(("arbitrary",)))(*args)


def _conv_bwd(dyc, p0, dw_w, dims, tr):
    n_lat, cc = dyc.shape
    cb, cw, off = dims["cb"], dims["conv_w"], dims["off"]
    nc = cc // cb
    n_tiles = n_lat // tr
    pad = cw // 2
    prev_map, next_map = _halo_maps(tr, n_tiles)

    def body(*refs):
        it = iter(refs)
        a_c = [next(it) for _ in range(nc)]
        b_c = [next(it) for _ in range(nc)]
        a_p = [next(it) for _ in range(nc)]
        b_p = [next(it) for _ in range(nc)]
        a_n = [next(it) for _ in range(nc)]
        b_n = [next(it) for _ in range(nc)]
        d_c, d_p, d_n, w_ref, dab_ref, dw_ref, ypad, dpad = (next(it) for _ in range(8))
        i = pl.program_id(0)

        @pl.when(i == 0)
        def _():
            dw_ref[...] = jnp.zeros_like(dw_ref)

        first, last = (i > 0).astype(F32), (i < n_tiles - 1).astype(F32)
        av, bv = _cat(a_c), _cat(b_c)
        sg = _sigmoid(bv)
        ypad[pl.ds(0, CONV_HALO), :] = _cat(a_p) * _sigmoid(_cat(b_p)) * first
        ypad[pl.ds(CONV_HALO, tr), :] = av * sg
        ypad[pl.ds(CONV_HALO + tr, CONV_HALO), :] = _cat(a_n) * _sigmoid(_cat(b_n)) * last
        dcur = d_c[...]
        dpad[pl.ds(0, CONV_HALO), :] = d_p[...] * first
        dpad[pl.ds(CONV_HALO, tr), :] = dcur
        dpad[pl.ds(CONV_HALO + tr, CONV_HALO), :] = d_n[...] * last
        dy = jnp.zeros((tr, cc), F32)
        for k in range(cw):
            dy = dy + dpad[pl.ds(CONV_HALO + pad - k, tr), :] * w_ref[pl.ds(k, 1), :]
            dw_ref[pl.ds(k, 1), :] += jnp.sum(dcur * ypad[pl.ds(CONV_HALO - pad + k, tr), :], axis=0, keepdims=True)
        dab_ref[:, :cc] = (dy * sg).astype(BF16)
        dab_ref[:, cc:] = (dy * av * sg * (1.0 - sg)).astype(BF16)

    rm = lambda i: i
    in_specs = (_col_specs(tr, off["a"], cc, cb, rm) + _col_specs(tr, off["b"], cc, cb, rm)
                + _col_specs(CONV_HALO, off["a"], cc, cb, prev_map) + _col_specs(CONV_HALO, off["b"], cc, cb, prev_map)
                + _col_specs(CONV_HALO, off["a"], cc, cb, next_map) + _col_specs(CONV_HALO, off["b"], cc, cb, next_map)
                + [pl.BlockSpec((tr, cc), lambda i: (i, 0)),
                   pl.BlockSpec((CONV_HALO, cc), lambda i: (prev_map(i), 0)),
                   pl.BlockSpec((CONV_HALO, cc), lambda i: (jnp.minimum(next_map(i), n_lat // CONV_HALO - 1), 0)),
                   pl.BlockSpec(dw_w.shape, lambda i: (0, 0))])
    args = [p0] * (6 * nc) + [dyc, dyc, dyc, dw_w]
    return pl.pallas_call(
        body, name="conv_bwd", grid=(n_tiles,), in_specs=in_specs,
        out_specs=[pl.BlockSpec((tr, 2 * cc), lambda i: (i, 0)), pl.BlockSpec(dw_w.shape, lambda i: (0, 0))],
        out_shape=[jax.ShapeDtypeStruct((n_lat, 2 * cc), BF16), jax.ShapeDtypeStruct(dw_w.shape, F32)],
        scratch_shapes=[pltpu.VMEM((tr + 2 * CONV_HALO, cc), F32), pltpu.VMEM((tr + 2 * CONV_HALO, cc), F32)],
        compiler_params=_cp(("arbitrary",)))(*args)


def _sgu_parts(pu, pv, ln_g, ln_b):
    u = _gelu(pu)
    v = _gelu(pv)
    mu = jnp.mean(v, axis=-1, keepdims=True)
    xc = v - mu
    rs = lax.rsqrt(jnp.mean(xc * xc, axis=-1, keepdims=True) + EPS)
    xh = xc * rs
    return u, xh, rs, xh * ln_g + ln_b


def _sgu_fwd(p1, ln_g, ln_b, ws, bs_t, tr):
    n_lat, w3 = p1.shape
    w = w3 // 3
    ng, ch = ws.shape[0], ws.shape[1]
    gwid = w // ng
    n_ch = tr // ch

    def body(pu_ref, pv_ref, pg_ref, g_ref, b_ref, ws_ref, bs_ref, o_ref):
        u, _, _, vln = _sgu_parts(pu_ref[...], pv_ref[...], g_ref[...], b_ref[...])
        gate = _silu(pg_ref[...])
        vb = vln.astype(BF16)
        for c in range(n_ch):
            rs_ = slice(c * ch, (c + 1) * ch)
            for gi in range(ng):
                cs_ = slice(gi * gwid, (gi + 1) * gwid)
                mixed = jnp.dot(ws_ref[gi], vb[rs_, cs_], preferred_element_type=F32) + bs_ref[:, gi:gi + 1]
                o_ref[rs_, cs_] = (u[rs_, cs_] * mixed * gate[rs_, cs_]).astype(BF16)

    col = lambda t: pl.BlockSpec((tr, w), lambda i, _t=t: (i, _t))
    return pl.pallas_call(
        body, name="sgu_fwd", grid=(n_lat // tr,),
        in_specs=[col(0), col(1), col(2), _vec(w), _vec(w),
                  pl.BlockSpec(ws.shape, lambda i: (0, 0, 0)), pl.BlockSpec(bs_t.shape, lambda i: (0, 0))],
        out_specs=pl.BlockSpec((tr, w), lambda i: (i, 0)),
        out_shape=jax.ShapeDtypeStruct((n_lat, w), BF16),
        compiler_params=_cp(("parallel",)))(p1, p1, p1, ln_g, ln_b, ws, bs_t)


def _sgu_bwd(dm, p1, ln_g, ln_b, ws, ws_t, bs_t, tr):
    n_lat, w3 = p1.shape
    w = w3 // 3
    ng, ch = ws.shape[0], ws.shape[1]
    gwid = w // ng
    n_ch = tr // ch
    dn_nt = (((1,), (1,)), ((), ()))

    def body(dm_ref, pu_ref, pv_ref, pg_ref, g_ref, b_ref, ws_ref, wst_ref, bs_ref,
             dp_ref, dws_ref, dbs_ref, dg_ref, dbb_ref, dvln_scr):
        i = pl.program_id(0)

        @pl.when(i == 0)
        def _():
            dws_ref[...] = jnp.zeros_like(dws_ref)
            dbs_ref[...] = jnp.zeros_like(dbs_ref)
            dg_ref[...] = jnp.zeros_like(dg_ref)
            dbb_ref[...] = jnp.zeros_like(dbb_ref)

        puv, pvv, pgv = pu_ref[...], pv_ref[...], pg_ref[...]
        u, xh, rs, vln = _sgu_parts(puv, pvv, g_ref[...], b_ref[...])
        gate = _silu(pgv)
        dmv = dm_ref[...]
        vb = vln.astype(BF16)
        dmu = dmv * u
        dbs_cols = [jnp.zeros((ch, 1), F32) for _ in range(ng)]
        for c in range(n_ch):
            rs_ = slice(c * ch, (c + 1) * ch)
            for gi in range(ng):
                cs_ = slice(gi * gwid, (gi + 1) * gwid)
                mixed = jnp.dot(ws_ref[gi], vb[rs_, cs_], preferred_element_type=F32) + bs_ref[:, gi:gi + 1]
                dmixed = dmu[rs_, cs_] * gate[rs_, cs_]
                dmb = dmixed.astype(BF16)
                dp_ref[rs_, gi * gwid:(gi + 1) * gwid] = (
                    dmv[rs_, cs_] * mixed * gate[rs_, cs_] * _dgelu(puv[rs_, cs_])).astype(BF16)
                dp_ref[rs_, 2 * w + gi * gwid:2 * w + (gi + 1) * gwid] = (
                    dmu[rs_, cs_] * mixed * _dsilu(pgv[rs_, cs_])).astype(BF16)
                dvln_scr[rs_, cs_] = jnp.dot(wst_ref[gi], dmb, preferred_element_type=F32)
                dws_ref[gi] += lax.dot_general(dmb, vb[rs_, cs_], dn_nt, preferred_element_type=F32)
                dbs_cols[gi] = dbs_cols[gi] + jnp.sum(dmixed, axis=-1, keepdims=True)
        dbs_ref[...] += jnp.concatenate(dbs_cols, axis=1)
        dvln = dvln_scr[...]
        dg_ref[...] += jnp.sum(dvln * xh, axis=0, keepdims=True)
        dbb_ref[...] += jnp.sum(dvln, axis=0, keepdims=True)
        dxh = dvln * g_ref[...]
        dv = rs * (dxh - jnp.mean(dxh, axis=-1, keepdims=True) - xh * jnp.mean(dxh * xh, axis=-1, keepdims=True))
        dp_ref[:, w:2 * w] = (dv * _dgelu(pvv)).astype(BF16)

    col = lambda t: pl.BlockSpec((tr, w), lambda i, _t=t: (i, _t))
    return pl.pallas_call(
        body, name="sgu_bwd", grid=(n_lat // tr,),
        in_specs=[pl.BlockSpec((tr, w), lambda i: (i, 0)), col(0), col(1), col(2), _vec(w), _vec(w),
                  pl.BlockSpec(ws.shape, lambda i: (0, 0, 0)), pl.BlockSpec(ws.shape, lambda i: (0, 0, 0)),
                  pl.BlockSpec(bs_t.shape, lambda i: (0, 0))],
        out_specs=[pl.BlockSpec((tr, w3), lambda i: (i, 0)), pl.BlockSpec(ws.shape, lambda i: (0, 0, 0)),
                   pl.BlockSpec(bs_t.shape, lambda i: (0, 0)), _vec(w), _vec(w)],
        out_shape=[jax.ShapeDtypeStruct((n_lat, w3), BF16), jax.ShapeDtypeStruct(ws.shape, F32),
                   jax.ShapeDtypeStruct(bs_t.shape, F32), jax.ShapeDtypeStruct((1, w), F32),
                   jax.ShapeDtypeStruct((1, w), F32)],
        scratch_shapes=[pltpu.VMEM((tr, w), F32)],
        compiler_params=_cp(("arbitrary",)))(dm, p1, p1, p1, ln_g, ln_b, ws, ws_t, bs_t)


def _final_loss(x2, target, final_g, o_prev, gate_prev, tr):
    n_lat, d = x2.shape

    def body(x_ref, t_ref, g_ref, o_ref, gp_ref, dx_ref, do_ref, ls_ref, dg_ref, dgp_ref):
        i = pl.program_id(0)

        @pl.when(i == 0)
        def _():
            ls_ref[...] = jnp.zeros_like(ls_ref)
            dg_ref[...] = jnp.zeros_like(dg_ref)
            dgp_ref[...] = jnp.zeros_like(dgp_ref)

        xv = x_ref[...]
        gv = g_ref[...]
        rs = lax.rsqrt(jnp.mean(xv * xv, axis=-1, keepdims=True) + EPS)
        xn = xv * rs
        err = xn * gv - t_ref[...]
        ls_ref[...] += jnp.sum(err * err, axis=0, keepdims=True)
        dy = err * (1.0 / d)
        dg_ref[...] += jnp.sum(dy * xn, axis=0, keepdims=True)
        gy = dy * gv
        dx = rs * (gy - xn * jnp.mean(gy * xn, axis=-1, keepdims=True))
        dx_ref[...] = dx
        do_ref[...] = (gp_ref[...] * dx).astype(BF16)
        dgp_ref[...] += jnp.sum(dx * o_ref[...], axis=0, keepdims=True)

    row = pl.BlockSpec((tr, d), lambda i: (i, 0))
    return pl.pallas_call(
        body, name="final_loss", grid=(n_lat // tr,), in_specs=[row, row, _vec(d), row, _vec(d)],
        out_specs=[row, row, _vec(d), _vec(d), _vec(d)],
        out_shape=[jax.ShapeDtypeStruct((n_lat, d), F32), jax.ShapeDtypeStruct((n_lat, d), BF16)]
                  + [jax.ShapeDtypeStruct((1, d), F32)] * 3,
        compiler_params=_cp(("arbitrary",)))(x2, target, final_g, o_prev, gate_prev)


def _pack(arrays):
    flat = jnp.concatenate([a.reshape(-1).astype(F32) for a in arrays])
    n = flat.shape[0]
    rows = -(-n // LANES)
    rows = -(-rows // 8) * 8
    return jnp.pad(flat, (0, rows * LANES - n)).reshape(rows, LANES)


def _unpack(buf, shapes):
    flat = buf.reshape(buf.shape[:-2] + (-1,))
    out, pos = [], 0
    for shp in shapes:
        n = math.prod(shp)
        out.append(flat[..., pos:pos + n].reshape(buf.shape[:-2] + tuple(shp)))
        pos += n
    return out


def _rope_tables(n_lat, n_ctx):
    rows = n_lat // GRID_W
    row = jnp.repeat(jnp.arange(rows, dtype=F32), GRID_W)
    col = jnp.tile(jnp.arange(GRID_W, dtype=F32), rows)
    n_freq, axis_dim = HEAD_DIM // 4, HEAD_DIM // 2
    inv = jnp.power(ROPE_THETA, jnp.arange(n_freq, dtype=F32) * (-2.0 / axis_dim))
    ang = jnp.concatenate([row[:, None] * inv, col[:, None] * inv], axis=-1)
    cos, sin = jnp.cos(ang), jnp.sin(ang)
    cos_i = jnp.repeat(cos, 2, axis=-1)
    sin_s = jnp.stack([-sin, sin], axis=-1).reshape(n_lat, HEAD_DIM)
    cos_i = jnp.concatenate([cos_i, jnp.ones((n_ctx, HEAD_DIM), F32)], axis=0)
    sin_s = jnp.concatenate([sin_s, jnp.zeros((n_ctx, HEAD_DIM), F32)], axis=0)
    return cos_i, sin_s


def kernel(x, c, ctx, c_ctx, ada_w, ada_b, norm_g, ev_w_in, ev_q_norm, ev_k_norm, ev_dw_w, ev_dw_b, ev_ln_g, ev_ln_b, ev_w_out, od_w_in, od_ln_g, od_ln_b, od_ws, od_bs, od_w_out, final_g, loss_target, m_c_ctx, m_ada_w, m_ada_b, m_norm_g, m_ev_w_in, m_ev_q_norm, m_ev_k_norm, m_ev_dw_w, m_ev_dw_b, m_ev_ln_g, m_ev_ln_b, m_ev_w_out, m_od_w_in, m_od_ln_g, m_od_ln_b, m_od_ws, m_od_bs, m_od_w_out, m_final_g, v_c_ctx, v_ada_w, v_ada_b, v_norm_g, v_ev_w_in, v_ev_q_norm, v_ev_k_norm, v_ev_dw_w, v_ev_dw_b, v_ev_ln_g, v_ev_ln_b, v_ev_w_out, v_od_w_in, v_od_ln_g, v_od_ln_b, v_od_ws, v_od_bs, v_od_w_out, v_final_g):
    weights = dict(c_ctx=c_ctx, ada_w=ada_w, ada_b=ada_b, norm_g=norm_g, ev_w_in=ev_w_in, ev_q_norm=ev_q_norm,
                   ev_k_norm=ev_k_norm, ev_dw_w=ev_dw_w, ev_dw_b=ev_dw_b, ev_ln_g=ev_ln_g, ev_ln_b=ev_ln_b,
                   ev_w_out=ev_w_out, od_w_in=od_w_in, od_ln_g=od_ln_g, od_ln_b=od_ln_b, od_ws=od_ws, od_bs=od_bs,
                   od_w_out=od_w_out, final_g=final_g)
    mom_m = dict(c_ctx=m_c_ctx, ada_w=m_ada_w, ada_b=m_ada_b, norm_g=m_norm_g, ev_w_in=m_ev_w_in,
                 ev_q_norm=m_ev_q_norm, ev_k_norm=m_ev_k_norm, ev_dw_w=m_ev_dw_w, ev_dw_b=m_ev_dw_b,
                 ev_ln_g=m_ev_ln_g, ev_ln_b=m_ev_ln_b, ev_w_out=m_ev_w_out, od_w_in=m_od_w_in, od_ln_g=m_od_ln_g,
                 od_ln_b=m_od_ln_b, od_ws=m_od_ws, od_bs=m_od_bs, od_w_out=m_od_w_out, final_g=m_final_g)
    mom_v = dict(c_ctx=v_c_ctx, ada_w=v_ada_w, ada_b=v_ada_b, norm_g=v_norm_g, ev_w_in=v_ev_w_in,
                 ev_q_norm=v_ev_q_norm, ev_k_norm=v_ev_k_norm, ev_dw_w=v_ev_dw_w, ev_dw_b=v_ev_dw_b,
                 ev_ln_g=v_ev_ln_g, ev_ln_b=v_ev_ln_b, ev_w_out=v_ev_w_out, od_w_in=v_od_w_in, od_ln_g=v_od_ln_g,
                 od_ln_b=v_od_ln_b, od_ws=v_od_ws, od_bs=v_od_bs, od_w_out=v_od_w_out, final_g=v_final_g)
    order = list(weights)

    _, n_lat, d = x.shape
    n_ctx = ctx.shape[1]
    n_ext = n_lat + n_ctx
    ev_in = ev_w_in.shape[-1] * N_CHIP
    ev_mix = ev_w_out.shape[1] * N_CHIP
    conv_ch = ev_dw_b.shape[-1]
    conv_w = ev_dw_w.shape[1]
    attn_w = ev_mix - conv_ch
    kv_w = N_KV_HEADS * HEAD_DIM
    assert ev_in == 2 * kv_w + 2 * attn_w + 3 * conv_ch and conv_w // 2 < CONV_HALO
    sgu_w = od_w_out.shape[1] * N_CHIP
    wa = ada_w.shape[-1]
    cb = math.gcd(2 * kv_w, attn_w, conv_ch)
    off = dict(k=0, v=kv_w, q=2 * kv_w, za=2 * kv_w + attn_w, a=2 * kv_w + 2 * attn_w,
               b=2 * kv_w + 2 * attn_w + conv_ch, zb=2 * kv_w + 2 * attn_w + 2 * conv_ch)
    dims = dict(kv_w=kv_w, attn_w=attn_w, conv_ch=conv_ch, conv_w=conv_w, cb=cb, off=off)
    tr = 256 if (n_lat % 256 == 0 and n_ctx % 256 == 0) else 128

    mx, my, mc = lax.axis_index("x"), lax.axis_index("y"), lax.axis_index("c")
    me = 4 * mx + 2 * my + mc
    chip = 2 * mx + my

    x2d, tgt2d, ctx2d = x[0], loss_target[0], ctx[0]
    ev_dw_w_l = ev_dw_w[0]
    dwc = ev_dw_w_l.shape[1]
    lnc = od_ln_g.shape[1]

    g_c = _allgather_small(jnp.broadcast_to(c, (8, d)), "gather_cond")[:, 0, :]
    c_rows = jnp.concatenate([g_c, c_ctx[None, :], jnp.zeros((MOD_ROWS - N_DEV - 1, d), F32)], axis=0)
    c_rows_t = c_rows.T
    ada_b_shard = lax.dynamic_slice_in_dim(ada_b, chip * wa, wa, axis=1)[:, None, :]
    mod_part = _mod_fwd(c_rows_t, ada_w, ada_b_shard)
    part_shapes = [(2, MOD_ROWS, wa), (conv_w, dwc), (1, lnc), (1, lnc)]
    g_parts = _allgather_small(_pack([mod_part, ev_dw_w_l, od_ln_g, od_ln_b]), "gather_mod")
    per_chip = [_unpack(g_parts[2 * s], part_shapes) for s in range(N_CHIP)]
    mod_all = jnp.concatenate([p[0] for p in per_chip], axis=-1)
    dw_w_full = jnp.concatenate([p[1] for p in per_chip], axis=-1)
    od_ln_g_full = jnp.concatenate([p[2] for p in per_chip], axis=-1)
    od_ln_b_full = jnp.concatenate([p[3] for p in per_chip], axis=-1)
    dw_w_pad = jnp.pad(dw_w_full, ((0, 2 * CONV_HALO - conv_w), (0, 0)))
    mod_me = lax.dynamic_slice_in_dim(mod_all, me, 1, axis=1)
    shift0, scale0, gate0 = mod_me[0, :, :d], mod_me[0, :, d:2 * d], mod_me[0, :, 2 * d:]
    shift1, scale1, gate1 = mod_me[1, :, :d], mod_me[1, :, d:2 * d], mod_me[1, :, 2 * d:]
    shift_c, scale_c = mod_all[0, N_DEV:N_DEV + 1, :d], mod_all[0, N_DEV:N_DEV + 1, d:2 * d]
    g0, g1 = norm_g[0:1], norm_g[1:2]

    lay = dict(ev_w_in=_Sharded((d, ev_in), True), ev_w_out=_Sharded((ev_mix, d), False),
               od_w_in=_Sharded((d, 3 * sgu_w), True), od_w_out=_Sharded((sgu_w, d), False))
    big = list(lay)
    w_full = dict(zip(big, _gather_weights([weights[n][0].astype(BF16) for n in big], [lay[n] for n in big])))

    h0 = _adaln_fwd(x2d, g0, shift0, scale0, tr, "adaln0_fwd")
    hc = _adaln_fwd(ctx2d, g0, shift_c, scale_c, tr, "adaln0_ctx_fwd")
    h0e = jnp.concatenate([h0, hc], axis=0)
    tm_e = _pick(n_ext, (1408, 768, 640, 512, 256, 128))
    tk_e = _pick(n_ext, (768, 640, 512, 256, 128))
    tm_l = _pick(n_lat, (1024, 512, 256, 128))
    p0 = _mm(h0e, w_full["ev_w_in"], name="mm_ev_in", tm=tm_e, tn=_pick(ev_in, (512, 256, 128)), tk=d)
    cos_i, sin_s = _rope_tables(n_lat, n_ctx)
    q_hat, k_all, v_all = _qk_prep(p0, cos_i, sin_s, ev_q_norm, ev_k_norm, dims, tr)
    tq = _pick(n_lat, (256, 128))
    tkk = _pick(n_ext, (768, 640, 512, 256, 128))
    attn, lse = _flash_fwd(q_hat, k_all, v_all, n_lat, dims, tq, tkk)
    mix, yc = _mix_fwd(attn, p0, dw_w_pad, ev_dw_b, ev_ln_g, ev_ln_b, dims, tr)
    o0, x1 = _mm(mix, w_full["ev_w_out"], name="mm_ev_out", tm=tm_l, tn=_pick(d, (1024, 512, 256)),
                 tk=_pick(ev_mix, (512, 256)), res=x2d, gate=gate0)

    h1 = _adaln_fwd(x1, g1, shift1, scale1, tr, "adaln1_fwd")
    p1 = _mm(h1, w_full["od_w_in"], name="mm_od_in", tm=tm_l, tn=_pick(3 * sgu_w, (512, 256, 128)), tk=d)
    ws_b = od_ws[0].astype(BF16)
    ws_t_b = jnp.swapaxes(od_ws[0], 1, 2).astype(BF16)
    bs_t = od_bs[0].T
    m1 = _sgu_fwd(p1, od_ln_g_full, od_ln_b_full, ws_b, bs_t, tr)
    o1, x2 = _mm(m1, w_full["od_w_out"], name="mm_od_out", tm=tm_l, tn=_pick(d, (1024, 512, 256)),
                 tk=_pick(sgu_w, (512, 256)), res=x1, gate=gate1)

    dx2, do1, loss_cols, d_final_g, dgate1 = _final_loss(x2, tgt2d, final_g[None, :], o1, gate1, tr)
    loss = lax.psum(0.5 / d * jnp.sum(loss_cols), ("x", "y", "c"))

    tk_l = _pick(n_lat, (1024, 512, 256, 128))
    dm1 = _mm(do1, w_full["od_w_out"], name="mm_od_out_dx", tb=True, tm=tm_l, tn=_pick(sgu_w, (1024, 512, 256)),
              tk=_pick(d, (512, 256)))
    g_od_w_out = _mm(m1, do1, name="mm_od_out_dw", ta=True, tm=_pick(sgu_w, (1024, 512, 256)),
                     tn=_pick(d, (1024, 512, 256)), tk=tk_l)
    dp1, d_ws, d_bs_t, d_od_ln_g, d_od_ln_b = _sgu_bwd(dm1, p1, od_ln_g_full, od_ln_b_full, ws_b, ws_t_b, bs_t, tr)
    dh1 = _mm(dp1, w_full["od_w_in"], name="mm_od_in_dx", tb=True, tm=tm_l, tn=_pick(d, (1024, 512, 256)),
              tk=_pick(3 * sgu_w, (512, 256)))
    g_od_w_in = _mm(h1, dp1, name="mm_od_in_dw", ta=True, tm=_pick(d, (1024, 512, 256)),
                    tn=_pick(3 * sgu_w, (1536, 768, 512, 384, 256, 128)), tk=tk_l)
    zero_d = jnp.zeros((1, d), F32)
    b1 = _adaln_bwd(x1, dh1, 0, g1, scale1, zero_d, tr, "adaln1_bwd", dres=dx2, o_prev=o0, gate_prev=gate0)
    dx1, do0, dgate0 = b1["dx"], b1["do_prev"], b1["dgate_prev"]

    dmix = _mm(do0, w_full["ev_w_out"], name="mm_ev_out_dx", tb=True, tm=tm_l, tn=_pick(ev_mix, (1024, 512, 256)),
               tk=_pick(d, (512, 256)))
    g_ev_w_out = _mm(mix, do0, name="mm_ev_out_dw", ta=True, tm=_pick(ev_mix, (1024, 512, 256)),
                     tn=_pick(d, (1024, 512, 256)), tk=tk_l)
    dattn, dza, dzb, dyc, d_ev_ln_g, d_ev_ln_b, d_dw_b = _mix_bwd_pointwise(
        dmix, attn, p0, yc, ev_ln_g, ev_ln_b, dims, tr)
    dab, d_dw_w_pad = _conv_bwd(dyc, p0, dw_w_pad, dims, tr)
    dq_hat, dk_hat, dv_all = _flash_bwd(q_hat, k_all, v_all, dattn, attn, lse, n_lat, dims, tq, tkk)
    dkvq, d_q_norm, d_k_norm = _qk_prep_bwd(dq_hat, dk_hat, dv_all, p0, cos_i, sin_s, ev_q_norm, ev_k_norm,
                                            dims, tr, n_lat)
    rest = jnp.pad(jnp.concatenate([dza, dab, dzb], axis=1), ((0, n_ctx), (0, 0)))
    dp0 = jnp.concatenate([dkvq, rest], axis=1)
    dh0 = _mm(dp0, w_full["ev_w_in"], name="mm_ev_in_dx", tb=True, tm=tm_e, tn=_pick(d, (1024, 512, 256)),
              tk=_pick(ev_in, (512, 256, 128)))
    g_ev_w_in = _mm(h0e, dp0, name="mm_ev_in_dw", ta=True, tm=_pick(d, (1024, 512, 256)),
                    tn=_pick(ev_in, (1408, 768, 512, 256, 128)), tk=tk_e)
    bc = _adaln_bwd(ctx2d, dh0, n_lat, g0, scale_c, zero_d, tr, "adaln0_ctx_bwd")
    b0 = _adaln_bwd(x2d, dh0, 0, g0, scale0, bc["dg"], tr, "adaln0_bwd", dres=dx1)
    grad_x = b0["dx"]

    zeros_d = jnp.zeros((1, d), F32)
    dmod0 = jnp.concatenate([b0["dshift"], b0["dscale"], dgate0], axis=1)
    dmod1 = jnp.concatenate([b1["dshift"], b1["dscale"], dgate1], axis=1)
    dmodc = jnp.concatenate([bc["dshift"], bc["dscale"], zeros_d], axis=1)
    small = [dmod0, dmod1, dmodc, b0["dg"], b1["dg"], d_q_norm, d_k_norm, d_dw_w_pad[:conv_w], d_dw_b,
             d_ev_ln_g, d_ev_ln_b, d_od_ln_g, d_od_ln_b, d_ws, d_bs_t.T, d_final_g]
    small_shapes = [a.shape for a in small]
    g_small = _allgather_small(_pack(small), "gather_small_grads")
    tot = _unpack(_sum_slots(g_small, "sum_small_grads"), small_shapes)
    (t_dmod0, t_dmod1, t_dmodc, t_g0, t_g1, t_qn, t_kn, t_dw_w, t_dw_b, t_eln_g, t_eln_b, t_oln_g, t_oln_b,
     t_ws, t_bs, t_fg) = tot
    rows_dmod = _unpack(g_small, small_shapes[:2])
    dmod0_rows, dmod1_rows = rows_dmod[0][:, 0, :], rows_dmod[1][:, 0, :]
    pad_rows = jnp.zeros((MOD_ROWS - N_DEV - 1, 3 * d), F32)
    dm_l0 = jnp.concatenate([dmod0_rows, t_dmodc, pad_rows], axis=0)
    dm_l1 = jnp.concatenate([dmod1_rows, jnp.zeros((MOD_ROWS - N_DEV, 3 * d), F32)], axis=0)
    dm_shard = lax.dynamic_slice_in_dim(jnp.stack([dm_l0, dm_l1]), chip * wa, wa, axis=2)
    g_ada_w, dsc = _mod_bwd(c_rows_t, dm_shard, ada_w)
    g_dsc = _allgather_small(_pack([dsc[0]]), "gather_cctx")
    g_c_ctx = _cctx_grad(g_dsc, _pack([c_ctx])).reshape(-1)[:d]
    g_ada_b = jnp.stack([t_dmod0[0] + t_dmodc[0], t_dmod1[0]])

    lays = [lay[n] for n in big]
    mine, theirs = (lambda o: (o[:len(big)], o[len(big):]))(
        _pair_exchange([g_ev_w_in, g_ev_w_out, g_od_w_in, g_od_w_out], lays))
    pair_sums = [_add2(a, b, f"pair_sum_{n}") for a, b, n in zip(mine, theirs, big)]
    slots = _chip_scatter(pair_sums, lays)
    regions = [_sum_slots(s, f"chip_sum_{n}") for s, n in zip(slots, big)]
    g_big = dict(zip(big, _pair_share(regions, lays)))

    grads = dict(
        c_ctx=g_c_ctx, ada_w=g_ada_w, ada_b=g_ada_b, norm_g=jnp.concatenate([t_g0, t_g1], axis=0),
        ev_w_in=g_big["ev_w_in"][None], ev_q_norm=t_qn, ev_k_norm=t_kn,
        ev_dw_w=lax.dynamic_slice_in_dim(t_dw_w, chip * dwc, dwc, axis=1)[None], ev_dw_b=t_dw_b,
        ev_ln_g=t_eln_g, ev_ln_b=t_eln_b, ev_w_out=g_big["ev_w_out"][None], od_w_in=g_big["od_w_in"][None],
        od_ln_g=lax.dynamic_slice_in_dim(t_oln_g, chip * lnc, lnc, axis=1),
        od_ln_b=lax.dynamic_slice_in_dim(t_oln_b, chip * lnc, lnc, axis=1),
        od_ws=t_ws[None], od_bs=t_bs[None], od_w_out=g_big["od_w_out"][None], final_g=t_fg[0])
    grads = {n: grads[n].reshape(weights[n].shape) for n in order}

    delta, new_m, new_v = {}, {}, {}
    large = ("ada_w", "ev_w_in", "ev_w_out", "od_w_in", "od_w_out")
    for n in large:
        shp = weights[n].shape
        as2d = lambda a: a.reshape(-1, shp[-1])
        dl, nm, nv = _adamw(as2d(weights[n]), as2d(grads[n]), as2d(mom_m[n]), as2d(mom_v[n]), f"adamw_{n}")
        delta[n], new_m[n], new_v[n] = dl.reshape(shp), nm.reshape(shp), nv.reshape(shp)
    rest_names = [n for n in order if n not in large]
    rest_shapes = [weights[n].shape for n in rest_names]
    dl, nm, nv = _adamw(_pack([weights[n] for n in rest_names]), _pack([grads[n] for n in rest_names]),
                        _pack([mom_m[n] for n in rest_names]), _pack([mom_v[n] for n in rest_names]), "adamw_small")
    for n, a, b_, c_ in zip(rest_names, _unpack(dl, rest_shapes), _unpack(nm, rest_shapes), _unpack(nv, rest_shapes)):
        delta[n], new_m[n], new_v[n] = a, b_, c_

    return (loss, grad_x[None], *[grads[n] for n in order], *[delta[n] for n in order],
            *[new_m[n] for n in order], *[new_v[n] for n in order])
```

```python
import math

import jax
import jax.numpy as jnp
from jax import lax
from jax.experimental import pallas as pl
from jax.experimental.pallas import tpu as pltpu

F32 = jnp.float32
BF16 = jnp.bfloat16
EPS = 1e-6
GRID_W = 64
ROPE_THETA = 10000.0
HEAD_DIM = 128
N_KV_HEADS = 2
CONV_HALO = 16
LANES = 128
N_DEV = 8
N_CHIP = 4
MOD_ROWS = 16
PACK_ROWS = 64
ADAM_LR, ADAM_B1, ADAM_B2, ADAM_EPS, ADAM_WD, ADAM_STEP = 0.001, 0.9, 0.999, 1e-08, 0.01, 10
VMEM_LIMIT = 56 * 1024 * 1024
MESH = pl.DeviceIdType.MESH
ANY = pl.BlockSpec(memory_space=pl.ANY)
VMEM_SPEC = pl.BlockSpec(memory_space=pltpu.VMEM)
CHIP_DELTAS = ((1, 0), (0, 1), (1, 1))


def _cp(sem=None):
    return pltpu.CompilerParams(dimension_semantics=sem, vmem_limit_bytes=VMEM_LIMIT)


def _pick(n, cands):
    for c in cands:
        if n % c == 0:
            return c
    raise ValueError(f"no tile for {n} in {cands}")


def _sigmoid(x):
    return 1.0 / (1.0 + jnp.exp(-x))


def _silu(x):
    return x * _sigmoid(x)


def _dsilu(x):
    s = _sigmoid(x)
    return s * (1.0 + x * (1.0 - s))


_GELU_C = math.sqrt(2.0 / math.pi)


def _gelu(x):
    return 0.5 * x * (1.0 + jnp.tanh(_GELU_C * (x + 0.044715 * x * x * x)))


def _dgelu(x):
    t = jnp.tanh(_GELU_C * (x + 0.044715 * x * x * x))
    return 0.5 * (1.0 + t) + 0.5 * x * (1.0 - t * t) * _GELU_C * (1.0 + 3.0 * 0.044715 * x * x)


def _vec(d):
    return pl.BlockSpec((1, d), lambda *_: (0, 0))


def _cat(refs):
    return refs[0][...] if len(refs) == 1 else jnp.concatenate([r[...] for r in refs], axis=1)


def _col_specs(rows, off, width, cb, row_map):
    assert off % cb == 0 and width % cb == 0
    return [pl.BlockSpec((rows, cb), (lambda *g, _c=off // cb + t: (row_map(*g), _c))) for t in range(width // cb)]


def _my_pos():
    return lax.axis_index("x"), lax.axis_index("y"), lax.axis_index("c")


def _allgather_small(x, name):
    r, c = x.shape

    def body(x_ref, out_ref, send_sems, recv_sems, local_sem):
        mx, my, mc = _my_pos()
        me = 4 * mx + 2 * my + mc
        mine = pltpu.make_async_copy(x_ref, out_ref.at[me], local_sem)
        mine.start()
        deltas = [(dx, dy, dc) for dx in (0, 1) for dy in (0, 1) for dc in (0, 1) if (dx, dy, dc) != (0, 0, 0)]
        sends = []
        for k, (dx, dy, dc) in enumerate(deltas):
            px, py, pc = (mx + dx) % 2, (my + dy) % 2, (mc + dc) % 2
            cp = pltpu.make_async_remote_copy(
                src_ref=x_ref, dst_ref=out_ref.at[me], send_sem=send_sems.at[k], recv_sem=recv_sems.at[k],
                device_id=(px, py, pc), device_id_type=MESH)
            cp.start()
            sends.append(cp)
        for k, (dx, dy, dc) in enumerate(deltas):
            px, py, pc = (mx + dx) % 2, (my + dy) % 2, (mc + dc) % 2
            peer = 4 * px + 2 * py + pc
            pltpu.make_async_remote_copy(
                src_ref=x_ref, dst_ref=out_ref.at[peer], send_sem=send_sems.at[k], recv_sem=recv_sems.at[k],
                device_id=(px, py, pc), device_id_type=MESH).wait_recv()
        for cp in sends:
            cp.wait_send()
        mine.wait()

    return pl.pallas_call(
        body, name=name,
        out_shape=jax.ShapeDtypeStruct((N_DEV, r, c), x.dtype),
        in_specs=[VMEM_SPEC], out_specs=VMEM_SPEC,
        scratch_shapes=[pltpu.SemaphoreType.DMA((N_DEV - 1,)), pltpu.SemaphoreType.DMA((N_DEV - 1,)),
                        pltpu.SemaphoreType.DMA],
        compiler_params=pltpu.CompilerParams(vmem_limit_bytes=VMEM_LIMIT),
    )(x)


class _Sharded:
    def __init__(self, full_shape, by_cols):
        self.full = full_shape
        self.by_cols = by_cols
        rows, cols = full_shape
        if by_cols:
            self.shard, self.half, self.halves = (rows, cols // N_CHIP), (rows // 2, cols // N_CHIP), (rows // 2, cols)
        else:
            self.shard, self.half, self.halves = (rows // N_CHIP, cols), (rows // N_CHIP, cols // 2), (rows, cols // 2)

    def region(self, ref, s, h):
        if self.by_cols:
            return ref.at[pl.ds(h * self.half[0], self.half[0]), pl.ds(s * self.shard[1], self.shard[1])]
        return ref.at[pl.ds(s * self.shard[0], self.shard[0]), pl.ds(h * self.half[1], self.half[1])]

    def halves_of_full(self, ref, h):
        if self.by_cols:
            return ref.at[pl.ds(h * self.halves[0], self.halves[0]), :]
        return ref.at[:, pl.ds(h * self.halves[1], self.halves[1])]

    def region_in_halves(self, ref, s):
        if self.by_cols:
            return ref.at[:, pl.ds(s * self.shard[1], self.shard[1])]
        return ref.at[pl.ds(s * self.shard[0], self.shard[0]), :]

    def half_of_shard(self, ref, h):
        if self.by_cols:
            return ref.at[pl.ds(h * self.half[0], self.half[0]), :]
        return ref.at[:, pl.ds(h * self.half[1], self.half[1])]


def _row_tile(rows, row_bytes):
    for t in (512, 256, 128, 64, 32, 16):
        if rows % t == 0 and t * row_bytes <= 2 * 1024 * 1024:
            return t
    return 16


def _cast_into_full(w_shard, lay, chip_arr, name):
    r, c = lay.shard
    tr = _row_tile(r, c * 4)
    nt = r // tr

    def body(chip_ref, w_ref, o_ref):
        o_ref[...] = w_ref[...].astype(BF16)

    if lay.by_cols:
        out_map = lambda i, chip_ref: (i, chip_ref[0])
    else:
        out_map = lambda i, chip_ref: (chip_ref[0] * nt + i, 0)
    return pl.pallas_call(
        body, name=name,
        grid_spec=pltpu.PrefetchScalarGridSpec(
            num_scalar_prefetch=1, grid=(nt,),
            in_specs=[pl.BlockSpec((tr, c), lambda i, chip_ref: (i, 0))],
            out_specs=pl.BlockSpec((tr, c), out_map)),
        out_shape=jax.ShapeDtypeStruct(lay.full, BF16), compiler_params=_cp(("parallel",)))(chip_arr, w_shard)


def _gather_weights(fulls, layouts):
    n = len(fulls)

    def body(*refs):
        ins, outs = refs[:n], refs[n:2 * n]
        send_sems, recv_sems = refs[2 * n:]
        mx, my, mc = _my_pos()
        chip = 2 * mx + my
        sibling = (mx, my, 1 - mc)
        sends = []
        for a, lay in enumerate(layouts):
            for j, (dx, dy) in enumerate(CHIP_DELTAS):
                px, py = (mx + dx) % 2, (my + dy) % 2
                cp = pltpu.make_async_remote_copy(
                    src_ref=lay.region(ins[a], chip, mc), dst_ref=lay.region(outs[a], chip, mc),
                    send_sem=send_sems.at[a, j], recv_sem=recv_sems.at[a, j],
                    device_id=(px, py, mc), device_id_type=MESH)
                cp.start()
                sends.append(cp)
        for a, lay in enumerate(layouts):
            for j, (dx, dy) in enumerate(CHIP_DELTAS):
                px, py = (mx + dx) % 2, (my + dy) % 2
                other = 2 * px + py
                landed = lay.region(outs[a], other, mc)
                pltpu.make_async_remote_copy(
                    src_ref=landed, dst_ref=landed, send_sem=send_sems.at[a, j], recv_sem=recv_sems.at[a, j],
                    device_id=(px, py, mc), device_id_type=MESH).wait_recv()
                cp = pltpu.make_async_remote_copy(
                    src_ref=landed, dst_ref=landed, send_sem=send_sems.at[a, 3 + j], recv_sem=recv_sems.at[a, 3 + j],
                    device_id=sibling, device_id_type=MESH)
                cp.start()
                sends.append(cp)
        for a, lay in enumerate(layouts):
            for j, (dx, dy) in enumerate(CHIP_DELTAS):
                other = 2 * ((mx + dx) % 2) + (my + dy) % 2
                landed = lay.region(outs[a], other, 1 - mc)
                pltpu.make_async_remote_copy(
                    src_ref=landed, dst_ref=landed, send_sem=send_sems.at[a, 3 + j], recv_sem=recv_sems.at[a, 3 + j],
                    device_id=sibling, device_id_type=MESH).wait_recv()
        for cp in sends:
            cp.wait_send()

    return pl.pallas_call(
        body, name="gather_weights",
        out_shape=[jax.ShapeDtypeStruct(lay.full, BF16) for lay in layouts],
        in_specs=[ANY] * n, out_specs=[ANY] * n, input_output_aliases={a: a for a in range(n)},
        scratch_shapes=[pltpu.SemaphoreType.DMA((n, 6)), pltpu.SemaphoreType.DMA((n, 6))],
    )(*fulls)


def _pair_exchange(grads, layouts):
    n = len(grads)

    def body(*refs):
        ins, outs = refs[:n], refs[n:2 * n]
        send_sems, recv_sems = refs[2 * n:]
        mx, my, mc = _my_pos()
        copies = []
        for a, lay in enumerate(layouts):
            cp = pltpu.make_async_remote_copy(
                src_ref=lay.halves_of_full(ins[a], 1 - mc), dst_ref=outs[a],
                send_sem=send_sems.at[a], recv_sem=recv_sems.at[a],
                device_id=(mx, my, 1 - mc), device_id_type=MESH)
            cp.start()
            copies.append(cp)
        for cp in copies:
            cp.wait()

    return pl.pallas_call(
        body, name="grad_pair_exchange",
        out_shape=[jax.ShapeDtypeStruct(lay.halves, F32) for lay in layouts],
        in_specs=[ANY] * n, out_specs=[ANY] * n,
        scratch_shapes=[pltpu.SemaphoreType.DMA((n,)), pltpu.SemaphoreType.DMA((n,))],
    )(*grads)


def _pair_sum(g, theirs, lay, core_arr, name):
    r, c = lay.halves
    tr = _row_tile(r, c * 4)
    nt = r // tr

    def body(core_ref, g_ref, t_ref, o_ref):
        o_ref[...] = (g_ref[...] + t_ref[...]).astype(BF16)

    if lay.by_cols:
        g_map = lambda i, core_ref: (core_ref[0] * nt + i, 0)
    else:
        g_map = lambda i, core_ref: (i, core_ref[0])
    plain = pl.BlockSpec((tr, c), lambda i, core_ref: (i, 0))
    return pl.pallas_call(
        body, name=name,
        grid_spec=pltpu.PrefetchScalarGridSpec(
            num_scalar_prefetch=1, grid=(nt,), in_specs=[pl.BlockSpec((tr, c), g_map), plain], out_specs=plain),
        out_shape=jax.ShapeDtypeStruct((r, c), BF16), compiler_params=_cp(("parallel",)))(core_arr, g, theirs)


def _chip_scatter(pair_sums, layouts):
    n = len(pair_sums)

    def body(*refs):
        ins, outs = refs[:n], refs[n:2 * n]
        send_sems, recv_sems = refs[2 * n:]
        mx, my, mc = _my_pos()
        sends = []
        for a, lay in enumerate(layouts):
            for j, (dx, dy) in enumerate(CHIP_DELTAS):
                px, py = (mx + dx) % 2, (my + dy) % 2
                cp = pltpu.make_async_remote_copy(
                    src_ref=lay.region_in_halves(ins[a], 2 * px + py), dst_ref=outs[a].at[j],
                    send_sem=send_sems.at[a, j], recv_sem=recv_sems.at[a, j],
                    device_id=(px, py, mc), device_id_type=MESH)
                cp.start()
                sends.append(cp)
        for cp in sends:
            cp.wait()

    return pl.pallas_call(
        body, name="grad_chip_scatter",
        out_shape=[jax.ShapeDtypeStruct((3,) + lay.half, BF16) for lay in layouts],
        in_specs=[ANY] * n, out_specs=[ANY] * n,
        scratch_shapes=[pltpu.SemaphoreType.DMA((n, 3)), pltpu.SemaphoreType.DMA((n, 3))],
    )(*pair_sums)


def _chip_sum(pair_sum, slots, lay, chip_arr, core_arr, name):
    r, c = lay.half
    tr = _row_tile(r, c * 4)
    nt = r // tr

    def body(chip_ref, core_ref, s_ref, slot_ref, o_ref):
        acc = s_ref[...].astype(F32)
        for j in range(3):
            acc = acc + slot_ref[j].astype(F32)
        o_ref[...] = acc

    if lay.by_cols:
        s_map = lambda i, chip_ref, core_ref: (i, chip_ref[0])
        o_map = lambda i, chip_ref, core_ref: (core_ref[0] * nt + i, 0)
    else:
        s_map = lambda i, chip_ref, core_ref: (chip_ref[0] * nt + i, 0)
        o_map = lambda i, chip_ref, core_ref: (i, core_ref[0])
    return pl.pallas_call(
        body, name=name,
        grid_spec=pltpu.PrefetchScalarGridSpec(
            num_scalar_prefetch=2, grid=(nt,),
            in_specs=[pl.BlockSpec((tr, c), s_map),
                      pl.BlockSpec((3, tr, c), lambda i, chip_ref, core_ref: (0, i, 0))],
            out_specs=pl.BlockSpec((tr, c), o_map)),
        out_shape=jax.ShapeDtypeStruct(lay.shard, F32), compiler_params=_cp(("parallel",)))(
            chip_arr, core_arr, pair_sum, slots)


def _pair_share(bufs, layouts):
    n = len(bufs)

    def body(*refs):
        ins, outs = refs[:n], refs[n:2 * n]
        send_sems, recv_sems = refs[2 * n:]
        mx, my, mc = _my_pos()
        copies = []
        for a, lay in enumerate(layouts):
            cp = pltpu.make_async_remote_copy(
                src_ref=lay.half_of_shard(ins[a], mc), dst_ref=lay.half_of_shard(outs[a], mc),
                send_sem=send_sems.at[a], recv_sem=recv_sems.at[a],
                device_id=(mx, my, 1 - mc), device_id_type=MESH)
            cp.start()
            copies.append(cp)
        for a, lay in enumerate(layouts):
            theirs = lay.half_of_shard(outs[a], 1 - mc)
            pltpu.make_async_remote_copy(
                src_ref=theirs, dst_ref=theirs, send_sem=send_sems.at[a], recv_sem=recv_sems.at[a],
                device_id=(mx, my, 1 - mc), device_id_type=MESH).wait_recv()
        for cp in copies:
            cp.wait_send()

    return pl.pallas_call(
        body, name="grad_pair_share",
        out_shape=[jax.ShapeDtypeStruct(lay.shard, F32) for lay in layouts],
        in_specs=[ANY] * n, out_specs=[ANY] * n, input_output_aliases={a: a for a in range(n)},
        scratch_shapes=[pltpu.SemaphoreType.DMA((n,)), pltpu.SemaphoreType.DMA((n,))],
    )(*bufs)


def _sum_slots(x, name):
    s, r, c = x.shape
    tr = _pick(r, (256, 128, 64, 32, 16, 8))

    def body(x_ref, o_ref):
        acc = x_ref[0]
        for k in range(1, s):
            acc = acc + x_ref[k]
        o_ref[...] = acc

    return pl.pallas_call(body, name=name, grid=(r // tr,),
                          in_specs=[pl.BlockSpec((s, tr, c), lambda i: (0, i, 0))],
                          out_specs=pl.BlockSpec((tr, c), lambda i: (i, 0)),
                          out_shape=jax.ShapeDtypeStruct((r, c), F32), compiler_params=_cp(("parallel",)))(x)


def _adamw(w, g, m, v, name):
    r, c = w.shape
    tr = _pick(r, (256, 128, 64, 32, 16, 8))
    bc1 = 1.0 - ADAM_B1 ** ADAM_STEP
    bc2 = 1.0 - ADAM_B2 ** ADAM_STEP

    def body(w_ref, g_ref, m_ref, v_ref, d_ref, nm_ref, nv_ref):
        gv = g_ref[...]
        nm = ADAM_B1 * m_ref[...] + (1.0 - ADAM_B1) * gv
        nv = ADAM_B2 * v_ref[...] + (1.0 - ADAM_B2) * (gv * gv)
        d_ref[...] = -ADAM_LR * ((nm / bc1) / (jnp.sqrt(nv / bc2) + ADAM_EPS) + ADAM_WD * w_ref[...])
        nm_ref[...] = nm
        nv_ref[...] = nv

    spec = pl.BlockSpec((tr, c), lambda i: (i, 0))
    shp = jax.ShapeDtypeStruct((r, c), F32)
    return pl.pallas_call(body, name=name, grid=(r // tr,), in_specs=[spec] * 4, out_specs=[spec] * 3,
                          out_shape=[shp] * 3, compiler_params=_cp(("parallel",)))(w, g, m, v)


def _cctx_grad(parts, c_ctx2d):
    def body(p_ref, c_ref, o_ref):
        tot = ((p_ref[0] + p_ref[2]) + p_ref[4]) + p_ref[6]
        o_ref[...] = tot * _dsilu(c_ref[...])

    return pl.pallas_call(body, name="cctx_grad", in_specs=[VMEM_SPEC, VMEM_SPEC], out_specs=VMEM_SPEC,
                          out_shape=jax.ShapeDtypeStruct(c_ctx2d.shape, F32))(parts, c_ctx2d)


def _mod_fwd(c_rows_t, ada_w, ada_b_shard):
    nl, d, w = ada_w.shape
    td = _pick(d, (256, 128))
    nd = d // td

    def body(ct_ref, w_ref, b_ref, o_ref):
        i = pl.program_id(1)

        @pl.when(i == 0)
        def _():
            o_ref[0] = jnp.broadcast_to(b_ref[0], (MOD_ROWS, w))

        st = _silu(ct_ref[...])
        wv = w_ref[0]
        rows = [jnp.sum(st[:, r:r + 1] * wv, axis=0, keepdims=True) for r in range(MOD_ROWS)]
        o_ref[0] += jnp.concatenate(rows, axis=0)

    return pl.pallas_call(
        body, name="mod_fwd", grid=(nl, nd),
        in_specs=[pl.BlockSpec((td, MOD_ROWS), lambda l, i: (i, 0)),
                  pl.BlockSpec((1, td, w), lambda l, i: (l, i, 0)),
                  pl.BlockSpec((1, 1, w), lambda l, i: (l, 0, 0))],
        out_specs=pl.BlockSpec((1, MOD_ROWS, w), lambda l, i: (l, 0, 0)),
        out_shape=jax.ShapeDtypeStruct((nl, MOD_ROWS, w), F32),
        compiler_params=_cp(("parallel", "arbitrary")),
    )(c_rows_t, ada_w, ada_b_shard)


def _mod_bwd(c_rows_t, dmod, ada_w):
    nl, d, w = ada_w.shape
    td = _pick(d, (256, 128))
    ctx_row = N_DEV

    def body(ct_ref, dm_ref, w_ref, gw_ref, ds_ref):
        st = _silu(ct_ref[...])
        dm = dm_ref[0]
        acc = st[:, 0:1] * dm[0:1, :]
        for r in range(1, ctx_row + 1):
            acc = acc + st[:, r:r + 1] * dm[r:r + 1, :]
        gw_ref[0] = acc
        ds_ref[0] = jnp.sum(w_ref[0] * dm[ctx_row:ctx_row + 1, :], axis=1, keepdims=True)

    return pl.pallas_call(
        body, name="mod_bwd", grid=(nl, d // td),
        in_specs=[pl.BlockSpec((td, MOD_ROWS), lambda l, i: (i, 0)),
                  pl.BlockSpec((1, MOD_ROWS, w), lambda l, i: (l, 0, 0)),
                  pl.BlockSpec((1, td, w), lambda l, i: (l, i, 0))],
        out_specs=[pl.BlockSpec((1, td, w), lambda l, i: (l, i, 0)),
                   pl.BlockSpec((1, td, 1), lambda l, i: (l, i, 0))],
        out_shape=[jax.ShapeDtypeStruct((nl, d, w), F32), jax.ShapeDtypeStruct((nl, d, 1), F32)],
        compiler_params=_cp(("parallel", "parallel")),
    )(c_rows_t, dmod, ada_w)


def _adaln_fwd(x, g, shift, scale, tr, name):
    r, d = x.shape

    def body(x_ref, g_ref, sh_ref, sc_ref, o_ref):
        xv = x_ref[...]
        rs = lax.rsqrt(jnp.mean(xv * xv, axis=-1, keepdims=True) + EPS)
        o_ref[...] = ((xv * rs * g_ref[...]) * (1.0 + sc_ref[...]) + sh_ref[...]).astype(BF16)

    spec = pl.BlockSpec((tr, d), lambda i: (i, 0))
    return pl.pallas_call(body, name=name, grid=(r // tr,), in_specs=[spec, _vec(d), _vec(d), _vec(d)],
                          out_specs=spec, out_shape=jax.ShapeDtypeStruct((r, d), BF16),
                          compiler_params=_cp(("parallel",)))(x, g, shift, scale)


def _adaln_bwd(xin, dh, row0, g, scale, dg_init, tr, name, dres=None, o_prev=None, gate_prev=None):
    r, d = xin.shape
    assert row0 % tr == 0
    rb0 = row0 // tr
    want_dx = dres is not None
    want_prev = o_prev is not None
    assert want_dx or not want_prev

    def body(*refs):
        it = iter(refs)
        x_ref, dh_ref, g_ref, sc_ref, dgi_ref = next(it), next(it), next(it), next(it), next(it)
        dres_ref = next(it) if want_dx else None
        o_ref, gp_ref = (next(it), next(it)) if want_prev else (None, None)
        dx_ref = next(it) if want_dx else None
        do_ref = next(it) if want_prev else None
        dsh_ref, dsc_ref, dg_ref = next(it), next(it), next(it)
        dgp_ref = next(it) if want_prev else None
        i = pl.program_id(0)

        @pl.when(i == 0)
        def _():
            dsh_ref[...] = jnp.zeros_like(dsh_ref)
            dsc_ref[...] = jnp.zeros_like(dsc_ref)
            dg_ref[...] = dgi_ref[...]
            if want_prev:
                dgp_ref[...] = jnp.zeros_like(dgp_ref)

        xv = x_ref[...]
        dhv = dh_ref[...].astype(F32)
        gv = g_ref[...]
        rs = lax.rsqrt(jnp.mean(xv * xv, axis=-1, keepdims=True) + EPS)
        xn = xv * rs
        dsh_ref[...] += jnp.sum(dhv, axis=0, keepdims=True)
        dsc_ref[...] += jnp.sum(dhv * (xn * gv), axis=0, keepdims=True)
        dr = dhv * (1.0 + sc_ref[...])
        dg_ref[...] += jnp.sum(dr * xn, axis=0, keepdims=True)
        if want_dx:
            gy = dr * gv
            dx = dres_ref[...] + rs * (gy - xn * jnp.mean(gy * xn, axis=-1, keepdims=True))
            dx_ref[...] = dx
            if want_prev:
                do_ref[...] = (gp_ref[...] * dx).astype(BF16)
                dgp_ref[...] += jnp.sum(dx * o_ref[...], axis=0, keepdims=True)

    row = pl.BlockSpec((tr, d), lambda i: (i, 0))
    in_specs = [row, pl.BlockSpec((tr, d), lambda i: (rb0 + i, 0)), _vec(d), _vec(d), _vec(d)]
    args = [xin, dh, g, scale, dg_init]
    out_specs, out_shape, names = [], [], []
    if want_dx:
        in_specs.append(row)
        args.append(dres)
    if want_prev:
        in_specs += [row, _vec(d)]
        args += [o_prev, gate_prev]
    if want_dx:
        out_specs.append(row)
        out_shape.append(jax.ShapeDtypeStruct((r, d), F32))
        names.append("dx")
    if want_prev:
        out_specs.append(row)
        out_shape.append(jax.ShapeDtypeStruct((r, d), BF16))
        names.append("do_prev")
    for nm in ("dshift", "dscale", "dg") + (("dgate_prev",) if want_prev else ()):
        out_specs.append(_vec(d))
        out_shape.append(jax.ShapeDtypeStruct((1, d), F32))
        names.append(nm)
    outs = pl.pallas_call(body, name=name, grid=(r // tr,), in_specs=in_specs, out_specs=out_specs,
                          out_shape=out_shape, compiler_params=_cp(("arbitrary",)))(*args)
    return dict(zip(names, outs))


def _mm(a, b, *, name, tm, tn, tk, ta=False, tb=False, out_dtype=F32, res=None, gate=None):
    if ta:
        kd, m = a.shape
    else:
        m, kd = a.shape
    if tb:
        n, kd2 = b.shape
    else:
        kd2, n = b.shape
    assert kd == kd2 and m % tm == 0 and n % tn == 0 and kd % tk == 0, (a.shape, b.shape, tm, tn, tk)
    nk = kd // tk
    dn = (((0 if ta else 1,), (1 if tb else 0,)), ((), ()))
    with_res = res is not None

    def body(*refs):
        if with_res:
            a_ref, b_ref, r_ref, g_ref, o_ref, o2_ref, acc_ref = refs
        else:
            a_ref, b_ref, o_ref, acc_ref = refs
        k = pl.program_id(2)

        @pl.when(k == 0)
        def _():
            acc_ref[...] = jnp.zeros_like(acc_ref)

        acc_ref[...] += lax.dot_general(a_ref[...], b_ref[...], dn, preferred_element_type=F32)

        @pl.when(k == nk - 1)
        def _():
            acc = acc_ref[...]
            o_ref[...] = acc.astype(o_ref.dtype)
            if with_res:
                o2_ref[...] = r_ref[...] + g_ref[...] * acc

    a_spec = pl.BlockSpec((tk, tm), lambda i, j, k: (k, i)) if ta else pl.BlockSpec((tm, tk), lambda i, j, k: (i, k))
    b_spec = pl.BlockSpec((tn, tk), lambda i, j, k: (j, k)) if tb else pl.BlockSpec((tk, tn), lambda i, j, k: (k, j))
    o_spec = pl.BlockSpec((tm, tn), lambda i, j, k: (i, j))
    in_specs, args = [a_spec, b_spec], [a, b]
    out_specs, out_shape = [o_spec], [jax.ShapeDtypeStruct((m, n), out_dtype)]
    if with_res:
        in_specs += [o_spec, pl.BlockSpec((1, tn), lambda i, j, k: (0, j))]
        args += [res, gate]
        out_specs.append(o_spec)
        out_shape.append(jax.ShapeDtypeStruct((m, n), F32))
    outs = pl.pallas_call(body, name=name, grid=(m // tm, n // tn, nk), in_specs=in_specs, out_specs=out_specs,
                          out_shape=out_shape, scratch_shapes=[pltpu.VMEM((tm, tn), F32)],
                          compiler_params=_cp(("parallel", "parallel", "arbitrary")))(*args)
    return outs if with_res else outs[0]


def _swap_pairs(x):
    lane = lax.broadcasted_iota(jnp.int32, x.shape, 1)
    return jnp.where(lane % 2 == 0, pltpu.roll(x, HEAD_DIM - 1, 1), pltpu.roll(x, 1, 1))


def _qk_prep(p0, cos_i, sin_s, q_norm, k_norm, dims, tr):
    rows = p0.shape[0]
    kvw, aw, cb = dims["kv_w"], dims["attn_w"], dims["cb"]
    nkv, nq = kvw // HEAD_DIM, aw // HEAD_DIM
    scale = HEAD_DIM ** -0.5
    n_kv_specs, n_q_specs = (2 * kvw) // cb, aw // cb

    def body(*refs):
        kv_refs = refs[:n_kv_specs]
        q_refs = refs[n_kv_specs:n_kv_specs + n_q_specs]
        cos_ref, sin_ref, qn_ref, kn_ref, qo_ref, ko_ref, vo_ref = refs[n_kv_specs + n_q_specs:]
        kv = _cat(kv_refs)
        qv = _cat(q_refs)
        cs, sn = cos_ref[...], sin_ref[...]

        def norm_rope(xh, gvec):
            rs = lax.rsqrt(jnp.mean(xh * xh, axis=-1, keepdims=True) + EPS)
            xn = xh * rs * gvec
            return xn * cs + _swap_pairs(xn) * sn

        for h in range(nkv):
            sl = slice(h * HEAD_DIM, (h + 1) * HEAD_DIM)
            ko_ref[:, sl] = norm_rope(kv[:, sl], kn_ref[...]).astype(BF16)
        vo_ref[...] = kv[:, kvw:].astype(BF16)
        for h in range(nq):
            sl = slice(h * HEAD_DIM, (h + 1) * HEAD_DIM)
            qo_ref[:, sl] = (norm_rope(qv[:, sl], qn_ref[...]) * scale).astype(BF16)

    rm = lambda i: i
    in_specs = (_col_specs(tr, 0, 2 * kvw, cb, rm) + _col_specs(tr, 2 * kvw, aw, cb, rm)
                + [pl.BlockSpec((tr, HEAD_DIM), lambda i: (i, 0))] * 2 + [_vec(HEAD_DIM)] * 2)
    args = [p0] * (n_kv_specs + n_q_specs) + [cos_i, sin_s, q_norm, k_norm]
    return pl.pallas_call(
        body, name="qk_prep", grid=(rows // tr,), in_specs=in_specs,
        out_specs=[pl.BlockSpec((tr, aw), lambda i: (i, 0)), pl.BlockSpec((tr, kvw), lambda i: (i, 0)),
                   pl.BlockSpec((tr, kvw), lambda i: (i, 0))],
        out_shape=[jax.ShapeDtypeStruct((rows, aw), BF16), jax.ShapeDtypeStruct((rows, kvw), BF16),
                   jax.ShapeDtypeStruct((rows, kvw), BF16)],
        compiler_params=_cp(("parallel",)))(*args)


def _qk_prep_bwd(dq_hat, dk_hat, dv, p0, cos_i, sin_s, q_norm, k_norm, dims, tr, n_lat):
    rows = p0.shape[0]
    kvw, aw, cb = dims["kv_w"], dims["attn_w"], dims["cb"]
    nkv, nq = kvw // HEAD_DIM, aw // HEAD_DIM
    scale = HEAD_DIM ** -0.5
    n_kv_specs, n_q_specs = (2 * kvw) // cb, aw // cb
    lat_tiles = n_lat // tr

    def body(*refs):
        kv_refs = refs[:n_kv_specs]
        q_refs = refs[n_kv_specs:n_kv_specs + n_q_specs]
        (dq_ref, dk_ref, dv_ref, cos_ref, sin_ref, qn_ref, kn_ref,
         out_ref, dqn_ref, dkn_ref) = refs[n_kv_specs + n_q_specs:]
        i = pl.program_id(0)

        @pl.when(i == 0)
        def _():
            dqn_ref[...] = jnp.zeros_like(dqn_ref)
            dkn_ref[...] = jnp.zeros_like(dkn_ref)

        kv = _cat(kv_refs)
        qv = _cat(q_refs)
        cs, sn = cos_ref[...], sin_ref[...]
        is_lat = (i < lat_tiles).astype(F32)

        def head_bwd(xh, dhat, gvec):
            dn = dhat * cs + _swap_pairs(dhat * sn)
            rs = lax.rsqrt(jnp.mean(xh * xh, axis=-1, keepdims=True) + EPS)
            xn = xh * rs
            gy = dn * gvec
            dx = rs * (gy - xn * jnp.mean(gy * xn, axis=-1, keepdims=True))
            return dx, jnp.sum(dn * xn, axis=0, keepdims=True)

        dkn = jnp.zeros((1, HEAD_DIM), F32)
        for h in range(nkv):
            sl = slice(h * HEAD_DIM, (h + 1) * HEAD_DIM)
            dx, dgv = head_bwd(kv[:, sl], dk_ref[:, sl], kn_ref[...])
            out_ref[:, sl] = dx.astype(BF16)
            dkn = dkn + dgv
        dkn_ref[...] += dkn
        out_ref[:, kvw:2 * kvw] = dv_ref[...].astype(BF16)
        dqn = jnp.zeros((1, HEAD_DIM), F32)
        for h in range(nq):
            sl = slice(h * HEAD_DIM, (h + 1) * HEAD_DIM)
            dx, dgv = head_bwd(qv[:, sl], dq_ref[:, sl] * (scale * is_lat), qn_ref[...])
            out_ref[:, 2 * kvw + h * HEAD_DIM:2 * kvw + (h + 1) * HEAD_DIM] = dx.astype(BF16)
            dqn = dqn + dgv
        dqn_ref[...] += dqn

    rm = lambda i: i
    wout = 2 * kvw + aw
    in_specs = (_col_specs(tr, 0, 2 * kvw, cb, rm) + _col_specs(tr, 2 * kvw, aw, cb, rm)
                + [pl.BlockSpec((tr, aw), lambda i: (jnp.minimum(i, lat_tiles - 1), 0)),
                   pl.BlockSpec((tr, kvw), lambda i: (i, 0)), pl.BlockSpec((tr, kvw), lambda i: (i, 0)),
                   pl.BlockSpec((tr, HEAD_DIM), lambda i: (i, 0)), pl.BlockSpec((tr, HEAD_DIM), lambda i: (i, 0)),
                   _vec(HEAD_DIM), _vec(HEAD_DIM)])
    args = [p0] * (n_kv_specs + n_q_specs) + [dq_hat, dk_hat, dv, cos_i, sin_s, q_norm, k_norm]
    return pl.pallas_call(
        body, name="qk_prep_bwd", grid=(rows // tr,), in_specs=in_specs,
        out_specs=[pl.BlockSpec((tr, wout), lambda i: (i, 0)), _vec(HEAD_DIM), _vec(HEAD_DIM)],
        out_shape=[jax.ShapeDtypeStruct((rows, wout), BF16), jax.ShapeDtypeStruct((1, HEAD_DIM), F32),
                   jax.ShapeDtypeStruct((1, HEAD_DIM), F32)],
        compiler_params=_cp(("arbitrary",)))(*args)


def _stack_heads(x, g):
    return jnp.concatenate([x[:, h * HEAD_DIM:(h + 1) * HEAD_DIM] for h in range(g)], axis=0)


def _flash_fwd(q_hat, k_all, v_all, n_lat, dims, tq, tk):
    kvw, aw = dims["kv_w"], dims["attn_w"]
    nkv = kvw // HEAD_DIM
    g = aw // kvw
    gw = g * HEAD_DIM
    n_keys = k_all.shape[0]
    ni, nj = n_lat // tq, n_keys // tk
    dn_nt = (((1,), (1,)), ((), ()))

    def body(q_ref, k_ref, v_ref, o_ref, lse_ref):
        qs = _stack_heads(q_ref[...], g)
        m = jnp.full((g * tq, 1), -1e30, F32)
        l = jnp.zeros((g * tq, 1), F32)
        acc = jnp.zeros((g * tq, HEAD_DIM), F32)
        for j in range(nj):
            kb = k_ref[pl.ds(j * tk, tk), :]
            vb = v_ref[pl.ds(j * tk, tk), :]
            s = lax.dot_general(qs, kb, dn_nt, preferred_element_type=F32)
            m_new = jnp.maximum(m, jnp.max(s, axis=-1, keepdims=True))
            alpha = jnp.exp(m - m_new)
            p = jnp.exp(s - m_new)
            l = alpha * l + jnp.sum(p, axis=-1, keepdims=True)
            acc = alpha * acc + jnp.dot(p.astype(BF16), vb, preferred_element_type=F32)
            m = m_new
        o = acc / l
        for h in range(g):
            o_ref[:, h * HEAD_DIM:(h + 1) * HEAD_DIM] = o[h * tq:(h + 1) * tq]
        lse_ref[...] = m + jnp.log(l)

    return pl.pallas_call(
        body, name="flash_fwd", grid=(nkv, ni),
        in_specs=[pl.BlockSpec((tq, gw), lambda h, i: (i, h)),
                  pl.BlockSpec((n_keys, HEAD_DIM), lambda h, i: (0, h)),
                  pl.BlockSpec((n_keys, HEAD_DIM), lambda h, i: (0, h))],
        out_specs=[pl.BlockSpec((tq, gw), lambda h, i: (i, h)),
                   pl.BlockSpec((g * tq, 1), lambda h, i: (h * ni + i, 0))],
        out_shape=[jax.ShapeDtypeStruct((n_lat, aw), F32), jax.ShapeDtypeStruct((nkv * ni * g * tq, 1), F32)],
        compiler_params=_cp(("parallel", "parallel")))(q_hat, k_all, v_all)


def _flash_bwd(q_hat, k_all, v_all, do, o, lse, n_lat, dims, tq, tk):
    kvw, aw = dims["kv_w"], dims["attn_w"]
    nkv = kvw // HEAD_DIM
    g = aw // kvw
    gw = g * HEAD_DIM
    n_keys = k_all.shape[0]
    ni, nj = n_lat // tq, n_keys // tk
    dn_nt = (((1,), (1,)), ((), ()))
    dn_tn = (((0,), (0,)), ((), ()))

    def body(q_ref, k_ref, v_ref, do_ref, o_ref, lse_ref, dq_ref, dk_ref, dv_ref):
        i = pl.program_id(1)

        @pl.when(i == 0)
        def _():
            dk_ref[...] = jnp.zeros_like(dk_ref)
            dv_ref[...] = jnp.zeros_like(dv_ref)

        dos = _stack_heads(do_ref[...], g)
        qs = _stack_heads(q_ref[...], g)
        delta = jnp.sum(dos.astype(F32) * _stack_heads(o_ref[...], g), axis=-1, keepdims=True)
        lse_v = lse_ref[...]
        dq = jnp.zeros((g * tq, HEAD_DIM), F32)
        for j in range(nj):
            rows = pl.ds(j * tk, tk)
            kb, vb = k_ref[rows, :], v_ref[rows, :]
            s = lax.dot_general(qs, kb, dn_nt, preferred_element_type=F32)
            p = jnp.exp(s - lse_v)
            dp = lax.dot_general(dos, vb, dn_nt, preferred_element_type=F32)
            ds = (p * (dp - delta)).astype(BF16)
            dv_ref[rows, :] += lax.dot_general(p.astype(BF16), dos, dn_tn, preferred_element_type=F32)
            dk_ref[rows, :] += lax.dot_general(ds, qs, dn_tn, preferred_element_type=F32)
            dq = dq + jnp.dot(ds, kb, preferred_element_type=F32)
        for h in range(g):
            dq_ref[:, h * HEAD_DIM:(h + 1) * HEAD_DIM] = dq[h * tq:(h + 1) * tq]

    qspec = pl.BlockSpec((tq, gw), lambda h, i: (i, h))
    full_k = pl.BlockSpec((n_keys, HEAD_DIM), lambda h, i: (0, h))
    return pl.pallas_call(
        body, name="flash_bwd", grid=(nkv, ni),
        in_specs=[qspec, full_k, full_k, qspec, qspec, pl.BlockSpec((g * tq, 1), lambda h, i: (h * ni + i, 0))],
        out_specs=[qspec, full_k, full_k],
        out_shape=[jax.ShapeDtypeStruct((n_lat, aw), F32), jax.ShapeDtypeStruct((n_keys, kvw), F32),
                   jax.ShapeDtypeStruct((n_keys, kvw), F32)],
        compiler_params=_cp(("parallel", "arbitrary")))(q_hat, k_all, v_all, do, o, lse)


def _halo_maps(tr, n_tiles):
    per = tr // CONV_HALO
    prev = lambda i: jnp.maximum(i * per - 1, 0)
    nxt = lambda i: (i + 1) * per
    return prev, nxt


def _mix_fwd(attn, p0, dw_w, dw_b, ln_g, ln_b, dims, tr):
    n_lat, aw = attn.shape
    cc, cb, cw = dims["conv_ch"], dims["cb"], dims["conv_w"]
    off = dims["off"]
    n_tiles = n_lat // tr
    pad = cw // 2
    na, nc = aw // cb, cc // cb
    prev_map, next_map = _halo_maps(tr, n_tiles)

    def body(*refs):
        it = iter(refs)
        attn_ref = next(it)
        za = [next(it) for _ in range(na)]
        a_c = [next(it) for _ in range(nc)]
        b_c = [next(it) for _ in range(nc)]
        zb = [next(it) for _ in range(nc)]
        a_p = [next(it) for _ in range(nc)]
        b_p = [next(it) for _ in range(nc)]
        a_n = [next(it) for _ in range(nc)]
        b_n = [next(it) for _ in range(nc)]
        w_ref, db_ref, g_ref, bb_ref, mix_ref, yc_ref, ypad = (next(it) for _ in range(7))
        i = pl.program_id(0)
        ypad[pl.ds(0, CONV_HALO), :] = _cat(a_p) * _sigmoid(_cat(b_p)) * (i > 0).astype(F32)
        ypad[pl.ds(CONV_HALO, tr), :] = _cat(a_c) * _sigmoid(_cat(b_c))
        ypad[pl.ds(CONV_HALO + tr, CONV_HALO), :] = _cat(a_n) * _sigmoid(_cat(b_n)) * (i < n_tiles - 1).astype(F32)
        acc = jnp.broadcast_to(db_ref[...], (tr, cc))
        for k in range(cw):
            acc = acc + ypad[pl.ds(CONV_HALO - pad + k, tr), :] * w_ref[pl.ds(k, 1), :]
        yc_ref[...] = acc
        mu = jnp.mean(acc, axis=-1, keepdims=True)
        xc = acc - mu
        rs = lax.rsqrt(jnp.mean(xc * xc, axis=-1, keepdims=True) + EPS)
        nv = xc * rs * g_ref[...] + bb_ref[...]
        mix_ref[:, :aw] = (attn_ref[...] * _silu(_cat(za))).astype(BF16)
        mix_ref[:, aw:] = (_silu(nv) * _silu(_cat(zb))).astype(BF16)

    rm = lambda i: i
    in_specs = ([pl.BlockSpec((tr, aw), lambda i: (i, 0))]
                + _col_specs(tr, off["za"], aw, cb, rm) + _col_specs(tr, off["a"], cc, cb, rm)
                + _col_specs(tr, off["b"], cc, cb, rm) + _col_specs(tr, off["zb"], cc, cb, rm)
                + _col_specs(CONV_HALO, off["a"], cc, cb, prev_map) + _col_specs(CONV_HALO, off["b"], cc, cb, prev_map)
                + _col_specs(CONV_HALO, off["a"], cc, cb, next_map) + _col_specs(CONV_HALO, off["b"], cc, cb, next_map)
                + [pl.BlockSpec(dw_w.shape, lambda i: (0, 0)), _vec(cc), _vec(cc), _vec(cc)])
    args = [attn] + [p0] * (na + 7 * nc) + [dw_w, dw_b, ln_g, ln_b]
    return pl.pallas_call(
        body, name="mix_fwd", grid=(n_tiles,), in_specs=in_specs,
        out_specs=[pl.BlockSpec((tr, aw + cc), lambda i: (i, 0)), pl.BlockSpec((tr, cc), lambda i: (i, 0))],
        out_shape=[jax.ShapeDtypeStruct((n_lat, aw + cc), BF16), jax.ShapeDtypeStruct((n_lat, cc), F32)],
        scratch_shapes=[pltpu.VMEM((tr + 2 * CONV_HALO, cc), F32)],
        compiler_params=_cp(("parallel",)))(*args)


def _mix_bwd_pointwise(dmix, attn, p0, yc, ln_g, ln_b, dims, tr):
    n_lat, aw = attn.shape
    cc, cb, off = dims["conv_ch"], dims["cb"], dims["off"]
    na, nc = aw // cb, cc // cb

    def body(*refs):
        it = iter(refs)
        dmix_ref, attn_ref = next(it), next(it)
        za = [next(it) for _ in range(na)]
        zb = [next(it) for _ in range(nc)]
        yc_ref, g_ref, bb_ref = next(it), next(it), next(it)
        dattn_ref, dza_ref, dzb_ref, dyc_ref, dg_ref, dbb_ref, ddb_ref = (next(it) for _ in range(7))
        i = pl.program_id(0)

        @pl.when(i == 0)
        def _():
            dg_ref[...] = jnp.zeros_like(dg_ref)
            dbb_ref[...] = jnp.zeros_like(dbb_ref)
            ddb_ref[...] = jnp.zeros_like(ddb_ref)

        dm = dmix_ref[...]
        dma, dmb = dm[:, :aw], dm[:, aw:]
        zav, zbv = _cat(za), _cat(zb)
        dattn_ref[...] = (dma * _silu(zav)).astype(BF16)
        dza_ref[...] = (dma * attn_ref[...] * _dsilu(zav)).astype(BF16)
        ycv = yc_ref[...]
        mu = jnp.mean(ycv, axis=-1, keepdims=True)
        xc = ycv - mu
        rs = lax.rsqrt(jnp.mean(xc * xc, axis=-1, keepdims=True) + EPS)
        xh = xc * rs
        nv = xh * g_ref[...] + bb_ref[...]
        dzb_ref[...] = (dmb * _silu(nv) * _dsilu(zbv)).astype(BF16)
        dn = dmb * _silu(zbv) * _dsilu(nv)
        dg_ref[...] += jnp.sum(dn * xh, axis=0, keepdims=True)
        dbb_ref[...] += jnp.sum(dn, axis=0, keepdims=True)
        dxh = dn * g_ref[...]
        dyc = rs * (dxh - jnp.mean(dxh, axis=-1, keepdims=True) - xh * jnp.mean(dxh * xh, axis=-1, keepdims=True))
        dyc_ref[...] = dyc
        ddb_ref[...] += jnp.sum(dyc, axis=0, keepdims=True)

    rm = lambda i: i
    row = lambda w: pl.BlockSpec((tr, w), lambda i: (i, 0))
    in_specs = ([row(aw + cc), row(aw)] + _col_specs(tr, off["za"], aw, cb, rm)
                + _col_specs(tr, off["zb"], cc, cb, rm) + [row(cc), _vec(cc), _vec(cc)])
    args = [dmix, attn] + [p0] * (na + nc) + [yc, ln_g, ln_b]
    return pl.pallas_call(
        body, name="mix_bwd_pointwise", grid=(n_lat // tr,), in_specs=in_specs,
        out_specs=[row(aw), row(aw), row(cc), row(cc), _vec(cc), _vec(cc), _vec(cc)],
        out_shape=[jax.ShapeDtypeStruct((n_lat, aw), BF16), jax.ShapeDtypeStruct((n_lat, aw), BF16),
                   jax.ShapeDtypeStruct((n_lat, cc), BF16), jax.ShapeDtypeStruct((n_lat, cc), F32)]
                  + [jax.ShapeDtypeStruct((1, cc), F32)] * 3,
        compiler_params=_cp(("arbitrary",)))(*args)


def _conv_bwd(dyc, p0, dw_w, dims, tr):
    n_lat, cc = dyc.shape
    cb, cw, off = dims["cb"], dims["conv_w"], dims["off"]
    nc = cc // cb
    n_tiles = n_lat // tr
    pad = cw // 2
    prev_map, next_map = _halo_maps(tr, n_tiles)

    def body(*refs):
        it = iter(refs)
        a_c = [next(it) for _ in range(nc)]
        b_c = [next(it) for _ in range(nc)]
        a_p = [next(it) for _ in range(nc)]
        b_p = [next(it) for _ in range(nc)]
        a_n = [next(it) for _ in range(nc)]
        b_n = [next(it) for _ in range(nc)]
        d_c, d_p, d_n, w_ref, dab_ref, dw_ref, ypad, dpad = (next(it) for _ in range(8))
        i = pl.program_id(0)

        @pl.when(i == 0)
        def _():
            dw_ref[...] = jnp.zeros_like(dw_ref)

        first, last = (i > 0).astype(F32), (i < n_tiles - 1).astype(F32)
        av, bv = _cat(a_c), _cat(b_c)
        sg = _sigmoid(bv)
        ypad[pl.ds(0, CONV_HALO), :] = _cat(a_p) * _sigmoid(_cat(b_p)) * first
        ypad[pl.ds(CONV_HALO, tr), :] = av * sg
        ypad[pl.ds(CONV_HALO + tr, CONV_HALO), :] = _cat(a_n) * _sigmoid(_cat(b_n)) * last
        dcur = d_c[...]
        dpad[pl.ds(0, CONV_HALO), :] = d_p[...] * first
        dpad[pl.ds(CONV_HALO, tr), :] = dcur
        dpad[pl.ds(CONV_HALO + tr, CONV_HALO), :] = d_n[...] * last
        dy = jnp.zeros((tr, cc), F32)
        for k in range(cw):
            dy = dy + dpad[pl.ds(CONV_HALO + pad - k, tr), :] * w_ref[pl.ds(k, 1), :]
            dw_ref[pl.ds(k, 1), :] += jnp.sum(dcur * ypad[pl.ds(CONV_HALO - pad + k, tr), :], axis=0, keepdims=True)
        dab_ref[:, :cc] = (dy * sg).astype(BF16)
        dab_ref[:, cc:] = (dy * av * sg * (1.0 - sg)).astype(BF16)

    rm = lambda i: i
    in_specs = (_col_specs(tr, off["a"], cc, cb, rm) + _col_specs(tr, off["b"], cc, cb, rm)
                + _col_specs(CONV_HALO, off["a"], cc, cb, prev_map) + _col_specs(CONV_HALO, off["b"], cc, cb, prev_map)
                + _col_specs(CONV_HALO, off["a"], cc, cb, next_map) + _col_specs(CONV_HALO, off["b"], cc, cb, next_map)
                + [pl.BlockSpec((tr, cc), lambda i: (i, 0)),
                   pl.BlockSpec((CONV_HALO, cc), lambda i: (prev_map(i), 0)),
                   pl.BlockSpec((CONV_HALO, cc), lambda i: (jnp.minimum(next_map(i), n_lat // CONV_HALO - 1), 0)),
                   pl.BlockSpec(dw_w.shape, lambda i: (0, 0))])
    args = [p0] * (6 * nc) + [dyc, dyc, dyc, dw_w]
    return pl.pallas_call(
        body, name="conv_bwd", grid=(n_tiles,), in_specs=in_specs,
        out_specs=[pl.BlockSpec((tr, 2 * cc), lambda i: (i, 0)), pl.BlockSpec(dw_w.shape, lambda i: (0, 0))],
        out_shape=[jax.ShapeDtypeStruct((n_lat, 2 * cc), BF16), jax.ShapeDtypeStruct(dw_w.shape, F32)],
        scratch_shapes=[pltpu.VMEM((tr + 2 * CONV_HALO, cc), F32), pltpu.VMEM((tr + 2 * CONV_HALO, cc), F32)],
        compiler_params=_cp(("arbitrary",)))(*args)


def _sgu_parts(pu, pv, ln_g, ln_b):
    u = _gelu(pu)
    v = _gelu(pv)
    mu = jnp.mean(v, axis=-1, keepdims=True)
    xc = v - mu
    rs = lax.rsqrt(jnp.mean(xc * xc, axis=-1, keepdims=True) + EPS)
    xh = xc * rs
    return u, xh, rs, xh * ln_g + ln_b


def _sgu_fwd(p1, ln_g, ln_b, ws, bs_t, tr):
    n_lat, w3 = p1.shape
    w = w3 // 3
    ng, ch = ws.shape[0], ws.shape[1]
    gwid = w // ng
    n_ch = tr // ch

    def body(pu_ref, pv_ref, pg_ref, g_ref, b_ref, ws_ref, bs_ref, o_ref):
        u, _, _, vln = _sgu_parts(pu_ref[...], pv_ref[...], g_ref[...], b_ref[...])
        gate = _silu(pg_ref[...])
        vb = vln.astype(BF16)
        for c in range(n_ch):
            rs_ = slice(c * ch, (c + 1) * ch)
            for gi in range(ng):
                cs_ = slice(gi * gwid, (gi + 1) * gwid)
                mixed = jnp.dot(ws_ref[gi], vb[rs_, cs_], preferred_element_type=F32) + bs_ref[:, gi:gi + 1]
                o_ref[rs_, cs_] = (u[rs_, cs_] * mixed * gate[rs_, cs_]).astype(BF16)

    col = lambda t: pl.BlockSpec((tr, w), lambda i, _t=t: (i, _t))
    return pl.pallas_call(
        body, name="sgu_fwd", grid=(n_lat // tr,),
        in_specs=[col(0), col(1), col(2), _vec(w), _vec(w),
                  pl.BlockSpec(ws.shape, lambda i: (0, 0, 0)), pl.BlockSpec(bs_t.shape, lambda i: (0, 0))],
        out_specs=pl.BlockSpec((tr, w), lambda i: (i, 0)),
        out_shape=jax.ShapeDtypeStruct((n_lat, w), BF16),
        compiler_params=_cp(("parallel",)))(p1, p1, p1, ln_g, ln_b, ws, bs_t)


def _sgu_bwd(dm, p1, ln_g, ln_b, ws, ws_t, bs_t, tr):
    n_lat, w3 = p1.shape
    w = w3 // 3
    ng, ch = ws.shape[0], ws.shape[1]
    gwid = w // ng
    n_ch = tr // ch
    dn_nt = (((1,), (1,)), ((), ()))

    def body(dm_ref, pu_ref, pv_ref, pg_ref, g_ref, b_ref, ws_ref, wst_ref, bs_ref,
             dp_ref, dws_ref, dbs_ref, dg_ref, dbb_ref, dvln_scr):
        i = pl.program_id(0)

        @pl.when(i == 0)
        def _():
            dws_ref[...] = jnp.zeros_like(dws_ref)
            dbs_ref[...] = jnp.zeros_like(dbs_ref)
            dg_ref[...] = jnp.zeros_like(dg_ref)
            dbb_ref[...] = jnp.zeros_like(dbb_ref)

        puv, pvv, pgv = pu_ref[...], pv_ref[...], pg_ref[...]
        u, xh, rs, vln = _sgu_parts(puv, pvv, g_ref[...], b_ref[...])
        gate = _silu(pgv)
        dmv = dm_ref[...]
        vb = vln.astype(BF16)
        dmu = dmv * u
        dbs_cols = [jnp.zeros((ch, 1), F32) for _ in range(ng)]
        for c in range(n_ch):
            rs_ = slice(c * ch, (c + 1) * ch)
            for gi in range(ng):
                cs_ = slice(gi * gwid, (gi + 1) * gwid)
                mixed = jnp.dot(ws_ref[gi], vb[rs_, cs_], preferred_element_type=F32) + bs_ref[:, gi:gi + 1]
                dmixed = dmu[rs_, cs_] * gate[rs_, cs_]
                dmb = dmixed.astype(BF16)
                dp_ref[rs_, gi * gwid:(gi + 1) * gwid] = (
                    dmv[rs_, cs_] * mixed * gate[rs_, cs_] * _dgelu(puv[rs_, cs_])).astype(BF16)
                dp_ref[rs_, 2 * w + gi * gwid:2 * w + (gi + 1) * gwid] = (
                    dmu[rs_, cs_] * mixed * _dsilu(pgv[rs_, cs_])).astype(BF16)
                dvln_scr[rs_, cs_] = jnp.dot(wst_ref[gi], dmb, preferred_element_type=F32)
                dws_ref[gi] += lax.dot_general(dmb, vb[rs_, cs_], dn_nt, preferred_element_type=F32)
                dbs_cols[gi] = dbs_cols[gi] + jnp.sum(dmixed, axis=-1, keepdims=True)
        dbs_ref[...] += jnp.concatenate(dbs_cols, axis=1)
        dvln = dvln_scr[...]
        dg_ref[...] += jnp.sum(dvln * xh, axis=0, keepdims=True)
        dbb_ref[...] += jnp.sum(dvln, axis=0, keepdims=True)
        dxh = dvln * g_ref[...]
        dv = rs * (dxh - jnp.mean(dxh, axis=-1, keepdims=True) - xh * jnp.mean(dxh * xh, axis=-1, keepdims=True))
        dp_ref[:, w:2 * w] = (dv * _dgelu(pvv)).astype(BF16)

    col = lambda t: pl.BlockSpec((tr, w), lambda i, _t=t: (i, _t))
    return pl.pallas_call(
        body, name="sgu_bwd", grid=(n_lat // tr,),
        in_specs=[pl.BlockSpec((tr, w), lambda i: (i, 0)), col(0), col(1), col(2), _vec(w), _vec(w),
                  pl.BlockSpec(ws.shape, lambda i: (0, 0, 0)), pl.BlockSpec(ws.shape, lambda i: (0, 0, 0)),
                  pl.BlockSpec(bs_t.shape, lambda i: (0, 0))],
        out_specs=[pl.BlockSpec((tr, w3), lambda i: (i, 0)), pl.BlockSpec(ws.shape, lambda i: (0, 0, 0)),
                   pl.BlockSpec(bs_t.shape, lambda i: (0, 0)), _vec(w), _vec(w)],
        out_shape=[jax.ShapeDtypeStruct((n_lat, w3), BF16), jax.ShapeDtypeStruct(ws.shape, F32),
                   jax.ShapeDtypeStruct(bs_t.shape, F32), jax.ShapeDtypeStruct((1, w), F32),
                   jax.ShapeDtypeStruct((1, w), F32)],
        scratch_shapes=[pltpu.VMEM((tr, w), F32)],
        compiler_params=_cp(("arbitrary",)))(dm, p1, p1, p1, ln_g, ln_b, ws, ws_t, bs_t)


def _final_loss(x2, target, final_g, o_prev, gate_prev, tr):
    n_lat, d = x2.shape

    def body(x_ref, t_ref, g_ref, o_ref, gp_ref, dx_ref, do_ref, ls_ref, dg_ref, dgp_ref):
        i = pl.program_id(0)

        @pl.when(i == 0)
        def _():
            ls_ref[...] = jnp.zeros_like(ls_ref)
            dg_ref[...] = jnp.zeros_like(dg_ref)
            dgp_ref[...] = jnp.zeros_like(dgp_ref)

        xv = x_ref[...]
        gv = g_ref[...]
        rs = lax.rsqrt(jnp.mean(xv * xv, axis=-1, keepdims=True) + EPS)
        xn = xv * rs
        err = xn * gv - t_ref[...]
        ls_ref[...] += jnp.sum(err * err, axis=0, keepdims=True)
        dy = err * (1.0 / d)
        dg_ref[...] += jnp.sum(dy * xn, axis=0, keepdims=True)
        gy = dy * gv
        dx = rs * (gy - xn * jnp.mean(gy * xn, axis=-1, keepdims=True))
        dx_ref[...] = dx
        do_ref[...] = (gp_ref[...] * dx).astype(BF16)
        dgp_ref[...] += jnp.sum(dx * o_ref[...], axis=0, keepdims=True)

    row = pl.BlockSpec((tr, d), lambda i: (i, 0))
    return pl.pallas_call(
        body, name="final_loss", grid=(n_lat // tr,), in_specs=[row, row, _vec(d), row, _vec(d)],
        out_specs=[row, row, _vec(d), _vec(d), _vec(d)],
        out_shape=[jax.ShapeDtypeStruct((n_lat, d), F32), jax.ShapeDtypeStruct((n_lat, d), BF16)]
                  + [jax.ShapeDtypeStruct((1, d), F32)] * 3,
        compiler_params=_cp(("arbitrary",)))(x2, target, final_g, o_prev, gate_prev)


def _pack(arrays):
    flat = jnp.concatenate([a.reshape(-1).astype(F32) for a in arrays])
    n = flat.shape[0]
    rows = -(-n // LANES)
    rows = -(-rows // PACK_ROWS) * PACK_ROWS
    return jnp.pad(flat, (0, rows * LANES - n)).reshape(rows, LANES)


def _unpack(buf, shapes):
    flat = buf.reshape(buf.shape[:-2] + (-1,))
    out, pos = [], 0
    for shp in shapes:
        n = math.prod(shp)
        out.append(flat[..., pos:pos + n].reshape(buf.shape[:-2] + tuple(shp)))
        pos += n
    return out


def _rope_tables(n_lat, n_ctx):
    rows = n_lat // GRID_W
    row = jnp.repeat(jnp.arange(rows, dtype=F32), GRID_W)
    col = jnp.tile(jnp.arange(GRID_W, dtype=F32), rows)
    n_freq, axis_dim = HEAD_DIM // 4, HEAD_DIM // 2
    inv = jnp.power(ROPE_THETA, jnp.arange(n_freq, dtype=F32) * (-2.0 / axis_dim))
    ang = jnp.concatenate([row[:, None] * inv, col[:, None] * inv], axis=-1)
    cos, sin = jnp.cos(ang), jnp.sin(ang)
    cos_i = jnp.repeat(cos, 2, axis=-1)
    sin_s = jnp.stack([-sin, sin], axis=-1).reshape(n_lat, HEAD_DIM)
    cos_i = jnp.concatenate([cos_i, jnp.ones((n_ctx, HEAD_DIM), F32)], axis=0)
    sin_s = jnp.concatenate([sin_s, jnp.zeros((n_ctx, HEAD_DIM), F32)], axis=0)
    return cos_i, sin_s


def kernel(x, c, ctx, c_ctx, ada_w, ada_b, norm_g, ev_w_in, ev_q_norm, ev_k_norm, ev_dw_w, ev_dw_b, ev_ln_g, ev_ln_b, ev_w_out, od_w_in, od_ln_g, od_ln_b, od_ws, od_bs, od_w_out, final_g, loss_target, m_c_ctx, m_ada_w, m_ada_b, m_norm_g, m_ev_w_in, m_ev_q_norm, m_ev_k_norm, m_ev_dw_w, m_ev_dw_b, m_ev_ln_g, m_ev_ln_b, m_ev_w_out, m_od_w_in, m_od_ln_g, m_od_ln_b, m_od_ws, m_od_bs, m_od_w_out, m_final_g, v_c_ctx, v_ada_w, v_ada_b, v_norm_g, v_ev_w_in, v_ev_q_norm, v_ev_k_norm, v_ev_dw_w, v_ev_dw_b, v_ev_ln_g, v_ev_ln_b, v_ev_w_out, v_od_w_in, v_od_ln_g, v_od_ln_b, v_od_ws, v_od_bs, v_od_w_out, v_final_g):
    weights = dict(c_ctx=c_ctx, ada_w=ada_w, ada_b=ada_b, norm_g=norm_g, ev_w_in=ev_w_in, ev_q_norm=ev_q_norm,
                   ev_k_norm=ev_k_norm, ev_dw_w=ev_dw_w, ev_dw_b=ev_dw_b, ev_ln_g=ev_ln_g, ev_ln_b=ev_ln_b,
                   ev_w_out=ev_w_out, od_w_in=od_w_in, od_ln_g=od_ln_g, od_ln_b=od_ln_b, od_ws=od_ws, od_bs=od_bs,
                   od_w_out=od_w_out, final_g=final_g)
    mom_m = dict(c_ctx=m_c_ctx, ada_w=m_ada_w, ada_b=m_ada_b, norm_g=m_norm_g, ev_w_in=m_ev_w_in,
                 ev_q_norm=m_ev_q_norm, ev_k_norm=m_ev_k_norm, ev_dw_w=m_ev_dw_w, ev_dw_b=m_ev_dw_b,
                 ev_ln_g=m_ev_ln_g, ev_ln_b=m_ev_ln_b, ev_w_out=m_ev_w_out, od_w_in=m_od_w_in, od_ln_g=m_od_ln_g,
                 od_ln_b=m_od_ln_b, od_ws=m_od_ws, od_bs=m_od_bs, od_w_out=m_od_w_out, final_g=m_final_g)
    mom_v = dict(c_ctx=v_c_ctx, ada_w=v_ada_w, ada_b=v_ada_b, norm_g=v_norm_g, ev_w_in=v_ev_w_in,
                 ev_q_norm=v_ev_q_norm, ev_k_norm=v_ev_k_norm, ev_dw_w=v_ev_dw_w, ev_dw_b=v_ev_dw_b,
                 ev_ln_g=v_ev_ln_g, ev_ln_b=v_ev_ln_b, ev_w_out=v_ev_w_out, od_w_in=v_od_w_in, od_ln_g=v_od_ln_g,
                 od_ln_b=v_od_ln_b, od_ws=v_od_ws, od_bs=v_od_bs, od_w_out=v_od_w_out, final_g=v_final_g)
    order = list(weights)

    _, n_lat, d = x.shape
    n_ctx = ctx.shape[1]
    n_ext = n_lat + n_ctx
    ev_in = ev_w_in.shape[-1] * N_CHIP
    ev_mix = ev_w_out.shape[1] * N_CHIP
    conv_ch = ev_dw_b.shape[-1]
    conv_w = ev_dw_w.shape[1]
    attn_w = ev_mix - conv_ch
    kv_w = N_KV_HEADS * HEAD_DIM
    assert ev_in == 2 * kv_w + 2 * attn_w + 3 * conv_ch and conv_w // 2 < CONV_HALO
    sgu_w = od_w_out.shape[1] * N_CHIP
    wa = ada_w.shape[-1]
    cb = math.gcd(2 * kv_w, attn_w, conv_ch)
    off = dict(k=0, v=kv_w, q=2 * kv_w, za=2 * kv_w + attn_w, a=2 * kv_w + 2 * attn_w,
               b=2 * kv_w + 2 * attn_w + conv_ch, zb=2 * kv_w + 2 * attn_w + 2 * conv_ch)
    dims = dict(kv_w=kv_w, attn_w=attn_w, conv_ch=conv_ch, conv_w=conv_w, cb=cb, off=off)
    tr = 256 if (n_lat % 256 == 0 and n_ctx % 256 == 0) else 128

    mx, my, mc = lax.axis_index("x"), lax.axis_index("y"), lax.axis_index("c")
    me = 4 * mx + 2 * my + mc
    chip = 2 * mx + my

    x2d, tgt2d, ctx2d = x[0], loss_target[0], ctx[0]
    ev_dw_w_l = ev_dw_w[0]
    dwc = ev_dw_w_l.shape[1]
    lnc = od_ln_g.shape[1]

    g_c = _allgather_small(jnp.broadcast_to(c, (8, d)), "gather_cond")[:, 0, :]
    c_rows = jnp.concatenate([g_c, c_ctx[None, :], jnp.zeros((MOD_ROWS - N_DEV - 1, d), F32)], axis=0)
    c_rows_t = c_rows.T
    ada_b_shard = lax.dynamic_slice_in_dim(ada_b, chip * wa, wa, axis=1)[:, None, :]
    mod_part = _mod_fwd(c_rows_t, ada_w, ada_b_shard)
    part_shapes = [(2, MOD_ROWS, wa), (conv_w, dwc), (1, lnc), (1, lnc)]
    g_parts = _allgather_small(_pack([mod_part, ev_dw_w_l, od_ln_g, od_ln_b]), "gather_mod")
    per_chip = [_unpack(g_parts[2 * s], part_shapes) for s in range(N_CHIP)]
    mod_all = jnp.concatenate([p[0] for p in per_chip], axis=-1)
    dw_w_full = jnp.concatenate([p[1] for p in per_chip], axis=-1)
    od_ln_g_full = jnp.concatenate([p[2] for p in per_chip], axis=-1)
    od_ln_b_full = jnp.concatenate([p[3] for p in per_chip], axis=-1)
    dw_w_pad = jnp.pad(dw_w_full, ((0, 2 * CONV_HALO - conv_w), (0, 0)))
    mod_me = lax.dynamic_slice_in_dim(mod_all, me, 1, axis=1)
    shift0, scale0, gate0 = mod_me[0, :, :d], mod_me[0, :, d:2 * d], mod_me[0, :, 2 * d:]
    shift1, scale1, gate1 = mod_me[1, :, :d], mod_me[1, :, d:2 * d], mod_me[1, :, 2 * d:]
    shift_c, scale_c = mod_all[0, N_DEV:N_DEV + 1, :d], mod_all[0, N_DEV:N_DEV + 1, d:2 * d]
    g0, g1 = norm_g[0:1], norm_g[1:2]

    lay = dict(ev_w_in=_Sharded((d, ev_in), True), ev_w_out=_Sharded((ev_mix, d), False),
               od_w_in=_Sharded((d, 3 * sgu_w), True), od_w_out=_Sharded((sgu_w, d), False))
    big = list(lay)
    chip_arr = jnp.reshape(chip, (1,)).astype(jnp.int32)
    core_arr = jnp.reshape(mc, (1,)).astype(jnp.int32)
    own = [_cast_into_full(weights[n][0], lay[n], chip_arr, f"cast_{n}") for n in big]
    w_full = dict(zip(big, _gather_weights(own, [lay[n] for n in big])))

    h0 = _adaln_fwd(x2d, g0, shift0, scale0, tr, "adaln0_fwd")
    hc = _adaln_fwd(ctx2d, g0, shift_c, scale_c, tr, "adaln0_ctx_fwd")
    h0e = jnp.concatenate([h0, hc], axis=0)
    tm_e = _pick(n_ext, (1408, 768, 640, 512, 256, 128))
    tk_e = _pick(n_ext, (768, 640, 512, 256, 128))
    tm_l = _pick(n_lat, (1024, 512, 256, 128))
    p0 = _mm(h0e, w_full["ev_w_in"], name="mm_ev_in", tm=tm_e, tn=_pick(ev_in, (512, 256, 128)), tk=d)
    cos_i, sin_s = _rope_tables(n_lat, n_ctx)
    q_hat, k_all, v_all = _qk_prep(p0, cos_i, sin_s, ev_q_norm, ev_k_norm, dims, tr)
    tq = _pick(n_lat, (256, 128))
    tkk = _pick(n_ext, (768, 640, 512, 256, 128))
    attn, lse = _flash_fwd(q_hat, k_all, v_all, n_lat, dims, tq, tkk)
    mix, yc = _mix_fwd(attn, p0, dw_w_pad, ev_dw_b, ev_ln_g, ev_ln_b, dims, tr)
    o0, x1 = _mm(mix, w_full["ev_w_out"], name="mm_ev_out", tm=tm_l, tn=_pick(d, (1024, 512, 256)),
                 tk=_pick(ev_mix, (512, 256)), res=x2d, gate=gate0)

    h1 = _adaln_fwd(x1, g1, shift1, scale1, tr, "adaln1_fwd")
    p1 = _mm(h1, w_full["od_w_in"], name="mm_od_in", tm=tm_l, tn=_pick(3 * sgu_w, (512, 256, 128)), tk=d)
    ws_b = od_ws[0].astype(BF16)
    ws_t_b = jnp.swapaxes(od_ws[0], 1, 2).astype(BF16)
    bs_t = od_bs[0].T
    m1 = _sgu_fwd(p1, od_ln_g_full, od_ln_b_full, ws_b, bs_t, tr)
    o1, x2 = _mm(m1, w_full["od_w_out"], name="mm_od_out", tm=tm_l, tn=_pick(d, (1024, 512, 256)),
                 tk=_pick(sgu_w, (512, 256)), res=x1, gate=gate1)

    dx2, do1, loss_cols, d_final_g, dgate1 = _final_loss(x2, tgt2d, final_g[None, :], o1, gate1, tr)
    loss = lax.psum(0.5 / d * jnp.sum(loss_cols), ("x", "y", "c"))

    tk_l = _pick(n_lat, (1024, 512, 256, 128))
    dm1 = _mm(do1, w_full["od_w_out"], name="mm_od_out_dx", tb=True, tm=tm_l, tn=_pick(sgu_w, (1024, 512, 256)),
              tk=_pick(d, (512, 256)))
    g_od_w_out = _mm(m1, do1, name="mm_od_out_dw", ta=True, tm=_pick(sgu_w, (1024, 512, 256)),
                     tn=_pick(d, (1024, 512, 256)), tk=tk_l)
    dp1, d_ws, d_bs_t, d_od_ln_g, d_od_ln_b = _sgu_bwd(dm1, p1, od_ln_g_full, od_ln_b_full, ws_b, ws_t_b, bs_t, tr)
    dh1 = _mm(dp1, w_full["od_w_in"], name="mm_od_in_dx", tb=True, tm=tm_l, tn=_pick(d, (1024, 512, 256)),
              tk=_pick(3 * sgu_w, (512, 256)))
    g_od_w_in = _mm(h1, dp1, name="mm_od_in_dw", ta=True, tm=_pick(d, (1024, 512, 256)),
                    tn=_pick(3 * sgu_w, (1536, 768, 512, 384, 256, 128)), tk=tk_l)
    zero_d = jnp.zeros((1, d), F32)
    b1 = _adaln_bwd(x1, dh1, 0, g1, scale1, zero_d, tr, "adaln1_bwd", dres=dx2, o_prev=o0, gate_prev=gate0)
    dx1, do0, dgate0 = b1["dx"], b1["do_prev"], b1["dgate_prev"]

    dmix = _mm(do0, w_full["ev_w_out"], name="mm_ev_out_dx", tb=True, tm=tm_l, tn=_pick(ev_mix, (1024, 512, 256)),
               tk=_pick(d, (512, 256)))
    g_ev_w_out = _mm(mix, do0, name="mm_ev_out_dw", ta=True, tm=_pick(ev_mix, (1024, 512, 256)),
                     tn=_pick(d, (1024, 512, 256)), tk=tk_l)
    dattn, dza, dzb, dyc, d_ev_ln_g, d_ev_ln_b, d_dw_b = _mix_bwd_pointwise(
        dmix, attn, p0, yc, ev_ln_g, ev_ln_b, dims, tr)
    dab, d_dw_w_pad = _conv_bwd(dyc, p0, dw_w_pad, dims, tr)
    dq_hat, dk_hat, dv_all = _flash_bwd(q_hat, k_all, v_all, dattn, attn, lse, n_lat, dims, tq, tkk)
    dkvq, d_q_norm, d_k_norm = _qk_prep_bwd(dq_hat, dk_hat, dv_all, p0, cos_i, sin_s, ev_q_norm, ev_k_norm,
                                            dims, tr, n_lat)
    rest = jnp.pad(jnp.concatenate([dza, dab, dzb], axis=1), ((0, n_ctx), (0, 0)))
    dp0 = jnp.concatenate([dkvq, rest], axis=1)
    dh0 = _mm(dp0, w_full["ev_w_in"], name="mm_ev_in_dx", tb=True, tm=tm_e, tn=_pick(d, (1024, 512, 256)),
              tk=_pick(ev_in, (512, 256, 128)))
    g_ev_w_in = _mm(h0e, dp0, name="mm_ev_in_dw", ta=True, tm=_pick(d, (1024, 512, 256)),
                    tn=_pick(ev_in, (1408, 768, 512, 256, 128)), tk=tk_e)
    bc = _adaln_bwd(ctx2d, dh0, n_lat, g0, scale_c, zero_d, tr, "adaln0_ctx_bwd")
    b0 = _adaln_bwd(x2d, dh0, 0, g0, scale0, bc["dg"], tr, "adaln0_bwd", dres=dx1)
    grad_x = b0["dx"]

    zeros_d = jnp.zeros((1, d), F32)
    dmod0 = jnp.concatenate([b0["dshift"], b0["dscale"], dgate0], axis=1)
    dmod1 = jnp.concatenate([b1["dshift"], b1["dscale"], dgate1], axis=1)
    dmodc = jnp.concatenate([bc["dshift"], bc["dscale"], zeros_d], axis=1)
    small = [dmod0, dmod1, dmodc, b0["dg"], b1["dg"], d_q_norm, d_k_norm, d_dw_w_pad[:conv_w], d_dw_b,
             d_ev_ln_g, d_ev_ln_b, d_od_ln_g, d_od_ln_b, d_ws, d_bs_t.T, d_final_g]
    small_shapes = [a.shape for a in small]
    g_small = _allgather_small(_pack(small), "gather_small_grads")
    tot = _unpack(_sum_slots(g_small, "sum_small_grads"), small_shapes)
    (t_dmod0, t_dmod1, t_dmodc, t_g0, t_g1, t_qn, t_kn, t_dw_w, t_dw_b, t_eln_g, t_eln_b, t_oln_g, t_oln_b,
     t_ws, t_bs, t_fg) = tot
    rows_dmod = _unpack(g_small, small_shapes[:2])
    dmod0_rows, dmod1_rows = rows_dmod[0][:, 0, :], rows_dmod[1][:, 0, :]
    pad_rows = jnp.zeros((MOD_ROWS - N_DEV - 1, 3 * d), F32)
    dm_l0 = jnp.concatenate([dmod0_rows, t_dmodc, pad_rows], axis=0)
    dm_l1 = jnp.concatenate([dmod1_rows, jnp.zeros((MOD_ROWS - N_DEV, 3 * d), F32)], axis=0)
    dm_shard = lax.dynamic_slice_in_dim(jnp.stack([dm_l0, dm_l1]), chip * wa, wa, axis=2)
    g_ada_w, dsc = _mod_bwd(c_rows_t, dm_shard, ada_w)
    g_dsc = _allgather_small(_pack([dsc[0]]), "gather_cctx")
    g_c_ctx = _cctx_grad(g_dsc, _pack([c_ctx])).reshape(-1)[:d]
    g_ada_b = jnp.stack([t_dmod0[0] + t_dmodc[0], t_dmod1[0]])

    lays = [lay[n] for n in big]
    g_full = [g_ev_w_in, g_ev_w_out, g_od_w_in, g_od_w_out]
    theirs = _pair_exchange(g_full, lays)
    pair_sums = [_pair_sum(g, t, l_, core_arr, f"pair_sum_{n}") for g, t, l_, n in zip(g_full, theirs, lays, big)]
    slots = _chip_scatter(pair_sums, lays)
    halves = [_chip_sum(s, sl, l_, chip_arr, core_arr, f"chip_sum_{n}")
              for s, sl, l_, n in zip(pair_sums, slots, lays, big)]
    g_big = dict(zip(big, _pair_share(halves, lays)))

    grads = dict(
        c_ctx=g_c_ctx, ada_w=g_ada_w, ada_b=g_ada_b, norm_g=jnp.concatenate([t_g0, t_g1], axis=0),
        ev_w_in=g_big["ev_w_in"][None], ev_q_norm=t_qn, ev_k_norm=t_kn,
        ev_dw_w=lax.dynamic_slice_in_dim(t_dw_w, chip * dwc, dwc, axis=1)[None], ev_dw_b=t_dw_b,
        ev_ln_g=t_eln_g, ev_ln_b=t_eln_b, ev_w_out=g_big["ev_w_out"][None], od_w_in=g_big["od_w_in"][None],
        od_ln_g=lax.dynamic_slice_in_dim(t_oln_g, chip * lnc, lnc, axis=1),
        od_ln_b=lax.dynamic_slice_in_dim(t_oln_b, chip * lnc, lnc, axis=1),
        od_ws=t_ws[None], od_bs=t_bs[None], od_w_out=g_big["od_w_out"][None], final_g=t_fg[0])
    grads = {n: grads[n].reshape(weights[n].shape) for n in order}

    delta, new_m, new_v = {}, {}, {}
    large = ("ada_w", "ev_w_in", "ev_w_out", "od_w_in", "od_w_out")
    for n in large:
        shp = weights[n].shape
        as2d = lambda a: a.reshape(-1, shp[-1])
        dl, nm, nv = _adamw(as2d(weights[n]), as2d(grads[n]), as2d(mom_m[n]), as2d(mom_v[n]), f"adamw_{n}")
        delta[n], new_m[n], new_v[n] = dl.reshape(shp), nm.reshape(shp), nv.reshape(shp)
    rest_names = [n for n in order if n not in large]
    rest_shapes = [weights[n].shape for n in rest_names]
    dl, nm, nv = _adamw(_pack([weights[n] for n in rest_names]), _pack([grads[n] for n in rest_names]),
                        _pack([mom_m[n] for n in rest_names]), _pack([mom_v[n] for n in rest_names]), "adamw_small")
    for n, a, b_, c_ in zip(rest_names, _unpack(dl, rest_shapes), _unpack(nm, rest_shapes), _unpack(nv, rest_shapes)):
        delta[n], new_m[n], new_v[n] = a, b_, c_

    return (loss, grad_x[None], *[grads[n] for n in order], *[delta[n] for n in order],
            *[new_m[n] for n in order], *[new_v[n] for n in order])
```

```python
import math

import jax
import jax.numpy as jnp
from jax import lax
from jax.experimental import pallas as pl
from jax.experimental.pallas import tpu as pltpu

F32 = jnp.float32
BF16 = jnp.bfloat16
EPS = 1e-6
GRID_W = 64
ROPE_THETA = 10000.0
HEAD_DIM = 128
N_KV_HEADS = 2
CONV_HALO = 16
LANES = 128
SUBLANES = 8
N_DEV = 8
N_CHIP = 4
MOD_ROWS = 16
PACK_ROWS = 64
ADAM_LR, ADAM_B1, ADAM_B2, ADAM_EPS, ADAM_WD, ADAM_STEP = 0.001, 0.9, 0.999, 1e-08, 0.01, 10
VMEM_LIMIT = 56 * 1024 * 1024
MESH = pl.DeviceIdType.MESH
ANY = pl.BlockSpec(memory_space=pl.ANY)
VMEM_SPEC = pl.BlockSpec(memory_space=pltpu.VMEM)
CHIP_DELTAS = ((1, 0), (0, 1), (1, 1))


def _cp(sem=None):
    return pltpu.CompilerParams(dimension_semantics=sem, vmem_limit_bytes=VMEM_LIMIT)


def _pick(n, cands):
    for c in cands:
        if n % c == 0:
            return c
    raise ValueError(f"no tile for {n} in {cands}")


def _sigmoid(x):
    return 1.0 / (1.0 + jnp.exp(-x))


def _silu(x):
    return x * _sigmoid(x)


def _dsilu(x):
    s = _sigmoid(x)
    return s * (1.0 + x * (1.0 - s))


_GELU_C = math.sqrt(2.0 / math.pi)


def _gelu(x):
    return 0.5 * x * (1.0 + jnp.tanh(_GELU_C * (x + 0.044715 * x * x * x)))


def _dgelu(x):
    t = jnp.tanh(_GELU_C * (x + 0.044715 * x * x * x))
    return 0.5 * (1.0 + t) + 0.5 * x * (1.0 - t * t) * _GELU_C * (1.0 + 3.0 * 0.044715 * x * x)


def _vec(d):
    return pl.BlockSpec((1, d), lambda *_: (0, 0))


def _cat(refs):
    return refs[0][...] if len(refs) == 1 else jnp.concatenate([r[...] for r in refs], axis=1)


def _col_specs(rows, off, width, cb, row_map):
    assert off % cb == 0 and width % cb == 0
    return [pl.BlockSpec((rows, cb), (lambda *g, _c=off // cb + t: (row_map(*g), _c))) for t in range(width // cb)]


def _my_pos():
    return lax.axis_index("x"), lax.axis_index("y"), lax.axis_index("c")


def _allgather_small(x, name):
    r, c = x.shape

    def body(x_ref, out_ref, send_sems, recv_sems, local_sem):
        mx, my, mc = _my_pos()
        me = 4 * mx + 2 * my + mc
        mine = pltpu.make_async_copy(x_ref, out_ref.at[me], local_sem)
        mine.start()
        deltas = [(dx, dy, dc) for dx in (0, 1) for dy in (0, 1) for dc in (0, 1) if (dx, dy, dc) != (0, 0, 0)]
        sends = []
        for k, (dx, dy, dc) in enumerate(deltas):
            px, py, pc = (mx + dx) % 2, (my + dy) % 2, (mc + dc) % 2
            cp = pltpu.make_async_remote_copy(
                src_ref=x_ref, dst_ref=out_ref.at[me], send_sem=send_sems.at[k], recv_sem=recv_sems.at[k],
                device_id=(px, py, pc), device_id_type=MESH)
            cp.start()
            sends.append(cp)
        for k, (dx, dy, dc) in enumerate(deltas):
            px, py, pc = (mx + dx) % 2, (my + dy) % 2, (mc + dc) % 2
            peer = 4 * px + 2 * py + pc
            pltpu.make_async_remote_copy(
                src_ref=x_ref, dst_ref=out_ref.at[peer], send_sem=send_sems.at[k], recv_sem=recv_sems.at[k],
                device_id=(px, py, pc), device_id_type=MESH).wait_recv()
        for cp in sends:
            cp.wait_send()
        mine.wait()

    return pl.pallas_call(
        body, name=name,
        out_shape=jax.ShapeDtypeStruct((N_DEV, r, c), x.dtype),
        in_specs=[VMEM_SPEC], out_specs=VMEM_SPEC,
        scratch_shapes=[pltpu.SemaphoreType.DMA((N_DEV - 1,)), pltpu.SemaphoreType.DMA((N_DEV - 1,)),
                        pltpu.SemaphoreType.DMA],
        compiler_params=pltpu.CompilerParams(vmem_limit_bytes=VMEM_LIMIT),
    )(x)


class _Sharded:
    def __init__(self, full_shape, by_cols):
        self.full = full_shape
        self.by_cols = by_cols
        rows, cols = full_shape
        if by_cols:
            self.shard, self.half, self.halves = (rows, cols // N_CHIP), (rows // 2, cols // N_CHIP), (rows // 2, cols)
        else:
            self.shard, self.half, self.halves = (rows // N_CHIP, cols), (rows // N_CHIP, cols // 2), (rows, cols // 2)

    def region(self, ref, s, h):
        if self.by_cols:
            return ref.at[pl.ds(h * self.half[0], self.half[0]), pl.ds(s * self.shard[1], self.shard[1])]
        return ref.at[pl.ds(s * self.shard[0], self.shard[0]), pl.ds(h * self.half[1], self.half[1])]

    def halves_of_full(self, ref, h):
        if self.by_cols:
            return ref.at[pl.ds(h * self.halves[0], self.halves[0]), :]
        return ref.at[:, pl.ds(h * self.halves[1], self.halves[1])]

    def region_in_halves(self, ref, s):
        if self.by_cols:
            return ref.at[:, pl.ds(s * self.shard[1], self.shard[1])]
        return ref.at[pl.ds(s * self.shard[0], self.shard[0]), :]

    def half_of_shard(self, ref, h):
        if self.by_cols:
            return ref.at[pl.ds(h * self.half[0], self.half[0]), :]
        return ref.at[:, pl.ds(h * self.half[1], self.half[1])]


def _row_tile(rows, row_bytes):
    for t in (512, 256, 128, 64, 32, 16):
        if rows % t == 0 and t * row_bytes <= 2 * 1024 * 1024:
            return t
    return 16


def _cast_into_full(w_shard, lay, chip_arr, name):
    r, c = lay.shard
    tr = _row_tile(r, c * 4)
    nt = r // tr

    def body(chip_ref, w_ref, o_ref):
        o_ref[...] = w_ref[...].astype(BF16)

    if lay.by_cols:
        out_map = lambda i, chip_ref: (i, chip_ref[0])
    else:
        out_map = lambda i, chip_ref: (chip_ref[0] * nt + i, 0)
    return pl.pallas_call(
        body, name=name,
        grid_spec=pltpu.PrefetchScalarGridSpec(
            num_scalar_prefetch=1, grid=(nt,),
            in_specs=[pl.BlockSpec((tr, c), lambda i, chip_ref: (i, 0))],
            out_specs=pl.BlockSpec((tr, c), out_map)),
        out_shape=jax.ShapeDtypeStruct(lay.full, BF16), compiler_params=_cp(("parallel",)))(chip_arr, w_shard)


class _Carried:
    def __init__(self, ins, out_shapes, aliases, sem_shape, start, finish):
        self.ins, self.out_shapes, self.aliases, self.sem_shape = list(ins), list(out_shapes), dict(aliases), sem_shape
        self.start, self.finish = start, finish

    def scratch(self):
        return [pltpu.SemaphoreType.DMA(self.sem_shape), pltpu.SemaphoreType.DMA(self.sem_shape)]

    def split(self, in_refs, out_refs, scratch_refs):
        ni, no = len(self.ins), len(self.out_shapes)
        return in_refs[len(in_refs) - ni:], out_refs[len(out_refs) - no:], scratch_refs[-2], scratch_refs[-1]


def _run_comm(comm, name):
    ni, no = len(comm.ins), len(comm.out_shapes)

    def body(*refs):
        ins, outs, send_sems, recv_sems = refs[:ni], refs[ni:ni + no], refs[ni + no], refs[ni + no + 1]
        comm.start(ins, outs, send_sems, recv_sems)
        comm.finish(ins, outs, send_sems, recv_sems)

    return pl.pallas_call(body, name=name, out_shape=comm.out_shapes, in_specs=[ANY] * ni, out_specs=[ANY] * no,
                          input_output_aliases=comm.aliases, scratch_shapes=comm.scratch())(*comm.ins)


def _gather_comm(fulls, layouts):
    n = len(fulls)

    def ici(ins, outs, send_sems, recv_sems, a, j, landed=False):
        mx, my, mc = _my_pos()
        dx, dy = CHIP_DELTAS[j]
        px, py = (mx + dx) % 2, (my + dy) % 2
        src_chip = 2 * px + py if landed else 2 * mx + my
        return pltpu.make_async_remote_copy(
            src_ref=layouts[a].region(ins[a], src_chip, mc), dst_ref=layouts[a].region(outs[a], src_chip, mc),
            send_sem=send_sems.at[a, j], recv_sem=recv_sems.at[a, j], device_id=(px, py, mc), device_id_type=MESH)

    def d2d(ins, outs, send_sems, recv_sems, a, j, landed=False):
        mx, my, mc = _my_pos()
        dx, dy = CHIP_DELTAS[j]
        other = 2 * ((mx + dx) % 2) + (my + dy) % 2
        half = 1 - mc if landed else mc
        region = layouts[a].region(outs[a], other, half)
        return pltpu.make_async_remote_copy(
            src_ref=region, dst_ref=region, send_sem=send_sems.at[a, 3 + j], recv_sem=recv_sems.at[a, 3 + j],
            device_id=(mx, my, 1 - mc), device_id_type=MESH)

    pairs = [(a, j) for a in range(n) for j in range(3)]

    def start(*r):
        for a, j in pairs:
            ici(*r, a, j).start()

    def finish(*r):
        for a, j in pairs:
            ici(*r, a, j, landed=True).wait_recv()
            d2d(*r, a, j).start()
        for a, j in pairs:
            d2d(*r, a, j, landed=True).wait_recv()
        for a, j in pairs:
            ici(*r, a, j).wait_send()
            d2d(*r, a, j).wait_send()

    return _Carried(fulls, [jax.ShapeDtypeStruct(lay.full, BF16) for lay in layouts], {a: a for a in range(n)},
                    (n, 6), start, finish)


def _pair_exchange(grads, layouts, name):
    n = len(grads)

    def body(*refs):
        ins, outs = refs[:n], refs[n:2 * n]
        send_sems, recv_sems = refs[2 * n:]
        mx, my, mc = _my_pos()
        copies = []
        for a, lay in enumerate(layouts):
            cp = pltpu.make_async_remote_copy(
                src_ref=lay.halves_of_full(ins[a], 1 - mc), dst_ref=outs[a],
                send_sem=send_sems.at[a], recv_sem=recv_sems.at[a],
                device_id=(mx, my, 1 - mc), device_id_type=MESH)
            cp.start()
            copies.append(cp)
        for cp in copies:
            cp.wait()

    return pl.pallas_call(
        body, name=name,
        out_shape=[jax.ShapeDtypeStruct(lay.halves, F32) for lay in layouts],
        in_specs=[ANY] * n, out_specs=[ANY] * n,
        scratch_shapes=[pltpu.SemaphoreType.DMA((n,)), pltpu.SemaphoreType.DMA((n,))],
    )(*grads)


def _pair_sum(g, theirs, lay, core_arr, name):
    r, c = lay.halves
    tr = _row_tile(r, c * 4)
    nt = r // tr

    def body(core_ref, g_ref, t_ref, o_ref):
        o_ref[...] = (g_ref[...] + t_ref[...]).astype(BF16)

    if lay.by_cols:
        g_map = lambda i, core_ref: (core_ref[0] * nt + i, 0)
    else:
        g_map = lambda i, core_ref: (i, core_ref[0])
    plain = pl.BlockSpec((tr, c), lambda i, core_ref: (i, 0))
    return pl.pallas_call(
        body, name=name,
        grid_spec=pltpu.PrefetchScalarGridSpec(
            num_scalar_prefetch=1, grid=(nt,), in_specs=[pl.BlockSpec((tr, c), g_map), plain], out_specs=plain),
        out_shape=jax.ShapeDtypeStruct((r, c), BF16), compiler_params=_cp(("parallel",)))(core_arr, g, theirs)


def _scatter_comm(pair_sums, layouts):
    n = len(pair_sums)

    def copy(ins, outs, send_sems, recv_sems, a, j):
        mx, my, mc = _my_pos()
        dx, dy = CHIP_DELTAS[j]
        px, py = (mx + dx) % 2, (my + dy) % 2
        return pltpu.make_async_remote_copy(
            src_ref=layouts[a].region_in_halves(ins[a], 2 * px + py), dst_ref=outs[a].at[j],
            send_sem=send_sems.at[a, j], recv_sem=recv_sems.at[a, j], device_id=(px, py, mc), device_id_type=MESH)

    pairs = [(a, j) for a in range(n) for j in range(3)]

    def start(*r):
        for a, j in pairs:
            copy(*r, a, j).start()

    def finish(*r):
        for a, j in pairs:
            copy(*r, a, j).wait()

    return _Carried(pair_sums, [jax.ShapeDtypeStruct((3,) + lay.half, BF16) for lay in layouts], {}, (n, 3),
                    start, finish)


def _chip_sum(pair_sum, slots, lay, chip_arr, core_arr, name):
    r, c = lay.half
    tr = _row_tile(r, c * 4)
    nt = r // tr

    def body(chip_ref, core_ref, s_ref, slot_ref, o_ref):
        acc = s_ref[...].astype(F32)
        for j in range(3):
            acc = acc + slot_ref[j].astype(F32)
        o_ref[...] = acc

    if lay.by_cols:
        s_map = lambda i, chip_ref, core_ref: (i, chip_ref[0])
        o_map = lambda i, chip_ref, core_ref: (core_ref[0] * nt + i, 0)
    else:
        s_map = lambda i, chip_ref, core_ref: (chip_ref[0] * nt + i, 0)
        o_map = lambda i, chip_ref, core_ref: (i, core_ref[0])
    return pl.pallas_call(
        body, name=name,
        grid_spec=pltpu.PrefetchScalarGridSpec(
            num_scalar_prefetch=2, grid=(nt,),
            in_specs=[pl.BlockSpec((tr, c), s_map),
                      pl.BlockSpec((3, tr, c), lambda i, chip_ref, core_ref: (0, i, 0))],
            out_specs=pl.BlockSpec((tr, c), o_map)),
        out_shape=jax.ShapeDtypeStruct(lay.shard, F32), compiler_params=_cp(("parallel",)))(
            chip_arr, core_arr, pair_sum, slots)


def _pair_share(bufs, layouts):
    n = len(bufs)

    def body(*refs):
        ins, outs = refs[:n], refs[n:2 * n]
        send_sems, recv_sems = refs[2 * n:]
        mx, my, mc = _my_pos()
        copies = []
        for a, lay in enumerate(layouts):
            cp = pltpu.make_async_remote_copy(
                src_ref=lay.half_of_shard(ins[a], mc), dst_ref=lay.half_of_shard(outs[a], mc),
                send_sem=send_sems.at[a], recv_sem=recv_sems.at[a],
                device_id=(mx, my, 1 - mc), device_id_type=MESH)
            cp.start()
            copies.append(cp)
        for a, lay in enumerate(layouts):
            theirs = lay.half_of_shard(outs[a], 1 - mc)
            pltpu.make_async_remote_copy(
                src_ref=theirs, dst_ref=theirs, send_sem=send_sems.at[a], recv_sem=recv_sems.at[a],
                device_id=(mx, my, 1 - mc), device_id_type=MESH).wait_recv()
        for cp in copies:
            cp.wait_send()

    return pl.pallas_call(
        body, name="grad_pair_share",
        out_shape=[jax.ShapeDtypeStruct(lay.shard, F32) for lay in layouts],
        in_specs=[ANY] * n, out_specs=[ANY] * n, input_output_aliases={a: a for a in range(n)},
        scratch_shapes=[pltpu.SemaphoreType.DMA((n,)), pltpu.SemaphoreType.DMA((n,))],
    )(*bufs)


def _sum_slots(x, name):
    s, r, c = x.shape
    tr = _pick(r, (256, 128, 64, 32, 16, 8))

    def body(x_ref, o_ref):
        acc = x_ref[0]
        for k in range(1, s):
            acc = acc + x_ref[k]
        o_ref[...] = acc

    return pl.pallas_call(body, name=name, grid=(r // tr,),
                          in_specs=[pl.BlockSpec((s, tr, c), lambda i: (0, i, 0))],
                          out_specs=pl.BlockSpec((tr, c), lambda i: (i, 0)),
                          out_shape=jax.ShapeDtypeStruct((r, c), F32), compiler_params=_cp(("parallel",)))(x)


def _adamw(w, g, m, v, name):
    r, c = w.shape
    tr = _pick(r, (256, 128, 64, 32, 16, 8))
    bc1 = 1.0 - ADAM_B1 ** ADAM_STEP
    bc2 = 1.0 - ADAM_B2 ** ADAM_STEP

    def body(w_ref, g_ref, m_ref, v_ref, d_ref, nm_ref, nv_ref):
        gv = g_ref[...]
        nm = ADAM_B1 * m_ref[...] + (1.0 - ADAM_B1) * gv
        nv = ADAM_B2 * v_ref[...] + (1.0 - ADAM_B2) * (gv * gv)
        d_ref[...] = -ADAM_LR * ((nm / bc1) / (jnp.sqrt(nv / bc2) + ADAM_EPS) + ADAM_WD * w_ref[...])
        nm_ref[...] = nm
        nv_ref[...] = nv

    spec = pl.BlockSpec((tr, c), lambda i: (i, 0))
    shp = jax.ShapeDtypeStruct((r, c), F32)
    return pl.pallas_call(body, name=name, grid=(r // tr,), in_specs=[spec] * 4, out_specs=[spec] * 3,
                          out_shape=[shp] * 3, compiler_params=_cp(("parallel",)))(w, g, m, v)


def _cctx_grad(parts, c_ctx2d):
    def body(p_ref, c_ref, o_ref):
        tot = ((p_ref[0] + p_ref[2]) + p_ref[4]) + p_ref[6]
        o_ref[...] = tot * _dsilu(c_ref[...])

    return pl.pallas_call(body, name="cctx_grad", in_specs=[VMEM_SPEC, VMEM_SPEC], out_specs=VMEM_SPEC,
                          out_shape=jax.ShapeDtypeStruct(c_ctx2d.shape, F32))(parts, c_ctx2d)


def _mod_fwd(c_rows_t, ada_w, ada_b_shard):
    nl, d, w = ada_w.shape
    td = _pick(d, (256, 128))
    nd = d // td

    def body(ct_ref, w_ref, b_ref, o_ref):
        i = pl.program_id(1)

        @pl.when(i == 0)
        def _():
            o_ref[0] = jnp.broadcast_to(b_ref[0], (MOD_ROWS, w))

        st = _silu(ct_ref[...])
        wv = w_ref[0]
        rows = [jnp.sum(st[:, r:r + 1] * wv, axis=0, keepdims=True) for r in range(MOD_ROWS)]
        o_ref[0] += jnp.concatenate(rows, axis=0)

    return pl.pallas_call(
        body, name="mod_fwd", grid=(nl, nd),
        in_specs=[pl.BlockSpec((td, MOD_ROWS), lambda l, i: (i, 0)),
                  pl.BlockSpec((1, td, w), lambda l, i: (l, i, 0)),
                  pl.BlockSpec((1, 1, w), lambda l, i: (l, 0, 0))],
        out_specs=pl.BlockSpec((1, MOD_ROWS, w), lambda l, i: (l, 0, 0)),
        out_shape=jax.ShapeDtypeStruct((nl, MOD_ROWS, w), F32),
        compiler_params=_cp(("parallel", "arbitrary")),
    )(c_rows_t, ada_w, ada_b_shard)


def _mod_bwd(c_rows_t, dmod, ada_w):
    nl, d, w = ada_w.shape
    td = _pick(d, (256, 128))
    ctx_row = N_DEV

    def body(ct_ref, dm_ref, w_ref, gw_ref, ds_ref):
        st = _silu(ct_ref[...])
        dm = dm_ref[0]
        acc = st[:, 0:1] * dm[0:1, :]
        for r in range(1, ctx_row + 1):
            acc = acc + st[:, r:r + 1] * dm[r:r + 1, :]
        gw_ref[0] = acc
        ds_ref[0] = jnp.sum(w_ref[0] * dm[ctx_row:ctx_row + 1, :], axis=1, keepdims=True)

    return pl.pallas_call(
        body, name="mod_bwd", grid=(nl, d // td),
        in_specs=[pl.BlockSpec((td, MOD_ROWS), lambda l, i: (i, 0)),
                  pl.BlockSpec((1, MOD_ROWS, w), lambda l, i: (l, 0, 0)),
                  pl.BlockSpec((1, td, w), lambda l, i: (l, i, 0))],
        out_specs=[pl.BlockSpec((1, td, w), lambda l, i: (l, i, 0)),
                   pl.BlockSpec((1, td, 1), lambda l, i: (l, i, 0))],
        out_shape=[jax.ShapeDtypeStruct((nl, d, w), F32), jax.ShapeDtypeStruct((nl, d, 1), F32)],
        compiler_params=_cp(("parallel", "parallel")),
    )(c_rows_t, dmod, ada_w)


def _adaln_fwd(x, g, shift, scale, tr, name):
    r, d = x.shape

    def body(x_ref, g_ref, sh_ref, sc_ref, o_ref):
        xv = x_ref[...]
        rs = lax.rsqrt(jnp.mean(xv * xv, axis=-1, keepdims=True) + EPS)
        o_ref[...] = ((xv * rs * g_ref[...]) * (1.0 + sc_ref[...]) + sh_ref[...]).astype(BF16)

    spec = pl.BlockSpec((tr, d), lambda i: (i, 0))
    return pl.pallas_call(body, name=name, grid=(r // tr,), in_specs=[spec, _vec(d), _vec(d), _vec(d)],
                          out_specs=spec, out_shape=jax.ShapeDtypeStruct((r, d), BF16),
                          compiler_params=_cp(("parallel",)))(x, g, shift, scale)


def _adaln_bwd(xin, dh, row0, g, scale, dg_init, tr, name, dres=None, o_prev=None, gate_prev=None):
    r, d = xin.shape
    assert row0 % tr == 0
    rb0 = row0 // tr
    want_dx = dres is not None
    want_prev = o_prev is not None
    assert want_dx or not want_prev

    def body(*refs):
        it = iter(refs)
        x_ref, dh_ref, g_ref, sc_ref, dgi_ref = next(it), next(it), next(it), next(it), next(it)
        dres_ref = next(it) if want_dx else None
        o_ref, gp_ref = (next(it), next(it)) if want_prev else (None, None)
        dx_ref = next(it) if want_dx else None
        do_ref = next(it) if want_prev else None
        dsh_ref, dsc_ref, dg_ref = next(it), next(it), next(it)
        dgp_ref = next(it) if want_prev else None
        i = pl.program_id(0)

        @pl.when(i == 0)
        def _():
            dsh_ref[...] = jnp.zeros_like(dsh_ref)
            dsc_ref[...] = jnp.zeros_like(dsc_ref)
            dg_ref[...] = dgi_ref[...]
            if want_prev:
                dgp_ref[...] = jnp.zeros_like(dgp_ref)

        xv = x_ref[...]
        dhv = dh_ref[...].astype(F32)
        gv = g_ref[...]
        rs = lax.rsqrt(jnp.mean(xv * xv, axis=-1, keepdims=True) + EPS)
        xn = xv * rs
        dsh_ref[...] += jnp.sum(dhv, axis=0, keepdims=True)
        dsc_ref[...] += jnp.sum(dhv * (xn * gv), axis=0, keepdims=True)
        dr = dhv * (1.0 + sc_ref[...])
        dg_ref[...] += jnp.sum(dr * xn, axis=0, keepdims=True)
        if want_dx:
            gy = dr * gv
            dx = dres_ref[...] + rs * (gy - xn * jnp.mean(gy * xn, axis=-1, keepdims=True))
            dx_ref[...] = dx
            if want_prev:
                do_ref[...] = (gp_ref[...] * dx).astype(BF16)
                dgp_ref[...] += jnp.sum(dx * o_ref[...], axis=0, keepdims=True)

    row = pl.BlockSpec((tr, d), lambda i: (i, 0))
    in_specs = [row, pl.BlockSpec((tr, d), lambda i: (rb0 + i, 0)), _vec(d), _vec(d), _vec(d)]
    args = [xin, dh, g, scale, dg_init]
    out_specs, out_shape, names = [], [], []
    if want_dx:
        in_specs.append(row)
        args.append(dres)
    if want_prev:
        in_specs += [row, _vec(d)]
        args += [o_prev, gate_prev]
    if want_dx:
        out_specs.append(row)
        out_shape.append(jax.ShapeDtypeStruct((r, d), F32))
        names.append("dx")
    if want_prev:
        out_specs.append(row)
        out_shape.append(jax.ShapeDtypeStruct((r, d), BF16))
        names.append("do_prev")
    for nm in ("dshift", "dscale", "dg") + (("dgate_prev",) if want_prev else ()):
        out_specs.append(_vec(d))
        out_shape.append(jax.ShapeDtypeStruct((1, d), F32))
        names.append(nm)
    outs = pl.pallas_call(body, name=name, grid=(r // tr,), in_specs=in_specs, out_specs=out_specs,
                          out_shape=out_shape, compiler_params=_cp(("arbitrary",)))(*args)
    return dict(zip(names, outs))


def _mm(a, b, *, name, tm, tn, tk, ta=False, tb=False, out_dtype=F32, res=None, gate=None, comm=None):
    if ta:
        kd, m = a.shape
    else:
        m, kd = a.shape
    if tb:
        n, kd2 = b.shape
    else:
        kd2, n = b.shape
    assert kd == kd2 and m % tm == 0 and n % tn == 0 and kd % tk == 0, (a.shape, b.shape, tm, tn, tk)
    ni, nj, nk = m // tm, n // tn, kd // tk
    dn = (((0 if ta else 1,), (1 if tb else 0,)), ((), ()))
    with_res = res is not None
    n_in = 4 if with_res else 2
    n_out = 2 if with_res else 1
    n_cin = len(comm.ins) if comm else 0
    n_cout = len(comm.out_shapes) if comm else 0

    def body(*refs):
        in_refs = refs[:n_in + n_cin]
        out_refs = refs[n_in + n_cin:n_in + n_cin + n_out + n_cout]
        scratch = refs[n_in + n_cin + n_out + n_cout:]
        a_ref, b_ref = in_refs[0], in_refs[1]
        o_ref, acc_ref = out_refs[0], scratch[0]
        i, j, k = pl.program_id(0), pl.program_id(1), pl.program_id(2)
        if comm:
            carried = comm.split(in_refs, out_refs, scratch)

            @pl.when(jnp.logical_and(jnp.logical_and(i == 0, j == 0), k == 0))
            def _():
                comm.start(*carried)

        @pl.when(k == 0)
        def _():
            acc_ref[...] = jnp.zeros_like(acc_ref)

        acc_ref[...] += lax.dot_general(a_ref[...], b_ref[...], dn, preferred_element_type=F32)

        @pl.when(k == nk - 1)
        def _():
            acc = acc_ref[...]
            o_ref[...] = acc.astype(o_ref.dtype)
            if with_res:
                out_refs[1][...] = in_refs[2][...] + in_refs[3][...] * acc

        if comm:
            @pl.when(jnp.logical_and(jnp.logical_and(i == ni - 1, j == nj - 1), k == nk - 1))
            def _():
                comm.finish(*carried)

    a_spec = pl.BlockSpec((tk, tm), lambda i, j, k: (k, i)) if ta else pl.BlockSpec((tm, tk), lambda i, j, k: (i, k))
    b_spec = pl.BlockSpec((tn, tk), lambda i, j, k: (j, k)) if tb else pl.BlockSpec((tk, tn), lambda i, j, k: (k, j))
    o_spec = pl.BlockSpec((tm, tn), lambda i, j, k: (i, j))
    in_specs, args = [a_spec, b_spec], [a, b]
    out_specs, out_shape = [o_spec], [jax.ShapeDtypeStruct((m, n), out_dtype)]
    if with_res:
        in_specs += [o_spec, pl.BlockSpec((1, tn), lambda i, j, k: (0, j))]
        args += [res, gate]
        out_specs.append(o_spec)
        out_shape.append(jax.ShapeDtypeStruct((m, n), F32))
    scratch_shapes = [pltpu.VMEM((tm, tn), F32)]
    aliases = {}
    sem = ("parallel", "parallel", "arbitrary")
    if comm:
        in_specs += [ANY] * n_cin
        args += comm.ins
        out_specs += [ANY] * n_cout
        out_shape += comm.out_shapes
        scratch_shapes += comm.scratch()
        aliases = {n_in + s: n_out + d for s, d in comm.aliases.items()}
        sem = ("arbitrary", "arbitrary", "arbitrary")
    outs = pl.pallas_call(body, name=name, grid=(ni, nj, nk), in_specs=in_specs, out_specs=out_specs,
                          out_shape=out_shape, scratch_shapes=scratch_shapes, input_output_aliases=aliases,
                          compiler_params=_cp(sem))(*args)
    return outs if (with_res or comm) else outs[0]


def _swap_pairs(x):
    lane = lax.broadcasted_iota(jnp.int32, x.shape, 1)
    return jnp.where(lane % 2 == 0, pltpu.roll(x, HEAD_DIM - 1, 1), pltpu.roll(x, 1, 1))


def _qk_prep(p0, cos_i, sin_s, q_norm, k_norm, dims, tr):
    rows = p0.shape[0]
    kvw, aw, cb = dims["kv_w"], dims["attn_w"], dims["cb"]
    nkv, nq = kvw // HEAD_DIM, aw // HEAD_DIM
    scale = HEAD_DIM ** -0.5
    n_kv_specs, n_q_specs = (2 * kvw) // cb, aw // cb

    def body(*refs):
        kv_refs = refs[:n_kv_specs]
        q_refs = refs[n_kv_specs:n_kv_specs + n_q_specs]
        cos_ref, sin_ref, qn_ref, kn_ref, qo_ref, ko_ref, vo_ref = refs[n_kv_specs + n_q_specs:]
        kv = _cat(kv_refs)
        qv = _cat(q_refs)
        cs, sn = cos_ref[...], sin_ref[...]

        def norm_rope(xh, gvec):
            rs = lax.rsqrt(jnp.mean(xh * xh, axis=-1, keepdims=True) + EPS)
            xn = xh * rs * gvec
            return xn * cs + _swap_pairs(xn) * sn

        for h in range(nkv):
            sl = slice(h * HEAD_DIM, (h + 1) * HEAD_DIM)
            ko_ref[:, sl] = norm_rope(kv[:, sl], kn_ref[...]).astype(BF16)
        vo_ref[...] = kv[:, kvw:].astype(BF16)
        for h in range(nq):
            sl = slice(h * HEAD_DIM, (h + 1) * HEAD_DIM)
            qo_ref[:, sl] = (norm_rope(qv[:, sl], qn_ref[...]) * scale).astype(BF16)

    rm = lambda i: i
    in_specs = (_col_specs(tr, 0, 2 * kvw, cb, rm) + _col_specs(tr, 2 * kvw, aw, cb, rm)
                + [pl.BlockSpec((tr, HEAD_DIM), lambda i: (i, 0))] * 2 + [_vec(HEAD_DIM)] * 2)
    args = [p0] * (n_kv_specs + n_q_specs) + [cos_i, sin_s, q_norm, k_norm]
    return pl.pallas_call(
        body, name="qk_prep", grid=(rows // tr,), in_specs=in_specs,
        out_specs=[pl.BlockSpec((tr, aw), lambda i: (i, 0)), pl.BlockSpec((tr, kvw), lambda i: (i, 0)),
                   pl.BlockSpec((tr, kvw), lambda i: (i, 0))],
        out_shape=[jax.ShapeDtypeStruct((rows, aw), BF16), jax.ShapeDtypeStruct((rows, kvw), BF16),
                   jax.ShapeDtypeStruct((rows, kvw), BF16)],
        compiler_params=_cp(("parallel",)))(*args)


def _qk_prep_bwd(dq_hat, dk_hat, dv, p0, cos_i, sin_s, q_norm, k_norm, dims, tr, n_lat):
    rows = p0.shape[0]
    kvw, aw, cb = dims["kv_w"], dims["attn_w"], dims["cb"]
    nkv, nq = kvw // HEAD_DIM, aw // HEAD_DIM
    scale = HEAD_DIM ** -0.5
    n_kv_specs, n_q_specs = (2 * kvw) // cb, aw // cb
    lat_tiles = n_lat // tr

    def body(*refs):
        kv_refs = refs[:n_kv_specs]
        q_refs = refs[n_kv_specs:n_kv_specs + n_q_specs]
        (dq_ref, dk_ref, dv_ref, cos_ref, sin_ref, qn_ref, kn_ref,
         out_ref, dqn_ref, dkn_ref) = refs[n_kv_specs + n_q_specs:]
        i = pl.program_id(0)

        @pl.when(i == 0)
        def _():
            dqn_ref[...] = jnp.zeros_like(dqn_ref)
            dkn_ref[...] = jnp.zeros_like(dkn_ref)

        kv = _cat(kv_refs)
        qv = _cat(q_refs)
        cs, sn = cos_ref[...], sin_ref[...]
        is_lat = (i < lat_tiles).astype(F32)

        def head_bwd(xh, dhat, gvec):
            dn = dhat * cs + _swap_pairs(dhat * sn)
            rs = lax.rsqrt(jnp.mean(xh * xh, axis=-1, keepdims=True) + EPS)
            xn = xh * rs
            gy = dn * gvec
            dx = rs * (gy - xn * jnp.mean(gy * xn, axis=-1, keepdims=True))
            return dx, jnp.sum(dn * xn, axis=0, keepdims=True)

        dkn = jnp.zeros((1, HEAD_DIM), F32)
        for h in range(nkv):
            sl = slice(h * HEAD_DIM, (h + 1) * HEAD_DIM)
            dx, dgv = head_bwd(kv[:, sl], dk_ref[:, sl], kn_ref[...])
            out_ref[:, sl] = dx.astype(BF16)
            dkn = dkn + dgv
        dkn_ref[...] += dkn
        out_ref[:, kvw:2 * kvw] = dv_ref[...].astype(BF16)
        dqn = jnp.zeros((1, HEAD_DIM), F32)
        for h in range(nq):
            sl = slice(h * HEAD_DIM, (h + 1) * HEAD_DIM)
            dx, dgv = head_bwd(qv[:, sl], dq_ref[:, sl] * (scale * is_lat), qn_ref[...])
            out_ref[:, 2 * kvw + h * HEAD_DIM:2 * kvw + (h + 1) * HEAD_DIM] = dx.astype(BF16)
            dqn = dqn + dgv
        dqn_ref[...] += dqn

    rm = lambda i: i
    wout = 2 * kvw + aw
    in_specs = (_col_specs(tr, 0, 2 * kvw, cb, rm) + _col_specs(tr, 2 * kvw, aw, cb, rm)
                + [pl.BlockSpec((tr, aw), lambda i: (jnp.minimum(i, lat_tiles - 1), 0)),
                   pl.BlockSpec((tr, kvw), lambda i: (i, 0)), pl.BlockSpec((tr, kvw), lambda i: (i, 0)),
                   pl.BlockSpec((tr, HEAD_DIM), lambda i: (i, 0)), pl.BlockSpec((tr, HEAD_DIM), lambda i: (i, 0)),
                   _vec(HEAD_DIM), _vec(HEAD_DIM)])
    args = [p0] * (n_kv_specs + n_q_specs) + [dq_hat, dk_hat, dv, cos_i, sin_s, q_norm, k_norm]
    return pl.pallas_call(
        body, name="qk_prep_bwd", grid=(rows // tr,), in_specs=in_specs,
        out_specs=[pl.BlockSpec((tr, wout), lambda i: (i, 0)), _vec(HEAD_DIM), _vec(HEAD_DIM)],
        out_shape=[jax.ShapeDtypeStruct((rows, wout), BF16), jax.ShapeDtypeStruct((1, HEAD_DIM), F32),
                   jax.ShapeDtypeStruct((1, HEAD_DIM), F32)],
        compiler_params=_cp(("arbitrary",)))(*args)


def _stack_heads(x, g):
    return jnp.concatenate([x[:, h * HEAD_DIM:(h + 1) * HEAD_DIM] for h in range(g)], axis=0)


def _flash_fwd(q_hat, k_all, v_all, n_lat, dims, tq, tk):
    kvw, aw = dims["kv_w"], dims["attn_w"]
    nkv = kvw // HEAD_DIM
    g = aw // kvw
    gw = g * HEAD_DIM
    n_keys = k_all.shape[0]
    ni, nj = n_lat // tq, n_keys // tk
    dn_nt = (((1,), (1,)), ((), ()))

    def body(q_ref, k_ref, v_ref, o_ref, lse_ref):
        qs = _stack_heads(q_ref[...], g)
        m = jnp.full((g * tq, 1), -1e30, F32)
        l = jnp.zeros((g * tq, 1), F32)
        acc = jnp.zeros((g * tq, HEAD_DIM), F32)
        for j in range(nj):
            kb = k_ref[pl.ds(j * tk, tk), :]
            vb = v_ref[pl.ds(j * tk, tk), :]
            s = lax.dot_general(qs, kb, dn_nt, preferred_element_type=F32)
            m_new = jnp.maximum(m, jnp.max(s, axis=-1, keepdims=True))
            alpha = jnp.exp(m - m_new)
            p = jnp.exp(s - m_new)
            l = alpha * l + jnp.sum(p, axis=-1, keepdims=True)
            acc = alpha * acc + jnp.dot(p.astype(BF16), vb, preferred_element_type=F32)
            m = m_new
        o = acc / l
        for h in range(g):
            o_ref[:, h * HEAD_DIM:(h + 1) * HEAD_DIM] = o[h * tq:(h + 1) * tq]
        lse_ref[...] = m + jnp.log(l)

    return pl.pallas_call(
        body, name="flash_fwd", grid=(nkv, ni),
        in_specs=[pl.BlockSpec((tq, gw), lambda h, i: (i, h)),
                  pl.BlockSpec((n_keys, HEAD_DIM), lambda h, i: (0, h)),
                  pl.BlockSpec((n_keys, HEAD_DIM), lambda h, i: (0, h))],
        out_specs=[pl.BlockSpec((tq, gw), lambda h, i: (i, h)),
                   pl.BlockSpec((g * tq, 1), lambda h, i: (h * ni + i, 0))],
        out_shape=[jax.ShapeDtypeStruct((n_lat, aw), F32), jax.ShapeDtypeStruct((nkv * ni * g * tq, 1), F32)],
        compiler_params=_cp(("parallel", "parallel")))(q_hat, k_all, v_all)


def _flash_bwd(q_hat, k_all, v_all, do, o, lse, n_lat, dims, tq, tk):
    kvw, aw = dims["kv_w"], dims["attn_w"]
    nkv = kvw // HEAD_DIM
    g = aw // kvw
    gw = g * HEAD_DIM
    n_keys = k_all.shape[0]
    ni, nj = n_lat // tq, n_keys // tk
    dn_nt = (((1,), (1,)), ((), ()))
    dn_tn = (((0,), (0,)), ((), ()))

    def body(q_ref, k_ref, v_ref, do_ref, o_ref, lse_ref, dq_ref, dk_ref, dv_ref):
        i = pl.program_id(1)

        @pl.when(i == 0)
        def _():
            dk_ref[...] = jnp.zeros_like(dk_ref)
            dv_ref[...] = jnp.zeros_like(dv_ref)

        dos = _stack_heads(do_ref[...], g)
        qs = _stack_heads(q_ref[...], g)
        delta = jnp.sum(dos.astype(F32) * _stack_heads(o_ref[...], g), axis=-1, keepdims=True)
        lse_v = lse_ref[...]
        dq = jnp.zeros((g * tq, HEAD_DIM), F32)
        for j in range(nj):
            rows = pl.ds(j * tk, tk)
            kb, vb = k_ref[rows, :], v_ref[rows, :]
            s = lax.dot_general(qs, kb, dn_nt, preferred_element_type=F32)
            p = jnp.exp(s - lse_v)
            dp = lax.dot_general(dos, vb, dn_nt, preferred_element_type=F32)
            ds = (p * (dp - delta)).astype(BF16)
            dv_ref[rows, :] += lax.dot_general(p.astype(BF16), dos, dn_tn, preferred_element_type=F32)
            dk_ref[rows, :] += lax.dot_general(ds, qs, dn_tn, preferred_element_type=F32)
            dq = dq + jnp.dot(ds, kb, preferred_element_type=F32)
        for h in range(g):
            dq_ref[:, h * HEAD_DIM:(h + 1) * HEAD_DIM] = dq[h * tq:(h + 1) * tq]

    qspec = pl.BlockSpec((tq, gw), lambda h, i: (i, h))
    full_k = pl.BlockSpec((n_keys, HEAD_DIM), lambda h, i: (0, h))
    return pl.pallas_call(
        body, name="flash_bwd", grid=(nkv, ni),
        in_specs=[qspec, full_k, full_k, qspec, qspec, pl.BlockSpec((g * tq, 1), lambda h, i: (h * ni + i, 0))],
        out_specs=[qspec, full_k, full_k],
        out_shape=[jax.ShapeDtypeStruct((n_lat, aw), F32), jax.ShapeDtypeStruct((n_keys, kvw), F32),
                   jax.ShapeDtypeStruct((n_keys, kvw), F32)],
        compiler_params=_cp(("parallel", "arbitrary")))(q_hat, k_all, v_all, do, o, lse)


def _shifted_copies(pad_ref, sh_ref, tr):
    rows = tr + 2 * CONV_HALO - SUBLANES
    for r in range(SUBLANES):
        sh_ref[r] = pad_ref[pl.ds(r, rows), :]


def _tap(sh_ref, offset, tr):
    return sh_ref[offset % SUBLANES, pl.ds(SUBLANES * (offset // SUBLANES), tr), :]


def _halo_maps(tr, n_tiles):
    per = tr // CONV_HALO
    prev = lambda i: jnp.maximum(i * per - 1, 0)
    nxt = lambda i: (i + 1) * per
    return prev, nxt


def _mix_fwd(attn, p0, dw_w, dw_b, ln_g, ln_b, dims, tr):
    n_lat, aw = attn.shape
    cc, cb, cw = dims["conv_ch"], dims["cb"], dims["conv_w"]
    off = dims["off"]
    n_tiles = n_lat // tr
    pad = cw // 2
    na, nc = aw // cb, cc // cb
    prev_map, next_map = _halo_maps(tr, n_tiles)

    def body(*refs):
        it = iter(refs)
        attn_ref = next(it)
        za = [next(it) for _ in range(na)]
        a_c = [next(it) for _ in range(nc)]
        b_c = [next(it) for _ in range(nc)]
        zb = [next(it) for _ in range(nc)]
        a_p = [next(it) for _ in range(nc)]
        b_p = [next(it) for _ in range(nc)]
        a_n = [next(it) for _ in range(nc)]
        b_n = [next(it) for _ in range(nc)]
        w_ref, db_ref, g_ref, bb_ref, mix_ref, yc_ref, ypad, ysh = (next(it) for _ in range(8))
        i = pl.program_id(0)
        ypad[pl.ds(0, CONV_HALO), :] = _cat(a_p) * _sigmoid(_cat(b_p)) * (i > 0).astype(F32)
        ypad[pl.ds(CONV_HALO, tr), :] = _cat(a_c) * _sigmoid(_cat(b_c))
        ypad[pl.ds(CONV_HALO + tr, CONV_HALO), :] = _cat(a_n) * _sigmoid(_cat(b_n)) * (i < n_tiles - 1).astype(F32)
        _shifted_copies(ypad, ysh, tr)
        acc = jnp.broadcast_to(db_ref[...], (tr, cc))
        for k in range(cw):
            acc = acc + _tap(ysh, CONV_HALO - pad + k, tr) * w_ref[pl.ds(k, 1), :]
        yc_ref[...] = acc
        mu = jnp.mean(acc, axis=-1, keepdims=True)
        xc = acc - mu
        rs = lax.rsqrt(jnp.mean(xc * xc, axis=-1, keepdims=True) + EPS)
        nv = xc * rs * g_ref[...] + bb_ref[...]
        mix_ref[:, :aw] = (attn_ref[...] * _silu(_cat(za))).astype(BF16)
        mix_ref[:, aw:] = (_silu(nv) * _silu(_cat(zb))).astype(BF16)

    rm = lambda i: i
    in_specs = ([pl.BlockSpec((tr, aw), lambda i: (i, 0))]
                + _col_specs(tr, off["za"], aw, cb, rm) + _col_specs(tr, off["a"], cc, cb, rm)
                + _col_specs(tr, off["b"], cc, cb, rm) + _col_specs(tr, off["zb"], cc, cb, rm)
                + _col_specs(CONV_HALO, off["a"], cc, cb, prev_map) + _col_specs(CONV_HALO, off["b"], cc, cb, prev_map)
                + _col_specs(CONV_HALO, off["a"], cc, cb, next_map) + _col_specs(CONV_HALO, off["b"], cc, cb, next_map)
                + [pl.BlockSpec(dw_w.shape, lambda i: (0, 0)), _vec(cc), _vec(cc), _vec(cc)])
    args = [attn] + [p0] * (na + 7 * nc) + [dw_w, dw_b, ln_g, ln_b]
    return pl.pallas_call(
        body, name="mix_fwd", grid=(n_tiles,), in_specs=in_specs,
        out_specs=[pl.BlockSpec((tr, aw + cc), lambda i: (i, 0)), pl.BlockSpec((tr, cc), lambda i: (i, 0))],
        out_shape=[jax.ShapeDtypeStruct((n_lat, aw + cc), BF16), jax.ShapeDtypeStruct((n_lat, cc), F32)],
        scratch_shapes=[pltpu.VMEM((tr + 2 * CONV_HALO, cc), F32),
                        pltpu.VMEM((SUBLANES, tr + 2 * CONV_HALO - SUBLANES, cc), F32)],
        compiler_params=_cp(("parallel",)))(*args)


def _mix_bwd_pointwise(dmix, attn, p0, yc, ln_g, ln_b, dims, tr):
    n_lat, aw = attn.shape
    cc, cb, off = dims["conv_ch"], dims["cb"], dims["off"]
    na, nc = aw // cb, cc // cb

    def body(*refs):
        it = iter(refs)
        dmix_ref, attn_ref = next(it), next(it)
        za = [next(it) for _ in range(na)]
        zb = [next(it) for _ in range(nc)]
        yc_ref, g_ref, bb_ref = next(it), next(it), next(it)
        dattn_ref, dza_ref, dzb_ref, dyc_ref, dg_ref, dbb_ref, ddb_ref = (next(it) for _ in range(7))
        i = pl.program_id(0)

        @pl.when(i == 0)
        def _():
            dg_ref[...] = jnp.zeros_like(dg_ref)
            dbb_ref[...] = jnp.zeros_like(dbb_ref)
            ddb_ref[...] = jnp.zeros_like(ddb_ref)

        dm = dmix_ref[...]
        dma, dmb = dm[:, :aw], dm[:, aw:]
        zav, zbv = _cat(za), _cat(zb)
        dattn_ref[...] = (dma * _silu(zav)).astype(BF16)
        dza_ref[...] = (dma * attn_ref[...] * _dsilu(zav)).astype(BF16)
        ycv = yc_ref[...]
        mu = jnp.mean(ycv, axis=-1, keepdims=True)
        xc = ycv - mu
        rs = lax.rsqrt(jnp.mean(xc * xc, axis=-1, keepdims=True) + EPS)
        xh = xc * rs
        nv = xh * g_ref[...] + bb_ref[...]
        dzb_ref[...] = (dmb * _silu(nv) * _dsilu(zbv)).astype(BF16)
        dn = dmb * _silu(zbv) * _dsilu(nv)
        dg_ref[...] += jnp.sum(dn * xh, axis=0, keepdims=True)
        dbb_ref[...] += jnp.sum(dn, axis=0, keepdims=True)
        dxh = dn * g_ref[...]
        dyc = rs * (dxh - jnp.mean(dxh, axis=-1, keepdims=True) - xh * jnp.mean(dxh * xh, axis=-1, keepdims=True))
        dyc_ref[...] = dyc
        ddb_ref[...] += jnp.sum(dyc, axis=0, keepdims=True)

    rm = lambda i: i
    row = lambda w: pl.BlockSpec((tr, w), lambda i: (i, 0))
    in_specs = ([row(aw + cc), row(aw)] + _col_specs(tr, off["za"], aw, cb, rm)
                + _col_specs(tr, off["zb"], cc, cb, rm) + [row(cc), _vec(cc), _vec(cc)])
    args = [dmix, attn] + [p0] * (na + nc) + [yc, ln_g, ln_b]
    return pl.pallas_call(
        body, name="mix_bwd_pointwise", grid=(n_lat // tr,), in_specs=in_specs,
        out_specs=[row(aw), row(aw), row(cc), row(cc), _vec(cc), _vec(cc), _vec(cc)],
        out_shape=[jax.ShapeDtypeStruct((n_lat, aw), BF16), jax.ShapeDtypeStruct((n_lat, aw), BF16),
                   jax.ShapeDtypeStruct((n_lat, cc), BF16), jax.ShapeDtypeStruct((n_lat, cc), F32)]
                  + [jax.ShapeDtypeStruct((1, cc), F32)] * 3,
        compiler_params=_cp(("arbitrary",)))(*args)


def _conv_bwd(dyc, p0, dw_w, dims, tr):
    n_lat, cc = dyc.shape
    cb, cw, off = dims["cb"], dims["conv_w"], dims["off"]
    nc = cc // cb
    n_tiles = n_lat // tr
    pad = cw // 2
    prev_map, next_map = _halo_maps(tr, n_tiles)

    def body(*refs):
        it = iter(refs)
        a_c = [next(it) for _ in range(nc)]
        b_c = [next(it) for _ in range(nc)]
        a_p = [next(it) for _ in range(nc)]
        b_p = [next(it) for _ in range(nc)]
        a_n = [next(it) for _ in range(nc)]
        b_n = [next(it) for _ in range(nc)]
        d_c, d_p, d_n, w_ref, dab_ref, dw_ref, ypad, dpad, ysh, dsh = (next(it) for _ in range(10))
        i = pl.program_id(0)

        @pl.when(i == 0)
        def _():
            dw_ref[...] = jnp.zeros_like(dw_ref)

        first, last = (i > 0).astype(F32), (i < n_tiles - 1).astype(F32)
        av, bv = _cat(a_c), _cat(b_c)
        sg = _sigmoid(bv)
        ypad[pl.ds(0, CONV_HALO), :] = _cat(a_p) * _sigmoid(_cat(b_p)) * first
        ypad[pl.ds(CONV_HALO, tr), :] = av * sg
        ypad[pl.ds(CONV_HALO + tr, CONV_HALO), :] = _cat(a_n) * _sigmoid(_cat(b_n)) * last
        dcur = d_c[...]
        dpad[pl.ds(0, CONV_HALO), :] = d_p[...] * first
        dpad[pl.ds(CONV_HALO, tr), :] = dcur
        dpad[pl.ds(CONV_HALO + tr, CONV_HALO), :] = d_n[...] * last
        _shifted_copies(ypad, ysh, tr)
        _shifted_copies(dpad, dsh, tr)
        dy = jnp.zeros((tr, cc), F32)
        for k in range(cw):
            dy = dy + _tap(dsh, CONV_HALO + pad - k, tr) * w_ref[pl.ds(k, 1), :]
            dw_ref[pl.ds(k, 1), :] += jnp.sum(dcur * _tap(ysh, CONV_HALO - pad + k, tr), axis=0, keepdims=True)
        dab_ref[:, :cc] = (dy * sg).astype(BF16)
        dab_ref[:, cc:] = (dy * av * sg * (1.0 - sg)).astype(BF16)

    rm = lambda i: i
    in_specs = (_col_specs(tr, off["a"], cc, cb, rm) + _col_specs(tr, off["b"], cc, cb, rm)
                + _col_specs(CONV_HALO, off["a"], cc, cb, prev_map) + _col_specs(CONV_HALO, off["b"], cc, cb, prev_map)
                + _col_specs(CONV_HALO, off["a"], cc, cb, next_map) + _col_specs(CONV_HALO, off["b"], cc, cb, next_map)
                + [pl.BlockSpec((tr, cc), lambda i: (i, 0)),
                   pl.BlockSpec((CONV_HALO, cc), lambda i: (prev_map(i), 0)),
                   pl.BlockSpec((CONV_HALO, cc), lambda i: (jnp.minimum(next_map(i), n_lat // CONV_HALO - 1), 0)),
                   pl.BlockSpec(dw_w.shape, lambda i: (0, 0))])
    args = [p0] * (6 * nc) + [dyc, dyc, dyc, dw_w]
    return pl.pallas_call(
        body, name="conv_bwd", grid=(n_tiles,), in_specs=in_specs,
        out_specs=[pl.BlockSpec((tr, 2 * cc), lambda i: (i, 0)), pl.BlockSpec(dw_w.shape, lambda i: (0, 0))],
        out_shape=[jax.ShapeDtypeStruct((n_lat, 2 * cc), BF16), jax.ShapeDtypeStruct(dw_w.shape, F32)],
        scratch_shapes=[pltpu.VMEM((tr + 2 * CONV_HALO, cc), F32), pltpu.VMEM((tr + 2 * CONV_HALO, cc), F32),
                        pltpu.VMEM((SUBLANES, tr + 2 * CONV_HALO - SUBLANES, cc), F32),
                        pltpu.VMEM((SUBLANES, tr + 2 * CONV_HALO - SUBLANES, cc), F32)],
        compiler_params=_cp(("arbitrary",)))(*args)


def _sgu_parts(pu, pv, ln_g, ln_b):
    u = _gelu(pu)
    v = _gelu(pv)
    mu = jnp.mean(v, axis=-1, keepdims=True)
    xc = v - mu
    rs = lax.rsqrt(jnp.mean(xc * xc, axis=-1, keepdims=True) + EPS)
    xh = xc * rs
    return u, xh, rs, xh * ln_g + ln_b


def _sgu_fwd(p1, ln_g, ln_b, ws, bs_t, tr):
    n_lat, w3 = p1.shape
    w = w3 // 3
    ng, ch = ws.shape[0], ws.shape[1]
    gwid = w // ng
    n_ch = tr // ch

    def body(pu_ref, pv_ref, pg_ref, g_ref, b_ref, ws_ref, bs_ref, o_ref):
        u, _, _, vln = _sgu_parts(pu_ref[...], pv_ref[...], g_ref[...], b_ref[...])
        gate = _silu(pg_ref[...])
        vb = vln.astype(BF16)
        for c in range(n_ch):
            rs_ = slice(c * ch, (c + 1) * ch)
            for gi in range(ng):
                cs_ = slice(gi * gwid, (gi + 1) * gwid)
                mixed = jnp.dot(ws_ref[gi], vb[rs_, cs_], preferred_element_type=F32) + bs_ref[:, gi:gi + 1]
                o_ref[rs_, cs_] = (u[rs_, cs_] * mixed * gate[rs_, cs_]).astype(BF16)

    col = lambda t: pl.BlockSpec((tr, w), lambda i, _t=t: (i, _t))
    return pl.pallas_call(
        body, name="sgu_fwd", grid=(n_lat // tr,),
        in_specs=[col(0), col(1), col(2), _vec(w), _vec(w),
                  pl.BlockSpec(ws.shape, lambda i: (0, 0, 0)), pl.BlockSpec(bs_t.shape, lambda i: (0, 0))],
        out_specs=pl.BlockSpec((tr, w), lambda i: (i, 0)),
        out_shape=jax.ShapeDtypeStruct((n_lat, w), BF16),
        compiler_params=_cp(("parallel",)))(p1, p1, p1, ln_g, ln_b, ws, bs_t)


def _sgu_bwd(dm, p1, ln_g, ln_b, ws, ws_t, bs_t, tr):
    n_lat, w3 = p1.shape
    w = w3 // 3
    ng, ch = ws.shape[0], ws.shape[1]
    gwid = w // ng
    n_ch = tr // ch
    dn_nt = (((1,), (1,)), ((), ()))

    def body(dm_ref, pu_ref, pv_ref, pg_ref, g_ref, b_ref, ws_ref, wst_ref, bs_ref,
             dp_ref, dws_ref, dbs_ref, dg_ref, dbb_ref, dvln_scr):
        i = pl.program_id(0)

        @pl.when(i == 0)
        def _():
            dws_ref[...] = jnp.zeros_like(dws_ref)
            dbs_ref[...] = jnp.zeros_like(dbs_ref)
            dg_ref[...] = jnp.zeros_like(dg_ref)
            dbb_ref[...] = jnp.zeros_like(dbb_ref)

        puv, pvv, pgv = pu_ref[...], pv_ref[...], pg_ref[...]
        u, xh, rs, vln = _sgu_parts(puv, pvv, g_ref[...], b_ref[...])
        gate = _silu(pgv)
        dmv = dm_ref[...]
        vb = vln.astype(BF16)
        dmu = dmv * u
        dbs_cols = [jnp.zeros((ch, 1), F32) for _ in range(ng)]
        for c in range(n_ch):
            rs_ = slice(c * ch, (c + 1) * ch)
            for gi in range(ng):
                cs_ = slice(gi * gwid, (gi + 1) * gwid)
                mixed = jnp.dot(ws_ref[gi], vb[rs_, cs_], preferred_element_type=F32) + bs_ref[:, gi:gi + 1]
                dmixed = dmu[rs_, cs_] * gate[rs_, cs_]
                dmb = dmixed.astype(BF16)
                dp_ref[rs_, gi * gwid:(gi + 1) * gwid] = (
                    dmv[rs_, cs_] * mixed * gate[rs_, cs_] * _dgelu(puv[rs_, cs_])).astype(BF16)
                dp_ref[rs_, 2 * w + gi * gwid:2 * w + (gi + 1) * gwid] = (
                    dmu[rs_, cs_] * mixed * _dsilu(pgv[rs_, cs_])).astype(BF16)
                dvln_scr[rs_, cs_] = jnp.dot(wst_ref[gi], dmb, preferred_element_type=F32)
                dws_ref[gi] += lax.dot_general(dmb, vb[rs_, cs_], dn_nt, preferred_element_type=F32)
                dbs_cols[gi] = dbs_cols[gi] + jnp.sum(dmixed, axis=-1, keepdims=True)
        dbs_ref[...] += jnp.concatenate(dbs_cols, axis=1)
        dvln = dvln_scr[...]
        dg_ref[...] += jnp.sum(dvln * xh, axis=0, keepdims=True)
        dbb_ref[...] += jnp.sum(dvln, axis=0, keepdims=True)
        dxh = dvln * g_ref[...]
        dv = rs * (dxh - jnp.mean(dxh, axis=-1, keepdims=True) - xh * jnp.mean(dxh * xh, axis=-1, keepdims=True))
        dp_ref[:, w:2 * w] = (dv * _dgelu(pvv)).astype(BF16)

    col = lambda t: pl.BlockSpec((tr, w), lambda i, _t=t: (i, _t))
    return pl.pallas_call(
        body, name="sgu_bwd", grid=(n_lat // tr,),
        in_specs=[pl.BlockSpec((tr, w), lambda i: (i, 0)), col(0), col(1), col(2), _vec(w), _vec(w),
                  pl.BlockSpec(ws.shape, lambda i: (0, 0, 0)), pl.BlockSpec(ws.shape, lambda i: (0, 0, 0)),
                  pl.BlockSpec(bs_t.shape, lambda i: (0, 0))],
        out_specs=[pl.BlockSpec((tr, w3), lambda i: (i, 0)), pl.BlockSpec(ws.shape, lambda i: (0, 0, 0)),
                   pl.BlockSpec(bs_t.shape, lambda i: (0, 0)), _vec(w), _vec(w)],
        out_shape=[jax.ShapeDtypeStruct((n_lat, w3), BF16), jax.ShapeDtypeStruct(ws.shape, F32),
                   jax.ShapeDtypeStruct(bs_t.shape, F32), jax.ShapeDtypeStruct((1, w), F32),
                   jax.ShapeDtypeStruct((1, w), F32)],
        scratch_shapes=[pltpu.VMEM((tr, w), F32)],
        compiler_params=_cp(("arbitrary",)))(dm, p1, p1, p1, ln_g, ln_b, ws, ws_t, bs_t)


def _final_loss(x2, target, final_g, o_prev, gate_prev, tr):
    n_lat, d = x2.shape

    def body(x_ref, t_ref, g_ref, o_ref, gp_ref, dx_ref, do_ref, ls_ref, dg_ref, dgp_ref):
        i = pl.program_id(0)

        @pl.when(i == 0)
        def _():
            ls_ref[...] = jnp.zeros_like(ls_ref)
            dg_ref[...] = jnp.zeros_like(dg_ref)
            dgp_ref[...] = jnp.zeros_like(dgp_ref)

        xv = x_ref[...]
        gv = g_ref[...]
        rs = lax.rsqrt(jnp.mean(xv * xv, axis=-1, keepdims=True) + EPS)
        xn = xv * rs
        err = xn * gv - t_ref[...]
        ls_ref[...] += jnp.sum(err * err, axis=0, keepdims=True)
        dy = err * (1.0 / d)
        dg_ref[...] += jnp.sum(dy * xn, axis=0, keepdims=True)
        gy = dy * gv
        dx = rs * (gy - xn * jnp.mean(gy * xn, axis=-1, keepdims=True))
        dx_ref[...] = dx
        do_ref[...] = (gp_ref[...] * dx).astype(BF16)
        dgp_ref[...] += jnp.sum(dx * o_ref[...], axis=0, keepdims=True)

    row = pl.BlockSpec((tr, d), lambda i: (i, 0))
    return pl.pallas_call(
        body, name="final_loss", grid=(n_lat // tr,), in_specs=[row, row, _vec(d), row, _vec(d)],
        out_specs=[row, row, _vec(d), _vec(d), _vec(d)],
        out_shape=[jax.ShapeDtypeStruct((n_lat, d), F32), jax.ShapeDtypeStruct((n_lat, d), BF16)]
                  + [jax.ShapeDtypeStruct((1, d), F32)] * 3,
        compiler_params=_cp(("arbitrary",)))(x2, target, final_g, o_prev, gate_prev)


def _pack(arrays):
    flat = jnp.concatenate([a.reshape(-1).astype(F32) for a in arrays])
    n = flat.shape[0]
    rows = -(-n // LANES)
    rows = -(-rows // PACK_ROWS) * PACK_ROWS
    return jnp.pad(flat, (0, rows * LANES - n)).reshape(rows, LANES)


def _unpack(buf, shapes):
    flat = buf.reshape(buf.shape[:-2] + (-1,))
    out, pos = [], 0
    for shp in shapes:
        n = math.prod(shp)
        out.append(flat[..., pos:pos + n].reshape(buf.shape[:-2] + tuple(shp)))
        pos += n
    return out


def _rope_tables(n_lat, n_ctx):
    rows = n_lat // GRID_W
    row = jnp.repeat(jnp.arange(rows, dtype=F32), GRID_W)
    col = jnp.tile(jnp.arange(GRID_W, dtype=F32), rows)
    n_freq, axis_dim = HEAD_DIM // 4, HEAD_DIM // 2
    inv = jnp.power(ROPE_THETA, jnp.arange(n_freq, dtype=F32) * (-2.0 / axis_dim))
    ang = jnp.concatenate([row[:, None] * inv, col[:, None] * inv], axis=-1)
    cos, sin = jnp.cos(ang), jnp.sin(ang)
    cos_i = jnp.repeat(cos, 2, axis=-1)
    sin_s = jnp.stack([-sin, sin], axis=-1).reshape(n_lat, HEAD_DIM)
    cos_i = jnp.concatenate([cos_i, jnp.ones((n_ctx, HEAD_DIM), F32)], axis=0)
    sin_s = jnp.concatenate([sin_s, jnp.zeros((n_ctx, HEAD_DIM), F32)], axis=0)
    return cos_i, sin_s


def kernel(x, c, ctx, c_ctx, ada_w, ada_b, norm_g, ev_w_in, ev_q_norm, ev_k_norm, ev_dw_w, ev_dw_b, ev_ln_g, ev_ln_b, ev_w_out, od_w_in, od_ln_g, od_ln_b, od_ws, od_bs, od_w_out, final_g, loss_target, m_c_ctx, m_ada_w, m_ada_b, m_norm_g, m_ev_w_in, m_ev_q_norm, m_ev_k_norm, m_ev_dw_w, m_ev_dw_b, m_ev_ln_g, m_ev_ln_b, m_ev_w_out, m_od_w_in, m_od_ln_g, m_od_ln_b, m_od_ws, m_od_bs, m_od_w_out, m_final_g, v_c_ctx, v_ada_w, v_ada_b, v_norm_g, v_ev_w_in, v_ev_q_norm, v_ev_k_norm, v_ev_dw_w, v_ev_dw_b, v_ev_ln_g, v_ev_ln_b, v_ev_w_out, v_od_w_in, v_od_ln_g, v_od_ln_b, v_od_ws, v_od_bs, v_od_w_out, v_final_g):
    weights = dict(c_ctx=c_ctx, ada_w=ada_w, ada_b=ada_b, norm_g=norm_g, ev_w_in=ev_w_in, ev_q_norm=ev_q_norm,
                   ev_k_norm=ev_k_norm, ev_dw_w=ev_dw_w, ev_dw_b=ev_dw_b, ev_ln_g=ev_ln_g, ev_ln_b=ev_ln_b,
                   ev_w_out=ev_w_out, od_w_in=od_w_in, od_ln_g=od_ln_g, od_ln_b=od_ln_b, od_ws=od_ws, od_bs=od_bs,
                   od_w_out=od_w_out, final_g=final_g)
    mom_m = dict(c_ctx=m_c_ctx, ada_w=m_ada_w, ada_b=m_ada_b, norm_g=m_norm_g, ev_w_in=m_ev_w_in,
                 ev_q_norm=m_ev_q_norm, ev_k_norm=m_ev_k_norm, ev_dw_w=m_ev_dw_w, ev_dw_b=m_ev_dw_b,
                 ev_ln_g=m_ev_ln_g, ev_ln_b=m_ev_ln_b, ev_w_out=m_ev_w_out, od_w_in=m_od_w_in, od_ln_g=m_od_ln_g,
                 od_ln_b=m_od_ln_b, od_ws=m_od_ws, od_bs=m_od_bs, od_w_out=m_od_w_out, final_g=m_final_g)
    mom_v = dict(c_ctx=v_c_ctx, ada_w=v_ada_w, ada_b=v_ada_b, norm_g=v_norm_g, ev_w_in=v_ev_w_in,
                 ev_q_norm=v_ev_q_norm, ev_k_norm=v_ev_k_norm, ev_dw_w=v_ev_dw_w, ev_dw_b=v_ev_dw_b,
                 ev_ln_g=v_ev_ln_g, ev_ln_b=v_ev_ln_b, ev_w_out=v_ev_w_out, od_w_in=v_od_w_in, od_ln_g=v_od_ln_g,
                 od_ln_b=v_od_ln_b, od_ws=v_od_ws, od_bs=v_od_bs, od_w_out=v_od_w_out, final_g=v_final_g)
    order = list(weights)

    _, n_lat, d = x.shape
    n_ctx = ctx.shape[1]
    n_ext = n_lat + n_ctx
    ev_in = ev_w_in.shape[-1] * N_CHIP
    ev_mix = ev_w_out.shape[1] * N_CHIP
    conv_ch = ev_dw_b.shape[-1]
    conv_w = ev_dw_w.shape[1]
    attn_w = ev_mix - conv_ch
    kv_w = N_KV_HEADS * HEAD_DIM
    assert ev_in == 2 * kv_w + 2 * attn_w + 3 * conv_ch and conv_w // 2 < CONV_HALO
    sgu_w = od_w_out.shape[1] * N_CHIP
    wa = ada_w.shape[-1]
    cb = math.gcd(2 * kv_w, attn_w, conv_ch)
    off = dict(k=0, v=kv_w, q=2 * kv_w, za=2 * kv_w + attn_w, a=2 * kv_w + 2 * attn_w,
               b=2 * kv_w + 2 * attn_w + conv_ch, zb=2 * kv_w + 2 * attn_w + 2 * conv_ch)
    dims = dict(kv_w=kv_w, attn_w=attn_w, conv_ch=conv_ch, conv_w=conv_w, cb=cb, off=off)
    tr = 256 if (n_lat % 256 == 0 and n_ctx % 256 == 0) else 128

    mx, my, mc = lax.axis_index("x"), lax.axis_index("y"), lax.axis_index("c")
    me = 4 * mx + 2 * my + mc
    chip = 2 * mx + my

    x2d, tgt2d, ctx2d = x[0], loss_target[0], ctx[0]
    ev_dw_w_l = ev_dw_w[0]
    dwc = ev_dw_w_l.shape[1]
    lnc = od_ln_g.shape[1]

    g_c = _allgather_small(jnp.broadcast_to(c, (8, d)), "gather_cond")[:, 0, :]
    c_rows = jnp.concatenate([g_c, c_ctx[None, :], jnp.zeros((MOD_ROWS - N_DEV - 1, d), F32)], axis=0)
    c_rows_t = c_rows.T
    ada_b_shard = lax.dynamic_slice_in_dim(ada_b, chip * wa, wa, axis=1)[:, None, :]
    mod_part = _mod_fwd(c_rows_t, ada_w, ada_b_shard)
    part_shapes = [(2, MOD_ROWS, wa), (conv_w, dwc), (1, lnc), (1, lnc)]
    g_parts = _allgather_small(_pack([mod_part, ev_dw_w_l, od_ln_g, od_ln_b]), "gather_mod")
    per_chip = [_unpack(g_parts[2 * s], part_shapes) for s in range(N_CHIP)]
    mod_all = jnp.concatenate([p[0] for p in per_chip], axis=-1)
    dw_w_full = jnp.concatenate([p[1] for p in per_chip], axis=-1)
    od_ln_g_full = jnp.concatenate([p[2] for p in per_chip], axis=-1)
    od_ln_b_full = jnp.concatenate([p[3] for p in per_chip], axis=-1)
    dw_w_pad = jnp.pad(dw_w_full, ((0, 2 * CONV_HALO - conv_w), (0, 0)))
    mod_me = lax.dynamic_slice_in_dim(mod_all, me, 1, axis=1)
    shift0, scale0, gate0 = mod_me[0, :, :d], mod_me[0, :, d:2 * d], mod_me[0, :, 2 * d:]
    shift1, scale1, gate1 = mod_me[1, :, :d], mod_me[1, :, d:2 * d], mod_me[1, :, 2 * d:]
    shift_c, scale_c = mod_all[0, N_DEV:N_DEV + 1, :d], mod_all[0, N_DEV:N_DEV + 1, d:2 * d]
    g0, g1 = norm_g[0:1], norm_g[1:2]

    lay = dict(ev_w_in=_Sharded((d, ev_in), True), ev_w_out=_Sharded((ev_mix, d), False),
               od_w_in=_Sharded((d, 3 * sgu_w), True), od_w_out=_Sharded((sgu_w, d), False))
    big = list(lay)
    chip_arr = jnp.reshape(chip, (1,)).astype(jnp.int32)
    core_arr = jnp.reshape(mc, (1,)).astype(jnp.int32)
    own = {n: _cast_into_full(weights[n][0], lay[n], chip_arr, f"cast_{n}") for n in big}
    w_full = {"ev_w_in": _run_comm(_gather_comm([own["ev_w_in"]], [lay["ev_w_in"]]), "gather_ev_w_in")[0]}
    later = big[1:]
    gather_later = _gather_comm([own[n] for n in later], [lay[n] for n in later])

    def reduce_start(g, n):
        theirs = _pair_exchange([g], [lay[n]], f"pair_exchange_{n}")[0]
        psum = _pair_sum(g, theirs, lay[n], core_arr, f"pair_sum_{n}")
        return psum, _scatter_comm([psum], [lay[n]])

    h0 = _adaln_fwd(x2d, g0, shift0, scale0, tr, "adaln0_fwd")
    hc = _adaln_fwd(ctx2d, g0, shift_c, scale_c, tr, "adaln0_ctx_fwd")
    h0e = jnp.concatenate([h0, hc], axis=0)
    tm_e = _pick(n_ext, (1408, 768, 640, 512, 256, 128))
    tk_e = _pick(n_ext, (768, 640, 512, 256, 128))
    tm_l = _pick(n_lat, (1024, 512, 256, 128))
    p0, *gathered = _mm(h0e, w_full["ev_w_in"], name="mm_ev_in", tm=tm_e, tn=_pick(ev_in, (512, 256, 128)), tk=d,
                        comm=gather_later)
    w_full.update(zip(later, gathered))
    cos_i, sin_s = _rope_tables(n_lat, n_ctx)
    q_hat, k_all, v_all = _qk_prep(p0, cos_i, sin_s, ev_q_norm, ev_k_norm, dims, tr)
    tq = _pick(n_lat, (256, 128))
    tkk = _pick(n_ext, (768, 640, 512, 256, 128))
    attn, lse = _flash_fwd(q_hat, k_all, v_all, n_lat, dims, tq, tkk)
    mix, yc = _mix_fwd(attn, p0, dw_w_pad, ev_dw_b, ev_ln_g, ev_ln_b, dims, tr)
    o0, x1 = _mm(mix, w_full["ev_w_out"], name="mm_ev_out", tm=tm_l, tn=_pick(d, (1024, 512, 256)),
                 tk=_pick(ev_mix, (512, 256)), res=x2d, gate=gate0)

    h1 = _adaln_fwd(x1, g1, shift1, scale1, tr, "adaln1_fwd")
    p1 = _mm(h1, w_full["od_w_in"], name="mm_od_in", tm=tm_l, tn=_pick(3 * sgu_w, (512, 256, 128)), tk=d)
    ws_b = od_ws[0].astype(BF16)
    ws_t_b = jnp.swapaxes(od_ws[0], 1, 2).astype(BF16)
    bs_t = od_bs[0].T
    m1 = _sgu_fwd(p1, od_ln_g_full, od_ln_b_full, ws_b, bs_t, tr)
    o1, x2 = _mm(m1, w_full["od_w_out"], name="mm_od_out", tm=tm_l, tn=_pick(d, (1024, 512, 256)),
                 tk=_pick(sgu_w, (512, 256)), res=x1, gate=gate1)

    dx2, do1, loss_cols, d_final_g, dgate1 = _final_loss(x2, tgt2d, final_g[None, :], o1, gate1, tr)
    loss = lax.psum(0.5 / d * jnp.sum(loss_cols), ("x", "y", "c"))

    tk_l = _pick(n_lat, (1024, 512, 256, 128))
    psums, slots = {}, {}
    tk_nt = (1408, 1024, 768, 512, 256, 128)
    g_od_w_out = _mm(m1, do1, name="mm_od_out_dw", ta=True, tm=_pick(sgu_w, (1024, 512, 256)),
                     tn=_pick(d, (1024, 512, 256)), tk=tk_l)
    psums["od_w_out"], sc = reduce_start(g_od_w_out, "od_w_out")
    dm1, slots["od_w_out"] = _mm(do1, w_full["od_w_out"], name="mm_od_out_dx", tb=True, tm=tm_l,
                                 tn=_pick(sgu_w, (1024, 512, 256)), tk=_pick(d, tk_nt), comm=sc)
    dp1, d_ws, d_bs_t, d_od_ln_g, d_od_ln_b = _sgu_bwd(dm1, p1, od_ln_g_full, od_ln_b_full, ws_b, ws_t_b, bs_t, tr)
    g_od_w_in = _mm(h1, dp1, name="mm_od_in_dw", ta=True, tm=_pick(d, (1024, 512, 256)),
                    tn=_pick(3 * sgu_w, (1536, 768, 512, 384, 256, 128)), tk=tk_l)
    psums["od_w_in"], sc = reduce_start(g_od_w_in, "od_w_in")
    dh1, slots["od_w_in"] = _mm(dp1, w_full["od_w_in"], name="mm_od_in_dx", tb=True, tm=tm_l,
                                tn=_pick(d, (1024, 512, 256)), tk=_pick(3 * sgu_w, tk_nt), comm=sc)
    zero_d = jnp.zeros((1, d), F32)
    b1 = _adaln_bwd(x1, dh1, 0, g1, scale1, zero_d, tr, "adaln1_bwd", dres=dx2, o_prev=o0, gate_prev=gate0)
    dx1, do0, dgate0 = b1["dx"], b1["do_prev"], b1["dgate_prev"]

    g_ev_w_out = _mm(mix, do0, name="mm_ev_out_dw", ta=True, tm=_pick(ev_mix, (1024, 512, 256)),
                     tn=_pick(d, (1024, 512, 256)), tk=tk_l)
    psums["ev_w_out"], sc = reduce_start(g_ev_w_out, "ev_w_out")
    dmix, slots["ev_w_out"] = _mm(do0, w_full["ev_w_out"], name="mm_ev_out_dx", tb=True, tm=tm_l,
                                  tn=_pick(ev_mix, (1024, 512, 256)), tk=_pick(d, tk_nt), comm=sc)
    dattn, dza, dzb, dyc, d_ev_ln_g, d_ev_ln_b, d_dw_b = _mix_bwd_pointwise(
        dmix, attn, p0, yc, ev_ln_g, ev_ln_b, dims, tr)
    dab, d_dw_w_pad = _conv_bwd(dyc, p0, dw_w_pad, dims, tr)
    dq_hat, dk_hat, dv_all = _flash_bwd(q_hat, k_all, v_all, dattn, attn, lse, n_lat, dims, tq, tkk)
    dkvq, d_q_norm, d_k_norm = _qk_prep_bwd(dq_hat, dk_hat, dv_all, p0, cos_i, sin_s, ev_q_norm, ev_k_norm,
                                            dims, tr, n_lat)
    rest = jnp.pad(jnp.concatenate([dza, dab, dzb], axis=1), ((0, n_ctx), (0, 0)))
    dp0 = jnp.concatenate([dkvq, rest], axis=1)
    g_ev_w_in = _mm(h0e, dp0, name="mm_ev_in_dw", ta=True, tm=_pick(d, (1024, 512, 256)),
                    tn=_pick(ev_in, (1408, 768, 512, 256, 128)), tk=tk_e)
    psums["ev_w_in"], sc = reduce_start(g_ev_w_in, "ev_w_in")
    dh0, slots["ev_w_in"] = _mm(dp0, w_full["ev_w_in"], name="mm_ev_in_dx", tb=True, tm=tm_e,
                                tn=_pick(d, (1024, 512, 256)), tk=_pick(ev_in, tk_nt), comm=sc)
    bc = _adaln_bwd(ctx2d, dh0, n_lat, g0, scale_c, zero_d, tr, "adaln0_ctx_bwd")
    b0 = _adaln_bwd(x2d, dh0, 0, g0, scale0, bc["dg"], tr, "adaln0_bwd", dres=dx1)
    grad_x = b0["dx"]

    zeros_d = jnp.zeros((1, d), F32)
    dmod0 = jnp.concatenate([b0["dshift"], b0["dscale"], dgate0], axis=1)
    dmod1 = jnp.concatenate([b1["dshift"], b1["dscale"], dgate1], axis=1)
    dmodc = jnp.concatenate([bc["dshift"], bc["dscale"], zeros_d], axis=1)
    small = [dmod0, dmod1, dmodc, b0["dg"], b1["dg"], d_q_norm, d_k_norm, d_dw_w_pad[:conv_w], d_dw_b,
             d_ev_ln_g, d_ev_ln_b, d_od_ln_g, d_od_ln_b, d_ws, d_bs_t.T, d_final_g]
    small_shapes = [a.shape for a in small]
    g_small = _allgather_small(_pack(small), "gather_small_grads")
    tot = _unpack(_sum_slots(g_small, "sum_small_grads"), small_shapes)
    (t_dmod0, t_dmod1, t_dmodc, t_g0, t_g1, t_qn, t_kn, t_dw_w, t_dw_b, t_eln_g, t_eln_b, t_oln_g, t_oln_b,
     t_ws, t_bs, t_fg) = tot
    rows_dmod = _unpack(g_small, small_shapes[:2])
    dmod0_rows, dmod1_rows = rows_dmod[0][:, 0, :], rows_dmod[1][:, 0, :]
    pad_rows = jnp.zeros((MOD_ROWS - N_DEV - 1, 3 * d), F32)
    dm_l0 = jnp.concatenate([dmod0_rows, t_dmodc, pad_rows], axis=0)
    dm_l1 = jnp.concatenate([dmod1_rows, jnp.zeros((MOD_ROWS - N_DEV, 3 * d), F32)], axis=0)
    dm_shard = lax.dynamic_slice_in_dim(jnp.stack([dm_l0, dm_l1]), chip * wa, wa, axis=2)
    g_ada_w, dsc = _mod_bwd(c_rows_t, dm_shard, ada_w)
    g_dsc = _allgather_small(_pack([dsc[0]]), "gather_cctx")
    g_c_ctx = _cctx_grad(g_dsc, _pack([c_ctx])).reshape(-1)[:d]
    g_ada_b = jnp.stack([t_dmod0[0] + t_dmodc[0], t_dmod1[0]])

    lays = [lay[n] for n in big]
    halves = [_chip_sum(psums[n], slots[n], lay[n], chip_arr, core_arr, f"chip_sum_{n}") for n in big]
    g_big = dict(zip(big, _pair_share(halves, lays)))

    grads = dict(
        c_ctx=g_c_ctx, ada_w=g_ada_w, ada_b=g_ada_b, norm_g=jnp.concatenate([t_g0, t_g1], axis=0),
        ev_w_in=g_big["ev_w_in"][None], ev_q_norm=t_qn, ev_k_norm=t_kn,
        ev_dw_w=lax.dynamic_slice_in_dim(t_dw_w, chip * dwc, dwc, axis=1)[None], ev_dw_b=t_dw_b,
        ev_ln_g=t_eln_g, ev_ln_b=t_eln_b, ev_w_out=g_big["ev_w_out"][None], od_w_in=g_big["od_w_in"][None],
        od_ln_g=lax.dynamic_slice_in_dim(t_oln_g, chip * lnc, lnc, axis=1),
        od_ln_b=lax.dynamic_slice_in_dim(t_oln_b, chip * lnc, lnc, axis=1),
        od_ws=t_ws[None], od_bs=t_bs[None], od_w_out=g_big["od_w_out"][None], final_g=t_fg[0])
    grads = {n: grads[n].reshape(weights[n].shape) for n in order}

    delta, new_m, new_v = {}, {}, {}
    large = ("ada_w", "ev_w_in", "ev_w_out", "od_w_in", "od_w_out")
    for n in large:
        shp = weights[n].shape
        as2d = lambda a: a.reshape(-1, shp[-1])
        dl, nm, nv = _adamw(as2d(weights[n]), as2d(grads[n]), as2d(mom_m[n]), as2d(mom_v[n]), f"adamw_{n}")
        delta[n], new_m[n], new_v[n] = dl.reshape(shp), nm.reshape(shp), nv.reshape(shp)
    rest_names = [n for n in order if n not in large]
    rest_shapes = [weights[n].shape for n in rest_names]
    dl, nm, nv = _adamw(_pack([weights[n] for n in rest_names]), _pack([grads[n] for n in rest_names]),
                        _pack([mom_m[n] for n in rest_names]), _pack([mom_v[n] for n in rest_names]), "adamw_small")
    for n, a, b_, c_ in zip(rest_names, _unpack(dl, rest_shapes), _unpack(nm, rest_shapes), _unpack(nv, rest_shapes)):
        delta[n], new_m[n], new_v[n] = a, b_, c_

    return (loss, grad_x[None], *[grads[n] for n in order], *[delta[n] for n in order],
            *[new_m[n] for n in order], *[new_v[n] for n in order])
```

```python
import math

import jax
import jax.numpy as jnp
from jax import lax
from jax.experimental import pallas as pl
from jax.experimental.pallas import tpu as pltpu

F32 = jnp.float32
BF16 = jnp.bfloat16
EPS = 1e-6
GRID_W = 64
ROPE_THETA = 10000.0
HEAD_DIM = 128
N_KV_HEADS = 2
CONV_HALO = 16
LANES = 128
SUBLANES = 8
STENCIL_ROWS = 32
STENCIL_CHAINS = 4
REDUCE_ROWS = 32
N_DEV = 8
N_CHIP = 4
MOD_ROWS = 16
PACK_ROWS = 64
ADAM_LR, ADAM_B1, ADAM_B2, ADAM_EPS, ADAM_WD, ADAM_STEP = 0.001, 0.9, 0.999, 1e-08, 0.01, 10
VMEM_LIMIT = 56 * 1024 * 1024
MESH = pl.DeviceIdType.MESH
ANY = pl.BlockSpec(memory_space=pl.ANY)
VMEM_SPEC = pl.BlockSpec(memory_space=pltpu.VMEM)
CHIP_DELTAS = ((1, 0), (0, 1), (1, 1))


def _cp(sem=None):
    return pltpu.CompilerParams(dimension_semantics=sem, vmem_limit_bytes=VMEM_LIMIT)


def _pick(n, cands):
    for c in cands:
        if n % c == 0:
            return c
    raise ValueError(f"no tile for {n} in {cands}")


def _sigmoid(x):
    return 1.0 / (1.0 + jnp.exp(-x))


def _silu(x):
    return x * _sigmoid(x)


def _dsilu(x):
    s = _sigmoid(x)
    return s * (1.0 + x * (1.0 - s))


_GELU_C = math.sqrt(2.0 / math.pi)


def _gelu(x):
    return 0.5 * x * (1.0 + jnp.tanh(_GELU_C * (x + 0.044715 * x * x * x)))


def _dgelu(x):
    t = jnp.tanh(_GELU_C * (x + 0.044715 * x * x * x))
    return 0.5 * (1.0 + t) + 0.5 * x * (1.0 - t * t) * _GELU_C * (1.0 + 3.0 * 0.044715 * x * x)


def _vec(d):
    return pl.BlockSpec((1, d), lambda *_: (0, 0))


def _cat(refs):
    parts = [r[...].astype(F32) for r in refs]
    return parts[0] if len(parts) == 1 else jnp.concatenate(parts, axis=1)


def _col_specs(rows, off, width, cb, row_map):
    assert off % cb == 0 and width % cb == 0
    return [pl.BlockSpec((rows, cb), (lambda *g, _c=off // cb + t: (row_map(*g), _c))) for t in range(width // cb)]


def _my_pos():
    return lax.axis_index("x"), lax.axis_index("y"), lax.axis_index("c")


def _allgather_small(x, name):
    r, c = x.shape

    def body(x_ref, out_ref, send_sems, recv_sems, local_sem):
        mx, my, mc = _my_pos()
        me = 4 * mx + 2 * my + mc
        mine = pltpu.make_async_copy(x_ref, out_ref.at[me], local_sem)
        mine.start()
        deltas = [(dx, dy, dc) for dx in (0, 1) for dy in (0, 1) for dc in (0, 1) if (dx, dy, dc) != (0, 0, 0)]
        sends = []
        for k, (dx, dy, dc) in enumerate(deltas):
            px, py, pc = (mx + dx) % 2, (my + dy) % 2, (mc + dc) % 2
            cp = pltpu.make_async_remote_copy(
                src_ref=x_ref, dst_ref=out_ref.at[me], send_sem=send_sems.at[k], recv_sem=recv_sems.at[k],
                device_id=(px, py, pc), device_id_type=MESH)
            cp.start()
            sends.append(cp)
        for k, (dx, dy, dc) in enumerate(deltas):
            px, py, pc = (mx + dx) % 2, (my + dy) % 2, (mc + dc) % 2
            peer = 4 * px + 2 * py + pc
            pltpu.make_async_remote_copy(
                src_ref=x_ref, dst_ref=out_ref.at[peer], send_sem=send_sems.at[k], recv_sem=recv_sems.at[k],
                device_id=(px, py, pc), device_id_type=MESH).wait_recv()
        for cp in sends:
            cp.wait_send()
        mine.wait()

    return pl.pallas_call(
        body, name=name,
        out_shape=jax.ShapeDtypeStruct((N_DEV, r, c), x.dtype),
        in_specs=[VMEM_SPEC], out_specs=VMEM_SPEC,
        scratch_shapes=[pltpu.SemaphoreType.DMA((N_DEV - 1,)), pltpu.SemaphoreType.DMA((N_DEV - 1,)),
                        pltpu.SemaphoreType.DMA],
        compiler_params=pltpu.CompilerParams(vmem_limit_bytes=VMEM_LIMIT),
    )(x)


class _Sharded:
    def __init__(self, full_shape, by_cols):
        self.full = full_shape
        self.by_cols = by_cols
        rows, cols = full_shape
        if by_cols:
            self.shard, self.half, self.halves = (rows, cols // N_CHIP), (rows // 2, cols // N_CHIP), (rows // 2, cols)
        else:
            self.shard, self.half, self.halves = (rows // N_CHIP, cols), (rows // N_CHIP, cols // 2), (rows, cols // 2)

    def region(self, ref, s, h):
        if self.by_cols:
            return ref.at[pl.ds(h * self.half[0], self.half[0]), pl.ds(s * self.shard[1], self.shard[1])]
        return ref.at[pl.ds(s * self.shard[0], self.shard[0]), pl.ds(h * self.half[1], self.half[1])]

    def halves_of_full(self, ref, h):
        if self.by_cols:
            return ref.at[pl.ds(h * self.halves[0], self.halves[0]), :]
        return ref.at[:, pl.ds(h * self.halves[1], self.halves[1])]

    def region_in_halves(self, ref, s):
        if self.by_cols:
            return ref.at[:, pl.ds(s * self.shard[1], self.shard[1])]
        return ref.at[pl.ds(s * self.shard[0], self.shard[0]), :]

    def half_of_shard(self, ref, h):
        if self.by_cols:
            return ref.at[pl.ds(h * self.half[0], self.half[0]), :]
        return ref.at[:, pl.ds(h * self.half[1], self.half[1])]


def _row_tile(rows, row_bytes):
    for t in (512, 256, 128, 64, 32, 16):
        if rows % t == 0 and t * row_bytes <= 2 * 1024 * 1024:
            return t
    return 16


def _cast_into_full(w_shard, lay, chip_arr, name):
    r, c = lay.shard
    tr = _row_tile(r, c * 4)
    nt = r // tr

    def body(chip_ref, w_ref, o_ref):
        o_ref[...] = w_ref[...].astype(BF16)

    if lay.by_cols:
        out_map = lambda i, chip_ref: (i, chip_ref[0])
    else:
        out_map = lambda i, chip_ref: (chip_ref[0] * nt + i, 0)
    return pl.pallas_call(
        body, name=name,
        grid_spec=pltpu.PrefetchScalarGridSpec(
            num_scalar_prefetch=1, grid=(nt,),
            in_specs=[pl.BlockSpec((tr, c), lambda i, chip_ref: (i, 0))],
            out_specs=pl.BlockSpec((tr, c), out_map)),
        out_shape=jax.ShapeDtypeStruct(lay.full, BF16), compiler_params=_cp(("parallel",)))(chip_arr, w_shard)


class _Carried:
    def __init__(self, ins, out_shapes, aliases, sem_shape, start, finish):
        self.ins, self.out_shapes, self.aliases, self.sem_shape = list(ins), list(out_shapes), dict(aliases), sem_shape
        self.start, self.finish = start, finish

    def scratch(self):
        return [pltpu.SemaphoreType.DMA(self.sem_shape), pltpu.SemaphoreType.DMA(self.sem_shape)]

    def split(self, in_refs, out_refs, scratch_refs):
        ni, no = len(self.ins), len(self.out_shapes)
        return in_refs[len(in_refs) - ni:], out_refs[len(out_refs) - no:], scratch_refs[-2], scratch_refs[-1]


def _run_comm(comm, name):
    ni, no = len(comm.ins), len(comm.out_shapes)

    def body(*refs):
        ins, outs, send_sems, recv_sems = refs[:ni], refs[ni:ni + no], refs[ni + no], refs[ni + no + 1]
        comm.start(ins, outs, send_sems, recv_sems)
        comm.finish(ins, outs, send_sems, recv_sems)

    return pl.pallas_call(body, name=name, out_shape=comm.out_shapes, in_specs=[ANY] * ni, out_specs=[ANY] * no,
                          input_output_aliases=comm.aliases, scratch_shapes=comm.scratch())(*comm.ins)


def _gather_comm(fulls, layouts):
    n = len(fulls)

    def ici(ins, outs, send_sems, recv_sems, a, j, landed=False):
        mx, my, mc = _my_pos()
        dx, dy = CHIP_DELTAS[j]
        px, py = (mx + dx) % 2, (my + dy) % 2
        src_chip = 2 * px + py if landed else 2 * mx + my
        return pltpu.make_async_remote_copy(
            src_ref=layouts[a].region(ins[a], src_chip, mc), dst_ref=layouts[a].region(outs[a], src_chip, mc),
            send_sem=send_sems.at[a, j], recv_sem=recv_sems.at[a, j], device_id=(px, py, mc), device_id_type=MESH)

    def d2d(ins, outs, send_sems, recv_sems, a, j, landed=False):
        mx, my, mc = _my_pos()
        dx, dy = CHIP_DELTAS[j]
        other = 2 * ((mx + dx) % 2) + (my + dy) % 2
        half = 1 - mc if landed else mc
        region = layouts[a].region(outs[a], other, half)
        return pltpu.make_async_remote_copy(
            src_ref=region, dst_ref=region, send_sem=send_sems.at[a, 3 + j], recv_sem=recv_sems.at[a, 3 + j],
            device_id=(mx, my, 1 - mc), device_id_type=MESH)

    pairs = [(a, j) for a in range(n) for j in range(3)]

    def start(*r):
        for a, j in pairs:
            ici(*r, a, j).start()

    def finish(*r):
        for a, j in pairs:
            ici(*r, a, j, landed=True).wait_recv()
            d2d(*r, a, j).start()
        for a, j in pairs:
            d2d(*r, a, j, landed=True).wait_recv()
        for a, j in pairs:
            ici(*r, a, j).wait_send()
            d2d(*r, a, j).wait_send()

    return _Carried(fulls, [jax.ShapeDtypeStruct(lay.full, BF16) for lay in layouts], {a: a for a in range(n)},
                    (n, 6), start, finish)


def _pair_exchange(grads, layouts, name):
    n = len(grads)

    def body(*refs):
        ins, outs = refs[:n], refs[n:2 * n]
        send_sems, recv_sems = refs[2 * n:]
        mx, my, mc = _my_pos()
        copies = []
        for a, lay in enumerate(layouts):
            cp = pltpu.make_async_remote_copy(
                src_ref=lay.halves_of_full(ins[a], 1 - mc), dst_ref=outs[a],
                send_sem=send_sems.at[a], recv_sem=recv_sems.at[a],
                device_id=(mx, my, 1 - mc), device_id_type=MESH)
            cp.start()
            copies.append(cp)
        for cp in copies:
            cp.wait()

    return pl.pallas_call(
        body, name=name,
        out_shape=[jax.ShapeDtypeStruct(lay.halves, F32) for lay in layouts],
        in_specs=[ANY] * n, out_specs=[ANY] * n,
        scratch_shapes=[pltpu.SemaphoreType.DMA((n,)), pltpu.SemaphoreType.DMA((n,))],
    )(*grads)


def _pair_sum(g, theirs, lay, core_arr, name):
    r, c = lay.halves
    tr = _row_tile(r, c * 4)
    nt = r // tr

    def body(core_ref, g_ref, t_ref, o_ref):
        o_ref[...] = (g_ref[...] + t_ref[...]).astype(BF16)

    if lay.by_cols:
        g_map = lambda i, core_ref: (core_ref[0] * nt + i, 0)
    else:
        g_map = lambda i, core_ref: (i, core_ref[0])
    plain = pl.BlockSpec((tr, c), lambda i, core_ref: (i, 0))
    return pl.pallas_call(
        body, name=name,
        grid_spec=pltpu.PrefetchScalarGridSpec(
            num_scalar_prefetch=1, grid=(nt,), in_specs=[pl.BlockSpec((tr, c), g_map), plain], out_specs=plain),
        out_shape=jax.ShapeDtypeStruct((r, c), BF16), compiler_params=_cp(("parallel",)))(core_arr, g, theirs)


def _scatter_comm(pair_sums, layouts):
    n = len(pair_sums)

    def copy(ins, outs, send_sems, recv_sems, a, j):
        mx, my, mc = _my_pos()
        dx, dy = CHIP_DELTAS[j]
        px, py = (mx + dx) % 2, (my + dy) % 2
        return pltpu.make_async_remote_copy(
            src_ref=layouts[a].region_in_halves(ins[a], 2 * px + py), dst_ref=outs[a].at[j],
            send_sem=send_sems.at[a, j], recv_sem=recv_sems.at[a, j], device_id=(px, py, mc), device_id_type=MESH)

    pairs = [(a, j) for a in range(n) for j in range(3)]

    def start(*r):
        for a, j in pairs:
            copy(*r, a, j).start()

    def finish(*r):
        for a, j in pairs:
            copy(*r, a, j).wait()

    return _Carried(pair_sums, [jax.ShapeDtypeStruct((3,) + lay.half, BF16) for lay in layouts], {}, (n, 3),
                    start, finish)


def _chip_sum(pair_sum, slots, lay, chip_arr, core_arr, name):
    r, c = lay.half
    tr = _row_tile(r, c * 4)
    nt = r // tr

    def body(chip_ref, core_ref, s_ref, slot_ref, o_ref):
        acc = s_ref[...].astype(F32)
        for j in range(3):
            acc = acc + slot_ref[j].astype(F32)
        o_ref[...] = acc

    if lay.by_cols:
        s_map = lambda i, chip_ref, core_ref: (i, chip_ref[0])
        o_map = lambda i, chip_ref, core_ref: (core_ref[0] * nt + i, 0)
    else:
        s_map = lambda i, chip_ref, core_ref: (chip_ref[0] * nt + i, 0)
        o_map = lambda i, chip_ref, core_ref: (i, core_ref[0])
    return pl.pallas_call(
        body, name=name,
        grid_spec=pltpu.PrefetchScalarGridSpec(
            num_scalar_prefetch=2, grid=(nt,),
            in_specs=[pl.BlockSpec((tr, c), s_map),
                      pl.BlockSpec((3, tr, c), lambda i, chip_ref, core_ref: (0, i, 0))],
            out_specs=pl.BlockSpec((tr, c), o_map)),
        out_shape=jax.ShapeDtypeStruct(lay.shard, F32), compiler_params=_cp(("parallel",)))(
            chip_arr, core_arr, pair_sum, slots)


def _pair_share(bufs, layouts):
    n = len(bufs)

    def body(*refs):
        ins, outs = refs[:n], refs[n:2 * n]
        send_sems, recv_sems = refs[2 * n:]
        mx, my, mc = _my_pos()
        copies = []
        for a, lay in enumerate(layouts):
            cp = pltpu.make_async_remote_copy(
                src_ref=lay.half_of_shard(ins[a], mc), dst_ref=lay.half_of_shard(outs[a], mc),
                send_sem=send_sems.at[a], recv_sem=recv_sems.at[a],
                device_id=(mx, my, 1 - mc), device_id_type=MESH)
            cp.start()
            copies.append(cp)
        for a, lay in enumerate(layouts):
            theirs = lay.half_of_shard(outs[a], 1 - mc)
            pltpu.make_async_remote_copy(
                src_ref=theirs, dst_ref=theirs, send_sem=send_sems.at[a], recv_sem=recv_sems.at[a],
                device_id=(mx, my, 1 - mc), device_id_type=MESH).wait_recv()
        for cp in copies:
            cp.wait_send()

    return pl.pallas_call(
        body, name="grad_pair_share",
        out_shape=[jax.ShapeDtypeStruct(lay.shard, F32) for lay in layouts],
        in_specs=[ANY] * n, out_specs=[ANY] * n, input_output_aliases={a: a for a in range(n)},
        scratch_shapes=[pltpu.SemaphoreType.DMA((n,)), pltpu.SemaphoreType.DMA((n,))],
    )(*bufs)


def _sum_slots(x, name):
    s, r, c = x.shape
    tr = _pick(r, (256, 128, 64, 32, 16, 8))

    def body(x_ref, o_ref):
        acc = x_ref[0]
        for k in range(1, s):
            acc = acc + x_ref[k]
        o_ref[...] = acc

    return pl.pallas_call(body, name=name, grid=(r // tr,),
                          in_specs=[pl.BlockSpec((s, tr, c), lambda i: (0, i, 0))],
                          out_specs=pl.BlockSpec((tr, c), lambda i: (i, 0)),
                          out_shape=jax.ShapeDtypeStruct((r, c), F32), compiler_params=_cp(("parallel",)))(x)


def _adamw(w, g, m, v, name):
    r, c = w.shape
    tr = _pick(r, (256, 128, 64, 32, 16, 8))
    bc1 = 1.0 - ADAM_B1 ** ADAM_STEP
    bc2 = 1.0 - ADAM_B2 ** ADAM_STEP

    def body(w_ref, g_ref, m_ref, v_ref, d_ref, nm_ref, nv_ref):
        gv = g_ref[...]
        nm = ADAM_B1 * m_ref[...] + (1.0 - ADAM_B1) * gv
        nv = ADAM_B2 * v_ref[...] + (1.0 - ADAM_B2) * (gv * gv)
        d_ref[...] = -ADAM_LR * ((nm / bc1) / (jnp.sqrt(nv / bc2) + ADAM_EPS) + ADAM_WD * w_ref[...])
        nm_ref[...] = nm
        nv_ref[...] = nv

    spec = pl.BlockSpec((tr, c), lambda i: (i, 0))
    shp = jax.ShapeDtypeStruct((r, c), F32)
    return pl.pallas_call(body, name=name, grid=(r // tr,), in_specs=[spec] * 4, out_specs=[spec] * 3,
                          out_shape=[shp] * 3, compiler_params=_cp(("parallel",)))(w, g, m, v)


def _cctx_grad(parts, c_ctx2d):
    def body(p_ref, c_ref, o_ref):
        tot = ((p_ref[0] + p_ref[2]) + p_ref[4]) + p_ref[6]
        o_ref[...] = tot * _dsilu(c_ref[...])

    return pl.pallas_call(body, name="cctx_grad", in_specs=[VMEM_SPEC, VMEM_SPEC], out_specs=VMEM_SPEC,
                          out_shape=jax.ShapeDtypeStruct(c_ctx2d.shape, F32))(parts, c_ctx2d)


def _mod_fwd(c_rows_t, ada_w, ada_b_shard):
    nl, d, w = ada_w.shape
    td = _pick(d, (256, 128))
    nd = d // td

    def body(ct_ref, w_ref, b_ref, o_ref):
        i = pl.program_id(1)

        @pl.when(i == 0)
        def _():
            o_ref[0] = jnp.broadcast_to(b_ref[0], (MOD_ROWS, w))

        st = _silu(ct_ref[...])
        wv = w_ref[0]
        rows = [jnp.sum(st[:, r:r + 1] * wv, axis=0, keepdims=True) for r in range(MOD_ROWS)]
        o_ref[0] += jnp.concatenate(rows, axis=0)

    return pl.pallas_call(
        body, name="mod_fwd", grid=(nl, nd),
        in_specs=[pl.BlockSpec((td, MOD_ROWS), lambda l, i: (i, 0)),
                  pl.BlockSpec((1, td, w), lambda l, i: (l, i, 0)),
                  pl.BlockSpec((1, 1, w), lambda l, i: (l, 0, 0))],
        out_specs=pl.BlockSpec((1, MOD_ROWS, w), lambda l, i: (l, 0, 0)),
        out_shape=jax.ShapeDtypeStruct((nl, MOD_ROWS, w), F32),
        compiler_params=_cp(("parallel", "arbitrary")),
    )(c_rows_t, ada_w, ada_b_shard)


def _mod_bwd(c_rows_t, dmod, ada_w):
    nl, d, w = ada_w.shape
    td = _pick(d, (256, 128))
    ctx_row = N_DEV

    def body(ct_ref, dm_ref, w_ref, gw_ref, ds_ref):
        st = _silu(ct_ref[...])
        dm = dm_ref[0]
        acc = st[:, 0:1] * dm[0:1, :]
        for r in range(1, ctx_row + 1):
            acc = acc + st[:, r:r + 1] * dm[r:r + 1, :]
        gw_ref[0] = acc
        ds_ref[0] = jnp.sum(w_ref[0] * dm[ctx_row:ctx_row + 1, :], axis=1, keepdims=True)

    return pl.pallas_call(
        body, name="mod_bwd", grid=(nl, d // td),
        in_specs=[pl.BlockSpec((td, MOD_ROWS), lambda l, i: (i, 0)),
                  pl.BlockSpec((1, MOD_ROWS, w), lambda l, i: (l, 0, 0)),
                  pl.BlockSpec((1, td, w), lambda l, i: (l, i, 0))],
        out_specs=[pl.BlockSpec((1, td, w), lambda l, i: (l, i, 0)),
                   pl.BlockSpec((1, td, 1), lambda l, i: (l, i, 0))],
        out_shape=[jax.ShapeDtypeStruct((nl, d, w), F32), jax.ShapeDtypeStruct((nl, d, 1), F32)],
        compiler_params=_cp(("parallel", "parallel")),
    )(c_rows_t, dmod, ada_w)


def _adaln_fwd(x, g, shift, scale, tr, name):
    r, d = x.shape

    def body(x_ref, g_ref, sh_ref, sc_ref, o_ref):
        xv = x_ref[...]
        rs = lax.rsqrt(jnp.mean(xv * xv, axis=-1, keepdims=True) + EPS)
        o_ref[...] = ((xv * rs * g_ref[...]) * (1.0 + sc_ref[...]) + sh_ref[...]).astype(BF16)

    spec = pl.BlockSpec((tr, d), lambda i: (i, 0))
    return pl.pallas_call(body, name=name, grid=(r // tr,), in_specs=[spec, _vec(d), _vec(d), _vec(d)],
                          out_specs=spec, out_shape=jax.ShapeDtypeStruct((r, d), BF16),
                          compiler_params=_cp(("parallel",)))(x, g, shift, scale)


def _adaln_bwd(xin, dh, row0, g, scale, dg_init, tr, name, dres=None, o_prev=None, gate_prev=None):
    r, d = xin.shape
    assert row0 % tr == 0
    rb0 = row0 // tr
    want_dx = dres is not None
    want_prev = o_prev is not None
    assert want_dx or not want_prev

    def body(*refs):
        it = iter(refs)
        x_ref, dh_ref, g_ref, sc_ref, dgi_ref = next(it), next(it), next(it), next(it), next(it)
        dres_ref = next(it) if want_dx else None
        o_ref, gp_ref = (next(it), next(it)) if want_prev else (None, None)
        dx_ref = next(it) if want_dx else None
        do_ref = next(it) if want_prev else None
        dsh_ref, dsc_ref, dg_ref = next(it), next(it), next(it)
        dgp_ref = next(it) if want_prev else None
        i = pl.program_id(0)

        @pl.when(i == 0)
        def _():
            dsh_ref[...] = jnp.zeros_like(dsh_ref)
            dsc_ref[...] = jnp.zeros_like(dsc_ref)
            dg_ref[...] = dgi_ref[...]
            if want_prev:
                dgp_ref[...] = jnp.zeros_like(dgp_ref)

        xv = x_ref[...]
        dhv = dh_ref[...].astype(F32)
        gv = g_ref[...]
        rs = lax.rsqrt(jnp.mean(xv * xv, axis=-1, keepdims=True) + EPS)
        xn = xv * rs
        dsh_ref[...] += jnp.sum(dhv, axis=0, keepdims=True)
        dsc_ref[...] += jnp.sum(dhv * (xn * gv), axis=0, keepdims=True)
        dr = dhv * (1.0 + sc_ref[...])
        dg_ref[...] += jnp.sum(dr * xn, axis=0, keepdims=True)
        if want_dx:
            gy = dr * gv
            dx = dres_ref[...] + rs * (gy - xn * jnp.mean(gy * xn, axis=-1, keepdims=True))
            dx_ref[...] = dx
            if want_prev:
                do_ref[...] = (gp_ref[...] * dx).astype(BF16)
                dgp_ref[...] += jnp.sum(dx * o_ref[...].astype(F32), axis=0, keepdims=True)

    row = pl.BlockSpec((tr, d), lambda i: (i, 0))
    in_specs = [row, pl.BlockSpec((tr, d), lambda i: (rb0 + i, 0)), _vec(d), _vec(d), _vec(d)]
    args = [xin, dh, g, scale, dg_init]
    out_specs, out_shape, names = [], [], []
    if want_dx:
        in_specs.append(row)
        args.append(dres)
    if want_prev:
        in_specs += [row, _vec(d)]
        args += [o_prev, gate_prev]
    if want_dx:
        out_specs.append(row)
        out_shape.append(jax.ShapeDtypeStruct((r, d), F32))
        names.append("dx")
    if want_prev:
        out_specs.append(row)
        out_shape.append(jax.ShapeDtypeStruct((r, d), BF16))
        names.append("do_prev")
    for nm in ("dshift", "dscale", "dg") + (("dgate_prev",) if want_prev else ()):
        out_specs.append(_vec(d))
        out_shape.append(jax.ShapeDtypeStruct((1, d), F32))
        names.append(nm)
    outs = pl.pallas_call(body, name=name, grid=(r // tr,), in_specs=in_specs, out_specs=out_specs,
                          out_shape=out_shape, compiler_params=_cp(("arbitrary",)))(*args)
    return dict(zip(names, outs))


def _mm(a, b, *, name, tm, tn, tk, ta=False, tb=False, out_dtype=F32, res=None, gate=None, comm=None):
    if ta:
        kd, m = a.shape
    else:
        m, kd = a.shape
    if tb:
        n, kd2 = b.shape
    else:
        kd2, n = b.shape
    assert kd == kd2 and m % tm == 0 and n % tn == 0 and kd % tk == 0, (a.shape, b.shape, tm, tn, tk)
    ni, nj, nk = m // tm, n // tn, kd // tk
    dn = (((0 if ta else 1,), (1 if tb else 0,)), ((), ()))
    with_res = res is not None
    n_in = 4 if with_res else 2
    n_out = 2 if with_res else 1
    n_cin = len(comm.ins) if comm else 0
    n_cout = len(comm.out_shapes) if comm else 0

    def body(*refs):
        in_refs = refs[:n_in + n_cin]
        out_refs = refs[n_in + n_cin:n_in + n_cin + n_out + n_cout]
        scratch = refs[n_in + n_cin + n_out + n_cout:]
        a_ref, b_ref = in_refs[0], in_refs[1]
        o_ref, acc_ref = out_refs[0], scratch[0]
        i, j, k = pl.program_id(0), pl.program_id(1), pl.program_id(2)
        if comm:
            carried = comm.split(in_refs, out_refs, scratch)

            @pl.when(jnp.logical_and(jnp.logical_and(i == 0, j == 0), k == 0))
            def _():
                comm.start(*carried)

        @pl.when(k == 0)
        def _():
            acc_ref[...] = jnp.zeros_like(acc_ref)

        acc_ref[...] += lax.dot_general(a_ref[...], b_ref[...], dn, preferred_element_type=F32)

        @pl.when(k == nk - 1)
        def _():
            acc = acc_ref[...]
            o_ref[...] = acc.astype(o_ref.dtype)
            if with_res:
                out_refs[1][...] = in_refs[2][...] + in_refs[3][...] * acc

        if comm:
            @pl.when(jnp.logical_and(jnp.logical_and(i == ni - 1, j == nj - 1), k == nk - 1))
            def _():
                comm.finish(*carried)

    a_spec = pl.BlockSpec((tk, tm), lambda i, j, k: (k, i)) if ta else pl.BlockSpec((tm, tk), lambda i, j, k: (i, k))
    b_spec = pl.BlockSpec((tn, tk), lambda i, j, k: (j, k)) if tb else pl.BlockSpec((tk, tn), lambda i, j, k: (k, j))
    o_spec = pl.BlockSpec((tm, tn), lambda i, j, k: (i, j))
    in_specs, args = [a_spec, b_spec], [a, b]
    out_specs, out_shape = [o_spec], [jax.ShapeDtypeStruct((m, n), out_dtype)]
    if with_res:
        in_specs += [o_spec, pl.BlockSpec((1, tn), lambda i, j, k: (0, j))]
        args += [res, gate]
        out_specs.append(o_spec)
        out_shape.append(jax.ShapeDtypeStruct((m, n), F32))
    scratch_shapes = [pltpu.VMEM((tm, tn), F32)]
    aliases = {}
    sem = ("parallel", "parallel", "arbitrary")
    if comm:
        in_specs += [ANY] * n_cin
        args += comm.ins
        out_specs += [ANY] * n_cout
        out_shape += comm.out_shapes
        scratch_shapes += comm.scratch()
        aliases = {n_in + s: n_out + d for s, d in comm.aliases.items()}
        sem = ("arbitrary", "arbitrary", "arbitrary")
    outs = pl.pallas_call(body, name=name, grid=(ni, nj, nk), in_specs=in_specs, out_specs=out_specs,
                          out_shape=out_shape, scratch_shapes=scratch_shapes, input_output_aliases=aliases,
                          compiler_params=_cp(sem))(*args)
    return outs if (with_res or comm) else outs[0]


def _swap_pairs(x):
    lane = lax.broadcasted_iota(jnp.int32, x.shape, 1)
    return jnp.where(lane % 2 == 0, pltpu.roll(x, HEAD_DIM - 1, 1), pltpu.roll(x, 1, 1))


def _qk_prep(p0, cos_i, sin_s, q_norm, k_norm, dims, tr):
    rows = p0.shape[0]
    kvw, aw, cb = dims["kv_w"], dims["attn_w"], dims["cb"]
    nkv, nq = kvw // HEAD_DIM, aw // HEAD_DIM
    scale = HEAD_DIM ** -0.5
    n_kv_specs, n_q_specs = (2 * kvw) // cb, aw // cb

    def body(*refs):
        kv_refs = refs[:n_kv_specs]
        q_refs = refs[n_kv_specs:n_kv_specs + n_q_specs]
        cos_ref, sin_ref, qn_ref, kn_ref, qo_ref, ko_ref, vo_ref = refs[n_kv_specs + n_q_specs:]
        kv = _cat(kv_refs)
        qv = _cat(q_refs)
        cs, sn = cos_ref[...], sin_ref[...]

        def norm_rope(xh, gvec):
            rs = lax.rsqrt(jnp.mean(xh * xh, axis=-1, keepdims=True) + EPS)
            xn = xh * rs * gvec
            return xn * cs + _swap_pairs(xn) * sn

        for h in range(nkv):
            sl = slice(h * HEAD_DIM, (h + 1) * HEAD_DIM)
            ko_ref[:, sl] = norm_rope(kv[:, sl], kn_ref[...]).astype(BF16)
        vo_ref[...] = kv[:, kvw:].astype(BF16)
        for h in range(nq):
            sl = slice(h * HEAD_DIM, (h + 1) * HEAD_DIM)
            qo_ref[:, sl] = (norm_rope(qv[:, sl], qn_ref[...]) * scale).astype(BF16)

    rm = lambda i: i
    in_specs = (_col_specs(tr, 0, 2 * kvw, cb, rm) + _col_specs(tr, 2 * kvw, aw, cb, rm)
                + [pl.BlockSpec((tr, HEAD_DIM), lambda i: (i, 0))] * 2 + [_vec(HEAD_DIM)] * 2)
    args = [p0] * (n_kv_specs + n_q_specs) + [cos_i, sin_s, q_norm, k_norm]
    return pl.pallas_call(
        body, name="qk_prep", grid=(rows // tr,), in_specs=in_specs,
        out_specs=[pl.BlockSpec((tr, aw), lambda i: (i, 0)), pl.BlockSpec((tr, kvw), lambda i: (i, 0)),
                   pl.BlockSpec((tr, kvw), lambda i: (i, 0))],
        out_shape=[jax.ShapeDtypeStruct((rows, aw), BF16), jax.ShapeDtypeStruct((rows, kvw), BF16),
                   jax.ShapeDtypeStruct((rows, kvw), BF16)],
        compiler_params=_cp(("parallel",)))(*args)


def _qk_prep_bwd(dq_hat, dk_hat, dv, p0, cos_i, sin_s, q_norm, k_norm, dims, tr, n_lat):
    rows = p0.shape[0]
    kvw, aw, cb = dims["kv_w"], dims["attn_w"], dims["cb"]
    nkv, nq = kvw // HEAD_DIM, aw // HEAD_DIM
    scale = HEAD_DIM ** -0.5
    n_kv_specs, n_q_specs = (2 * kvw) // cb, aw // cb
    lat_tiles = n_lat // tr

    def body(*refs):
        kv_refs = refs[:n_kv_specs]
        q_refs = refs[n_kv_specs:n_kv_specs + n_q_specs]
        (dq_ref, dk_ref, dv_ref, cos_ref, sin_ref, qn_ref, kn_ref,
         out_ref, dqn_ref, dkn_ref) = refs[n_kv_specs + n_q_specs:]
        i = pl.program_id(0)

        @pl.when(i == 0)
        def _():
            dqn_ref[...] = jnp.zeros_like(dqn_ref)
            dkn_ref[...] = jnp.zeros_like(dkn_ref)

        kv = _cat(kv_refs)
        qv = _cat(q_refs)
        cs, sn = cos_ref[...], sin_ref[...]
        is_lat = (i < lat_tiles).astype(F32)

        def head_bwd(xh, dhat, gvec):
            dn = dhat * cs + _swap_pairs(dhat * sn)
            rs = lax.rsqrt(jnp.mean(xh * xh, axis=-1, keepdims=True) + EPS)
            xn = xh * rs
            gy = dn * gvec
            dx = rs * (gy - xn * jnp.mean(gy * xn, axis=-1, keepdims=True))
            return dx, jnp.sum(dn * xn, axis=0, keepdims=True)

        dkn = jnp.zeros((1, HEAD_DIM), F32)
        for h in range(nkv):
            sl = slice(h * HEAD_DIM, (h + 1) * HEAD_DIM)
            dx, dgv = head_bwd(kv[:, sl], dk_ref[:, sl], kn_ref[...])
            out_ref[:, sl] = dx.astype(BF16)
            dkn = dkn + dgv
        dkn_ref[...] += dkn
        out_ref[:, kvw:2 * kvw] = dv_ref[...].astype(BF16)
        dqn = jnp.zeros((1, HEAD_DIM), F32)
        for h in range(nq):
            sl = slice(h * HEAD_DIM, (h + 1) * HEAD_DIM)
            dx, dgv = head_bwd(qv[:, sl], dq_ref[:, sl] * (scale * is_lat), qn_ref[...])
            out_ref[:, 2 * kvw + h * HEAD_DIM:2 * kvw + (h + 1) * HEAD_DIM] = dx.astype(BF16)
            dqn = dqn + dgv
        dqn_ref[...] += dqn

    rm = lambda i: i
    wout = 2 * kvw + aw
    in_specs = (_col_specs(tr, 0, 2 * kvw, cb, rm) + _col_specs(tr, 2 * kvw, aw, cb, rm)
                + [pl.BlockSpec((tr, aw), lambda i: (jnp.minimum(i, lat_tiles - 1), 0)),
                   pl.BlockSpec((tr, kvw), lambda i: (i, 0)), pl.BlockSpec((tr, kvw), lambda i: (i, 0)),
                   pl.BlockSpec((tr, HEAD_DIM), lambda i: (i, 0)), pl.BlockSpec((tr, HEAD_DIM), lambda i: (i, 0)),
                   _vec(HEAD_DIM), _vec(HEAD_DIM)])
    args = [p0] * (n_kv_specs + n_q_specs) + [dq_hat, dk_hat, dv, cos_i, sin_s, q_norm, k_norm]
    return pl.pallas_call(
        body, name="qk_prep_bwd", grid=(rows // tr,), in_specs=in_specs,
        out_specs=[pl.BlockSpec((tr, wout), lambda i: (i, 0)), _vec(HEAD_DIM), _vec(HEAD_DIM)],
        out_shape=[jax.ShapeDtypeStruct((rows, wout), BF16), jax.ShapeDtypeStruct((1, HEAD_DIM), F32),
                   jax.ShapeDtypeStruct((1, HEAD_DIM), F32)],
        compiler_params=_cp(("arbitrary",)))(*args)


def _stack_heads(x, g):
    return jnp.concatenate([x[:, h * HEAD_DIM:(h + 1) * HEAD_DIM] for h in range(g)], axis=0)


def _flash_fwd(q_hat, k_all, v_all, n_lat, dims, tq, tk):
    kvw, aw = dims["kv_w"], dims["attn_w"]
    nkv = kvw // HEAD_DIM
    g = aw // kvw
    gw = g * HEAD_DIM
    n_keys = k_all.shape[0]
    ni, nj = n_lat // tq, n_keys // tk
    dn_nt = (((1,), (1,)), ((), ()))

    def body(q_ref, k_ref, v_ref, o_ref, lse_ref):
        qs = _stack_heads(q_ref[...], g)
        m = jnp.full((g * tq, 1), -1e30, F32)
        l = jnp.zeros((g * tq, 1), F32)
        acc = jnp.zeros((g * tq, HEAD_DIM), F32)
        for j in range(nj):
            kb = k_ref[pl.ds(j * tk, tk), :]
            vb = v_ref[pl.ds(j * tk, tk), :]
            s = lax.dot_general(qs, kb, dn_nt, preferred_element_type=F32)
            m_new = jnp.maximum(m, jnp.max(s, axis=-1, keepdims=True))
            alpha = jnp.exp(m - m_new)
            p = jnp.exp(s - m_new)
            l = alpha * l + jnp.sum(p, axis=-1, keepdims=True)
            acc = alpha * acc + jnp.dot(p.astype(BF16), vb, preferred_element_type=F32)
            m = m_new
        o = acc / l
        for h in range(g):
            o_ref[:, h * HEAD_DIM:(h + 1) * HEAD_DIM] = o[h * tq:(h + 1) * tq]
        lse_ref[...] = m + jnp.log(l)

    return pl.pallas_call(
        body, name="flash_fwd", grid=(nkv, ni),
        in_specs=[pl.BlockSpec((tq, gw), lambda h, i: (i, h)),
                  pl.BlockSpec((n_keys, HEAD_DIM), lambda h, i: (0, h)),
                  pl.BlockSpec((n_keys, HEAD_DIM), lambda h, i: (0, h))],
        out_specs=[pl.BlockSpec((tq, gw), lambda h, i: (i, h)),
                   pl.BlockSpec((g * tq, 1), lambda h, i: (h * ni + i, 0))],
        out_shape=[jax.ShapeDtypeStruct((n_lat, aw), F32), jax.ShapeDtypeStruct((nkv * ni * g * tq, 1), F32)],
        compiler_params=_cp(("parallel", "parallel")))(q_hat, k_all, v_all)


def _flash_bwd(q_hat, k_all, v_all, do, o, lse, n_lat, dims, tq, tk):
    kvw, aw = dims["kv_w"], dims["attn_w"]
    nkv = kvw // HEAD_DIM
    g = aw // kvw
    gw = g * HEAD_DIM
    n_keys = k_all.shape[0]
    ni, nj = n_lat // tq, n_keys // tk
    dn_nt = (((1,), (1,)), ((), ()))
    dn_tn = (((0,), (0,)), ((), ()))

    def body(q_ref, k_ref, v_ref, do_ref, o_ref, lse_ref, dq_ref, dk_ref, dv_ref):
        i = pl.program_id(1)

        @pl.when(i == 0)
        def _():
            dk_ref[...] = jnp.zeros_like(dk_ref)
            dv_ref[...] = jnp.zeros_like(dv_ref)

        dos = _stack_heads(do_ref[...], g)
        qs = _stack_heads(q_ref[...], g)
        delta = jnp.sum(dos.astype(F32) * _stack_heads(o_ref[...], g), axis=-1, keepdims=True)
        lse_v = lse_ref[...]
        dq = jnp.zeros((g * tq, HEAD_DIM), F32)
        for j in range(nj):
            rows = pl.ds(j * tk, tk)
            kb, vb = k_ref[rows, :], v_ref[rows, :]
            s = lax.dot_general(qs, kb, dn_nt, preferred_element_type=F32)
            p = jnp.exp(s - lse_v)
            dp = lax.dot_general(dos, vb, dn_nt, preferred_element_type=F32)
            ds = (p * (dp - delta)).astype(BF16)
            dv_ref[rows, :] += lax.dot_general(p.astype(BF16), dos, dn_tn, preferred_element_type=F32)
            dk_ref[rows, :] += lax.dot_general(ds, qs, dn_tn, preferred_element_type=F32)
            dq = dq + jnp.dot(ds, kb, preferred_element_type=F32)
        for h in range(g):
            dq_ref[:, h * HEAD_DIM:(h + 1) * HEAD_DIM] = dq[h * tq:(h + 1) * tq]

    qspec = pl.BlockSpec((tq, gw), lambda h, i: (i, h))
    full_k = pl.BlockSpec((n_keys, HEAD_DIM), lambda h, i: (0, h))
    return pl.pallas_call(
        body, name="flash_bwd", grid=(nkv, ni),
        in_specs=[qspec, full_k, full_k, qspec, qspec, pl.BlockSpec((g * tq, 1), lambda h, i: (h * ni + i, 0))],
        out_specs=[qspec, full_k, full_k],
        out_shape=[jax.ShapeDtypeStruct((n_lat, aw), F32), jax.ShapeDtypeStruct((n_keys, kvw), F32),
                   jax.ShapeDtypeStruct((n_keys, kvw), F32)],
        compiler_params=_cp(("parallel", "arbitrary")))(q_hat, k_all, v_all, do, o, lse)


def _shifted_copies(pad_ref, sh_ref, tr):
    rows = tr + 2 * CONV_HALO - SUBLANES
    for r in range(SUBLANES):
        sh_ref[r] = pad_ref[pl.ds(r, rows), :]


def _stencil(sh_ref, w_ref, out_ref, offsets, tr, cc, init_ref=None):
    for c0 in range(0, cc, LANES):
        lanes = pl.ds(c0, LANES)
        wv = [jnp.broadcast_to(w_ref[pl.ds(k, 1), lanes], (STENCIL_ROWS, LANES)) for k in range(len(offsets))]
        if init_ref is None:
            init = jnp.zeros((STENCIL_ROWS, LANES), F32)
        else:
            init = jnp.broadcast_to(init_ref[:, lanes], (STENCIL_ROWS, LANES))

        def block(rb, carry, lanes=lanes, wv=wv, init=init):
            r0 = rb * STENCIL_ROWS
            parts = [init] + [None] * (STENCIL_CHAINS - 1)
            for k, o in enumerate(offsets):
                rows = pl.ds(SUBLANES * (o // SUBLANES) + r0, STENCIL_ROWS)
                term = sh_ref[o % SUBLANES, rows, lanes] * wv[k]
                q = k % STENCIL_CHAINS
                parts[q] = term if parts[q] is None else parts[q] + term
            acc = parts[0]
            for p_ in parts[1:]:
                acc = acc + p_
            out_ref[pl.ds(r0, STENCIL_ROWS), lanes] = acc
            return carry

        for rb in range(tr // STENCIL_ROWS):
            block(rb, 0)


def _stencil_weight_grad(sh_ref, d_ref, d_row0, dw_ref, offsets, tr, cc):
    rows_per = REDUCE_ROWS
    for c0 in range(0, cc, LANES):
        lanes = pl.ds(c0, LANES)

        def block(rb, accs, lanes=lanes):
            r0 = rb * rows_per
            dblk = d_ref[pl.ds(d_row0 + r0, rows_per), lanes]
            out = []
            for k, o in enumerate(offsets):
                prod = dblk * sh_ref[o % SUBLANES, pl.ds(SUBLANES * (o // SUBLANES) + r0, rows_per), lanes]
                part = prod[0:SUBLANES]
                for q in range(1, rows_per // SUBLANES):
                    part = part + prod[q * SUBLANES:(q + 1) * SUBLANES]
                out.append(accs[k] + part)
            return tuple(out)

        zero = jnp.zeros((SUBLANES, LANES), F32)
        accs = tuple(zero for _ in offsets)
        for rb in range(tr // rows_per):
            accs = block(rb, accs)
        for k in range(len(offsets)):
            dw_ref[pl.ds(k, 1), lanes] += jnp.sum(accs[k], axis=0, keepdims=True)


def _halo_maps(tr, n_tiles):
    per = tr // CONV_HALO
    prev = lambda i: jnp.maximum(i * per - 1, 0)
    nxt = lambda i: (i + 1) * per
    return prev, nxt


def _mix_fwd(attn, p0, dw_w, dw_b, ln_g, ln_b, dims, tr):
    n_lat, aw = attn.shape
    cc, cb, cw = dims["conv_ch"], dims["cb"], dims["conv_w"]
    off = dims["off"]
    n_tiles = n_lat // tr
    pad = cw // 2
    na, nc = aw // cb, cc // cb
    prev_map, next_map = _halo_maps(tr, n_tiles)

    def body(*refs):
        it = iter(refs)
        attn_ref = next(it)
        za = [next(it) for _ in range(na)]
        a_c = [next(it) for _ in range(nc)]
        b_c = [next(it) for _ in range(nc)]
        zb = [next(it) for _ in range(nc)]
        a_p = [next(it) for _ in range(nc)]
        b_p = [next(it) for _ in range(nc)]
        a_n = [next(it) for _ in range(nc)]
        b_n = [next(it) for _ in range(nc)]
        w_ref, db_ref, g_ref, bb_ref, mix_ref, yc_ref, ypad, ysh = (next(it) for _ in range(8))
        i = pl.program_id(0)
        ypad[pl.ds(0, CONV_HALO), :] = _cat(a_p) * _sigmoid(_cat(b_p)) * (i > 0).astype(F32)
        ypad[pl.ds(CONV_HALO, tr), :] = _cat(a_c) * _sigmoid(_cat(b_c))
        ypad[pl.ds(CONV_HALO + tr, CONV_HALO), :] = _cat(a_n) * _sigmoid(_cat(b_n)) * (i < n_tiles - 1).astype(F32)
        _shifted_copies(ypad, ysh, tr)
        _stencil(ysh, w_ref, yc_ref, [CONV_HALO - pad + k for k in range(cw)], tr, cc, init_ref=db_ref)
        acc = yc_ref[...]
        mu = jnp.mean(acc, axis=-1, keepdims=True)
        xc = acc - mu
        rs = lax.rsqrt(jnp.mean(xc * xc, axis=-1, keepdims=True) + EPS)
        nv = xc * rs * g_ref[...] + bb_ref[...]
        mix_ref[:, :aw] = (attn_ref[...] * _silu(_cat(za))).astype(BF16)
        mix_ref[:, aw:] = (_silu(nv) * _silu(_cat(zb))).astype(BF16)

    rm = lambda i: i
    in_specs = ([pl.BlockSpec((tr, aw), lambda i: (i, 0))]
                + _col_specs(tr, off["za"], aw, cb, rm) + _col_specs(tr, off["a"], cc, cb, rm)
                + _col_specs(tr, off["b"], cc, cb, rm) + _col_specs(tr, off["zb"], cc, cb, rm)
                + _col_specs(CONV_HALO, off["a"], cc, cb, prev_map) + _col_specs(CONV_HALO, off["b"], cc, cb, prev_map)
                + _col_specs(CONV_HALO, off["a"], cc, cb, next_map) + _col_specs(CONV_HALO, off["b"], cc, cb, next_map)
                + [pl.BlockSpec(dw_w.shape, lambda i: (0, 0)), _vec(cc), _vec(cc), _vec(cc)])
    args = [attn] + [p0] * (na + 7 * nc) + [dw_w, dw_b, ln_g, ln_b]
    return pl.pallas_call(
        body, name="mix_fwd", grid=(n_tiles,), in_specs=in_specs,
        out_specs=[pl.BlockSpec((tr, aw + cc), lambda i: (i, 0)), pl.BlockSpec((tr, cc), lambda i: (i, 0))],
        out_shape=[jax.ShapeDtypeStruct((n_lat, aw + cc), BF16), jax.ShapeDtypeStruct((n_lat, cc), F32)],
        scratch_shapes=[pltpu.VMEM((tr + 2 * CONV_HALO, cc), F32),
                        pltpu.VMEM((SUBLANES, tr + 2 * CONV_HALO - SUBLANES, cc), F32)],
        compiler_params=_cp(("parallel",)))(*args)


def _mix_bwd_pointwise(dmix, attn, p0, yc, ln_g, ln_b, dims, tr):
    n_lat, aw = attn.shape
    cc, cb, off = dims["conv_ch"], dims["cb"], dims["off"]
    na, nc = aw // cb, cc // cb

    def body(*refs):
        it = iter(refs)
        dmix_ref, attn_ref = next(it), next(it)
        za = [next(it) for _ in range(na)]
        zb = [next(it) for _ in range(nc)]
        yc_ref, g_ref, bb_ref = next(it), next(it), next(it)
        dattn_ref, dza_ref, dzb_ref, dyc_ref, dg_ref, dbb_ref, ddb_ref = (next(it) for _ in range(7))
        i = pl.program_id(0)

        @pl.when(i == 0)
        def _():
            dg_ref[...] = jnp.zeros_like(dg_ref)
            dbb_ref[...] = jnp.zeros_like(dbb_ref)
            ddb_ref[...] = jnp.zeros_like(ddb_ref)

        dm = dmix_ref[...].astype(F32)
        dma, dmb = dm[:, :aw], dm[:, aw:]
        zav, zbv = _cat(za), _cat(zb)
        dattn_ref[...] = (dma * _silu(zav)).astype(BF16)
        dza_ref[...] = (dma * attn_ref[...] * _dsilu(zav)).astype(BF16)
        ycv = yc_ref[...]
        mu = jnp.mean(ycv, axis=-1, keepdims=True)
        xc = ycv - mu
        rs = lax.rsqrt(jnp.mean(xc * xc, axis=-1, keepdims=True) + EPS)
        xh = xc * rs
        nv = xh * g_ref[...] + bb_ref[...]
        dzb_ref[...] = (dmb * _silu(nv) * _dsilu(zbv)).astype(BF16)
        dn = dmb * _silu(zbv) * _dsilu(nv)
        dg_ref[...] += jnp.sum(dn * xh, axis=0, keepdims=True)
        dbb_ref[...] += jnp.sum(dn, axis=0, keepdims=True)
        dxh = dn * g_ref[...]
        dyc = rs * (dxh - jnp.mean(dxh, axis=-1, keepdims=True) - xh * jnp.mean(dxh * xh, axis=-1, keepdims=True))
        dyc_ref[...] = dyc
        ddb_ref[...] += jnp.sum(dyc, axis=0, keepdims=True)

    rm = lambda i: i
    row = lambda w: pl.BlockSpec((tr, w), lambda i: (i, 0))
    in_specs = ([row(aw + cc), row(aw)] + _col_specs(tr, off["za"], aw, cb, rm)
                + _col_specs(tr, off["zb"], cc, cb, rm) + [row(cc), _vec(cc), _vec(cc)])
    args = [dmix, attn] + [p0] * (na + nc) + [yc, ln_g, ln_b]
    return pl.pallas_call(
        body, name="mix_bwd_pointwise", grid=(n_lat // tr,), in_specs=in_specs,
        out_specs=[row(aw), row(aw), row(cc), row(cc), _vec(cc), _vec(cc), _vec(cc)],
        out_shape=[jax.ShapeDtypeStruct((n_lat, aw), BF16), jax.ShapeDtypeStruct((n_lat, aw), BF16),
                   jax.ShapeDtypeStruct((n_lat, cc), BF16), jax.ShapeDtypeStruct((n_lat, cc), F32)]
                  + [jax.ShapeDtypeStruct((1, cc), F32)] * 3,
        compiler_params=_cp(("arbitrary",)))(*args)


def _conv_bwd(dyc, p0, dw_w, dims, tr):
    n_lat, cc = dyc.shape
    cb, cw, off = dims["cb"], dims["conv_w"], dims["off"]
    nc = cc // cb
    n_tiles = n_lat // tr
    pad = cw // 2
    prev_map, next_map = _halo_maps(tr, n_tiles)

    def body(*refs):
        it = iter(refs)
        a_c = [next(it) for _ in range(nc)]
        b_c = [next(it) for _ in range(nc)]
        a_p = [next(it) for _ in range(nc)]
        b_p = [next(it) for _ in range(nc)]
        a_n = [next(it) for _ in range(nc)]
        b_n = [next(it) for _ in range(nc)]
        d_c, d_p, d_n, w_ref, dab_ref, dw_ref, ypad, dpad, ysh, dsh, dy_scr = (next(it) for _ in range(11))
        i = pl.program_id(0)

        @pl.when(i == 0)
        def _():
            dw_ref[...] = jnp.zeros_like(dw_ref)

        first, last = (i > 0).astype(F32), (i < n_tiles - 1).astype(F32)
        av, bv = _cat(a_c), _cat(b_c)
        sg = _sigmoid(bv)
        ypad[pl.ds(0, CONV_HALO), :] = _cat(a_p) * _sigmoid(_cat(b_p)) * first
        ypad[pl.ds(CONV_HALO, tr), :] = av * sg
        ypad[pl.ds(CONV_HALO + tr, CONV_HALO), :] = _cat(a_n) * _sigmoid(_cat(b_n)) * last
        dcur = d_c[...]
        dpad[pl.ds(0, CONV_HALO), :] = d_p[...] * first
        dpad[pl.ds(CONV_HALO, tr), :] = dcur
        dpad[pl.ds(CONV_HALO + tr, CONV_HALO), :] = d_n[...] * last
        _shifted_copies(ypad, ysh, tr)
        _shifted_copies(dpad, dsh, tr)
        _stencil(dsh, w_ref, dy_scr, [CONV_HALO + pad - k for k in range(cw)], tr, cc)
        _stencil_weight_grad(ysh, dpad, CONV_HALO, dw_ref, [CONV_HALO - pad + k for k in range(cw)], tr, cc)
        dy = dy_scr[...]
        dab_ref[:, :cc] = (dy * sg).astype(BF16)
        dab_ref[:, cc:] = (dy * av * sg * (1.0 - sg)).astype(BF16)

    rm = lambda i: i
    in_specs = (_col_specs(tr, off["a"], cc, cb, rm) + _col_specs(tr, off["b"], cc, cb, rm)
                + _col_specs(CONV_HALO, off["a"], cc, cb, prev_map) + _col_specs(CONV_HALO, off["b"], cc, cb, prev_map)
                + _col_specs(CONV_HALO, off["a"], cc, cb, next_map) + _col_specs(CONV_HALO, off["b"], cc, cb, next_map)
                + [pl.BlockSpec((tr, cc), lambda i: (i, 0)),
                   pl.BlockSpec((CONV_HALO, cc), lambda i: (prev_map(i), 0)),
                   pl.BlockSpec((CONV_HALO, cc), lambda i: (jnp.minimum(next_map(i), n_lat // CONV_HALO - 1), 0)),
                   pl.BlockSpec(dw_w.shape, lambda i: (0, 0))])
    args = [p0] * (6 * nc) + [dyc, dyc, dyc, dw_w]
    return pl.pallas_call(
        body, name="conv_bwd", grid=(n_tiles,), in_specs=in_specs,
        out_specs=[pl.BlockSpec((tr, 2 * cc), lambda i: (i, 0)), pl.BlockSpec(dw_w.shape, lambda i: (0, 0))],
        out_shape=[jax.ShapeDtypeStruct((n_lat, 2 * cc), BF16), jax.ShapeDtypeStruct(dw_w.shape, F32)],
        scratch_shapes=[pltpu.VMEM((tr + 2 * CONV_HALO, cc), F32), pltpu.VMEM((tr + 2 * CONV_HALO, cc), F32),
                        pltpu.VMEM((SUBLANES, tr + 2 * CONV_HALO - SUBLANES, cc), F32),
                        pltpu.VMEM((SUBLANES, tr + 2 * CONV_HALO - SUBLANES, cc), F32),
                        pltpu.VMEM((tr, cc), F32)],
        compiler_params=_cp(("arbitrary",)))(*args)


def _sgu_parts(pu, pv, ln_g, ln_b):
    u = _gelu(pu)
    v = _gelu(pv)
    mu = jnp.mean(v, axis=-1, keepdims=True)
    xc = v - mu
    rs = lax.rsqrt(jnp.mean(xc * xc, axis=-1, keepdims=True) + EPS)
    xh = xc * rs
    return u, xh, rs, xh * ln_g + ln_b


def _sgu_fwd(p1, ln_g, ln_b, ws, bs_t, tr):
    n_lat, w3 = p1.shape
    w = w3 // 3
    ng, ch = ws.shape[0], ws.shape[1]
    gwid = w // ng
    n_ch = tr // ch

    def body(pu_ref, pv_ref, pg_ref, g_ref, b_ref, ws_ref, bs_ref, o_ref):
        u, _, _, vln = _sgu_parts(pu_ref[...].astype(F32), pv_ref[...].astype(F32), g_ref[...], b_ref[...])
        gate = _silu(pg_ref[...].astype(F32))
        vb = vln.astype(BF16)
        for c in range(n_ch):
            rs_ = slice(c * ch, (c + 1) * ch)
            for gi in range(ng):
                cs_ = slice(gi * gwid, (gi + 1) * gwid)
                mixed = jnp.dot(ws_ref[gi], vb[rs_, cs_], preferred_element_type=F32) + bs_ref[:, gi:gi + 1]
                o_ref[rs_, cs_] = (u[rs_, cs_] * mixed * gate[rs_, cs_]).astype(BF16)

    col = lambda t: pl.BlockSpec((tr, w), lambda i, _t=t: (i, _t))
    return pl.pallas_call(
        body, name="sgu_fwd", grid=(n_lat // tr,),
        in_specs=[col(0), col(1), col(2), _vec(w), _vec(w),
                  pl.BlockSpec(ws.shape, lambda i: (0, 0, 0)), pl.BlockSpec(bs_t.shape, lambda i: (0, 0))],
        out_specs=pl.BlockSpec((tr, w), lambda i: (i, 0)),
        out_shape=jax.ShapeDtypeStruct((n_lat, w), BF16),
        compiler_params=_cp(("parallel",)))(p1, p1, p1, ln_g, ln_b, ws, bs_t)


def _sgu_bwd(dm, p1, ln_g, ln_b, ws, ws_t, bs_t, tr):
    n_lat, w3 = p1.shape
    w = w3 // 3
    ng, ch = ws.shape[0], ws.shape[1]
    gwid = w // ng
    n_ch = tr // ch
    dn_nt = (((1,), (1,)), ((), ()))

    def body(dm_ref, pu_ref, pv_ref, pg_ref, g_ref, b_ref, ws_ref, wst_ref, bs_ref,
             dp_ref, dws_ref, dbs_ref, dg_ref, dbb_ref, dvln_scr):
        i = pl.program_id(0)

        @pl.when(i == 0)
        def _():
            dws_ref[...] = jnp.zeros_like(dws_ref)
            dbs_ref[...] = jnp.zeros_like(dbs_ref)
            dg_ref[...] = jnp.zeros_like(dg_ref)
            dbb_ref[...] = jnp.zeros_like(dbb_ref)

        puv, pvv, pgv = pu_ref[...].astype(F32), pv_ref[...].astype(F32), pg_ref[...].astype(F32)
        u, xh, rs, vln = _sgu_parts(puv, pvv, g_ref[...], b_ref[...])
        gate = _silu(pgv)
        dmv = dm_ref[...].astype(F32)
        vb = vln.astype(BF16)
        dmu = dmv * u
        dbs_cols = [jnp.zeros((ch, 1), F32) for _ in range(ng)]
        for c in range(n_ch):
            rs_ = slice(c * ch, (c + 1) * ch)
            for gi in range(ng):
                cs_ = slice(gi * gwid, (gi + 1) * gwid)
                mixed = jnp.dot(ws_ref[gi], vb[rs_, cs_], preferred_element_type=F32) + bs_ref[:, gi:gi + 1]
                dmixed = dmu[rs_, cs_] * gate[rs_, cs_]
                dmb = dmixed.astype(BF16)
                dp_ref[rs_, gi * gwid:(gi + 1) * gwid] = (
                    dmv[rs_, cs_] * mixed * gate[rs_, cs_] * _dgelu(puv[rs_, cs_])).astype(BF16)
                dp_ref[rs_, 2 * w + gi * gwid:2 * w + (gi + 1) * gwid] = (
                    dmu[rs_, cs_] * mixed * _dsilu(pgv[rs_, cs_])).astype(BF16)
                dvln_scr[rs_, cs_] = jnp.dot(wst_ref[gi], dmb, preferred_element_type=F32)
                dws_ref[gi] += lax.dot_general(dmb, vb[rs_, cs_], dn_nt, preferred_element_type=F32)
                dbs_cols[gi] = dbs_cols[gi] + jnp.sum(dmixed, axis=-1, keepdims=True)
        dbs_ref[...] += jnp.concatenate(dbs_cols, axis=1)
        dvln = dvln_scr[...]
        dg_ref[...] += jnp.sum(dvln * xh, axis=0, keepdims=True)
        dbb_ref[...] += jnp.sum(dvln, axis=0, keepdims=True)
        dxh = dvln * g_ref[...]
        dv = rs * (dxh - jnp.mean(dxh, axis=-1, keepdims=True) - xh * jnp.mean(dxh * xh, axis=-1, keepdims=True))
        dp_ref[:, w:2 * w] = (dv * _dgelu(pvv)).astype(BF16)

    col = lambda t: pl.BlockSpec((tr, w), lambda i, _t=t: (i, _t))
    return pl.pallas_call(
        body, name="sgu_bwd", grid=(n_lat // tr,),
        in_specs=[pl.BlockSpec((tr, w), lambda i: (i, 0)), col(0), col(1), col(2), _vec(w), _vec(w),
                  pl.BlockSpec(ws.shape, lambda i: (0, 0, 0)), pl.BlockSpec(ws.shape, lambda i: (0, 0, 0)),
                  pl.BlockSpec(bs_t.shape, lambda i: (0, 0))],
        out_specs=[pl.BlockSpec((tr, w3), lambda i: (i, 0)), pl.BlockSpec(ws.shape, lambda i: (0, 0, 0)),
                   pl.BlockSpec(bs_t.shape, lambda i: (0, 0)), _vec(w), _vec(w)],
        out_shape=[jax.ShapeDtypeStruct((n_lat, w3), BF16), jax.ShapeDtypeStruct(ws.shape, F32),
                   jax.ShapeDtypeStruct(bs_t.shape, F32), jax.ShapeDtypeStruct((1, w), F32),
                   jax.ShapeDtypeStruct((1, w), F32)],
        scratch_shapes=[pltpu.VMEM((tr, w), F32)],
        compiler_params=_cp(("arbitrary",)))(dm, p1, p1, p1, ln_g, ln_b, ws, ws_t, bs_t)


def _final_loss(x2, target, final_g, o_prev, gate_prev, tr):
    n_lat, d = x2.shape

    def body(x_ref, t_ref, g_ref, o_ref, gp_ref, dx_ref, do_ref, ls_ref, dg_ref, dgp_ref):
        i = pl.program_id(0)

        @pl.when(i == 0)
        def _():
            ls_ref[...] = jnp.zeros_like(ls_ref)
            dg_ref[...] = jnp.zeros_like(dg_ref)
            dgp_ref[...] = jnp.zeros_like(dgp_ref)

        xv = x_ref[...]
        gv = g_ref[...]
        rs = lax.rsqrt(jnp.mean(xv * xv, axis=-1, keepdims=True) + EPS)
        xn = xv * rs
        err = xn * gv - t_ref[...]
        ls_ref[...] += jnp.sum(err * err, axis=0, keepdims=True)
        dy = err * (1.0 / d)
        dg_ref[...] += jnp.sum(dy * xn, axis=0, keepdims=True)
        gy = dy * gv
        dx = rs * (gy - xn * jnp.mean(gy * xn, axis=-1, keepdims=True))
        dx_ref[...] = dx
        do_ref[...] = (gp_ref[...] * dx).astype(BF16)
        dgp_ref[...] += jnp.sum(dx * o_ref[...].astype(F32), axis=0, keepdims=True)

    row = pl.BlockSpec((tr, d), lambda i: (i, 0))
    return pl.pallas_call(
        body, name="final_loss", grid=(n_lat // tr,), in_specs=[row, row, _vec(d), row, _vec(d)],
        out_specs=[row, row, _vec(d), _vec(d), _vec(d)],
        out_shape=[jax.ShapeDtypeStruct((n_lat, d), F32), jax.ShapeDtypeStruct((n_lat, d), BF16)]
                  + [jax.ShapeDtypeStruct((1, d), F32)] * 3,
        compiler_params=_cp(("arbitrary",)))(x2, target, final_g, o_prev, gate_prev)


def _pack(arrays):
    flat = jnp.concatenate([a.reshape(-1).astype(F32) for a in arrays])
    n = flat.shape[0]
    rows = -(-n // LANES)
    rows = -(-rows // PACK_ROWS) * PACK_ROWS
    return jnp.pad(flat, (0, rows * LANES - n)).reshape(rows, LANES)


def _unpack(buf, shapes):
    flat = buf.reshape(buf.shape[:-2] + (-1,))
    out, pos = [], 0
    for shp in shapes:
        n = math.prod(shp)
        out.append(flat[..., pos:pos + n].reshape(buf.shape[:-2] + tuple(shp)))
        pos += n
    return out


def _rope_tables(n_lat, n_ctx):
    rows = n_lat // GRID_W
    row = jnp.repeat(jnp.arange(rows, dtype=F32), GRID_W)
    col = jnp.tile(jnp.arange(GRID_W, dtype=F32), rows)
    n_freq, axis_dim = HEAD_DIM // 4, HEAD_DIM // 2
    inv = jnp.power(ROPE_THETA, jnp.arange(n_freq, dtype=F32) * (-2.0 / axis_dim))
    ang = jnp.concatenate([row[:, None] * inv, col[:, None] * inv], axis=-1)
    cos, sin = jnp.cos(ang), jnp.sin(ang)
    cos_i = jnp.repeat(cos, 2, axis=-1)
    sin_s = jnp.stack([-sin, sin], axis=-1).reshape(n_lat, HEAD_DIM)
    cos_i = jnp.concatenate([cos_i, jnp.ones((n_ctx, HEAD_DIM), F32)], axis=0)
    sin_s = jnp.concatenate([sin_s, jnp.zeros((n_ctx, HEAD_DIM), F32)], axis=0)
    return cos_i, sin_s


def kernel(x, c, ctx, c_ctx, ada_w, ada_b, norm_g, ev_w_in, ev_q_norm, ev_k_norm, ev_dw_w, ev_dw_b, ev_ln_g, ev_ln_b, ev_w_out, od_w_in, od_ln_g, od_ln_b, od_ws, od_bs, od_w_out, final_g, loss_target, m_c_ctx, m_ada_w, m_ada_b, m_norm_g, m_ev_w_in, m_ev_q_norm, m_ev_k_norm, m_ev_dw_w, m_ev_dw_b, m_ev_ln_g, m_ev_ln_b, m_ev_w_out, m_od_w_in, m_od_ln_g, m_od_ln_b, m_od_ws, m_od_bs, m_od_w_out, m_final_g, v_c_ctx, v_ada_w, v_ada_b, v_norm_g, v_ev_w_in, v_ev_q_norm, v_ev_k_norm, v_ev_dw_w, v_ev_dw_b, v_ev_ln_g, v_ev_ln_b, v_ev_w_out, v_od_w_in, v_od_ln_g, v_od_ln_b, v_od_ws, v_od_bs, v_od_w_out, v_final_g):
    weights = dict(c_ctx=c_ctx, ada_w=ada_w, ada_b=ada_b, norm_g=norm_g, ev_w_in=ev_w_in, ev_q_norm=ev_q_norm,
                   ev_k_norm=ev_k_norm, ev_dw_w=ev_dw_w, ev_dw_b=ev_dw_b, ev_ln_g=ev_ln_g, ev_ln_b=ev_ln_b,
                   ev_w_out=ev_w_out, od_w_in=od_w_in, od_ln_g=od_ln_g, od_ln_b=od_ln_b, od_ws=od_ws, od_bs=od_bs,
                   od_w_out=od_w_out, final_g=final_g)
    mom_m = dict(c_ctx=m_c_ctx, ada_w=m_ada_w, ada_b=m_ada_b, norm_g=m_norm_g, ev_w_in=m_ev_w_in,
                 ev_q_norm=m_ev_q_norm, ev_k_norm=m_ev_k_norm, ev_dw_w=m_ev_dw_w, ev_dw_b=m_ev_dw_b,
                 ev_ln_g=m_ev_ln_g, ev_ln_b=m_ev_ln_b, ev_w_out=m_ev_w_out, od_w_in=m_od_w_in, od_ln_g=m_od_ln_g,
                 od_ln_b=m_od_ln_b, od_ws=m_od_ws, od_bs=m_od_bs, od_w_out=m_od_w_out, final_g=m_final_g)
    mom_v = dict(c_ctx=v_c_ctx, ada_w=v_ada_w, ada_b=v_ada_b, norm_g=v_norm_g, ev_w_in=v_ev_w_in,
                 ev_q_norm=v_ev_q_norm, ev_k_norm=v_ev_k_norm, ev_dw_w=v_ev_dw_w, ev_dw_b=v_ev_dw_b,
                 ev_ln_g=v_ev_ln_g, ev_ln_b=v_ev_ln_b, ev_w_out=v_ev_w_out, od_w_in=v_od_w_in, od_ln_g=v_od_ln_g,
                 od_ln_b=v_od_ln_b, od_ws=v_od_ws, od_bs=v_od_bs, od_w_out=v_od_w_out, final_g=v_final_g)
    order = list(weights)

    _, n_lat, d = x.shape
    n_ctx = ctx.shape[1]
    n_ext = n_lat + n_ctx
    ev_in = ev_w_in.shape[-1] * N_CHIP
    ev_mix = ev_w_out.shape[1] * N_CHIP
    conv_ch = ev_dw_b.shape[-1]
    conv_w = ev_dw_w.shape[1]
    attn_w = ev_mix - conv_ch
    kv_w = N_KV_HEADS * HEAD_DIM
    assert ev_in == 2 * kv_w + 2 * attn_w + 3 * conv_ch and conv_w // 2 < CONV_HALO
    sgu_w = od_w_out.shape[1] * N_CHIP
    wa = ada_w.shape[-1]
    cb = math.gcd(2 * kv_w, attn_w, conv_ch)
    off = dict(k=0, v=kv_w, q=2 * kv_w, za=2 * kv_w + attn_w, a=2 * kv_w + 2 * attn_w,
               b=2 * kv_w + 2 * attn_w + conv_ch, zb=2 * kv_w + 2 * attn_w + 2 * conv_ch)
    dims = dict(kv_w=kv_w, attn_w=attn_w, conv_ch=conv_ch, conv_w=conv_w, cb=cb, off=off)
    tr = 256 if (n_lat % 256 == 0 and n_ctx % 256 == 0) else 128

    mx, my, mc = lax.axis_index("x"), lax.axis_index("y"), lax.axis_index("c")
    me = 4 * mx + 2 * my + mc
    chip = 2 * mx + my

    x2d, tgt2d, ctx2d = x[0], loss_target[0], ctx[0]
    ev_dw_w_l = ev_dw_w[0]
    dwc = ev_dw_w_l.shape[1]
    lnc = od_ln_g.shape[1]

    g_c = _allgather_small(jnp.broadcast_to(c, (8, d)), "gather_cond")[:, 0, :]
    c_rows = jnp.concatenate([g_c, c_ctx[None, :], jnp.zeros((MOD_ROWS - N_DEV - 1, d), F32)], axis=0)
    c_rows_t = c_rows.T
    ada_b_shard = lax.dynamic_slice_in_dim(ada_b, chip * wa, wa, axis=1)[:, None, :]
    mod_part = _mod_fwd(c_rows_t, ada_w, ada_b_shard)
    part_shapes = [(2, MOD_ROWS, wa), (conv_w, dwc), (1, lnc), (1, lnc)]
    g_parts = _allgather_small(_pack([mod_part, ev_dw_w_l, od_ln_g, od_ln_b]), "gather_mod")
    per_chip = [_unpack(g_parts[2 * s], part_shapes) for s in range(N_CHIP)]
    mod_all = jnp.concatenate([p[0] for p in per_chip], axis=-1)
    dw_w_full = jnp.concatenate([p[1] for p in per_chip], axis=-1)
    od_ln_g_full = jnp.concatenate([p[2] for p in per_chip], axis=-1)
    od_ln_b_full = jnp.concatenate([p[3] for p in per_chip], axis=-1)
    dw_w_pad = jnp.pad(dw_w_full, ((0, 2 * CONV_HALO - conv_w), (0, 0)))
    mod_me = lax.dynamic_slice_in_dim(mod_all, me, 1, axis=1)
    shift0, scale0, gate0 = mod_me[0, :, :d], mod_me[0, :, d:2 * d], mod_me[0, :, 2 * d:]
    shift1, scale1, gate1 = mod_me[1, :, :d], mod_me[1, :, d:2 * d], mod_me[1, :, 2 * d:]
    shift_c, scale_c = mod_all[0, N_DEV:N_DEV + 1, :d], mod_all[0, N_DEV:N_DEV + 1, d:2 * d]
    g0, g1 = norm_g[0:1], norm_g[1:2]

    lay = dict(ev_w_in=_Sharded((d, ev_in), True), ev_w_out=_Sharded((ev_mix, d), False),
               od_w_in=_Sharded((d, 3 * sgu_w), True), od_w_out=_Sharded((sgu_w, d), False))
    big = list(lay)
    chip_arr = jnp.reshape(chip, (1,)).astype(jnp.int32)
    core_arr = jnp.reshape(mc, (1,)).astype(jnp.int32)
    own = {n: _cast_into_full(weights[n][0], lay[n], chip_arr, f"cast_{n}") for n in big}
    w_full = {"ev_w_in": _run_comm(_gather_comm([own["ev_w_in"]], [lay["ev_w_in"]]), "gather_ev_w_in")[0]}
    later = big[1:]
    gather_later = _gather_comm([own[n] for n in later], [lay[n] for n in later])

    def reduce_start(g, n):
        theirs = _pair_exchange([g], [lay[n]], f"pair_exchange_{n}")[0]
        psum = _pair_sum(g, theirs, lay[n], core_arr, f"pair_sum_{n}")
        return psum, _scatter_comm([psum], [lay[n]])

    h0 = _adaln_fwd(x2d, g0, shift0, scale0, tr, "adaln0_fwd")
    hc = _adaln_fwd(ctx2d, g0, shift_c, scale_c, tr, "adaln0_ctx_fwd")
    h0e = jnp.concatenate([h0, hc], axis=0)
    tm_e = _pick(n_ext, (1408, 768, 640, 512, 256, 128))
    tk_e = _pick(n_ext, (768, 640, 512, 256, 128))
    tm_l = _pick(n_lat, (1024, 512, 256, 128))
    p0, *gathered = _mm(h0e, w_full["ev_w_in"], name="mm_ev_in", tm=tm_e, tn=_pick(ev_in, (512, 256, 128)), tk=d,
                        out_dtype=BF16, comm=gather_later)
    w_full.update(zip(later, gathered))
    cos_i, sin_s = _rope_tables(n_lat, n_ctx)
    q_hat, k_all, v_all = _qk_prep(p0, cos_i, sin_s, ev_q_norm, ev_k_norm, dims, tr)
    tq = _pick(n_lat, (256, 128))
    tkk = _pick(n_ext, (768, 640, 512, 256, 128))
    attn, lse = _flash_fwd(q_hat, k_all, v_all, n_lat, dims, tq, tkk)
    mix, yc = _mix_fwd(attn, p0, dw_w_pad, ev_dw_b, ev_ln_g, ev_ln_b, dims, tr)
    o0, x1 = _mm(mix, w_full["ev_w_out"], name="mm_ev_out", tm=tm_l, tn=_pick(d, (1024, 512, 256)),
                 tk=_pick(ev_mix, (512, 256)), out_dtype=BF16, res=x2d, gate=gate0)

    h1 = _adaln_fwd(x1, g1, shift1, scale1, tr, "adaln1_fwd")
    p1 = _mm(h1, w_full["od_w_in"], name="mm_od_in", tm=tm_l, tn=_pick(3 * sgu_w, (512, 256, 128)), tk=d,
             out_dtype=BF16)
    ws_b = od_ws[0].astype(BF16)
    ws_t_b = jnp.swapaxes(od_ws[0], 1, 2).astype(BF16)
    bs_t = od_bs[0].T
    m1 = _sgu_fwd(p1, od_ln_g_full, od_ln_b_full, ws_b, bs_t, tr)
    o1, x2 = _mm(m1, w_full["od_w_out"], name="mm_od_out", tm=tm_l, tn=_pick(d, (1024, 512, 256)),
                 tk=_pick(sgu_w, (512, 256)), out_dtype=BF16, res=x1, gate=gate1)

    dx2, do1, loss_cols, d_final_g, dgate1 = _final_loss(x2, tgt2d, final_g[None, :], o1, gate1, tr)
    loss = lax.psum(0.5 / d * jnp.sum(loss_cols), ("x", "y", "c"))

    tk_l = _pick(n_lat, (1024, 512, 256, 128))
    psums, slots = {}, {}
    tk_nt = (1408, 1024, 768, 512, 256, 128)
    g_od_w_out = _mm(m1, do1, name="mm_od_out_dw", ta=True, tm=_pick(sgu_w, (1024, 512, 256)),
                     tn=_pick(d, (1024, 512, 256)), tk=tk_l)
    psums["od_w_out"], sc = reduce_start(g_od_w_out, "od_w_out")
    dm1, slots["od_w_out"] = _mm(do1, w_full["od_w_out"], name="mm_od_out_dx", tb=True, tm=tm_l,
                                 tn=_pick(sgu_w, (1024, 512, 256)), tk=_pick(d, tk_nt), out_dtype=BF16, comm=sc)
    dp1, d_ws, d_bs_t, d_od_ln_g, d_od_ln_b = _sgu_bwd(dm1, p1, od_ln_g_full, od_ln_b_full, ws_b, ws_t_b, bs_t, tr)
    g_od_w_in = _mm(h1, dp1, name="mm_od_in_dw", ta=True, tm=_pick(d, (1024, 512, 256)),
                    tn=_pick(3 * sgu_w, (1536, 768, 512, 384, 256, 128)), tk=tk_l)
    psums["od_w_in"], sc = reduce_start(g_od_w_in, "od_w_in")
    dh1, slots["od_w_in"] = _mm(dp1, w_full["od_w_in"], name="mm_od_in_dx", tb=True, tm=tm_l,
                                tn=_pick(d, (1024, 512, 256)), tk=_pick(3 * sgu_w, tk_nt), comm=sc)
    zero_d = jnp.zeros((1, d), F32)
    b1 = _adaln_bwd(x1, dh1, 0, g1, scale1, zero_d, tr, "adaln1_bwd", dres=dx2, o_prev=o0, gate_prev=gate0)
    dx1, do0, dgate0 = b1["dx"], b1["do_prev"], b1["dgate_prev"]

    g_ev_w_out = _mm(mix, do0, name="mm_ev_out_dw", ta=True, tm=_pick(ev_mix, (1024, 512, 256)),
                     tn=_pick(d, (1024, 512, 256)), tk=tk_l)
    psums["ev_w_out"], sc = reduce_start(g_ev_w_out, "ev_w_out")
    dmix, slots["ev_w_out"] = _mm(do0, w_full["ev_w_out"], name="mm_ev_out_dx", tb=True, tm=tm_l,
                                  tn=_pick(ev_mix, (1024, 512, 256)), tk=_pick(d, tk_nt), out_dtype=BF16, comm=sc)
    dattn, dza, dzb, dyc, d_ev_ln_g, d_ev_ln_b, d_dw_b = _mix_bwd_pointwise(
        dmix, attn, p0, yc, ev_ln_g, ev_ln_b, dims, tr)
    dab, d_dw_w_pad = _conv_bwd(dyc, p0, dw_w_pad, dims, tr)
    dq_hat, dk_hat, dv_all = _flash_bwd(q_hat, k_all, v_all, dattn, attn, lse, n_lat, dims, tq, tkk)
    dkvq, d_q_norm, d_k_norm = _qk_prep_bwd(dq_hat, dk_hat, dv_all, p0, cos_i, sin_s, ev_q_norm, ev_k_norm,
                                            dims, tr, n_lat)
    rest = jnp.pad(jnp.concatenate([dza, dab, dzb], axis=1), ((0, n_ctx), (0, 0)))
    dp0 = jnp.concatenate([dkvq, rest], axis=1)
    g_ev_w_in = _mm(h0e, dp0, name="mm_ev_in_dw", ta=True, tm=_pick(d, (1024, 512, 256)),
                    tn=_pick(ev_in, (1408, 768, 512, 256, 128)), tk=tk_e)
    psums["ev_w_in"], sc = reduce_start(g_ev_w_in, "ev_w_in")
    dh0, slots["ev_w_in"] = _mm(dp0, w_full["ev_w_in"], name="mm_ev_in_dx", tb=True, tm=tm_e,
                                tn=_pick(d, (1024, 512, 256)), tk=_pick(ev_in, tk_nt), comm=sc)
    bc = _adaln_bwd(ctx2d, dh0, n_lat, g0, scale_c, zero_d, tr, "adaln0_ctx_bwd")
    b0 = _adaln_bwd(x2d, dh0, 0, g0, scale0, bc["dg"], tr, "adaln0_bwd", dres=dx1)
    grad_x = b0["dx"]

    zeros_d = jnp.zeros((1, d), F32)
    dmod0 = jnp.concatenate([b0["dshift"], b0["dscale"], dgate0], axis=1)
    dmod1 = jnp.concatenate([b1["dshift"], b1["dscale"], dgate1], axis=1)
    dmodc = jnp.concatenate([bc["dshift"], bc["dscale"], zeros_d], axis=1)
    small = [dmod0, dmod1, dmodc, b0["dg"], b1["dg"], d_q_norm, d_k_norm, d_dw_w_pad[:conv_w], d_dw_b,
             d_ev_ln_g, d_ev_ln_b, d_od_ln_g, d_od_ln_b, d_ws, d_bs_t.T, d_final_g]
    small_shapes = [a.shape for a in small]
    g_small = _allgather_small(_pack(small), "gather_small_grads")
    tot = _unpack(_sum_slots(g_small, "sum_small_grads"), small_shapes)
    (t_dmod0, t_dmod1, t_dmodc, t_g0, t_g1, t_qn, t_kn, t_dw_w, t_dw_b, t_eln_g, t_eln_b, t_oln_g, t_oln_b,
     t_ws, t_bs, t_fg) = tot
    rows_dmod = _unpack(g_small, small_shapes[:2])
    dmod0_rows, dmod1_rows = rows_dmod[0][:, 0, :], rows_dmod[1][:, 0, :]
    pad_rows = jnp.zeros((MOD_ROWS - N_DEV - 1, 3 * d), F32)
    dm_l0 = jnp.concatenate([dmod0_rows, t_dmodc, pad_rows], axis=0)
    dm_l1 = jnp.concatenate([dmod1_rows, jnp.zeros((MOD_ROWS - N_DEV, 3 * d), F32)], axis=0)
    dm_shard = lax.dynamic_slice_in_dim(jnp.stack([dm_l0, dm_l1]), chip * wa, wa, axis=2)
    g_ada_w, dsc = _mod_bwd(c_rows_t, dm_shard, ada_w)
    g_dsc = _allgather_small(_pack([dsc[0]]), "gather_cctx")
    g_c_ctx = _cctx_grad(g_dsc, _pack([c_ctx])).reshape(-1)[:d]
    g_ada_b = jnp.stack([t_dmod0[0] + t_dmodc[0], t_dmod1[0]])

    lays = [lay[n] for n in big]
    halves = [_chip_sum(psums[n], slots[n], lay[n], chip_arr, core_arr, f"chip_sum_{n}") for n in big]
    g_big = dict(zip(big, _pair_share(halves, lays)))

    grads = dict(
        c_ctx=g_c_ctx, ada_w=g_ada_w, ada_b=g_ada_b, norm_g=jnp.concatenate([t_g0, t_g1], axis=0),
        ev_w_in=g_big["ev_w_in"][None], ev_q_norm=t_qn, ev_k_norm=t_kn,
        ev_dw_w=lax.dynamic_slice_in_dim(t_dw_w, chip * dwc, dwc, axis=1)[None], ev_dw_b=t_dw_b,
        ev_ln_g=t_eln_g, ev_ln_b=t_eln_b, ev_w_out=g_big["ev_w_out"][None], od_w_in=g_big["od_w_in"][None],
        od_ln_g=lax.dynamic_slice_in_dim(t_oln_g, chip * lnc, lnc, axis=1),
        od_ln_b=lax.dynamic_slice_in_dim(t_oln_b, chip * lnc, lnc, axis=1),
        od_ws=t_ws[None], od_bs=t_bs[None], od_w_out=g_big["od_w_out"][None], final_g=t_fg[0])
    grads = {n: grads[n].reshape(weights[n].shape) for n in order}

    delta, new_m, new_v = {}, {}, {}
    large = ("ada_w", "ev_w_in", "ev_w_out", "od_w_in", "od_w_out")
    for n in large:
        shp = weights[n].shape
        as2d = lambda a: a.reshape(-1, shp[-1])
        dl, nm, nv = _adamw(as2d(weights[n]), as2d(grads[n]), as2d(mom_m[n]), as2d(mom_v[n]), f"adamw_{n}")
        delta[n], new_m[n], new_v[n] = dl.reshape(shp), nm.reshape(shp), nv.reshape(shp)
    rest_names = [n for n in order if n not in large]
    rest_shapes = [weights[n].shape for n in rest_names]
    dl, nm, nv = _adamw(_pack([weights[n] for n in rest_names]), _pack([grads[n] for n in rest_names]),
                        _pack([mom_m[n] for n in rest_names]), _pack([mom_v[n] for n in rest_names]), "adamw_small")
    for n, a, b_, c_ in zip(rest_names, _unpack(dl, rest_shapes), _unpack(nm, rest_shapes), _unpack(nv, rest_shapes)):
        delta[n], new_m[n], new_v[n] = a, b_, c_

    return (loss, grad_x[None], *[grads[n] for n in order], *[delta[n] for n in order],
            *[new_m[n] for n in order], *[new_v[n] for n in order])
```

```python
import math

import jax
import jax.numpy as jnp
from jax import lax
from jax.experimental import pallas as pl
from jax.experimental.pallas import tpu as pltpu

F32 = jnp.float32
BF16 = jnp.bfloat16
EPS = 1e-6
GRID_W = 64
ROPE_THETA = 10000.0
HEAD_DIM = 128
N_KV_HEADS = 2
CONV_HALO = 16
LANES = 128
SUBLANES = 8
STENCIL_ROWS = 32
STENCIL_CHAINS = 4
REDUCE_ROWS = 32
N_DEV = 8
N_CHIP = 4
MOD_ROWS = 16
PACK_ROWS = 64
ADAM_LR, ADAM_B1, ADAM_B2, ADAM_EPS, ADAM_WD, ADAM_STEP = 0.001, 0.9, 0.999, 1e-08, 0.01, 10
VMEM_LIMIT = 56 * 1024 * 1024
MESH = pl.DeviceIdType.MESH
ANY = pl.BlockSpec(memory_space=pl.ANY)
VMEM_SPEC = pl.BlockSpec(memory_space=pltpu.VMEM)
CHIP_DELTAS = ((1, 0), (0, 1), (1, 1))


def _cp(sem=None):
    return pltpu.CompilerParams(dimension_semantics=sem, vmem_limit_bytes=VMEM_LIMIT)


def _pick(n, cands):
    for c in cands:
        if n % c == 0:
            return c
    raise ValueError(f"no tile for {n} in {cands}")


def _sigmoid(x):
    return 1.0 / (1.0 + jnp.exp(-x))


def _silu(x):
    return x * _sigmoid(x)


def _silu_and_grad(x):
    s = _sigmoid(x)
    y = x * s
    return y, s + y * (1.0 - s)


def _dsilu(x):
    return _silu_and_grad(x)[1]


_GELU_C = math.sqrt(2.0 / math.pi)
_GELU_A = 0.044715


def _gelu_and_grad(x):
    x2 = x * x
    t = jnp.tanh(x * (_GELU_C + (_GELU_C * _GELU_A) * x2))
    h = 0.5 + 0.5 * t
    return x * h, h + x * (1.0 - t * t) * (0.5 * _GELU_C + (1.5 * _GELU_C * _GELU_A) * x2)


def _gelu(x):
    x2 = x * x
    return x * (0.5 + 0.5 * jnp.tanh(x * (_GELU_C + (_GELU_C * _GELU_A) * x2)))


def _vec(d):
    return pl.BlockSpec((1, d), lambda *_: (0, 0))


def _cat(refs):
    parts = [r[...].astype(F32) for r in refs]
    return parts[0] if len(parts) == 1 else jnp.concatenate(parts, axis=1)


def _col_specs(rows, off, width, cb, row_map):
    assert off % cb == 0 and width % cb == 0
    return [pl.BlockSpec((rows, cb), (lambda *g, _c=off // cb + t: (row_map(*g), _c))) for t in range(width // cb)]


def _my_pos():
    return lax.axis_index("x"), lax.axis_index("y"), lax.axis_index("c")


def _allgather_small(x, name):
    r, c = x.shape

    def body(x_ref, out_ref, send_sems, recv_sems, local_sem):
        mx, my, mc = _my_pos()
        me = 4 * mx + 2 * my + mc
        mine = pltpu.make_async_copy(x_ref, out_ref.at[me], local_sem)
        mine.start()
        deltas = [(dx, dy, dc) for dx in (0, 1) for dy in (0, 1) for dc in (0, 1) if (dx, dy, dc) != (0, 0, 0)]
        sends = []
        for k, (dx, dy, dc) in enumerate(deltas):
            px, py, pc = (mx + dx) % 2, (my + dy) % 2, (mc + dc) % 2
            cp = pltpu.make_async_remote_copy(
                src_ref=x_ref, dst_ref=out_ref.at[me], send_sem=send_sems.at[k], recv_sem=recv_sems.at[k],
                device_id=(px, py, pc), device_id_type=MESH)
            cp.start()
            sends.append(cp)
        for k, (dx, dy, dc) in enumerate(deltas):
            px, py, pc = (mx + dx) % 2, (my + dy) % 2, (mc + dc) % 2
            peer = 4 * px + 2 * py + pc
            pltpu.make_async_remote_copy(
                src_ref=x_ref, dst_ref=out_ref.at[peer], send_sem=send_sems.at[k], recv_sem=recv_sems.at[k],
                device_id=(px, py, pc), device_id_type=MESH).wait_recv()
        for cp in sends:
            cp.wait_send()
        mine.wait()

    return pl.pallas_call(
        body, name=name,
        out_shape=jax.ShapeDtypeStruct((N_DEV, r, c), x.dtype),
        in_specs=[VMEM_SPEC], out_specs=VMEM_SPEC,
        scratch_shapes=[pltpu.SemaphoreType.DMA((N_DEV - 1,)), pltpu.SemaphoreType.DMA((N_DEV - 1,)),
                        pltpu.SemaphoreType.DMA],
        compiler_params=pltpu.CompilerParams(vmem_limit_bytes=VMEM_LIMIT),
    )(x)


class _Sharded:
    def __init__(self, full_shape, by_cols):
        self.full = full_shape
        self.by_cols = by_cols
        rows, cols = full_shape
        if by_cols:
            self.shard, self.half, self.halves = (rows, cols // N_CHIP), (rows // 2, cols // N_CHIP), (rows // 2, cols)
        else:
            self.shard, self.half, self.halves = (rows // N_CHIP, cols), (rows // N_CHIP, cols // 2), (rows, cols // 2)

    def region(self, ref, s, h):
        if self.by_cols:
            return ref.at[pl.ds(h * self.half[0], self.half[0]), pl.ds(s * self.shard[1], self.shard[1])]
        return ref.at[pl.ds(s * self.shard[0], self.shard[0]), pl.ds(h * self.half[1], self.half[1])]

    def halves_of_full(self, ref, h):
        if self.by_cols:
            return ref.at[pl.ds(h * self.halves[0], self.halves[0]), :]
        return ref.at[:, pl.ds(h * self.halves[1], self.halves[1])]

    def region_in_halves(self, ref, s):
        if self.by_cols:
            return ref.at[:, pl.ds(s * self.shard[1], self.shard[1])]
        return ref.at[pl.ds(s * self.shard[0], self.shard[0]), :]

    def half_of_shard(self, ref, h):
        if self.by_cols:
            return ref.at[pl.ds(h * self.half[0], self.half[0]), :]
        return ref.at[:, pl.ds(h * self.half[1], self.half[1])]


def _row_tile(rows, row_bytes):
    for t in (512, 256, 128, 64, 32, 16):
        if rows % t == 0 and t * row_bytes <= 2 * 1024 * 1024:
            return t
    return 16


def _cast_into_full(w_shard, lay, chip_arr, name):
    r, c = lay.shard
    tr = _row_tile(r, c * 4)
    nt = r // tr

    def body(chip_ref, w_ref, o_ref):
        o_ref[...] = w_ref[...].astype(BF16)

    if lay.by_cols:
        out_map = lambda i, chip_ref: (i, chip_ref[0])
    else:
        out_map = lambda i, chip_ref: (chip_ref[0] * nt + i, 0)
    return pl.pallas_call(
        body, name=name,
        grid_spec=pltpu.PrefetchScalarGridSpec(
            num_scalar_prefetch=1, grid=(nt,),
            in_specs=[pl.BlockSpec((tr, c), lambda i, chip_ref: (i, 0))],
            out_specs=pl.BlockSpec((tr, c), out_map)),
        out_shape=jax.ShapeDtypeStruct(lay.full, BF16), compiler_params=_cp(("parallel",)))(chip_arr, w_shard)


class _Carried:
    def __init__(self, ins, out_shapes, aliases, sem_shape, start, finish):
        self.ins, self.out_shapes, self.aliases, self.sem_shape = list(ins), list(out_shapes), dict(aliases), sem_shape
        self.start, self.finish = start, finish

    def scratch(self):
        return [pltpu.SemaphoreType.DMA(self.sem_shape), pltpu.SemaphoreType.DMA(self.sem_shape)]

    def split(self, in_refs, out_refs, scratch_refs):
        ni, no = len(self.ins), len(self.out_shapes)
        return in_refs[len(in_refs) - ni:], out_refs[len(out_refs) - no:], scratch_refs[-2], scratch_refs[-1]


def _run_comm(comm, name):
    ni, no = len(comm.ins), len(comm.out_shapes)

    def body(*refs):
        ins, outs, send_sems, recv_sems = refs[:ni], refs[ni:ni + no], refs[ni + no], refs[ni + no + 1]
        comm.start(ins, outs, send_sems, recv_sems)
        comm.finish(ins, outs, send_sems, recv_sems)

    return pl.pallas_call(body, name=name, out_shape=comm.out_shapes, in_specs=[ANY] * ni, out_specs=[ANY] * no,
                          input_output_aliases=comm.aliases, scratch_shapes=comm.scratch())(*comm.ins)


def _gather_comm(fulls, layouts):
    n = len(fulls)

    def ici(ins, outs, send_sems, recv_sems, a, j, landed=False):
        mx, my, mc = _my_pos()
        dx, dy = CHIP_DELTAS[j]
        px, py = (mx + dx) % 2, (my + dy) % 2
        src_chip = 2 * px + py if landed else 2 * mx + my
        return pltpu.make_async_remote_copy(
            src_ref=layouts[a].region(ins[a], src_chip, mc), dst_ref=layouts[a].region(outs[a], src_chip, mc),
            send_sem=send_sems.at[a, j], recv_sem=recv_sems.at[a, j], device_id=(px, py, mc), device_id_type=MESH)

    def d2d(ins, outs, send_sems, recv_sems, a, j, landed=False):
        mx, my, mc = _my_pos()
        dx, dy = CHIP_DELTAS[j]
        other = 2 * ((mx + dx) % 2) + (my + dy) % 2
        half = 1 - mc if landed else mc
        region = layouts[a].region(outs[a], other, half)
        return pltpu.make_async_remote_copy(
            src_ref=region, dst_ref=region, send_sem=send_sems.at[a, 3 + j], recv_sem=recv_sems.at[a, 3 + j],
            device_id=(mx, my, 1 - mc), device_id_type=MESH)

    pairs = [(a, j) for a in range(n) for j in range(3)]

    def start(*r):
        for a, j in pairs:
            ici(*r, a, j).start()

    def finish(*r):
        for a, j in pairs:
            ici(*r, a, j, landed=True).wait_recv()
            d2d(*r, a, j).start()
        for a, j in pairs:
            d2d(*r, a, j, landed=True).wait_recv()
        for a, j in pairs:
            ici(*r, a, j).wait_send()
            d2d(*r, a, j).wait_send()

    return _Carried(fulls, [jax.ShapeDtypeStruct(lay.full, BF16) for lay in layouts], {a: a for a in range(n)},
                    (n, 6), start, finish)


def _pair_exchange(grads, layouts, name):
    n = len(grads)

    def body(*refs):
        ins, outs = refs[:n], refs[n:2 * n]
        send_sems, recv_sems = refs[2 * n:]
        mx, my, mc = _my_pos()
        copies = []
        for a, lay in enumerate(layouts):
            cp = pltpu.make_async_remote_copy(
                src_ref=lay.halves_of_full(ins[a], 1 - mc), dst_ref=outs[a],
                send_sem=send_sems.at[a], recv_sem=recv_sems.at[a],
                device_id=(mx, my, 1 - mc), device_id_type=MESH)
            cp.start()
            copies.append(cp)
        for cp in copies:
            cp.wait()

    return pl.pallas_call(
        body, name=name,
        out_shape=[jax.ShapeDtypeStruct(lay.halves, g.dtype) for lay, g in zip(layouts, grads)],
        in_specs=[ANY] * n, out_specs=[ANY] * n,
        scratch_shapes=[pltpu.SemaphoreType.DMA((n,)), pltpu.SemaphoreType.DMA((n,))],
    )(*grads)


def _pair_sum(g, theirs, lay, core_arr, name):
    r, c = lay.halves
    tr = _row_tile(r, c * 4)
    nt = r // tr

    def body(core_ref, g_ref, t_ref, o_ref):
        o_ref[...] = (g_ref[...].astype(F32) + t_ref[...].astype(F32)).astype(BF16)

    if lay.by_cols:
        g_map = lambda i, core_ref: (core_ref[0] * nt + i, 0)
    else:
        g_map = lambda i, core_ref: (i, core_ref[0])
    plain = pl.BlockSpec((tr, c), lambda i, core_ref: (i, 0))
    return pl.pallas_call(
        body, name=name,
        grid_spec=pltpu.PrefetchScalarGridSpec(
            num_scalar_prefetch=1, grid=(nt,), in_specs=[pl.BlockSpec((tr, c), g_map), plain], out_specs=plain),
        out_shape=jax.ShapeDtypeStruct((r, c), BF16), compiler_params=_cp(("parallel",)))(core_arr, g, theirs)


def _scatter_comm(pair_sums, layouts):
    n = len(pair_sums)

    def copy(ins, outs, send_sems, recv_sems, a, j):
        mx, my, mc = _my_pos()
        dx, dy = CHIP_DELTAS[j]
        px, py = (mx + dx) % 2, (my + dy) % 2
        return pltpu.make_async_remote_copy(
            src_ref=layouts[a].region_in_halves(ins[a], 2 * px + py), dst_ref=outs[a].at[j],
            send_sem=send_sems.at[a, j], recv_sem=recv_sems.at[a, j], device_id=(px, py, mc), device_id_type=MESH)

    pairs = [(a, j) for a in range(n) for j in range(3)]

    def start(*r):
        for a, j in pairs:
            copy(*r, a, j).start()

    def finish(*r):
        for a, j in pairs:
            copy(*r, a, j).wait()

    return _Carried(pair_sums, [jax.ShapeDtypeStruct((3,) + lay.half, BF16) for lay in layouts], {}, (n, 3),
                    start, finish)


def _chip_sum(pair_sum, slots, lay, chip_arr, core_arr, name):
    r, c = lay.half
    tr = _row_tile(r, c * 4)
    nt = r // tr

    def body(chip_ref, core_ref, s_ref, slot_ref, o_ref):
        acc = s_ref[...].astype(F32)
        for j in range(3):
            acc = acc + slot_ref[j].astype(F32)
        o_ref[...] = acc

    if lay.by_cols:
        s_map = lambda i, chip_ref, core_ref: (i, chip_ref[0])
        o_map = lambda i, chip_ref, core_ref: (core_ref[0] * nt + i, 0)
    else:
        s_map = lambda i, chip_ref, core_ref: (chip_ref[0] * nt + i, 0)
        o_map = lambda i, chip_ref, core_ref: (i, core_ref[0])
    return pl.pallas_call(
        body, name=name,
        grid_spec=pltpu.PrefetchScalarGridSpec(
            num_scalar_prefetch=2, grid=(nt,),
            in_specs=[pl.BlockSpec((tr, c), s_map),
                      pl.BlockSpec((3, tr, c), lambda i, chip_ref, core_ref: (0, i, 0))],
            out_specs=pl.BlockSpec((tr, c), o_map)),
        out_shape=jax.ShapeDtypeStruct(lay.shard, F32), compiler_params=_cp(("parallel",)))(
            chip_arr, core_arr, pair_sum, slots)


def _pair_share(bufs, layouts):
    n = len(bufs)

    def body(*refs):
        ins, outs = refs[:n], refs[n:2 * n]
        send_sems, recv_sems = refs[2 * n:]
        mx, my, mc = _my_pos()
        copies = []
        for a, lay in enumerate(layouts):
            cp = pltpu.make_async_remote_copy(
                src_ref=lay.half_of_shard(ins[a], mc), dst_ref=lay.half_of_shard(outs[a], mc),
                send_sem=send_sems.at[a], recv_sem=recv_sems.at[a],
                device_id=(mx, my, 1 - mc), device_id_type=MESH)
            cp.start()
            copies.append(cp)
        for a, lay in enumerate(layouts):
            theirs = lay.half_of_shard(outs[a], 1 - mc)
            pltpu.make_async_remote_copy(
                src_ref=theirs, dst_ref=theirs, send_sem=send_sems.at[a], recv_sem=recv_sems.at[a],
                device_id=(mx, my, 1 - mc), device_id_type=MESH).wait_recv()
        for cp in copies:
            cp.wait_send()

    return pl.pallas_call(
        body, name="grad_pair_share",
        out_shape=[jax.ShapeDtypeStruct(lay.shard, F32) for lay in layouts],
        in_specs=[ANY] * n, out_specs=[ANY] * n, input_output_aliases={a: a for a in range(n)},
        scratch_shapes=[pltpu.SemaphoreType.DMA((n,)), pltpu.SemaphoreType.DMA((n,))],
    )(*bufs)


def _sum_slots(x, name):
    s, r, c = x.shape
    tr = _pick(r, (256, 128, 64, 32, 16, 8))

    def body(x_ref, o_ref):
        acc = x_ref[0]
        for k in range(1, s):
            acc = acc + x_ref[k]
        o_ref[...] = acc

    return pl.pallas_call(body, name=name, grid=(r // tr,),
                          in_specs=[pl.BlockSpec((s, tr, c), lambda i: (0, i, 0))],
                          out_specs=pl.BlockSpec((tr, c), lambda i: (i, 0)),
                          out_shape=jax.ShapeDtypeStruct((r, c), F32), compiler_params=_cp(("parallel",)))(x)


def _adamw(w, g, m, v, name):
    r, c = w.shape
    tr = _pick(r, (256, 128, 64, 32, 16, 8))
    bc1 = 1.0 - ADAM_B1 ** ADAM_STEP
    bc2 = 1.0 - ADAM_B2 ** ADAM_STEP

    def body(w_ref, g_ref, m_ref, v_ref, d_ref, nm_ref, nv_ref):
        gv = g_ref[...]
        nm = ADAM_B1 * m_ref[...] + (1.0 - ADAM_B1) * gv
        nv = ADAM_B2 * v_ref[...] + (1.0 - ADAM_B2) * (gv * gv)
        d_ref[...] = -ADAM_LR * ((nm / bc1) / (jnp.sqrt(nv / bc2) + ADAM_EPS) + ADAM_WD * w_ref[...])
        nm_ref[...] = nm
        nv_ref[...] = nv

    spec = pl.BlockSpec((tr, c), lambda i: (i, 0))
    shp = jax.ShapeDtypeStruct((r, c), F32)
    return pl.pallas_call(body, name=name, grid=(r // tr,), in_specs=[spec] * 4, out_specs=[spec] * 3,
                          out_shape=[shp] * 3, compiler_params=_cp(("parallel",)))(w, g, m, v)


def _cctx_grad(parts, c_ctx2d):
    def body(p_ref, c_ref, o_ref):
        tot = ((p_ref[0] + p_ref[2]) + p_ref[4]) + p_ref[6]
        o_ref[...] = tot * _dsilu(c_ref[...])

    return pl.pallas_call(body, name="cctx_grad", in_specs=[VMEM_SPEC, VMEM_SPEC], out_specs=VMEM_SPEC,
                          out_shape=jax.ShapeDtypeStruct(c_ctx2d.shape, F32))(parts, c_ctx2d)


def _mod_fwd(c_rows_t, ada_w, ada_b_shard):
    nl, d, w = ada_w.shape
    td = _pick(d, (256, 128))
    nd = d // td

    def body(ct_ref, w_ref, b_ref, o_ref):
        i = pl.program_id(1)

        @pl.when(i == 0)
        def _():
            o_ref[0] = jnp.broadcast_to(b_ref[0], (MOD_ROWS, w))

        st = _silu(ct_ref[...])
        wv = w_ref[0]
        rows = [jnp.sum(st[:, r:r + 1] * wv, axis=0, keepdims=True) for r in range(MOD_ROWS)]
        o_ref[0] += jnp.concatenate(rows, axis=0)

    return pl.pallas_call(
        body, name="mod_fwd", grid=(nl, nd),
        in_specs=[pl.BlockSpec((td, MOD_ROWS), lambda l, i: (i, 0)),
                  pl.BlockSpec((1, td, w), lambda l, i: (l, i, 0)),
                  pl.BlockSpec((1, 1, w), lambda l, i: (l, 0, 0))],
        out_specs=pl.BlockSpec((1, MOD_ROWS, w), lambda l, i: (l, 0, 0)),
        out_shape=jax.ShapeDtypeStruct((nl, MOD_ROWS, w), F32),
        compiler_params=_cp(("parallel", "arbitrary")),
    )(c_rows_t, ada_w, ada_b_shard)


def _mod_bwd(c_rows_t, dmod, ada_w):
    nl, d, w = ada_w.shape
    td = _pick(d, (256, 128))
    ctx_row = N_DEV

    def body(ct_ref, dm_ref, w_ref, gw_ref, ds_ref):
        st = _silu(ct_ref[...])
        dm = dm_ref[0]
        acc = st[:, 0:1] * dm[0:1, :]
        for r in range(1, ctx_row + 1):
            acc = acc + st[:, r:r + 1] * dm[r:r + 1, :]
        gw_ref[0] = acc
        ds_ref[0] = jnp.sum(w_ref[0] * dm[ctx_row:ctx_row + 1, :], axis=1, keepdims=True)

    return pl.pallas_call(
        body, name="mod_bwd", grid=(nl, d // td),
        in_specs=[pl.BlockSpec((td, MOD_ROWS), lambda l, i: (i, 0)),
                  pl.BlockSpec((1, MOD_ROWS, w), lambda l, i: (l, 0, 0)),
                  pl.BlockSpec((1, td, w), lambda l, i: (l, i, 0))],
        out_specs=[pl.BlockSpec((1, td, w), lambda l, i: (l, i, 0)),
                   pl.BlockSpec((1, td, 1), lambda l, i: (l, i, 0))],
        out_shape=[jax.ShapeDtypeStruct((nl, d, w), F32), jax.ShapeDtypeStruct((nl, d, 1), F32)],
        compiler_params=_cp(("parallel", "parallel")),
    )(c_rows_t, dmod, ada_w)


def _adaln_fwd(x, g, shift, scale, tr, name):
    r, d = x.shape

    def body(x_ref, g_ref, sh_ref, sc_ref, o_ref):
        xv = x_ref[...]
        rs = lax.rsqrt(jnp.mean(xv * xv, axis=-1, keepdims=True) + EPS)
        o_ref[...] = ((xv * rs * g_ref[...]) * (1.0 + sc_ref[...]) + sh_ref[...]).astype(BF16)

    spec = pl.BlockSpec((tr, d), lambda i: (i, 0))
    return pl.pallas_call(body, name=name, grid=(r // tr,), in_specs=[spec, _vec(d), _vec(d), _vec(d)],
                          out_specs=spec, out_shape=jax.ShapeDtypeStruct((r, d), BF16),
                          compiler_params=_cp(("parallel",)))(x, g, shift, scale)


def _adaln_bwd(xin, dh, row0, g, scale, dg_init, tr, name, dres=None, o_prev=None, gate_prev=None):
    r, d = xin.shape
    assert row0 % tr == 0
    rb0 = row0 // tr
    want_dx = dres is not None
    want_prev = o_prev is not None
    assert want_dx or not want_prev

    def body(*refs):
        it = iter(refs)
        x_ref, dh_ref, g_ref, sc_ref, dgi_ref = next(it), next(it), next(it), next(it), next(it)
        dres_ref = next(it) if want_dx else None
        o_ref, gp_ref = (next(it), next(it)) if want_prev else (None, None)
        dx_ref = next(it) if want_dx else None
        do_ref = next(it) if want_prev else None
        dsh_ref, dsc_ref, dg_ref = next(it), next(it), next(it)
        dgp_ref = next(it) if want_prev else None
        i = pl.program_id(0)

        @pl.when(i == 0)
        def _():
            dsh_ref[...] = jnp.zeros_like(dsh_ref)
            dsc_ref[...] = jnp.zeros_like(dsc_ref)
            dg_ref[...] = dgi_ref[...]
            if want_prev:
                dgp_ref[...] = jnp.zeros_like(dgp_ref)

        xv = x_ref[...]
        dhv = dh_ref[...].astype(F32)
        gv = g_ref[...]
        rs = lax.rsqrt(jnp.mean(xv * xv, axis=-1, keepdims=True) + EPS)
        xn = xv * rs
        dsh_ref[...] += jnp.sum(dhv, axis=0, keepdims=True)
        dsc_ref[...] += jnp.sum(dhv * (xn * gv), axis=0, keepdims=True)
        dr = dhv * (1.0 + sc_ref[...])
        dg_ref[...] += jnp.sum(dr * xn, axis=0, keepdims=True)
        if want_dx:
            gy = dr * gv
            dx = dres_ref[...] + rs * (gy - xn * jnp.mean(gy * xn, axis=-1, keepdims=True))
            dx_ref[...] = dx
            if want_prev:
                do_ref[...] = (gp_ref[...] * dx).astype(BF16)
                dgp_ref[...] += jnp.sum(dx * o_ref[...].astype(F32), axis=0, keepdims=True)

    row = pl.BlockSpec((tr, d), lambda i: (i, 0))
    in_specs = [row, pl.BlockSpec((tr, d), lambda i: (rb0 + i, 0)), _vec(d), _vec(d), _vec(d)]
    args = [xin, dh, g, scale, dg_init]
    out_specs, out_shape, names = [], [], []
    if want_dx:
        in_specs.append(row)
        args.append(dres)
    if want_prev:
        in_specs += [row, _vec(d)]
        args += [o_prev, gate_prev]
    if want_dx:
        out_specs.append(row)
        out_shape.append(jax.ShapeDtypeStruct((r, d), F32))
        names.append("dx")
    if want_prev:
        out_specs.append(row)
        out_shape.append(jax.ShapeDtypeStruct((r, d), BF16))
        names.append("do_prev")
    for nm in ("dshift", "dscale", "dg") + (("dgate_prev",) if want_prev else ()):
        out_specs.append(_vec(d))
        out_shape.append(jax.ShapeDtypeStruct((1, d), F32))
        names.append(nm)
    outs = pl.pallas_call(body, name=name, grid=(r // tr,), in_specs=in_specs, out_specs=out_specs,
                          out_shape=out_shape, compiler_params=_cp(("arbitrary",)))(*args)
    return dict(zip(names, outs))


def _mm(a, b, *, name, tm, tn, tk, ta=False, tb=False, out_dtype=F32, res=None, gate=None, comm=None):
    if ta:
        kd, m = a.shape
    else:
        m, kd = a.shape
    if tb:
        n, kd2 = b.shape
    else:
        kd2, n = b.shape
    assert kd == kd2 and m % tm == 0 and n % tn == 0 and kd % tk == 0, (a.shape, b.shape, tm, tn, tk)
    ni, nj, nk = m // tm, n // tn, kd // tk
    dn = (((0 if ta else 1,), (1 if tb else 0,)), ((), ()))
    with_res = res is not None
    n_in = 4 if with_res else 2
    n_out = 2 if with_res else 1
    n_cin = len(comm.ins) if comm else 0
    n_cout = len(comm.out_shapes) if comm else 0

    def body(*refs):
        in_refs = refs[:n_in + n_cin]
        out_refs = refs[n_in + n_cin:n_in + n_cin + n_out + n_cout]
        scratch = refs[n_in + n_cin + n_out + n_cout:]
        a_ref, b_ref = in_refs[0], in_refs[1]
        o_ref = out_refs[0]
        i, j, k = pl.program_id(0), pl.program_id(1), pl.program_id(2)
        if comm:
            carried = comm.split(in_refs, out_refs, scratch)

            @pl.when(jnp.logical_and(jnp.logical_and(i == 0, j == 0), k == 0))
            def _():
                comm.start(*carried)

        def emit(acc):
            o_ref[...] = acc.astype(o_ref.dtype)
            if with_res:
                out_refs[1][...] = in_refs[2][...] + in_refs[3][...] * acc

        if nk == 1:
            emit(lax.dot_general(a_ref[...], b_ref[...], dn, preferred_element_type=F32))
        else:
            acc_ref = scratch[0]

            @pl.when(k == 0)
            def _():
                acc_ref[...] = jnp.zeros_like(acc_ref)

            acc_ref[...] += lax.dot_general(a_ref[...], b_ref[...], dn, preferred_element_type=F32)

            @pl.when(k == nk - 1)
            def _():
                emit(acc_ref[...])

        if comm:
            @pl.when(jnp.logical_and(jnp.logical_and(i == ni - 1, j == nj - 1), k == nk - 1))
            def _():
                comm.finish(*carried)

    a_spec = pl.BlockSpec((tk, tm), lambda i, j, k: (k, i)) if ta else pl.BlockSpec((tm, tk), lambda i, j, k: (i, k))
    b_spec = pl.BlockSpec((tn, tk), lambda i, j, k: (j, k)) if tb else pl.BlockSpec((tk, tn), lambda i, j, k: (k, j))
    o_spec = pl.BlockSpec((tm, tn), lambda i, j, k: (i, j))
    in_specs, args = [a_spec, b_spec], [a, b]
    out_specs, out_shape = [o_spec], [jax.ShapeDtypeStruct((m, n), out_dtype)]
    if with_res:
        in_specs += [o_spec, pl.BlockSpec((1, tn), lambda i, j, k: (0, j))]
        args += [res, gate]
        out_specs.append(o_spec)
        out_shape.append(jax.ShapeDtypeStruct((m, n), F32))
    scratch_shapes = [pltpu.VMEM((tm, tn), F32)] if nk > 1 else []
    aliases = {}
    sem = ("parallel", "parallel", "arbitrary")
    if comm:
        in_specs += [ANY] * n_cin
        args += comm.ins
        out_specs += [ANY] * n_cout
        out_shape += comm.out_shapes
        scratch_shapes += comm.scratch()
        aliases = {n_in + s: n_out + d for s, d in comm.aliases.items()}
        sem = ("arbitrary", "arbitrary", "arbitrary")
    outs = pl.pallas_call(body, name=name, grid=(ni, nj, nk), in_specs=in_specs, out_specs=out_specs,
                          out_shape=out_shape, scratch_shapes=scratch_shapes, input_output_aliases=aliases,
                          compiler_params=_cp(sem))(*args)
    return outs if (with_res or comm) else outs[0]


def _swap_pairs(x):
    lane = lax.broadcasted_iota(jnp.int32, x.shape, 1)
    return jnp.where(lane % 2 == 0, pltpu.roll(x, HEAD_DIM - 1, 1), pltpu.roll(x, 1, 1))


def _qk_prep(p0, cos_i, sin_s, q_norm, k_norm, dims, tr):
    rows = p0.shape[0]
    kvw, aw, cb = dims["kv_w"], dims["attn_w"], dims["cb"]
    nkv, nq = kvw // HEAD_DIM, aw // HEAD_DIM
    scale = HEAD_DIM ** -0.5
    n_kv_specs, n_q_specs = (2 * kvw) // cb, aw // cb

    def body(*refs):
        kv_refs = refs[:n_kv_specs]
        q_refs = refs[n_kv_specs:n_kv_specs + n_q_specs]
        cos_ref, sin_ref, qn_ref, kn_ref, qo_ref, ko_ref, vo_ref = refs[n_kv_specs + n_q_specs:]
        kv = _cat(kv_refs)
        qv = _cat(q_refs)
        cs, sn = cos_ref[...], sin_ref[...]

        def norm_rope(xh, gvec):
            rs = lax.rsqrt(jnp.mean(xh * xh, axis=-1, keepdims=True) + EPS)
            xn = xh * rs * gvec
            return xn * cs + _swap_pairs(xn) * sn

        for h in range(nkv):
            sl = slice(h * HEAD_DIM, (h + 1) * HEAD_DIM)
            ko_ref[:, sl] = norm_rope(kv[:, sl], kn_ref[...]).astype(BF16)
        vo_ref[...] = kv[:, kvw:].astype(BF16)
        for h in range(nq):
            sl = slice(h * HEAD_DIM, (h + 1) * HEAD_DIM)
            qo_ref[:, sl] = (norm_rope(qv[:, sl], qn_ref[...]) * scale).astype(BF16)

    rm = lambda i: i
    in_specs = (_col_specs(tr, 0, 2 * kvw, cb, rm) + _col_specs(tr, 2 * kvw, aw, cb, rm)
                + [pl.BlockSpec((tr, HEAD_DIM), lambda i: (i, 0))] * 2 + [_vec(HEAD_DIM)] * 2)
    args = [p0] * (n_kv_specs + n_q_specs) + [cos_i, sin_s, q_norm, k_norm]
    return pl.pallas_call(
        body, name="qk_prep", grid=(rows // tr,), in_specs=in_specs,
        out_specs=[pl.BlockSpec((tr, aw), lambda i: (i, 0)), pl.BlockSpec((tr, kvw), lambda i: (i, 0)),
                   pl.BlockSpec((tr, kvw), lambda i: (i, 0))],
        out_shape=[jax.ShapeDtypeStruct((rows, aw), BF16), jax.ShapeDtypeStruct((rows, kvw), BF16),
                   jax.ShapeDtypeStruct((rows, kvw), BF16)],
        compiler_params=_cp(("parallel",)))(*args)


def _qk_prep_bwd(dq_hat, dk_hat, dv, p0, cos_i, sin_s, q_norm, k_norm, dims, tr, n_lat):
    rows = p0.shape[0]
    kvw, aw, cb = dims["kv_w"], dims["attn_w"], dims["cb"]
    nkv, nq = kvw // HEAD_DIM, aw // HEAD_DIM
    scale = HEAD_DIM ** -0.5
    n_kv_specs, n_q_specs = (2 * kvw) // cb, aw // cb
    lat_tiles = n_lat // tr

    def body(*refs):
        kv_refs = refs[:n_kv_specs]
        q_refs = refs[n_kv_specs:n_kv_specs + n_q_specs]
        (dq_ref, dk_ref, dv_ref, cos_ref, sin_ref, qn_ref, kn_ref,
         out_ref, dqn_ref, dkn_ref) = refs[n_kv_specs + n_q_specs:]
        i = pl.program_id(0)

        @pl.when(i == 0)
        def _():
            dqn_ref[...] = jnp.zeros_like(dqn_ref)
            dkn_ref[...] = jnp.zeros_like(dkn_ref)

        kv = _cat(kv_refs)
        qv = _cat(q_refs)
        cs, sn = cos_ref[...], sin_ref[...]
        is_lat = (i < lat_tiles).astype(F32)

        def head_bwd(xh, dhat, gvec):
            dn = dhat * cs + _swap_pairs(dhat * sn)
            rs = lax.rsqrt(jnp.mean(xh * xh, axis=-1, keepdims=True) + EPS)
            xn = xh * rs
            gy = dn * gvec
            dx = rs * (gy - xn * jnp.mean(gy * xn, axis=-1, keepdims=True))
            return dx, jnp.sum(dn * xn, axis=0, keepdims=True)

        dkn = jnp.zeros((1, HEAD_DIM), F32)
        for h in range(nkv):
            sl = slice(h * HEAD_DIM, (h + 1) * HEAD_DIM)
            dx, dgv = head_bwd(kv[:, sl], dk_ref[:, sl], kn_ref[...])
            out_ref[:, sl] = dx.astype(BF16)
            dkn = dkn + dgv
        dkn_ref[...] += dkn
        out_ref[:, kvw:2 * kvw] = dv_ref[...].astype(BF16)
        dqn = jnp.zeros((1, HEAD_DIM), F32)
        for h in range(nq):
            sl = slice(h * HEAD_DIM, (h + 1) * HEAD_DIM)
            dx, dgv = head_bwd(qv[:, sl], dq_ref[:, sl] * (scale * is_lat), qn_ref[...])
            out_ref[:, 2 * kvw + h * HEAD_DIM:2 * kvw + (h + 1) * HEAD_DIM] = dx.astype(BF16)
            dqn = dqn + dgv
        dqn_ref[...] += dqn

    rm = lambda i: i
    wout = 2 * kvw + aw
    in_specs = (_col_specs(tr, 0, 2 * kvw, cb, rm) + _col_specs(tr, 2 * kvw, aw, cb, rm)
                + [pl.BlockSpec((tr, aw), lambda i: (jnp.minimum(i, lat_tiles - 1), 0)),
                   pl.BlockSpec((tr, kvw), lambda i: (i, 0)), pl.BlockSpec((tr, kvw), lambda i: (i, 0)),
                   pl.BlockSpec((tr, HEAD_DIM), lambda i: (i, 0)), pl.BlockSpec((tr, HEAD_DIM), lambda i: (i, 0)),
                   _vec(HEAD_DIM), _vec(HEAD_DIM)])
    args = [p0] * (n_kv_specs + n_q_specs) + [dq_hat, dk_hat, dv, cos_i, sin_s, q_norm, k_norm]
    return pl.pallas_call(
        body, name="qk_prep_bwd", grid=(rows // tr,), in_specs=in_specs,
        out_specs=[pl.BlockSpec((tr, wout), lambda i: (i, 0)), _vec(HEAD_DIM), _vec(HEAD_DIM)],
        out_shape=[jax.ShapeDtypeStruct((rows, wout), BF16), jax.ShapeDtypeStruct((1, HEAD_DIM), F32),
                   jax.ShapeDtypeStruct((1, HEAD_DIM), F32)],
        compiler_params=_cp(("arbitrary",)))(*args)


def _stack_heads(x, g):
    return jnp.concatenate([x[:, h * HEAD_DIM:(h + 1) * HEAD_DIM] for h in range(g)], axis=0)


def _flash_fwd(q_hat, k_all, v_all, n_lat, dims, tq, tk):
    kvw, aw = dims["kv_w"], dims["attn_w"]
    nkv = kvw // HEAD_DIM
    g = aw // kvw
    gw = g * HEAD_DIM
    n_keys = k_all.shape[0]
    ni, nj = n_lat // tq, n_keys // tk
    dn_nt = (((1,), (1,)), ((), ()))

    def body(q_ref, k_ref, v_ref, o_ref, lse_ref):
        qs = _stack_heads(q_ref[...], g)
        m = jnp.full((g * tq, 1), -1e30, F32)
        l = jnp.zeros((g * tq, 1), F32)
        acc = jnp.zeros((g * tq, HEAD_DIM), F32)
        for j in range(nj):
            kb = k_ref[pl.ds(j * tk, tk), :]
            vb = v_ref[pl.ds(j * tk, tk), :]
            s = lax.dot_general(qs, kb, dn_nt, preferred_element_type=F32)
            m_new = jnp.maximum(m, jnp.max(s, axis=-1, keepdims=True))
            alpha = jnp.exp(m - m_new)
            p = jnp.exp(s - m_new)
            l = alpha * l + jnp.sum(p, axis=-1, keepdims=True)
            acc = alpha * acc + jnp.dot(p.astype(BF16), vb, preferred_element_type=F32)
            m = m_new
        o = acc / l
        for h in range(g):
            o_ref[:, h * HEAD_DIM:(h + 1) * HEAD_DIM] = o[h * tq:(h + 1) * tq]
        lse_ref[...] = m + jnp.log(l)

    return pl.pallas_call(
        body, name="flash_fwd", grid=(nkv, ni),
        in_specs=[pl.BlockSpec((tq, gw), lambda h, i: (i, h)),
                  pl.BlockSpec((n_keys, HEAD_DIM), lambda h, i: (0, h)),
                  pl.BlockSpec((n_keys, HEAD_DIM), lambda h, i: (0, h))],
        out_specs=[pl.BlockSpec((tq, gw), lambda h, i: (i, h)),
                   pl.BlockSpec((g * tq, 1), lambda h, i: (h * ni + i, 0))],
        out_shape=[jax.ShapeDtypeStruct((n_lat, aw), F32), jax.ShapeDtypeStruct((nkv * ni * g * tq, 1), F32)],
        compiler_params=_cp(("parallel", "parallel")))(q_hat, k_all, v_all)


def _flash_bwd(q_hat, k_all, v_all, do, o, lse, n_lat, dims, tq, tk):
    kvw, aw = dims["kv_w"], dims["attn_w"]
    nkv = kvw // HEAD_DIM
    g = aw // kvw
    gw = g * HEAD_DIM
    n_keys = k_all.shape[0]
    ni, nj = n_lat // tq, n_keys // tk
    dn_nt = (((1,), (1,)), ((), ()))
    dn_tn = (((0,), (0,)), ((), ()))

    def body(q_ref, k_ref, v_ref, do_ref, o_ref, lse_ref, dq_ref, dk_ref, dv_ref):
        i = pl.program_id(1)

        @pl.when(i == 0)
        def _():
            dk_ref[...] = jnp.zeros_like(dk_ref)
            dv_ref[...] = jnp.zeros_like(dv_ref)

        dos = _stack_heads(do_ref[...], g)
        qs = _stack_heads(q_ref[...], g)
        delta = jnp.sum(dos.astype(F32) * _stack_heads(o_ref[...], g), axis=-1, keepdims=True)
        lse_v = lse_ref[...]
        dq = jnp.zeros((g * tq, HEAD_DIM), F32)
        for j in range(nj):
            rows = pl.ds(j * tk, tk)
            kb, vb = k_ref[rows, :], v_ref[rows, :]
            s = lax.dot_general(qs, kb, dn_nt, preferred_element_type=F32)
            p = jnp.exp(s - lse_v)
            dp = lax.dot_general(dos, vb, dn_nt, preferred_element_type=F32)
            ds = (p * (dp - delta)).astype(BF16)
            dv_ref[rows, :] += lax.dot_general(p.astype(BF16), dos, dn_tn, preferred_element_type=F32)
            dk_ref[rows, :] += lax.dot_general(ds, qs, dn_tn, preferred_element_type=F32)
            dq = dq + jnp.dot(ds, kb, preferred_element_type=F32)
        for h in range(g):
            dq_ref[:, h * HEAD_DIM:(h + 1) * HEAD_DIM] = dq[h * tq:(h + 1) * tq]

    qspec = pl.BlockSpec((tq, gw), lambda h, i: (i, h))
    full_k = pl.BlockSpec((n_keys, HEAD_DIM), lambda h, i: (0, h))
    return pl.pallas_call(
        body, name="flash_bwd", grid=(nkv, ni),
        in_specs=[qspec, full_k, full_k, qspec, qspec, pl.BlockSpec((g * tq, 1), lambda h, i: (h * ni + i, 0))],
        out_specs=[qspec, full_k, full_k],
        out_shape=[jax.ShapeDtypeStruct((n_lat, aw), F32), jax.ShapeDtypeStruct((n_keys, kvw), F32),
                   jax.ShapeDtypeStruct((n_keys, kvw), F32)],
        compiler_params=_cp(("parallel", "arbitrary")))(q_hat, k_all, v_all, do, o, lse)


def _shifted_copies(pad_ref, sh_ref, tr):
    rows = tr + 2 * CONV_HALO - SUBLANES
    for r in range(SUBLANES):
        sh_ref[r] = pad_ref[pl.ds(r, rows), :]


def _stencil(sh_ref, w_ref, out_ref, offsets, tr, cc, init_ref=None):
    for c0 in range(0, cc, LANES):
        lanes = pl.ds(c0, LANES)
        wv = [jnp.broadcast_to(w_ref[pl.ds(k, 1), lanes], (STENCIL_ROWS, LANES)) for k in range(len(offsets))]
        if init_ref is None:
            init = jnp.zeros((STENCIL_ROWS, LANES), F32)
        else:
            init = jnp.broadcast_to(init_ref[:, lanes], (STENCIL_ROWS, LANES))

        def block(rb, carry, lanes=lanes, wv=wv, init=init):
            r0 = rb * STENCIL_ROWS
            parts = [init] + [None] * (STENCIL_CHAINS - 1)
            for k, o in enumerate(offsets):
                rows = pl.ds(SUBLANES * (o // SUBLANES) + r0, STENCIL_ROWS)
                term = sh_ref[o % SUBLANES, rows, lanes] * wv[k]
                q = k % STENCIL_CHAINS
                parts[q] = term if parts[q] is None else parts[q] + term
            acc = parts[0]
            for p_ in parts[1:]:
                acc = acc + p_
            out_ref[pl.ds(r0, STENCIL_ROWS), lanes] = acc
            return carry

        for rb in range(tr // STENCIL_ROWS):
            block(rb, 0)


def _stencil_weight_grad(sh_ref, d_ref, d_row0, dw_ref, offsets, tr, cc):
    rows_per = REDUCE_ROWS
    for c0 in range(0, cc, LANES):
        lanes = pl.ds(c0, LANES)

        def block(rb, accs, lanes=lanes):
            r0 = rb * rows_per
            dblk = d_ref[pl.ds(d_row0 + r0, rows_per), lanes]
            out = []
            for k, o in enumerate(offsets):
                prod = dblk * sh_ref[o % SUBLANES, pl.ds(SUBLANES * (o // SUBLANES) + r0, rows_per), lanes]
                part = prod[0:SUBLANES]
                for q in range(1, rows_per // SUBLANES):
                    part = part + prod[q * SUBLANES:(q + 1) * SUBLANES]
                out.append(accs[k] + part)
            return tuple(out)

        zero = jnp.zeros((SUBLANES, LANES), F32)
        accs = tuple(zero for _ in offsets)
        for rb in range(tr // rows_per):
            accs = block(rb, accs)
        for k in range(len(offsets)):
            dw_ref[pl.ds(k, 1), lanes] += jnp.sum(accs[k], axis=0, keepdims=True)


def _halo_maps(tr, n_tiles):
    per = tr // CONV_HALO
    prev = lambda i: jnp.maximum(i * per - 1, 0)
    nxt = lambda i: (i + 1) * per
    return prev, nxt


def _mix_fwd(attn, p0, dw_w, dw_b, ln_g, ln_b, dims, tr):
    n_lat, aw = attn.shape
    cc, cb, cw = dims["conv_ch"], dims["cb"], dims["conv_w"]
    off = dims["off"]
    n_tiles = n_lat // tr
    pad = cw // 2
    na, nc = aw // cb, cc // cb
    prev_map, next_map = _halo_maps(tr, n_tiles)

    def body(*refs):
        it = iter(refs)
        attn_ref = next(it)
        za = [next(it) for _ in range(na)]
        a_c = [next(it) for _ in range(nc)]
        b_c = [next(it) for _ in range(nc)]
        zb = [next(it) for _ in range(nc)]
        a_p = [next(it) for _ in range(nc)]
        b_p = [next(it) for _ in range(nc)]
        a_n = [next(it) for _ in range(nc)]
        b_n = [next(it) for _ in range(nc)]
        w_ref, db_ref, g_ref, bb_ref, mix_ref, yc_ref, ypad, ysh = (next(it) for _ in range(8))
        i = pl.program_id(0)
        ypad[pl.ds(0, CONV_HALO), :] = _cat(a_p) * _sigmoid(_cat(b_p)) * (i > 0).astype(F32)
        ypad[pl.ds(CONV_HALO, tr), :] = _cat(a_c) * _sigmoid(_cat(b_c))
        ypad[pl.ds(CONV_HALO + tr, CONV_HALO), :] = _cat(a_n) * _sigmoid(_cat(b_n)) * (i < n_tiles - 1).astype(F32)
        _shifted_copies(ypad, ysh, tr)
        _stencil(ysh, w_ref, yc_ref, [CONV_HALO - pad + k for k in range(cw)], tr, cc, init_ref=db_ref)
        acc = yc_ref[...]
        mu = jnp.mean(acc, axis=-1, keepdims=True)
        xc = acc - mu
        rs = lax.rsqrt(jnp.mean(xc * xc, axis=-1, keepdims=True) + EPS)
        nv = xc * rs * g_ref[...] + bb_ref[...]
        mix_ref[:, :aw] = (attn_ref[...] * _silu(_cat(za))).astype(BF16)
        mix_ref[:, aw:] = (_silu(nv) * _silu(_cat(zb))).astype(BF16)

    rm = lambda i: i
    in_specs = ([pl.BlockSpec((tr, aw), lambda i: (i, 0))]
                + _col_specs(tr, off["za"], aw, cb, rm) + _col_specs(tr, off["a"], cc, cb, rm)
                + _col_specs(tr, off["b"], cc, cb, rm) + _col_specs(tr, off["zb"], cc, cb, rm)
                + _col_specs(CONV_HALO, off["a"], cc, cb, prev_map) + _col_specs(CONV_HALO, off["b"], cc, cb, prev_map)
                + _col_specs(CONV_HALO, off["a"], cc, cb, next_map) + _col_specs(CONV_HALO, off["b"], cc, cb, next_map)
                + [pl.BlockSpec(dw_w.shape, lambda i: (0, 0)), _vec(cc), _vec(cc), _vec(cc)])
    args = [attn] + [p0] * (na + 7 * nc) + [dw_w, dw_b, ln_g, ln_b]
    return pl.pallas_call(
        body, name="mix_fwd", grid=(n_tiles,), in_specs=in_specs,
        out_specs=[pl.BlockSpec((tr, aw + cc), lambda i: (i, 0)), pl.BlockSpec((tr, cc), lambda i: (i, 0))],
        out_shape=[jax.ShapeDtypeStruct((n_lat, aw + cc), BF16), jax.ShapeDtypeStruct((n_lat, cc), F32)],
        scratch_shapes=[pltpu.VMEM((tr + 2 * CONV_HALO, cc), F32),
                        pltpu.VMEM((SUBLANES, tr + 2 * CONV_HALO - SUBLANES, cc), F32)],
        compiler_params=_cp(("parallel",)))(*args)


def _mix_bwd_pointwise(dmix, attn, p0, yc, ln_g, ln_b, dims, tr, n_ext):
    n_lat, aw = attn.shape
    cc, cb, off = dims["conv_ch"], dims["cb"], dims["off"]
    na, nc = aw // cb, cc // cb
    lat_tiles = n_lat // tr

    def body(*refs):
        it = iter(refs)
        dmix_ref, attn_ref = next(it), next(it)
        za = [next(it) for _ in range(na)]
        zb = [next(it) for _ in range(nc)]
        yc_ref, g_ref, bb_ref = next(it), next(it), next(it)
        dattn_ref, dza_ref, dzb_ref, dyc_ref, dg_ref, dbb_ref, ddb_ref = (next(it) for _ in range(7))
        i = pl.program_id(0)

        @pl.when(i == 0)
        def _():
            dg_ref[...] = jnp.zeros_like(dg_ref)
            dbb_ref[...] = jnp.zeros_like(dbb_ref)
            ddb_ref[...] = jnp.zeros_like(ddb_ref)

        lat = (i < lat_tiles).astype(F32)
        dm = dmix_ref[...].astype(F32)
        dma, dmb = dm[:, :aw], dm[:, aw:]
        zav, zbv = _cat(za), _cat(zb)
        sa, dsa = _silu_and_grad(zav)
        sb, dsb = _silu_and_grad(zbv)
        dattn_ref[...] = (dma * sa).astype(BF16)
        dza_ref[...] = (dma * attn_ref[...] * dsa * lat).astype(BF16)
        ycv = yc_ref[...]
        mu = jnp.mean(ycv, axis=-1, keepdims=True)
        xc = ycv - mu
        rs = lax.rsqrt(jnp.mean(xc * xc, axis=-1, keepdims=True) + EPS)
        xh = xc * rs
        nv = xh * g_ref[...] + bb_ref[...]
        sn, dsn = _silu_and_grad(nv)
        dzb_ref[...] = (dmb * sn * dsb * lat).astype(BF16)
        dn = dmb * sb * dsn
        dg_ref[...] += lat * jnp.sum(dn * xh, axis=0, keepdims=True)
        dbb_ref[...] += lat * jnp.sum(dn, axis=0, keepdims=True)
        dxh = dn * g_ref[...]
        dyc = rs * (dxh - jnp.mean(dxh, axis=-1, keepdims=True) - xh * jnp.mean(dxh * xh, axis=-1, keepdims=True))
        dyc_ref[...] = dyc
        ddb_ref[...] += lat * jnp.sum(dyc, axis=0, keepdims=True)

    rm = lambda i: jnp.minimum(i, lat_tiles - 1)
    row = lambda w: pl.BlockSpec((tr, w), lambda i: (rm(i), 0))
    ext = lambda w: pl.BlockSpec((tr, w), lambda i: (i, 0))
    in_specs = ([row(aw + cc), row(aw)] + _col_specs(tr, off["za"], aw, cb, rm)
                + _col_specs(tr, off["zb"], cc, cb, rm) + [row(cc), _vec(cc), _vec(cc)])
    args = [dmix, attn] + [p0] * (na + nc) + [yc, ln_g, ln_b]
    return pl.pallas_call(
        body, name="mix_bwd_pointwise", grid=(n_ext // tr,), in_specs=in_specs,
        out_specs=[row(aw), ext(aw), ext(cc), row(cc), _vec(cc), _vec(cc), _vec(cc)],
        out_shape=[jax.ShapeDtypeStruct((n_lat, aw), BF16), jax.ShapeDtypeStruct((n_ext, aw), BF16),
                   jax.ShapeDtypeStruct((n_ext, cc), BF16), jax.ShapeDtypeStruct((n_lat, cc), F32)]
                  + [jax.ShapeDtypeStruct((1, cc), F32)] * 3,
        compiler_params=_cp(("arbitrary",)))(*args)


def _conv_bwd(dyc, p0, dw_w, dims, tr, n_ext):
    n_lat, cc = dyc.shape
    cb, cw, off = dims["cb"], dims["conv_w"], dims["off"]
    nc = cc // cb
    n_tiles = n_lat // tr
    pad = cw // 2
    prev_lat, next_lat = _halo_maps(tr, n_tiles)
    rm = lambda i: jnp.minimum(i, n_tiles - 1)
    prev_map = lambda i: prev_lat(rm(i))
    next_map = lambda i: next_lat(rm(i))

    def body(*refs):
        it = iter(refs)
        a_c = [next(it) for _ in range(nc)]
        b_c = [next(it) for _ in range(nc)]
        a_p = [next(it) for _ in range(nc)]
        b_p = [next(it) for _ in range(nc)]
        a_n = [next(it) for _ in range(nc)]
        b_n = [next(it) for _ in range(nc)]
        d_c, d_p, d_n, w_ref, dab_ref, dw_ref, ypad, dpad, ysh, dsh, dy_scr = (next(it) for _ in range(11))
        i = pl.program_id(0)

        @pl.when(i == 0)
        def _():
            dw_ref[...] = jnp.zeros_like(dw_ref)

        @pl.when(i >= n_tiles)
        def _():
            dab_ref[...] = jnp.zeros_like(dab_ref)

        @pl.when(i < n_tiles)
        def _():
            first, last = (i > 0).astype(F32), (i < n_tiles - 1).astype(F32)
            av, bv = _cat(a_c), _cat(b_c)
            sg = _sigmoid(bv)
            ypad[pl.ds(0, CONV_HALO), :] = _cat(a_p) * _sigmoid(_cat(b_p)) * first
            ypad[pl.ds(CONV_HALO, tr), :] = av * sg
            ypad[pl.ds(CONV_HALO + tr, CONV_HALO), :] = _cat(a_n) * _sigmoid(_cat(b_n)) * last
            dpad[pl.ds(0, CONV_HALO), :] = d_p[...] * first
            dpad[pl.ds(CONV_HALO, tr), :] = d_c[...]
            dpad[pl.ds(CONV_HALO + tr, CONV_HALO), :] = d_n[...] * last
            _shifted_copies(ypad, ysh, tr)
            _shifted_copies(dpad, dsh, tr)
            _stencil(dsh, w_ref, dy_scr, [CONV_HALO + pad - k for k in range(cw)], tr, cc)
            _stencil_weight_grad(ysh, dpad, CONV_HALO, dw_ref, [CONV_HALO - pad + k for k in range(cw)], tr, cc)
            dy = dy_scr[...]
            dab_ref[:, :cc] = (dy * sg).astype(BF16)
            dab_ref[:, cc:] = (dy * av * sg * (1.0 - sg)).astype(BF16)

    in_specs = (_col_specs(tr, off["a"], cc, cb, rm) + _col_specs(tr, off["b"], cc, cb, rm)
                + _col_specs(CONV_HALO, off["a"], cc, cb, prev_map) + _col_specs(CONV_HALO, off["b"], cc, cb, prev_map)
                + _col_specs(CONV_HALO, off["a"], cc, cb, next_map) + _col_specs(CONV_HALO, off["b"], cc, cb, next_map)
                + [pl.BlockSpec((tr, cc), lambda i: (rm(i), 0)),
                   pl.BlockSpec((CONV_HALO, cc), lambda i: (prev_map(i), 0)),
                   pl.BlockSpec((CONV_HALO, cc), lambda i: (jnp.minimum(next_map(i), n_lat // CONV_HALO - 1), 0)),
                   pl.BlockSpec(dw_w.shape, lambda i: (0, 0))])
    args = [p0] * (6 * nc) + [dyc, dyc, dyc, dw_w]
    return pl.pallas_call(
        body, name="conv_bwd", grid=(n_ext // tr,), in_specs=in_specs,
        out_specs=[pl.BlockSpec((tr, 2 * cc), lambda i: (i, 0)), pl.BlockSpec(dw_w.shape, lambda i: (0, 0))],
        out_shape=[jax.ShapeDtypeStruct((n_ext, 2 * cc), BF16), jax.ShapeDtypeStruct(dw_w.shape, F32)],
        scratch_shapes=[pltpu.VMEM((tr + 2 * CONV_HALO, cc), F32), pltpu.VMEM((tr + 2 * CONV_HALO, cc), F32),
                        pltpu.VMEM((SUBLANES, tr + 2 * CONV_HALO - SUBLANES, cc), F32),
                        pltpu.VMEM((SUBLANES, tr + 2 * CONV_HALO - SUBLANES, cc), F32),
                        pltpu.VMEM((tr, cc), F32)],
        compiler_params=_cp(("arbitrary",)))(*args)


def _sgu_parts(u, v, ln_g, ln_b):
    mu = jnp.mean(v, axis=-1, keepdims=True)
    xc = v - mu
    rs = lax.rsqrt(jnp.mean(xc * xc, axis=-1, keepdims=True) + EPS)
    xh = xc * rs
    return u, xh, rs, xh * ln_g + ln_b


def _sgu_fwd(p1, ln_g, ln_b, ws, bs_t, tr):
    n_lat, w3 = p1.shape
    w = w3 // 3
    ng, ch = ws.shape[0], ws.shape[1]
    gwid = w // ng
    n_ch = tr // ch

    def body(pu_ref, pv_ref, pg_ref, g_ref, b_ref, ws_ref, bs_ref, o_ref):
        u, _, _, vln = _sgu_parts(_gelu(pu_ref[...].astype(F32)), _gelu(pv_ref[...].astype(F32)),
                                  g_ref[...], b_ref[...])
        gate = _silu(pg_ref[...].astype(F32))
        vb = vln.astype(BF16)
        for c in range(n_ch):
            rs_ = slice(c * ch, (c + 1) * ch)
            for gi in range(ng):
                cs_ = slice(gi * gwid, (gi + 1) * gwid)
                mixed = jnp.dot(ws_ref[gi], vb[rs_, cs_], preferred_element_type=F32) + bs_ref[:, gi:gi + 1]
                o_ref[rs_, cs_] = (u[rs_, cs_] * mixed * gate[rs_, cs_]).astype(BF16)

    col = lambda t: pl.BlockSpec((tr, w), lambda i, _t=t: (i, _t))
    return pl.pallas_call(
        body, name="sgu_fwd", grid=(n_lat // tr,),
        in_specs=[col(0), col(1), col(2), _vec(w), _vec(w),
                  pl.BlockSpec(ws.shape, lambda i: (0, 0, 0)), pl.BlockSpec(bs_t.shape, lambda i: (0, 0))],
        out_specs=pl.BlockSpec((tr, w), lambda i: (i, 0)),
        out_shape=jax.ShapeDtypeStruct((n_lat, w), BF16),
        compiler_params=_cp(("parallel",)))(p1, p1, p1, ln_g, ln_b, ws, bs_t)


def _sgu_bwd(dm, p1, ln_g, ln_b, ws, ws_t, bs_t, tr):
    n_lat, w3 = p1.shape
    w = w3 // 3
    ng, ch = ws.shape[0], ws.shape[1]
    gwid = w // ng
    n_ch = tr // ch
    dn_nt = (((1,), (1,)), ((), ()))

    def body(dm_ref, pu_ref, pv_ref, pg_ref, g_ref, b_ref, ws_ref, wst_ref, bs_ref,
             dp_ref, dws_ref, dbs_ref, dg_ref, dbb_ref, dvln_scr):
        i = pl.program_id(0)

        @pl.when(i == 0)
        def _():
            dws_ref[...] = jnp.zeros_like(dws_ref)
            dbs_ref[...] = jnp.zeros_like(dbs_ref)
            dg_ref[...] = jnp.zeros_like(dg_ref)
            dbb_ref[...] = jnp.zeros_like(dbb_ref)

        puv, pvv, pgv = pu_ref[...].astype(F32), pv_ref[...].astype(F32), pg_ref[...].astype(F32)
        gu, dgu = _gelu_and_grad(puv)
        gv, dgv = _gelu_and_grad(pvv)
        gate, dgate = _silu_and_grad(pgv)
        u, xh, rs, vln = _sgu_parts(gu, gv, g_ref[...], b_ref[...])
        dmv = dm_ref[...].astype(F32)
        vb = vln.astype(BF16)
        dmu = dmv * u
        du_pre = dmv * gate * dgu
        dg_pre = dmu * dgate
        dmix_all = dmu * gate
        dbs_cols = [jnp.zeros((ch, 1), F32) for _ in range(ng)]
        for c in range(n_ch):
            rs_ = slice(c * ch, (c + 1) * ch)
            for gi in range(ng):
                cs_ = slice(gi * gwid, (gi + 1) * gwid)
                mixed = jnp.dot(ws_ref[gi], vb[rs_, cs_], preferred_element_type=F32) + bs_ref[:, gi:gi + 1]
                dmixed = dmix_all[rs_, cs_]
                dmb = dmixed.astype(BF16)
                dp_ref[rs_, gi * gwid:(gi + 1) * gwid] = (du_pre[rs_, cs_] * mixed).astype(BF16)
                dp_ref[rs_, 2 * w + gi * gwid:2 * w + (gi + 1) * gwid] = (dg_pre[rs_, cs_] * mixed).astype(BF16)
                dvln_scr[rs_, cs_] = jnp.dot(wst_ref[gi], dmb, preferred_element_type=F32)
                dws_ref[gi] += lax.dot_general(dmb, vb[rs_, cs_], dn_nt, preferred_element_type=F32)
                dbs_cols[gi] = dbs_cols[gi] + jnp.sum(dmixed, axis=-1, keepdims=True)
        dbs_ref[...] += jnp.concatenate(dbs_cols, axis=1)
        dvln = dvln_scr[...]
        dg_ref[...] += jnp.sum(dvln * xh, axis=0, keepdims=True)
        dbb_ref[...] += jnp.sum(dvln, axis=0, keepdims=True)
        dxh = dvln * g_ref[...]
        dv = rs * (dxh - jnp.mean(dxh, axis=-1, keepdims=True) - xh * jnp.mean(dxh * xh, axis=-1, keepdims=True))
        dp_ref[:, w:2 * w] = (dv * dgv).astype(BF16)

    col = lambda t: pl.BlockSpec((tr, w), lambda i, _t=t: (i, _t))
    return pl.pallas_call(
        body, name="sgu_bwd", grid=(n_lat // tr,),
        in_specs=[pl.BlockSpec((tr, w), lambda i: (i, 0)), col(0), col(1), col(2), _vec(w), _vec(w),
                  pl.BlockSpec(ws.shape, lambda i: (0, 0, 0)), pl.BlockSpec(ws.shape, lambda i: (0, 0, 0)),
                  pl.BlockSpec(bs_t.shape, lambda i: (0, 0))],
        out_specs=[pl.BlockSpec((tr, w3), lambda i: (i, 0)), pl.BlockSpec(ws.shape, lambda i: (0, 0, 0)),
                   pl.BlockSpec(bs_t.shape, lambda i: (0, 0)), _vec(w), _vec(w)],
        out_shape=[jax.ShapeDtypeStruct((n_lat, w3), BF16), jax.ShapeDtypeStruct(ws.shape, F32),
                   jax.ShapeDtypeStruct(bs_t.shape, F32), jax.ShapeDtypeStruct((1, w), F32),
                   jax.ShapeDtypeStruct((1, w), F32)],
        scratch_shapes=[pltpu.VMEM((tr, w), F32)],
        compiler_params=_cp(("arbitrary",)))(dm, p1, p1, p1, ln_g, ln_b, ws, ws_t, bs_t)


def _final_loss(x2, target, final_g, o_prev, gate_prev, tr):
    n_lat, d = x2.shape

    def body(x_ref, t_ref, g_ref, o_ref, gp_ref, dx_ref, do_ref, ls_ref, dg_ref, dgp_ref):
        i = pl.program_id(0)

        @pl.when(i == 0)
        def _():
            ls_ref[...] = jnp.zeros_like(ls_ref)
            dg_ref[...] = jnp.zeros_like(dg_ref)
            dgp_ref[...] = jnp.zeros_like(dgp_ref)

        xv = x_ref[...]
        gv = g_ref[...]
        rs = lax.rsqrt(jnp.mean(xv * xv, axis=-1, keepdims=True) + EPS)
        xn = xv * rs
        err = xn * gv - t_ref[...]
        ls_ref[...] += jnp.sum(err * err, axis=0, keepdims=True)
        dy = err * (1.0 / d)
        dg_ref[...] += jnp.sum(dy * xn, axis=0, keepdims=True)
        gy = dy * gv
        dx = rs * (gy - xn * jnp.mean(gy * xn, axis=-1, keepdims=True))
        dx_ref[...] = dx
        do_ref[...] = (gp_ref[...] * dx).astype(BF16)
        dgp_ref[...] += jnp.sum(dx * o_ref[...].astype(F32), axis=0, keepdims=True)

    row = pl.BlockSpec((tr, d), lambda i: (i, 0))
    return pl.pallas_call(
        body, name="final_loss", grid=(n_lat // tr,), in_specs=[row, row, _vec(d), row, _vec(d)],
        out_specs=[row, row, _vec(d), _vec(d), _vec(d)],
        out_shape=[jax.ShapeDtypeStruct((n_lat, d), F32), jax.ShapeDtypeStruct((n_lat, d), BF16)]
                  + [jax.ShapeDtypeStruct((1, d), F32)] * 3,
        compiler_params=_cp(("arbitrary",)))(x2, target, final_g, o_prev, gate_prev)


def _pack(arrays):
    flat = jnp.concatenate([a.reshape(-1).astype(F32) for a in arrays])
    n = flat.shape[0]
    rows = -(-n // LANES)
    rows = -(-rows // PACK_ROWS) * PACK_ROWS
    return jnp.pad(flat, (0, rows * LANES - n)).reshape(rows, LANES)


def _unpack(buf, shapes):
    flat = buf.reshape(buf.shape[:-2] + (-1,))
    out, pos = [], 0
    for shp in shapes:
        n = math.prod(shp)
        out.append(flat[..., pos:pos + n].reshape(buf.shape[:-2] + tuple(shp)))
        pos += n
    return out


def _rope_tables(n_lat, n_ctx):
    rows = n_lat // GRID_W
    row = jnp.repeat(jnp.arange(rows, dtype=F32), GRID_W)
    col = jnp.tile(jnp.arange(GRID_W, dtype=F32), rows)
    n_freq, axis_dim = HEAD_DIM // 4, HEAD_DIM // 2
    inv = jnp.power(ROPE_THETA, jnp.arange(n_freq, dtype=F32) * (-2.0 / axis_dim))
    ang = jnp.concatenate([row[:, None] * inv, col[:, None] * inv], axis=-1)
    cos, sin = jnp.cos(ang), jnp.sin(ang)
    cos_i = jnp.repeat(cos, 2, axis=-1)
    sin_s = jnp.stack([-sin, sin], axis=-1).reshape(n_lat, HEAD_DIM)
    cos_i = jnp.concatenate([cos_i, jnp.ones((n_ctx, HEAD_DIM), F32)], axis=0)
    sin_s = jnp.concatenate([sin_s, jnp.zeros((n_ctx, HEAD_DIM), F32)], axis=0)
    return cos_i, sin_s


def kernel(x, c, ctx, c_ctx, ada_w, ada_b, norm_g, ev_w_in, ev_q_norm, ev_k_norm, ev_dw_w, ev_dw_b, ev_ln_g, ev_ln_b, ev_w_out, od_w_in, od_ln_g, od_ln_b, od_ws, od_bs, od_w_out, final_g, loss_target, m_c_ctx, m_ada_w, m_ada_b, m_norm_g, m_ev_w_in, m_ev_q_norm, m_ev_k_norm, m_ev_dw_w, m_ev_dw_b, m_ev_ln_g, m_ev_ln_b, m_ev_w_out, m_od_w_in, m_od_ln_g, m_od_ln_b, m_od_ws, m_od_bs, m_od_w_out, m_final_g, v_c_ctx, v_ada_w, v_ada_b, v_norm_g, v_ev_w_in, v_ev_q_norm, v_ev_k_norm, v_ev_dw_w, v_ev_dw_b, v_ev_ln_g, v_ev_ln_b, v_ev_w_out, v_od_w_in, v_od_ln_g, v_od_ln_b, v_od_ws, v_od_bs, v_od_w_out, v_final_g):
    weights = dict(c_ctx=c_ctx, ada_w=ada_w, ada_b=ada_b, norm_g=norm_g, ev_w_in=ev_w_in, ev_q_norm=ev_q_norm,
                   ev_k_norm=ev_k_norm, ev_dw_w=ev_dw_w, ev_dw_b=ev_dw_b, ev_ln_g=ev_ln_g, ev_ln_b=ev_ln_b,
                   ev_w_out=ev_w_out, od_w_in=od_w_in, od_ln_g=od_ln_g, od_ln_b=od_ln_b, od_ws=od_ws, od_bs=od_bs,
                   od_w_out=od_w_out, final_g=final_g)
    mom_m = dict(c_ctx=m_c_ctx, ada_w=m_ada_w, ada_b=m_ada_b, norm_g=m_norm_g, ev_w_in=m_ev_w_in,
                 ev_q_norm=m_ev_q_norm, ev_k_norm=m_ev_k_norm, ev_dw_w=m_ev_dw_w, ev_dw_b=m_ev_dw_b,
                 ev_ln_g=m_ev_ln_g, ev_ln_b=m_ev_ln_b, ev_w_out=m_ev_w_out, od_w_in=m_od_w_in, od_ln_g=m_od_ln_g,
                 od_ln_b=m_od_ln_b, od_ws=m_od_ws, od_bs=m_od_bs, od_w_out=m_od_w_out, final_g=m_final_g)
    mom_v = dict(c_ctx=v_c_ctx, ada_w=v_ada_w, ada_b=v_ada_b, norm_g=v_norm_g, ev_w_in=v_ev_w_in,
                 ev_q_norm=v_ev_q_norm, ev_k_norm=v_ev_k_norm, ev_dw_w=v_ev_dw_w, ev_dw_b=v_ev_dw_b,
                 ev_ln_g=v_ev_ln_g, ev_ln_b=v_ev_ln_b, ev_w_out=v_ev_w_out, od_w_in=v_od_w_in, od_ln_g=v_od_ln_g,
                 od_ln_b=v_od_ln_b, od_ws=v_od_ws, od_bs=v_od_bs, od_w_out=v_od_w_out, final_g=v_final_g)
    order = list(weights)

    _, n_lat, d = x.shape
    n_ctx = ctx.shape[1]
    n_ext = n_lat + n_ctx
    ev_in = ev_w_in.shape[-1] * N_CHIP
    ev_mix = ev_w_out.shape[1] * N_CHIP
    conv_ch = ev_dw_b.shape[-1]
    conv_w = ev_dw_w.shape[1]
    attn_w = ev_mix - conv_ch
    kv_w = N_KV_HEADS * HEAD_DIM
    assert ev_in == 2 * kv_w + 2 * attn_w + 3 * conv_ch and conv_w // 2 < CONV_HALO
    sgu_w = od_w_out.shape[1] * N_CHIP
    wa = ada_w.shape[-1]
    cb = math.gcd(2 * kv_w, attn_w, conv_ch)
    off = dict(k=0, v=kv_w, q=2 * kv_w, za=2 * kv_w + attn_w, a=2 * kv_w + 2 * attn_w,
               b=2 * kv_w + 2 * attn_w + conv_ch, zb=2 * kv_w + 2 * attn_w + 2 * conv_ch)
    dims = dict(kv_w=kv_w, attn_w=attn_w, conv_ch=conv_ch, conv_w=conv_w, cb=cb, off=off)
    tr = 256 if (n_lat % 256 == 0 and n_ctx % 256 == 0) else 128

    mx, my, mc = lax.axis_index("x"), lax.axis_index("y"), lax.axis_index("c")
    me = 4 * mx + 2 * my + mc
    chip = 2 * mx + my

    x2d, tgt2d, ctx2d = x[0], loss_target[0], ctx[0]
    ev_dw_w_l = ev_dw_w[0]
    dwc = ev_dw_w_l.shape[1]
    lnc = od_ln_g.shape[1]

    g_c = _allgather_small(jnp.broadcast_to(c, (8, d)), "gather_cond")[:, 0, :]
    c_rows = jnp.concatenate([g_c, c_ctx[None, :], jnp.zeros((MOD_ROWS - N_DEV - 1, d), F32)], axis=0)
    c_rows_t = c_rows.T
    ada_b_shard = lax.dynamic_slice_in_dim(ada_b, chip * wa, wa, axis=1)[:, None, :]
    mod_part = _mod_fwd(c_rows_t, ada_w, ada_b_shard)
    part_shapes = [(2, MOD_ROWS, wa), (conv_w, dwc), (1, lnc), (1, lnc)]
    g_parts = _allgather_small(_pack([mod_part, ev_dw_w_l, od_ln_g, od_ln_b]), "gather_mod")
    per_chip = [_unpack(g_parts[2 * s], part_shapes) for s in range(N_CHIP)]
    mod_all = jnp.concatenate([p[0] for p in per_chip], axis=-1)
    dw_w_full = jnp.concatenate([p[1] for p in per_chip], axis=-1)
    od_ln_g_full = jnp.concatenate([p[2] for p in per_chip], axis=-1)
    od_ln_b_full = jnp.concatenate([p[3] for p in per_chip], axis=-1)
    dw_w_pad = jnp.pad(dw_w_full, ((0, 2 * CONV_HALO - conv_w), (0, 0)))
    mod_me = lax.dynamic_slice_in_dim(mod_all, me, 1, axis=1)
    shift0, scale0, gate0 = mod_me[0, :, :d], mod_me[0, :, d:2 * d], mod_me[0, :, 2 * d:]
    shift1, scale1, gate1 = mod_me[1, :, :d], mod_me[1, :, d:2 * d], mod_me[1, :, 2 * d:]
    shift_c, scale_c = mod_all[0, N_DEV:N_DEV + 1, :d], mod_all[0, N_DEV:N_DEV + 1, d:2 * d]
    g0, g1 = norm_g[0:1], norm_g[1:2]

    lay = dict(ev_w_in=_Sharded((d, ev_in), True), ev_w_out=_Sharded((ev_mix, d), False),
               od_w_in=_Sharded((d, 3 * sgu_w), True), od_w_out=_Sharded((sgu_w, d), False))
    big = list(lay)
    chip_arr = jnp.reshape(chip, (1,)).astype(jnp.int32)
    core_arr = jnp.reshape(mc, (1,)).astype(jnp.int32)
    own = {n: _cast_into_full(weights[n][0], lay[n], chip_arr, f"cast_{n}") for n in big}
    w_full = {"ev_w_in": _run_comm(_gather_comm([own["ev_w_in"]], [lay["ev_w_in"]]), "gather_ev_w_in")[0]}
    later = big[1:]
    gather_later = _gather_comm([own[n] for n in later], [lay[n] for n in later])

    def reduce_start(g, n):
        theirs = _pair_exchange([g], [lay[n]], f"pair_exchange_{n}")[0]
        psum = _pair_sum(g, theirs, lay[n], core_arr, f"pair_sum_{n}")
        return psum, _scatter_comm([psum], [lay[n]])

    h0 = _adaln_fwd(x2d, g0, shift0, scale0, tr, "adaln0_fwd")
    hc = _adaln_fwd(ctx2d, g0, shift_c, scale_c, tr, "adaln0_ctx_fwd")
    h0e = jnp.concatenate([h0, hc], axis=0)
    tm_e = _pick(n_ext, (1408, 768, 640, 512, 256, 128))
    tk_e = _pick(n_ext, (768, 640, 512, 256, 128))
    tm_l = _pick(n_lat, (1024, 512, 256, 128))
    p0, *gathered = _mm(h0e, w_full["ev_w_in"], name="mm_ev_in", tm=tm_e, tn=_pick(ev_in, (512, 256, 128)), tk=d,
                        out_dtype=BF16, comm=gather_later)
    w_full.update(zip(later, gathered))
    cos_i, sin_s = _rope_tables(n_lat, n_ctx)
    q_hat, k_all, v_all = _qk_prep(p0, cos_i, sin_s, ev_q_norm, ev_k_norm, dims, tr)
    tq = _pick(n_lat, (256, 128))
    tkk = _pick(n_ext, (1408, 640, 512, 256, 128))
    attn, lse = _flash_fwd(q_hat, k_all, v_all, n_lat, dims, tq, tkk)
    mix, yc = _mix_fwd(attn, p0, dw_w_pad, ev_dw_b, ev_ln_g, ev_ln_b, dims, tr)
    o0, x1 = _mm(mix, w_full["ev_w_out"], name="mm_ev_out", tm=tm_l, tn=_pick(d, (1024, 512, 256)),
                 tk=ev_mix, out_dtype=BF16, res=x2d, gate=gate0)

    h1 = _adaln_fwd(x1, g1, shift1, scale1, tr, "adaln1_fwd")
    p1 = _mm(h1, w_full["od_w_in"], name="mm_od_in", tm=tm_l, tn=_pick(3 * sgu_w, (512, 256, 128)), tk=d,
             out_dtype=BF16)
    ws_b = od_ws[0].astype(BF16)
    ws_t_b = jnp.swapaxes(od_ws[0], 1, 2).astype(BF16)
    bs_t = od_bs[0].T
    m1 = _sgu_fwd(p1, od_ln_g_full, od_ln_b_full, ws_b, bs_t, tr)
    o1, x2 = _mm(m1, w_full["od_w_out"], name="mm_od_out", tm=tm_l, tn=_pick(d, (1024, 512, 256)),
                 tk=sgu_w, out_dtype=BF16, res=x1, gate=gate1)

    dx2, do1, loss_cols, d_final_g, dgate1 = _final_loss(x2, tgt2d, final_g[None, :], o1, gate1, tr)
    loss = lax.psum(0.5 / d * jnp.sum(loss_cols), ("x", "y", "c"))

    tk_l = _pick(n_lat, (1024, 512, 256, 128))
    psums, slots = {}, {}
    tk_nt = (1408, 1024, 768, 512, 256, 128)
    g_od_w_out = _mm(m1, do1, name="mm_od_out_dw", ta=True, tm=_pick(sgu_w, (1024, 512, 256)),
                     tn=_pick(d, (1024, 512, 256)), tk=tk_l, out_dtype=BF16)
    psums["od_w_out"], sc = reduce_start(g_od_w_out, "od_w_out")
    dm1, slots["od_w_out"] = _mm(do1, w_full["od_w_out"], name="mm_od_out_dx", tb=True, tm=tm_l,
                                 tn=_pick(sgu_w, (1024, 512, 256)), tk=d, out_dtype=BF16, comm=sc)
    dp1, d_ws, d_bs_t, d_od_ln_g, d_od_ln_b = _sgu_bwd(dm1, p1, od_ln_g_full, od_ln_b_full, ws_b, ws_t_b, bs_t, tr)
    g_od_w_in = _mm(h1, dp1, name="mm_od_in_dw", ta=True, tm=_pick(d, (1024, 512, 256)),
                    tn=_pick(3 * sgu_w, (1536, 768, 512, 384, 256, 128)), tk=tk_l, out_dtype=BF16)
    psums["od_w_in"], sc = reduce_start(g_od_w_in, "od_w_in")
    dh1, slots["od_w_in"] = _mm(dp1, w_full["od_w_in"], name="mm_od_in_dx", tb=True, tm=tm_l,
                                tn=_pick(d, (1024, 512, 256)), tk=_pick(3 * sgu_w, tk_nt), comm=sc)
    zero_d = jnp.zeros((1, d), F32)
    b1 = _adaln_bwd(x1, dh1, 0, g1, scale1, zero_d, tr, "adaln1_bwd", dres=dx2, o_prev=o0, gate_prev=gate0)
    dx1, do0, dgate0 = b1["dx"], b1["do_prev"], b1["dgate_prev"]

    g_ev_w_out = _mm(mix, do0, name="mm_ev_out_dw", ta=True, tm=_pick(ev_mix, (1024, 512, 256)),
                     tn=_pick(d, (1024, 512, 256)), tk=tk_l, out_dtype=BF16)
    psums["ev_w_out"], sc = reduce_start(g_ev_w_out, "ev_w_out")
    dmix, slots["ev_w_out"] = _mm(do0, w_full["ev_w_out"], name="mm_ev_out_dx", tb=True, tm=tm_l,
                                  tn=_pick(ev_mix, (1024, 512, 256)), tk=d, out_dtype=BF16, comm=sc)
    dattn, dza, dzb, dyc, d_ev_ln_g, d_ev_ln_b, d_dw_b = _mix_bwd_pointwise(
        dmix, attn, p0, yc, ev_ln_g, ev_ln_b, dims, tr, n_ext)
    dab, d_dw_w_pad = _conv_bwd(dyc, p0, dw_w_pad, dims, tr, n_ext)
    dq_hat, dk_hat, dv_all = _flash_bwd(q_hat, k_all, v_all, dattn, attn, lse, n_lat, dims, tq, tkk)
    dkvq, d_q_norm, d_k_norm = _qk_prep_bwd(dq_hat, dk_hat, dv_all, p0, cos_i, sin_s, ev_q_norm, ev_k_norm,
                                            dims, tr, n_lat)
    dp0 = jnp.concatenate([dkvq, dza, dab, dzb], axis=1)
    g_ev_w_in = _mm(h0e, dp0, name="mm_ev_in_dw", ta=True, tm=_pick(d, (1024, 512, 256)),
                    tn=_pick(ev_in, (1408, 768, 512, 256, 128)), tk=tk_e, out_dtype=BF16)
    psums["ev_w_in"], sc = reduce_start(g_ev_w_in, "ev_w_in")
    dh0, slots["ev_w_in"] = _mm(dp0, w_full["ev_w_in"], name="mm_ev_in_dx", tb=True, tm=tm_e,
                                tn=_pick(d, (1024, 512, 256)), tk=_pick(ev_in, tk_nt), comm=sc)
    bc = _adaln_bwd(ctx2d, dh0, n_lat, g0, scale_c, zero_d, tr, "adaln0_ctx_bwd")
    b0 = _adaln_bwd(x2d, dh0, 0, g0, scale0, bc["dg"], tr, "adaln0_bwd", dres=dx1)
    grad_x = b0["dx"]

    zeros_d = jnp.zeros((1, d), F32)
    dmod0 = jnp.concatenate([b0["dshift"], b0["dscale"], dgate0], axis=1)
    dmod1 = jnp.concatenate([b1["dshift"], b1["dscale"], dgate1], axis=1)
    dmodc = jnp.concatenate([bc["dshift"], bc["dscale"], zeros_d], axis=1)
    small = [dmod0, dmod1, dmodc, b0["dg"], b1["dg"], d_q_norm, d_k_norm, d_dw_w_pad[:conv_w], d_dw_b,
             d_ev_ln_g, d_ev_ln_b, d_od_ln_g, d_od_ln_b, d_ws, d_bs_t.T, d_final_g]
    small_shapes = [a.shape for a in small]
    g_small = _allgather_small(_pack(small), "gather_small_grads")
    tot = _unpack(_sum_slots(g_small, "sum_small_grads"), small_shapes)
    (t_dmod0, t_dmod1, t_dmodc, t_g0, t_g1, t_qn, t_kn, t_dw_w, t_dw_b, t_eln_g, t_eln_b, t_oln_g, t_oln_b,
     t_ws, t_bs, t_fg) = tot
    rows_dmod = _unpack(g_small, small_shapes[:2])
    dmod0_rows, dmod1_rows = rows_dmod[0][:, 0, :], rows_dmod[1][:, 0, :]
    pad_rows = jnp.zeros((MOD_ROWS - N_DEV - 1, 3 * d), F32)
    dm_l0 = jnp.concatenate([dmod0_rows, t_dmodc, pad_rows], axis=0)
    dm_l1 = jnp.concatenate([dmod1_rows, jnp.zeros((MOD_ROWS - N_DEV, 3 * d), F32)], axis=0)
    dm_shard = lax.dynamic_slice_in_dim(jnp.stack([dm_l0, dm_l1]), chip * wa, wa, axis=2)
    g_ada_w, dsc = _mod_bwd(c_rows_t, dm_shard, ada_w)
    g_dsc = _allgather_small(_pack([dsc[0]]), "gather_cctx")
    g_c_ctx = _cctx_grad(g_dsc, _pack([c_ctx])).reshape(-1)[:d]
    g_ada_b = jnp.stack([t_dmod0[0] + t_dmodc[0], t_dmod1[0]])

    lays = [lay[n] for n in big]
    halves = [_chip_sum(psums[n], slots[n], lay[n], chip_arr, core_arr, f"chip_sum_{n}") for n in big]
    g_big = dict(zip(big, _pair_share(halves, lays)))

    grads = dict(
        c_ctx=g_c_ctx, ada_w=g_ada_w, ada_b=g_ada_b, norm_g=jnp.concatenate([t_g0, t_g1], axis=0),
        ev_w_in=g_big["ev_w_in"][None], ev_q_norm=t_qn, ev_k_norm=t_kn,
        ev_dw_w=lax.dynamic_slice_in_dim(t_dw_w, chip * dwc, dwc, axis=1)[None], ev_dw_b=t_dw_b,
        ev_ln_g=t_eln_g, ev_ln_b=t_eln_b, ev_w_out=g_big["ev_w_out"][None], od_w_in=g_big["od_w_in"][None],
        od_ln_g=lax.dynamic_slice_in_dim(t_oln_g, chip * lnc, lnc, axis=1),
        od_ln_b=lax.dynamic_slice_in_dim(t_oln_b, chip * lnc, lnc, axis=1),
        od_ws=t_ws[None], od_bs=t_bs[None], od_w_out=g_big["od_w_out"][None], final_g=t_fg[0])
    grads = {n: grads[n].reshape(weights[n].shape) for n in order}

    delta, new_m, new_v = {}, {}, {}
    large = ("ada_w", "ev_w_in", "ev_w_out", "od_w_in", "od_w_out")
    for n in large:
        shp = weights[n].shape
        as2d = lambda a: a.reshape(-1, shp[-1])
        dl, nm, nv = _adamw(as2d(weights[n]), as2d(grads[n]), as2d(mom_m[n]), as2d(mom_v[n]), f"adamw_{n}")
        delta[n], new_m[n], new_v[n] = dl.reshape(shp), nm.reshape(shp), nv.reshape(shp)
    rest_names = [n for n in order if n not in large]
    rest_shapes = [weights[n].shape for n in rest_names]
    dl, nm, nv = _adamw(_pack([weights[n] for n in rest_names]), _pack([grads[n] for n in rest_names]),
                        _pack([mom_m[n] for n in rest_names]), _pack([mom_v[n] for n in rest_names]), "adamw_small")
    for n, a, b_, c_ in zip(rest_names, _unpack(dl, rest_shapes), _unpack(nm, rest_shapes), _unpack(nv, rest_shapes)):
        delta[n], new_m[n], new_v[n] = a, b_, c_

    return (loss, grad_x[None], *[grads[n] for n in order], *[delta[n] for n in order],
            *[new_m[n] for n in order], *[new_v[n] for n in order])
```

```python
import math

import jax
import jax.numpy as jnp
from jax import lax
from jax.experimental import pallas as pl
from jax.experimental.pallas import tpu as pltpu

F32 = jnp.float32
BF16 = jnp.bfloat16
EPS = 1e-6
GRID_W = 64
ROPE_THETA = 10000.0
HEAD_DIM = 128
N_KV_HEADS = 2
CONV_HALO = 16
LANES = 128
SUBLANES = 8
STENCIL_ROWS = 32
STENCIL_CHAINS = 4
REDUCE_ROWS = 32
N_DEV = 8
N_CHIP = 4
MOD_ROWS = 16
PACK_ROWS = 64
ADAM_LR, ADAM_B1, ADAM_B2, ADAM_EPS, ADAM_WD, ADAM_STEP = 0.001, 0.9, 0.999, 1e-08, 0.01, 10
VMEM_LIMIT = 56 * 1024 * 1024
MESH = pl.DeviceIdType.MESH
ANY = pl.BlockSpec(memory_space=pl.ANY)
VMEM_SPEC = pl.BlockSpec(memory_space=pltpu.VMEM)
CHIP_DELTAS = ((1, 0), (0, 1), (1, 1))


def _cp(sem=None):
    return pltpu.CompilerParams(dimension_semantics=sem, vmem_limit_bytes=VMEM_LIMIT)


def _pick(n, cands):
    for c in cands:
        if n % c == 0:
            return c
    raise ValueError(f"no tile for {n} in {cands}")


def _sigmoid(x):
    return 1.0 / (1.0 + jnp.exp(-x))


def _silu(x):
    return x * _sigmoid(x)


def _silu_and_grad(x):
    s = _sigmoid(x)
    y = x * s
    return y, s + y * (1.0 - s)


def _dsilu(x):
    return _silu_and_grad(x)[1]


_GELU_C = math.sqrt(2.0 / math.pi)
_GELU_A = 0.044715


def _gelu_and_grad(x):
    x2 = x * x
    t = jnp.tanh(x * (_GELU_C + (_GELU_C * _GELU_A) * x2))
    h = 0.5 + 0.5 * t
    return x * h, h + x * (1.0 - t * t) * (0.5 * _GELU_C + (1.5 * _GELU_C * _GELU_A) * x2)


def _gelu(x):
    x2 = x * x
    return x * (0.5 + 0.5 * jnp.tanh(x * (_GELU_C + (_GELU_C * _GELU_A) * x2)))


def _vec(d):
    return pl.BlockSpec((1, d), lambda *_: (0, 0))


def _cat(refs):
    parts = [r[...].astype(F32) for r in refs]
    return parts[0] if len(parts) == 1 else jnp.concatenate(parts, axis=1)


def _col_specs(rows, off, width, cb, row_map):
    assert off % cb == 0 and width % cb == 0
    return [pl.BlockSpec((rows, cb), (lambda *g, _c=off // cb + t: (row_map(*g), _c))) for t in range(width // cb)]


def _my_pos():
    return lax.axis_index("x"), lax.axis_index("y"), lax.axis_index("c")


def _allgather_small(x, name):
    r, c = x.shape

    def body(x_ref, out_ref, send_sems, recv_sems, local_sem):
        mx, my, mc = _my_pos()
        me = 4 * mx + 2 * my + mc
        mine = pltpu.make_async_copy(x_ref, out_ref.at[me], local_sem)
        mine.start()
        deltas = [(dx, dy, dc) for dx in (0, 1) for dy in (0, 1) for dc in (0, 1) if (dx, dy, dc) != (0, 0, 0)]
        sends = []
        for k, (dx, dy, dc) in enumerate(deltas):
            px, py, pc = (mx + dx) % 2, (my + dy) % 2, (mc + dc) % 2
            cp = pltpu.make_async_remote_copy(
                src_ref=x_ref, dst_ref=out_ref.at[me], send_sem=send_sems.at[k], recv_sem=recv_sems.at[k],
                device_id=(px, py, pc), device_id_type=MESH)
            cp.start()
            sends.append(cp)
        for k, (dx, dy, dc) in enumerate(deltas):
            px, py, pc = (mx + dx) % 2, (my + dy) % 2, (mc + dc) % 2
            peer = 4 * px + 2 * py + pc
            pltpu.make_async_remote_copy(
                src_ref=x_ref, dst_ref=out_ref.at[peer], send_sem=send_sems.at[k], recv_sem=recv_sems.at[k],
                device_id=(px, py, pc), device_id_type=MESH).wait_recv()
        for cp in sends:
            cp.wait_send()
        mine.wait()

    return pl.pallas_call(
        body, name=name,
        out_shape=jax.ShapeDtypeStruct((N_DEV, r, c), x.dtype),
        in_specs=[VMEM_SPEC], out_specs=VMEM_SPEC,
        scratch_shapes=[pltpu.SemaphoreType.DMA((N_DEV - 1,)), pltpu.SemaphoreType.DMA((N_DEV - 1,)),
                        pltpu.SemaphoreType.DMA],
        compiler_params=pltpu.CompilerParams(vmem_limit_bytes=VMEM_LIMIT),
    )(x)


class _Sharded:
    def __init__(self, full_shape, by_cols):
        self.full = full_shape
        self.by_cols = by_cols
        rows, cols = full_shape
        if by_cols:
            self.shard, self.half, self.halves = (rows, cols // N_CHIP), (rows // 2, cols // N_CHIP), (rows // 2, cols)
        else:
            self.shard, self.half, self.halves = (rows // N_CHIP, cols), (rows // N_CHIP, cols // 2), (rows, cols // 2)

    def region(self, ref, s, h):
        if self.by_cols:
            return ref.at[pl.ds(h * self.half[0], self.half[0]), pl.ds(s * self.shard[1], self.shard[1])]
        return ref.at[pl.ds(s * self.shard[0], self.shard[0]), pl.ds(h * self.half[1], self.half[1])]

    def halves_of_full(self, ref, h):
        if self.by_cols:
            return ref.at[pl.ds(h * self.halves[0], self.halves[0]), :]
        return ref.at[:, pl.ds(h * self.halves[1], self.halves[1])]

    def region_in_halves(self, ref, s):
        if self.by_cols:
            return ref.at[:, pl.ds(s * self.shard[1], self.shard[1])]
        return ref.at[pl.ds(s * self.shard[0], self.shard[0]), :]

    def half_of_shard(self, ref, h):
        if self.by_cols:
            return ref.at[pl.ds(h * self.half[0], self.half[0]), :]
        return ref.at[:, pl.ds(h * self.half[1], self.half[1])]


def _row_tile(rows, row_bytes):
    for t in (512, 256, 128, 64, 32, 16):
        if rows % t == 0 and t * row_bytes <= 2 * 1024 * 1024:
            return t
    return 16


def _cast_into_full(w_shard, lay, chip_arr, name):
    r, c = lay.shard
    tr = _row_tile(r, c * 4)
    nt = r // tr

    def body(chip_ref, w_ref, o_ref):
        o_ref[...] = w_ref[...].astype(BF16)

    if lay.by_cols:
        out_map = lambda i, chip_ref: (i, chip_ref[0])
    else:
        out_map = lambda i, chip_ref: (chip_ref[0] * nt + i, 0)
    return pl.pallas_call(
        body, name=name,
        grid_spec=pltpu.PrefetchScalarGridSpec(
            num_scalar_prefetch=1, grid=(nt,),
            in_specs=[pl.BlockSpec((tr, c), lambda i, chip_ref: (i, 0))],
            out_specs=pl.BlockSpec((tr, c), out_map)),
        out_shape=jax.ShapeDtypeStruct(lay.full, BF16), compiler_params=_cp(("parallel",)))(chip_arr, w_shard)


class _Carried:
    def __init__(self, ins, out_shapes, aliases, sem_shape, start, finish):
        self.ins, self.out_shapes, self.aliases, self.sem_shape = list(ins), list(out_shapes), dict(aliases), sem_shape
        self.start, self.finish = start, finish

    def scratch(self):
        return [pltpu.SemaphoreType.DMA(self.sem_shape), pltpu.SemaphoreType.DMA(self.sem_shape)]

    def split(self, in_refs, out_refs, scratch_refs):
        ni, no = len(self.ins), len(self.out_shapes)
        return in_refs[len(in_refs) - ni:], out_refs[len(out_refs) - no:], scratch_refs[-2], scratch_refs[-1]


def _call(body, *, name, grid, in_specs, out_specs, out_shape, args, sem, comm=None):
    if comm is None:
        return pl.pallas_call(body, name=name, grid=grid, in_specs=in_specs, out_specs=out_specs,
                              out_shape=out_shape, compiler_params=_cp(sem))(*args)
    n_in, n_out, n_ci, n_co = len(in_specs), len(out_specs), len(comm.ins), len(comm.out_shapes)

    def carrying(*refs):
        in_refs, out_refs = refs[:n_in + n_ci], refs[n_in + n_ci:n_in + n_ci + n_out + n_co]
        carried = comm.split(in_refs, out_refs, refs[n_in + n_ci + n_out + n_co:])
        first, last = None, None
        for axis, extent in enumerate(grid):
            at0, at1 = pl.program_id(axis) == 0, pl.program_id(axis) == extent - 1
            first = at0 if first is None else jnp.logical_and(first, at0)
            last = at1 if last is None else jnp.logical_and(last, at1)

        @pl.when(first)
        def _():
            comm.start(*carried)

        body(*in_refs[:n_in], *out_refs[:n_out])

        @pl.when(last)
        def _():
            comm.finish(*carried)

    return pl.pallas_call(
        carrying, name=name, grid=grid, in_specs=list(in_specs) + [ANY] * n_ci,
        out_specs=list(out_specs) + [ANY] * n_co, out_shape=list(out_shape) + comm.out_shapes,
        scratch_shapes=comm.scratch(), input_output_aliases={n_in + s: n_out + d for s, d in comm.aliases.items()},
        compiler_params=_cp(("arbitrary",) * len(grid)))(*args, *comm.ins)


def _gather_comm(fulls, layouts):
    n = len(fulls)

    def ici(ins, outs, send_sems, recv_sems, a, j, landed=False):
        mx, my, mc = _my_pos()
        dx, dy = CHIP_DELTAS[j]
        px, py = (mx + dx) % 2, (my + dy) % 2
        src_chip = 2 * px + py if landed else 2 * mx + my
        return pltpu.make_async_remote_copy(
            src_ref=layouts[a].region(ins[a], src_chip, mc), dst_ref=layouts[a].region(outs[a], src_chip, mc),
            send_sem=send_sems.at[a, j], recv_sem=recv_sems.at[a, j], device_id=(px, py, mc), device_id_type=MESH)

    def d2d(ins, outs, send_sems, recv_sems, a, j, landed=False):
        mx, my, mc = _my_pos()
        dx, dy = CHIP_DELTAS[j]
        other = 2 * ((mx + dx) % 2) + (my + dy) % 2
        half = 1 - mc if landed else mc
        region = layouts[a].region(outs[a], other, half)
        return pltpu.make_async_remote_copy(
            src_ref=region, dst_ref=region, send_sem=send_sems.at[a, 3 + j], recv_sem=recv_sems.at[a, 3 + j],
            device_id=(mx, my, 1 - mc), device_id_type=MESH)

    pairs = [(a, j) for a in range(n) for j in range(3)]

    def start(*r):
        for a, j in pairs:
            ici(*r, a, j).start()

    def finish(*r):
        for a, j in pairs:
            ici(*r, a, j, landed=True).wait_recv()
            d2d(*r, a, j).start()
        for a, j in pairs:
            d2d(*r, a, j, landed=True).wait_recv()
        for a, j in pairs:
            ici(*r, a, j).wait_send()
            d2d(*r, a, j).wait_send()

    return _Carried(fulls, [jax.ShapeDtypeStruct(lay.full, BF16) for lay in layouts], {a: a for a in range(n)},
                    (n, 6), start, finish)


def _pair_exchange(grads, layouts, name):
    n = len(grads)

    def body(*refs):
        ins, outs = refs[:n], refs[n:2 * n]
        send_sems, recv_sems = refs[2 * n:]
        mx, my, mc = _my_pos()
        copies = []
        for a, lay in enumerate(layouts):
            cp = pltpu.make_async_remote_copy(
                src_ref=lay.halves_of_full(ins[a], 1 - mc), dst_ref=outs[a],
                send_sem=send_sems.at[a], recv_sem=recv_sems.at[a],
                device_id=(mx, my, 1 - mc), device_id_type=MESH)
            cp.start()
            copies.append(cp)
        for cp in copies:
            cp.wait()

    return pl.pallas_call(
        body, name=name,
        out_shape=[jax.ShapeDtypeStruct(lay.halves, g.dtype) for lay, g in zip(layouts, grads)],
        in_specs=[ANY] * n, out_specs=[ANY] * n,
        scratch_shapes=[pltpu.SemaphoreType.DMA((n,)), pltpu.SemaphoreType.DMA((n,))],
    )(*grads)


def _pair_sum(g, theirs, lay, core_arr, name):
    r, c = lay.halves
    tr = _row_tile(r, c * 4)
    nt = r // tr

    def body(core_ref, g_ref, t_ref, o_ref):
        o_ref[...] = (g_ref[...].astype(F32) + t_ref[...].astype(F32)).astype(BF16)

    if lay.by_cols:
        g_map = lambda i, core_ref: (core_ref[0] * nt + i, 0)
    else:
        g_map = lambda i, core_ref: (i, core_ref[0])
    plain = pl.BlockSpec((tr, c), lambda i, core_ref: (i, 0))
    return pl.pallas_call(
        body, name=name,
        grid_spec=pltpu.PrefetchScalarGridSpec(
            num_scalar_prefetch=1, grid=(nt,), in_specs=[pl.BlockSpec((tr, c), g_map), plain], out_specs=plain),
        out_shape=jax.ShapeDtypeStruct((r, c), BF16), compiler_params=_cp(("parallel",)))(core_arr, g, theirs)


def _scatter_comm(pair_sums, layouts):
    n = len(pair_sums)

    def copy(ins, outs, send_sems, recv_sems, a, j):
        mx, my, mc = _my_pos()
        dx, dy = CHIP_DELTAS[j]
        px, py = (mx + dx) % 2, (my + dy) % 2
        return pltpu.make_async_remote_copy(
            src_ref=layouts[a].region_in_halves(ins[a], 2 * px + py), dst_ref=outs[a].at[j],
            send_sem=send_sems.at[a, j], recv_sem=recv_sems.at[a, j], device_id=(px, py, mc), device_id_type=MESH)

    pairs = [(a, j) for a in range(n) for j in range(3)]

    def start(*r):
        for a, j in pairs:
            copy(*r, a, j).start()

    def finish(*r):
        for a, j in pairs:
            copy(*r, a, j).wait()

    return _Carried(pair_sums, [jax.ShapeDtypeStruct((3,) + lay.half, BF16) for lay in layouts], {}, (n, 3),
                    start, finish)


def _chip_sum(pair_sum, slots, lay, chip_arr, core_arr, name):
    r, c = lay.half
    tr = _row_tile(r, c * 4)
    nt = r // tr

    def body(chip_ref, core_ref, s_ref, slot_ref, o_ref):
        acc = s_ref[...].astype(F32)
        for j in range(3):
            acc = acc + slot_ref[j].astype(F32)
        o_ref[...] = acc

    if lay.by_cols:
        s_map = lambda i, chip_ref, core_ref: (i, chip_ref[0])
        o_map = lambda i, chip_ref, core_ref: (core_ref[0] * nt + i, 0)
    else:
        s_map = lambda i, chip_ref, core_ref: (chip_ref[0] * nt + i, 0)
        o_map = lambda i, chip_ref, core_ref: (i, core_ref[0])
    return pl.pallas_call(
        body, name=name,
        grid_spec=pltpu.PrefetchScalarGridSpec(
            num_scalar_prefetch=2, grid=(nt,),
            in_specs=[pl.BlockSpec((tr, c), s_map),
                      pl.BlockSpec((3, tr, c), lambda i, chip_ref, core_ref: (0, i, 0))],
            out_specs=pl.BlockSpec((tr, c), o_map)),
        out_shape=jax.ShapeDtypeStruct(lay.shard, F32), compiler_params=_cp(("parallel",)))(
            chip_arr, core_arr, pair_sum, slots)


def _share_comm(bufs, layouts):
    n = len(bufs)

    def copy(ins, outs, send_sems, recv_sems, a, landed=False):
        mx, my, mc = _my_pos()
        half = 1 - mc if landed else mc
        return pltpu.make_async_remote_copy(
            src_ref=layouts[a].half_of_shard(ins[a], half), dst_ref=layouts[a].half_of_shard(outs[a], half),
            send_sem=send_sems.at[a], recv_sem=recv_sems.at[a], device_id=(mx, my, 1 - mc), device_id_type=MESH)

    def start(*r):
        for a in range(n):
            copy(*r, a).start()

    def finish(*r):
        for a in range(n):
            copy(*r, a, landed=True).wait_recv()
        for a in range(n):
            copy(*r, a).wait_send()

    return _Carried(bufs, [jax.ShapeDtypeStruct(lay.shard, F32) for lay in layouts], {a: a for a in range(n)},
                    (n,), start, finish)


def _allgather_comm(x):
    deltas = [(dx, dy, dc) for dx in (0, 1) for dy in (0, 1) for dc in (0, 1) if (dx, dy, dc) != (0, 0, 0)]
    local = len(deltas)

    def remote(ins, outs, send_sems, recv_sems, k, landed=False):
        mx, my, mc = _my_pos()
        dx, dy, dc = deltas[k]
        px, py, pc = (mx + dx) % 2, (my + dy) % 2, (mc + dc) % 2
        slot = 4 * px + 2 * py + pc if landed else 4 * mx + 2 * my + mc
        return pltpu.make_async_remote_copy(
            src_ref=ins[0], dst_ref=outs[0].at[slot], send_sem=send_sems.at[k], recv_sem=recv_sems.at[k],
            device_id=(px, py, pc), device_id_type=MESH)

    def mine(ins, outs, send_sems, recv_sems):
        mx, my, mc = _my_pos()
        return pltpu.make_async_copy(ins[0], outs[0].at[4 * mx + 2 * my + mc], send_sems.at[local])

    def start(*r):
        mine(*r).start()
        for k in range(len(deltas)):
            remote(*r, k).start()

    def finish(*r):
        for k in range(len(deltas)):
            remote(*r, k, landed=True).wait_recv()
        for k in range(len(deltas)):
            remote(*r, k).wait_send()
        mine(*r).wait()

    return _Carried([x], [jax.ShapeDtypeStruct((N_DEV,) + x.shape, x.dtype)], {}, (N_DEV,), start, finish)


def _sum_slots(x, name):
    s, r, c = x.shape
    tr = _pick(r, (256, 128, 64, 32, 16, 8))

    def body(x_ref, o_ref):
        acc = x_ref[0]
        for k in range(1, s):
            acc = acc + x_ref[k]
        o_ref[...] = acc

    return pl.pallas_call(body, name=name, grid=(r // tr,),
                          in_specs=[pl.BlockSpec((s, tr, c), lambda i: (0, i, 0))],
                          out_specs=pl.BlockSpec((tr, c), lambda i: (i, 0)),
                          out_shape=jax.ShapeDtypeStruct((r, c), F32), compiler_params=_cp(("parallel",)))(x)


def _adamw(w, g, m, v, name, comm=None):
    r, c = w.shape
    tr = _pick(r, (256, 128, 64, 32, 16, 8))
    bc1 = 1.0 - ADAM_B1 ** ADAM_STEP
    bc2 = 1.0 - ADAM_B2 ** ADAM_STEP

    def body(w_ref, g_ref, m_ref, v_ref, d_ref, nm_ref, nv_ref):
        gv = g_ref[...]
        nm = ADAM_B1 * m_ref[...] + (1.0 - ADAM_B1) * gv
        nv = ADAM_B2 * v_ref[...] + (1.0 - ADAM_B2) * (gv * gv)
        d_ref[...] = -ADAM_LR * ((nm / bc1) / (jnp.sqrt(nv / bc2) + ADAM_EPS) + ADAM_WD * w_ref[...])
        nm_ref[...] = nm
        nv_ref[...] = nv

    spec = pl.BlockSpec((tr, c), lambda i: (i, 0))
    shp = jax.ShapeDtypeStruct((r, c), F32)
    return _call(body, name=name, grid=(r // tr,), in_specs=[spec] * 4, out_specs=[spec] * 3, out_shape=[shp] * 3,
                 args=(w, g, m, v), sem=("parallel",), comm=comm)


def _cctx_grad(parts, c_ctx2d):
    def body(p_ref, c_ref, o_ref):
        tot = ((p_ref[0] + p_ref[2]) + p_ref[4]) + p_ref[6]
        o_ref[...] = tot * _dsilu(c_ref[...])

    return pl.pallas_call(body, name="cctx_grad", in_specs=[VMEM_SPEC, VMEM_SPEC], out_specs=VMEM_SPEC,
                          out_shape=jax.ShapeDtypeStruct(c_ctx2d.shape, F32))(parts, c_ctx2d)


def _mod_fwd(c_rows_t, ada_w, ada_b_shard):
    nl, d, w = ada_w.shape
    td = _pick(d, (256, 128))
    nd = d // td

    def body(ct_ref, w_ref, b_ref, o_ref):
        i = pl.program_id(1)

        @pl.when(i == 0)
        def _():
            o_ref[0] = jnp.broadcast_to(b_ref[0], (MOD_ROWS, w))

        st = _silu(ct_ref[...])
        wv = w_ref[0]
        rows = [jnp.sum(st[:, r:r + 1] * wv, axis=0, keepdims=True) for r in range(MOD_ROWS)]
        o_ref[0] += jnp.concatenate(rows, axis=0)

    return pl.pallas_call(
        body, name="mod_fwd", grid=(nl, nd),
        in_specs=[pl.BlockSpec((td, MOD_ROWS), lambda l, i: (i, 0)),
                  pl.BlockSpec((1, td, w), lambda l, i: (l, i, 0)),
                  pl.BlockSpec((1, 1, w), lambda l, i: (l, 0, 0))],
        out_specs=pl.BlockSpec((1, MOD_ROWS, w), lambda l, i: (l, 0, 0)),
        out_shape=jax.ShapeDtypeStruct((nl, MOD_ROWS, w), F32),
        compiler_params=_cp(("parallel", "arbitrary")),
    )(c_rows_t, ada_w, ada_b_shard)


def _mod_bwd(c_rows_t, dmod, ada_w):
    nl, d, w = ada_w.shape
    td = _pick(d, (256, 128))
    ctx_row = N_DEV

    def body(ct_ref, dm_ref, w_ref, gw_ref, ds_ref):
        st = _silu(ct_ref[...])
        dm = dm_ref[0]
        acc = st[:, 0:1] * dm[0:1, :]
        for r in range(1, ctx_row + 1):
            acc = acc + st[:, r:r + 1] * dm[r:r + 1, :]
        gw_ref[0] = acc
        ds_ref[0] = jnp.sum(w_ref[0] * dm[ctx_row:ctx_row + 1, :], axis=1, keepdims=True)

    return pl.pallas_call(
        body, name="mod_bwd", grid=(nl, d // td),
        in_specs=[pl.BlockSpec((td, MOD_ROWS), lambda l, i: (i, 0)),
                  pl.BlockSpec((1, MOD_ROWS, w), lambda l, i: (l, 0, 0)),
                  pl.BlockSpec((1, td, w), lambda l, i: (l, i, 0))],
        out_specs=[pl.BlockSpec((1, td, w), lambda l, i: (l, i, 0)),
                   pl.BlockSpec((1, td, 1), lambda l, i: (l, i, 0))],
        out_shape=[jax.ShapeDtypeStruct((nl, d, w), F32), jax.ShapeDtypeStruct((nl, d, 1), F32)],
        compiler_params=_cp(("parallel", "parallel")),
    )(c_rows_t, dmod, ada_w)


def _adaln_fwd(x, g, shift, scale, tr, name, comm=None):
    r, d = x.shape

    def body(x_ref, g_ref, sh_ref, sc_ref, o_ref):
        xv = x_ref[...]
        rs = lax.rsqrt(jnp.mean(xv * xv, axis=-1, keepdims=True) + EPS)
        o_ref[...] = ((xv * rs * g_ref[...]) * (1.0 + sc_ref[...]) + sh_ref[...]).astype(BF16)

    spec = pl.BlockSpec((tr, d), lambda i: (i, 0))
    outs = _call(body, name=name, grid=(r // tr,), in_specs=[spec, _vec(d), _vec(d), _vec(d)], out_specs=[spec],
                 out_shape=[jax.ShapeDtypeStruct((r, d), BF16)], args=(x, g, shift, scale), sem=("parallel",),
                 comm=comm)
    return outs if comm else outs[0]


def _adaln_bwd(xin, dh, row0, g, scale, dg_init, tr, name, dres=None, o_prev=None, gate_prev=None, comm=None):
    r, d = xin.shape
    assert row0 % tr == 0
    rb0 = row0 // tr
    want_dx = dres is not None
    want_prev = o_prev is not None
    assert want_dx or not want_prev

    def body(*refs):
        it = iter(refs)
        x_ref, dh_ref, g_ref, sc_ref, dgi_ref = next(it), next(it), next(it), next(it), next(it)
        dres_ref = next(it) if want_dx else None
        o_ref, gp_ref = (next(it), next(it)) if want_prev else (None, None)
        dx_ref = next(it) if want_dx else None
        do_ref = next(it) if want_prev else None
        dsh_ref, dsc_ref, dg_ref = next(it), next(it), next(it)
        dgp_ref = next(it) if want_prev else None
        i = pl.program_id(0)

        @pl.when(i == 0)
        def _():
            dsh_ref[...] = jnp.zeros_like(dsh_ref)
            dsc_ref[...] = jnp.zeros_like(dsc_ref)
            dg_ref[...] = dgi_ref[...]
            if want_prev:
                dgp_ref[...] = jnp.zeros_like(dgp_ref)

        xv = x_ref[...]
        dhv = dh_ref[...].astype(F32)
        gv = g_ref[...]
        rs = lax.rsqrt(jnp.mean(xv * xv, axis=-1, keepdims=True) + EPS)
        xn = xv * rs
        dsh_ref[...] += jnp.sum(dhv, axis=0, keepdims=True)
        dsc_ref[...] += jnp.sum(dhv * (xn * gv), axis=0, keepdims=True)
        dr = dhv * (1.0 + sc_ref[...])
        dg_ref[...] += jnp.sum(dr * xn, axis=0, keepdims=True)
        if want_dx:
            gy = dr * gv
            dx = dres_ref[...] + rs * (gy - xn * jnp.mean(gy * xn, axis=-1, keepdims=True))
            dx_ref[...] = dx
            if want_prev:
                do_ref[...] = (gp_ref[...] * dx).astype(BF16)
                dgp_ref[...] += jnp.sum(dx * o_ref[...].astype(F32), axis=0, keepdims=True)

    row = pl.BlockSpec((tr, d), lambda i: (i, 0))
    in_specs = [row, pl.BlockSpec((tr, d), lambda i: (rb0 + i, 0)), _vec(d), _vec(d), _vec(d)]
    args = [xin, dh, g, scale, dg_init]
    out_specs, out_shape, names = [], [], []
    if want_dx:
        in_specs.append(row)
        args.append(dres)
    if want_prev:
        in_specs += [row, _vec(d)]
        args += [o_prev, gate_prev]
    if want_dx:
        out_specs.append(row)
        out_shape.append(jax.ShapeDtypeStruct((r, d), F32))
        names.append("dx")
    if want_prev:
        out_specs.append(row)
        out_shape.append(jax.ShapeDtypeStruct((r, d), BF16))
        names.append("do_prev")
    for nm in ("dshift", "dscale", "dg") + (("dgate_prev",) if want_prev else ()):
        out_specs.append(_vec(d))
        out_shape.append(jax.ShapeDtypeStruct((1, d), F32))
        names.append(nm)
    outs = _call(body, name=name, grid=(r // tr,), in_specs=in_specs, out_specs=out_specs, out_shape=out_shape,
                 args=args, sem=("arbitrary",), comm=comm)
    res = dict(zip(names, outs))
    if comm:
        res["carried"] = outs[len(names):]
    return res


def _mm(a, b, *, name, tm, tn, tk, ta=False, tb=False, out_dtype=F32, res=None, gate=None, comm=None):
    if ta:
        kd, m = a.shape
    else:
        m, kd = a.shape
    if tb:
        n, kd2 = b.shape
    else:
        kd2, n = b.shape
    assert kd == kd2 and m % tm == 0 and n % tn == 0 and kd % tk == 0, (a.shape, b.shape, tm, tn, tk)
    ni, nj, nk = m // tm, n // tn, kd // tk
    dn = (((0 if ta else 1,), (1 if tb else 0,)), ((), ()))
    with_res = res is not None
    n_in = 4 if with_res else 2
    n_out = 2 if with_res else 1
    n_cin = len(comm.ins) if comm else 0
    n_cout = len(comm.out_shapes) if comm else 0

    def body(*refs):
        in_refs = refs[:n_in + n_cin]
        out_refs = refs[n_in + n_cin:n_in + n_cin + n_out + n_cout]
        scratch = refs[n_in + n_cin + n_out + n_cout:]
        a_ref, b_ref = in_refs[0], in_refs[1]
        o_ref = out_refs[0]
        i, j, k = pl.program_id(0), pl.program_id(1), pl.program_id(2)
        if comm:
            carried = comm.split(in_refs, out_refs, scratch)

            @pl.when(jnp.logical_and(jnp.logical_and(i == 0, j == 0), k == 0))
            def _():
                comm.start(*carried)

        def emit(acc):
            o_ref[...] = acc.astype(o_ref.dtype)
            if with_res:
                out_refs[1][...] = in_refs[2][...] + in_refs[3][...] * acc

        if nk == 1:
            emit(lax.dot_general(a_ref[...], b_ref[...], dn, preferred_element_type=F32))
        else:
            acc_ref = scratch[0]

            @pl.when(k == 0)
            def _():
                acc_ref[...] = jnp.zeros_like(acc_ref)

            acc_ref[...] += lax.dot_general(a_ref[...], b_ref[...], dn, preferred_element_type=F32)

            @pl.when(k == nk - 1)
            def _():
                emit(acc_ref[...])

        if comm:
            @pl.when(jnp.logical_and(jnp.logical_and(i == ni - 1, j == nj - 1), k == nk - 1))
            def _():
                comm.finish(*carried)

    a_spec = pl.BlockSpec((tk, tm), lambda i, j, k: (k, i)) if ta else pl.BlockSpec((tm, tk), lambda i, j, k: (i, k))
    b_spec = pl.BlockSpec((tn, tk), lambda i, j, k: (j, k)) if tb else pl.BlockSpec((tk, tn), lambda i, j, k: (k, j))
    o_spec = pl.BlockSpec((tm, tn), lambda i, j, k: (i, j))
    in_specs, args = [a_spec, b_spec], [a, b]
    out_specs, out_shape = [o_spec], [jax.ShapeDtypeStruct((m, n), out_dtype)]
    if with_res:
        in_specs += [o_spec, pl.BlockSpec((1, tn), lambda i, j, k: (0, j))]
        args += [res, gate]
        out_specs.append(o_spec)
        out_shape.append(jax.ShapeDtypeStruct((m, n), F32))
    scratch_shapes = [pltpu.VMEM((tm, tn), F32)] if nk > 1 else []
    aliases = {}
    sem = ("parallel", "parallel", "arbitrary")
    if comm:
        in_specs += [ANY] * n_cin
        args += comm.ins
        out_specs += [ANY] * n_cout
        out_shape += comm.out_shapes
        scratch_shapes += comm.scratch()
        aliases = {n_in + s: n_out + d for s, d in comm.aliases.items()}
        sem = ("arbitrary", "arbitrary", "arbitrary")
    outs = pl.pallas_call(body, name=name, grid=(ni, nj, nk), in_specs=in_specs, out_specs=out_specs,
                          out_shape=out_shape, scratch_shapes=scratch_shapes, input_output_aliases=aliases,
                          compiler_params=_cp(sem))(*args)
    return outs if (with_res or comm) else outs[0]


def _swap_pairs(x):
    lane = lax.broadcasted_iota(jnp.int32, x.shape, 1)
    return jnp.where(lane % 2 == 0, pltpu.roll(x, HEAD_DIM - 1, 1), pltpu.roll(x, 1, 1))


def _qk_prep(p0, cos_i, sin_s, q_norm, k_norm, dims, tr):
    rows = p0.shape[0]
    kvw, aw, cb = dims["kv_w"], dims["attn_w"], dims["cb"]
    nkv, nq = kvw // HEAD_DIM, aw // HEAD_DIM
    scale = HEAD_DIM ** -0.5
    n_kv_specs, n_q_specs = (2 * kvw) // cb, aw // cb

    def body(*refs):
        kv_refs = refs[:n_kv_specs]
        q_refs = refs[n_kv_specs:n_kv_specs + n_q_specs]
        cos_ref, sin_ref, qn_ref, kn_ref, qo_ref, ko_ref, vo_ref = refs[n_kv_specs + n_q_specs:]
        kv = _cat(kv_refs)
        qv = _cat(q_refs)
        cs, sn = cos_ref[...], sin_ref[...]

        def norm_rope(xh, gvec):
            rs = lax.rsqrt(jnp.mean(xh * xh, axis=-1, keepdims=True) + EPS)
            xn = xh * rs * gvec
            return xn * cs + _swap_pairs(xn) * sn

        for h in range(nkv):
            sl = slice(h * HEAD_DIM, (h + 1) * HEAD_DIM)
            ko_ref[:, sl] = norm_rope(kv[:, sl], kn_ref[...]).astype(BF16)
        vo_ref[...] = kv[:, kvw:].astype(BF16)
        for h in range(nq):
            sl = slice(h * HEAD_DIM, (h + 1) * HEAD_DIM)
            qo_ref[:, sl] = (norm_rope(qv[:, sl], qn_ref[...]) * scale).astype(BF16)

    rm = lambda i: i
    in_specs = (_col_specs(tr, 0, 2 * kvw, cb, rm) + _col_specs(tr, 2 * kvw, aw, cb, rm)
                + [pl.BlockSpec((tr, HEAD_DIM), lambda i: (i, 0))] * 2 + [_vec(HEAD_DIM)] * 2)
    args = [p0] * (n_kv_specs + n_q_specs) + [cos_i, sin_s, q_norm, k_norm]
    return pl.pallas_call(
        body, name="qk_prep", grid=(rows // tr,), in_specs=in_specs,
        out_specs=[pl.BlockSpec((tr, aw), lambda i: (i, 0)), pl.BlockSpec((tr, kvw), lambda i: (i, 0)),
                   pl.BlockSpec((tr, kvw), lambda i: (i, 0))],
        out_shape=[jax.ShapeDtypeStruct((rows, aw), BF16), jax.ShapeDtypeStruct((rows, kvw), BF16),
                   jax.ShapeDtypeStruct((rows, kvw), BF16)],
        compiler_params=_cp(("parallel",)))(*args)


def _qk_prep_bwd(dq_hat, dk_hat, dv, p0, cos_i, sin_s, q_norm, k_norm, dims, tr, n_lat):
    rows = p0.shape[0]
    kvw, aw, cb = dims["kv_w"], dims["attn_w"], dims["cb"]
    nkv, nq = kvw // HEAD_DIM, aw // HEAD_DIM
    scale = HEAD_DIM ** -0.5
    n_kv_specs, n_q_specs = (2 * kvw) // cb, aw // cb
    lat_tiles = n_lat // tr

    def body(*refs):
        kv_refs = refs[:n_kv_specs]
        q_refs = refs[n_kv_specs:n_kv_specs + n_q_specs]
        (dq_ref, dk_ref, dv_ref, cos_ref, sin_ref, qn_ref, kn_ref,
         out_ref, dqn_ref, dkn_ref) = refs[n_kv_specs + n_q_specs:]
        i = pl.program_id(0)

        @pl.when(i == 0)
        def _():
            dqn_ref[...] = jnp.zeros_like(dqn_ref)
            dkn_ref[...] = jnp.zeros_like(dkn_ref)

        kv = _cat(kv_refs)
        qv = _cat(q_refs)
        cs, sn = cos_ref[...], sin_ref[...]
        is_lat = (i < lat_tiles).astype(F32)

        def head_bwd(xh, dhat, gvec):
            dn = dhat * cs + _swap_pairs(dhat * sn)
            rs = lax.rsqrt(jnp.mean(xh * xh, axis=-1, keepdims=True) + EPS)
            xn = xh * rs
            gy = dn * gvec
            dx = rs * (gy - xn * jnp.mean(gy * xn, axis=-1, keepdims=True))
            return dx, jnp.sum(dn * xn, axis=0, keepdims=True)

        dkn = jnp.zeros((1, HEAD_DIM), F32)
        for h in range(nkv):
            sl = slice(h * HEAD_DIM, (h + 1) * HEAD_DIM)
            dx, dgv = head_bwd(kv[:, sl], dk_ref[:, sl], kn_ref[...])
            out_ref[:, sl] = dx.astype(BF16)
            dkn = dkn + dgv
        dkn_ref[...] += dkn
        out_ref[:, kvw:2 * kvw] = dv_ref[...].astype(BF16)
        dqn = jnp.zeros((1, HEAD_DIM), F32)
        for h in range(nq):
            sl = slice(h * HEAD_DIM, (h + 1) * HEAD_DIM)
            dx, dgv = head_bwd(qv[:, sl], dq_ref[:, sl] * (scale * is_lat), qn_ref[...])
            out_ref[:, 2 * kvw + h * HEAD_DIM:2 * kvw + (h + 1) * HEAD_DIM] = dx.astype(BF16)
            dqn = dqn + dgv
        dqn_ref[...] += dqn

    rm = lambda i: i
    wout = 2 * kvw + aw
    in_specs = (_col_specs(tr, 0, 2 * kvw, cb, rm) + _col_specs(tr, 2 * kvw, aw, cb, rm)
                + [pl.BlockSpec((tr, aw), lambda i: (jnp.minimum(i, lat_tiles - 1), 0)),
                   pl.BlockSpec((tr, kvw), lambda i: (i, 0)), pl.BlockSpec((tr, kvw), lambda i: (i, 0)),
                   pl.BlockSpec((tr, HEAD_DIM), lambda i: (i, 0)), pl.BlockSpec((tr, HEAD_DIM), lambda i: (i, 0)),
                   _vec(HEAD_DIM), _vec(HEAD_DIM)])
    args = [p0] * (n_kv_specs + n_q_specs) + [dq_hat, dk_hat, dv, cos_i, sin_s, q_norm, k_norm]
    return pl.pallas_call(
        body, name="qk_prep_bwd", grid=(rows // tr,), in_specs=in_specs,
        out_specs=[pl.BlockSpec((tr, wout), lambda i: (i, 0)), _vec(HEAD_DIM), _vec(HEAD_DIM)],
        out_shape=[jax.ShapeDtypeStruct((rows, wout), BF16), jax.ShapeDtypeStruct((1, HEAD_DIM), F32),
                   jax.ShapeDtypeStruct((1, HEAD_DIM), F32)],
        compiler_params=_cp(("arbitrary",)))(*args)


def _stack_heads(x, g):
    return jnp.concatenate([x[:, h * HEAD_DIM:(h + 1) * HEAD_DIM] for h in range(g)], axis=0)


def _flash_fwd(q_hat, k_all, v_all, n_lat, dims, tq, tk):
    kvw, aw = dims["kv_w"], dims["attn_w"]
    nkv = kvw // HEAD_DIM
    g = aw // kvw
    gw = g * HEAD_DIM
    n_keys = k_all.shape[0]
    ni, nj = n_lat // tq, n_keys // tk
    dn_nt = (((1,), (1,)), ((), ()))

    def body(q_ref, k_ref, v_ref, o_ref, lse_ref):
        qs = _stack_heads(q_ref[...], g)
        m = jnp.full((g * tq, 1), -1e30, F32)
        l = jnp.zeros((g * tq, 1), F32)
        acc = jnp.zeros((g * tq, HEAD_DIM), F32)
        for j in range(nj):
            kb = k_ref[pl.ds(j * tk, tk), :]
            vb = v_ref[pl.ds(j * tk, tk), :]
            s = lax.dot_general(qs, kb, dn_nt, preferred_element_type=F32)
            m_new = jnp.maximum(m, jnp.max(s, axis=-1, keepdims=True))
            alpha = jnp.exp(m - m_new)
            p = jnp.exp(s - m_new)
            l = alpha * l + jnp.sum(p, axis=-1, keepdims=True)
            acc = alpha * acc + jnp.dot(p.astype(BF16), vb, preferred_element_type=F32)
            m = m_new
        o = acc / l
        for h in range(g):
            o_ref[:, h * HEAD_DIM:(h + 1) * HEAD_DIM] = o[h * tq:(h + 1) * tq]
        lse_ref[...] = m + jnp.log(l)

    return pl.pallas_call(
        body, name="flash_fwd", grid=(nkv, ni),
        in_specs=[pl.BlockSpec((tq, gw), lambda h, i: (i, h)),
                  pl.BlockSpec((n_keys, HEAD_DIM), lambda h, i: (0, h)),
                  pl.BlockSpec((n_keys, HEAD_DIM), lambda h, i: (0, h))],
        out_specs=[pl.BlockSpec((tq, gw), lambda h, i: (i, h)),
                   pl.BlockSpec((g * tq, 1), lambda h, i: (h * ni + i, 0))],
        out_shape=[jax.ShapeDtypeStruct((n_lat, aw), F32), jax.ShapeDtypeStruct((nkv * ni * g * tq, 1), F32)],
        compiler_params=_cp(("parallel", "parallel")))(q_hat, k_all, v_all)


def _flash_bwd(q_hat, k_all, v_all, do, o, lse, n_lat, dims, tq, tk):
    kvw, aw = dims["kv_w"], dims["attn_w"]
    nkv = kvw // HEAD_DIM
    g = aw // kvw
    gw = g * HEAD_DIM
    n_keys = k_all.shape[0]
    ni, nj = n_lat // tq, n_keys // tk
    dn_nt = (((1,), (1,)), ((), ()))
    dn_tn = (((0,), (0,)), ((), ()))

    def body(q_ref, k_ref, v_ref, do_ref, o_ref, lse_ref, dq_ref, dk_ref, dv_ref):
        i = pl.program_id(1)

        @pl.when(i == 0)
        def _():
            dk_ref[...] = jnp.zeros_like(dk_ref)
            dv_ref[...] = jnp.zeros_like(dv_ref)

        dos = _stack_heads(do_ref[...], g)
        qs = _stack_heads(q_ref[...], g)
        delta = jnp.sum(dos.astype(F32) * _stack_heads(o_ref[...], g), axis=-1, keepdims=True)
        lse_v = lse_ref[...]
        dq = jnp.zeros((g * tq, HEAD_DIM), F32)
        for j in range(nj):
            rows = pl.ds(j * tk, tk)
            kb, vb = k_ref[rows, :], v_ref[rows, :]
            s = lax.dot_general(qs, kb, dn_nt, preferred_element_type=F32)
            p = jnp.exp(s - lse_v)
            dp = lax.dot_general(dos, vb, dn_nt, preferred_element_type=F32)
            ds = (p * (dp - delta)).astype(BF16)
            dv_ref[rows, :] += lax.dot_general(p.astype(BF16), dos, dn_tn, preferred_element_type=F32)
            dk_ref[rows, :] += lax.dot_general(ds, qs, dn_tn, preferred_element_type=F32)
            dq = dq + jnp.dot(ds, kb, preferred_element_type=F32)
        for h in range(g):
            dq_ref[:, h * HEAD_DIM:(h + 1) * HEAD_DIM] = dq[h * tq:(h + 1) * tq]

    qspec = pl.BlockSpec((tq, gw), lambda h, i: (i, h))
    full_k = pl.BlockSpec((n_keys, HEAD_DIM), lambda h, i: (0, h))
    return pl.pallas_call(
        body, name="flash_bwd", grid=(nkv, ni),
        in_specs=[qspec, full_k, full_k, qspec, qspec, pl.BlockSpec((g * tq, 1), lambda h, i: (h * ni + i, 0))],
        out_specs=[qspec, full_k, full_k],
        out_shape=[jax.ShapeDtypeStruct((n_lat, aw), F32), jax.ShapeDtypeStruct((n_keys, kvw), F32),
                   jax.ShapeDtypeStruct((n_keys, kvw), F32)],
        compiler_params=_cp(("parallel", "arbitrary")))(q_hat, k_all, v_all, do, o, lse)


def _shifted_copies(pad_ref, sh_ref, tr):
    rows = tr + 2 * CONV_HALO - SUBLANES
    for r in range(SUBLANES):
        sh_ref[r] = pad_ref[pl.ds(r, rows), :]


def _stencil(sh_ref, w_ref, out_ref, offsets, tr, cc, init_ref=None):
    for c0 in range(0, cc, LANES):
        lanes = pl.ds(c0, LANES)
        wv = [jnp.broadcast_to(w_ref[pl.ds(k, 1), lanes], (STENCIL_ROWS, LANES)) for k in range(len(offsets))]
        if init_ref is None:
            init = jnp.zeros((STENCIL_ROWS, LANES), F32)
        else:
            init = jnp.broadcast_to(init_ref[:, lanes], (STENCIL_ROWS, LANES))

        def block(rb, carry, lanes=lanes, wv=wv, init=init):
            r0 = rb * STENCIL_ROWS
            parts = [init] + [None] * (STENCIL_CHAINS - 1)
            for k, o in enumerate(offsets):
                rows = pl.ds(SUBLANES * (o // SUBLANES) + r0, STENCIL_ROWS)
                term = sh_ref[o % SUBLANES, rows, lanes] * wv[k]
                q = k % STENCIL_CHAINS
                parts[q] = term if parts[q] is None else parts[q] + term
            acc = parts[0]
            for p_ in parts[1:]:
                acc = acc + p_
            out_ref[pl.ds(r0, STENCIL_ROWS), lanes] = acc
            return carry

        for rb in range(tr // STENCIL_ROWS):
            block(rb, 0)


def _stencil_weight_grad(sh_ref, d_ref, d_row0, dw_ref, offsets, tr, cc):
    rows_per = REDUCE_ROWS
    for c0 in range(0, cc, LANES):
        lanes = pl.ds(c0, LANES)

        def block(rb, accs, lanes=lanes):
            r0 = rb * rows_per
            dblk = d_ref[pl.ds(d_row0 + r0, rows_per), lanes]
            out = []
            for k, o in enumerate(offsets):
                prod = dblk * sh_ref[o % SUBLANES, pl.ds(SUBLANES * (o // SUBLANES) + r0, rows_per), lanes]
                part = prod[0:SUBLANES]
                for q in range(1, rows_per // SUBLANES):
                    part = part + prod[q * SUBLANES:(q + 1) * SUBLANES]
                out.append(accs[k] + part)
            return tuple(out)

        zero = jnp.zeros((SUBLANES, LANES), F32)
        accs = tuple(zero for _ in offsets)
        for rb in range(tr // rows_per):
            accs = block(rb, accs)
        for k in range(len(offsets)):
            dw_ref[pl.ds(k, 1), lanes] += jnp.sum(accs[k], axis=0, keepdims=True)


def _halo_maps(tr, n_tiles):
    per = tr // CONV_HALO
    prev = lambda i: jnp.maximum(i * per - 1, 0)
    nxt = lambda i: (i + 1) * per
    return prev, nxt


def _mix_fwd(attn, p0, dw_w, dw_b, ln_g, ln_b, dims, tr):
    n_lat, aw = attn.shape
    cc, cb, cw = dims["conv_ch"], dims["cb"], dims["conv_w"]
    off = dims["off"]
    n_tiles = n_lat // tr
    pad = cw // 2
    na, nc = aw // cb, cc // cb
    prev_map, next_map = _halo_maps(tr, n_tiles)

    def body(*refs):
        it = iter(refs)
        attn_ref = next(it)
        za = [next(it) for _ in range(na)]
        a_c = [next(it) for _ in range(nc)]
        b_c = [next(it) for _ in range(nc)]
        zb = [next(it) for _ in range(nc)]
        a_p = [next(it) for _ in range(nc)]
        b_p = [next(it) for _ in range(nc)]
        a_n = [next(it) for _ in range(nc)]
        b_n = [next(it) for _ in range(nc)]
        w_ref, db_ref, g_ref, bb_ref, mix_ref, yc_ref, ypad, ysh = (next(it) for _ in range(8))
        i = pl.program_id(0)
        ypad[pl.ds(0, CONV_HALO), :] = _cat(a_p) * _sigmoid(_cat(b_p)) * (i > 0).astype(F32)
        ypad[pl.ds(CONV_HALO, tr), :] = _cat(a_c) * _sigmoid(_cat(b_c))
        ypad[pl.ds(CONV_HALO + tr, CONV_HALO), :] = _cat(a_n) * _sigmoid(_cat(b_n)) * (i < n_tiles - 1).astype(F32)
        _shifted_copies(ypad, ysh, tr)
        _stencil(ysh, w_ref, yc_ref, [CONV_HALO - pad + k for k in range(cw)], tr, cc, init_ref=db_ref)
        acc = yc_ref[...]
        mu = jnp.mean(acc, axis=-1, keepdims=True)
        xc = acc - mu
        rs = lax.rsqrt(jnp.mean(xc * xc, axis=-1, keepdims=True) + EPS)
        nv = xc * rs * g_ref[...] + bb_ref[...]
        mix_ref[:, :aw] = (attn_ref[...] * _silu(_cat(za))).astype(BF16)
        mix_ref[:, aw:] = (_silu(nv) * _silu(_cat(zb))).astype(BF16)

    rm = lambda i: i
    in_specs = ([pl.BlockSpec((tr, aw), lambda i: (i, 0))]
                + _col_specs(tr, off["za"], aw, cb, rm) + _col_specs(tr, off["a"], cc, cb, rm)
                + _col_specs(tr, off["b"], cc, cb, rm) + _col_specs(tr, off["zb"], cc, cb, rm)
                + _col_specs(CONV_HALO, off["a"], cc, cb, prev_map) + _col_specs(CONV_HALO, off["b"], cc, cb, prev_map)
                + _col_specs(CONV_HALO, off["a"], cc, cb, next_map) + _col_specs(CONV_HALO, off["b"], cc, cb, next_map)
                + [pl.BlockSpec(dw_w.shape, lambda i: (0, 0)), _vec(cc), _vec(cc), _vec(cc)])
    args = [attn] + [p0] * (na + 7 * nc) + [dw_w, dw_b, ln_g, ln_b]
    return pl.pallas_call(
        body, name="mix_fwd", grid=(n_tiles,), in_specs=in_specs,
        out_specs=[pl.BlockSpec((tr, aw + cc), lambda i: (i, 0)), pl.BlockSpec((tr, cc), lambda i: (i, 0))],
        out_shape=[jax.ShapeDtypeStruct((n_lat, aw + cc), BF16), jax.ShapeDtypeStruct((n_lat, cc), F32)],
        scratch_shapes=[pltpu.VMEM((tr + 2 * CONV_HALO, cc), F32),
                        pltpu.VMEM((SUBLANES, tr + 2 * CONV_HALO - SUBLANES, cc), F32)],
        compiler_params=_cp(("parallel",)))(*args)


def _mix_bwd_pointwise(dmix, attn, p0, yc, ln_g, ln_b, dims, tr, n_ext):
    n_lat, aw = attn.shape
    cc, cb, off = dims["conv_ch"], dims["cb"], dims["off"]
    na, nc = aw // cb, cc // cb
    lat_tiles = n_lat // tr

    def body(*refs):
        it = iter(refs)
        dmix_ref, attn_ref = next(it), next(it)
        za = [next(it) for _ in range(na)]
        zb = [next(it) for _ in range(nc)]
        yc_ref, g_ref, bb_ref = next(it), next(it), next(it)
        dattn_ref, dza_ref, dzb_ref, dyc_ref, dg_ref, dbb_ref, ddb_ref = (next(it) for _ in range(7))
        i = pl.program_id(0)

        @pl.when(i == 0)
        def _():
            dg_ref[...] = jnp.zeros_like(dg_ref)
            dbb_ref[...] = jnp.zeros_like(dbb_ref)
            ddb_ref[...] = jnp.zeros_like(ddb_ref)

        lat = (i < lat_tiles).astype(F32)
        dm = dmix_ref[...].astype(F32)
        dma, dmb = dm[:, :aw], dm[:, aw:]
        zav, zbv = _cat(za), _cat(zb)
        sa, dsa = _silu_and_grad(zav)
        sb, dsb = _silu_and_grad(zbv)
        dattn_ref[...] = (dma * sa).astype(BF16)
        dza_ref[...] = (dma * attn_ref[...] * dsa * lat).astype(BF16)
        ycv = yc_ref[...]
        mu = jnp.mean(ycv, axis=-1, keepdims=True)
        xc = ycv - mu
        rs = lax.rsqrt(jnp.mean(xc * xc, axis=-1, keepdims=True) + EPS)
        xh = xc * rs
        nv = xh * g_ref[...] + bb_ref[...]
        sn, dsn = _silu_and_grad(nv)
        dzb_ref[...] = (dmb * sn * dsb * lat).astype(BF16)
        dn = dmb * sb * dsn
        dg_ref[...] += lat * jnp.sum(dn * xh, axis=0, keepdims=True)
        dbb_ref[...] += lat * jnp.sum(dn, axis=0, keepdims=True)
        dxh = dn * g_ref[...]
        dyc = rs * (dxh - jnp.mean(dxh, axis=-1, keepdims=True) - xh * jnp.mean(dxh * xh, axis=-1, keepdims=True))
        dyc_ref[...] = dyc
        ddb_ref[...] += lat * jnp.sum(dyc, axis=0, keepdims=True)

    rm = lambda i: jnp.minimum(i, lat_tiles - 1)
    row = lambda w: pl.BlockSpec((tr, w), lambda i: (rm(i), 0))
    ext = lambda w: pl.BlockSpec((tr, w), lambda i: (i, 0))
    in_specs = ([row(aw + cc), row(aw)] + _col_specs(tr, off["za"], aw, cb, rm)
                + _col_specs(tr, off["zb"], cc, cb, rm) + [row(cc), _vec(cc), _vec(cc)])
    args = [dmix, attn] + [p0] * (na + nc) + [yc, ln_g, ln_b]
    return pl.pallas_call(
        body, name="mix_bwd_pointwise", grid=(n_ext // tr,), in_specs=in_specs,
        out_specs=[row(aw), ext(aw), ext(cc), row(cc), _vec(cc), _vec(cc), _vec(cc)],
        out_shape=[jax.ShapeDtypeStruct((n_lat, aw), BF16), jax.ShapeDtypeStruct((n_ext, aw), BF16),
                   jax.ShapeDtypeStruct((n_ext, cc), BF16), jax.ShapeDtypeStruct((n_lat, cc), F32)]
                  + [jax.ShapeDtypeStruct((1, cc), F32)] * 3,
        compiler_params=_cp(("arbitrary",)))(*args)


def _conv_bwd(dyc, p0, dw_w, dims, tr, n_ext):
    n_lat, cc = dyc.shape
    cb, cw, off = dims["cb"], dims["conv_w"], dims["off"]
    nc = cc // cb
    n_tiles = n_lat // tr
    pad = cw // 2
    prev_lat, next_lat = _halo_maps(tr, n_tiles)
    rm = lambda i: jnp.minimum(i, n_tiles - 1)
    prev_map = lambda i: prev_lat(rm(i))
    next_map = lambda i: next_lat(rm(i))

    def body(*refs):
        it = iter(refs)
        a_c = [next(it) for _ in range(nc)]
        b_c = [next(it) for _ in range(nc)]
        a_p = [next(it) for _ in range(nc)]
        b_p = [next(it) for _ in range(nc)]
        a_n = [next(it) for _ in range(nc)]
        b_n = [next(it) for _ in range(nc)]
        d_c, d_p, d_n, w_ref, dab_ref, dw_ref, ypad, dpad, ysh, dsh, dy_scr = (next(it) for _ in range(11))
        i = pl.program_id(0)

        @pl.when(i == 0)
        def _():
            dw_ref[...] = jnp.zeros_like(dw_ref)

        @pl.when(i >= n_tiles)
        def _():
            dab_ref[...] = jnp.zeros_like(dab_ref)

        @pl.when(i < n_tiles)
        def _():
            first, last = (i > 0).astype(F32), (i < n_tiles - 1).astype(F32)
            av, bv = _cat(a_c), _cat(b_c)
            sg = _sigmoid(bv)
            ypad[pl.ds(0, CONV_HALO), :] = _cat(a_p) * _sigmoid(_cat(b_p)) * first
            ypad[pl.ds(CONV_HALO, tr), :] = av * sg
            ypad[pl.ds(CONV_HALO + tr, CONV_HALO), :] = _cat(a_n) * _sigmoid(_cat(b_n)) * last
            dpad[pl.ds(0, CONV_HALO), :] = d_p[...] * first
            dpad[pl.ds(CONV_HALO, tr), :] = d_c[...]
            dpad[pl.ds(CONV_HALO + tr, CONV_HALO), :] = d_n[...] * last
            _shifted_copies(ypad, ysh, tr)
            _shifted_copies(dpad, dsh, tr)
            _stencil(dsh, w_ref, dy_scr, [CONV_HALO + pad - k for k in range(cw)], tr, cc)
            _stencil_weight_grad(ysh, dpad, CONV_HALO, dw_ref, [CONV_HALO - pad + k for k in range(cw)], tr, cc)
            dy = dy_scr[...]
            dab_ref[:, :cc] = (dy * sg).astype(BF16)
            dab_ref[:, cc:] = (dy * av * sg * (1.0 - sg)).astype(BF16)

    in_specs = (_col_specs(tr, off["a"], cc, cb, rm) + _col_specs(tr, off["b"], cc, cb, rm)
                + _col_specs(CONV_HALO, off["a"], cc, cb, prev_map) + _col_specs(CONV_HALO, off["b"], cc, cb, prev_map)
                + _col_specs(CONV_HALO, off["a"], cc, cb, next_map) + _col_specs(CONV_HALO, off["b"], cc, cb, next_map)
                + [pl.BlockSpec((tr, cc), lambda i: (rm(i), 0)),
                   pl.BlockSpec((CONV_HALO, cc), lambda i: (prev_map(i), 0)),
                   pl.BlockSpec((CONV_HALO, cc), lambda i: (jnp.minimum(next_map(i), n_lat // CONV_HALO - 1), 0)),
                   pl.BlockSpec(dw_w.shape, lambda i: (0, 0))])
    args = [p0] * (6 * nc) + [dyc, dyc, dyc, dw_w]
    return pl.pallas_call(
        body, name="conv_bwd", grid=(n_ext // tr,), in_specs=in_specs,
        out_specs=[pl.BlockSpec((tr, 2 * cc), lambda i: (i, 0)), pl.BlockSpec(dw_w.shape, lambda i: (0, 0))],
        out_shape=[jax.ShapeDtypeStruct((n_ext, 2 * cc), BF16), jax.ShapeDtypeStruct(dw_w.shape, F32)],
        scratch_shapes=[pltpu.VMEM((tr + 2 * CONV_HALO, cc), F32), pltpu.VMEM((tr + 2 * CONV_HALO, cc), F32),
                        pltpu.VMEM((SUBLANES, tr + 2 * CONV_HALO - SUBLANES, cc), F32),
                        pltpu.VMEM((SUBLANES, tr + 2 * CONV_HALO - SUBLANES, cc), F32),
                        pltpu.VMEM((tr, cc), F32)],
        compiler_params=_cp(("arbitrary",)))(*args)


def _sgu_parts(u, v, ln_g, ln_b):
    mu = jnp.mean(v, axis=-1, keepdims=True)
    xc = v - mu
    rs = lax.rsqrt(jnp.mean(xc * xc, axis=-1, keepdims=True) + EPS)
    xh = xc * rs
    return u, xh, rs, xh * ln_g + ln_b


def _sgu_fwd(p1, ln_g, ln_b, ws, bs_t, tr):
    n_lat, w3 = p1.shape
    w = w3 // 3
    ng, ch = ws.shape[0], ws.shape[1]
    gwid = w // ng
    n_ch = tr // ch

    def body(pu_ref, pv_ref, pg_ref, g_ref, b_ref, ws_ref, bs_ref, o_ref):
        u, _, _, vln = _sgu_parts(_gelu(pu_ref[...].astype(F32)), _gelu(pv_ref[...].astype(F32)),
                                  g_ref[...], b_ref[...])
        gate = _silu(pg_ref[...].astype(F32))
        vb = vln.astype(BF16)
        for c in range(n_ch):
            rs_ = slice(c * ch, (c + 1) * ch)
            for gi in range(ng):
                cs_ = slice(gi * gwid, (gi + 1) * gwid)
                mixed = jnp.dot(ws_ref[gi], vb[rs_, cs_], preferred_element_type=F32) + bs_ref[:, gi:gi + 1]
                o_ref[rs_, cs_] = (u[rs_, cs_] * mixed * gate[rs_, cs_]).astype(BF16)

    col = lambda t: pl.BlockSpec((tr, w), lambda i, _t=t: (i, _t))
    return pl.pallas_call(
        body, name="sgu_fwd", grid=(n_lat // tr,),
        in_specs=[col(0), col(1), col(2), _vec(w), _vec(w),
                  pl.BlockSpec(ws.shape, lambda i: (0, 0, 0)), pl.BlockSpec(bs_t.shape, lambda i: (0, 0))],
        out_specs=pl.BlockSpec((tr, w), lambda i: (i, 0)),
        out_shape=jax.ShapeDtypeStruct((n_lat, w), BF16),
        compiler_params=_cp(("parallel",)))(p1, p1, p1, ln_g, ln_b, ws, bs_t)


def _sgu_bwd(dm, p1, ln_g, ln_b, ws, ws_t, bs_t, tr):
    n_lat, w3 = p1.shape
    w = w3 // 3
    ng, ch = ws.shape[0], ws.shape[1]
    gwid = w // ng
    n_ch = tr // ch
    dn_nt = (((1,), (1,)), ((), ()))

    def body(dm_ref, pu_ref, pv_ref, pg_ref, g_ref, b_ref, ws_ref, wst_ref, bs_ref,
             dp_ref, dws_ref, dbs_ref, dg_ref, dbb_ref, dvln_scr):
        i = pl.program_id(0)

        @pl.when(i == 0)
        def _():
            dws_ref[...] = jnp.zeros_like(dws_ref)
            dbs_ref[...] = jnp.zeros_like(dbs_ref)
            dg_ref[...] = jnp.zeros_like(dg_ref)
            dbb_ref[...] = jnp.zeros_like(dbb_ref)

        puv, pvv, pgv = pu_ref[...].astype(F32), pv_ref[...].astype(F32), pg_ref[...].astype(F32)
        gu, dgu = _gelu_and_grad(puv)
        gv, dgv = _gelu_and_grad(pvv)
        gate, dgate = _silu_and_grad(pgv)
        u, xh, rs, vln = _sgu_parts(gu, gv, g_ref[...], b_ref[...])
        dmv = dm_ref[...].astype(F32)
        vb = vln.astype(BF16)
        dmu = dmv * u
        du_pre = dmv * gate * dgu
        dg_pre = dmu * dgate
        dmix_all = dmu * gate
        dbs_cols = [jnp.zeros((ch, 1), F32) for _ in range(ng)]
        for c in range(n_ch):
            rs_ = slice(c * ch, (c + 1) * ch)
            for gi in range(ng):
                cs_ = slice(gi * gwid, (gi + 1) * gwid)
                mixed = jnp.dot(ws_ref[gi], vb[rs_, cs_], preferred_element_type=F32) + bs_ref[:, gi:gi + 1]
                dmixed = dmix_all[rs_, cs_]
                dmb = dmixed.astype(BF16)
                dp_ref[rs_, gi * gwid:(gi + 1) * gwid] = (du_pre[rs_, cs_] * mixed).astype(BF16)
                dp_ref[rs_, 2 * w + gi * gwid:2 * w + (gi + 1) * gwid] = (dg_pre[rs_, cs_] * mixed).astype(BF16)
                dvln_scr[rs_, cs_] = jnp.dot(wst_ref[gi], dmb, preferred_element_type=F32)
                dws_ref[gi] += lax.dot_general(dmb, vb[rs_, cs_], dn_nt, preferred_element_type=F32)
                dbs_cols[gi] = dbs_cols[gi] + jnp.sum(dmixed, axis=-1, keepdims=True)
        dbs_ref[...] += jnp.concatenate(dbs_cols, axis=1)
        dvln = dvln_scr[...]
        dg_ref[...] += jnp.sum(dvln * xh, axis=0, keepdims=True)
        dbb_ref[...] += jnp.sum(dvln, axis=0, keepdims=True)
        dxh = dvln * g_ref[...]
        dv = rs * (dxh - jnp.mean(dxh, axis=-1, keepdims=True) - xh * jnp.mean(dxh * xh, axis=-1, keepdims=True))
        dp_ref[:, w:2 * w] = (dv * dgv).astype(BF16)

    col = lambda t: pl.BlockSpec((tr, w), lambda i, _t=t: (i, _t))
    return pl.pallas_call(
        body, name="sgu_bwd", grid=(n_lat // tr,),
        in_specs=[pl.BlockSpec((tr, w), lambda i: (i, 0)), col(0), col(1), col(2), _vec(w), _vec(w),
                  pl.BlockSpec(ws.shape, lambda i: (0, 0, 0)), pl.BlockSpec(ws.shape, lambda i: (0, 0, 0)),
                  pl.BlockSpec(bs_t.shape, lambda i: (0, 0))],
        out_specs=[pl.BlockSpec((tr, w3), lambda i: (i, 0)), pl.BlockSpec(ws.shape, lambda i: (0, 0, 0)),
                   pl.BlockSpec(bs_t.shape, lambda i: (0, 0)), _vec(w), _vec(w)],
        out_shape=[jax.ShapeDtypeStruct((n_lat, w3), BF16), jax.ShapeDtypeStruct(ws.shape, F32),
                   jax.ShapeDtypeStruct(bs_t.shape, F32), jax.ShapeDtypeStruct((1, w), F32),
                   jax.ShapeDtypeStruct((1, w), F32)],
        scratch_shapes=[pltpu.VMEM((tr, w), F32)],
        compiler_params=_cp(("arbitrary",)))(dm, p1, p1, p1, ln_g, ln_b, ws, ws_t, bs_t)


def _final_loss(x2, target, final_g, o_prev, gate_prev, tr):
    n_lat, d = x2.shape

    def body(x_ref, t_ref, g_ref, o_ref, gp_ref, dx_ref, do_ref, ls_ref, dg_ref, dgp_ref):
        i = pl.program_id(0)

        @pl.when(i == 0)
        def _():
            ls_ref[...] = jnp.zeros_like(ls_ref)
            dg_ref[...] = jnp.zeros_like(dg_ref)
            dgp_ref[...] = jnp.zeros_like(dgp_ref)

        xv = x_ref[...]
        gv = g_ref[...]
        rs = lax.rsqrt(jnp.mean(xv * xv, axis=-1, keepdims=True) + EPS)
        xn = xv * rs
        err = xn * gv - t_ref[...]
        ls_ref[...] += jnp.sum(err * err, axis=0, keepdims=True)
        dy = err * (1.0 / d)
        dg_ref[...] += jnp.sum(dy * xn, axis=0, keepdims=True)
        gy = dy * gv
        dx = rs * (gy - xn * jnp.mean(gy * xn, axis=-1, keepdims=True))
        dx_ref[...] = dx
        do_ref[...] = (gp_ref[...] * dx).astype(BF16)
        dgp_ref[...] += jnp.sum(dx * o_ref[...].astype(F32), axis=0, keepdims=True)

    row = pl.BlockSpec((tr, d), lambda i: (i, 0))
    return pl.pallas_call(
        body, name="final_loss", grid=(n_lat // tr,), in_specs=[row, row, _vec(d), row, _vec(d)],
        out_specs=[row, row, _vec(d), _vec(d), _vec(d)],
        out_shape=[jax.ShapeDtypeStruct((n_lat, d), F32), jax.ShapeDtypeStruct((n_lat, d), BF16)]
                  + [jax.ShapeDtypeStruct((1, d), F32)] * 3,
        compiler_params=_cp(("arbitrary",)))(x2, target, final_g, o_prev, gate_prev)


def _pack(arrays):
    flat = jnp.concatenate([a.reshape(-1).astype(F32) for a in arrays])
    n = flat.shape[0]
    rows = -(-n // LANES)
    rows = -(-rows // PACK_ROWS) * PACK_ROWS
    return jnp.pad(flat, (0, rows * LANES - n)).reshape(rows, LANES)


def _unpack(buf, shapes):
    flat = buf.reshape(buf.shape[:-2] + (-1,))
    out, pos = [], 0
    for shp in shapes:
        n = math.prod(shp)
        out.append(flat[..., pos:pos + n].reshape(buf.shape[:-2] + tuple(shp)))
        pos += n
    return out


def _rope_tables(n_lat, n_ctx):
    rows = n_lat // GRID_W
    row = jnp.repeat(jnp.arange(rows, dtype=F32), GRID_W)
    col = jnp.tile(jnp.arange(GRID_W, dtype=F32), rows)
    n_freq, axis_dim = HEAD_DIM // 4, HEAD_DIM // 2
    inv = jnp.power(ROPE_THETA, jnp.arange(n_freq, dtype=F32) * (-2.0 / axis_dim))
    ang = jnp.concatenate([row[:, None] * inv, col[:, None] * inv], axis=-1)
    cos, sin = jnp.cos(ang), jnp.sin(ang)
    cos_i = jnp.repeat(cos, 2, axis=-1)
    sin_s = jnp.stack([-sin, sin], axis=-1).reshape(n_lat, HEAD_DIM)
    cos_i = jnp.concatenate([cos_i, jnp.ones((n_ctx, HEAD_DIM), F32)], axis=0)
    sin_s = jnp.concatenate([sin_s, jnp.zeros((n_ctx, HEAD_DIM), F32)], axis=0)
    return cos_i, sin_s


def kernel(x, c, ctx, c_ctx, ada_w, ada_b, norm_g, ev_w_in, ev_q_norm, ev_k_norm, ev_dw_w, ev_dw_b, ev_ln_g, ev_ln_b, ev_w_out, od_w_in, od_ln_g, od_ln_b, od_ws, od_bs, od_w_out, final_g, loss_target, m_c_ctx, m_ada_w, m_ada_b, m_norm_g, m_ev_w_in, m_ev_q_norm, m_ev_k_norm, m_ev_dw_w, m_ev_dw_b, m_ev_ln_g, m_ev_ln_b, m_ev_w_out, m_od_w_in, m_od_ln_g, m_od_ln_b, m_od_ws, m_od_bs, m_od_w_out, m_final_g, v_c_ctx, v_ada_w, v_ada_b, v_norm_g, v_ev_w_in, v_ev_q_norm, v_ev_k_norm, v_ev_dw_w, v_ev_dw_b, v_ev_ln_g, v_ev_ln_b, v_ev_w_out, v_od_w_in, v_od_ln_g, v_od_ln_b, v_od_ws, v_od_bs, v_od_w_out, v_final_g):
    weights = dict(c_ctx=c_ctx, ada_w=ada_w, ada_b=ada_b, norm_g=norm_g, ev_w_in=ev_w_in, ev_q_norm=ev_q_norm,
                   ev_k_norm=ev_k_norm, ev_dw_w=ev_dw_w, ev_dw_b=ev_dw_b, ev_ln_g=ev_ln_g, ev_ln_b=ev_ln_b,
                   ev_w_out=ev_w_out, od_w_in=od_w_in, od_ln_g=od_ln_g, od_ln_b=od_ln_b, od_ws=od_ws, od_bs=od_bs,
                   od_w_out=od_w_out, final_g=final_g)
    mom_m = dict(c_ctx=m_c_ctx, ada_w=m_ada_w, ada_b=m_ada_b, norm_g=m_norm_g, ev_w_in=m_ev_w_in,
                 ev_q_norm=m_ev_q_norm, ev_k_norm=m_ev_k_norm, ev_dw_w=m_ev_dw_w, ev_dw_b=m_ev_dw_b,
                 ev_ln_g=m_ev_ln_g, ev_ln_b=m_ev_ln_b, ev_w_out=m_ev_w_out, od_w_in=m_od_w_in, od_ln_g=m_od_ln_g,
                 od_ln_b=m_od_ln_b, od_ws=m_od_ws, od_bs=m_od_bs, od_w_out=m_od_w_out, final_g=m_final_g)
    mom_v = dict(c_ctx=v_c_ctx, ada_w=v_ada_w, ada_b=v_ada_b, norm_g=v_norm_g, ev_w_in=v_ev_w_in,
                 ev_q_norm=v_ev_q_norm, ev_k_norm=v_ev_k_norm, ev_dw_w=v_ev_dw_w, ev_dw_b=v_ev_dw_b,
                 ev_ln_g=v_ev_ln_g, ev_ln_b=v_ev_ln_b, ev_w_out=v_ev_w_out, od_w_in=v_od_w_in, od_ln_g=v_od_ln_g,
                 od_ln_b=v_od_ln_b, od_ws=v_od_ws, od_bs=v_od_bs, od_w_out=v_od_w_out, final_g=v_final_g)
    order = list(weights)

    _, n_lat, d = x.shape
    n_ctx = ctx.shape[1]
    n_ext = n_lat + n_ctx
    ev_in = ev_w_in.shape[-1] * N_CHIP
    ev_mix = ev_w_out.shape[1] * N_CHIP
    conv_ch = ev_dw_b.shape[-1]
    conv_w = ev_dw_w.shape[1]
    attn_w = ev_mix - conv_ch
    kv_w = N_KV_HEADS * HEAD_DIM
    assert ev_in == 2 * kv_w + 2 * attn_w + 3 * conv_ch and conv_w // 2 < CONV_HALO
    sgu_w = od_w_out.shape[1] * N_CHIP
    wa = ada_w.shape[-1]
    cb = math.gcd(2 * kv_w, attn_w, conv_ch)
    off = dict(k=0, v=kv_w, q=2 * kv_w, za=2 * kv_w + attn_w, a=2 * kv_w + 2 * attn_w,
               b=2 * kv_w + 2 * attn_w + conv_ch, zb=2 * kv_w + 2 * attn_w + 2 * conv_ch)
    dims = dict(kv_w=kv_w, attn_w=attn_w, conv_ch=conv_ch, conv_w=conv_w, cb=cb, off=off)
    tr = 256 if (n_lat % 256 == 0 and n_ctx % 256 == 0) else 128

    mx, my, mc = lax.axis_index("x"), lax.axis_index("y"), lax.axis_index("c")
    me = 4 * mx + 2 * my + mc
    chip = 2 * mx + my

    x2d, tgt2d, ctx2d = x[0], loss_target[0], ctx[0]
    ev_dw_w_l = ev_dw_w[0]
    dwc = ev_dw_w_l.shape[1]
    lnc = od_ln_g.shape[1]

    g_c = _allgather_small(jnp.broadcast_to(c, (8, d)), "gather_cond")[:, 0, :]
    c_rows = jnp.concatenate([g_c, c_ctx[None, :], jnp.zeros((MOD_ROWS - N_DEV - 1, d), F32)], axis=0)
    c_rows_t = c_rows.T
    ada_b_shard = lax.dynamic_slice_in_dim(ada_b, chip * wa, wa, axis=1)[:, None, :]
    mod_part = _mod_fwd(c_rows_t, ada_w, ada_b_shard)
    part_shapes = [(2, MOD_ROWS, wa), (conv_w, dwc), (1, lnc), (1, lnc)]
    g_parts = _allgather_small(_pack([mod_part, ev_dw_w_l, od_ln_g, od_ln_b]), "gather_mod")
    per_chip = [_unpack(g_parts[2 * s], part_shapes) for s in range(N_CHIP)]
    mod_all = jnp.concatenate([p[0] for p in per_chip], axis=-1)
    dw_w_full = jnp.concatenate([p[1] for p in per_chip], axis=-1)
    od_ln_g_full = jnp.concatenate([p[2] for p in per_chip], axis=-1)
    od_ln_b_full = jnp.concatenate([p[3] for p in per_chip], axis=-1)
    dw_w_pad = jnp.pad(dw_w_full, ((0, 2 * CONV_HALO - conv_w), (0, 0)))
    mod_me = lax.dynamic_slice_in_dim(mod_all, me, 1, axis=1)
    shift0, scale0, gate0 = mod_me[0, :, :d], mod_me[0, :, d:2 * d], mod_me[0, :, 2 * d:]
    shift1, scale1, gate1 = mod_me[1, :, :d], mod_me[1, :, d:2 * d], mod_me[1, :, 2 * d:]
    shift_c, scale_c = mod_all[0, N_DEV:N_DEV + 1, :d], mod_all[0, N_DEV:N_DEV + 1, d:2 * d]
    g0, g1 = norm_g[0:1], norm_g[1:2]

    lay = dict(ev_w_in=_Sharded((d, ev_in), True), ev_w_out=_Sharded((ev_mix, d), False),
               od_w_in=_Sharded((d, 3 * sgu_w), True), od_w_out=_Sharded((sgu_w, d), False))
    big = list(lay)
    chip_arr = jnp.reshape(chip, (1,)).astype(jnp.int32)
    core_arr = jnp.reshape(mc, (1,)).astype(jnp.int32)
    own = {n: _cast_into_full(weights[n][0], lay[n], chip_arr, f"cast_{n}") for n in big}
    later = big[1:]
    gather_later = _gather_comm([own[n] for n in later], [lay[n] for n in later])

    def reduce_start(g, n):
        theirs = _pair_exchange([g], [lay[n]], f"pair_exchange_{n}")[0]
        psum = _pair_sum(g, theirs, lay[n], core_arr, f"pair_sum_{n}")
        return psum, _scatter_comm([psum], [lay[n]])

    h0, w_ev_in = _adaln_fwd(x2d, g0, shift0, scale0, tr, "adaln0_fwd",
                             comm=_gather_comm([own["ev_w_in"]], [lay["ev_w_in"]]))
    w_full = {"ev_w_in": w_ev_in}
    hc = _adaln_fwd(ctx2d, g0, shift_c, scale_c, tr, "adaln0_ctx_fwd")
    h0e = jnp.concatenate([h0, hc], axis=0)
    tm_e = _pick(n_ext, (1408, 768, 640, 512, 256, 128))
    tk_e = _pick(n_ext, (768, 640, 512, 256, 128))
    tm_l = _pick(n_lat, (1024, 512, 256, 128))
    p0, *gathered = _mm(h0e, w_full["ev_w_in"], name="mm_ev_in", tm=tm_e, tn=_pick(ev_in, (512, 256, 128)), tk=d,
                        out_dtype=BF16, comm=gather_later)
    w_full.update(zip(later, gathered))
    cos_i, sin_s = _rope_tables(n_lat, n_ctx)
    q_hat, k_all, v_all = _qk_prep(p0, cos_i, sin_s, ev_q_norm, ev_k_norm, dims, tr)
    tq = _pick(n_lat, (256, 128))
    tkk = _pick(n_ext, (1408, 640, 512, 256, 128))
    attn, lse = _flash_fwd(q_hat, k_all, v_all, n_lat, dims, tq, tkk)
    mix, yc = _mix_fwd(attn, p0, dw_w_pad, ev_dw_b, ev_ln_g, ev_ln_b, dims, tr)
    o0, x1 = _mm(mix, w_full["ev_w_out"], name="mm_ev_out", tm=tm_l, tn=_pick(d, (1024, 512, 256)),
                 tk=ev_mix, out_dtype=BF16, res=x2d, gate=gate0)

    h1 = _adaln_fwd(x1, g1, shift1, scale1, tr, "adaln1_fwd")
    p1 = _mm(h1, w_full["od_w_in"], name="mm_od_in", tm=tm_l, tn=_pick(3 * sgu_w, (512, 256, 128)), tk=d,
             out_dtype=BF16)
    ws_b = od_ws[0].astype(BF16)
    ws_t_b = jnp.swapaxes(od_ws[0], 1, 2).astype(BF16)
    bs_t = od_bs[0].T
    m1 = _sgu_fwd(p1, od_ln_g_full, od_ln_b_full, ws_b, bs_t, tr)
    o1, x2 = _mm(m1, w_full["od_w_out"], name="mm_od_out", tm=tm_l, tn=_pick(d, (1024, 512, 256)),
                 tk=sgu_w, out_dtype=BF16, res=x1, gate=gate1)

    dx2, do1, loss_cols, d_final_g, dgate1 = _final_loss(x2, tgt2d, final_g[None, :], o1, gate1, tr)
    loss = lax.psum(0.5 / d * jnp.sum(loss_cols), ("x", "y", "c"))

    tk_l = _pick(n_lat, (1024, 512, 256, 128))
    psums, slots = {}, {}
    tk_nt = (1408, 1024, 768, 512, 256, 128)
    g_od_w_out = _mm(m1, do1, name="mm_od_out_dw", ta=True, tm=_pick(sgu_w, (1024, 512, 256)),
                     tn=_pick(d, (1024, 512, 256)), tk=tk_l, out_dtype=BF16)
    psums["od_w_out"], sc = reduce_start(g_od_w_out, "od_w_out")
    dm1, slots["od_w_out"] = _mm(do1, w_full["od_w_out"], name="mm_od_out_dx", tb=True, tm=tm_l,
                                 tn=_pick(sgu_w, (1024, 512, 256)), tk=d, out_dtype=BF16, comm=sc)
    dp1, d_ws, d_bs_t, d_od_ln_g, d_od_ln_b = _sgu_bwd(dm1, p1, od_ln_g_full, od_ln_b_full, ws_b, ws_t_b, bs_t, tr)
    g_od_w_in = _mm(h1, dp1, name="mm_od_in_dw", ta=True, tm=_pick(d, (1024, 512, 256)),
                    tn=_pick(3 * sgu_w, (1536, 768, 512, 384, 256, 128)), tk=tk_l, out_dtype=BF16)
    psums["od_w_in"], sc = reduce_start(g_od_w_in, "od_w_in")
    dh1, slots["od_w_in"] = _mm(dp1, w_full["od_w_in"], name="mm_od_in_dx", tb=True, tm=tm_l,
                                tn=_pick(d, (1024, 512, 256)), tk=_pick(3 * sgu_w, tk_nt), out_dtype=BF16, comm=sc)
    zero_d = jnp.zeros((1, d), F32)
    b1 = _adaln_bwd(x1, dh1, 0, g1, scale1, zero_d, tr, "adaln1_bwd", dres=dx2, o_prev=o0, gate_prev=gate0)
    dx1, do0, dgate0 = b1["dx"], b1["do_prev"], b1["dgate_prev"]

    g_ev_w_out = _mm(mix, do0, name="mm_ev_out_dw", ta=True, tm=_pick(ev_mix, (1024, 512, 256)),
                     tn=_pick(d, (1024, 512, 256)), tk=tk_l, out_dtype=BF16)
    psums["ev_w_out"], sc = reduce_start(g_ev_w_out, "ev_w_out")
    dmix, slots["ev_w_out"] = _mm(do0, w_full["ev_w_out"], name="mm_ev_out_dx", tb=True, tm=tm_l,
                                  tn=_pick(ev_mix, (1024, 512, 256)), tk=d, out_dtype=BF16, comm=sc)
    dattn, dza, dzb, dyc, d_ev_ln_g, d_ev_ln_b, d_dw_b = _mix_bwd_pointwise(
        dmix, attn, p0, yc, ev_ln_g, ev_ln_b, dims, tr, n_ext)
    dab, d_dw_w_pad = _conv_bwd(dyc, p0, dw_w_pad, dims, tr, n_ext)
    dq_hat, dk_hat, dv_all = _flash_bwd(q_hat, k_all, v_all, dattn, attn, lse, n_lat, dims, tq, tkk)
    dkvq, d_q_norm, d_k_norm = _qk_prep_bwd(dq_hat, dk_hat, dv_all, p0, cos_i, sin_s, ev_q_norm, ev_k_norm,
                                            dims, tr, n_lat)
    dp0 = jnp.concatenate([dkvq, dza, dab, dzb], axis=1)
    g_ev_w_in = _mm(h0e, dp0, name="mm_ev_in_dw", ta=True, tm=_pick(d, (1024, 512, 256)),
                    tn=_pick(ev_in, (1408, 768, 512, 256, 128)), tk=tk_e, out_dtype=BF16)
    psums["ev_w_in"], sc = reduce_start(g_ev_w_in, "ev_w_in")
    dh0, slots["ev_w_in"] = _mm(dp0, w_full["ev_w_in"], name="mm_ev_in_dx", tb=True, tm=tm_e,
                                tn=_pick(d, (1024, 512, 256)), tk=_pick(ev_in, tk_nt), out_dtype=BF16, comm=sc)
    lays = [lay[n] for n in big]
    halves = [_chip_sum(psums[n], slots[n], lay[n], chip_arr, core_arr, f"chip_sum_{n}") for n in big]
    bc = _adaln_bwd(ctx2d, dh0, n_lat, g0, scale_c, zero_d, tr, "adaln0_ctx_bwd")
    b0 = _adaln_bwd(x2d, dh0, 0, g0, scale0, bc["dg"], tr, "adaln0_bwd", dres=dx1, comm=_share_comm(halves, lays))
    g_big = dict(zip(big, b0["carried"]))
    grad_x = b0["dx"]

    zeros_d = jnp.zeros((1, d), F32)
    dmod0 = jnp.concatenate([b0["dshift"], b0["dscale"], dgate0], axis=1)
    dmod1 = jnp.concatenate([b1["dshift"], b1["dscale"], dgate1], axis=1)
    dmodc = jnp.concatenate([bc["dshift"], bc["dscale"], zeros_d], axis=1)
    small = [dmod0, dmod1, dmodc, b0["dg"], b1["dg"], d_q_norm, d_k_norm, d_dw_w_pad[:conv_w], d_dw_b,
             d_ev_ln_g, d_ev_ln_b, d_od_ln_g, d_od_ln_b, d_ws, d_bs_t.T, d_final_g]
    small_shapes = [a.shape for a in small]
    delta, new_m, new_v = {}, {}, {}

    def adamw_big(n, grad, comm=None):
        shp = weights[n].shape
        as2d = lambda a: a.reshape(-1, shp[-1])
        outs = _adamw(as2d(weights[n]), as2d(grad), as2d(mom_m[n]), as2d(mom_v[n]), f"adamw_{n}", comm=comm)
        delta[n], new_m[n], new_v[n] = (o.reshape(shp) for o in outs[:3])
        return outs[3:]

    g_small, = adamw_big("od_w_in", g_big["od_w_in"], comm=_allgather_comm(_pack(small)))
    tot = _unpack(_sum_slots(g_small, "sum_small_grads"), small_shapes)
    (t_dmod0, t_dmod1, t_dmodc, t_g0, t_g1, t_qn, t_kn, t_dw_w, t_dw_b, t_eln_g, t_eln_b, t_oln_g, t_oln_b,
     t_ws, t_bs, t_fg) = tot
    rows_dmod = _unpack(g_small, small_shapes[:2])
    dmod0_rows, dmod1_rows = rows_dmod[0][:, 0, :], rows_dmod[1][:, 0, :]
    pad_rows = jnp.zeros((MOD_ROWS - N_DEV - 1, 3 * d), F32)
    dm_l0 = jnp.concatenate([dmod0_rows, t_dmodc, pad_rows], axis=0)
    dm_l1 = jnp.concatenate([dmod1_rows, jnp.zeros((MOD_ROWS - N_DEV, 3 * d), F32)], axis=0)
    dm_shard = lax.dynamic_slice_in_dim(jnp.stack([dm_l0, dm_l1]), chip * wa, wa, axis=2)
    g_ada_w, dsc = _mod_bwd(c_rows_t, dm_shard, ada_w)
    g_dsc = _allgather_small(_pack([dsc[0]]), "gather_cctx")
    g_c_ctx = _cctx_grad(g_dsc, _pack([c_ctx])).reshape(-1)[:d]
    g_ada_b = jnp.stack([t_dmod0[0] + t_dmodc[0], t_dmod1[0]])

    grads = dict(
        c_ctx=g_c_ctx, ada_w=g_ada_w, ada_b=g_ada_b, norm_g=jnp.concatenate([t_g0, t_g1], axis=0),
        ev_w_in=g_big["ev_w_in"][None], ev_q_norm=t_qn, ev_k_norm=t_kn,
        ev_dw_w=lax.dynamic_slice_in_dim(t_dw_w, chip * dwc, dwc, axis=1)[None], ev_dw_b=t_dw_b,
        ev_ln_g=t_eln_g, ev_ln_b=t_eln_b, ev_w_out=g_big["ev_w_out"][None], od_w_in=g_big["od_w_in"][None],
        od_ln_g=lax.dynamic_slice_in_dim(t_oln_g, chip * lnc, lnc, axis=1),
        od_ln_b=lax.dynamic_slice_in_dim(t_oln_b, chip * lnc, lnc, axis=1),
        od_ws=t_ws[None], od_bs=t_bs[None], od_w_out=g_big["od_w_out"][None], final_g=t_fg[0])
    grads = {n: grads[n].reshape(weights[n].shape) for n in order}

    large = ("ada_w", "ev_w_in", "ev_w_out", "od_w_in", "od_w_out")
    for n in large:
        if n not in delta:
            adamw_big(n, grads[n])
    rest_names = [n for n in order if n not in large]
    rest_shapes = [weights[n].shape for n in rest_names]
    dl, nm, nv = _adamw(_pack([weights[n] for n in rest_names]), _pack([grads[n] for n in rest_names]),
                        _pack([mom_m[n] for n in rest_names]), _pack([mom_v[n] for n in rest_names]), "adamw_small")
    for n, a, b_, c_ in zip(rest_names, _unpack(dl, rest_shapes), _unpack(nm, rest_shapes), _unpack(nv, rest_shapes)):
        delta[n], new_m[n], new_v[n] = a, b_, c_

    return (loss, grad_x[None], *[grads[n] for n in order], *[delta[n] for n in order],
            *[new_m[n] for n in order], *[new_v[n] for n in order])
```

```python
import math

import jax
import jax.numpy as jnp
from jax import lax
from jax.experimental import pallas as pl
from jax.experimental.pallas import tpu as pltpu

F32 = jnp.float32
BF16 = jnp.bfloat16
EPS = 1e-6
GRID_W = 64
ROPE_THETA = 10000.0
HEAD_DIM = 128
N_KV_HEADS = 2
CONV_HALO = 16
LANES = 128
SUBLANES = 8
STENCIL_ROWS = 32
STENCIL_CHAINS = 4
REDUCE_ROWS = 32
N_DEV = 8
N_CHIP = 4
MOD_ROWS = 16
PACK_ROWS = 64
ADAM_LR, ADAM_B1, ADAM_B2, ADAM_EPS, ADAM_WD, ADAM_STEP = 0.001, 0.9, 0.999, 1e-08, 0.01, 10
VMEM_LIMIT = 56 * 1024 * 1024
MESH = pl.DeviceIdType.MESH
ANY = pl.BlockSpec(memory_space=pl.ANY)
VMEM_SPEC = pl.BlockSpec(memory_space=pltpu.VMEM)
CHIP_DELTAS = ((1, 0), (0, 1), (1, 1))


def _cp(sem=None):
    return pltpu.CompilerParams(dimension_semantics=sem, vmem_limit_bytes=VMEM_LIMIT)


def _pick(n, cands):
    for c in cands:
        if n % c == 0:
            return c
    raise ValueError(f"no tile for {n} in {cands}")


def _sigmoid(x):
    return 1.0 / (1.0 + jnp.exp(-x))


def _silu(x):
    return x * _sigmoid(x)


def _silu_and_grad(x):
    s = _sigmoid(x)
    y = x * s
    return y, s + y * (1.0 - s)


def _dsilu(x):
    return _silu_and_grad(x)[1]


_GELU_C = math.sqrt(2.0 / math.pi)
_GELU_A = 0.044715


def _gelu_and_grad(x):
    x2 = x * x
    t = jnp.tanh(x * (_GELU_C + (_GELU_C * _GELU_A) * x2))
    h = 0.5 + 0.5 * t
    return x * h, h + x * (1.0 - t * t) * (0.5 * _GELU_C + (1.5 * _GELU_C * _GELU_A) * x2)


def _gelu(x):
    x2 = x * x
    return x * (0.5 + 0.5 * jnp.tanh(x * (_GELU_C + (_GELU_C * _GELU_A) * x2)))


def _vec(d):
    return pl.BlockSpec((1, d), lambda *_: (0, 0))


def _cat(refs):
    parts = [r[...].astype(F32) for r in refs]
    return parts[0] if len(parts) == 1 else jnp.concatenate(parts, axis=1)


def _col_specs(rows, off, width, cb, row_map):
    assert off % cb == 0 and width % cb == 0
    return [pl.BlockSpec((rows, cb), (lambda *g, _c=off // cb + t: (row_map(*g), _c))) for t in range(width // cb)]


def _my_pos():
    return lax.axis_index("x"), lax.axis_index("y"), lax.axis_index("c")


def _allgather_small(x, name):
    r, c = x.shape

    def body(x_ref, out_ref, send_sems, recv_sems, local_sem):
        mx, my, mc = _my_pos()
        me = 4 * mx + 2 * my + mc
        mine = pltpu.make_async_copy(x_ref, out_ref.at[me], local_sem)
        mine.start()
        deltas = [(dx, dy, dc) for dx in (0, 1) for dy in (0, 1) for dc in (0, 1) if (dx, dy, dc) != (0, 0, 0)]
        sends = []
        for k, (dx, dy, dc) in enumerate(deltas):
            px, py, pc = (mx + dx) % 2, (my + dy) % 2, (mc + dc) % 2
            cp = pltpu.make_async_remote_copy(
                src_ref=x_ref, dst_ref=out_ref.at[me], send_sem=send_sems.at[k], recv_sem=recv_sems.at[k],
                device_id=(px, py, pc), device_id_type=MESH)
            cp.start()
            sends.append(cp)
        for k, (dx, dy, dc) in enumerate(deltas):
            px, py, pc = (mx + dx) % 2, (my + dy) % 2, (mc + dc) % 2
            peer = 4 * px + 2 * py + pc
            pltpu.make_async_remote_copy(
                src_ref=x_ref, dst_ref=out_ref.at[peer], send_sem=send_sems.at[k], recv_sem=recv_sems.at[k],
                device_id=(px, py, pc), device_id_type=MESH).wait_recv()
        for cp in sends:
            cp.wait_send()
        mine.wait()

    return pl.pallas_call(
        body, name=name,
        out_shape=jax.ShapeDtypeStruct((N_DEV, r, c), x.dtype),
        in_specs=[VMEM_SPEC], out_specs=VMEM_SPEC,
        scratch_shapes=[pltpu.SemaphoreType.DMA((N_DEV - 1,)), pltpu.SemaphoreType.DMA((N_DEV - 1,)),
                        pltpu.SemaphoreType.DMA],
        compiler_params=pltpu.CompilerParams(vmem_limit_bytes=VMEM_LIMIT),
    )(x)


class _Sharded:
    def __init__(self, full_shape, by_cols):
        self.full = full_shape
        self.by_cols = by_cols
        rows, cols = full_shape
        if by_cols:
            self.shard, self.half, self.halves = (rows, cols // N_CHIP), (rows // 2, cols // N_CHIP), (rows // 2, cols)
        else:
            self.shard, self.half, self.halves = (rows // N_CHIP, cols), (rows // N_CHIP, cols // 2), (rows, cols // 2)

    def region(self, ref, s, h):
        if self.by_cols:
            return ref.at[pl.ds(h * self.half[0], self.half[0]), pl.ds(s * self.shard[1], self.shard[1])]
        return ref.at[pl.ds(s * self.shard[0], self.shard[0]), pl.ds(h * self.half[1], self.half[1])]

    def halves_of_full(self, ref, h):
        if self.by_cols:
            return ref.at[pl.ds(h * self.halves[0], self.halves[0]), :]
        return ref.at[:, pl.ds(h * self.halves[1], self.halves[1])]

    def region_in_halves(self, ref, s):
        if self.by_cols:
            return ref.at[:, pl.ds(s * self.shard[1], self.shard[1])]
        return ref.at[pl.ds(s * self.shard[0], self.shard[0]), :]

    def half_of_shard(self, ref, h):
        if self.by_cols:
            return ref.at[pl.ds(h * self.half[0], self.half[0]), :]
        return ref.at[:, pl.ds(h * self.half[1], self.half[1])]


def _row_tile(rows, row_bytes):
    for t in (512, 256, 128, 64, 32, 16):
        if rows % t == 0 and t * row_bytes <= 2 * 1024 * 1024:
            return t
    return 16


def _cast_into_full(w_shard, lay, chip_arr, name):
    r, c = lay.shard
    tr = _row_tile(r, c * 4)
    nt = r // tr

    def body(chip_ref, w_ref, o_ref):
        o_ref[...] = w_ref[...].astype(BF16)

    if lay.by_cols:
        out_map = lambda i, chip_ref: (i, chip_ref[0])
    else:
        out_map = lambda i, chip_ref: (chip_ref[0] * nt + i, 0)
    return pl.pallas_call(
        body, name=name,
        grid_spec=pltpu.PrefetchScalarGridSpec(
            num_scalar_prefetch=1, grid=(nt,),
            in_specs=[pl.BlockSpec((tr, c), lambda i, chip_ref: (i, 0))],
            out_specs=pl.BlockSpec((tr, c), out_map)),
        out_shape=jax.ShapeDtypeStruct(lay.full, BF16), compiler_params=_cp(("parallel",)))(chip_arr, w_shard)


class _Carried:
    def __init__(self, ins, out_shapes, aliases, sem_shape, start, finish):
        self.ins, self.out_shapes, self.aliases, self.sem_shape = list(ins), list(out_shapes), dict(aliases), sem_shape
        self.start, self.finish = start, finish

    def scratch(self):
        return [pltpu.SemaphoreType.DMA(self.sem_shape), pltpu.SemaphoreType.DMA(self.sem_shape)]

    def split(self, in_refs, out_refs, scratch_refs):
        ni, no = len(self.ins), len(self.out_shapes)
        return in_refs[len(in_refs) - ni:], out_refs[len(out_refs) - no:], scratch_refs[-2], scratch_refs[-1]


def _call(body, *, name, grid, in_specs, out_specs, out_shape, args, sem, comm=None):
    if comm is None:
        return pl.pallas_call(body, name=name, grid=grid, in_specs=in_specs, out_specs=out_specs,
                              out_shape=out_shape, compiler_params=_cp(sem))(*args)
    n_in, n_out, n_ci, n_co = len(in_specs), len(out_specs), len(comm.ins), len(comm.out_shapes)

    def carrying(*refs):
        in_refs, out_refs = refs[:n_in + n_ci], refs[n_in + n_ci:n_in + n_ci + n_out + n_co]
        carried = comm.split(in_refs, out_refs, refs[n_in + n_ci + n_out + n_co:])
        first, last = None, None
        for axis, extent in enumerate(grid):
            at0, at1 = pl.program_id(axis) == 0, pl.program_id(axis) == extent - 1
            first = at0 if first is None else jnp.logical_and(first, at0)
            last = at1 if last is None else jnp.logical_and(last, at1)

        @pl.when(first)
        def _():
            comm.start(*carried)

        body(*in_refs[:n_in], *out_refs[:n_out])

        @pl.when(last)
        def _():
            comm.finish(*carried)

    return pl.pallas_call(
        carrying, name=name, grid=grid, in_specs=list(in_specs) + [ANY] * n_ci,
        out_specs=list(out_specs) + [ANY] * n_co, out_shape=list(out_shape) + comm.out_shapes,
        scratch_shapes=comm.scratch(), input_output_aliases={n_in + s: n_out + d for s, d in comm.aliases.items()},
        compiler_params=_cp(("arbitrary",) * len(grid)))(*args, *comm.ins)


def _gather_comm(fulls, layouts):
    n = len(fulls)

    def ici(ins, outs, send_sems, recv_sems, a, j, landed=False):
        mx, my, mc = _my_pos()
        dx, dy = CHIP_DELTAS[j]
        px, py = (mx + dx) % 2, (my + dy) % 2
        src_chip = 2 * px + py if landed else 2 * mx + my
        return pltpu.make_async_remote_copy(
            src_ref=layouts[a].region(ins[a], src_chip, mc), dst_ref=layouts[a].region(outs[a], src_chip, mc),
            send_sem=send_sems.at[a, j], recv_sem=recv_sems.at[a, j], device_id=(px, py, mc), device_id_type=MESH)

    def d2d(ins, outs, send_sems, recv_sems, a, j, landed=False):
        mx, my, mc = _my_pos()
        dx, dy = CHIP_DELTAS[j]
        other = 2 * ((mx + dx) % 2) + (my + dy) % 2
        half = 1 - mc if landed else mc
        region = layouts[a].region(outs[a], other, half)
        return pltpu.make_async_remote_copy(
            src_ref=region, dst_ref=region, send_sem=send_sems.at[a, 3 + j], recv_sem=recv_sems.at[a, 3 + j],
            device_id=(mx, my, 1 - mc), device_id_type=MESH)

    pairs = [(a, j) for a in range(n) for j in range(3)]

    def start(*r):
        for a, j in pairs:
            ici(*r, a, j).start()

    def finish(*r):
        for a, j in pairs:
            ici(*r, a, j, landed=True).wait_recv()
            d2d(*r, a, j).start()
        for a, j in pairs:
            d2d(*r, a, j, landed=True).wait_recv()
        for a, j in pairs:
            ici(*r, a, j).wait_send()
            d2d(*r, a, j).wait_send()

    return _Carried(fulls, [jax.ShapeDtypeStruct(lay.full, BF16) for lay in layouts], {a: a for a in range(n)},
                    (n, 6), start, finish)


def _pair_exchange(grads, layouts, name):
    n = len(grads)

    def body(*refs):
        ins, outs = refs[:n], refs[n:2 * n]
        send_sems, recv_sems = refs[2 * n:]
        mx, my, mc = _my_pos()
        copies = []
        for a, lay in enumerate(layouts):
            cp = pltpu.make_async_remote_copy(
                src_ref=lay.halves_of_full(ins[a], 1 - mc), dst_ref=outs[a],
                send_sem=send_sems.at[a], recv_sem=recv_sems.at[a],
                device_id=(mx, my, 1 - mc), device_id_type=MESH)
            cp.start()
            copies.append(cp)
        for cp in copies:
            cp.wait()

    return pl.pallas_call(
        body, name=name,
        out_shape=[jax.ShapeDtypeStruct(lay.halves, g.dtype) for lay, g in zip(layouts, grads)],
        in_specs=[ANY] * n, out_specs=[ANY] * n,
        scratch_shapes=[pltpu.SemaphoreType.DMA((n,)), pltpu.SemaphoreType.DMA((n,))],
    )(*grads)


def _pair_sum(g, theirs, lay, core_arr, name):
    r, c = lay.halves
    tr = _row_tile(r, c * 4)
    nt = r // tr

    def body(core_ref, g_ref, t_ref, o_ref):
        o_ref[...] = (g_ref[...].astype(F32) + t_ref[...].astype(F32)).astype(BF16)

    if lay.by_cols:
        g_map = lambda i, core_ref: (core_ref[0] * nt + i, 0)
    else:
        g_map = lambda i, core_ref: (i, core_ref[0])
    plain = pl.BlockSpec((tr, c), lambda i, core_ref: (i, 0))
    return pl.pallas_call(
        body, name=name,
        grid_spec=pltpu.PrefetchScalarGridSpec(
            num_scalar_prefetch=1, grid=(nt,), in_specs=[pl.BlockSpec((tr, c), g_map), plain], out_specs=plain),
        out_shape=jax.ShapeDtypeStruct((r, c), BF16), compiler_params=_cp(("parallel",)))(core_arr, g, theirs)


def _scatter_comm(pair_sums, layouts):
    n = len(pair_sums)

    def copy(ins, outs, send_sems, recv_sems, a, j):
        mx, my, mc = _my_pos()
        dx, dy = CHIP_DELTAS[j]
        px, py = (mx + dx) % 2, (my + dy) % 2
        return pltpu.make_async_remote_copy(
            src_ref=layouts[a].region_in_halves(ins[a], 2 * px + py), dst_ref=outs[a].at[j],
            send_sem=send_sems.at[a, j], recv_sem=recv_sems.at[a, j], device_id=(px, py, mc), device_id_type=MESH)

    pairs = [(a, j) for a in range(n) for j in range(3)]

    def start(*r):
        for a, j in pairs:
            copy(*r, a, j).start()

    def finish(*r):
        for a, j in pairs:
            copy(*r, a, j).wait()

    return _Carried(pair_sums, [jax.ShapeDtypeStruct((3,) + lay.half, BF16) for lay in layouts], {}, (n, 3),
                    start, finish)


def _chip_sum(pair_sum, slots, lay, chip_arr, core_arr, name):
    r, c = lay.half
    tr = _row_tile(r, c * 4)
    nt = r // tr

    def body(chip_ref, core_ref, s_ref, slot_ref, o_ref):
        acc = s_ref[...].astype(F32)
        for j in range(3):
            acc = acc + slot_ref[j].astype(F32)
        o_ref[...] = acc

    if lay.by_cols:
        s_map = lambda i, chip_ref, core_ref: (i, chip_ref[0])
        o_map = lambda i, chip_ref, core_ref: (core_ref[0] * nt + i, 0)
    else:
        s_map = lambda i, chip_ref, core_ref: (chip_ref[0] * nt + i, 0)
        o_map = lambda i, chip_ref, core_ref: (i, core_ref[0])
    return pl.pallas_call(
        body, name=name,
        grid_spec=pltpu.PrefetchScalarGridSpec(
            num_scalar_prefetch=2, grid=(nt,),
            in_specs=[pl.BlockSpec((tr, c), s_map),
                      pl.BlockSpec((3, tr, c), lambda i, chip_ref, core_ref: (0, i, 0))],
            out_specs=pl.BlockSpec((tr, c), o_map)),
        out_shape=jax.ShapeDtypeStruct(lay.shard, F32), compiler_params=_cp(("parallel",)))(
            chip_arr, core_arr, pair_sum, slots)


def _share_comm(bufs, layouts):
    n = len(bufs)

    def copy(ins, outs, send_sems, recv_sems, a, landed=False):
        mx, my, mc = _my_pos()
        half = 1 - mc if landed else mc
        return pltpu.make_async_remote_copy(
            src_ref=layouts[a].half_of_shard(ins[a], half), dst_ref=layouts[a].half_of_shard(outs[a], half),
            send_sem=send_sems.at[a], recv_sem=recv_sems.at[a], device_id=(mx, my, 1 - mc), device_id_type=MESH)

    def start(*r):
        for a in range(n):
            copy(*r, a).start()

    def finish(*r):
        for a in range(n):
            copy(*r, a, landed=True).wait_recv()
        for a in range(n):
            copy(*r, a).wait_send()

    return _Carried(bufs, [jax.ShapeDtypeStruct(lay.shard, F32) for lay in layouts], {a: a for a in range(n)},
                    (n,), start, finish)


def _allgather_comm(x):
    deltas = [(dx, dy, dc) for dx in (0, 1) for dy in (0, 1) for dc in (0, 1) if (dx, dy, dc) != (0, 0, 0)]
    local = len(deltas)

    def remote(ins, outs, send_sems, recv_sems, k, landed=False):
        mx, my, mc = _my_pos()
        dx, dy, dc = deltas[k]
        px, py, pc = (mx + dx) % 2, (my + dy) % 2, (mc + dc) % 2
        slot = 4 * px + 2 * py + pc if landed else 4 * mx + 2 * my + mc
        return pltpu.make_async_remote_copy(
            src_ref=ins[0], dst_ref=outs[0].at[slot], send_sem=send_sems.at[k], recv_sem=recv_sems.at[k],
            device_id=(px, py, pc), device_id_type=MESH)

    def mine(ins, outs, send_sems, recv_sems):
        mx, my, mc = _my_pos()
        return pltpu.make_async_copy(ins[0], outs[0].at[4 * mx + 2 * my + mc], send_sems.at[local])

    def start(*r):
        mine(*r).start()
        for k in range(len(deltas)):
            remote(*r, k).start()

    def finish(*r):
        for k in range(len(deltas)):
            remote(*r, k, landed=True).wait_recv()
        for k in range(len(deltas)):
            remote(*r, k).wait_send()
        mine(*r).wait()

    return _Carried([x], [jax.ShapeDtypeStruct((N_DEV,) + x.shape, x.dtype)], {}, (N_DEV,), start, finish)


def _sum_slots(x, name):
    s, r, c = x.shape
    tr = _pick(r, (256, 128, 64, 32, 16, 8))

    def body(x_ref, o_ref):
        acc = x_ref[0]
        for k in range(1, s):
            acc = acc + x_ref[k]
        o_ref[...] = acc

    return pl.pallas_call(body, name=name, grid=(r // tr,),
                          in_specs=[pl.BlockSpec((s, tr, c), lambda i: (0, i, 0))],
                          out_specs=pl.BlockSpec((tr, c), lambda i: (i, 0)),
                          out_shape=jax.ShapeDtypeStruct((r, c), F32), compiler_params=_cp(("parallel",)))(x)


def _adamw(w, g, m, v, name, copy_grad=False):
    r, c = w.shape
    tr = _pick(r, (256, 128, 64, 32, 16, 8))
    bc1 = 1.0 - ADAM_B1 ** ADAM_STEP
    bc2 = 1.0 - ADAM_B2 ** ADAM_STEP
    n_out = 4 if copy_grad else 3

    def body(w_ref, g_ref, m_ref, v_ref, d_ref, nm_ref, nv_ref, *g_copy):
        gv = g_ref[...]
        nm = ADAM_B1 * m_ref[...] + (1.0 - ADAM_B1) * gv
        nv = ADAM_B2 * v_ref[...] + (1.0 - ADAM_B2) * (gv * gv)
        d_ref[...] = -ADAM_LR * ((nm / bc1) / (jnp.sqrt(nv / bc2) + ADAM_EPS) + ADAM_WD * w_ref[...])
        nm_ref[...] = nm
        nv_ref[...] = nv
        if copy_grad:
            g_copy[0][...] = gv

    spec = pl.BlockSpec((tr, c), lambda i: (i, 0))
    shp = jax.ShapeDtypeStruct((r, c), F32)
    return _call(body, name=name, grid=(r // tr,), in_specs=[spec] * 4, out_specs=[spec] * n_out,
                 out_shape=[shp] * n_out, args=(w, g, m, v), sem=("parallel",))


def _cctx_grad(parts, c_ctx2d):
    def body(p_ref, c_ref, o_ref):
        tot = ((p_ref[0] + p_ref[2]) + p_ref[4]) + p_ref[6]
        o_ref[...] = tot * _dsilu(c_ref[...])

    return pl.pallas_call(body, name="cctx_grad", in_specs=[VMEM_SPEC, VMEM_SPEC], out_specs=VMEM_SPEC,
                          out_shape=jax.ShapeDtypeStruct(c_ctx2d.shape, F32))(parts, c_ctx2d)


def _mod_fwd(c_rows_t, ada_w, ada_b_shard):
    nl, d, w = ada_w.shape
    td = _pick(d, (256, 128))
    nd = d // td

    def body(ct_ref, w_ref, b_ref, o_ref):
        i = pl.program_id(1)

        @pl.when(i == 0)
        def _():
            o_ref[0] = jnp.broadcast_to(b_ref[0], (MOD_ROWS, w))

        st = _silu(ct_ref[...])
        wv = w_ref[0]
        rows = [jnp.sum(st[:, r:r + 1] * wv, axis=0, keepdims=True) for r in range(MOD_ROWS)]
        o_ref[0] += jnp.concatenate(rows, axis=0)

    return pl.pallas_call(
        body, name="mod_fwd", grid=(nl, nd),
        in_specs=[pl.BlockSpec((td, MOD_ROWS), lambda l, i: (i, 0)),
                  pl.BlockSpec((1, td, w), lambda l, i: (l, i, 0)),
                  pl.BlockSpec((1, 1, w), lambda l, i: (l, 0, 0))],
        out_specs=pl.BlockSpec((1, MOD_ROWS, w), lambda l, i: (l, 0, 0)),
        out_shape=jax.ShapeDtypeStruct((nl, MOD_ROWS, w), F32),
        compiler_params=_cp(("parallel", "arbitrary")),
    )(c_rows_t, ada_w, ada_b_shard)


def _mod_bwd(c_rows_t, dmod, ada_w):
    nl, d, w = ada_w.shape
    td = _pick(d, (256, 128))
    ctx_row = N_DEV

    def body(ct_ref, dm_ref, w_ref, gw_ref, ds_ref):
        st = _silu(ct_ref[...])
        dm = dm_ref[0]
        acc = st[:, 0:1] * dm[0:1, :]
        for r in range(1, ctx_row + 1):
            acc = acc + st[:, r:r + 1] * dm[r:r + 1, :]
        gw_ref[0] = acc
        ds_ref[0] = jnp.sum(w_ref[0] * dm[ctx_row:ctx_row + 1, :], axis=1, keepdims=True)

    return pl.pallas_call(
        body, name="mod_bwd", grid=(nl, d // td),
        in_specs=[pl.BlockSpec((td, MOD_ROWS), lambda l, i: (i, 0)),
                  pl.BlockSpec((1, MOD_ROWS, w), lambda l, i: (l, 0, 0)),
                  pl.BlockSpec((1, td, w), lambda l, i: (l, i, 0))],
        out_specs=[pl.BlockSpec((1, td, w), lambda l, i: (l, i, 0)),
                   pl.BlockSpec((1, td, 1), lambda l, i: (l, i, 0))],
        out_shape=[jax.ShapeDtypeStruct((nl, d, w), F32), jax.ShapeDtypeStruct((nl, d, 1), F32)],
        compiler_params=_cp(("parallel", "parallel")),
    )(c_rows_t, dmod, ada_w)


def _adaln_fwd(x, g, shift, scale, tr, name, comm=None):
    r, d = x.shape

    def body(x_ref, g_ref, sh_ref, sc_ref, o_ref):
        xv = x_ref[...]
        rs = lax.rsqrt(jnp.mean(xv * xv, axis=-1, keepdims=True) + EPS)
        o_ref[...] = ((xv * rs * g_ref[...]) * (1.0 + sc_ref[...]) + sh_ref[...]).astype(BF16)

    spec = pl.BlockSpec((tr, d), lambda i: (i, 0))
    outs = _call(body, name=name, grid=(r // tr,), in_specs=[spec, _vec(d), _vec(d), _vec(d)], out_specs=[spec],
                 out_shape=[jax.ShapeDtypeStruct((r, d), BF16)], args=(x, g, shift, scale), sem=("parallel",),
                 comm=comm)
    return outs if comm else outs[0]


def _adaln_bwd(xin, dh, row0, g, scale, dg_init, tr, name, dres=None, o_prev=None, gate_prev=None, comm=None):
    r, d = xin.shape
    assert row0 % tr == 0
    rb0 = row0 // tr
    want_dx = dres is not None
    want_prev = o_prev is not None
    assert want_dx or not want_prev

    def body(*refs):
        it = iter(refs)
        x_ref, dh_ref, g_ref, sc_ref, dgi_ref = next(it), next(it), next(it), next(it), next(it)
        dres_ref = next(it) if want_dx else None
        o_ref, gp_ref = (next(it), next(it)) if want_prev else (None, None)
        dx_ref = next(it) if want_dx else None
        do_ref = next(it) if want_prev else None
        dsh_ref, dsc_ref, dg_ref = next(it), next(it), next(it)
        dgp_ref = next(it) if want_prev else None
        i = pl.program_id(0)

        @pl.when(i == 0)
        def _():
            dsh_ref[...] = jnp.zeros_like(dsh_ref)
            dsc_ref[...] = jnp.zeros_like(dsc_ref)
            dg_ref[...] = dgi_ref[...]
            if want_prev:
                dgp_ref[...] = jnp.zeros_like(dgp_ref)

        xv = x_ref[...]
        dhv = dh_ref[...].astype(F32)
        gv = g_ref[...]
        rs = lax.rsqrt(jnp.mean(xv * xv, axis=-1, keepdims=True) + EPS)
        xn = xv * rs
        dsh_ref[...] += jnp.sum(dhv, axis=0, keepdims=True)
        dsc_ref[...] += jnp.sum(dhv * (xn * gv), axis=0, keepdims=True)
        dr = dhv * (1.0 + sc_ref[...])
        dg_ref[...] += jnp.sum(dr * xn, axis=0, keepdims=True)
        if want_dx:
            gy = dr * gv
            dx = dres_ref[...] + rs * (gy - xn * jnp.mean(gy * xn, axis=-1, keepdims=True))
            dx_ref[...] = dx
            if want_prev:
                do_ref[...] = (gp_ref[...] * dx).astype(BF16)
                dgp_ref[...] += jnp.sum(dx * o_ref[...].astype(F32), axis=0, keepdims=True)

    row = pl.BlockSpec((tr, d), lambda i: (i, 0))
    in_specs = [row, pl.BlockSpec((tr, d), lambda i: (rb0 + i, 0)), _vec(d), _vec(d), _vec(d)]
    args = [xin, dh, g, scale, dg_init]
    out_specs, out_shape, names = [], [], []
    if want_dx:
        in_specs.append(row)
        args.append(dres)
    if want_prev:
        in_specs += [row, _vec(d)]
        args += [o_prev, gate_prev]
    if want_dx:
        out_specs.append(row)
        out_shape.append(jax.ShapeDtypeStruct((r, d), F32))
        names.append("dx")
    if want_prev:
        out_specs.append(row)
        out_shape.append(jax.ShapeDtypeStruct((r, d), BF16))
        names.append("do_prev")
    for nm in ("dshift", "dscale", "dg") + (("dgate_prev",) if want_prev else ()):
        out_specs.append(_vec(d))
        out_shape.append(jax.ShapeDtypeStruct((1, d), F32))
        names.append(nm)
    outs = _call(body, name=name, grid=(r // tr,), in_specs=in_specs, out_specs=out_specs, out_shape=out_shape,
                 args=args, sem=("arbitrary",), comm=comm)
    res = dict(zip(names, outs))
    if comm:
        res["carried"] = outs[len(names):]
    return res


def _mm(a, b, *, name, tm, tn, tk, ta=False, tb=False, out_dtype=F32, res=None, gate=None, comm=None):
    if ta:
        kd, m = a.shape
    else:
        m, kd = a.shape
    if tb:
        n, kd2 = b.shape
    else:
        kd2, n = b.shape
    assert kd == kd2 and m % tm == 0 and n % tn == 0 and kd % tk == 0, (a.shape, b.shape, tm, tn, tk)
    ni, nj, nk = m // tm, n // tn, kd // tk
    dn = (((0 if ta else 1,), (1 if tb else 0,)), ((), ()))
    with_res = res is not None
    n_in = 4 if with_res else 2
    n_out = 2 if with_res else 1
    n_cin = len(comm.ins) if comm else 0
    n_cout = len(comm.out_shapes) if comm else 0

    def body(*refs):
        in_refs = refs[:n_in + n_cin]
        out_refs = refs[n_in + n_cin:n_in + n_cin + n_out + n_cout]
        scratch = refs[n_in + n_cin + n_out + n_cout:]
        a_ref, b_ref = in_refs[0], in_refs[1]
        o_ref = out_refs[0]
        i, j, k = pl.program_id(0), pl.program_id(1), pl.program_id(2)
        if comm:
            carried = comm.split(in_refs, out_refs, scratch)

            @pl.when(jnp.logical_and(jnp.logical_and(i == 0, j == 0), k == 0))
            def _():
                comm.start(*carried)

        def emit(acc):
            o_ref[...] = acc.astype(o_ref.dtype)
            if with_res:
                out_refs[1][...] = in_refs[2][...] + in_refs[3][...] * acc

        if nk == 1:
            emit(lax.dot_general(a_ref[...], b_ref[...], dn, preferred_element_type=F32))
        else:
            acc_ref = scratch[0]

            @pl.when(k == 0)
            def _():
                acc_ref[...] = jnp.zeros_like(acc_ref)

            acc_ref[...] += lax.dot_general(a_ref[...], b_ref[...], dn, preferred_element_type=F32)

            @pl.when(k == nk - 1)
            def _():
                emit(acc_ref[...])

        if comm:
            @pl.when(jnp.logical_and(jnp.logical_and(i == ni - 1, j == nj - 1), k == nk - 1))
            def _():
                comm.finish(*carried)

    a_spec = pl.BlockSpec((tk, tm), lambda i, j, k: (k, i)) if ta else pl.BlockSpec((tm, tk), lambda i, j, k: (i, k))
    b_spec = pl.BlockSpec((tn, tk), lambda i, j, k: (j, k)) if tb else pl.BlockSpec((tk, tn), lambda i, j, k: (k, j))
    o_spec = pl.BlockSpec((tm, tn), lambda i, j, k: (i, j))
    in_specs, args = [a_spec, b_spec], [a, b]
    out_specs, out_shape = [o_spec], [jax.ShapeDtypeStruct((m, n), out_dtype)]
    if with_res:
        in_specs += [o_spec, pl.BlockSpec((1, tn), lambda i, j, k: (0, j))]
        args += [res, gate]
        out_specs.append(o_spec)
        out_shape.append(jax.ShapeDtypeStruct((m, n), F32))
    scratch_shapes = [pltpu.VMEM((tm, tn), F32)] if nk > 1 else []
    aliases = {}
    sem = ("parallel", "parallel", "arbitrary")
    if comm:
        in_specs += [ANY] * n_cin
        args += comm.ins
        out_specs += [ANY] * n_cout
        out_shape += comm.out_shapes
        scratch_shapes += comm.scratch()
        aliases = {n_in + s: n_out + d for s, d in comm.aliases.items()}
        sem = ("arbitrary", "arbitrary", "arbitrary")
    outs = pl.pallas_call(body, name=name, grid=(ni, nj, nk), in_specs=in_specs, out_specs=out_specs,
                          out_shape=out_shape, scratch_shapes=scratch_shapes, input_output_aliases=aliases,
                          compiler_params=_cp(sem))(*args)
    return outs if (with_res or comm) else outs[0]


def _swap_pairs(x):
    lane = lax.broadcasted_iota(jnp.int32, x.shape, 1)
    return jnp.where(lane % 2 == 0, pltpu.roll(x, HEAD_DIM - 1, 1), pltpu.roll(x, 1, 1))


def _qk_prep(p0, cos_i, sin_s, q_norm, k_norm, dims, tr):
    rows = p0.shape[0]
    kvw, aw, cb = dims["kv_w"], dims["attn_w"], dims["cb"]
    nkv, nq = kvw // HEAD_DIM, aw // HEAD_DIM
    scale = HEAD_DIM ** -0.5
    n_kv_specs, n_q_specs = (2 * kvw) // cb, aw // cb

    def body(*refs):
        kv_refs = refs[:n_kv_specs]
        q_refs = refs[n_kv_specs:n_kv_specs + n_q_specs]
        cos_ref, sin_ref, qn_ref, kn_ref, qo_ref, ko_ref, vo_ref = refs[n_kv_specs + n_q_specs:]
        kv = _cat(kv_refs)
        qv = _cat(q_refs)
        cs, sn = cos_ref[...], sin_ref[...]

        def norm_rope(xh, gvec):
            rs = lax.rsqrt(jnp.mean(xh * xh, axis=-1, keepdims=True) + EPS)
            xn = xh * rs * gvec
            return xn * cs + _swap_pairs(xn) * sn

        for h in range(nkv):
            sl = slice(h * HEAD_DIM, (h + 1) * HEAD_DIM)
            ko_ref[:, sl] = norm_rope(kv[:, sl], kn_ref[...]).astype(BF16)
        vo_ref[...] = kv[:, kvw:].astype(BF16)
        for h in range(nq):
            sl = slice(h * HEAD_DIM, (h + 1) * HEAD_DIM)
            qo_ref[:, sl] = (norm_rope(qv[:, sl], qn_ref[...]) * scale).astype(BF16)

    rm = lambda i: i
    in_specs = (_col_specs(tr, 0, 2 * kvw, cb, rm) + _col_specs(tr, 2 * kvw, aw, cb, rm)
                + [pl.BlockSpec((tr, HEAD_DIM), lambda i: (i, 0))] * 2 + [_vec(HEAD_DIM)] * 2)
    args = [p0] * (n_kv_specs + n_q_specs) + [cos_i, sin_s, q_norm, k_norm]
    return pl.pallas_call(
        body, name="qk_prep", grid=(rows // tr,), in_specs=in_specs,
        out_specs=[pl.BlockSpec((tr, aw), lambda i: (i, 0)), pl.BlockSpec((tr, kvw), lambda i: (i, 0)),
                   pl.BlockSpec((tr, kvw), lambda i: (i, 0))],
        out_shape=[jax.ShapeDtypeStruct((rows, aw), BF16), jax.ShapeDtypeStruct((rows, kvw), BF16),
                   jax.ShapeDtypeStruct((rows, kvw), BF16)],
        compiler_params=_cp(("parallel",)))(*args)


def _qk_prep_bwd(dq_hat, dk_hat, dv, p0, cos_i, sin_s, q_norm, k_norm, dims, tr, n_lat):
    rows = p0.shape[0]
    kvw, aw, cb = dims["kv_w"], dims["attn_w"], dims["cb"]
    nkv, nq = kvw // HEAD_DIM, aw // HEAD_DIM
    scale = HEAD_DIM ** -0.5
    n_kv_specs, n_q_specs = (2 * kvw) // cb, aw // cb
    lat_tiles = n_lat // tr

    def body(*refs):
        kv_refs = refs[:n_kv_specs]
        q_refs = refs[n_kv_specs:n_kv_specs + n_q_specs]
        (dq_ref, dk_ref, dv_ref, cos_ref, sin_ref, qn_ref, kn_ref,
         out_ref, dqn_ref, dkn_ref) = refs[n_kv_specs + n_q_specs:]
        i = pl.program_id(0)

        @pl.when(i == 0)
        def _():
            dqn_ref[...] = jnp.zeros_like(dqn_ref)
            dkn_ref[...] = jnp.zeros_like(dkn_ref)

        kv = _cat(kv_refs)
        qv = _cat(q_refs)
        cs, sn = cos_ref[...], sin_ref[...]
        is_lat = (i < lat_tiles).astype(F32)

        def head_bwd(xh, dhat, gvec):
            dn = dhat * cs + _swap_pairs(dhat * sn)
            rs = lax.rsqrt(jnp.mean(xh * xh, axis=-1, keepdims=True) + EPS)
            xn = xh * rs
            gy = dn * gvec
            dx = rs * (gy - xn * jnp.mean(gy * xn, axis=-1, keepdims=True))
            return dx, jnp.sum(dn * xn, axis=0, keepdims=True)

        dkn = jnp.zeros((1, HEAD_DIM), F32)
        for h in range(nkv):
            sl = slice(h * HEAD_DIM, (h + 1) * HEAD_DIM)
            dx, dgv = head_bwd(kv[:, sl], dk_ref[:, sl], kn_ref[...])
            out_ref[:, sl] = dx.astype(BF16)
            dkn = dkn + dgv
        dkn_ref[...] += dkn
        out_ref[:, kvw:2 * kvw] = dv_ref[...].astype(BF16)
        dqn = jnp.zeros((1, HEAD_DIM), F32)
        for h in range(nq):
            sl = slice(h * HEAD_DIM, (h + 1) * HEAD_DIM)
            dx, dgv = head_bwd(qv[:, sl], dq_ref[:, sl] * (scale * is_lat), qn_ref[...])
            out_ref[:, 2 * kvw + h * HEAD_DIM:2 * kvw + (h + 1) * HEAD_DIM] = dx.astype(BF16)
            dqn = dqn + dgv
        dqn_ref[...] += dqn

    rm = lambda i: i
    wout = 2 * kvw + aw
    in_specs = (_col_specs(tr, 0, 2 * kvw, cb, rm) + _col_specs(tr, 2 * kvw, aw, cb, rm)
                + [pl.BlockSpec((tr, aw), lambda i: (jnp.minimum(i, lat_tiles - 1), 0)),
                   pl.BlockSpec((tr, kvw), lambda i: (i, 0)), pl.BlockSpec((tr, kvw), lambda i: (i, 0)),
                   pl.BlockSpec((tr, HEAD_DIM), lambda i: (i, 0)), pl.BlockSpec((tr, HEAD_DIM), lambda i: (i, 0)),
                   _vec(HEAD_DIM), _vec(HEAD_DIM)])
    args = [p0] * (n_kv_specs + n_q_specs) + [dq_hat, dk_hat, dv, cos_i, sin_s, q_norm, k_norm]
    return pl.pallas_call(
        body, name="qk_prep_bwd", grid=(rows // tr,), in_specs=in_specs,
        out_specs=[pl.BlockSpec((tr, wout), lambda i: (i, 0)), _vec(HEAD_DIM), _vec(HEAD_DIM)],
        out_shape=[jax.ShapeDtypeStruct((rows, wout), BF16), jax.ShapeDtypeStruct((1, HEAD_DIM), F32),
                   jax.ShapeDtypeStruct((1, HEAD_DIM), F32)],
        compiler_params=_cp(("arbitrary",)))(*args)


def _stack_heads(x, g):
    return jnp.concatenate([x[:, h * HEAD_DIM:(h + 1) * HEAD_DIM] for h in range(g)], axis=0)


def _flash_fwd(q_hat, k_all, v_all, n_lat, dims, tq, tk):
    kvw, aw = dims["kv_w"], dims["attn_w"]
    nkv = kvw // HEAD_DIM
    g = aw // kvw
    gw = g * HEAD_DIM
    n_keys = k_all.shape[0]
    ni, nj = n_lat // tq, n_keys // tk
    dn_nt = (((1,), (1,)), ((), ()))

    def body(q_ref, k_ref, v_ref, o_ref, lse_ref):
        qs = _stack_heads(q_ref[...], g)
        m = jnp.full((g * tq, 1), -1e30, F32)
        l = jnp.zeros((g * tq, 1), F32)
        acc = jnp.zeros((g * tq, HEAD_DIM), F32)
        for j in range(nj):
            kb = k_ref[pl.ds(j * tk, tk), :]
            vb = v_ref[pl.ds(j * tk, tk), :]
            s = lax.dot_general(qs, kb, dn_nt, preferred_element_type=F32)
            m_new = jnp.maximum(m, jnp.max(s, axis=-1, keepdims=True))
            alpha = jnp.exp(m - m_new)
            p = jnp.exp(s - m_new)
            l = alpha * l + jnp.sum(p, axis=-1, keepdims=True)
            acc = alpha * acc + jnp.dot(p.astype(BF16), vb, preferred_element_type=F32)
            m = m_new
        o = acc / l
        for h in range(g):
            o_ref[:, h * HEAD_DIM:(h + 1) * HEAD_DIM] = o[h * tq:(h + 1) * tq]
        lse_ref[...] = m + jnp.log(l)

    return pl.pallas_call(
        body, name="flash_fwd", grid=(nkv, ni),
        in_specs=[pl.BlockSpec((tq, gw), lambda h, i: (i, h)),
                  pl.BlockSpec((n_keys, HEAD_DIM), lambda h, i: (0, h)),
                  pl.BlockSpec((n_keys, HEAD_DIM), lambda h, i: (0, h))],
        out_specs=[pl.BlockSpec((tq, gw), lambda h, i: (i, h)),
                   pl.BlockSpec((g * tq, 1), lambda h, i: (h * ni + i, 0))],
        out_shape=[jax.ShapeDtypeStruct((n_lat, aw), F32), jax.ShapeDtypeStruct((nkv * ni * g * tq, 1), F32)],
        compiler_params=_cp(("parallel", "parallel")))(q_hat, k_all, v_all)


def _flash_bwd(q_hat, k_all, v_all, do, o, lse, n_lat, dims, tq, tk):
    kvw, aw = dims["kv_w"], dims["attn_w"]
    nkv = kvw // HEAD_DIM
    g = aw // kvw
    gw = g * HEAD_DIM
    n_keys = k_all.shape[0]
    ni, nj = n_lat // tq, n_keys // tk
    dn_nt = (((1,), (1,)), ((), ()))
    dn_tn = (((0,), (0,)), ((), ()))

    def body(q_ref, k_ref, v_ref, do_ref, o_ref, lse_ref, dq_ref, dk_ref, dv_ref):
        i = pl.program_id(1)

        @pl.when(i == 0)
        def _():
            dk_ref[...] = jnp.zeros_like(dk_ref)
            dv_ref[...] = jnp.zeros_like(dv_ref)

        dos = _stack_heads(do_ref[...], g)
        qs = _stack_heads(q_ref[...], g)
        delta = jnp.sum(dos.astype(F32) * _stack_heads(o_ref[...], g), axis=-1, keepdims=True)
        lse_v = lse_ref[...]
        dq = jnp.zeros((g * tq, HEAD_DIM), F32)
        for j in range(nj):
            rows = pl.ds(j * tk, tk)
            kb, vb = k_ref[rows, :], v_ref[rows, :]
            s = lax.dot_general(qs, kb, dn_nt, preferred_element_type=F32)
            p = jnp.exp(s - lse_v)
            dp = lax.dot_general(dos, vb, dn_nt, preferred_element_type=F32)
            ds = (p * (dp - delta)).astype(BF16)
            dv_ref[rows, :] += lax.dot_general(p.astype(BF16), dos, dn_tn, preferred_element_type=F32)
            dk_ref[rows, :] += lax.dot_general(ds, qs, dn_tn, preferred_element_type=F32)
            dq = dq + jnp.dot(ds, kb, preferred_element_type=F32)
        for h in range(g):
            dq_ref[:, h * HEAD_DIM:(h + 1) * HEAD_DIM] = dq[h * tq:(h + 1) * tq]

    qspec = pl.BlockSpec((tq, gw), lambda h, i: (i, h))
    full_k = pl.BlockSpec((n_keys, HEAD_DIM), lambda h, i: (0, h))
    return pl.pallas_call(
        body, name="flash_bwd", grid=(nkv, ni),
        in_specs=[qspec, full_k, full_k, qspec, qspec, pl.BlockSpec((g * tq, 1), lambda h, i: (h * ni + i, 0))],
        out_specs=[qspec, full_k, full_k],
        out_shape=[jax.ShapeDtypeStruct((n_lat, aw), F32), jax.ShapeDtypeStruct((n_keys, kvw), F32),
                   jax.ShapeDtypeStruct((n_keys, kvw), F32)],
        compiler_params=_cp(("parallel", "arbitrary")))(q_hat, k_all, v_all, do, o, lse)


def _shifted_copies(pad_ref, sh_ref, tr):
    rows = tr + 2 * CONV_HALO - SUBLANES
    for r in range(SUBLANES):
        sh_ref[r] = pad_ref[pl.ds(r, rows), :]


def _stencil(sh_ref, w_ref, out_ref, offsets, tr, cc, init_ref=None):
    for c0 in range(0, cc, LANES):
        lanes = pl.ds(c0, LANES)
        wv = [jnp.broadcast_to(w_ref[pl.ds(k, 1), lanes], (STENCIL_ROWS, LANES)) for k in range(len(offsets))]
        if init_ref is None:
            init = jnp.zeros((STENCIL_ROWS, LANES), F32)
        else:
            init = jnp.broadcast_to(init_ref[:, lanes], (STENCIL_ROWS, LANES))

        def block(rb, carry, lanes=lanes, wv=wv, init=init):
            r0 = rb * STENCIL_ROWS
            parts = [init] + [None] * (STENCIL_CHAINS - 1)
            for k, o in enumerate(offsets):
                rows = pl.ds(SUBLANES * (o // SUBLANES) + r0, STENCIL_ROWS)
                term = sh_ref[o % SUBLANES, rows, lanes] * wv[k]
                q = k % STENCIL_CHAINS
                parts[q] = term if parts[q] is None else parts[q] + term
            acc = parts[0]
            for p_ in parts[1:]:
                acc = acc + p_
            out_ref[pl.ds(r0, STENCIL_ROWS), lanes] = acc
            return carry

        for rb in range(tr // STENCIL_ROWS):
            block(rb, 0)


def _stencil_weight_grad(sh_ref, d_ref, d_row0, dw_ref, offsets, tr, cc):
    rows_per = REDUCE_ROWS
    for c0 in range(0, cc, LANES):
        lanes = pl.ds(c0, LANES)

        def block(rb, accs, lanes=lanes):
            r0 = rb * rows_per
            dblk = d_ref[pl.ds(d_row0 + r0, rows_per), lanes]
            out = []
            for k, o in enumerate(offsets):
                prod = dblk * sh_ref[o % SUBLANES, pl.ds(SUBLANES * (o // SUBLANES) + r0, rows_per), lanes]
                part = prod[0:SUBLANES]
                for q in range(1, rows_per // SUBLANES):
                    part = part + prod[q * SUBLANES:(q + 1) * SUBLANES]
                out.append(accs[k] + part)
            return tuple(out)

        zero = jnp.zeros((SUBLANES, LANES), F32)
        accs = tuple(zero for _ in offsets)
        for rb in range(tr // rows_per):
            accs = block(rb, accs)
        for k in range(len(offsets)):
            dw_ref[pl.ds(k, 1), lanes] += jnp.sum(accs[k], axis=0, keepdims=True)


def _halo_maps(tr, n_tiles):
    per = tr // CONV_HALO
    prev = lambda i: jnp.maximum(i * per - 1, 0)
    nxt = lambda i: (i + 1) * per
    return prev, nxt


def _mix_fwd(attn, p0, dw_w, dw_b, ln_g, ln_b, dims, tr):
    n_lat, aw = attn.shape
    cc, cb, cw = dims["conv_ch"], dims["cb"], dims["conv_w"]
    off = dims["off"]
    n_tiles = n_lat // tr
    pad = cw // 2
    na, nc = aw // cb, cc // cb
    prev_map, next_map = _halo_maps(tr, n_tiles)

    def body(*refs):
        it = iter(refs)
        attn_ref = next(it)
        za = [next(it) for _ in range(na)]
        a_c = [next(it) for _ in range(nc)]
        b_c = [next(it) for _ in range(nc)]
        zb = [next(it) for _ in range(nc)]
        a_p = [next(it) for _ in range(nc)]
        b_p = [next(it) for _ in range(nc)]
        a_n = [next(it) for _ in range(nc)]
        b_n = [next(it) for _ in range(nc)]
        w_ref, db_ref, g_ref, bb_ref, mix_ref, yc_ref, ypad, ysh = (next(it) for _ in range(8))
        i = pl.program_id(0)
        ypad[pl.ds(0, CONV_HALO), :] = _cat(a_p) * _sigmoid(_cat(b_p)) * (i > 0).astype(F32)
        ypad[pl.ds(CONV_HALO, tr), :] = _cat(a_c) * _sigmoid(_cat(b_c))
        ypad[pl.ds(CONV_HALO + tr, CONV_HALO), :] = _cat(a_n) * _sigmoid(_cat(b_n)) * (i < n_tiles - 1).astype(F32)
        _shifted_copies(ypad, ysh, tr)
        _stencil(ysh, w_ref, yc_ref, [CONV_HALO - pad + k for k in range(cw)], tr, cc, init_ref=db_ref)
        acc = yc_ref[...]
        mu = jnp.mean(acc, axis=-1, keepdims=True)
        xc = acc - mu
        rs = lax.rsqrt(jnp.mean(xc * xc, axis=-1, keepdims=True) + EPS)
        nv = xc * rs * g_ref[...] + bb_ref[...]
        mix_ref[:, :aw] = (attn_ref[...] * _silu(_cat(za))).astype(BF16)
        mix_ref[:, aw:] = (_silu(nv) * _silu(_cat(zb))).astype(BF16)

    rm = lambda i: i
    in_specs = ([pl.BlockSpec((tr, aw), lambda i: (i, 0))]
                + _col_specs(tr, off["za"], aw, cb, rm) + _col_specs(tr, off["a"], cc, cb, rm)
                + _col_specs(tr, off["b"], cc, cb, rm) + _col_specs(tr, off["zb"], cc, cb, rm)
                + _col_specs(CONV_HALO, off["a"], cc, cb, prev_map) + _col_specs(CONV_HALO, off["b"], cc, cb, prev_map)
                + _col_specs(CONV_HALO, off["a"], cc, cb, next_map) + _col_specs(CONV_HALO, off["b"], cc, cb, next_map)
                + [pl.BlockSpec(dw_w.shape, lambda i: (0, 0)), _vec(cc), _vec(cc), _vec(cc)])
    args = [attn] + [p0] * (na + 7 * nc) + [dw_w, dw_b, ln_g, ln_b]
    return pl.pallas_call(
        body, name="mix_fwd", grid=(n_tiles,), in_specs=in_specs,
        out_specs=[pl.BlockSpec((tr, aw + cc), lambda i: (i, 0)), pl.BlockSpec((tr, cc), lambda i: (i, 0))],
        out_shape=[jax.ShapeDtypeStruct((n_lat, aw + cc), BF16), jax.ShapeDtypeStruct((n_lat, cc), F32)],
        scratch_shapes=[pltpu.VMEM((tr + 2 * CONV_HALO, cc), F32),
                        pltpu.VMEM((SUBLANES, tr + 2 * CONV_HALO - SUBLANES, cc), F32)],
        compiler_params=_cp(("parallel",)))(*args)


def _mix_bwd_pointwise(dmix, attn, p0, yc, ln_g, ln_b, dims, tr, n_ext):
    n_lat, aw = attn.shape
    cc, cb, off = dims["conv_ch"], dims["cb"], dims["off"]
    na, nc = aw // cb, cc // cb
    lat_tiles = n_lat // tr

    def body(*refs):
        it = iter(refs)
        dmix_ref, attn_ref = next(it), next(it)
        za = [next(it) for _ in range(na)]
        zb = [next(it) for _ in range(nc)]
        yc_ref, g_ref, bb_ref = next(it), next(it), next(it)
        dattn_ref, dza_ref, dzb_ref, dyc_ref, dg_ref, dbb_ref, ddb_ref = (next(it) for _ in range(7))
        i = pl.program_id(0)

        @pl.when(i == 0)
        def _():
            dg_ref[...] = jnp.zeros_like(dg_ref)
            dbb_ref[...] = jnp.zeros_like(dbb_ref)
            ddb_ref[...] = jnp.zeros_like(ddb_ref)

        lat = (i < lat_tiles).astype(F32)
        dm = dmix_ref[...].astype(F32)
        dma, dmb = dm[:, :aw], dm[:, aw:]
        zav, zbv = _cat(za), _cat(zb)
        sa, dsa = _silu_and_grad(zav)
        sb, dsb = _silu_and_grad(zbv)
        dattn_ref[...] = (dma * sa).astype(BF16)
        dza_ref[...] = (dma * attn_ref[...] * dsa * lat).astype(BF16)
        ycv = yc_ref[...]
        mu = jnp.mean(ycv, axis=-1, keepdims=True)
        xc = ycv - mu
        rs = lax.rsqrt(jnp.mean(xc * xc, axis=-1, keepdims=True) + EPS)
        xh = xc * rs
        nv = xh * g_ref[...] + bb_ref[...]
        sn, dsn = _silu_and_grad(nv)
        dzb_ref[...] = (dmb * sn * dsb * lat).astype(BF16)
        dn = dmb * sb * dsn
        dg_ref[...] += lat * jnp.sum(dn * xh, axis=0, keepdims=True)
        dbb_ref[...] += lat * jnp.sum(dn, axis=0, keepdims=True)
        dxh = dn * g_ref[...]
        dyc = rs * (dxh - jnp.mean(dxh, axis=-1, keepdims=True) - xh * jnp.mean(dxh * xh, axis=-1, keepdims=True))
        dyc_ref[...] = dyc
        ddb_ref[...] += lat * jnp.sum(dyc, axis=0, keepdims=True)

    rm = lambda i: jnp.minimum(i, lat_tiles - 1)
    row = lambda w: pl.BlockSpec((tr, w), lambda i: (rm(i), 0))
    ext = lambda w: pl.BlockSpec((tr, w), lambda i: (i, 0))
    in_specs = ([row(aw + cc), row(aw)] + _col_specs(tr, off["za"], aw, cb, rm)
                + _col_specs(tr, off["zb"], cc, cb, rm) + [row(cc), _vec(cc), _vec(cc)])
    args = [dmix, attn] + [p0] * (na + nc) + [yc, ln_g, ln_b]
    return pl.pallas_call(
        body, name="mix_bwd_pointwise", grid=(n_ext // tr,), in_specs=in_specs,
        out_specs=[row(aw), ext(aw), ext(cc), row(cc), _vec(cc), _vec(cc), _vec(cc)],
        out_shape=[jax.ShapeDtypeStruct((n_lat, aw), BF16), jax.ShapeDtypeStruct((n_ext, aw), BF16),
                   jax.ShapeDtypeStruct((n_ext, cc), BF16), jax.ShapeDtypeStruct((n_lat, cc), F32)]
                  + [jax.ShapeDtypeStruct((1, cc), F32)] * 3,
        compiler_params=_cp(("arbitrary",)))(*args)


def _conv_bwd(dyc, p0, dw_w, dims, tr, n_ext):
    n_lat, cc = dyc.shape
    cb, cw, off = dims["cb"], dims["conv_w"], dims["off"]
    nc = cc // cb
    n_tiles = n_lat // tr
    pad = cw // 2
    prev_lat, next_lat = _halo_maps(tr, n_tiles)
    rm = lambda i: jnp.minimum(i, n_tiles - 1)
    prev_map = lambda i: prev_lat(rm(i))
    next_map = lambda i: next_lat(rm(i))

    def body(*refs):
        it = iter(refs)
        a_c = [next(it) for _ in range(nc)]
        b_c = [next(it) for _ in range(nc)]
        a_p = [next(it) for _ in range(nc)]
        b_p = [next(it) for _ in range(nc)]
        a_n = [next(it) for _ in range(nc)]
        b_n = [next(it) for _ in range(nc)]
        d_c, d_p, d_n, w_ref, dab_ref, dw_ref, ypad, dpad, ysh, dsh, dy_scr = (next(it) for _ in range(11))
        i = pl.program_id(0)

        @pl.when(i == 0)
        def _():
            dw_ref[...] = jnp.zeros_like(dw_ref)

        @pl.when(i >= n_tiles)
        def _():
            dab_ref[...] = jnp.zeros_like(dab_ref)

        @pl.when(i < n_tiles)
        def _():
            first, last = (i > 0).astype(F32), (i < n_tiles - 1).astype(F32)
            av, bv = _cat(a_c), _cat(b_c)
            sg = _sigmoid(bv)
            ypad[pl.ds(0, CONV_HALO), :] = _cat(a_p) * _sigmoid(_cat(b_p)) * first
            ypad[pl.ds(CONV_HALO, tr), :] = av * sg
            ypad[pl.ds(CONV_HALO + tr, CONV_HALO), :] = _cat(a_n) * _sigmoid(_cat(b_n)) * last
            dpad[pl.ds(0, CONV_HALO), :] = d_p[...] * first
            dpad[pl.ds(CONV_HALO, tr), :] = d_c[...]
            dpad[pl.ds(CONV_HALO + tr, CONV_HALO), :] = d_n[...] * last
            _shifted_copies(ypad, ysh, tr)
            _shifted_copies(dpad, dsh, tr)
            _stencil(dsh, w_ref, dy_scr, [CONV_HALO + pad - k for k in range(cw)], tr, cc)
            _stencil_weight_grad(ysh, dpad, CONV_HALO, dw_ref, [CONV_HALO - pad + k for k in range(cw)], tr, cc)
            dy = dy_scr[...]
            dab_ref[:, :cc] = (dy * sg).astype(BF16)
            dab_ref[:, cc:] = (dy * av * sg * (1.0 - sg)).astype(BF16)

    in_specs = (_col_specs(tr, off["a"], cc, cb, rm) + _col_specs(tr, off["b"], cc, cb, rm)
                + _col_specs(CONV_HALO, off["a"], cc, cb, prev_map) + _col_specs(CONV_HALO, off["b"], cc, cb, prev_map)
                + _col_specs(CONV_HALO, off["a"], cc, cb, next_map) + _col_specs(CONV_HALO, off["b"], cc, cb, next_map)
                + [pl.BlockSpec((tr, cc), lambda i: (rm(i), 0)),
                   pl.BlockSpec((CONV_HALO, cc), lambda i: (prev_map(i), 0)),
                   pl.BlockSpec((CONV_HALO, cc), lambda i: (jnp.minimum(next_map(i), n_lat // CONV_HALO - 1), 0)),
                   pl.BlockSpec(dw_w.shape, lambda i: (0, 0))])
    args = [p0] * (6 * nc) + [dyc, dyc, dyc, dw_w]
    return pl.pallas_call(
        body, name="conv_bwd", grid=(n_ext // tr,), in_specs=in_specs,
        out_specs=[pl.BlockSpec((tr, 2 * cc), lambda i: (i, 0)), pl.BlockSpec(dw_w.shape, lambda i: (0, 0))],
        out_shape=[jax.ShapeDtypeStruct((n_ext, 2 * cc), BF16), jax.ShapeDtypeStruct(dw_w.shape, F32)],
        scratch_shapes=[pltpu.VMEM((tr + 2 * CONV_HALO, cc), F32), pltpu.VMEM((tr + 2 * CONV_HALO, cc), F32),
                        pltpu.VMEM((SUBLANES, tr + 2 * CONV_HALO - SUBLANES, cc), F32),
                        pltpu.VMEM((SUBLANES, tr + 2 * CONV_HALO - SUBLANES, cc), F32),
                        pltpu.VMEM((tr, cc), F32)],
        compiler_params=_cp(("arbitrary",)))(*args)


def _sgu_parts(u, v, ln_g, ln_b):
    mu = jnp.mean(v, axis=-1, keepdims=True)
    xc = v - mu
    rs = lax.rsqrt(jnp.mean(xc * xc, axis=-1, keepdims=True) + EPS)
    xh = xc * rs
    return u, xh, rs, xh * ln_g + ln_b


def _sgu_fwd(p1, ln_g, ln_b, ws, bs_t, tr):
    n_lat, w3 = p1.shape
    w = w3 // 3
    ng, ch = ws.shape[0], ws.shape[1]
    gwid = w // ng
    n_ch = tr // ch

    def body(pu_ref, pv_ref, pg_ref, g_ref, b_ref, ws_ref, bs_ref, o_ref):
        u, _, _, vln = _sgu_parts(_gelu(pu_ref[...].astype(F32)), _gelu(pv_ref[...].astype(F32)),
                                  g_ref[...], b_ref[...])
        gate = _silu(pg_ref[...].astype(F32))
        vb = vln.astype(BF16)
        for c in range(n_ch):
            rs_ = slice(c * ch, (c + 1) * ch)
            for gi in range(ng):
                cs_ = slice(gi * gwid, (gi + 1) * gwid)
                mixed = jnp.dot(ws_ref[gi], vb[rs_, cs_], preferred_element_type=F32) + bs_ref[:, gi:gi + 1]
                o_ref[rs_, cs_] = (u[rs_, cs_] * mixed * gate[rs_, cs_]).astype(BF16)

    col = lambda t: pl.BlockSpec((tr, w), lambda i, _t=t: (i, _t))
    return pl.pallas_call(
        body, name="sgu_fwd", grid=(n_lat // tr,),
        in_specs=[col(0), col(1), col(2), _vec(w), _vec(w),
                  pl.BlockSpec(ws.shape, lambda i: (0, 0, 0)), pl.BlockSpec(bs_t.shape, lambda i: (0, 0))],
        out_specs=pl.BlockSpec((tr, w), lambda i: (i, 0)),
        out_shape=jax.ShapeDtypeStruct((n_lat, w), BF16),
        compiler_params=_cp(("parallel",)))(p1, p1, p1, ln_g, ln_b, ws, bs_t)


def _sgu_bwd(dm, p1, ln_g, ln_b, ws, ws_t, bs_t, tr):
    n_lat, w3 = p1.shape
    w = w3 // 3
    ng, ch = ws.shape[0], ws.shape[1]
    gwid = w // ng
    n_ch = tr // ch
    dn_nt = (((1,), (1,)), ((), ()))

    def body(dm_ref, pu_ref, pv_ref, pg_ref, g_ref, b_ref, ws_ref, wst_ref, bs_ref,
             dp_ref, dws_ref, dbs_ref, dg_ref, dbb_ref, dvln_scr):
        i = pl.program_id(0)

        @pl.when(i == 0)
        def _():
            dws_ref[...] = jnp.zeros_like(dws_ref)
            dbs_ref[...] = jnp.zeros_like(dbs_ref)
            dg_ref[...] = jnp.zeros_like(dg_ref)
            dbb_ref[...] = jnp.zeros_like(dbb_ref)

        puv, pvv, pgv = pu_ref[...].astype(F32), pv_ref[...].astype(F32), pg_ref[...].astype(F32)
        gu, dgu = _gelu_and_grad(puv)
        gv, dgv = _gelu_and_grad(pvv)
        gate, dgate = _silu_and_grad(pgv)
        u, xh, rs, vln = _sgu_parts(gu, gv, g_ref[...], b_ref[...])
        dmv = dm_ref[...].astype(F32)
        vb = vln.astype(BF16)
        dmu = dmv * u
        du_pre = dmv * gate * dgu
        dg_pre = dmu * dgate
        dmix_all = dmu * gate
        dbs_cols = [jnp.zeros((ch, 1), F32) for _ in range(ng)]
        for c in range(n_ch):
            rs_ = slice(c * ch, (c + 1) * ch)
            for gi in range(ng):
                cs_ = slice(gi * gwid, (gi + 1) * gwid)
                mixed = jnp.dot(ws_ref[gi], vb[rs_, cs_], preferred_element_type=F32) + bs_ref[:, gi:gi + 1]
                dmixed = dmix_all[rs_, cs_]
                dmb = dmixed.astype(BF16)
                dp_ref[rs_, gi * gwid:(gi + 1) * gwid] = (du_pre[rs_, cs_] * mixed).astype(BF16)
                dp_ref[rs_, 2 * w + gi * gwid:2 * w + (gi + 1) * gwid] = (dg_pre[rs_, cs_] * mixed).astype(BF16)
                dvln_scr[rs_, cs_] = jnp.dot(wst_ref[gi], dmb, preferred_element_type=F32)
                dws_ref[gi] += lax.dot_general(dmb, vb[rs_, cs_], dn_nt, preferred_element_type=F32)
                dbs_cols[gi] = dbs_cols[gi] + jnp.sum(dmixed, axis=-1, keepdims=True)
        dbs_ref[...] += jnp.concatenate(dbs_cols, axis=1)
        dvln = dvln_scr[...]
        dg_ref[...] += jnp.sum(dvln * xh, axis=0, keepdims=True)
        dbb_ref[...] += jnp.sum(dvln, axis=0, keepdims=True)
        dxh = dvln * g_ref[...]
        dv = rs * (dxh - jnp.mean(dxh, axis=-1, keepdims=True) - xh * jnp.mean(dxh * xh, axis=-1, keepdims=True))
        dp_ref[:, w:2 * w] = (dv * dgv).astype(BF16)

    col = lambda t: pl.BlockSpec((tr, w), lambda i, _t=t: (i, _t))
    return pl.pallas_call(
        body, name="sgu_bwd", grid=(n_lat // tr,),
        in_specs=[pl.BlockSpec((tr, w), lambda i: (i, 0)), col(0), col(1), col(2), _vec(w), _vec(w),
                  pl.BlockSpec(ws.shape, lambda i: (0, 0, 0)), pl.BlockSpec(ws.shape, lambda i: (0, 0, 0)),
                  pl.BlockSpec(bs_t.shape, lambda i: (0, 0))],
        out_specs=[pl.BlockSpec((tr, w3), lambda i: (i, 0)), pl.BlockSpec(ws.shape, lambda i: (0, 0, 0)),
                   pl.BlockSpec(bs_t.shape, lambda i: (0, 0)), _vec(w), _vec(w)],
        out_shape=[jax.ShapeDtypeStruct((n_lat, w3), BF16), jax.ShapeDtypeStruct(ws.shape, F32),
                   jax.ShapeDtypeStruct(bs_t.shape, F32), jax.ShapeDtypeStruct((1, w), F32),
                   jax.ShapeDtypeStruct((1, w), F32)],
        scratch_shapes=[pltpu.VMEM((tr, w), F32)],
        compiler_params=_cp(("arbitrary",)))(dm, p1, p1, p1, ln_g, ln_b, ws, ws_t, bs_t)


def _final_loss(x2, target, final_g, o_prev, gate_prev, tr):
    n_lat, d = x2.shape

    def body(x_ref, t_ref, g_ref, o_ref, gp_ref, dx_ref, do_ref, ls_ref, dg_ref, dgp_ref):
        i = pl.program_id(0)

        @pl.when(i == 0)
        def _():
            ls_ref[...] = jnp.zeros_like(ls_ref)
            dg_ref[...] = jnp.zeros_like(dg_ref)
            dgp_ref[...] = jnp.zeros_like(dgp_ref)

        xv = x_ref[...]
        gv = g_ref[...]
        rs = lax.rsqrt(jnp.mean(xv * xv, axis=-1, keepdims=True) + EPS)
        xn = xv * rs
        err = xn * gv - t_ref[...]
        ls_ref[...] += jnp.sum(err * err, axis=0, keepdims=True)
        dy = err * (1.0 / d)
        dg_ref[...] += jnp.sum(dy * xn, axis=0, keepdims=True)
        gy = dy * gv
        dx = rs * (gy - xn * jnp.mean(gy * xn, axis=-1, keepdims=True))
        dx_ref[...] = dx
        do_ref[...] = (gp_ref[...] * dx).astype(BF16)
        dgp_ref[...] += jnp.sum(dx * o_ref[...].astype(F32), axis=0, keepdims=True)

    row = pl.BlockSpec((tr, d), lambda i: (i, 0))
    return pl.pallas_call(
        body, name="final_loss", grid=(n_lat // tr,), in_specs=[row, row, _vec(d), row, _vec(d)],
        out_specs=[row, row, _vec(d), _vec(d), _vec(d)],
        out_shape=[jax.ShapeDtypeStruct((n_lat, d), F32), jax.ShapeDtypeStruct((n_lat, d), BF16)]
                  + [jax.ShapeDtypeStruct((1, d), F32)] * 3,
        compiler_params=_cp(("arbitrary",)))(x2, target, final_g, o_prev, gate_prev)


def _pack(arrays):
    flat = jnp.concatenate([a.reshape(-1).astype(F32) for a in arrays])
    n = flat.shape[0]
    rows = -(-n // LANES)
    rows = -(-rows // PACK_ROWS) * PACK_ROWS
    return jnp.pad(flat, (0, rows * LANES - n)).reshape(rows, LANES)


def _unpack(buf, shapes):
    flat = buf.reshape(buf.shape[:-2] + (-1,))
    out, pos = [], 0
    for shp in shapes:
        n = math.prod(shp)
        out.append(flat[..., pos:pos + n].reshape(buf.shape[:-2] + tuple(shp)))
        pos += n
    return out


def _rope_tables(n_lat, n_ctx):
    rows = n_lat // GRID_W
    row = jnp.repeat(jnp.arange(rows, dtype=F32), GRID_W)
    col = jnp.tile(jnp.arange(GRID_W, dtype=F32), rows)
    n_freq, axis_dim = HEAD_DIM // 4, HEAD_DIM // 2
    inv = jnp.power(ROPE_THETA, jnp.arange(n_freq, dtype=F32) * (-2.0 / axis_dim))
    ang = jnp.concatenate([row[:, None] * inv, col[:, None] * inv], axis=-1)
    cos, sin = jnp.cos(ang), jnp.sin(ang)
    cos_i = jnp.repeat(cos, 2, axis=-1)
    sin_s = jnp.stack([-sin, sin], axis=-1).reshape(n_lat, HEAD_DIM)
    cos_i = jnp.concatenate([cos_i, jnp.ones((n_ctx, HEAD_DIM), F32)], axis=0)
    sin_s = jnp.concatenate([sin_s, jnp.zeros((n_ctx, HEAD_DIM), F32)], axis=0)
    return cos_i, sin_s


def kernel(x, c, ctx, c_ctx, ada_w, ada_b, norm_g, ev_w_in, ev_q_norm, ev_k_norm, ev_dw_w, ev_dw_b, ev_ln_g, ev_ln_b, ev_w_out, od_w_in, od_ln_g, od_ln_b, od_ws, od_bs, od_w_out, final_g, loss_target, m_c_ctx, m_ada_w, m_ada_b, m_norm_g, m_ev_w_in, m_ev_q_norm, m_ev_k_norm, m_ev_dw_w, m_ev_dw_b, m_ev_ln_g, m_ev_ln_b, m_ev_w_out, m_od_w_in, m_od_ln_g, m_od_ln_b, m_od_ws, m_od_bs, m_od_w_out, m_final_g, v_c_ctx, v_ada_w, v_ada_b, v_norm_g, v_ev_w_in, v_ev_q_norm, v_ev_k_norm, v_ev_dw_w, v_ev_dw_b, v_ev_ln_g, v_ev_ln_b, v_ev_w_out, v_od_w_in, v_od_ln_g, v_od_ln_b, v_od_ws, v_od_bs, v_od_w_out, v_final_g):
    weights = dict(c_ctx=c_ctx, ada_w=ada_w, ada_b=ada_b, norm_g=norm_g, ev_w_in=ev_w_in, ev_q_norm=ev_q_norm,
                   ev_k_norm=ev_k_norm, ev_dw_w=ev_dw_w, ev_dw_b=ev_dw_b, ev_ln_g=ev_ln_g, ev_ln_b=ev_ln_b,
                   ev_w_out=ev_w_out, od_w_in=od_w_in, od_ln_g=od_ln_g, od_ln_b=od_ln_b, od_ws=od_ws, od_bs=od_bs,
                   od_w_out=od_w_out, final_g=final_g)
    mom_m = dict(c_ctx=m_c_ctx, ada_w=m_ada_w, ada_b=m_ada_b, norm_g=m_norm_g, ev_w_in=m_ev_w_in,
                 ev_q_norm=m_ev_q_norm, ev_k_norm=m_ev_k_norm, ev_dw_w=m_ev_dw_w, ev_dw_b=m_ev_dw_b,
                 ev_ln_g=m_ev_ln_g, ev_ln_b=m_ev_ln_b, ev_w_out=m_ev_w_out, od_w_in=m_od_w_in, od_ln_g=m_od_ln_g,
                 od_ln_b=m_od_ln_b, od_ws=m_od_ws, od_bs=m_od_bs, od_w_out=m_od_w_out, final_g=m_final_g)
    mom_v = dict(c_ctx=v_c_ctx, ada_w=v_ada_w, ada_b=v_ada_b, norm_g=v_norm_g, ev_w_in=v_ev_w_in,
                 ev_q_norm=v_ev_q_norm, ev_k_norm=v_ev_k_norm, ev_dw_w=v_ev_dw_w, ev_dw_b=v_ev_dw_b,
                 ev_ln_g=v_ev_ln_g, ev_ln_b=v_ev_ln_b, ev_w_out=v_ev_w_out, od_w_in=v_od_w_in, od_ln_g=v_od_ln_g,
                 od_ln_b=v_od_ln_b, od_ws=v_od_ws, od_bs=v_od_bs, od_w_out=v_od_w_out, final_g=v_final_g)
    order = list(weights)

    _, n_lat, d = x.shape
    n_ctx = ctx.shape[1]
    n_ext = n_lat + n_ctx
    ev_in = ev_w_in.shape[-1] * N_CHIP
    ev_mix = ev_w_out.shape[1] * N_CHIP
    conv_ch = ev_dw_b.shape[-1]
    conv_w = ev_dw_w.shape[1]
    attn_w = ev_mix - conv_ch
    kv_w = N_KV_HEADS * HEAD_DIM
    assert ev_in == 2 * kv_w + 2 * attn_w + 3 * conv_ch and conv_w // 2 < CONV_HALO
    sgu_w = od_w_out.shape[1] * N_CHIP
    wa = ada_w.shape[-1]
    cb = math.gcd(2 * kv_w, attn_w, conv_ch)
    off = dict(k=0, v=kv_w, q=2 * kv_w, za=2 * kv_w + attn_w, a=2 * kv_w + 2 * attn_w,
               b=2 * kv_w + 2 * attn_w + conv_ch, zb=2 * kv_w + 2 * attn_w + 2 * conv_ch)
    dims = dict(kv_w=kv_w, attn_w=attn_w, conv_ch=conv_ch, conv_w=conv_w, cb=cb, off=off)
    tr = 256 if (n_lat % 256 == 0 and n_ctx % 256 == 0) else 128

    mx, my, mc = lax.axis_index("x"), lax.axis_index("y"), lax.axis_index("c")
    me = 4 * mx + 2 * my + mc
    chip = 2 * mx + my

    x2d, tgt2d, ctx2d = x[0], loss_target[0], ctx[0]
    ev_dw_w_l = ev_dw_w[0]
    dwc = ev_dw_w_l.shape[1]
    lnc = od_ln_g.shape[1]

    g_c = _allgather_small(jnp.broadcast_to(c, (8, d)), "gather_cond")[:, 0, :]
    c_rows = jnp.concatenate([g_c, c_ctx[None, :], jnp.zeros((MOD_ROWS - N_DEV - 1, d), F32)], axis=0)
    c_rows_t = c_rows.T
    ada_b_shard = lax.dynamic_slice_in_dim(ada_b, chip * wa, wa, axis=1)[:, None, :]
    mod_part = _mod_fwd(c_rows_t, ada_w, ada_b_shard)
    part_shapes = [(2, MOD_ROWS, wa), (conv_w, dwc), (1, lnc), (1, lnc)]
    g_parts = _allgather_small(_pack([mod_part, ev_dw_w_l, od_ln_g, od_ln_b]), "gather_mod")
    per_chip = [_unpack(g_parts[2 * s], part_shapes) for s in range(N_CHIP)]
    mod_all = jnp.concatenate([p[0] for p in per_chip], axis=-1)
    dw_w_full = jnp.concatenate([p[1] for p in per_chip], axis=-1)
    od_ln_g_full = jnp.concatenate([p[2] for p in per_chip], axis=-1)
    od_ln_b_full = jnp.concatenate([p[3] for p in per_chip], axis=-1)
    dw_w_pad = jnp.pad(dw_w_full, ((0, 2 * CONV_HALO - conv_w), (0, 0)))
    mod_me = lax.dynamic_slice_in_dim(mod_all, me, 1, axis=1)
    shift0, scale0, gate0 = mod_me[0, :, :d], mod_me[0, :, d:2 * d], mod_me[0, :, 2 * d:]
    shift1, scale1, gate1 = mod_me[1, :, :d], mod_me[1, :, d:2 * d], mod_me[1, :, 2 * d:]
    shift_c, scale_c = mod_all[0, N_DEV:N_DEV + 1, :d], mod_all[0, N_DEV:N_DEV + 1, d:2 * d]
    g0, g1 = norm_g[0:1], norm_g[1:2]

    lay = dict(ev_w_in=_Sharded((d, ev_in), True), ev_w_out=_Sharded((ev_mix, d), False),
               od_w_in=_Sharded((d, 3 * sgu_w), True), od_w_out=_Sharded((sgu_w, d), False))
    big = list(lay)
    chip_arr = jnp.reshape(chip, (1,)).astype(jnp.int32)
    core_arr = jnp.reshape(mc, (1,)).astype(jnp.int32)
    own = {n: _cast_into_full(weights[n][0], lay[n], chip_arr, f"cast_{n}") for n in big}
    later = big[1:]
    gather_later = _gather_comm([own[n] for n in later], [lay[n] for n in later])

    def reduce_start(g, n):
        theirs = _pair_exchange([g], [lay[n]], f"pair_exchange_{n}")[0]
        psum = _pair_sum(g, theirs, lay[n], core_arr, f"pair_sum_{n}")
        return psum, _scatter_comm([psum], [lay[n]])

    h0, w_ev_in = _adaln_fwd(x2d, g0, shift0, scale0, tr, "adaln0_fwd",
                             comm=_gather_comm([own["ev_w_in"]], [lay["ev_w_in"]]))
    w_full = {"ev_w_in": w_ev_in}
    hc = _adaln_fwd(ctx2d, g0, shift_c, scale_c, tr, "adaln0_ctx_fwd")
    h0e = jnp.concatenate([h0, hc], axis=0)
    tm_e = _pick(n_ext, (1408, 768, 640, 512, 256, 128))
    tk_e = _pick(n_ext, (768, 640, 512, 256, 128))
    tm_l = _pick(n_lat, (1024, 512, 256, 128))
    p0, *gathered = _mm(h0e, w_full["ev_w_in"], name="mm_ev_in", tm=tm_e, tn=_pick(ev_in, (512, 256, 128)), tk=d,
                        out_dtype=BF16, comm=gather_later)
    w_full.update(zip(later, gathered))
    cos_i, sin_s = _rope_tables(n_lat, n_ctx)
    q_hat, k_all, v_all = _qk_prep(p0, cos_i, sin_s, ev_q_norm, ev_k_norm, dims, tr)
    tq = _pick(n_lat, (256, 128))
    tkk = _pick(n_ext, (1408, 640, 512, 256, 128))
    attn, lse = _flash_fwd(q_hat, k_all, v_all, n_lat, dims, tq, tkk)
    mix, yc = _mix_fwd(attn, p0, dw_w_pad, ev_dw_b, ev_ln_g, ev_ln_b, dims, tr)
    o0, x1 = _mm(mix, w_full["ev_w_out"], name="mm_ev_out", tm=tm_l, tn=_pick(d, (1024, 512, 256)),
                 tk=ev_mix, out_dtype=BF16, res=x2d, gate=gate0)

    h1 = _adaln_fwd(x1, g1, shift1, scale1, tr, "adaln1_fwd")
    p1 = _mm(h1, w_full["od_w_in"], name="mm_od_in", tm=tm_l, tn=_pick(3 * sgu_w, (512, 256, 128)), tk=d,
             out_dtype=BF16)
    ws_b = od_ws[0].astype(BF16)
    ws_t_b = jnp.swapaxes(od_ws[0], 1, 2).astype(BF16)
    bs_t = od_bs[0].T
    m1 = _sgu_fwd(p1, od_ln_g_full, od_ln_b_full, ws_b, bs_t, tr)
    o1, x2 = _mm(m1, w_full["od_w_out"], name="mm_od_out", tm=tm_l, tn=_pick(d, (1024, 512, 256)),
                 tk=sgu_w, out_dtype=BF16, res=x1, gate=gate1)

    dx2, do1, loss_cols, d_final_g, dgate1 = _final_loss(x2, tgt2d, final_g[None, :], o1, gate1, tr)
    loss = lax.psum(0.5 / d * jnp.sum(loss_cols), ("x", "y", "c"))

    tk_l = _pick(n_lat, (1024, 512, 256, 128))
    psums, slots = {}, {}
    tk_nt = (1408, 1024, 768, 512, 256, 128)
    g_od_w_out = _mm(m1, do1, name="mm_od_out_dw", ta=True, tm=_pick(sgu_w, (1024, 512, 256)),
                     tn=_pick(d, (1024, 512, 256)), tk=tk_l, out_dtype=BF16)
    psums["od_w_out"], sc = reduce_start(g_od_w_out, "od_w_out")
    dm1, slots["od_w_out"] = _mm(do1, w_full["od_w_out"], name="mm_od_out_dx", tb=True, tm=tm_l,
                                 tn=_pick(sgu_w, (1024, 512, 256)), tk=d, out_dtype=BF16, comm=sc)
    dp1, d_ws, d_bs_t, d_od_ln_g, d_od_ln_b = _sgu_bwd(dm1, p1, od_ln_g_full, od_ln_b_full, ws_b, ws_t_b, bs_t, tr)
    small1 = [dgate1, d_final_g, d_od_ln_g, d_od_ln_b, d_ws, d_bs_t.T]
    g_od_w_in, g_small1 = _mm(h1, dp1, name="mm_od_in_dw", ta=True, tm=_pick(d, (1024, 512, 256)),
                              tn=_pick(3 * sgu_w, (1536, 768, 512, 384, 256, 128)), tk=tk_l, out_dtype=BF16,
                              comm=_allgather_comm(_pack(small1)))
    psums["od_w_in"], sc = reduce_start(g_od_w_in, "od_w_in")
    dh1, slots["od_w_in"] = _mm(dp1, w_full["od_w_in"], name="mm_od_in_dx", tb=True, tm=tm_l,
                                tn=_pick(d, (1024, 512, 256)), tk=_pick(3 * sgu_w, tk_nt), out_dtype=BF16, comm=sc)
    zero_d = jnp.zeros((1, d), F32)
    b1 = _adaln_bwd(x1, dh1, 0, g1, scale1, zero_d, tr, "adaln1_bwd", dres=dx2, o_prev=o0, gate_prev=gate0)
    dx1, do0, dgate0 = b1["dx"], b1["do_prev"], b1["dgate_prev"]

    g_ev_w_out = _mm(mix, do0, name="mm_ev_out_dw", ta=True, tm=_pick(ev_mix, (1024, 512, 256)),
                     tn=_pick(d, (1024, 512, 256)), tk=tk_l, out_dtype=BF16)
    psums["ev_w_out"], sc = reduce_start(g_ev_w_out, "ev_w_out")
    dmix, slots["ev_w_out"] = _mm(do0, w_full["ev_w_out"], name="mm_ev_out_dx", tb=True, tm=tm_l,
                                  tn=_pick(ev_mix, (1024, 512, 256)), tk=d, out_dtype=BF16, comm=sc)
    dattn, dza, dzb, dyc, d_ev_ln_g, d_ev_ln_b, d_dw_b = _mix_bwd_pointwise(
        dmix, attn, p0, yc, ev_ln_g, ev_ln_b, dims, tr, n_ext)
    dab, d_dw_w_pad = _conv_bwd(dyc, p0, dw_w_pad, dims, tr, n_ext)
    dq_hat, dk_hat, dv_all = _flash_bwd(q_hat, k_all, v_all, dattn, attn, lse, n_lat, dims, tq, tkk)
    dkvq, d_q_norm, d_k_norm = _qk_prep_bwd(dq_hat, dk_hat, dv_all, p0, cos_i, sin_s, ev_q_norm, ev_k_norm,
                                            dims, tr, n_lat)
    dp0 = jnp.concatenate([dkvq, dza, dab, dzb], axis=1)
    small2 = [b1["dshift"], b1["dscale"], dgate0, b1["dg"], d_q_norm, d_k_norm, d_dw_w_pad[:conv_w], d_dw_b,
              d_ev_ln_g, d_ev_ln_b]
    g_ev_w_in, g_small2 = _mm(h0e, dp0, name="mm_ev_in_dw", ta=True, tm=_pick(d, (1024, 512, 256)),
                              tn=_pick(ev_in, (1408, 768, 512, 256, 128)), tk=tk_e, out_dtype=BF16,
                              comm=_allgather_comm(_pack(small2)))
    psums["ev_w_in"], sc = reduce_start(g_ev_w_in, "ev_w_in")
    dh0, slots["ev_w_in"] = _mm(dp0, w_full["ev_w_in"], name="mm_ev_in_dx", tb=True, tm=tm_e,
                                tn=_pick(d, (1024, 512, 256)), tk=_pick(ev_in, tk_nt), out_dtype=BF16, comm=sc)
    lays = [lay[n] for n in big]
    halves = [_chip_sum(psums[n], slots[n], lay[n], chip_arr, core_arr, f"chip_sum_{n}") for n in big]
    bc = _adaln_bwd(ctx2d, dh0, n_lat, g0, scale_c, zero_d, tr, "adaln0_ctx_bwd")
    b0 = _adaln_bwd(x2d, dh0, 0, g0, scale0, bc["dg"], tr, "adaln0_bwd", dres=dx1, comm=_share_comm(halves, lays))
    g_big = dict(zip(big, b0["carried"]))
    grad_x = b0["dx"]

    small3 = [b0["dshift"], b0["dscale"], bc["dshift"], bc["dscale"], b0["dg"]]
    g_small3 = _allgather_small(_pack(small3), "gather_small_grads")

    def totals_and_rows(gathered, parts, n_rows, name):
        shapes = [a.shape for a in parts]
        tot = _unpack(_sum_slots(gathered, name), shapes)
        return tot, [r[:, 0, :] for r in _unpack(gathered, shapes[:n_rows])]

    (t_dgate1, t_fg, t_oln_g, t_oln_b, t_ws, t_bs), (dgate1_rows,) = totals_and_rows(
        g_small1, small1, 1, "sum_small_grads1")
    ((t_dshift1, t_dscale1, t_dgate0, t_g1, t_qn, t_kn, t_dw_w, t_dw_b, t_eln_g, t_eln_b),
     (dshift1_rows, dscale1_rows, dgate0_rows)) = totals_and_rows(g_small2, small2, 3, "sum_small_grads2")
    (t_dshift0, t_dscale0, t_dshift_c, t_dscale_c, t_g0), (dshift0_rows, dscale0_rows) = totals_and_rows(
        g_small3, small3, 2, "sum_small_grads3")
    zeros_d = jnp.zeros((1, d), F32)
    t_dmodc = jnp.concatenate([t_dshift_c, t_dscale_c, zeros_d], axis=1)
    t_dmod0 = jnp.concatenate([t_dshift0, t_dscale0, t_dgate0], axis=1)
    t_dmod1 = jnp.concatenate([t_dshift1, t_dscale1, t_dgate1], axis=1)
    dmod0_rows = jnp.concatenate([dshift0_rows, dscale0_rows, dgate0_rows], axis=1)
    dmod1_rows = jnp.concatenate([dshift1_rows, dscale1_rows, dgate1_rows], axis=1)
    pad_rows = jnp.zeros((MOD_ROWS - N_DEV - 1, 3 * d), F32)
    dm_l0 = jnp.concatenate([dmod0_rows, t_dmodc, pad_rows], axis=0)
    dm_l1 = jnp.concatenate([dmod1_rows, jnp.zeros((MOD_ROWS - N_DEV, 3 * d), F32)], axis=0)
    dm_shard = lax.dynamic_slice_in_dim(jnp.stack([dm_l0, dm_l1]), chip * wa, wa, axis=2)
    g_ada_w, dsc = _mod_bwd(c_rows_t, dm_shard, ada_w)
    g_dsc = _allgather_small(_pack([dsc[0]]), "gather_cctx")
    g_c_ctx = _cctx_grad(g_dsc, _pack([c_ctx])).reshape(-1)[:d]
    g_ada_b = jnp.stack([t_dmod0[0] + t_dmodc[0], t_dmod1[0]])

    grads = dict(
        c_ctx=g_c_ctx, ada_w=g_ada_w, ada_b=g_ada_b, norm_g=jnp.concatenate([t_g0, t_g1], axis=0),
        ev_w_in=g_big["ev_w_in"][None], ev_q_norm=t_qn, ev_k_norm=t_kn,
        ev_dw_w=lax.dynamic_slice_in_dim(t_dw_w, chip * dwc, dwc, axis=1)[None], ev_dw_b=t_dw_b,
        ev_ln_g=t_eln_g, ev_ln_b=t_eln_b, ev_w_out=g_big["ev_w_out"][None], od_w_in=g_big["od_w_in"][None],
        od_ln_g=lax.dynamic_slice_in_dim(t_oln_g, chip * lnc, lnc, axis=1),
        od_ln_b=lax.dynamic_slice_in_dim(t_oln_b, chip * lnc, lnc, axis=1),
        od_ws=t_ws[None], od_bs=t_bs[None], od_w_out=g_big["od_w_out"][None], final_g=t_fg[0])
    grads = {n: grads[n].reshape(weights[n].shape) for n in order}

    delta, new_m, new_v = {}, {}, {}
    large = ("ada_w", "ev_w_in", "ev_w_out", "od_w_in", "od_w_out")
    for n in large:
        shp = weights[n].shape
        as2d = lambda a: a.reshape(-1, shp[-1])
        outs = _adamw(as2d(weights[n]), as2d(grads[n]), as2d(mom_m[n]), as2d(mom_v[n]), f"adamw_{n}",
                      copy_grad=n in big)
        delta[n], new_m[n], new_v[n] = (o.reshape(shp) for o in outs[:3])
        if n in big:
            grads[n] = outs[3].reshape(shp)
    rest_names = [n for n in order if n not in large]
    rest_shapes = [weights[n].shape for n in rest_names]
    dl, nm, nv = _adamw(_pack([weights[n] for n in rest_names]), _pack([grads[n] for n in rest_names]),
                        _pack([mom_m[n] for n in rest_names]), _pack([mom_v[n] for n in rest_names]), "adamw_small")
    for n, a, b_, c_ in zip(rest_names, _unpack(dl, rest_shapes), _unpack(nm, rest_shapes), _unpack(nv, rest_shapes)):
        delta[n], new_m[n], new_v[n] = a, b_, c_

    return (loss, grad_x[None], *[grads[n] for n in order], *[delta[n] for n in order],
            *[new_m[n] for n in order], *[new_v[n] for n in order])
```

```python
import math

import jax
import jax.numpy as jnp
from jax import lax
from jax.experimental import pallas as pl
from jax.experimental.pallas import tpu as pltpu

F32 = jnp.float32
BF16 = jnp.bfloat16
EPS = 1e-6
GRID_W = 64
ROPE_THETA = 10000.0
HEAD_DIM = 128
N_KV_HEADS = 2
CONV_HALO = 16
LANES = 128
SUBLANES = 8
STENCIL_ROWS = 32
STENCIL_CHAINS = 4
REDUCE_ROWS = 32
N_DEV = 8
N_CHIP = 4
MOD_ROWS = 16
PACK_ROWS = 64
ADAM_LR, ADAM_B1, ADAM_B2, ADAM_EPS, ADAM_WD, ADAM_STEP = 0.001, 0.9, 0.999, 1e-08, 0.01, 10
VMEM_LIMIT = 56 * 1024 * 1024
MESH = pl.DeviceIdType.MESH
ANY = pl.BlockSpec(memory_space=pl.ANY)
VMEM_SPEC = pl.BlockSpec(memory_space=pltpu.VMEM)
CHIP_DELTAS = ((1, 0), (0, 1), (1, 1))


def _cp(sem=None):
    return pltpu.CompilerParams(dimension_semantics=sem, vmem_limit_bytes=VMEM_LIMIT)


def _pick(n, cands):
    for c in cands:
        if n % c == 0:
            return c
    raise ValueError(f"no tile for {n} in {cands}")


def _sigmoid(x):
    return 1.0 / (1.0 + jnp.exp(-x))


def _silu(x):
    return x * _sigmoid(x)


def _silu_and_grad(x):
    s = _sigmoid(x)
    y = x * s
    return y, s + y * (1.0 - s)


def _dsilu(x):
    return _silu_and_grad(x)[1]


_GELU_C = math.sqrt(2.0 / math.pi)
_GELU_A = 0.044715


def _gelu_and_grad(x):
    x2 = x * x
    t = jnp.tanh(x * (_GELU_C + (_GELU_C * _GELU_A) * x2))
    h = 0.5 + 0.5 * t
    return x * h, h + x * (1.0 - t * t) * (0.5 * _GELU_C + (1.5 * _GELU_C * _GELU_A) * x2)


def _gelu(x):
    x2 = x * x
    return x * (0.5 + 0.5 * jnp.tanh(x * (_GELU_C + (_GELU_C * _GELU_A) * x2)))


def _vec(d):
    return pl.BlockSpec((1, d), lambda *_: (0, 0))


def _cat(refs):
    parts = [r[...].astype(F32) for r in refs]
    return parts[0] if len(parts) == 1 else jnp.concatenate(parts, axis=1)


def _col_specs(rows, off, width, cb, row_map):
    assert off % cb == 0 and width % cb == 0
    return [pl.BlockSpec((rows, cb), (lambda *g, _c=off // cb + t: (row_map(*g), _c))) for t in range(width // cb)]


def _my_pos():
    return lax.axis_index("x"), lax.axis_index("y"), lax.axis_index("c")


def _allgather_small(x, name):
    r, c = x.shape

    def body(x_ref, out_ref, send_sems, recv_sems, local_sem):
        mx, my, mc = _my_pos()
        me = 4 * mx + 2 * my + mc
        mine = pltpu.make_async_copy(x_ref, out_ref.at[me], local_sem)
        mine.start()
        deltas = [(dx, dy, dc) for dx in (0, 1) for dy in (0, 1) for dc in (0, 1) if (dx, dy, dc) != (0, 0, 0)]
        sends = []
        for k, (dx, dy, dc) in enumerate(deltas):
            px, py, pc = (mx + dx) % 2, (my + dy) % 2, (mc + dc) % 2
            cp = pltpu.make_async_remote_copy(
                src_ref=x_ref, dst_ref=out_ref.at[me], send_sem=send_sems.at[k], recv_sem=recv_sems.at[k],
                device_id=(px, py, pc), device_id_type=MESH)
            cp.start()
            sends.append(cp)
        for k, (dx, dy, dc) in enumerate(deltas):
            px, py, pc = (mx + dx) % 2, (my + dy) % 2, (mc + dc) % 2
            peer = 4 * px + 2 * py + pc
            pltpu.make_async_remote_copy(
                src_ref=x_ref, dst_ref=out_ref.at[peer], send_sem=send_sems.at[k], recv_sem=recv_sems.at[k],
                device_id=(px, py, pc), device_id_type=MESH).wait_recv()
        for cp in sends:
            cp.wait_send()
        mine.wait()

    return pl.pallas_call(
        body, name=name,
        out_shape=jax.ShapeDtypeStruct((N_DEV, r, c), x.dtype),
        in_specs=[VMEM_SPEC], out_specs=VMEM_SPEC,
        scratch_shapes=[pltpu.SemaphoreType.DMA((N_DEV - 1,)), pltpu.SemaphoreType.DMA((N_DEV - 1,)),
                        pltpu.SemaphoreType.DMA],
        compiler_params=pltpu.CompilerParams(vmem_limit_bytes=VMEM_LIMIT),
    )(x)


class _Sharded:
    def __init__(self, full_shape, by_cols):
        self.full = full_shape
        self.by_cols = by_cols
        rows, cols = full_shape
        if by_cols:
            self.shard, self.half, self.halves = (rows, cols // N_CHIP), (rows // 2, cols // N_CHIP), (rows // 2, cols)
        else:
            self.shard, self.half, self.halves = (rows // N_CHIP, cols), (rows // N_CHIP, cols // 2), (rows, cols // 2)

    def region(self, ref, s, h):
        if self.by_cols:
            return ref.at[pl.ds(h * self.half[0], self.half[0]), pl.ds(s * self.shard[1], self.shard[1])]
        return ref.at[pl.ds(s * self.shard[0], self.shard[0]), pl.ds(h * self.half[1], self.half[1])]

    def halves_of_full(self, ref, h):
        if self.by_cols:
            return ref.at[pl.ds(h * self.halves[0], self.halves[0]), :]
        return ref.at[:, pl.ds(h * self.halves[1], self.halves[1])]

    def region_in_halves(self, ref, s):
        if self.by_cols:
            return ref.at[:, pl.ds(s * self.shard[1], self.shard[1])]
        return ref.at[pl.ds(s * self.shard[0], self.shard[0]), :]

    def half_of_shard(self, ref, h):
        if self.by_cols:
            return ref.at[pl.ds(h * self.half[0], self.half[0]), :]
        return ref.at[:, pl.ds(h * self.half[1], self.half[1])]


def _row_tile(rows, row_bytes):
    for t in (512, 256, 128, 64, 32, 16):
        if rows % t == 0 and t * row_bytes <= 2 * 1024 * 1024:
            return t
    return 16


def _cast_into_full(w_shard, lay, chip_arr, name):
    r, c = lay.shard
    tr = _row_tile(r, c * 4)
    nt = r // tr

    def body(chip_ref, w_ref, o_ref):
        o_ref[...] = w_ref[...].astype(BF16)

    if lay.by_cols:
        out_map = lambda i, chip_ref: (i, chip_ref[0])
    else:
        out_map = lambda i, chip_ref: (chip_ref[0] * nt + i, 0)
    return pl.pallas_call(
        body, name=name,
        grid_spec=pltpu.PrefetchScalarGridSpec(
            num_scalar_prefetch=1, grid=(nt,),
            in_specs=[pl.BlockSpec((tr, c), lambda i, chip_ref: (i, 0))],
            out_specs=pl.BlockSpec((tr, c), out_map)),
        out_shape=jax.ShapeDtypeStruct(lay.full, BF16), compiler_params=_cp(("parallel",)))(chip_arr, w_shard)


class _Carried:
    def __init__(self, ins, out_shapes, aliases, sem_shape, start, finish):
        self.ins, self.out_shapes, self.aliases, self.sem_shape = list(ins), list(out_shapes), dict(aliases), sem_shape
        self.start, self.finish = start, finish

    def scratch(self):
        return [pltpu.SemaphoreType.DMA(self.sem_shape), pltpu.SemaphoreType.DMA(self.sem_shape)]

    def split(self, in_refs, out_refs, scratch_refs):
        ni, no = len(self.ins), len(self.out_shapes)
        return in_refs[len(in_refs) - ni:], out_refs[len(out_refs) - no:], scratch_refs[-2], scratch_refs[-1]


def _call(body, *, name, grid, in_specs, out_specs, out_shape, args, sem, comm=None, aliases=None):
    if comm is None:
        return pl.pallas_call(body, name=name, grid=grid, in_specs=in_specs, out_specs=out_specs,
                              out_shape=out_shape, input_output_aliases=aliases or {},
                              compiler_params=_cp(sem))(*args)
    n_in, n_out, n_ci, n_co = len(in_specs), len(out_specs), len(comm.ins), len(comm.out_shapes)

    def carrying(*refs):
        in_refs, out_refs = refs[:n_in + n_ci], refs[n_in + n_ci:n_in + n_ci + n_out + n_co]
        carried = comm.split(in_refs, out_refs, refs[n_in + n_ci + n_out + n_co:])
        first, last = None, None
        for axis, extent in enumerate(grid):
            at0, at1 = pl.program_id(axis) == 0, pl.program_id(axis) == extent - 1
            first = at0 if first is None else jnp.logical_and(first, at0)
            last = at1 if last is None else jnp.logical_and(last, at1)

        @pl.when(first)
        def _():
            comm.start(*carried)

        body(*in_refs[:n_in], *out_refs[:n_out])

        @pl.when(last)
        def _():
            comm.finish(*carried)

    return pl.pallas_call(
        carrying, name=name, grid=grid, in_specs=list(in_specs) + [ANY] * n_ci,
        out_specs=list(out_specs) + [ANY] * n_co, out_shape=list(out_shape) + comm.out_shapes,
        scratch_shapes=comm.scratch(),
        input_output_aliases={**(aliases or {}), **{n_in + s: n_out + d for s, d in comm.aliases.items()}},
        compiler_params=_cp(("arbitrary",) * len(grid)))(*args, *comm.ins)


def _gather_comm(fulls, layouts):
    n = len(fulls)

    def ici(ins, outs, send_sems, recv_sems, a, j, landed=False):
        mx, my, mc = _my_pos()
        dx, dy = CHIP_DELTAS[j]
        px, py = (mx + dx) % 2, (my + dy) % 2
        src_chip = 2 * px + py if landed else 2 * mx + my
        return pltpu.make_async_remote_copy(
            src_ref=layouts[a].region(ins[a], src_chip, mc), dst_ref=layouts[a].region(outs[a], src_chip, mc),
            send_sem=send_sems.at[a, j], recv_sem=recv_sems.at[a, j], device_id=(px, py, mc), device_id_type=MESH)

    def d2d(ins, outs, send_sems, recv_sems, a, j, landed=False):
        mx, my, mc = _my_pos()
        dx, dy = CHIP_DELTAS[j]
        other = 2 * ((mx + dx) % 2) + (my + dy) % 2
        half = 1 - mc if landed else mc
        region = layouts[a].region(outs[a], other, half)
        return pltpu.make_async_remote_copy(
            src_ref=region, dst_ref=region, send_sem=send_sems.at[a, 3 + j], recv_sem=recv_sems.at[a, 3 + j],
            device_id=(mx, my, 1 - mc), device_id_type=MESH)

    pairs = [(a, j) for a in range(n) for j in range(3)]

    def start(*r):
        for a, j in pairs:
            ici(*r, a, j).start()

    def finish(*r):
        for a, j in pairs:
            ici(*r, a, j, landed=True).wait_recv()
            d2d(*r, a, j).start()
        for a, j in pairs:
            d2d(*r, a, j, landed=True).wait_recv()
        for a, j in pairs:
            ici(*r, a, j).wait_send()
            d2d(*r, a, j).wait_send()

    return _Carried(fulls, [jax.ShapeDtypeStruct(lay.full, BF16) for lay in layouts], {a: a for a in range(n)},
                    (n, 6), start, finish)


def _pair_exchange(grads, layouts, name):
    n = len(grads)

    def body(*refs):
        ins, outs = refs[:n], refs[n:2 * n]
        send_sems, recv_sems = refs[2 * n:]
        mx, my, mc = _my_pos()
        copies = []
        for a, lay in enumerate(layouts):
            cp = pltpu.make_async_remote_copy(
                src_ref=lay.halves_of_full(ins[a], 1 - mc), dst_ref=outs[a],
                send_sem=send_sems.at[a], recv_sem=recv_sems.at[a],
                device_id=(mx, my, 1 - mc), device_id_type=MESH)
            cp.start()
            copies.append(cp)
        for cp in copies:
            cp.wait()

    return pl.pallas_call(
        body, name=name,
        out_shape=[jax.ShapeDtypeStruct(lay.halves, g.dtype) for lay, g in zip(layouts, grads)],
        in_specs=[ANY] * n, out_specs=[ANY] * n,
        scratch_shapes=[pltpu.SemaphoreType.DMA((n,)), pltpu.SemaphoreType.DMA((n,))],
    )(*grads)


def _pair_sum(g, theirs, lay, core_arr, name):
    r, c = lay.halves
    tr = _row_tile(r, c * 4)
    nt = r // tr

    def body(core_ref, g_ref, t_ref, o_ref):
        o_ref[...] = (g_ref[...].astype(F32) + t_ref[...].astype(F32)).astype(BF16)

    if lay.by_cols:
        g_map = lambda i, core_ref: (core_ref[0] * nt + i, 0)
    else:
        g_map = lambda i, core_ref: (i, core_ref[0])
    plain = pl.BlockSpec((tr, c), lambda i, core_ref: (i, 0))
    return pl.pallas_call(
        body, name=name,
        grid_spec=pltpu.PrefetchScalarGridSpec(
            num_scalar_prefetch=1, grid=(nt,), in_specs=[pl.BlockSpec((tr, c), g_map), plain], out_specs=plain),
        out_shape=jax.ShapeDtypeStruct((r, c), BF16), compiler_params=_cp(("parallel",)))(core_arr, g, theirs)


def _scatter_comm(pair_sums, layouts):
    n = len(pair_sums)

    def copy(ins, outs, send_sems, recv_sems, a, j):
        mx, my, mc = _my_pos()
        dx, dy = CHIP_DELTAS[j]
        px, py = (mx + dx) % 2, (my + dy) % 2
        return pltpu.make_async_remote_copy(
            src_ref=layouts[a].region_in_halves(ins[a], 2 * px + py), dst_ref=outs[a].at[j],
            send_sem=send_sems.at[a, j], recv_sem=recv_sems.at[a, j], device_id=(px, py, mc), device_id_type=MESH)

    pairs = [(a, j) for a in range(n) for j in range(3)]

    def start(*r):
        for a, j in pairs:
            copy(*r, a, j).start()

    def finish(*r):
        for a, j in pairs:
            copy(*r, a, j).wait()

    return _Carried(pair_sums, [jax.ShapeDtypeStruct((3,) + lay.half, BF16) for lay in layouts], {}, (n, 3),
                    start, finish)


def _chip_sum(pair_sum, slots, lay, chip_arr, core_arr, name):
    r, c = lay.half
    tr = _row_tile(r, c * 4)
    nt = r // tr

    def body(chip_ref, core_ref, s_ref, slot_ref, o_ref):
        acc = s_ref[...].astype(F32)
        for j in range(3):
            acc = acc + slot_ref[j].astype(F32)
        o_ref[...] = acc

    if lay.by_cols:
        s_map = lambda i, chip_ref, core_ref: (i, chip_ref[0])
        o_map = lambda i, chip_ref, core_ref: (core_ref[0] * nt + i, 0)
    else:
        s_map = lambda i, chip_ref, core_ref: (chip_ref[0] * nt + i, 0)
        o_map = lambda i, chip_ref, core_ref: (i, core_ref[0])
    return pl.pallas_call(
        body, name=name,
        grid_spec=pltpu.PrefetchScalarGridSpec(
            num_scalar_prefetch=2, grid=(nt,),
            in_specs=[pl.BlockSpec((tr, c), s_map),
                      pl.BlockSpec((3, tr, c), lambda i, chip_ref, core_ref: (0, i, 0))],
            out_specs=pl.BlockSpec((tr, c), o_map)),
        out_shape=jax.ShapeDtypeStruct(lay.shard, F32), compiler_params=_cp(("parallel",)))(
            chip_arr, core_arr, pair_sum, slots)


def _share_comm(bufs, layouts):
    n = len(bufs)

    def copy(ins, outs, send_sems, recv_sems, a, landed=False):
        mx, my, mc = _my_pos()
        half = 1 - mc if landed else mc
        return pltpu.make_async_remote_copy(
            src_ref=layouts[a].half_of_shard(ins[a], half), dst_ref=layouts[a].half_of_shard(outs[a], half),
            send_sem=send_sems.at[a], recv_sem=recv_sems.at[a], device_id=(mx, my, 1 - mc), device_id_type=MESH)

    def start(*r):
        for a in range(n):
            copy(*r, a).start()

    def finish(*r):
        for a in range(n):
            copy(*r, a, landed=True).wait_recv()
        for a in range(n):
            copy(*r, a).wait_send()

    return _Carried(bufs, [jax.ShapeDtypeStruct(lay.shard, F32) for lay in layouts], {a: a for a in range(n)},
                    (n,), start, finish)


def _allgather_comm(x):
    deltas = [(dx, dy, dc) for dx in (0, 1) for dy in (0, 1) for dc in (0, 1) if (dx, dy, dc) != (0, 0, 0)]
    local = len(deltas)

    def remote(ins, outs, send_sems, recv_sems, k, landed=False):
        mx, my, mc = _my_pos()
        dx, dy, dc = deltas[k]
        px, py, pc = (mx + dx) % 2, (my + dy) % 2, (mc + dc) % 2
        slot = 4 * px + 2 * py + pc if landed else 4 * mx + 2 * my + mc
        return pltpu.make_async_remote_copy(
            src_ref=ins[0], dst_ref=outs[0].at[slot], send_sem=send_sems.at[k], recv_sem=recv_sems.at[k],
            device_id=(px, py, pc), device_id_type=MESH)

    def mine(ins, outs, send_sems, recv_sems):
        mx, my, mc = _my_pos()
        return pltpu.make_async_copy(ins[0], outs[0].at[4 * mx + 2 * my + mc], send_sems.at[local])

    def start(*r):
        mine(*r).start()
        for k in range(len(deltas)):
            remote(*r, k).start()

    def finish(*r):
        for k in range(len(deltas)):
            remote(*r, k, landed=True).wait_recv()
        for k in range(len(deltas)):
            remote(*r, k).wait_send()
        mine(*r).wait()

    return _Carried([x], [jax.ShapeDtypeStruct((N_DEV,) + x.shape, x.dtype)], {}, (N_DEV,), start, finish)


def _sum_slots(x, name):
    s, r, c = x.shape
    tr = _pick(r, (256, 128, 64, 32, 16, 8))

    def body(x_ref, o_ref):
        acc = x_ref[0]
        for k in range(1, s):
            acc = acc + x_ref[k]
        o_ref[...] = acc

    return pl.pallas_call(body, name=name, grid=(r // tr,),
                          in_specs=[pl.BlockSpec((s, tr, c), lambda i: (0, i, 0))],
                          out_specs=pl.BlockSpec((tr, c), lambda i: (i, 0)),
                          out_shape=jax.ShapeDtypeStruct((r, c), F32), compiler_params=_cp(("parallel",)))(x)


def _adamw(w, g, m, v, name, copy_grad=False):
    r, c = w.shape
    tr = _pick(r, (256, 128, 64, 32, 16, 8))
    bc1 = 1.0 - ADAM_B1 ** ADAM_STEP
    bc2 = 1.0 - ADAM_B2 ** ADAM_STEP
    n_out = 4 if copy_grad else 3

    def body(w_ref, g_ref, m_ref, v_ref, d_ref, nm_ref, nv_ref, *g_copy):
        gv = g_ref[...]
        nm = ADAM_B1 * m_ref[...] + (1.0 - ADAM_B1) * gv
        nv = ADAM_B2 * v_ref[...] + (1.0 - ADAM_B2) * (gv * gv)
        d_ref[...] = -ADAM_LR * ((nm / bc1) / (jnp.sqrt(nv / bc2) + ADAM_EPS) + ADAM_WD * w_ref[...])
        nm_ref[...] = nm
        nv_ref[...] = nv
        if copy_grad:
            g_copy[0][...] = gv

    spec = pl.BlockSpec((tr, c), lambda i: (i, 0))
    shp = jax.ShapeDtypeStruct((r, c), F32)
    return _call(body, name=name, grid=(r // tr,), in_specs=[spec] * 4, out_specs=[spec] * n_out,
                 out_shape=[shp] * n_out, args=(w, g, m, v), sem=("parallel",))


def _cctx_grad(parts, c_ctx2d):
    def body(p_ref, c_ref, o_ref):
        tot = ((p_ref[0] + p_ref[2]) + p_ref[4]) + p_ref[6]
        o_ref[...] = tot * _dsilu(c_ref[...])

    return pl.pallas_call(body, name="cctx_grad", in_specs=[VMEM_SPEC, VMEM_SPEC], out_specs=VMEM_SPEC,
                          out_shape=jax.ShapeDtypeStruct(c_ctx2d.shape, F32))(parts, c_ctx2d)


def _mod_fwd(c_rows_t, ada_w, ada_b_shard):
    nl, d, w = ada_w.shape
    td = _pick(d, (256, 128))
    nd = d // td

    def body(ct_ref, w_ref, b_ref, o_ref):
        i = pl.program_id(1)

        @pl.when(i == 0)
        def _():
            o_ref[0] = jnp.broadcast_to(b_ref[0], (MOD_ROWS, w))

        st = _silu(ct_ref[...])
        wv = w_ref[0]
        used = N_DEV + 1
        rows = [jnp.sum(st[:, r:r + 1] * wv, axis=0, keepdims=True) for r in range(used)]
        rows.append(jnp.zeros((MOD_ROWS - used, w), F32))
        o_ref[0] += jnp.concatenate(rows, axis=0)

    return pl.pallas_call(
        body, name="mod_fwd", grid=(nl, nd),
        in_specs=[pl.BlockSpec((td, MOD_ROWS), lambda l, i: (i, 0)),
                  pl.BlockSpec((1, td, w), lambda l, i: (l, i, 0)),
                  pl.BlockSpec((1, 1, w), lambda l, i: (l, 0, 0))],
        out_specs=pl.BlockSpec((1, MOD_ROWS, w), lambda l, i: (l, 0, 0)),
        out_shape=jax.ShapeDtypeStruct((nl, MOD_ROWS, w), F32),
        compiler_params=_cp(("parallel", "arbitrary")),
    )(c_rows_t, ada_w, ada_b_shard)


def _mod_bwd(c_rows_t, dmod, ada_w):
    nl, d, w = ada_w.shape
    td = _pick(d, (256, 128))
    ctx_row = N_DEV

    def body(ct_ref, dm_ref, w_ref, gw_ref, ds_ref):
        st = _silu(ct_ref[...])
        dm = dm_ref[0]
        acc = st[:, 0:1] * dm[0:1, :]
        for r in range(1, ctx_row + 1):
            acc = acc + st[:, r:r + 1] * dm[r:r + 1, :]
        gw_ref[0] = acc
        ds_ref[0] = jnp.sum(w_ref[0] * dm[ctx_row:ctx_row + 1, :], axis=1, keepdims=True)

    return pl.pallas_call(
        body, name="mod_bwd", grid=(nl, d // td),
        in_specs=[pl.BlockSpec((td, MOD_ROWS), lambda l, i: (i, 0)),
                  pl.BlockSpec((1, MOD_ROWS, w), lambda l, i: (l, 0, 0)),
                  pl.BlockSpec((1, td, w), lambda l, i: (l, i, 0))],
        out_specs=[pl.BlockSpec((1, td, w), lambda l, i: (l, i, 0)),
                   pl.BlockSpec((1, td, 1), lambda l, i: (l, i, 0))],
        out_shape=[jax.ShapeDtypeStruct((nl, d, w), F32), jax.ShapeDtypeStruct((nl, d, 1), F32)],
        compiler_params=_cp(("parallel", "parallel")),
    )(c_rows_t, dmod, ada_w)


def _adaln_fwd(x, g, shift, scale, tr, name, comm=None, out_rows=None, into=None, row0=0):
    r, d = x.shape
    rb0 = row0 // tr
    total = into.shape[0] if into is not None else (out_rows or r)

    def body(x_ref, g_ref, sh_ref, sc_ref, *rest):
        o_ref = rest[-1]
        xv = x_ref[...]
        rs = lax.rsqrt(jnp.mean(xv * xv, axis=-1, keepdims=True) + EPS)
        o_ref[...] = ((xv * rs * g_ref[...]) * (1.0 + sc_ref[...]) + sh_ref[...]).astype(BF16)

    spec = pl.BlockSpec((tr, d), lambda i: (i, 0))
    in_specs, args, aliases = [spec, _vec(d), _vec(d), _vec(d)], [x, g, shift, scale], None
    if into is not None:
        in_specs.append(ANY)
        args.append(into)
        aliases = {4: 0}
    outs = _call(body, name=name, grid=(r // tr,), in_specs=in_specs,
                 out_specs=[pl.BlockSpec((tr, d), lambda i: (rb0 + i, 0))],
                 out_shape=[jax.ShapeDtypeStruct((total, d), BF16)], args=args, sem=("parallel",),
                 comm=comm, aliases=aliases)
    return outs if comm else outs[0]


def _adaln_bwd(xin, dh, row0, g, scale, dg_init, tr, name, dres=None, o_prev=None, gate_prev=None, comm=None):
    r, d = xin.shape
    assert row0 % tr == 0
    rb0 = row0 // tr
    want_dx = dres is not None
    want_prev = o_prev is not None
    assert want_dx or not want_prev

    def body(*refs):
        it = iter(refs)
        x_ref, dh_ref, g_ref, sc_ref, dgi_ref = next(it), next(it), next(it), next(it), next(it)
        dres_ref = next(it) if want_dx else None
        o_ref, gp_ref = (next(it), next(it)) if want_prev else (None, None)
        dx_ref = next(it) if want_dx else None
        do_ref = next(it) if want_prev else None
        dsh_ref, dsc_ref, dg_ref = next(it), next(it), next(it)
        dgp_ref = next(it) if want_prev else None
        i = pl.program_id(0)

        @pl.when(i == 0)
        def _():
            dsh_ref[...] = jnp.zeros_like(dsh_ref)
            dsc_ref[...] = jnp.zeros_like(dsc_ref)
            dg_ref[...] = dgi_ref[...]
            if want_prev:
                dgp_ref[...] = jnp.zeros_like(dgp_ref)

        xv = x_ref[...]
        dhv = dh_ref[...].astype(F32)
        gv = g_ref[...]
        rs = lax.rsqrt(jnp.mean(xv * xv, axis=-1, keepdims=True) + EPS)
        xn = xv * rs
        dsh_ref[...] += jnp.sum(dhv, axis=0, keepdims=True)
        dsc_ref[...] += jnp.sum(dhv * (xn * gv), axis=0, keepdims=True)
        dr = dhv * (1.0 + sc_ref[...])
        dg_ref[...] += jnp.sum(dr * xn, axis=0, keepdims=True)
        if want_dx:
            gy = dr * gv
            dx = dres_ref[...] + rs * (gy - xn * jnp.mean(gy * xn, axis=-1, keepdims=True))
            dx_ref[...] = dx
            if want_prev:
                do_ref[...] = (gp_ref[...] * dx).astype(BF16)
                dgp_ref[...] += jnp.sum(dx * o_ref[...].astype(F32), axis=0, keepdims=True)

    row = pl.BlockSpec((tr, d), lambda i: (i, 0))
    in_specs = [row, pl.BlockSpec((tr, d), lambda i: (rb0 + i, 0)), _vec(d), _vec(d), _vec(d)]
    args = [xin, dh, g, scale, dg_init]
    out_specs, out_shape, names = [], [], []
    if want_dx:
        in_specs.append(row)
        args.append(dres)
    if want_prev:
        in_specs += [row, _vec(d)]
        args += [o_prev, gate_prev]
    if want_dx:
        out_specs.append(row)
        out_shape.append(jax.ShapeDtypeStruct((r, d), F32))
        names.append("dx")
    if want_prev:
        out_specs.append(row)
        out_shape.append(jax.ShapeDtypeStruct((r, d), BF16))
        names.append("do_prev")
    for nm in ("dshift", "dscale", "dg") + (("dgate_prev",) if want_prev else ()):
        out_specs.append(_vec(d))
        out_shape.append(jax.ShapeDtypeStruct((1, d), F32))
        names.append(nm)
    outs = _call(body, name=name, grid=(r // tr,), in_specs=in_specs, out_specs=out_specs, out_shape=out_shape,
                 args=args, sem=("arbitrary",), comm=comm)
    res = dict(zip(names, outs))
    if comm:
        res["carried"] = outs[len(names):]
    return res


def _mm(a, b, *, name, tm, tn, tk, ta=False, tb=False, out_dtype=F32, res=None, gate=None, comm=None):
    if ta:
        kd, m = a.shape
    else:
        m, kd = a.shape
    if tb:
        n, kd2 = b.shape
    else:
        kd2, n = b.shape
    assert kd == kd2 and m % tm == 0 and n % tn == 0 and kd % tk == 0, (a.shape, b.shape, tm, tn, tk)
    ni, nj, nk = m // tm, n // tn, kd // tk
    dn = (((0 if ta else 1,), (1 if tb else 0,)), ((), ()))
    with_res = res is not None
    n_in = 4 if with_res else 2
    n_out = 2 if with_res else 1
    n_cin = len(comm.ins) if comm else 0
    n_cout = len(comm.out_shapes) if comm else 0

    def body(*refs):
        in_refs = refs[:n_in + n_cin]
        out_refs = refs[n_in + n_cin:n_in + n_cin + n_out + n_cout]
        scratch = refs[n_in + n_cin + n_out + n_cout:]
        a_ref, b_ref = in_refs[0], in_refs[1]
        o_ref = out_refs[0]
        i, j, k = pl.program_id(0), pl.program_id(1), pl.program_id(2)
        if comm:
            carried = comm.split(in_refs, out_refs, scratch)

            @pl.when(jnp.logical_and(jnp.logical_and(i == 0, j == 0), k == 0))
            def _():
                comm.start(*carried)

        def emit(acc):
            o_ref[...] = acc.astype(o_ref.dtype)
            if with_res:
                out_refs[1][...] = in_refs[2][...] + in_refs[3][...] * acc

        if nk == 1:
            emit(lax.dot_general(a_ref[...], b_ref[...], dn, preferred_element_type=F32))
        else:
            acc_ref = scratch[0]

            @pl.when(k == 0)
            def _():
                acc_ref[...] = jnp.zeros_like(acc_ref)

            acc_ref[...] += lax.dot_general(a_ref[...], b_ref[...], dn, preferred_element_type=F32)

            @pl.when(k == nk - 1)
            def _():
                emit(acc_ref[...])

        if comm:
            @pl.when(jnp.logical_and(jnp.logical_and(i == ni - 1, j == nj - 1), k == nk - 1))
            def _():
                comm.finish(*carried)

    a_spec = pl.BlockSpec((tk, tm), lambda i, j, k: (k, i)) if ta else pl.BlockSpec((tm, tk), lambda i, j, k: (i, k))
    b_spec = pl.BlockSpec((tn, tk), lambda i, j, k: (j, k)) if tb else pl.BlockSpec((tk, tn), lambda i, j, k: (k, j))
    o_spec = pl.BlockSpec((tm, tn), lambda i, j, k: (i, j))
    in_specs, args = [a_spec, b_spec], [a, b]
    out_specs, out_shape = [o_spec], [jax.ShapeDtypeStruct((m, n), out_dtype)]
    if with_res:
        in_specs += [o_spec, pl.BlockSpec((1, tn), lambda i, j, k: (0, j))]
        args += [res, gate]
        out_specs.append(o_spec)
        out_shape.append(jax.ShapeDtypeStruct((m, n), F32))
    scratch_shapes = [pltpu.VMEM((tm, tn), F32)] if nk > 1 else []
    aliases = {}
    sem = ("parallel", "parallel", "arbitrary")
    if comm:
        in_specs += [ANY] * n_cin
        args += comm.ins
        out_specs += [ANY] * n_cout
        out_shape += comm.out_shapes
        scratch_shapes += comm.scratch()
        aliases = {n_in + s: n_out + d for s, d in comm.aliases.items()}
        sem = ("arbitrary", "arbitrary", "arbitrary")
    outs = pl.pallas_call(body, name=name, grid=(ni, nj, nk), in_specs=in_specs, out_specs=out_specs,
                          out_shape=out_shape, scratch_shapes=scratch_shapes, input_output_aliases=aliases,
                          compiler_params=_cp(sem))(*args)
    return outs if (with_res or comm) else outs[0]


def _swap_pairs(x):
    lane = lax.broadcasted_iota(jnp.int32, x.shape, 1)
    return jnp.where(lane % 2 == 0, pltpu.roll(x, HEAD_DIM - 1, 1), pltpu.roll(x, 1, 1))


def _qk_prep(p0, cos_i, sin_s, q_norm, k_norm, dims, tr):
    rows = p0.shape[0]
    kvw, aw, cb = dims["kv_w"], dims["attn_w"], dims["cb"]
    nkv, nq = kvw // HEAD_DIM, aw // HEAD_DIM
    scale = HEAD_DIM ** -0.5
    n_kv_specs, n_q_specs = (2 * kvw) // cb, aw // cb

    def body(*refs):
        kv_refs = refs[:n_kv_specs]
        q_refs = refs[n_kv_specs:n_kv_specs + n_q_specs]
        cos_ref, sin_ref, qn_ref, kn_ref, qo_ref, ko_ref, vo_ref = refs[n_kv_specs + n_q_specs:]
        kv = _cat(kv_refs)
        qv = _cat(q_refs)
        cs, sn = cos_ref[...], sin_ref[...]

        def norm_rope(xh, gvec):
            rs = lax.rsqrt(jnp.mean(xh * xh, axis=-1, keepdims=True) + EPS)
            xn = xh * rs * gvec
            return xn * cs + _swap_pairs(xn) * sn

        for h in range(nkv):
            sl = slice(h * HEAD_DIM, (h + 1) * HEAD_DIM)
            ko_ref[:, sl] = norm_rope(kv[:, sl], kn_ref[...]).astype(BF16)
        vo_ref[...] = kv[:, kvw:].astype(BF16)
        for h in range(nq):
            sl = slice(h * HEAD_DIM, (h + 1) * HEAD_DIM)
            qo_ref[:, sl] = (norm_rope(qv[:, sl], qn_ref[...]) * scale).astype(BF16)

    rm = lambda i: i
    in_specs = (_col_specs(tr, 0, 2 * kvw, cb, rm) + _col_specs(tr, 2 * kvw, aw, cb, rm)
                + [pl.BlockSpec((tr, HEAD_DIM), lambda i: (i, 0))] * 2 + [_vec(HEAD_DIM)] * 2)
    args = [p0] * (n_kv_specs + n_q_specs) + [cos_i, sin_s, q_norm, k_norm]
    return pl.pallas_call(
        body, name="qk_prep", grid=(rows // tr,), in_specs=in_specs,
        out_specs=[pl.BlockSpec((tr, aw), lambda i: (i, 0)), pl.BlockSpec((tr, kvw), lambda i: (i, 0)),
                   pl.BlockSpec((tr, kvw), lambda i: (i, 0))],
        out_shape=[jax.ShapeDtypeStruct((rows, aw), BF16), jax.ShapeDtypeStruct((rows, kvw), BF16),
                   jax.ShapeDtypeStruct((rows, kvw), BF16)],
        compiler_params=_cp(("parallel",)))(*args)


def _qk_prep_bwd(dq_hat, dk_hat, dv, p0, cos_i, sin_s, q_norm, k_norm, dims, tr, n_lat):
    rows = p0.shape[0]
    kvw, aw, cb = dims["kv_w"], dims["attn_w"], dims["cb"]
    nkv, nq = kvw // HEAD_DIM, aw // HEAD_DIM
    scale = HEAD_DIM ** -0.5
    n_kv_specs, n_q_specs = (2 * kvw) // cb, aw // cb
    lat_tiles = n_lat // tr

    def body(*refs):
        kv_refs = refs[:n_kv_specs]
        q_refs = refs[n_kv_specs:n_kv_specs + n_q_specs]
        (dq_ref, dk_ref, dv_ref, cos_ref, sin_ref, qn_ref, kn_ref,
         out_ref, dqn_ref, dkn_ref) = refs[n_kv_specs + n_q_specs:]
        i = pl.program_id(0)

        @pl.when(i == 0)
        def _():
            dqn_ref[...] = jnp.zeros_like(dqn_ref)
            dkn_ref[...] = jnp.zeros_like(dkn_ref)

        kv = _cat(kv_refs)
        qv = _cat(q_refs)
        cs, sn = cos_ref[...], sin_ref[...]
        is_lat = (i < lat_tiles).astype(F32)

        def head_bwd(xh, dhat, gvec):
            dn = dhat * cs + _swap_pairs(dhat * sn)
            rs = lax.rsqrt(jnp.mean(xh * xh, axis=-1, keepdims=True) + EPS)
            xn = xh * rs
            gy = dn * gvec
            dx = rs * (gy - xn * jnp.mean(gy * xn, axis=-1, keepdims=True))
            return dx, jnp.sum(dn * xn, axis=0, keepdims=True)

        dkn = jnp.zeros((1, HEAD_DIM), F32)
        for h in range(nkv):
            sl = slice(h * HEAD_DIM, (h + 1) * HEAD_DIM)
            dx, dgv = head_bwd(kv[:, sl], dk_ref[:, sl], kn_ref[...])
            out_ref[:, sl] = dx.astype(BF16)
            dkn = dkn + dgv
        dkn_ref[...] += dkn
        out_ref[:, kvw:2 * kvw] = dv_ref[...].astype(BF16)
        dqn = jnp.zeros((1, HEAD_DIM), F32)
        for h in range(nq):
            sl = slice(h * HEAD_DIM, (h + 1) * HEAD_DIM)
            dx, dgv = head_bwd(qv[:, sl], dq_ref[:, sl] * (scale * is_lat), qn_ref[...])
            out_ref[:, 2 * kvw + h * HEAD_DIM:2 * kvw + (h + 1) * HEAD_DIM] = dx.astype(BF16)
            dqn = dqn + dgv
        dqn_ref[...] += dqn

    rm = lambda i: i
    wout = 2 * kvw + aw
    in_specs = (_col_specs(tr, 0, 2 * kvw, cb, rm) + _col_specs(tr, 2 * kvw, aw, cb, rm)
                + [pl.BlockSpec((tr, aw), lambda i: (jnp.minimum(i, lat_tiles - 1), 0)),
                   pl.BlockSpec((tr, kvw), lambda i: (i, 0)), pl.BlockSpec((tr, kvw), lambda i: (i, 0)),
                   pl.BlockSpec((tr, HEAD_DIM), lambda i: (i, 0)), pl.BlockSpec((tr, HEAD_DIM), lambda i: (i, 0)),
                   _vec(HEAD_DIM), _vec(HEAD_DIM)])
    args = [p0] * (n_kv_specs + n_q_specs) + [dq_hat, dk_hat, dv, cos_i, sin_s, q_norm, k_norm]
    return pl.pallas_call(
        body, name="qk_prep_bwd", grid=(rows // tr,), in_specs=in_specs,
        out_specs=[pl.BlockSpec((tr, wout), lambda i: (i, 0)), _vec(HEAD_DIM), _vec(HEAD_DIM)],
        out_shape=[jax.ShapeDtypeStruct((rows, wout), BF16), jax.ShapeDtypeStruct((1, HEAD_DIM), F32),
                   jax.ShapeDtypeStruct((1, HEAD_DIM), F32)],
        compiler_params=_cp(("arbitrary",)))(*args)


def _stack_heads(x, g):
    return jnp.concatenate([x[:, h * HEAD_DIM:(h + 1) * HEAD_DIM] for h in range(g)], axis=0)


def _flash_fwd(q_hat, k_all, v_all, n_lat, dims, tq, tk):
    kvw, aw = dims["kv_w"], dims["attn_w"]
    nkv = kvw // HEAD_DIM
    g = aw // kvw
    gw = g * HEAD_DIM
    n_keys = k_all.shape[0]
    ni, nj = n_lat // tq, n_keys // tk
    dn_nt = (((1,), (1,)), ((), ()))

    def body(q_ref, k_ref, v_ref, o_ref, lse_ref):
        qs = _stack_heads(q_ref[...], g)
        m = jnp.full((g * tq, 1), -1e30, F32)
        l = jnp.zeros((g * tq, 1), F32)
        acc = jnp.zeros((g * tq, HEAD_DIM), F32)
        for j in range(nj):
            kb = k_ref[pl.ds(j * tk, tk), :]
            vb = v_ref[pl.ds(j * tk, tk), :]
            s = lax.dot_general(qs, kb, dn_nt, preferred_element_type=F32)
            m_new = jnp.maximum(m, jnp.max(s, axis=-1, keepdims=True))
            alpha = jnp.exp(m - m_new)
            p = jnp.exp(s - m_new)
            l = alpha * l + jnp.sum(p, axis=-1, keepdims=True)
            acc = alpha * acc + jnp.dot(p.astype(BF16), vb, preferred_element_type=F32)
            m = m_new
        o = acc / l
        for h in range(g):
            o_ref[:, h * HEAD_DIM:(h + 1) * HEAD_DIM] = o[h * tq:(h + 1) * tq]
        lse_ref[...] = m + jnp.log(l)

    return pl.pallas_call(
        body, name="flash_fwd", grid=(nkv, ni),
        in_specs=[pl.BlockSpec((tq, gw), lambda h, i: (i, h)),
                  pl.BlockSpec((n_keys, HEAD_DIM), lambda h, i: (0, h)),
                  pl.BlockSpec((n_keys, HEAD_DIM), lambda h, i: (0, h))],
        out_specs=[pl.BlockSpec((tq, gw), lambda h, i: (i, h)),
                   pl.BlockSpec((g * tq, 1), lambda h, i: (h * ni + i, 0))],
        out_shape=[jax.ShapeDtypeStruct((n_lat, aw), F32), jax.ShapeDtypeStruct((nkv * ni * g * tq, 1), F32)],
        compiler_params=_cp(("parallel", "parallel")))(q_hat, k_all, v_all)


def _flash_bwd(q_hat, k_all, v_all, do, o, lse, n_lat, dims, tq, tk):
    kvw, aw = dims["kv_w"], dims["attn_w"]
    nkv = kvw // HEAD_DIM
    g = aw // kvw
    gw = g * HEAD_DIM
    n_keys = k_all.shape[0]
    ni, nj = n_lat // tq, n_keys // tk
    dn_nt = (((1,), (1,)), ((), ()))
    dn_tn = (((0,), (0,)), ((), ()))

    def body(q_ref, k_ref, v_ref, do_ref, o_ref, lse_ref, dq_ref, dk_ref, dv_ref):
        i = pl.program_id(1)

        @pl.when(i == 0)
        def _():
            dk_ref[...] = jnp.zeros_like(dk_ref)
            dv_ref[...] = jnp.zeros_like(dv_ref)

        dos = _stack_heads(do_ref[...], g)
        qs = _stack_heads(q_ref[...], g)
        delta = jnp.sum(dos.astype(F32) * _stack_heads(o_ref[...], g), axis=-1, keepdims=True)
        lse_v = lse_ref[...]
        dq = jnp.zeros((g * tq, HEAD_DIM), F32)
        for j in range(nj):
            rows = pl.ds(j * tk, tk)
            kb, vb = k_ref[rows, :], v_ref[rows, :]
            s = lax.dot_general(qs, kb, dn_nt, preferred_element_type=F32)
            p = jnp.exp(s - lse_v)
            dp = lax.dot_general(dos, vb, dn_nt, preferred_element_type=F32)
            ds = (p * (dp - delta)).astype(BF16)
            dv_ref[rows, :] += lax.dot_general(p.astype(BF16), dos, dn_tn, preferred_element_type=F32)
            dk_ref[rows, :] += lax.dot_general(ds, qs, dn_tn, preferred_element_type=F32)
            dq = dq + jnp.dot(ds, kb, preferred_element_type=F32)
        for h in range(g):
            dq_ref[:, h * HEAD_DIM:(h + 1) * HEAD_DIM] = dq[h * tq:(h + 1) * tq]

    qspec = pl.BlockSpec((tq, gw), lambda h, i: (i, h))
    full_k = pl.BlockSpec((n_keys, HEAD_DIM), lambda h, i: (0, h))
    return pl.pallas_call(
        body, name="flash_bwd", grid=(nkv, ni),
        in_specs=[qspec, full_k, full_k, qspec, qspec, pl.BlockSpec((g * tq, 1), lambda h, i: (h * ni + i, 0))],
        out_specs=[qspec, full_k, full_k],
        out_shape=[jax.ShapeDtypeStruct((n_lat, aw), F32), jax.ShapeDtypeStruct((n_keys, kvw), F32),
                   jax.ShapeDtypeStruct((n_keys, kvw), F32)],
        compiler_params=_cp(("parallel", "arbitrary")))(q_hat, k_all, v_all, do, o, lse)


def _shifted_copies(pad_ref, sh_ref, tr):
    rows = tr + 2 * CONV_HALO - SUBLANES
    for r in range(SUBLANES):
        sh_ref[r] = pad_ref[pl.ds(r, rows), :]


def _stencil(sh_ref, w_ref, out_ref, offsets, tr, cc, init_ref=None):
    for c0 in range(0, cc, LANES):
        lanes = pl.ds(c0, LANES)
        wv = [jnp.broadcast_to(w_ref[pl.ds(k, 1), lanes], (STENCIL_ROWS, LANES)) for k in range(len(offsets))]
        if init_ref is None:
            init = jnp.zeros((STENCIL_ROWS, LANES), F32)
        else:
            init = jnp.broadcast_to(init_ref[:, lanes], (STENCIL_ROWS, LANES))

        def block(rb, carry, lanes=lanes, wv=wv, init=init):
            r0 = rb * STENCIL_ROWS
            parts = [init] + [None] * (STENCIL_CHAINS - 1)
            for k, o in enumerate(offsets):
                rows = pl.ds(SUBLANES * (o // SUBLANES) + r0, STENCIL_ROWS)
                term = sh_ref[o % SUBLANES, rows, lanes] * wv[k]
                q = k % STENCIL_CHAINS
                parts[q] = term if parts[q] is None else parts[q] + term
            acc = parts[0]
            for p_ in parts[1:]:
                acc = acc + p_
            out_ref[pl.ds(r0, STENCIL_ROWS), lanes] = acc
            return carry

        for rb in range(tr // STENCIL_ROWS):
            block(rb, 0)


def _stencil_weight_grad(sh_ref, d_ref, d_row0, dw_ref, offsets, tr, cc):
    rows_per = REDUCE_ROWS
    for c0 in range(0, cc, LANES):
        lanes = pl.ds(c0, LANES)

        def block(rb, accs, lanes=lanes):
            r0 = rb * rows_per
            dblk = d_ref[pl.ds(d_row0 + r0, rows_per), lanes]
            out = []
            for k, o in enumerate(offsets):
                prod = dblk * sh_ref[o % SUBLANES, pl.ds(SUBLANES * (o // SUBLANES) + r0, rows_per), lanes]
                part = prod[0:SUBLANES]
                for q in range(1, rows_per // SUBLANES):
                    part = part + prod[q * SUBLANES:(q + 1) * SUBLANES]
                out.append(accs[k] + part)
            return tuple(out)

        zero = jnp.zeros((SUBLANES, LANES), F32)
        accs = tuple(zero for _ in offsets)
        for rb in range(tr // rows_per):
            accs = block(rb, accs)
        for k in range(len(offsets)):
            dw_ref[pl.ds(k, 1), lanes] += jnp.sum(accs[k], axis=0, keepdims=True)


def _halo_maps(tr, n_tiles):
    per = tr // CONV_HALO
    prev = lambda i: jnp.maximum(i * per - 1, 0)
    nxt = lambda i: (i + 1) * per
    return prev, nxt


def _mix_fwd(attn, p0, dw_w, dw_b, ln_g, ln_b, dims, tr):
    n_lat, aw = attn.shape
    cc, cb, cw = dims["conv_ch"], dims["cb"], dims["conv_w"]
    off = dims["off"]
    n_tiles = n_lat // tr
    pad = cw // 2
    na, nc = aw // cb, cc // cb
    prev_map, next_map = _halo_maps(tr, n_tiles)

    def body(*refs):
        it = iter(refs)
        attn_ref = next(it)
        za = [next(it) for _ in range(na)]
        a_c = [next(it) for _ in range(nc)]
        b_c = [next(it) for _ in range(nc)]
        zb = [next(it) for _ in range(nc)]
        a_p = [next(it) for _ in range(nc)]
        b_p = [next(it) for _ in range(nc)]
        a_n = [next(it) for _ in range(nc)]
        b_n = [next(it) for _ in range(nc)]
        w_ref, db_ref, g_ref, bb_ref, mix_ref, yc_ref, ypad, ysh = (next(it) for _ in range(8))
        i = pl.program_id(0)
        ypad[pl.ds(0, CONV_HALO), :] = _cat(a_p) * _sigmoid(_cat(b_p)) * (i > 0).astype(F32)
        ypad[pl.ds(CONV_HALO, tr), :] = _cat(a_c) * _sigmoid(_cat(b_c))
        ypad[pl.ds(CONV_HALO + tr, CONV_HALO), :] = _cat(a_n) * _sigmoid(_cat(b_n)) * (i < n_tiles - 1).astype(F32)
        _shifted_copies(ypad, ysh, tr)
        _stencil(ysh, w_ref, yc_ref, [CONV_HALO - pad + k for k in range(cw)], tr, cc, init_ref=db_ref)
        acc = yc_ref[...]
        mu = jnp.mean(acc, axis=-1, keepdims=True)
        xc = acc - mu
        rs = lax.rsqrt(jnp.mean(xc * xc, axis=-1, keepdims=True) + EPS)
        nv = xc * rs * g_ref[...] + bb_ref[...]
        mix_ref[:, :aw] = (attn_ref[...] * _silu(_cat(za))).astype(BF16)
        mix_ref[:, aw:] = (_silu(nv) * _silu(_cat(zb))).astype(BF16)

    rm = lambda i: i
    in_specs = ([pl.BlockSpec((tr, aw), lambda i: (i, 0))]
                + _col_specs(tr, off["za"], aw, cb, rm) + _col_specs(tr, off["a"], cc, cb, rm)
                + _col_specs(tr, off["b"], cc, cb, rm) + _col_specs(tr, off["zb"], cc, cb, rm)
                + _col_specs(CONV_HALO, off["a"], cc, cb, prev_map) + _col_specs(CONV_HALO, off["b"], cc, cb, prev_map)
                + _col_specs(CONV_HALO, off["a"], cc, cb, next_map) + _col_specs(CONV_HALO, off["b"], cc, cb, next_map)
                + [pl.BlockSpec(dw_w.shape, lambda i: (0, 0)), _vec(cc), _vec(cc), _vec(cc)])
    args = [attn] + [p0] * (na + 7 * nc) + [dw_w, dw_b, ln_g, ln_b]
    return pl.pallas_call(
        body, name="mix_fwd", grid=(n_tiles,), in_specs=in_specs,
        out_specs=[pl.BlockSpec((tr, aw + cc), lambda i: (i, 0)), pl.BlockSpec((tr, cc), lambda i: (i, 0))],
        out_shape=[jax.ShapeDtypeStruct((n_lat, aw + cc), BF16), jax.ShapeDtypeStruct((n_lat, cc), F32)],
        scratch_shapes=[pltpu.VMEM((tr + 2 * CONV_HALO, cc), F32),
                        pltpu.VMEM((SUBLANES, tr + 2 * CONV_HALO - SUBLANES, cc), F32)],
        compiler_params=_cp(("parallel",)))(*args)


def _mix_bwd_pointwise(dmix, attn, p0, yc, ln_g, ln_b, dims, tr, n_ext):
    n_lat, aw = attn.shape
    cc, cb, off = dims["conv_ch"], dims["cb"], dims["off"]
    na, nc = aw // cb, cc // cb
    lat_tiles = n_lat // tr

    def body(*refs):
        it = iter(refs)
        dmix_ref, attn_ref = next(it), next(it)
        za = [next(it) for _ in range(na)]
        zb = [next(it) for _ in range(nc)]
        yc_ref, g_ref, bb_ref = next(it), next(it), next(it)
        dattn_ref, dza_ref, dzb_ref, dyc_ref, dg_ref, dbb_ref, ddb_ref = (next(it) for _ in range(7))
        i = pl.program_id(0)

        @pl.when(i == 0)
        def _():
            dg_ref[...] = jnp.zeros_like(dg_ref)
            dbb_ref[...] = jnp.zeros_like(dbb_ref)
            ddb_ref[...] = jnp.zeros_like(ddb_ref)

        lat = (i < lat_tiles).astype(F32)
        dm = dmix_ref[...].astype(F32)
        dma, dmb = dm[:, :aw], dm[:, aw:]
        zav, zbv = _cat(za), _cat(zb)
        sa, dsa = _silu_and_grad(zav)
        sb, dsb = _silu_and_grad(zbv)
        dattn_ref[...] = (dma * sa).astype(BF16)
        dza_ref[...] = (dma * attn_ref[...] * dsa * lat).astype(BF16)
        ycv = yc_ref[...]
        mu = jnp.mean(ycv, axis=-1, keepdims=True)
        xc = ycv - mu
        rs = lax.rsqrt(jnp.mean(xc * xc, axis=-1, keepdims=True) + EPS)
        xh = xc * rs
        nv = xh * g_ref[...] + bb_ref[...]
        sn, dsn = _silu_and_grad(nv)
        dzb_ref[...] = (dmb * sn * dsb * lat).astype(BF16)
        dn = dmb * sb * dsn
        dg_ref[...] += lat * jnp.sum(dn * xh, axis=0, keepdims=True)
        dbb_ref[...] += lat * jnp.sum(dn, axis=0, keepdims=True)
        dxh = dn * g_ref[...]
        dyc = rs * (dxh - jnp.mean(dxh, axis=-1, keepdims=True) - xh * jnp.mean(dxh * xh, axis=-1, keepdims=True))
        dyc_ref[...] = dyc
        ddb_ref[...] += lat * jnp.sum(dyc, axis=0, keepdims=True)

    rm = lambda i: jnp.minimum(i, lat_tiles - 1)
    row = lambda w: pl.BlockSpec((tr, w), lambda i: (rm(i), 0))
    ext = lambda w: pl.BlockSpec((tr, w), lambda i: (i, 0))
    in_specs = ([row(aw + cc), row(aw)] + _col_specs(tr, off["za"], aw, cb, rm)
                + _col_specs(tr, off["zb"], cc, cb, rm) + [row(cc), _vec(cc), _vec(cc)])
    args = [dmix, attn] + [p0] * (na + nc) + [yc, ln_g, ln_b]
    return pl.pallas_call(
        body, name="mix_bwd_pointwise", grid=(n_ext // tr,), in_specs=in_specs,
        out_specs=[row(aw), ext(aw), ext(cc), row(cc), _vec(cc), _vec(cc), _vec(cc)],
        out_shape=[jax.ShapeDtypeStruct((n_lat, aw), BF16), jax.ShapeDtypeStruct((n_ext, aw), BF16),
                   jax.ShapeDtypeStruct((n_ext, cc), BF16), jax.ShapeDtypeStruct((n_lat, cc), F32)]
                  + [jax.ShapeDtypeStruct((1, cc), F32)] * 3,
        compiler_params=_cp(("arbitrary",)))(*args)


def _conv_bwd(dyc, p0, dw_w, dims, tr, n_ext):
    n_lat, cc = dyc.shape
    cb, cw, off = dims["cb"], dims["conv_w"], dims["off"]
    nc = cc // cb
    n_tiles = n_lat // tr
    pad = cw // 2
    prev_lat, next_lat = _halo_maps(tr, n_tiles)
    rm = lambda i: jnp.minimum(i, n_tiles - 1)
    prev_map = lambda i: prev_lat(rm(i))
    next_map = lambda i: next_lat(rm(i))

    def body(*refs):
        it = iter(refs)
        a_c = [next(it) for _ in range(nc)]
        b_c = [next(it) for _ in range(nc)]
        a_p = [next(it) for _ in range(nc)]
        b_p = [next(it) for _ in range(nc)]
        a_n = [next(it) for _ in range(nc)]
        b_n = [next(it) for _ in range(nc)]
        d_c, d_p, d_n, w_ref, dab_ref, dw_ref, ypad, dpad, ysh, dsh, dy_scr = (next(it) for _ in range(11))
        i = pl.program_id(0)

        @pl.when(i == 0)
        def _():
            dw_ref[...] = jnp.zeros_like(dw_ref)

        @pl.when(i >= n_tiles)
        def _():
            dab_ref[...] = jnp.zeros_like(dab_ref)

        @pl.when(i < n_tiles)
        def _():
            first, last = (i > 0).astype(F32), (i < n_tiles - 1).astype(F32)
            av, bv = _cat(a_c), _cat(b_c)
            sg = _sigmoid(bv)
            ypad[pl.ds(0, CONV_HALO), :] = _cat(a_p) * _sigmoid(_cat(b_p)) * first
            ypad[pl.ds(CONV_HALO, tr), :] = av * sg
            ypad[pl.ds(CONV_HALO + tr, CONV_HALO), :] = _cat(a_n) * _sigmoid(_cat(b_n)) * last
            dpad[pl.ds(0, CONV_HALO), :] = d_p[...] * first
            dpad[pl.ds(CONV_HALO, tr), :] = d_c[...]
            dpad[pl.ds(CONV_HALO + tr, CONV_HALO), :] = d_n[...] * last
            _shifted_copies(ypad, ysh, tr)
            _shifted_copies(dpad, dsh, tr)
            _stencil(dsh, w_ref, dy_scr, [CONV_HALO + pad - k for k in range(cw)], tr, cc)
            _stencil_weight_grad(ysh, dpad, CONV_HALO, dw_ref, [CONV_HALO - pad + k for k in range(cw)], tr, cc)
            dy = dy_scr[...]
            dab_ref[:, :cc] = (dy * sg).astype(BF16)
            dab_ref[:, cc:] = (dy * av * sg * (1.0 - sg)).astype(BF16)

    in_specs = (_col_specs(tr, off["a"], cc, cb, rm) + _col_specs(tr, off["b"], cc, cb, rm)
                + _col_specs(CONV_HALO, off["a"], cc, cb, prev_map) + _col_specs(CONV_HALO, off["b"], cc, cb, prev_map)
                + _col_specs(CONV_HALO, off["a"], cc, cb, next_map) + _col_specs(CONV_HALO, off["b"], cc, cb, next_map)
                + [pl.BlockSpec((tr, cc), lambda i: (rm(i), 0)),
                   pl.BlockSpec((CONV_HALO, cc), lambda i: (prev_map(i), 0)),
                   pl.BlockSpec((CONV_HALO, cc), lambda i: (jnp.minimum(next_map(i), n_lat // CONV_HALO - 1), 0)),
                   pl.BlockSpec(dw_w.shape, lambda i: (0, 0))])
    args = [p0] * (6 * nc) + [dyc, dyc, dyc, dw_w]
    return pl.pallas_call(
        body, name="conv_bwd", grid=(n_ext // tr,), in_specs=in_specs,
        out_specs=[pl.BlockSpec((tr, 2 * cc), lambda i: (i, 0)), pl.BlockSpec(dw_w.shape, lambda i: (0, 0))],
        out_shape=[jax.ShapeDtypeStruct((n_ext, 2 * cc), BF16), jax.ShapeDtypeStruct(dw_w.shape, F32)],
        scratch_shapes=[pltpu.VMEM((tr + 2 * CONV_HALO, cc), F32), pltpu.VMEM((tr + 2 * CONV_HALO, cc), F32),
                        pltpu.VMEM((SUBLANES, tr + 2 * CONV_HALO - SUBLANES, cc), F32),
                        pltpu.VMEM((SUBLANES, tr + 2 * CONV_HALO - SUBLANES, cc), F32),
                        pltpu.VMEM((tr, cc), F32)],
        compiler_params=_cp(("arbitrary",)))(*args)


def _sgu_parts(u, v, ln_g, ln_b):
    mu = jnp.mean(v, axis=-1, keepdims=True)
    xc = v - mu
    rs = lax.rsqrt(jnp.mean(xc * xc, axis=-1, keepdims=True) + EPS)
    xh = xc * rs
    return u, xh, rs, xh * ln_g + ln_b


def _sgu_fwd(p1, ln_g, ln_b, ws, bs_t, tr):
    n_lat, w3 = p1.shape
    w = w3 // 3
    ng, ch = ws.shape[0], ws.shape[1]
    gwid = w // ng
    n_ch = tr // ch

    def body(pu_ref, pv_ref, pg_ref, g_ref, b_ref, ws_ref, bs_ref, o_ref):
        u, _, _, vln = _sgu_parts(_gelu(pu_ref[...].astype(F32)), _gelu(pv_ref[...].astype(F32)),
                                  g_ref[...], b_ref[...])
        gate = _silu(pg_ref[...].astype(F32))
        vb = vln.astype(BF16)
        for c in range(n_ch):
            rs_ = slice(c * ch, (c + 1) * ch)
            for gi in range(ng):
                cs_ = slice(gi * gwid, (gi + 1) * gwid)
                mixed = jnp.dot(ws_ref[gi], vb[rs_, cs_], preferred_element_type=F32) + bs_ref[:, gi:gi + 1]
                o_ref[rs_, cs_] = (u[rs_, cs_] * mixed * gate[rs_, cs_]).astype(BF16)

    col = lambda t: pl.BlockSpec((tr, w), lambda i, _t=t: (i, _t))
    return pl.pallas_call(
        body, name="sgu_fwd", grid=(n_lat // tr,),
        in_specs=[col(0), col(1), col(2), _vec(w), _vec(w),
                  pl.BlockSpec(ws.shape, lambda i: (0, 0, 0)), pl.BlockSpec(bs_t.shape, lambda i: (0, 0))],
        out_specs=pl.BlockSpec((tr, w), lambda i: (i, 0)),
        out_shape=jax.ShapeDtypeStruct((n_lat, w), BF16),
        compiler_params=_cp(("parallel",)))(p1, p1, p1, ln_g, ln_b, ws, bs_t)


def _sgu_bwd(dm, p1, ln_g, ln_b, ws, ws_t, bs_t, tr):
    n_lat, w3 = p1.shape
    w = w3 // 3
    ng, ch = ws.shape[0], ws.shape[1]
    gwid = w // ng
    n_ch = tr // ch
    dn_nt = (((1,), (1,)), ((), ()))

    def body(dm_ref, pu_ref, pv_ref, pg_ref, g_ref, b_ref, ws_ref, wst_ref, bs_ref,
             dp_ref, dws_ref, dbs_ref, dg_ref, dbb_ref, dvln_scr):
        i = pl.program_id(0)

        @pl.when(i == 0)
        def _():
            dws_ref[...] = jnp.zeros_like(dws_ref)
            dbs_ref[...] = jnp.zeros_like(dbs_ref)
            dg_ref[...] = jnp.zeros_like(dg_ref)
            dbb_ref[...] = jnp.zeros_like(dbb_ref)

        puv, pvv, pgv = pu_ref[...].astype(F32), pv_ref[...].astype(F32), pg_ref[...].astype(F32)
        gu, dgu = _gelu_and_grad(puv)
        gv, dgv = _gelu_and_grad(pvv)
        gate, dgate = _silu_and_grad(pgv)
        u, xh, rs, vln = _sgu_parts(gu, gv, g_ref[...], b_ref[...])
        dmv = dm_ref[...].astype(F32)
        vb = vln.astype(BF16)
        dmu = dmv * u
        du_pre = dmv * gate * dgu
        dg_pre = dmu * dgate
        dmix_all = dmu * gate
        dbs_cols = [jnp.zeros((ch, 1), F32) for _ in range(ng)]
        for c in range(n_ch):
            rs_ = slice(c * ch, (c + 1) * ch)
            for gi in range(ng):
                cs_ = slice(gi * gwid, (gi + 1) * gwid)
                mixed = jnp.dot(ws_ref[gi], vb[rs_, cs_], preferred_element_type=F32) + bs_ref[:, gi:gi + 1]
                dmixed = dmix_all[rs_, cs_]
                dmb = dmixed.astype(BF16)
                dp_ref[rs_, gi * gwid:(gi + 1) * gwid] = (du_pre[rs_, cs_] * mixed).astype(BF16)
                dp_ref[rs_, 2 * w + gi * gwid:2 * w + (gi + 1) * gwid] = (dg_pre[rs_, cs_] * mixed).astype(BF16)
                dvln_scr[rs_, cs_] = jnp.dot(wst_ref[gi], dmb, preferred_element_type=F32)
                dws_ref[gi] += lax.dot_general(dmb, vb[rs_, cs_], dn_nt, preferred_element_type=F32)
                dbs_cols[gi] = dbs_cols[gi] + jnp.sum(dmixed, axis=-1, keepdims=True)
        dbs_ref[...] += jnp.concatenate(dbs_cols, axis=1)
        dvln = dvln_scr[...]
        dg_ref[...] += jnp.sum(dvln * xh, axis=0, keepdims=True)
        dbb_ref[...] += jnp.sum(dvln, axis=0, keepdims=True)
        dxh = dvln * g_ref[...]
        dv = rs * (dxh - jnp.mean(dxh, axis=-1, keepdims=True) - xh * jnp.mean(dxh * xh, axis=-1, keepdims=True))
        dp_ref[:, w:2 * w] = (dv * dgv).astype(BF16)

    col = lambda t: pl.BlockSpec((tr, w), lambda i, _t=t: (i, _t))
    return pl.pallas_call(
        body, name="sgu_bwd", grid=(n_lat // tr,),
        in_specs=[pl.BlockSpec((tr, w), lambda i: (i, 0)), col(0), col(1), col(2), _vec(w), _vec(w),
                  pl.BlockSpec(ws.shape, lambda i: (0, 0, 0)), pl.BlockSpec(ws.shape, lambda i: (0, 0, 0)),
                  pl.BlockSpec(bs_t.shape, lambda i: (0, 0))],
        out_specs=[pl.BlockSpec((tr, w3), lambda i: (i, 0)), pl.BlockSpec(ws.shape, lambda i: (0, 0, 0)),
                   pl.BlockSpec(bs_t.shape, lambda i: (0, 0)), _vec(w), _vec(w)],
        out_shape=[jax.ShapeDtypeStruct((n_lat, w3), BF16), jax.ShapeDtypeStruct(ws.shape, F32),
                   jax.ShapeDtypeStruct(bs_t.shape, F32), jax.ShapeDtypeStruct((1, w), F32),
                   jax.ShapeDtypeStruct((1, w), F32)],
        scratch_shapes=[pltpu.VMEM((tr, w), F32)],
        compiler_params=_cp(("arbitrary",)))(dm, p1, p1, p1, ln_g, ln_b, ws, ws_t, bs_t)


def _final_loss(x2, target, final_g, o_prev, gate_prev, tr):
    n_lat, d = x2.shape

    def body(x_ref, t_ref, g_ref, o_ref, gp_ref, dx_ref, do_ref, ls_ref, dg_ref, dgp_ref):
        i = pl.program_id(0)

        @pl.when(i == 0)
        def _():
            ls_ref[...] = jnp.zeros_like(ls_ref)
            dg_ref[...] = jnp.zeros_like(dg_ref)
            dgp_ref[...] = jnp.zeros_like(dgp_ref)

        xv = x_ref[...]
        gv = g_ref[...]
        rs = lax.rsqrt(jnp.mean(xv * xv, axis=-1, keepdims=True) + EPS)
        xn = xv * rs
        err = xn * gv - t_ref[...]
        ls_ref[...] += jnp.sum(err * err, axis=0, keepdims=True)
        dy = err * (1.0 / d)
        dg_ref[...] += jnp.sum(dy * xn, axis=0, keepdims=True)
        gy = dy * gv
        dx = rs * (gy - xn * jnp.mean(gy * xn, axis=-1, keepdims=True))
        dx_ref[...] = dx
        do_ref[...] = (gp_ref[...] * dx).astype(BF16)
        dgp_ref[...] += jnp.sum(dx * o_ref[...].astype(F32), axis=0, keepdims=True)

    row = pl.BlockSpec((tr, d), lambda i: (i, 0))
    return pl.pallas_call(
        body, name="final_loss", grid=(n_lat // tr,), in_specs=[row, row, _vec(d), row, _vec(d)],
        out_specs=[row, row, _vec(d), _vec(d), _vec(d)],
        out_shape=[jax.ShapeDtypeStruct((n_lat, d), F32), jax.ShapeDtypeStruct((n_lat, d), BF16)]
                  + [jax.ShapeDtypeStruct((1, d), F32)] * 3,
        compiler_params=_cp(("arbitrary",)))(x2, target, final_g, o_prev, gate_prev)


def _pack(arrays):
    flat = jnp.concatenate([a.reshape(-1).astype(F32) for a in arrays])
    n = flat.shape[0]
    rows = -(-n // LANES)
    rows = -(-rows // PACK_ROWS) * PACK_ROWS
    return jnp.pad(flat, (0, rows * LANES - n)).reshape(rows, LANES)


def _unpack(buf, shapes):
    flat = buf.reshape(buf.shape[:-2] + (-1,))
    out, pos = [], 0
    for shp in shapes:
        n = math.prod(shp)
        out.append(flat[..., pos:pos + n].reshape(buf.shape[:-2] + tuple(shp)))
        pos += n
    return out


def _rope_tables(n_lat, n_ctx):
    rows = n_lat // GRID_W
    row = jnp.repeat(jnp.arange(rows, dtype=F32), GRID_W)
    col = jnp.tile(jnp.arange(GRID_W, dtype=F32), rows)
    n_freq, axis_dim = HEAD_DIM // 4, HEAD_DIM // 2
    inv = jnp.power(ROPE_THETA, jnp.arange(n_freq, dtype=F32) * (-2.0 / axis_dim))
    ang = jnp.concatenate([row[:, None] * inv, col[:, None] * inv], axis=-1)
    cos, sin = jnp.cos(ang), jnp.sin(ang)
    cos_i = jnp.repeat(cos, 2, axis=-1)
    sin_s = jnp.stack([-sin, sin], axis=-1).reshape(n_lat, HEAD_DIM)
    cos_i = jnp.concatenate([cos_i, jnp.ones((n_ctx, HEAD_DIM), F32)], axis=0)
    sin_s = jnp.concatenate([sin_s, jnp.zeros((n_ctx, HEAD_DIM), F32)], axis=0)
    return cos_i, sin_s


def kernel(x, c, ctx, c_ctx, ada_w, ada_b, norm_g, ev_w_in, ev_q_norm, ev_k_norm, ev_dw_w, ev_dw_b, ev_ln_g, ev_ln_b, ev_w_out, od_w_in, od_ln_g, od_ln_b, od_ws, od_bs, od_w_out, final_g, loss_target, m_c_ctx, m_ada_w, m_ada_b, m_norm_g, m_ev_w_in, m_ev_q_norm, m_ev_k_norm, m_ev_dw_w, m_ev_dw_b, m_ev_ln_g, m_ev_ln_b, m_ev_w_out, m_od_w_in, m_od_ln_g, m_od_ln_b, m_od_ws, m_od_bs, m_od_w_out, m_final_g, v_c_ctx, v_ada_w, v_ada_b, v_norm_g, v_ev_w_in, v_ev_q_norm, v_ev_k_norm, v_ev_dw_w, v_ev_dw_b, v_ev_ln_g, v_ev_ln_b, v_ev_w_out, v_od_w_in, v_od_ln_g, v_od_ln_b, v_od_ws, v_od_bs, v_od_w_out, v_final_g):
    weights = dict(c_ctx=c_ctx, ada_w=ada_w, ada_b=ada_b, norm_g=norm_g, ev_w_in=ev_w_in, ev_q_norm=ev_q_norm,
                   ev_k_norm=ev_k_norm, ev_dw_w=ev_dw_w, ev_dw_b=ev_dw_b, ev_ln_g=ev_ln_g, ev_ln_b=ev_ln_b,
                   ev_w_out=ev_w_out, od_w_in=od_w_in, od_ln_g=od_ln_g, od_ln_b=od_ln_b, od_ws=od_ws, od_bs=od_bs,
                   od_w_out=od_w_out, final_g=final_g)
    mom_m = dict(c_ctx=m_c_ctx, ada_w=m_ada_w, ada_b=m_ada_b, norm_g=m_norm_g, ev_w_in=m_ev_w_in,
                 ev_q_norm=m_ev_q_norm, ev_k_norm=m_ev_k_norm, ev_dw_w=m_ev_dw_w, ev_dw_b=m_ev_dw_b,
                 ev_ln_g=m_ev_ln_g, ev_ln_b=m_ev_ln_b, ev_w_out=m_ev_w_out, od_w_in=m_od_w_in, od_ln_g=m_od_ln_g,
                 od_ln_b=m_od_ln_b, od_ws=m_od_ws, od_bs=m_od_bs, od_w_out=m_od_w_out, final_g=m_final_g)
    mom_v = dict(c_ctx=v_c_ctx, ada_w=v_ada_w, ada_b=v_ada_b, norm_g=v_norm_g, ev_w_in=v_ev_w_in,
                 ev_q_norm=v_ev_q_norm, ev_k_norm=v_ev_k_norm, ev_dw_w=v_ev_dw_w, ev_dw_b=v_ev_dw_b,
                 ev_ln_g=v_ev_ln_g, ev_ln_b=v_ev_ln_b, ev_w_out=v_ev_w_out, od_w_in=v_od_w_in, od_ln_g=v_od_ln_g,
                 od_ln_b=v_od_ln_b, od_ws=v_od_ws, od_bs=v_od_bs, od_w_out=v_od_w_out, final_g=v_final_g)
    order = list(weights)

    _, n_lat, d = x.shape
    n_ctx = ctx.shape[1]
    n_ext = n_lat + n_ctx
    ev_in = ev_w_in.shape[-1] * N_CHIP
    ev_mix = ev_w_out.shape[1] * N_CHIP
    conv_ch = ev_dw_b.shape[-1]
    conv_w = ev_dw_w.shape[1]
    attn_w = ev_mix - conv_ch
    kv_w = N_KV_HEADS * HEAD_DIM
    assert ev_in == 2 * kv_w + 2 * attn_w + 3 * conv_ch and conv_w // 2 < CONV_HALO
    sgu_w = od_w_out.shape[1] * N_CHIP
    wa = ada_w.shape[-1]
    cb = math.gcd(2 * kv_w, attn_w, conv_ch)
    off = dict(k=0, v=kv_w, q=2 * kv_w, za=2 * kv_w + attn_w, a=2 * kv_w + 2 * attn_w,
               b=2 * kv_w + 2 * attn_w + conv_ch, zb=2 * kv_w + 2 * attn_w + 2 * conv_ch)
    dims = dict(kv_w=kv_w, attn_w=attn_w, conv_ch=conv_ch, conv_w=conv_w, cb=cb, off=off)
    tr = 256 if (n_lat % 256 == 0 and n_ctx % 256 == 0) else 128

    mx, my, mc = lax.axis_index("x"), lax.axis_index("y"), lax.axis_index("c")
    me = 4 * mx + 2 * my + mc
    chip = 2 * mx + my

    x2d, tgt2d, ctx2d = x[0], loss_target[0], ctx[0]
    ev_dw_w_l = ev_dw_w[0]
    dwc = ev_dw_w_l.shape[1]
    lnc = od_ln_g.shape[1]

    g_c = _allgather_small(jnp.broadcast_to(c, (8, d)), "gather_cond")[:, 0, :]
    c_rows = jnp.concatenate([g_c, c_ctx[None, :], jnp.zeros((MOD_ROWS - N_DEV - 1, d), F32)], axis=0)
    c_rows_t = c_rows.T
    ada_b_shard = lax.dynamic_slice_in_dim(ada_b, chip * wa, wa, axis=1)[:, None, :]
    mod_part = _mod_fwd(c_rows_t, ada_w, ada_b_shard)
    part_shapes = [(2, MOD_ROWS, wa), (conv_w, dwc), (1, lnc), (1, lnc)]
    g_parts = _allgather_small(_pack([mod_part, ev_dw_w_l, od_ln_g, od_ln_b]), "gather_mod")
    per_chip = [_unpack(g_parts[2 * s], part_shapes) for s in range(N_CHIP)]
    mod_all = jnp.concatenate([p[0] for p in per_chip], axis=-1)
    dw_w_full = jnp.concatenate([p[1] for p in per_chip], axis=-1)
    od_ln_g_full = jnp.concatenate([p[2] for p in per_chip], axis=-1)
    od_ln_b_full = jnp.concatenate([p[3] for p in per_chip], axis=-1)
    dw_w_pad = jnp.pad(dw_w_full, ((0, 2 * CONV_HALO - conv_w), (0, 0)))
    mod_me = lax.dynamic_slice_in_dim(mod_all, me, 1, axis=1)
    shift0, scale0, gate0 = mod_me[0, :, :d], mod_me[0, :, d:2 * d], mod_me[0, :, 2 * d:]
    shift1, scale1, gate1 = mod_me[1, :, :d], mod_me[1, :, d:2 * d], mod_me[1, :, 2 * d:]
    shift_c, scale_c = mod_all[0, N_DEV:N_DEV + 1, :d], mod_all[0, N_DEV:N_DEV + 1, d:2 * d]
    g0, g1 = norm_g[0:1], norm_g[1:2]

    lay = dict(ev_w_in=_Sharded((d, ev_in), True), ev_w_out=_Sharded((ev_mix, d), False),
               od_w_in=_Sharded((d, 3 * sgu_w), True), od_w_out=_Sharded((sgu_w, d), False))
    big = list(lay)
    chip_arr = jnp.reshape(chip, (1,)).astype(jnp.int32)
    core_arr = jnp.reshape(mc, (1,)).astype(jnp.int32)
    own = {n: _cast_into_full(weights[n][0], lay[n], chip_arr, f"cast_{n}") for n in big}
    later = big[1:]
    gather_later = _gather_comm([own[n] for n in later], [lay[n] for n in later])

    def reduce_start(g, n):
        theirs = _pair_exchange([g], [lay[n]], f"pair_exchange_{n}")[0]
        psum = _pair_sum(g, theirs, lay[n], core_arr, f"pair_sum_{n}")
        return psum, _scatter_comm([psum], [lay[n]])

    h0, w_ev_in = _adaln_fwd(x2d, g0, shift0, scale0, tr, "adaln0_fwd",
                             comm=_gather_comm([own["ev_w_in"]], [lay["ev_w_in"]]), out_rows=n_ext)
    w_full = {"ev_w_in": w_ev_in}
    h0e = _adaln_fwd(ctx2d, g0, shift_c, scale_c, tr, "adaln0_ctx_fwd", into=h0, row0=n_lat)
    tm_e = _pick(n_ext, (1408, 768, 640, 512, 256, 128))
    tk_e = _pick(n_ext, (768, 640, 512, 256, 128))
    tm_l = _pick(n_lat, (1024, 512, 256, 128))
    p0, *gathered = _mm(h0e, w_full["ev_w_in"], name="mm_ev_in", tm=tm_e, tn=_pick(ev_in, (512, 256, 128)), tk=d,
                        out_dtype=BF16, comm=gather_later)
    w_full.update(zip(later, gathered))
    cos_i, sin_s = _rope_tables(n_lat, n_ctx)
    q_hat, k_all, v_all = _qk_prep(p0, cos_i, sin_s, ev_q_norm, ev_k_norm, dims, tr)
    tq = _pick(n_lat, (256, 128))
    tkk = _pick(n_ext, (1408, 640, 512, 256, 128))
    attn, lse = _flash_fwd(q_hat, k_all, v_all, n_lat, dims, tq, tkk)
    mix, yc = _mix_fwd(attn, p0, dw_w_pad, ev_dw_b, ev_ln_g, ev_ln_b, dims, tr)
    o0, x1 = _mm(mix, w_full["ev_w_out"], name="mm_ev_out", tm=tm_l, tn=_pick(d, (1024, 512, 256)),
                 tk=ev_mix, out_dtype=BF16, res=x2d, gate=gate0)

    h1 = _adaln_fwd(x1, g1, shift1, scale1, tr, "adaln1_fwd")
    p1 = _mm(h1, w_full["od_w_in"], name="mm_od_in", tm=tm_l, tn=_pick(3 * sgu_w, (512, 256, 128)), tk=d,
             out_dtype=BF16)
    ws_b = od_ws[0].astype(BF16)
    ws_t_b = jnp.swapaxes(od_ws[0], 1, 2).astype(BF16)
    bs_t = od_bs[0].T
    m1 = _sgu_fwd(p1, od_ln_g_full, od_ln_b_full, ws_b, bs_t, tr)
    o1, x2 = _mm(m1, w_full["od_w_out"], name="mm_od_out", tm=tm_l, tn=_pick(d, (1024, 512, 256)),
                 tk=sgu_w, out_dtype=BF16, res=x1, gate=gate1)

    dx2, do1, loss_cols, d_final_g, dgate1 = _final_loss(x2, tgt2d, final_g[None, :], o1, gate1, tr)
    loss = lax.psum(0.5 / d * jnp.sum(loss_cols), ("x", "y", "c"))

    tk_l = _pick(n_lat, (1024, 512, 256, 128))
    psums, slots = {}, {}
    tk_nt = (1408, 1024, 768, 512, 256, 128)
    g_od_w_out = _mm(m1, do1, name="mm_od_out_dw", ta=True, tm=_pick(sgu_w, (1024, 512, 256)),
                     tn=_pick(d, (1024, 512, 256)), tk=tk_l, out_dtype=BF16)
    psums["od_w_out"], sc = reduce_start(g_od_w_out, "od_w_out")
    dm1, slots["od_w_out"] = _mm(do1, w_full["od_w_out"], name="mm_od_out_dx", tb=True, tm=tm_l,
                                 tn=_pick(sgu_w, (1024, 512, 256)), tk=d, out_dtype=BF16, comm=sc)
    dp1, d_ws, d_bs_t, d_od_ln_g, d_od_ln_b = _sgu_bwd(dm1, p1, od_ln_g_full, od_ln_b_full, ws_b, ws_t_b, bs_t, tr)
    small1 = [dgate1, d_final_g, d_od_ln_g, d_od_ln_b, d_ws, d_bs_t.T]
    g_od_w_in, g_small1 = _mm(h1, dp1, name="mm_od_in_dw", ta=True, tm=_pick(d, (1024, 512, 256)),
                              tn=_pick(3 * sgu_w, (1536, 768, 512, 384, 256, 128)), tk=tk_l, out_dtype=BF16,
                              comm=_allgather_comm(_pack(small1)))
    psums["od_w_in"], sc = reduce_start(g_od_w_in, "od_w_in")
    dh1, slots["od_w_in"] = _mm(dp1, w_full["od_w_in"], name="mm_od_in_dx", tb=True, tm=tm_l,
                                tn=_pick(d, (1024, 512, 256)), tk=_pick(3 * sgu_w, tk_nt), out_dtype=BF16, comm=sc)
    zero_d = jnp.zeros((1, d), F32)
    b1 = _adaln_bwd(x1, dh1, 0, g1, scale1, zero_d, tr, "adaln1_bwd", dres=dx2, o_prev=o0, gate_prev=gate0)
    dx1, do0, dgate0 = b1["dx"], b1["do_prev"], b1["dgate_prev"]

    g_ev_w_out = _mm(mix, do0, name="mm_ev_out_dw", ta=True, tm=_pick(ev_mix, (1024, 512, 256)),
                     tn=_pick(d, (1024, 512, 256)), tk=tk_l, out_dtype=BF16)
    psums["ev_w_out"], sc = reduce_start(g_ev_w_out, "ev_w_out")
    dmix, slots["ev_w_out"] = _mm(do0, w_full["ev_w_out"], name="mm_ev_out_dx", tb=True, tm=tm_l,
                                  tn=_pick(ev_mix, (1024, 512, 256)), tk=d, out_dtype=BF16, comm=sc)
    dattn, dza, dzb, dyc, d_ev_ln_g, d_ev_ln_b, d_dw_b = _mix_bwd_pointwise(
        dmix, attn, p0, yc, ev_ln_g, ev_ln_b, dims, tr, n_ext)
    dab, d_dw_w_pad = _conv_bwd(dyc, p0, dw_w_pad, dims, tr, n_ext)
    dq_hat, dk_hat, dv_all = _flash_bwd(q_hat, k_all, v_all, dattn, attn, lse, n_lat, dims, tq, tkk)
    dkvq, d_q_norm, d_k_norm = _qk_prep_bwd(dq_hat, dk_hat, dv_all, p0, cos_i, sin_s, ev_q_norm, ev_k_norm,
                                            dims, tr, n_lat)
    dp0 = jnp.concatenate([dkvq, dza, dab, dzb], axis=1)
    small2 = [b1["dshift"], b1["dscale"], dgate0, b1["dg"], d_q_norm, d_k_norm, d_dw_w_pad[:conv_w], d_dw_b,
              d_ev_ln_g, d_ev_ln_b]
    g_ev_w_in, g_small2 = _mm(h0e, dp0, name="mm_ev_in_dw", ta=True, tm=_pick(d, (1024, 512, 256)),
                              tn=_pick(ev_in, (1408, 768, 512, 256, 128)), tk=tk_e, out_dtype=BF16,
                              comm=_allgather_comm(_pack(small2)))
    psums["ev_w_in"], sc = reduce_start(g_ev_w_in, "ev_w_in")
    dh0, slots["ev_w_in"] = _mm(dp0, w_full["ev_w_in"], name="mm_ev_in_dx", tb=True, tm=tm_e,
                                tn=_pick(d, (1024, 512, 256)), tk=_pick(ev_in, tk_nt), out_dtype=BF16, comm=sc)
    lays = [lay[n] for n in big]
    halves = [_chip_sum(psums[n], slots[n], lay[n], chip_arr, core_arr, f"chip_sum_{n}") for n in big]
    bc = _adaln_bwd(ctx2d, dh0, n_lat, g0, scale_c, zero_d, tr, "adaln0_ctx_bwd", comm=_share_comm(halves, lays))
    g_big = dict(zip(big, bc["carried"]))
    b0 = _adaln_bwd(x2d, dh0, 0, g0, scale0, bc["dg"], tr, "adaln0_bwd", dres=dx1)
    grad_x = b0["dx"]

    small3 = [b0["dshift"], b0["dscale"], bc["dshift"], bc["dscale"], b0["dg"]]
    g_small3 = _allgather_small(_pack(small3), "gather_small_grads")

    def totals_and_rows(gathered, parts, n_rows, name):
        shapes = [a.shape for a in parts]
        tot = _unpack(_sum_slots(gathered, name), shapes)
        return tot, [r[:, 0, :] for r in _unpack(gathered, shapes[:n_rows])]

    (t_dgate1, t_fg, t_oln_g, t_oln_b, t_ws, t_bs), (dgate1_rows,) = totals_and_rows(
        g_small1, small1, 1, "sum_small_grads1")
    ((t_dshift1, t_dscale1, t_dgate0, t_g1, t_qn, t_kn, t_dw_w, t_dw_b, t_eln_g, t_eln_b),
     (dshift1_rows, dscale1_rows, dgate0_rows)) = totals_and_rows(g_small2, small2, 3, "sum_small_grads2")
    (t_dshift0, t_dscale0, t_dshift_c, t_dscale_c, t_g0), (dshift0_rows, dscale0_rows) = totals_and_rows(
        g_small3, small3, 2, "sum_small_grads3")
    zeros_d = jnp.zeros((1, d), F32)
    t_dmodc = jnp.concatenate([t_dshift_c, t_dscale_c, zeros_d], axis=1)
    t_dmod0 = jnp.concatenate([t_dshift0, t_dscale0, t_dgate0], axis=1)
    t_dmod1 = jnp.concatenate([t_dshift1, t_dscale1, t_dgate1], axis=1)
    dmod0_rows = jnp.concatenate([dshift0_rows, dscale0_rows, dgate0_rows], axis=1)
    dmod1_rows = jnp.concatenate([dshift1_rows, dscale1_rows, dgate1_rows], axis=1)
    pad_rows = jnp.zeros((MOD_ROWS - N_DEV - 1, 3 * d), F32)
    dm_l0 = jnp.concatenate([dmod0_rows, t_dmodc, pad_rows], axis=0)
    dm_l1 = jnp.concatenate([dmod1_rows, jnp.zeros((MOD_ROWS - N_DEV, 3 * d), F32)], axis=0)
    dm_shard = lax.dynamic_slice_in_dim(jnp.stack([dm_l0, dm_l1]), chip * wa, wa, axis=2)
    g_ada_w, dsc = _mod_bwd(c_rows_t, dm_shard, ada_w)
    g_dsc = _allgather_small(_pack([dsc[0]]), "gather_cctx")
    g_c_ctx = _cctx_grad(g_dsc, _pack([c_ctx])).reshape(-1)[:d]
    g_ada_b = jnp.stack([t_dmod0[0] + t_dmodc[0], t_dmod1[0]])

    grads = dict(
        c_ctx=g_c_ctx, ada_w=g_ada_w, ada_b=g_ada_b, norm_g=jnp.concatenate([t_g0, t_g1], axis=0),
        ev_w_in=g_big["ev_w_in"][None], ev_q_norm=t_qn, ev_k_norm=t_kn,
        ev_dw_w=lax.dynamic_slice_in_dim(t_dw_w, chip * dwc, dwc, axis=1)[None], ev_dw_b=t_dw_b,
        ev_ln_g=t_eln_g, ev_ln_b=t_eln_b, ev_w_out=g_big["ev_w_out"][None], od_w_in=g_big["od_w_in"][None],
        od_ln_g=lax.dynamic_slice_in_dim(t_oln_g, chip * lnc, lnc, axis=1),
        od_ln_b=lax.dynamic_slice_in_dim(t_oln_b, chip * lnc, lnc, axis=1),
        od_ws=t_ws[None], od_bs=t_bs[None], od_w_out=g_big["od_w_out"][None], final_g=t_fg[0])
    grads = {n: grads[n].reshape(weights[n].shape) for n in order}

    delta, new_m, new_v = {}, {}, {}
    large = ("ada_w", "ev_w_in", "ev_w_out", "od_w_in", "od_w_out")
    for n in large:
        shp = weights[n].shape
        as2d = lambda a: a.reshape(-1, shp[-1])
        outs = _adamw(as2d(weights[n]), as2d(grads[n]), as2d(mom_m[n]), as2d(mom_v[n]), f"adamw_{n}",
                      copy_grad=n in big)
        delta[n], new_m[n], new_v[n] = (o.reshape(shp) for o in outs[:3])
        if n in big:
            grads[n] = outs[3].reshape(shp)
    rest_names = [n for n in order if n not in large]
    rest_shapes = [weights[n].shape for n in rest_names]
    dl, nm, nv = _adamw(_pack([weights[n] for n in rest_names]), _pack([grads[n] for n in rest_names]),
                        _pack([mom_m[n] for n in rest_names]), _pack([mom_v[n] for n in rest_names]), "adamw_small")
    for n, a, b_, c_ in zip(rest_names, _unpack(dl, rest_shapes), _unpack(nm, rest_shapes), _unpack(nv, rest_shapes)):
        delta[n], new_m[n], new_v[n] = a, b_, c_

    return (loss, grad_x[None], *[grads[n] for n in order], *[delta[n] for n in order],
            *[new_m[n] for n in order], *[new_v[n] for n in order])
```

```python
import math

import jax
import jax.numpy as jnp
from jax import lax
from jax.experimental import pallas as pl
from jax.experimental.pallas import tpu as pltpu

F32 = jnp.float32
BF16 = jnp.bfloat16
EPS = 1e-6
GRID_W = 64
ROPE_THETA = 10000.0
HEAD_DIM = 128
N_KV_HEADS = 2
CONV_HALO = 16
LANES = 128
SUBLANES = 8
STENCIL_ROWS = 32
STENCIL_CHAINS = 4
REDUCE_ROWS = 32
N_DEV = 8
N_CHIP = 4
MOD_ROWS = 16
PACK_ROWS = 64
ADAM_LR, ADAM_B1, ADAM_B2, ADAM_EPS, ADAM_WD, ADAM_STEP = 0.001, 0.9, 0.999, 1e-08, 0.01, 10
VMEM_LIMIT = 56 * 1024 * 1024
MESH = pl.DeviceIdType.MESH
ANY = pl.BlockSpec(memory_space=pl.ANY)
VMEM_SPEC = pl.BlockSpec(memory_space=pltpu.VMEM)
CHIP_DELTAS = ((1, 0), (0, 1), (1, 1))


def _cp(sem=None):
    return pltpu.CompilerParams(dimension_semantics=sem, vmem_limit_bytes=VMEM_LIMIT)


def _pick(n, cands):
    for c in cands:
        if n % c == 0:
            return c
    raise ValueError(f"no tile for {n} in {cands}")


def _sigmoid(x):
    return 1.0 / (1.0 + jnp.exp(-x))


def _silu(x):
    return x * _sigmoid(x)


def _silu_and_grad(x):
    s = _sigmoid(x)
    y = x * s
    return y, s + y * (1.0 - s)


def _dsilu(x):
    return _silu_and_grad(x)[1]


_GELU_C = math.sqrt(2.0 / math.pi)
_GELU_A = 0.044715


def _gelu_and_grad(x):
    x2 = x * x
    t = jnp.tanh(x * (_GELU_C + (_GELU_C * _GELU_A) * x2))
    h = 0.5 + 0.5 * t
    return x * h, h + x * (1.0 - t * t) * (0.5 * _GELU_C + (1.5 * _GELU_C * _GELU_A) * x2)


def _gelu(x):
    x2 = x * x
    return x * (0.5 + 0.5 * jnp.tanh(x * (_GELU_C + (_GELU_C * _GELU_A) * x2)))


def _vec(d):
    return pl.BlockSpec((1, d), lambda *_: (0, 0))


def _cat(refs):
    parts = [r[...].astype(F32) for r in refs]
    return parts[0] if len(parts) == 1 else jnp.concatenate(parts, axis=1)


def _col_specs(rows, off, width, cb, row_map):
    assert off % cb == 0 and width % cb == 0
    return [pl.BlockSpec((rows, cb), (lambda *g, _c=off // cb + t: (row_map(*g), _c))) for t in range(width // cb)]


def _my_pos():
    return lax.axis_index("x"), lax.axis_index("y"), lax.axis_index("c")


def _allgather_small(x, name):
    r, c = x.shape

    def body(x_ref, out_ref, send_sems, recv_sems, local_sem):
        mx, my, mc = _my_pos()
        me = 4 * mx + 2 * my + mc
        mine = pltpu.make_async_copy(x_ref, out_ref.at[me], local_sem)
        mine.start()
        deltas = [(dx, dy, dc) for dx in (0, 1) for dy in (0, 1) for dc in (0, 1) if (dx, dy, dc) != (0, 0, 0)]
        sends = []
        for k, (dx, dy, dc) in enumerate(deltas):
            px, py, pc = (mx + dx) % 2, (my + dy) % 2, (mc + dc) % 2
            cp = pltpu.make_async_remote_copy(
                src_ref=x_ref, dst_ref=out_ref.at[me], send_sem=send_sems.at[k], recv_sem=recv_sems.at[k],
                device_id=(px, py, pc), device_id_type=MESH)
            cp.start()
            sends.append(cp)
        for k, (dx, dy, dc) in enumerate(deltas):
            px, py, pc = (mx + dx) % 2, (my + dy) % 2, (mc + dc) % 2
            peer = 4 * px + 2 * py + pc
            pltpu.make_async_remote_copy(
                src_ref=x_ref, dst_ref=out_ref.at[peer], send_sem=send_sems.at[k], recv_sem=recv_sems.at[k],
                device_id=(px, py, pc), device_id_type=MESH).wait_recv()
        for cp in sends:
            cp.wait_send()
        mine.wait()

    return pl.pallas_call(
        body, name=name,
        out_shape=jax.ShapeDtypeStruct((N_DEV, r, c), x.dtype),
        in_specs=[VMEM_SPEC], out_specs=VMEM_SPEC,
        scratch_shapes=[pltpu.SemaphoreType.DMA((N_DEV - 1,)), pltpu.SemaphoreType.DMA((N_DEV - 1,)),
                        pltpu.SemaphoreType.DMA],
        compiler_params=pltpu.CompilerParams(vmem_limit_bytes=VMEM_LIMIT),
    )(x)


class _Sharded:
    def __init__(self, full_shape, by_cols):
        self.full = full_shape
        self.by_cols = by_cols
        rows, cols = full_shape
        if by_cols:
            self.shard, self.half, self.halves = (rows, cols // N_CHIP), (rows // 2, cols // N_CHIP), (rows // 2, cols)
        else:
            self.shard, self.half, self.halves = (rows // N_CHIP, cols), (rows // N_CHIP, cols // 2), (rows, cols // 2)

    def region(self, ref, s, h):
        if self.by_cols:
            return ref.at[pl.ds(h * self.half[0], self.half[0]), pl.ds(s * self.shard[1], self.shard[1])]
        return ref.at[pl.ds(s * self.shard[0], self.shard[0]), pl.ds(h * self.half[1], self.half[1])]

    def halves_of_full(self, ref, h):
        if self.by_cols:
            return ref.at[pl.ds(h * self.halves[0], self.halves[0]), :]
        return ref.at[:, pl.ds(h * self.halves[1], self.halves[1])]

    def region_in_halves(self, ref, s):
        if self.by_cols:
            return ref.at[:, pl.ds(s * self.shard[1], self.shard[1])]
        return ref.at[pl.ds(s * self.shard[0], self.shard[0]), :]

    def half_of_shard(self, ref, h):
        if self.by_cols:
            return ref.at[pl.ds(h * self.half[0], self.half[0]), :]
        return ref.at[:, pl.ds(h * self.half[1], self.half[1])]


def _row_tile(rows, row_bytes):
    for t in (512, 256, 128, 64, 32, 16):
        if rows % t == 0 and t * row_bytes <= 2 * 1024 * 1024:
            return t
    return 16


def _cast_into_full(w_shard, lay, chip_arr, name):
    r, c = lay.shard
    tr = _row_tile(r, c * 4)
    nt = r // tr

    def body(chip_ref, w_ref, o_ref):
        o_ref[...] = w_ref[...].astype(BF16)

    if lay.by_cols:
        out_map = lambda i, chip_ref: (i, chip_ref[0])
    else:
        out_map = lambda i, chip_ref: (chip_ref[0] * nt + i, 0)
    return pl.pallas_call(
        body, name=name,
        grid_spec=pltpu.PrefetchScalarGridSpec(
            num_scalar_prefetch=1, grid=(nt,),
            in_specs=[pl.BlockSpec((tr, c), lambda i, chip_ref: (i, 0))],
            out_specs=pl.BlockSpec((tr, c), out_map)),
        out_shape=jax.ShapeDtypeStruct(lay.full, BF16), compiler_params=_cp(("parallel",)))(chip_arr, w_shard)


class _Carried:
    def __init__(self, ins, out_shapes, aliases, sem_shape, start, finish):
        self.ins, self.out_shapes, self.aliases, self.sem_shape = list(ins), list(out_shapes), dict(aliases), sem_shape
        self.start, self.finish = start, finish

    def scratch(self):
        return [pltpu.SemaphoreType.DMA(self.sem_shape), pltpu.SemaphoreType.DMA(self.sem_shape)]

    def split(self, in_refs, out_refs, scratch_refs):
        ni, no = len(self.ins), len(self.out_shapes)
        return in_refs[len(in_refs) - ni:], out_refs[len(out_refs) - no:], scratch_refs[-2], scratch_refs[-1]


def _call(body, *, name, grid, in_specs, out_specs, out_shape, args, sem, comm=None, aliases=None):
    if comm is None:
        return pl.pallas_call(body, name=name, grid=grid, in_specs=in_specs, out_specs=out_specs,
                              out_shape=out_shape, input_output_aliases=aliases or {},
                              compiler_params=_cp(sem))(*args)
    n_in, n_out, n_ci, n_co = len(in_specs), len(out_specs), len(comm.ins), len(comm.out_shapes)

    def carrying(*refs):
        in_refs, out_refs = refs[:n_in + n_ci], refs[n_in + n_ci:n_in + n_ci + n_out + n_co]
        carried = comm.split(in_refs, out_refs, refs[n_in + n_ci + n_out + n_co:])
        first, last = None, None
        for axis, extent in enumerate(grid):
            at0, at1 = pl.program_id(axis) == 0, pl.program_id(axis) == extent - 1
            first = at0 if first is None else jnp.logical_and(first, at0)
            last = at1 if last is None else jnp.logical_and(last, at1)

        @pl.when(first)
        def _():
            comm.start(*carried)

        body(*in_refs[:n_in], *out_refs[:n_out])

        @pl.when(last)
        def _():
            comm.finish(*carried)

    return pl.pallas_call(
        carrying, name=name, grid=grid, in_specs=list(in_specs) + [ANY] * n_ci,
        out_specs=list(out_specs) + [ANY] * n_co, out_shape=list(out_shape) + comm.out_shapes,
        scratch_shapes=comm.scratch(),
        input_output_aliases={**(aliases or {}), **{n_in + s: n_out + d for s, d in comm.aliases.items()}},
        compiler_params=_cp(("arbitrary",) * len(grid)))(*args, *comm.ins)


def _gather_comm(fulls, layouts):
    n = len(fulls)

    def ici(ins, outs, send_sems, recv_sems, a, j, landed=False):
        mx, my, mc = _my_pos()
        dx, dy = CHIP_DELTAS[j]
        px, py = (mx + dx) % 2, (my + dy) % 2
        src_chip = 2 * px + py if landed else 2 * mx + my
        return pltpu.make_async_remote_copy(
            src_ref=layouts[a].region(ins[a], src_chip, mc), dst_ref=layouts[a].region(outs[a], src_chip, mc),
            send_sem=send_sems.at[a, j], recv_sem=recv_sems.at[a, j], device_id=(px, py, mc), device_id_type=MESH)

    def d2d(ins, outs, send_sems, recv_sems, a, j, landed=False):
        mx, my, mc = _my_pos()
        dx, dy = CHIP_DELTAS[j]
        other = 2 * ((mx + dx) % 2) + (my + dy) % 2
        half = 1 - mc if landed else mc
        region = layouts[a].region(outs[a], other, half)
        return pltpu.make_async_remote_copy(
            src_ref=region, dst_ref=region, send_sem=send_sems.at[a, 3 + j], recv_sem=recv_sems.at[a, 3 + j],
            device_id=(mx, my, 1 - mc), device_id_type=MESH)

    pairs = [(a, j) for a in range(n) for j in range(3)]

    def start(*r):
        for a, j in pairs:
            ici(*r, a, j).start()

    def finish(*r):
        for a, j in pairs:
            ici(*r, a, j, landed=True).wait_recv()
            d2d(*r, a, j).start()
        for a, j in pairs:
            d2d(*r, a, j, landed=True).wait_recv()
        for a, j in pairs:
            ici(*r, a, j).wait_send()
            d2d(*r, a, j).wait_send()

    return _Carried(fulls, [jax.ShapeDtypeStruct(lay.full, BF16) for lay in layouts], {a: a for a in range(n)},
                    (n, 6), start, finish)


def _pair_exchange(grads, layouts, name):
    n = len(grads)

    def body(*refs):
        ins, outs = refs[:n], refs[n:2 * n]
        send_sems, recv_sems = refs[2 * n:]
        mx, my, mc = _my_pos()
        copies = []
        for a, lay in enumerate(layouts):
            cp = pltpu.make_async_remote_copy(
                src_ref=lay.halves_of_full(ins[a], 1 - mc), dst_ref=outs[a],
                send_sem=send_sems.at[a], recv_sem=recv_sems.at[a],
                device_id=(mx, my, 1 - mc), device_id_type=MESH)
            cp.start()
            copies.append(cp)
        for cp in copies:
            cp.wait()

    return pl.pallas_call(
        body, name=name,
        out_shape=[jax.ShapeDtypeStruct(lay.halves, g.dtype) for lay, g in zip(layouts, grads)],
        in_specs=[ANY] * n, out_specs=[ANY] * n,
        scratch_shapes=[pltpu.SemaphoreType.DMA((n,)), pltpu.SemaphoreType.DMA((n,))],
    )(*grads)


def _pair_sum(g, theirs, lay, core_arr, name):
    r, c = lay.halves
    tr = _row_tile(r, c * 4)
    nt = r // tr

    def body(core_ref, g_ref, t_ref, o_ref):
        o_ref[...] = (g_ref[...].astype(F32) + t_ref[...].astype(F32)).astype(BF16)

    if lay.by_cols:
        g_map = lambda i, core_ref: (core_ref[0] * nt + i, 0)
    else:
        g_map = lambda i, core_ref: (i, core_ref[0])
    plain = pl.BlockSpec((tr, c), lambda i, core_ref: (i, 0))
    return pl.pallas_call(
        body, name=name,
        grid_spec=pltpu.PrefetchScalarGridSpec(
            num_scalar_prefetch=1, grid=(nt,), in_specs=[pl.BlockSpec((tr, c), g_map), plain], out_specs=plain),
        out_shape=jax.ShapeDtypeStruct((r, c), BF16), compiler_params=_cp(("parallel",)))(core_arr, g, theirs)


def _scatter_comm(pair_sums, layouts):
    n = len(pair_sums)

    def copy(ins, outs, send_sems, recv_sems, a, j):
        mx, my, mc = _my_pos()
        dx, dy = CHIP_DELTAS[j]
        px, py = (mx + dx) % 2, (my + dy) % 2
        return pltpu.make_async_remote_copy(
            src_ref=layouts[a].region_in_halves(ins[a], 2 * px + py), dst_ref=outs[a].at[j],
            send_sem=send_sems.at[a, j], recv_sem=recv_sems.at[a, j], device_id=(px, py, mc), device_id_type=MESH)

    pairs = [(a, j) for a in range(n) for j in range(3)]

    def start(*r):
        for a, j in pairs:
            copy(*r, a, j).start()

    def finish(*r):
        for a, j in pairs:
            copy(*r, a, j).wait()

    return _Carried(pair_sums, [jax.ShapeDtypeStruct((3,) + lay.half, BF16) for lay in layouts], {}, (n, 3),
                    start, finish)


def _chip_sum(pair_sum, slots, lay, chip_arr, core_arr, name):
    r, c = lay.half
    tr = _row_tile(r, c * 4)
    nt = r // tr

    def body(chip_ref, core_ref, s_ref, slot_ref, o_ref):
        acc = s_ref[...].astype(F32)
        for j in range(3):
            acc = acc + slot_ref[j].astype(F32)
        o_ref[...] = acc

    if lay.by_cols:
        s_map = lambda i, chip_ref, core_ref: (i, chip_ref[0])
        o_map = lambda i, chip_ref, core_ref: (core_ref[0] * nt + i, 0)
    else:
        s_map = lambda i, chip_ref, core_ref: (chip_ref[0] * nt + i, 0)
        o_map = lambda i, chip_ref, core_ref: (i, core_ref[0])
    return pl.pallas_call(
        body, name=name,
        grid_spec=pltpu.PrefetchScalarGridSpec(
            num_scalar_prefetch=2, grid=(nt,),
            in_specs=[pl.BlockSpec((tr, c), s_map),
                      pl.BlockSpec((3, tr, c), lambda i, chip_ref, core_ref: (0, i, 0))],
            out_specs=pl.BlockSpec((tr, c), o_map)),
        out_shape=jax.ShapeDtypeStruct(lay.shard, F32), compiler_params=_cp(("parallel",)))(
            chip_arr, core_arr, pair_sum, slots)


def _share_comm(bufs, layouts):
    n = len(bufs)

    def copy(ins, outs, send_sems, recv_sems, a, landed=False):
        mx, my, mc = _my_pos()
        half = 1 - mc if landed else mc
        return pltpu.make_async_remote_copy(
            src_ref=layouts[a].half_of_shard(ins[a], half), dst_ref=layouts[a].half_of_shard(outs[a], half),
            send_sem=send_sems.at[a], recv_sem=recv_sems.at[a], device_id=(mx, my, 1 - mc), device_id_type=MESH)

    def start(*r):
        for a in range(n):
            copy(*r, a).start()

    def finish(*r):
        for a in range(n):
            copy(*r, a, landed=True).wait_recv()
        for a in range(n):
            copy(*r, a).wait_send()

    return _Carried(bufs, [jax.ShapeDtypeStruct(lay.shard, F32) for lay in layouts], {a: a for a in range(n)},
                    (n,), start, finish)


def _allgather_comm(x):
    deltas = [(dx, dy, dc) for dx in (0, 1) for dy in (0, 1) for dc in (0, 1) if (dx, dy, dc) != (0, 0, 0)]
    local = len(deltas)

    def remote(ins, outs, send_sems, recv_sems, k, landed=False):
        mx, my, mc = _my_pos()
        dx, dy, dc = deltas[k]
        px, py, pc = (mx + dx) % 2, (my + dy) % 2, (mc + dc) % 2
        slot = 4 * px + 2 * py + pc if landed else 4 * mx + 2 * my + mc
        return pltpu.make_async_remote_copy(
            src_ref=ins[0], dst_ref=outs[0].at[slot], send_sem=send_sems.at[k], recv_sem=recv_sems.at[k],
            device_id=(px, py, pc), device_id_type=MESH)

    def mine(ins, outs, send_sems, recv_sems):
        mx, my, mc = _my_pos()
        return pltpu.make_async_copy(ins[0], outs[0].at[4 * mx + 2 * my + mc], send_sems.at[local])

    def start(*r):
        mine(*r).start()
        for k in range(len(deltas)):
            remote(*r, k).start()

    def finish(*r):
        for k in range(len(deltas)):
            remote(*r, k, landed=True).wait_recv()
        for k in range(len(deltas)):
            remote(*r, k).wait_send()
        mine(*r).wait()

    return _Carried([x], [jax.ShapeDtypeStruct((N_DEV,) + x.shape, x.dtype)], {}, (N_DEV,), start, finish)


def _sum_slots(x, name):
    s, r, c = x.shape
    tr = _pick(r, (256, 128, 64, 32, 16, 8))

    def body(x_ref, o_ref):
        acc = x_ref[0]
        for k in range(1, s):
            acc = acc + x_ref[k]
        o_ref[...] = acc

    return pl.pallas_call(body, name=name, grid=(r // tr,),
                          in_specs=[pl.BlockSpec((s, tr, c), lambda i: (0, i, 0))],
                          out_specs=pl.BlockSpec((tr, c), lambda i: (i, 0)),
                          out_shape=jax.ShapeDtypeStruct((r, c), F32), compiler_params=_cp(("parallel",)))(x)


def _adamw(w, g, m, v, name, copy_grad=False):
    r, c = w.shape
    tr = _pick(r, (256, 128, 64, 32, 16, 8))
    bc1 = 1.0 - ADAM_B1 ** ADAM_STEP
    bc2 = 1.0 - ADAM_B2 ** ADAM_STEP
    n_out = 4 if copy_grad else 3

    def body(w_ref, g_ref, m_ref, v_ref, d_ref, nm_ref, nv_ref, *g_copy):
        gv = g_ref[...]
        nm = ADAM_B1 * m_ref[...] + (1.0 - ADAM_B1) * gv
        nv = ADAM_B2 * v_ref[...] + (1.0 - ADAM_B2) * (gv * gv)
        d_ref[...] = -ADAM_LR * ((nm / bc1) / (jnp.sqrt(nv / bc2) + ADAM_EPS) + ADAM_WD * w_ref[...])
        nm_ref[...] = nm
        nv_ref[...] = nv
        if copy_grad:
            g_copy[0][...] = gv

    spec = pl.BlockSpec((tr, c), lambda i: (i, 0))
    shp = jax.ShapeDtypeStruct((r, c), F32)
    return _call(body, name=name, grid=(r // tr,), in_specs=[spec] * 4, out_specs=[spec] * n_out,
                 out_shape=[shp] * n_out, args=(w, g, m, v), sem=("parallel",))


def _cctx_grad(parts, c_ctx2d):
    def body(p_ref, c_ref, o_ref):
        tot = ((p_ref[0] + p_ref[2]) + p_ref[4]) + p_ref[6]
        o_ref[...] = tot * _dsilu(c_ref[...])

    return pl.pallas_call(body, name="cctx_grad", in_specs=[VMEM_SPEC, VMEM_SPEC], out_specs=VMEM_SPEC,
                          out_shape=jax.ShapeDtypeStruct(c_ctx2d.shape, F32))(parts, c_ctx2d)


def _mod_fwd(c_rows_t, ada_w, ada_b_shard):
    nl, d, w = ada_w.shape
    td = _pick(d, (256, 128))
    nd = d // td

    def body(ct_ref, w_ref, b_ref, o_ref):
        i = pl.program_id(1)

        @pl.when(i == 0)
        def _():
            o_ref[0] = jnp.broadcast_to(b_ref[0], (MOD_ROWS, w))

        st = _silu(ct_ref[...])
        wv = w_ref[0]
        used = N_DEV + 1
        rows = [jnp.sum(st[:, r:r + 1] * wv, axis=0, keepdims=True) for r in range(used)]
        rows.append(jnp.zeros((MOD_ROWS - used, w), F32))
        o_ref[0] += jnp.concatenate(rows, axis=0)

    return pl.pallas_call(
        body, name="mod_fwd", grid=(nl, nd),
        in_specs=[pl.BlockSpec((td, MOD_ROWS), lambda l, i: (i, 0)),
                  pl.BlockSpec((1, td, w), lambda l, i: (l, i, 0)),
                  pl.BlockSpec((1, 1, w), lambda l, i: (l, 0, 0))],
        out_specs=pl.BlockSpec((1, MOD_ROWS, w), lambda l, i: (l, 0, 0)),
        out_shape=jax.ShapeDtypeStruct((nl, MOD_ROWS, w), F32),
        compiler_params=_cp(("parallel", "arbitrary")),
    )(c_rows_t, ada_w, ada_b_shard)


def _mod_bwd(c_rows_t, dmod, ada_w):
    nl, d, w = ada_w.shape
    td = _pick(d, (256, 128))
    ctx_row = N_DEV

    def body(ct_ref, dm_ref, w_ref, gw_ref, ds_ref):
        st = _silu(ct_ref[...])
        dm = dm_ref[0]
        acc = st[:, 0:1] * dm[0:1, :]
        for r in range(1, ctx_row + 1):
            acc = acc + st[:, r:r + 1] * dm[r:r + 1, :]
        gw_ref[0] = acc
        ds_ref[0] = jnp.sum(w_ref[0] * dm[ctx_row:ctx_row + 1, :], axis=1, keepdims=True)

    return pl.pallas_call(
        body, name="mod_bwd", grid=(nl, d // td),
        in_specs=[pl.BlockSpec((td, MOD_ROWS), lambda l, i: (i, 0)),
                  pl.BlockSpec((1, MOD_ROWS, w), lambda l, i: (l, 0, 0)),
                  pl.BlockSpec((1, td, w), lambda l, i: (l, i, 0))],
        out_specs=[pl.BlockSpec((1, td, w), lambda l, i: (l, i, 0)),
                   pl.BlockSpec((1, td, 1), lambda l, i: (l, i, 0))],
        out_shape=[jax.ShapeDtypeStruct((nl, d, w), F32), jax.ShapeDtypeStruct((nl, d, 1), F32)],
        compiler_params=_cp(("parallel", "parallel")),
    )(c_rows_t, dmod, ada_w)


def _adaln_fwd(x, g, shift, scale, tr, name, comm=None, out_rows=None, into=None, row0=0):
    r, d = x.shape
    rb0 = row0 // tr
    total = into.shape[0] if into is not None else (out_rows or r)

    def body(x_ref, g_ref, sh_ref, sc_ref, *rest):
        o_ref = rest[-1]
        xv = x_ref[...]
        rs = lax.rsqrt(jnp.mean(xv * xv, axis=-1, keepdims=True) + EPS)
        o_ref[...] = ((xv * rs * g_ref[...]) * (1.0 + sc_ref[...]) + sh_ref[...]).astype(BF16)

    spec = pl.BlockSpec((tr, d), lambda i: (i, 0))
    in_specs, args, aliases = [spec, _vec(d), _vec(d), _vec(d)], [x, g, shift, scale], None
    if into is not None:
        in_specs.append(ANY)
        args.append(into)
        aliases = {4: 0}
    outs = _call(body, name=name, grid=(r // tr,), in_specs=in_specs,
                 out_specs=[pl.BlockSpec((tr, d), lambda i: (rb0 + i, 0))],
                 out_shape=[jax.ShapeDtypeStruct((total, d), BF16)], args=args, sem=("parallel",),
                 comm=comm, aliases=aliases)
    return outs if comm else outs[0]


def _adaln_bwd(xin, dh, row0, g, scale, dg_init, tr, name, dres=None, o_prev=None, gate_prev=None, comm=None):
    r, d = xin.shape
    assert row0 % tr == 0
    rb0 = row0 // tr
    want_dx = dres is not None
    want_prev = o_prev is not None
    assert want_dx or not want_prev

    def body(*refs):
        it = iter(refs)
        x_ref, dh_ref, g_ref, sc_ref, dgi_ref = next(it), next(it), next(it), next(it), next(it)
        dres_ref = next(it) if want_dx else None
        o_ref, gp_ref = (next(it), next(it)) if want_prev else (None, None)
        dx_ref = next(it) if want_dx else None
        do_ref = next(it) if want_prev else None
        dsh_ref, dsc_ref, dg_ref = next(it), next(it), next(it)
        dgp_ref = next(it) if want_prev else None
        i = pl.program_id(0)

        @pl.when(i == 0)
        def _():
            dsh_ref[...] = jnp.zeros_like(dsh_ref)
            dsc_ref[...] = jnp.zeros_like(dsc_ref)
            dg_ref[...] = dgi_ref[...]
            if want_prev:
                dgp_ref[...] = jnp.zeros_like(dgp_ref)

        xv = x_ref[...]
        dhv = dh_ref[...].astype(F32)
        gv = g_ref[...]
        rs = lax.rsqrt(jnp.mean(xv * xv, axis=-1, keepdims=True) + EPS)
        xn = xv * rs
        dsh_ref[...] += jnp.sum(dhv, axis=0, keepdims=True)
        dsc_ref[...] += jnp.sum(dhv * (xn * gv), axis=0, keepdims=True)
        dr = dhv * (1.0 + sc_ref[...])
        dg_ref[...] += jnp.sum(dr * xn, axis=0, keepdims=True)
        if want_dx:
            gy = dr * gv
            dx = dres_ref[...] + rs * (gy - xn * jnp.mean(gy * xn, axis=-1, keepdims=True))
            dx_ref[...] = dx
            if want_prev:
                do_ref[...] = (gp_ref[...] * dx).astype(BF16)
                dgp_ref[...] += jnp.sum(dx * o_ref[...].astype(F32), axis=0, keepdims=True)

    row = pl.BlockSpec((tr, d), lambda i: (i, 0))
    in_specs = [row, pl.BlockSpec((tr, d), lambda i: (rb0 + i, 0)), _vec(d), _vec(d), _vec(d)]
    args = [xin, dh, g, scale, dg_init]
    out_specs, out_shape, names = [], [], []
    if want_dx:
        in_specs.append(row)
        args.append(dres)
    if want_prev:
        in_specs += [row, _vec(d)]
        args += [o_prev, gate_prev]
    if want_dx:
        out_specs.append(row)
        out_shape.append(jax.ShapeDtypeStruct((r, d), F32))
        names.append("dx")
    if want_prev:
        out_specs.append(row)
        out_shape.append(jax.ShapeDtypeStruct((r, d), BF16))
        names.append("do_prev")
    for nm in ("dshift", "dscale", "dg") + (("dgate_prev",) if want_prev else ()):
        out_specs.append(_vec(d))
        out_shape.append(jax.ShapeDtypeStruct((1, d), F32))
        names.append(nm)
    outs = _call(body, name=name, grid=(r // tr,), in_specs=in_specs, out_specs=out_specs, out_shape=out_shape,
                 args=args, sem=("arbitrary",), comm=comm)
    res = dict(zip(names, outs))
    if comm:
        res["carried"] = outs[len(names):]
    return res


def _mm(a, b, *, name, tm, tn, tk, ta=False, tb=False, out_dtype=F32, res=None, gate=None, comm=None):
    if ta:
        kd, m = a.shape
    else:
        m, kd = a.shape
    if tb:
        n, kd2 = b.shape
    else:
        kd2, n = b.shape
    assert kd == kd2 and m % tm == 0 and n % tn == 0 and kd % tk == 0, (a.shape, b.shape, tm, tn, tk)
    ni, nj, nk = m // tm, n // tn, kd // tk
    dn = (((0 if ta else 1,), (1 if tb else 0,)), ((), ()))
    with_res = res is not None
    n_in = 4 if with_res else 2
    n_out = 2 if with_res else 1
    n_cin = len(comm.ins) if comm else 0
    n_cout = len(comm.out_shapes) if comm else 0

    def body(*refs):
        in_refs = refs[:n_in + n_cin]
        out_refs = refs[n_in + n_cin:n_in + n_cin + n_out + n_cout]
        scratch = refs[n_in + n_cin + n_out + n_cout:]
        a_ref, b_ref = in_refs[0], in_refs[1]
        o_ref = out_refs[0]
        i, j, k = pl.program_id(0), pl.program_id(1), pl.program_id(2)
        if comm:
            carried = comm.split(in_refs, out_refs, scratch)

            @pl.when(jnp.logical_and(jnp.logical_and(i == 0, j == 0), k == 0))
            def _():
                comm.start(*carried)

        def emit(acc):
            o_ref[...] = acc.astype(o_ref.dtype)
            if with_res:
                out_refs[1][...] = in_refs[2][...] + in_refs[3][...] * acc

        if nk == 1:
            emit(lax.dot_general(a_ref[...], b_ref[...], dn, preferred_element_type=F32))
        else:
            acc_ref = scratch[0]

            @pl.when(k == 0)
            def _():
                acc_ref[...] = jnp.zeros_like(acc_ref)

            acc_ref[...] += lax.dot_general(a_ref[...], b_ref[...], dn, preferred_element_type=F32)

            @pl.when(k == nk - 1)
            def _():
                emit(acc_ref[...])

        if comm:
            @pl.when(jnp.logical_and(jnp.logical_and(i == ni - 1, j == nj - 1), k == nk - 1))
            def _():
                comm.finish(*carried)

    a_spec = pl.BlockSpec((tk, tm), lambda i, j, k: (k, i)) if ta else pl.BlockSpec((tm, tk), lambda i, j, k: (i, k))
    b_spec = pl.BlockSpec((tn, tk), lambda i, j, k: (j, k)) if tb else pl.BlockSpec((tk, tn), lambda i, j, k: (k, j))
    o_spec = pl.BlockSpec((tm, tn), lambda i, j, k: (i, j))
    in_specs, args = [a_spec, b_spec], [a, b]
    out_specs, out_shape = [o_spec], [jax.ShapeDtypeStruct((m, n), out_dtype)]
    if with_res:
        in_specs += [o_spec, pl.BlockSpec((1, tn), lambda i, j, k: (0, j))]
        args += [res, gate]
        out_specs.append(o_spec)
        out_shape.append(jax.ShapeDtypeStruct((m, n), F32))
    scratch_shapes = [pltpu.VMEM((tm, tn), F32)] if nk > 1 else []
    aliases = {}
    sem = ("parallel", "parallel", "arbitrary")
    if comm:
        in_specs += [ANY] * n_cin
        args += comm.ins
        out_specs += [ANY] * n_cout
        out_shape += comm.out_shapes
        scratch_shapes += comm.scratch()
        aliases = {n_in + s: n_out + d for s, d in comm.aliases.items()}
        sem = ("arbitrary", "arbitrary", "arbitrary")
    outs = pl.pallas_call(body, name=name, grid=(ni, nj, nk), in_specs=in_specs, out_specs=out_specs,
                          out_shape=out_shape, scratch_shapes=scratch_shapes, input_output_aliases=aliases,
                          compiler_params=_cp(sem))(*args)
    return outs if (with_res or comm) else outs[0]


def _swap_pairs(x):
    lane = lax.broadcasted_iota(jnp.int32, x.shape, 1)
    return jnp.where(lane % 2 == 0, pltpu.roll(x, HEAD_DIM - 1, 1), pltpu.roll(x, 1, 1))


def _qk_prep(p0, cos_i, sin_s, q_norm, k_norm, dims, tr):
    rows = p0.shape[0]
    kvw, aw, cb = dims["kv_w"], dims["attn_w"], dims["cb"]
    nkv, nq = kvw // HEAD_DIM, aw // HEAD_DIM
    scale = HEAD_DIM ** -0.5
    n_kv_specs, n_q_specs = (2 * kvw) // cb, aw // cb

    def body(*refs):
        kv_refs = refs[:n_kv_specs]
        q_refs = refs[n_kv_specs:n_kv_specs + n_q_specs]
        cos_ref, sin_ref, qn_ref, kn_ref, qo_ref, ko_ref, vo_ref = refs[n_kv_specs + n_q_specs:]
        kv = _cat(kv_refs)
        qv = _cat(q_refs)
        cs, sn = cos_ref[...], sin_ref[...]

        def norm_rope(xh, gvec):
            rs = lax.rsqrt(jnp.mean(xh * xh, axis=-1, keepdims=True) + EPS)
            xn = xh * rs * gvec
            return xn * cs + _swap_pairs(xn) * sn

        for h in range(nkv):
            sl = slice(h * HEAD_DIM, (h + 1) * HEAD_DIM)
            ko_ref[:, sl] = norm_rope(kv[:, sl], kn_ref[...]).astype(BF16)
        vo_ref[...] = kv[:, kvw:].astype(BF16)
        for h in range(nq):
            sl = slice(h * HEAD_DIM, (h + 1) * HEAD_DIM)
            qo_ref[:, sl] = (norm_rope(qv[:, sl], qn_ref[...]) * scale).astype(BF16)

    rm = lambda i: i
    in_specs = (_col_specs(tr, 0, 2 * kvw, cb, rm) + _col_specs(tr, 2 * kvw, aw, cb, rm)
                + [pl.BlockSpec((tr, HEAD_DIM), lambda i: (i, 0))] * 2 + [_vec(HEAD_DIM)] * 2)
    args = [p0] * (n_kv_specs + n_q_specs) + [cos_i, sin_s, q_norm, k_norm]
    return pl.pallas_call(
        body, name="qk_prep", grid=(rows // tr,), in_specs=in_specs,
        out_specs=[pl.BlockSpec((tr, aw), lambda i: (i, 0)), pl.BlockSpec((tr, kvw), lambda i: (i, 0)),
                   pl.BlockSpec((tr, kvw), lambda i: (i, 0))],
        out_shape=[jax.ShapeDtypeStruct((rows, aw), BF16), jax.ShapeDtypeStruct((rows, kvw), BF16),
                   jax.ShapeDtypeStruct((rows, kvw), BF16)],
        compiler_params=_cp(("parallel",)))(*args)


def _qk_prep_bwd(dq_hat, dk_hat, dv, p0, cos_i, sin_s, q_norm, k_norm, dims, tr, n_lat):
    rows = p0.shape[0]
    kvw, aw, cb = dims["kv_w"], dims["attn_w"], dims["cb"]
    nkv, nq = kvw // HEAD_DIM, aw // HEAD_DIM
    scale = HEAD_DIM ** -0.5
    n_kv_specs, n_q_specs = (2 * kvw) // cb, aw // cb
    lat_tiles = n_lat // tr

    def body(*refs):
        kv_refs = refs[:n_kv_specs]
        q_refs = refs[n_kv_specs:n_kv_specs + n_q_specs]
        (dq_ref, dk_ref, dv_ref, cos_ref, sin_ref, qn_ref, kn_ref,
         out_ref, dqn_ref, dkn_ref) = refs[n_kv_specs + n_q_specs:]
        i = pl.program_id(0)

        @pl.when(i == 0)
        def _():
            dqn_ref[...] = jnp.zeros_like(dqn_ref)
            dkn_ref[...] = jnp.zeros_like(dkn_ref)

        kv = _cat(kv_refs)
        qv = _cat(q_refs)
        cs, sn = cos_ref[...], sin_ref[...]
        is_lat = (i < lat_tiles).astype(F32)

        def head_bwd(xh, dhat, gvec):
            dn = dhat * cs + _swap_pairs(dhat * sn)
            rs = lax.rsqrt(jnp.mean(xh * xh, axis=-1, keepdims=True) + EPS)
            xn = xh * rs
            gy = dn * gvec
            dx = rs * (gy - xn * jnp.mean(gy * xn, axis=-1, keepdims=True))
            return dx, jnp.sum(dn * xn, axis=0, keepdims=True)

        dkn = jnp.zeros((1, HEAD_DIM), F32)
        for h in range(nkv):
            sl = slice(h * HEAD_DIM, (h + 1) * HEAD_DIM)
            dx, dgv = head_bwd(kv[:, sl], dk_ref[:, sl], kn_ref[...])
            out_ref[:, sl] = dx.astype(BF16)
            dkn = dkn + dgv
        dkn_ref[...] += dkn
        out_ref[:, kvw:2 * kvw] = dv_ref[...].astype(BF16)
        dqn = jnp.zeros((1, HEAD_DIM), F32)
        for h in range(nq):
            sl = slice(h * HEAD_DIM, (h + 1) * HEAD_DIM)
            dx, dgv = head_bwd(qv[:, sl], dq_ref[:, sl] * (scale * is_lat), qn_ref[...])
            out_ref[:, 2 * kvw + h * HEAD_DIM:2 * kvw + (h + 1) * HEAD_DIM] = dx.astype(BF16)
            dqn = dqn + dgv
        dqn_ref[...] += dqn

    rm = lambda i: i
    wout = 2 * kvw + aw
    in_specs = (_col_specs(tr, 0, 2 * kvw, cb, rm) + _col_specs(tr, 2 * kvw, aw, cb, rm)
                + [pl.BlockSpec((tr, aw), lambda i: (jnp.minimum(i, lat_tiles - 1), 0)),
                   pl.BlockSpec((tr, kvw), lambda i: (i, 0)), pl.BlockSpec((tr, kvw), lambda i: (i, 0)),
                   pl.BlockSpec((tr, HEAD_DIM), lambda i: (i, 0)), pl.BlockSpec((tr, HEAD_DIM), lambda i: (i, 0)),
                   _vec(HEAD_DIM), _vec(HEAD_DIM)])
    args = [p0] * (n_kv_specs + n_q_specs) + [dq_hat, dk_hat, dv, cos_i, sin_s, q_norm, k_norm]
    return pl.pallas_call(
        body, name="qk_prep_bwd", grid=(rows // tr,), in_specs=in_specs,
        out_specs=[pl.BlockSpec((tr, wout), lambda i: (i, 0)), _vec(HEAD_DIM), _vec(HEAD_DIM)],
        out_shape=[jax.ShapeDtypeStruct((rows, wout), BF16), jax.ShapeDtypeStruct((1, HEAD_DIM), F32),
                   jax.ShapeDtypeStruct((1, HEAD_DIM), F32)],
        compiler_params=_cp(("arbitrary",)))(*args)


def _stack_heads(x, g):
    return jnp.concatenate([x[:, h * HEAD_DIM:(h + 1) * HEAD_DIM] for h in range(g)], axis=0)


def _flash_fwd(q_hat, k_all, v_all, n_lat, dims, tq, tk, comm=None):
    kvw, aw = dims["kv_w"], dims["attn_w"]
    nkv = kvw // HEAD_DIM
    g = aw // kvw
    gw = g * HEAD_DIM
    n_keys = k_all.shape[0]
    ni, nj = n_lat // tq, n_keys // tk
    dn_nt = (((1,), (1,)), ((), ()))

    def body(q_ref, k_ref, v_ref, o_ref, lse_ref):
        qs = _stack_heads(q_ref[...], g)
        m = jnp.full((g * tq, 1), -1e30, F32)
        l = jnp.zeros((g * tq, 1), F32)
        acc = jnp.zeros((g * tq, HEAD_DIM), F32)
        for j in range(nj):
            kb = k_ref[pl.ds(j * tk, tk), :]
            vb = v_ref[pl.ds(j * tk, tk), :]
            s = lax.dot_general(qs, kb, dn_nt, preferred_element_type=F32)
            m_new = jnp.maximum(m, jnp.max(s, axis=-1, keepdims=True))
            alpha = jnp.exp(m - m_new)
            p = jnp.exp(s - m_new)
            l = alpha * l + jnp.sum(p, axis=-1, keepdims=True)
            acc = alpha * acc + jnp.dot(p.astype(BF16), vb, preferred_element_type=F32)
            m = m_new
        o = acc / l
        for h in range(g):
            o_ref[:, h * HEAD_DIM:(h + 1) * HEAD_DIM] = o[h * tq:(h + 1) * tq]
        lse_ref[...] = m + jnp.log(l)

    return _call(
        body, name="flash_fwd", grid=(nkv, ni),
        in_specs=[pl.BlockSpec((tq, gw), lambda h, i: (i, h)),
                  pl.BlockSpec((n_keys, HEAD_DIM), lambda h, i: (0, h)),
                  pl.BlockSpec((n_keys, HEAD_DIM), lambda h, i: (0, h))],
        out_specs=[pl.BlockSpec((tq, gw), lambda h, i: (i, h)),
                   pl.BlockSpec((g * tq, 1), lambda h, i: (h * ni + i, 0))],
        out_shape=[jax.ShapeDtypeStruct((n_lat, aw), F32), jax.ShapeDtypeStruct((nkv * ni * g * tq, 1), F32)],
        args=(q_hat, k_all, v_all), sem=("parallel", "parallel"), comm=comm)


def _flash_bwd(q_hat, k_all, v_all, do, o, lse, n_lat, dims, tq, tk):
    kvw, aw = dims["kv_w"], dims["attn_w"]
    nkv = kvw // HEAD_DIM
    g = aw // kvw
    gw = g * HEAD_DIM
    n_keys = k_all.shape[0]
    ni, nj = n_lat // tq, n_keys // tk
    dn_nt = (((1,), (1,)), ((), ()))
    dn_tn = (((0,), (0,)), ((), ()))

    def body(q_ref, k_ref, v_ref, do_ref, o_ref, lse_ref, dq_ref, dk_ref, dv_ref):
        i = pl.program_id(1)

        @pl.when(i == 0)
        def _():
            dk_ref[...] = jnp.zeros_like(dk_ref)
            dv_ref[...] = jnp.zeros_like(dv_ref)

        dos = _stack_heads(do_ref[...], g)
        qs = _stack_heads(q_ref[...], g)
        delta = jnp.sum(dos.astype(F32) * _stack_heads(o_ref[...], g), axis=-1, keepdims=True)
        lse_v = lse_ref[...]
        dq = jnp.zeros((g * tq, HEAD_DIM), F32)
        for j in range(nj):
            rows = pl.ds(j * tk, tk)
            kb, vb = k_ref[rows, :], v_ref[rows, :]
            s = lax.dot_general(qs, kb, dn_nt, preferred_element_type=F32)
            p = jnp.exp(s - lse_v)
            dp = lax.dot_general(dos, vb, dn_nt, preferred_element_type=F32)
            ds = (p * (dp - delta)).astype(BF16)
            dv_ref[rows, :] += lax.dot_general(p.astype(BF16), dos, dn_tn, preferred_element_type=F32)
            dk_ref[rows, :] += lax.dot_general(ds, qs, dn_tn, preferred_element_type=F32)
            dq = dq + jnp.dot(ds, kb, preferred_element_type=F32)
        for h in range(g):
            dq_ref[:, h * HEAD_DIM:(h + 1) * HEAD_DIM] = dq[h * tq:(h + 1) * tq]

    qspec = pl.BlockSpec((tq, gw), lambda h, i: (i, h))
    full_k = pl.BlockSpec((n_keys, HEAD_DIM), lambda h, i: (0, h))
    return pl.pallas_call(
        body, name="flash_bwd", grid=(nkv, ni),
        in_specs=[qspec, full_k, full_k, qspec, qspec, pl.BlockSpec((g * tq, 1), lambda h, i: (h * ni + i, 0))],
        out_specs=[qspec, full_k, full_k],
        out_shape=[jax.ShapeDtypeStruct((n_lat, aw), F32), jax.ShapeDtypeStruct((n_keys, kvw), F32),
                   jax.ShapeDtypeStruct((n_keys, kvw), F32)],
        compiler_params=_cp(("parallel", "arbitrary")))(q_hat, k_all, v_all, do, o, lse)


def _shifted_copies(pad_ref, sh_ref, tr):
    rows = tr + 2 * CONV_HALO - SUBLANES
    for r in range(SUBLANES):
        sh_ref[r] = pad_ref[pl.ds(r, rows), :]


def _stencil(sh_ref, w_ref, out_ref, offsets, tr, cc, init_ref=None):
    for c0 in range(0, cc, LANES):
        lanes = pl.ds(c0, LANES)
        wv = [jnp.broadcast_to(w_ref[pl.ds(k, 1), lanes], (STENCIL_ROWS, LANES)) for k in range(len(offsets))]
        if init_ref is None:
            init = jnp.zeros((STENCIL_ROWS, LANES), F32)
        else:
            init = jnp.broadcast_to(init_ref[:, lanes], (STENCIL_ROWS, LANES))

        def block(rb, carry, lanes=lanes, wv=wv, init=init):
            r0 = rb * STENCIL_ROWS
            parts = [init] + [None] * (STENCIL_CHAINS - 1)
            for k, o in enumerate(offsets):
                rows = pl.ds(SUBLANES * (o // SUBLANES) + r0, STENCIL_ROWS)
                term = sh_ref[o % SUBLANES, rows, lanes] * wv[k]
                q = k % STENCIL_CHAINS
                parts[q] = term if parts[q] is None else parts[q] + term
            acc = parts[0]
            for p_ in parts[1:]:
                acc = acc + p_
            out_ref[pl.ds(r0, STENCIL_ROWS), lanes] = acc
            return carry

        for rb in range(tr // STENCIL_ROWS):
            block(rb, 0)


def _stencil_weight_grad(sh_ref, d_ref, d_row0, dw_ref, offsets, tr, cc):
    rows_per = REDUCE_ROWS
    for c0 in range(0, cc, LANES):
        lanes = pl.ds(c0, LANES)

        def block(rb, accs, lanes=lanes):
            r0 = rb * rows_per
            dblk = d_ref[pl.ds(d_row0 + r0, rows_per), lanes]
            out = []
            for k, o in enumerate(offsets):
                prod = dblk * sh_ref[o % SUBLANES, pl.ds(SUBLANES * (o // SUBLANES) + r0, rows_per), lanes]
                part = prod[0:SUBLANES]
                for q in range(1, rows_per // SUBLANES):
                    part = part + prod[q * SUBLANES:(q + 1) * SUBLANES]
                out.append(accs[k] + part)
            return tuple(out)

        zero = jnp.zeros((SUBLANES, LANES), F32)
        accs = tuple(zero for _ in offsets)
        for rb in range(tr // rows_per):
            accs = block(rb, accs)
        for k in range(len(offsets)):
            dw_ref[pl.ds(k, 1), lanes] += jnp.sum(accs[k], axis=0, keepdims=True)


def _halo_maps(tr, n_tiles):
    per = tr // CONV_HALO
    prev = lambda i: jnp.maximum(i * per - 1, 0)
    nxt = lambda i: (i + 1) * per
    return prev, nxt


def _mix_fwd(attn, p0, dw_w, dw_b, ln_g, ln_b, dims, tr):
    n_lat, aw = attn.shape
    cc, cb, cw = dims["conv_ch"], dims["cb"], dims["conv_w"]
    off = dims["off"]
    n_tiles = n_lat // tr
    pad = cw // 2
    na, nc = aw // cb, cc // cb
    prev_map, next_map = _halo_maps(tr, n_tiles)

    def body(*refs):
        it = iter(refs)
        attn_ref = next(it)
        za = [next(it) for _ in range(na)]
        a_c = [next(it) for _ in range(nc)]
        b_c = [next(it) for _ in range(nc)]
        zb = [next(it) for _ in range(nc)]
        a_p = [next(it) for _ in range(nc)]
        b_p = [next(it) for _ in range(nc)]
        a_n = [next(it) for _ in range(nc)]
        b_n = [next(it) for _ in range(nc)]
        w_ref, db_ref, g_ref, bb_ref, mix_ref, yc_ref, ypad, ysh = (next(it) for _ in range(8))
        i = pl.program_id(0)
        ypad[pl.ds(0, CONV_HALO), :] = _cat(a_p) * _sigmoid(_cat(b_p)) * (i > 0).astype(F32)
        ypad[pl.ds(CONV_HALO, tr), :] = _cat(a_c) * _sigmoid(_cat(b_c))
        ypad[pl.ds(CONV_HALO + tr, CONV_HALO), :] = _cat(a_n) * _sigmoid(_cat(b_n)) * (i < n_tiles - 1).astype(F32)
        _shifted_copies(ypad, ysh, tr)
        _stencil(ysh, w_ref, yc_ref, [CONV_HALO - pad + k for k in range(cw)], tr, cc, init_ref=db_ref)
        acc = yc_ref[...]
        mu = jnp.mean(acc, axis=-1, keepdims=True)
        xc = acc - mu
        rs = lax.rsqrt(jnp.mean(xc * xc, axis=-1, keepdims=True) + EPS)
        nv = xc * rs * g_ref[...] + bb_ref[...]
        mix_ref[:, :aw] = (attn_ref[...] * _silu(_cat(za))).astype(BF16)
        mix_ref[:, aw:] = (_silu(nv) * _silu(_cat(zb))).astype(BF16)

    rm = lambda i: i
    in_specs = ([pl.BlockSpec((tr, aw), lambda i: (i, 0))]
                + _col_specs(tr, off["za"], aw, cb, rm) + _col_specs(tr, off["a"], cc, cb, rm)
                + _col_specs(tr, off["b"], cc, cb, rm) + _col_specs(tr, off["zb"], cc, cb, rm)
                + _col_specs(CONV_HALO, off["a"], cc, cb, prev_map) + _col_specs(CONV_HALO, off["b"], cc, cb, prev_map)
                + _col_specs(CONV_HALO, off["a"], cc, cb, next_map) + _col_specs(CONV_HALO, off["b"], cc, cb, next_map)
                + [pl.BlockSpec(dw_w.shape, lambda i: (0, 0)), _vec(cc), _vec(cc), _vec(cc)])
    args = [attn] + [p0] * (na + 7 * nc) + [dw_w, dw_b, ln_g, ln_b]
    return pl.pallas_call(
        body, name="mix_fwd", grid=(n_tiles,), in_specs=in_specs,
        out_specs=[pl.BlockSpec((tr, aw + cc), lambda i: (i, 0)), pl.BlockSpec((tr, cc), lambda i: (i, 0))],
        out_shape=[jax.ShapeDtypeStruct((n_lat, aw + cc), BF16), jax.ShapeDtypeStruct((n_lat, cc), F32)],
        scratch_shapes=[pltpu.VMEM((tr + 2 * CONV_HALO, cc), F32),
                        pltpu.VMEM((SUBLANES, tr + 2 * CONV_HALO - SUBLANES, cc), F32)],
        compiler_params=_cp(("parallel",)))(*args)


def _mix_bwd_pointwise(dmix, attn, p0, yc, ln_g, ln_b, dims, tr, n_ext):
    n_lat, aw = attn.shape
    cc, cb, off = dims["conv_ch"], dims["cb"], dims["off"]
    na, nc = aw // cb, cc // cb
    lat_tiles = n_lat // tr

    def body(*refs):
        it = iter(refs)
        dmix_ref, attn_ref = next(it), next(it)
        za = [next(it) for _ in range(na)]
        zb = [next(it) for _ in range(nc)]
        yc_ref, g_ref, bb_ref = next(it), next(it), next(it)
        dattn_ref, dza_ref, dzb_ref, dyc_ref, dg_ref, dbb_ref, ddb_ref = (next(it) for _ in range(7))
        i = pl.program_id(0)

        @pl.when(i == 0)
        def _():
            dg_ref[...] = jnp.zeros_like(dg_ref)
            dbb_ref[...] = jnp.zeros_like(dbb_ref)
            ddb_ref[...] = jnp.zeros_like(ddb_ref)

        lat = (i < lat_tiles).astype(F32)
        dm = dmix_ref[...].astype(F32)
        dma, dmb = dm[:, :aw], dm[:, aw:]
        zav, zbv = _cat(za), _cat(zb)
        sa, dsa = _silu_and_grad(zav)
        sb, dsb = _silu_and_grad(zbv)
        dattn_ref[...] = (dma * sa).astype(BF16)
        dza_ref[...] = (dma * attn_ref[...] * dsa * lat).astype(BF16)
        ycv = yc_ref[...]
        mu = jnp.mean(ycv, axis=-1, keepdims=True)
        xc = ycv - mu
        rs = lax.rsqrt(jnp.mean(xc * xc, axis=-1, keepdims=True) + EPS)
        xh = xc * rs
        nv = xh * g_ref[...] + bb_ref[...]
        sn, dsn = _silu_and_grad(nv)
        dzb_ref[...] = (dmb * sn * dsb * lat).astype(BF16)
        dn = dmb * sb * dsn
        dg_ref[...] += lat * jnp.sum(dn * xh, axis=0, keepdims=True)
        dbb_ref[...] += lat * jnp.sum(dn, axis=0, keepdims=True)
        dxh = dn * g_ref[...]
        dyc = rs * (dxh - jnp.mean(dxh, axis=-1, keepdims=True) - xh * jnp.mean(dxh * xh, axis=-1, keepdims=True))
        dyc_ref[...] = dyc
        ddb_ref[...] += lat * jnp.sum(dyc, axis=0, keepdims=True)

    rm = lambda i: jnp.minimum(i, lat_tiles - 1)
    row = lambda w: pl.BlockSpec((tr, w), lambda i: (rm(i), 0))
    ext = lambda w: pl.BlockSpec((tr, w), lambda i: (i, 0))
    in_specs = ([row(aw + cc), row(aw)] + _col_specs(tr, off["za"], aw, cb, rm)
                + _col_specs(tr, off["zb"], cc, cb, rm) + [row(cc), _vec(cc), _vec(cc)])
    args = [dmix, attn] + [p0] * (na + nc) + [yc, ln_g, ln_b]
    return pl.pallas_call(
        body, name="mix_bwd_pointwise", grid=(n_ext // tr,), in_specs=in_specs,
        out_specs=[row(aw), ext(aw), ext(cc), row(cc), _vec(cc), _vec(cc), _vec(cc)],
        out_shape=[jax.ShapeDtypeStruct((n_lat, aw), BF16), jax.ShapeDtypeStruct((n_ext, aw), BF16),
                   jax.ShapeDtypeStruct((n_ext, cc), BF16), jax.ShapeDtypeStruct((n_lat, cc), F32)]
                  + [jax.ShapeDtypeStruct((1, cc), F32)] * 3,
        compiler_params=_cp(("arbitrary",)))(*args)


def _conv_bwd(dyc, p0, dw_w, dims, tr, n_ext):
    n_lat, cc = dyc.shape
    cb, cw, off = dims["cb"], dims["conv_w"], dims["off"]
    nc = cc // cb
    n_tiles = n_lat // tr
    pad = cw // 2
    prev_lat, next_lat = _halo_maps(tr, n_tiles)
    rm = lambda i: jnp.minimum(i, n_tiles - 1)
    prev_map = lambda i: prev_lat(rm(i))
    next_map = lambda i: next_lat(rm(i))

    def body(*refs):
        it = iter(refs)
        a_c = [next(it) for _ in range(nc)]
        b_c = [next(it) for _ in range(nc)]
        a_p = [next(it) for _ in range(nc)]
        b_p = [next(it) for _ in range(nc)]
        a_n = [next(it) for _ in range(nc)]
        b_n = [next(it) for _ in range(nc)]
        d_c, d_p, d_n, w_ref, dab_ref, dw_ref, ypad, dpad, ysh, dsh, dy_scr = (next(it) for _ in range(11))
        i = pl.program_id(0)

        @pl.when(i == 0)
        def _():
            dw_ref[...] = jnp.zeros_like(dw_ref)

        @pl.when(i >= n_tiles)
        def _():
            dab_ref[...] = jnp.zeros_like(dab_ref)

        @pl.when(i < n_tiles)
        def _():
            first, last = (i > 0).astype(F32), (i < n_tiles - 1).astype(F32)
            av, bv = _cat(a_c), _cat(b_c)
            sg = _sigmoid(bv)
            ypad[pl.ds(0, CONV_HALO), :] = _cat(a_p) * _sigmoid(_cat(b_p)) * first
            ypad[pl.ds(CONV_HALO, tr), :] = av * sg
            ypad[pl.ds(CONV_HALO + tr, CONV_HALO), :] = _cat(a_n) * _sigmoid(_cat(b_n)) * last
            dpad[pl.ds(0, CONV_HALO), :] = d_p[...] * first
            dpad[pl.ds(CONV_HALO, tr), :] = d_c[...]
            dpad[pl.ds(CONV_HALO + tr, CONV_HALO), :] = d_n[...] * last
            _shifted_copies(ypad, ysh, tr)
            _shifted_copies(dpad, dsh, tr)
            _stencil(dsh, w_ref, dy_scr, [CONV_HALO + pad - k for k in range(cw)], tr, cc)
            _stencil_weight_grad(ysh, dpad, CONV_HALO, dw_ref, [CONV_HALO - pad + k for k in range(cw)], tr, cc)
            dy = dy_scr[...]
            dab_ref[:, :cc] = (dy * sg).astype(BF16)
            dab_ref[:, cc:] = (dy * av * sg * (1.0 - sg)).astype(BF16)

    in_specs = (_col_specs(tr, off["a"], cc, cb, rm) + _col_specs(tr, off["b"], cc, cb, rm)
                + _col_specs(CONV_HALO, off["a"], cc, cb, prev_map) + _col_specs(CONV_HALO, off["b"], cc, cb, prev_map)
                + _col_specs(CONV_HALO, off["a"], cc, cb, next_map) + _col_specs(CONV_HALO, off["b"], cc, cb, next_map)
                + [pl.BlockSpec((tr, cc), lambda i: (rm(i), 0)),
                   pl.BlockSpec((CONV_HALO, cc), lambda i: (prev_map(i), 0)),
                   pl.BlockSpec((CONV_HALO, cc), lambda i: (jnp.minimum(next_map(i), n_lat // CONV_HALO - 1), 0)),
                   pl.BlockSpec(dw_w.shape, lambda i: (0, 0))])
    args = [p0] * (6 * nc) + [dyc, dyc, dyc, dw_w]
    return pl.pallas_call(
        body, name="conv_bwd", grid=(n_ext // tr,), in_specs=in_specs,
        out_specs=[pl.BlockSpec((tr, 2 * cc), lambda i: (i, 0)), pl.BlockSpec(dw_w.shape, lambda i: (0, 0))],
        out_shape=[jax.ShapeDtypeStruct((n_ext, 2 * cc), BF16), jax.ShapeDtypeStruct(dw_w.shape, F32)],
        scratch_shapes=[pltpu.VMEM((tr + 2 * CONV_HALO, cc), F32), pltpu.VMEM((tr + 2 * CONV_HALO, cc), F32),
                        pltpu.VMEM((SUBLANES, tr + 2 * CONV_HALO - SUBLANES, cc), F32),
                        pltpu.VMEM((SUBLANES, tr + 2 * CONV_HALO - SUBLANES, cc), F32),
                        pltpu.VMEM((tr, cc), F32)],
        compiler_params=_cp(("arbitrary",)))(*args)


def _sgu_parts(u, v, ln_g, ln_b):
    mu = jnp.mean(v, axis=-1, keepdims=True)
    xc = v - mu
    rs = lax.rsqrt(jnp.mean(xc * xc, axis=-1, keepdims=True) + EPS)
    xh = xc * rs
    return u, xh, rs, xh * ln_g + ln_b


def _sgu_fwd(p1, ln_g, ln_b, ws, bs_t, tr):
    n_lat, w3 = p1.shape
    w = w3 // 3
    ng, ch = ws.shape[0], ws.shape[1]
    gwid = w // ng
    n_ch = tr // ch

    def body(pu_ref, pv_ref, pg_ref, g_ref, b_ref, ws_ref, bs_ref, o_ref):
        u, _, _, vln = _sgu_parts(_gelu(pu_ref[...].astype(F32)), _gelu(pv_ref[...].astype(F32)),
                                  g_ref[...], b_ref[...])
        gate = _silu(pg_ref[...].astype(F32))
        vb = vln.astype(BF16)
        for c in range(n_ch):
            rs_ = slice(c * ch, (c + 1) * ch)
            for gi in range(ng):
                cs_ = slice(gi * gwid, (gi + 1) * gwid)
                mixed = jnp.dot(ws_ref[gi], vb[rs_, cs_], preferred_element_type=F32) + bs_ref[:, gi:gi + 1]
                o_ref[rs_, cs_] = (u[rs_, cs_] * mixed * gate[rs_, cs_]).astype(BF16)

    col = lambda t: pl.BlockSpec((tr, w), lambda i, _t=t: (i, _t))
    return pl.pallas_call(
        body, name="sgu_fwd", grid=(n_lat // tr,),
        in_specs=[col(0), col(1), col(2), _vec(w), _vec(w),
                  pl.BlockSpec(ws.shape, lambda i: (0, 0, 0)), pl.BlockSpec(bs_t.shape, lambda i: (0, 0))],
        out_specs=pl.BlockSpec((tr, w), lambda i: (i, 0)),
        out_shape=jax.ShapeDtypeStruct((n_lat, w), BF16),
        compiler_params=_cp(("parallel",)))(p1, p1, p1, ln_g, ln_b, ws, bs_t)


def _sgu_bwd(dm, p1, ln_g, ln_b, ws, ws_t, bs_t, tr):
    n_lat, w3 = p1.shape
    w = w3 // 3
    ng, ch = ws.shape[0], ws.shape[1]
    gwid = w // ng
    n_ch = tr // ch
    dn_nt = (((1,), (1,)), ((), ()))

    def body(dm_ref, pu_ref, pv_ref, pg_ref, g_ref, b_ref, ws_ref, wst_ref, bs_ref,
             dp_ref, dws_ref, dbs_ref, dg_ref, dbb_ref, dvln_scr):
        i = pl.program_id(0)

        @pl.when(i == 0)
        def _():
            dws_ref[...] = jnp.zeros_like(dws_ref)
            dbs_ref[...] = jnp.zeros_like(dbs_ref)
            dg_ref[...] = jnp.zeros_like(dg_ref)
            dbb_ref[...] = jnp.zeros_like(dbb_ref)

        puv, pvv, pgv = pu_ref[...].astype(F32), pv_ref[...].astype(F32), pg_ref[...].astype(F32)
        gu, dgu = _gelu_and_grad(puv)
        gv, dgv = _gelu_and_grad(pvv)
        gate, dgate = _silu_and_grad(pgv)
        u, xh, rs, vln = _sgu_parts(gu, gv, g_ref[...], b_ref[...])
        dmv = dm_ref[...].astype(F32)
        vb = vln.astype(BF16)
        dmu = dmv * u
        du_pre = dmv * gate * dgu
        dg_pre = dmu * dgate
        dmix_all = dmu * gate
        dbs_cols = [jnp.zeros((ch, 1), F32) for _ in range(ng)]
        for c in range(n_ch):
            rs_ = slice(c * ch, (c + 1) * ch)
            for gi in range(ng):
                cs_ = slice(gi * gwid, (gi + 1) * gwid)
                mixed = jnp.dot(ws_ref[gi], vb[rs_, cs_], preferred_element_type=F32) + bs_ref[:, gi:gi + 1]
                dmixed = dmix_all[rs_, cs_]
                dmb = dmixed.astype(BF16)
                dp_ref[rs_, gi * gwid:(gi + 1) * gwid] = (du_pre[rs_, cs_] * mixed).astype(BF16)
                dp_ref[rs_, 2 * w + gi * gwid:2 * w + (gi + 1) * gwid] = (dg_pre[rs_, cs_] * mixed).astype(BF16)
                dvln_scr[rs_, cs_] = jnp.dot(wst_ref[gi], dmb, preferred_element_type=F32)
                dws_ref[gi] += lax.dot_general(dmb, vb[rs_, cs_], dn_nt, preferred_element_type=F32)
                dbs_cols[gi] = dbs_cols[gi] + jnp.sum(dmixed, axis=-1, keepdims=True)
        dbs_ref[...] += jnp.concatenate(dbs_cols, axis=1)
        dvln = dvln_scr[...]
        dg_ref[...] += jnp.sum(dvln * xh, axis=0, keepdims=True)
        dbb_ref[...] += jnp.sum(dvln, axis=0, keepdims=True)
        dxh = dvln * g_ref[...]
        dv = rs * (dxh - jnp.mean(dxh, axis=-1, keepdims=True) - xh * jnp.mean(dxh * xh, axis=-1, keepdims=True))
        dp_ref[:, w:2 * w] = (dv * dgv).astype(BF16)

    col = lambda t: pl.BlockSpec((tr, w), lambda i, _t=t: (i, _t))
    return pl.pallas_call(
        body, name="sgu_bwd", grid=(n_lat // tr,),
        in_specs=[pl.BlockSpec((tr, w), lambda i: (i, 0)), col(0), col(1), col(2), _vec(w), _vec(w),
                  pl.BlockSpec(ws.shape, lambda i: (0, 0, 0)), pl.BlockSpec(ws.shape, lambda i: (0, 0, 0)),
                  pl.BlockSpec(bs_t.shape, lambda i: (0, 0))],
        out_specs=[pl.BlockSpec((tr, w3), lambda i: (i, 0)), pl.BlockSpec(ws.shape, lambda i: (0, 0, 0)),
                   pl.BlockSpec(bs_t.shape, lambda i: (0, 0)), _vec(w), _vec(w)],
        out_shape=[jax.ShapeDtypeStruct((n_lat, w3), BF16), jax.ShapeDtypeStruct(ws.shape, F32),
                   jax.ShapeDtypeStruct(bs_t.shape, F32), jax.ShapeDtypeStruct((1, w), F32),
                   jax.ShapeDtypeStruct((1, w), F32)],
        scratch_shapes=[pltpu.VMEM((tr, w), F32)],
        compiler_params=_cp(("arbitrary",)))(dm, p1, p1, p1, ln_g, ln_b, ws, ws_t, bs_t)


def _final_loss(x2, target, final_g, o_prev, gate_prev, tr):
    n_lat, d = x2.shape

    def body(x_ref, t_ref, g_ref, o_ref, gp_ref, dx_ref, do_ref, ls_ref, dg_ref, dgp_ref):
        i = pl.program_id(0)

        @pl.when(i == 0)
        def _():
            ls_ref[...] = jnp.zeros_like(ls_ref)
            dg_ref[...] = jnp.zeros_like(dg_ref)
            dgp_ref[...] = jnp.zeros_like(dgp_ref)

        xv = x_ref[...]
        gv = g_ref[...]
        rs = lax.rsqrt(jnp.mean(xv * xv, axis=-1, keepdims=True) + EPS)
        xn = xv * rs
        err = xn * gv - t_ref[...]
        ls_ref[...] += jnp.sum(err * err, axis=0, keepdims=True)
        dy = err * (1.0 / d)
        dg_ref[...] += jnp.sum(dy * xn, axis=0, keepdims=True)
        gy = dy * gv
        dx = rs * (gy - xn * jnp.mean(gy * xn, axis=-1, keepdims=True))
        dx_ref[...] = dx
        do_ref[...] = (gp_ref[...] * dx).astype(BF16)
        dgp_ref[...] += jnp.sum(dx * o_ref[...].astype(F32), axis=0, keepdims=True)

    row = pl.BlockSpec((tr, d), lambda i: (i, 0))
    return pl.pallas_call(
        body, name="final_loss", grid=(n_lat // tr,), in_specs=[row, row, _vec(d), row, _vec(d)],
        out_specs=[row, row, _vec(d), _vec(d), _vec(d)],
        out_shape=[jax.ShapeDtypeStruct((n_lat, d), F32), jax.ShapeDtypeStruct((n_lat, d), BF16)]
                  + [jax.ShapeDtypeStruct((1, d), F32)] * 3,
        compiler_params=_cp(("arbitrary",)))(x2, target, final_g, o_prev, gate_prev)


def _pack(arrays):
    flat = jnp.concatenate([a.reshape(-1).astype(F32) for a in arrays])
    n = flat.shape[0]
    rows = -(-n // LANES)
    rows = -(-rows // PACK_ROWS) * PACK_ROWS
    return jnp.pad(flat, (0, rows * LANES - n)).reshape(rows, LANES)


def _unpack(buf, shapes):
    flat = buf.reshape(buf.shape[:-2] + (-1,))
    out, pos = [], 0
    for shp in shapes:
        n = math.prod(shp)
        out.append(flat[..., pos:pos + n].reshape(buf.shape[:-2] + tuple(shp)))
        pos += n
    return out


def _rope_tables(n_lat, n_ctx):
    rows = n_lat // GRID_W
    row = jnp.repeat(jnp.arange(rows, dtype=F32), GRID_W)
    col = jnp.tile(jnp.arange(GRID_W, dtype=F32), rows)
    n_freq, axis_dim = HEAD_DIM // 4, HEAD_DIM // 2
    inv = jnp.power(ROPE_THETA, jnp.arange(n_freq, dtype=F32) * (-2.0 / axis_dim))
    ang = jnp.concatenate([row[:, None] * inv, col[:, None] * inv], axis=-1)
    cos, sin = jnp.cos(ang), jnp.sin(ang)
    cos_i = jnp.repeat(cos, 2, axis=-1)
    sin_s = jnp.stack([-sin, sin], axis=-1).reshape(n_lat, HEAD_DIM)
    cos_i = jnp.concatenate([cos_i, jnp.ones((n_ctx, HEAD_DIM), F32)], axis=0)
    sin_s = jnp.concatenate([sin_s, jnp.zeros((n_ctx, HEAD_DIM), F32)], axis=0)
    return cos_i, sin_s


def kernel(x, c, ctx, c_ctx, ada_w, ada_b, norm_g, ev_w_in, ev_q_norm, ev_k_norm, ev_dw_w, ev_dw_b, ev_ln_g, ev_ln_b, ev_w_out, od_w_in, od_ln_g, od_ln_b, od_ws, od_bs, od_w_out, final_g, loss_target, m_c_ctx, m_ada_w, m_ada_b, m_norm_g, m_ev_w_in, m_ev_q_norm, m_ev_k_norm, m_ev_dw_w, m_ev_dw_b, m_ev_ln_g, m_ev_ln_b, m_ev_w_out, m_od_w_in, m_od_ln_g, m_od_ln_b, m_od_ws, m_od_bs, m_od_w_out, m_final_g, v_c_ctx, v_ada_w, v_ada_b, v_norm_g, v_ev_w_in, v_ev_q_norm, v_ev_k_norm, v_ev_dw_w, v_ev_dw_b, v_ev_ln_g, v_ev_ln_b, v_ev_w_out, v_od_w_in, v_od_ln_g, v_od_ln_b, v_od_ws, v_od_bs, v_od_w_out, v_final_g):
    weights = dict(c_ctx=c_ctx, ada_w=ada_w, ada_b=ada_b, norm_g=norm_g, ev_w_in=ev_w_in, ev_q_norm=ev_q_norm,
                   ev_k_norm=ev_k_norm, ev_dw_w=ev_dw_w, ev_dw_b=ev_dw_b, ev_ln_g=ev_ln_g, ev_ln_b=ev_ln_b,
                   ev_w_out=ev_w_out, od_w_in=od_w_in, od_ln_g=od_ln_g, od_ln_b=od_ln_b, od_ws=od_ws, od_bs=od_bs,
                   od_w_out=od_w_out, final_g=final_g)
    mom_m = dict(c_ctx=m_c_ctx, ada_w=m_ada_w, ada_b=m_ada_b, norm_g=m_norm_g, ev_w_in=m_ev_w_in,
                 ev_q_norm=m_ev_q_norm, ev_k_norm=m_ev_k_norm, ev_dw_w=m_ev_dw_w, ev_dw_b=m_ev_dw_b,
                 ev_ln_g=m_ev_ln_g, ev_ln_b=m_ev_ln_b, ev_w_out=m_ev_w_out, od_w_in=m_od_w_in, od_ln_g=m_od_ln_g,
                 od_ln_b=m_od_ln_b, od_ws=m_od_ws, od_bs=m_od_bs, od_w_out=m_od_w_out, final_g=m_final_g)
    mom_v = dict(c_ctx=v_c_ctx, ada_w=v_ada_w, ada_b=v_ada_b, norm_g=v_norm_g, ev_w_in=v_ev_w_in,
                 ev_q_norm=v_ev_q_norm, ev_k_norm=v_ev_k_norm, ev_dw_w=v_ev_dw_w, ev_dw_b=v_ev_dw_b,
                 ev_ln_g=v_ev_ln_g, ev_ln_b=v_ev_ln_b, ev_w_out=v_ev_w_out, od_w_in=v_od_w_in, od_ln_g=v_od_ln_g,
                 od_ln_b=v_od_ln_b, od_ws=v_od_ws, od_bs=v_od_bs, od_w_out=v_od_w_out, final_g=v_final_g)
    order = list(weights)

    _, n_lat, d = x.shape
    n_ctx = ctx.shape[1]
    n_ext = n_lat + n_ctx
    ev_in = ev_w_in.shape[-1] * N_CHIP
    ev_mix = ev_w_out.shape[1] * N_CHIP
    conv_ch = ev_dw_b.shape[-1]
    conv_w = ev_dw_w.shape[1]
    attn_w = ev_mix - conv_ch
    kv_w = N_KV_HEADS * HEAD_DIM
    assert ev_in == 2 * kv_w + 2 * attn_w + 3 * conv_ch and conv_w // 2 < CONV_HALO
    sgu_w = od_w_out.shape[1] * N_CHIP
    wa = ada_w.shape[-1]
    cb = math.gcd(2 * kv_w, attn_w, conv_ch)
    off = dict(k=0, v=kv_w, q=2 * kv_w, za=2 * kv_w + attn_w, a=2 * kv_w + 2 * attn_w,
               b=2 * kv_w + 2 * attn_w + conv_ch, zb=2 * kv_w + 2 * attn_w + 2 * conv_ch)
    dims = dict(kv_w=kv_w, attn_w=attn_w, conv_ch=conv_ch, conv_w=conv_w, cb=cb, off=off)
    tr = 256 if (n_lat % 256 == 0 and n_ctx % 256 == 0) else 128

    mx, my, mc = lax.axis_index("x"), lax.axis_index("y"), lax.axis_index("c")
    me = 4 * mx + 2 * my + mc
    chip = 2 * mx + my

    x2d, tgt2d, ctx2d = x[0], loss_target[0], ctx[0]
    ev_dw_w_l = ev_dw_w[0]
    dwc = ev_dw_w_l.shape[1]
    lnc = od_ln_g.shape[1]

    g_c = _allgather_small(jnp.broadcast_to(c, (8, d)), "gather_cond")[:, 0, :]
    c_rows = jnp.concatenate([g_c, c_ctx[None, :], jnp.zeros((MOD_ROWS - N_DEV - 1, d), F32)], axis=0)
    c_rows_t = c_rows.T
    ada_b_shard = lax.dynamic_slice_in_dim(ada_b, chip * wa, wa, axis=1)[:, None, :]
    mod_part = _mod_fwd(c_rows_t, ada_w, ada_b_shard)
    part_shapes = [(2, MOD_ROWS, wa), (conv_w, dwc), (1, lnc), (1, lnc)]
    g_parts = _allgather_small(_pack([mod_part, ev_dw_w_l, od_ln_g, od_ln_b]), "gather_mod")
    per_chip = [_unpack(g_parts[2 * s], part_shapes) for s in range(N_CHIP)]
    mod_all = jnp.concatenate([p[0] for p in per_chip], axis=-1)
    dw_w_full = jnp.concatenate([p[1] for p in per_chip], axis=-1)
    od_ln_g_full = jnp.concatenate([p[2] for p in per_chip], axis=-1)
    od_ln_b_full = jnp.concatenate([p[3] for p in per_chip], axis=-1)
    dw_w_pad = jnp.pad(dw_w_full, ((0, 2 * CONV_HALO - conv_w), (0, 0)))
    mod_me = lax.dynamic_slice_in_dim(mod_all, me, 1, axis=1)
    shift0, scale0, gate0 = mod_me[0, :, :d], mod_me[0, :, d:2 * d], mod_me[0, :, 2 * d:]
    shift1, scale1, gate1 = mod_me[1, :, :d], mod_me[1, :, d:2 * d], mod_me[1, :, 2 * d:]
    shift_c, scale_c = mod_all[0, N_DEV:N_DEV + 1, :d], mod_all[0, N_DEV:N_DEV + 1, d:2 * d]
    g0, g1 = norm_g[0:1], norm_g[1:2]

    lay = dict(ev_w_in=_Sharded((d, ev_in), True), ev_w_out=_Sharded((ev_mix, d), False),
               od_w_in=_Sharded((d, 3 * sgu_w), True), od_w_out=_Sharded((sgu_w, d), False))
    big = list(lay)
    chip_arr = jnp.reshape(chip, (1,)).astype(jnp.int32)
    core_arr = jnp.reshape(mc, (1,)).astype(jnp.int32)
    own = {n: _cast_into_full(weights[n][0], lay[n], chip_arr, f"cast_{n}") for n in big}
    gather_ev_out = _gather_comm([own["ev_w_out"]], [lay["ev_w_out"]])
    layer1 = ["od_w_in", "od_w_out"]
    gather_layer1 = _gather_comm([own[n] for n in layer1], [lay[n] for n in layer1])

    def reduce_start(g, n):
        theirs = _pair_exchange([g], [lay[n]], f"pair_exchange_{n}")[0]
        psum = _pair_sum(g, theirs, lay[n], core_arr, f"pair_sum_{n}")
        return psum, _scatter_comm([psum], [lay[n]])

    h0, w_ev_in = _adaln_fwd(x2d, g0, shift0, scale0, tr, "adaln0_fwd",
                             comm=_gather_comm([own["ev_w_in"]], [lay["ev_w_in"]]), out_rows=n_ext)
    w_full = {"ev_w_in": w_ev_in}
    h0e = _adaln_fwd(ctx2d, g0, shift_c, scale_c, tr, "adaln0_ctx_fwd", into=h0, row0=n_lat)
    tm_e = _pick(n_ext, (1408, 768, 640, 512, 256, 128))
    tk_e = _pick(n_ext, (768, 640, 512, 256, 128))
    tm_l = _pick(n_lat, (1024, 512, 256, 128))
    p0, *gathered = _mm(h0e, w_full["ev_w_in"], name="mm_ev_in", tm=tm_e, tn=_pick(ev_in, (1408, 512, 256, 128)), tk=d,
                        out_dtype=BF16, comm=gather_ev_out)
    w_full["ev_w_out"] = gathered[0]
    cos_i, sin_s = _rope_tables(n_lat, n_ctx)
    q_hat, k_all, v_all = _qk_prep(p0, cos_i, sin_s, ev_q_norm, ev_k_norm, dims, tr)
    tq = _pick(n_lat, (256, 128))
    tkk = _pick(n_ext, (1408, 640, 512, 256, 128))
    attn, lse, *gathered = _flash_fwd(q_hat, k_all, v_all, n_lat, dims, tq, tkk, comm=gather_layer1)
    w_full.update(zip(layer1, gathered))
    mix, yc = _mix_fwd(attn, p0, dw_w_pad, ev_dw_b, ev_ln_g, ev_ln_b, dims, tr)
    o0, x1 = _mm(mix, w_full["ev_w_out"], name="mm_ev_out", tm=tm_l, tn=_pick(d, (1024, 512, 256)),
                 tk=ev_mix, out_dtype=BF16, res=x2d, gate=gate0)

    h1 = _adaln_fwd(x1, g1, shift1, scale1, tr, "adaln1_fwd")
    p1 = _mm(h1, w_full["od_w_in"], name="mm_od_in", tm=tm_l, tn=_pick(3 * sgu_w, (1536, 512, 256, 128)), tk=d,
             out_dtype=BF16)
    ws_b = od_ws[0].astype(BF16)
    ws_t_b = jnp.swapaxes(od_ws[0], 1, 2).astype(BF16)
    bs_t = od_bs[0].T
    m1 = _sgu_fwd(p1, od_ln_g_full, od_ln_b_full, ws_b, bs_t, tr)
    o1, x2 = _mm(m1, w_full["od_w_out"], name="mm_od_out", tm=tm_l, tn=_pick(d, (1024, 512, 256)),
                 tk=sgu_w, out_dtype=BF16, res=x1, gate=gate1)

    dx2, do1, loss_cols, d_final_g, dgate1 = _final_loss(x2, tgt2d, final_g[None, :], o1, gate1, tr)
    loss = lax.psum(0.5 / d * jnp.sum(loss_cols), ("x", "y", "c"))

    tk_l = _pick(n_lat, (1024, 512, 256, 128))
    psums, slots = {}, {}
    tk_nt = (1408, 1024, 768, 512, 256, 128)
    g_od_w_out = _mm(m1, do1, name="mm_od_out_dw", ta=True, tm=_pick(sgu_w, (2048, 1024, 512, 256)),
                     tn=_pick(d, (1024, 512, 256)), tk=tk_l, out_dtype=BF16)
    psums["od_w_out"], sc = reduce_start(g_od_w_out, "od_w_out")
    dm1, slots["od_w_out"] = _mm(do1, w_full["od_w_out"], name="mm_od_out_dx", tb=True, tm=tm_l,
                                 tn=_pick(sgu_w, (2048, 1024, 512, 256)), tk=d, out_dtype=BF16, comm=sc)
    dp1, d_ws, d_bs_t, d_od_ln_g, d_od_ln_b = _sgu_bwd(dm1, p1, od_ln_g_full, od_ln_b_full, ws_b, ws_t_b, bs_t, tr)
    small1 = [dgate1, d_final_g, d_od_ln_g, d_od_ln_b, d_ws, d_bs_t.T]
    g_od_w_in, g_small1 = _mm(h1, dp1, name="mm_od_in_dw", ta=True, tm=_pick(d, (2048, 1024, 512, 256)),
                              tn=_pick(3 * sgu_w, (1536, 768, 512, 384, 256, 128)), tk=tk_l, out_dtype=BF16,
                              comm=_allgather_comm(_pack(small1)))
    psums["od_w_in"], sc = reduce_start(g_od_w_in, "od_w_in")
    dh1, slots["od_w_in"] = _mm(dp1, w_full["od_w_in"], name="mm_od_in_dx", tb=True, tm=tm_l,
                                tn=_pick(d, (2048, 1024, 512, 256)), tk=_pick(3 * sgu_w, tk_nt), out_dtype=BF16, comm=sc)
    zero_d = jnp.zeros((1, d), F32)
    b1 = _adaln_bwd(x1, dh1, 0, g1, scale1, zero_d, tr, "adaln1_bwd", dres=dx2, o_prev=o0, gate_prev=gate0)
    dx1, do0, dgate0 = b1["dx"], b1["do_prev"], b1["dgate_prev"]

    g_ev_w_out = _mm(mix, do0, name="mm_ev_out_dw", ta=True, tm=_pick(ev_mix, (2048, 1024, 512, 256)),
                     tn=_pick(d, (1024, 512, 256)), tk=tk_l, out_dtype=BF16)
    psums["ev_w_out"], sc = reduce_start(g_ev_w_out, "ev_w_out")
    dmix, slots["ev_w_out"] = _mm(do0, w_full["ev_w_out"], name="mm_ev_out_dx", tb=True, tm=tm_l,
                                  tn=_pick(ev_mix, (2048, 1024, 512, 256)), tk=d, out_dtype=BF16, comm=sc)
    dattn, dza, dzb, dyc, d_ev_ln_g, d_ev_ln_b, d_dw_b = _mix_bwd_pointwise(
        dmix, attn, p0, yc, ev_ln_g, ev_ln_b, dims, tr, n_ext)
    dab, d_dw_w_pad = _conv_bwd(dyc, p0, dw_w_pad, dims, tr, n_ext)
    dq_hat, dk_hat, dv_all = _flash_bwd(q_hat, k_all, v_all, dattn, attn, lse, n_lat, dims, tq, tkk)
    dkvq, d_q_norm, d_k_norm = _qk_prep_bwd(dq_hat, dk_hat, dv_all, p0, cos_i, sin_s, ev_q_norm, ev_k_norm,
                                            dims, tr, n_lat)
    dp0 = jnp.concatenate([dkvq, dza, dab, dzb], axis=1)
    small2 = [b1["dshift"], b1["dscale"], dgate0, b1["dg"], d_q_norm, d_k_norm, d_dw_w_pad[:conv_w], d_dw_b,
              d_ev_ln_g, d_ev_ln_b]
    g_ev_w_in, g_small2 = _mm(h0e, dp0, name="mm_ev_in_dw", ta=True, tm=_pick(d, (2048, 1024, 512, 256)),
                              tn=_pick(ev_in, (1408, 768, 512, 256, 128)), tk=tk_e, out_dtype=BF16,
                              comm=_allgather_comm(_pack(small2)))
    psums["ev_w_in"], sc = reduce_start(g_ev_w_in, "ev_w_in")
    dh0, slots["ev_w_in"] = _mm(dp0, w_full["ev_w_in"], name="mm_ev_in_dx", tb=True, tm=tm_e,
                                tn=_pick(d, (2048, 1024, 512, 256)), tk=_pick(ev_in, tk_nt), out_dtype=BF16, comm=sc)
    lays = [lay[n] for n in big]
    halves = [_chip_sum(psums[n], slots[n], lay[n], chip_arr, core_arr, f"chip_sum_{n}") for n in big]
    bc = _adaln_bwd(ctx2d, dh0, n_lat, g0, scale_c, zero_d, tr, "adaln0_ctx_bwd", comm=_share_comm(halves, lays))
    g_big = dict(zip(big, bc["carried"]))
    b0 = _adaln_bwd(x2d, dh0, 0, g0, scale0, bc["dg"], tr, "adaln0_bwd", dres=dx1)
    grad_x = b0["dx"]

    small3 = [b0["dshift"], b0["dscale"], bc["dshift"], bc["dscale"], b0["dg"]]
    g_small3 = _allgather_small(_pack(small3), "gather_small_grads")

    def totals_and_rows(gathered, parts, n_rows, name):
        shapes = [a.shape for a in parts]
        tot = _unpack(_sum_slots(gathered, name), shapes)
        return tot, [r[:, 0, :] for r in _unpack(gathered, shapes[:n_rows])]

    (t_dgate1, t_fg, t_oln_g, t_oln_b, t_ws, t_bs), (dgate1_rows,) = totals_and_rows(
        g_small1, small1, 1, "sum_small_grads1")
    ((t_dshift1, t_dscale1, t_dgate0, t_g1, t_qn, t_kn, t_dw_w, t_dw_b, t_eln_g, t_eln_b),
     (dshift1_rows, dscale1_rows, dgate0_rows)) = totals_and_rows(g_small2, small2, 3, "sum_small_grads2")
    (t_dshift0, t_dscale0, t_dshift_c, t_dscale_c, t_g0), (dshift0_rows, dscale0_rows) = totals_and_rows(
        g_small3, small3, 2, "sum_small_grads3")
    zeros_d = jnp.zeros((1, d), F32)
    t_dmodc = jnp.concatenate([t_dshift_c, t_dscale_c, zeros_d], axis=1)
    t_dmod0 = jnp.concatenate([t_dshift0, t_dscale0, t_dgate0], axis=1)
    t_dmod1 = jnp.concatenate([t_dshift1, t_dscale1, t_dgate1], axis=1)
    dmod0_rows = jnp.concatenate([dshift0_rows, dscale0_rows, dgate0_rows], axis=1)
    dmod1_rows = jnp.concatenate([dshift1_rows, dscale1_rows, dgate1_rows], axis=1)
    pad_rows = jnp.zeros((MOD_ROWS - N_DEV - 1, 3 * d), F32)
    dm_l0 = jnp.concatenate([dmod0_rows, t_dmodc, pad_rows], axis=0)
    dm_l1 = jnp.concatenate([dmod1_rows, jnp.zeros((MOD_ROWS - N_DEV, 3 * d), F32)], axis=0)
    dm_shard = lax.dynamic_slice_in_dim(jnp.stack([dm_l0, dm_l1]), chip * wa, wa, axis=2)
    g_ada_w, dsc = _mod_bwd(c_rows_t, dm_shard, ada_w)
    g_dsc = _allgather_small(_pack([dsc[0]]), "gather_cctx")
    g_c_ctx = _cctx_grad(g_dsc, _pack([c_ctx])).reshape(-1)[:d]
    g_ada_b = jnp.stack([t_dmod0[0] + t_dmodc[0], t_dmod1[0]])

    grads = dict(
        c_ctx=g_c_ctx, ada_w=g_ada_w, ada_b=g_ada_b, norm_g=jnp.concatenate([t_g0, t_g1], axis=0),
        ev_w_in=g_big["ev_w_in"][None], ev_q_norm=t_qn, ev_k_norm=t_kn,
        ev_dw_w=lax.dynamic_slice_in_dim(t_dw_w, chip * dwc, dwc, axis=1)[None], ev_dw_b=t_dw_b,
        ev_ln_g=t_eln_g, ev_ln_b=t_eln_b, ev_w_out=g_big["ev_w_out"][None], od_w_in=g_big["od_w_in"][None],
        od_ln_g=lax.dynamic_slice_in_dim(t_oln_g, chip * lnc, lnc, axis=1),
        od_ln_b=lax.dynamic_slice_in_dim(t_oln_b, chip * lnc, lnc, axis=1),
        od_ws=t_ws[None], od_bs=t_bs[None], od_w_out=g_big["od_w_out"][None], final_g=t_fg[0])
    grads = {n: grads[n].reshape(weights[n].shape) for n in order}

    delta, new_m, new_v = {}, {}, {}
    large = ("ada_w", "ev_w_in", "ev_w_out", "od_w_in", "od_w_out")
    for n in large:
        shp = weights[n].shape
        as2d = lambda a: a.reshape(-1, shp[-1])
        outs = _adamw(as2d(weights[n]), as2d(grads[n]), as2d(mom_m[n]), as2d(mom_v[n]), f"adamw_{n}",
                      copy_grad=n in big)
        delta[n], new_m[n], new_v[n] = (o.reshape(shp) for o in outs[:3])
        if n in big:
            grads[n] = outs[3].reshape(shp)
    rest_names = [n for n in order if n not in large]
    rest_shapes = [weights[n].shape for n in rest_names]
    dl, nm, nv = _adamw(_pack([weights[n] for n in rest_names]), _pack([grads[n] for n in rest_names]),
                        _pack([mom_m[n] for n in rest_names]), _pack([mom_v[n] for n in rest_names]), "adamw_small")
    for n, a, b_, c_ in zip(rest_names, _unpack(dl, rest_shapes), _unpack(nm, rest_shapes), _unpack(nv, rest_shapes)):
        delta[n], new_m[n], new_v[n] = a, b_, c_

    return (loss, grad_x[None], *[grads[n] for n in order], *[delta[n] for n in order],
            *[new_m[n] for n in order], *[new_v[n] for n in order])
```

```python
import math

import jax
import jax.numpy as jnp
from jax import lax
from jax.experimental import pallas as pl
from jax.experimental.pallas import tpu as pltpu

F32 = jnp.float32
BF16 = jnp.bfloat16
EPS = 1e-6
GRID_W = 64
ROPE_THETA = 10000.0
HEAD_DIM = 128
N_KV_HEADS = 2
CONV_HALO = 16
LANES = 128
SUBLANES = 8
STENCIL_ROWS = 32
STENCIL_CHAINS = 4
REDUCE_ROWS = 32
N_DEV = 8
N_CHIP = 4
MOD_ROWS = 16
PACK_ROWS = 64
ADAM_LR, ADAM_B1, ADAM_B2, ADAM_EPS, ADAM_WD, ADAM_STEP = 0.001, 0.9, 0.999, 1e-08, 0.01, 10
VMEM_LIMIT = 56 * 1024 * 1024
MESH = pl.DeviceIdType.MESH
ANY = pl.BlockSpec(memory_space=pl.ANY)
VMEM_SPEC = pl.BlockSpec(memory_space=pltpu.VMEM)
CHIP_DELTAS = ((1, 0), (0, 1), (1, 1))


def _cp(sem=None):
    return pltpu.CompilerParams(dimension_semantics=sem, vmem_limit_bytes=VMEM_LIMIT)


def _pick(n, cands):
    for c in cands:
        if n % c == 0:
            return c
    raise ValueError(f"no tile for {n} in {cands}")


def _sigmoid(x):
    return 1.0 / (1.0 + jnp.exp(-x))


def _silu(x):
    return x * _sigmoid(x)


def _silu_and_grad(x):
    s = _sigmoid(x)
    y = x * s
    return y, s + y * (1.0 - s)


def _dsilu(x):
    return _silu_and_grad(x)[1]


_GELU_C = math.sqrt(2.0 / math.pi)
_GELU_A = 0.044715


def _gelu_and_grad(x):
    x2 = x * x
    t = jnp.tanh(x * (_GELU_C + (_GELU_C * _GELU_A) * x2))
    h = 0.5 + 0.5 * t
    return x * h, h + x * (1.0 - t * t) * (0.5 * _GELU_C + (1.5 * _GELU_C * _GELU_A) * x2)


def _gelu(x):
    x2 = x * x
    return x * (0.5 + 0.5 * jnp.tanh(x * (_GELU_C + (_GELU_C * _GELU_A) * x2)))


def _vec(d):
    return pl.BlockSpec((1, d), lambda *_: (0, 0))


def _cat(refs):
    parts = [r[...].astype(F32) for r in refs]
    return parts[0] if len(parts) == 1 else jnp.concatenate(parts, axis=1)


def _col_specs(rows, off, width, cb, row_map):
    assert off % cb == 0 and width % cb == 0
    return [pl.BlockSpec((rows, cb), (lambda *g, _c=off // cb + t: (row_map(*g), _c))) for t in range(width // cb)]


def _my_pos():
    return lax.axis_index("x"), lax.axis_index("y"), lax.axis_index("c")


def _allgather_small(x, name):
    r, c = x.shape

    def body(x_ref, out_ref, send_sems, recv_sems, local_sem):
        mx, my, mc = _my_pos()
        me = 4 * mx + 2 * my + mc
        mine = pltpu.make_async_copy(x_ref, out_ref.at[me], local_sem)
        mine.start()
        deltas = [(dx, dy, dc) for dx in (0, 1) for dy in (0, 1) for dc in (0, 1) if (dx, dy, dc) != (0, 0, 0)]
        sends = []
        for k, (dx, dy, dc) in enumerate(deltas):
            px, py, pc = (mx + dx) % 2, (my + dy) % 2, (mc + dc) % 2
            cp = pltpu.make_async_remote_copy(
                src_ref=x_ref, dst_ref=out_ref.at[me], send_sem=send_sems.at[k], recv_sem=recv_sems.at[k],
                device_id=(px, py, pc), device_id_type=MESH)
            cp.start()
            sends.append(cp)
        for k, (dx, dy, dc) in enumerate(deltas):
            px, py, pc = (mx + dx) % 2, (my + dy) % 2, (mc + dc) % 2
            peer = 4 * px + 2 * py + pc
            pltpu.make_async_remote_copy(
                src_ref=x_ref, dst_ref=out_ref.at[peer], send_sem=send_sems.at[k], recv_sem=recv_sems.at[k],
                device_id=(px, py, pc), device_id_type=MESH).wait_recv()
        for cp in sends:
            cp.wait_send()
        mine.wait()

    return pl.pallas_call(
        body, name=name,
        out_shape=jax.ShapeDtypeStruct((N_DEV, r, c), x.dtype),
        in_specs=[VMEM_SPEC], out_specs=VMEM_SPEC,
        scratch_shapes=[pltpu.SemaphoreType.DMA((N_DEV - 1,)), pltpu.SemaphoreType.DMA((N_DEV - 1,)),
                        pltpu.SemaphoreType.DMA],
        compiler_params=pltpu.CompilerParams(vmem_limit_bytes=VMEM_LIMIT),
    )(x)


class _Sharded:
    def __init__(self, full_shape, by_cols):
        self.full = full_shape
        self.by_cols = by_cols
        rows, cols = full_shape
        if by_cols:
            self.shard, self.half, self.halves = (rows, cols // N_CHIP), (rows // 2, cols // N_CHIP), (rows // 2, cols)
        else:
            self.shard, self.half, self.halves = (rows // N_CHIP, cols), (rows // N_CHIP, cols // 2), (rows, cols // 2)

    def region(self, ref, s, h):
        if self.by_cols:
            return ref.at[pl.ds(h * self.half[0], self.half[0]), pl.ds(s * self.shard[1], self.shard[1])]
        return ref.at[pl.ds(s * self.shard[0], self.shard[0]), pl.ds(h * self.half[1], self.half[1])]

    def halves_of_full(self, ref, h):
        if self.by_cols:
            return ref.at[pl.ds(h * self.halves[0], self.halves[0]), :]
        return ref.at[:, pl.ds(h * self.halves[1], self.halves[1])]

    def region_in_halves(self, ref, s):
        if self.by_cols:
            return ref.at[:, pl.ds(s * self.shard[1], self.shard[1])]
        return ref.at[pl.ds(s * self.shard[0], self.shard[0]), :]

    def half_of_shard(self, ref, h):
        if self.by_cols:
            return ref.at[pl.ds(h * self.half[0], self.half[0]), :]
        return ref.at[:, pl.ds(h * self.half[1], self.half[1])]


def _row_tile(rows, row_bytes):
    for t in (512, 256, 128, 64, 32, 16):
        if rows % t == 0 and t * row_bytes <= 2 * 1024 * 1024:
            return t
    return 16


def _cast_into_full(w_shard, lay, chip_arr, name):
    r, c = lay.shard
    tr = _row_tile(r, c * 4)
    nt = r // tr

    def body(chip_ref, w_ref, o_ref):
        o_ref[...] = w_ref[...].astype(BF16)

    if lay.by_cols:
        out_map = lambda i, chip_ref: (i, chip_ref[0])
    else:
        out_map = lambda i, chip_ref: (chip_ref[0] * nt + i, 0)
    return pl.pallas_call(
        body, name=name,
        grid_spec=pltpu.PrefetchScalarGridSpec(
            num_scalar_prefetch=1, grid=(nt,),
            in_specs=[pl.BlockSpec((tr, c), lambda i, chip_ref: (i, 0))],
            out_specs=pl.BlockSpec((tr, c), out_map)),
        out_shape=jax.ShapeDtypeStruct(lay.full, BF16), compiler_params=_cp(("parallel",)))(chip_arr, w_shard)


class _Carried:
    def __init__(self, ins, out_shapes, aliases, sem_shape, start, finish, middle=None):
        self.ins, self.out_shapes, self.aliases, self.sem_shape = list(ins), list(out_shapes), dict(aliases), sem_shape
        self.start, self.finish, self.middle = start, finish, middle

    def scratch(self):
        return [pltpu.SemaphoreType.DMA(self.sem_shape), pltpu.SemaphoreType.DMA(self.sem_shape)]

    def split(self, in_refs, out_refs, scratch_refs):
        ni, no = len(self.ins), len(self.out_shapes)
        return in_refs[len(in_refs) - ni:], out_refs[len(out_refs) - no:], scratch_refs[-2], scratch_refs[-1]


def _call(body, *, name, grid, in_specs, out_specs, out_shape, args, sem, comm=None, aliases=None):
    if comm is None:
        return pl.pallas_call(body, name=name, grid=grid, in_specs=in_specs, out_specs=out_specs,
                              out_shape=out_shape, input_output_aliases=aliases or {},
                              compiler_params=_cp(sem))(*args)
    n_in, n_out, n_ci, n_co = len(in_specs), len(out_specs), len(comm.ins), len(comm.out_shapes)

    def carrying(*refs):
        in_refs, out_refs = refs[:n_in + n_ci], refs[n_in + n_ci:n_in + n_ci + n_out + n_co]
        carried = comm.split(in_refs, out_refs, refs[n_in + n_ci + n_out + n_co:])
        first, last = None, None
        for axis, extent in enumerate(grid):
            at0, at1 = pl.program_id(axis) == 0, pl.program_id(axis) == extent - 1
            first = at0 if first is None else jnp.logical_and(first, at0)
            last = at1 if last is None else jnp.logical_and(last, at1)

        @pl.when(first)
        def _():
            comm.start(*carried)

        if comm.middle is not None:
            half = pl.program_id(0) == grid[0] // 2
            for axis in range(1, len(grid)):
                half = jnp.logical_and(half, pl.program_id(axis) == 0)

            @pl.when(half)
            def _():
                comm.middle(*carried)

        body(*in_refs[:n_in], *out_refs[:n_out])

        @pl.when(last)
        def _():
            comm.finish(*carried)

    return pl.pallas_call(
        carrying, name=name, grid=grid, in_specs=list(in_specs) + [ANY] * n_ci,
        out_specs=list(out_specs) + [ANY] * n_co, out_shape=list(out_shape) + comm.out_shapes,
        scratch_shapes=comm.scratch(),
        input_output_aliases={**(aliases or {}), **{n_in + s: n_out + d for s, d in comm.aliases.items()}},
        compiler_params=_cp(("arbitrary",) * len(grid)))(*args, *comm.ins)


def _gather_comm(fulls, layouts, forward_half_way=False):
    n = len(fulls)

    def ici(ins, outs, send_sems, recv_sems, a, j, landed=False):
        mx, my, mc = _my_pos()
        dx, dy = CHIP_DELTAS[j]
        px, py = (mx + dx) % 2, (my + dy) % 2
        src_chip = 2 * px + py if landed else 2 * mx + my
        return pltpu.make_async_remote_copy(
            src_ref=layouts[a].region(ins[a], src_chip, mc), dst_ref=layouts[a].region(outs[a], src_chip, mc),
            send_sem=send_sems.at[a, j], recv_sem=recv_sems.at[a, j], device_id=(px, py, mc), device_id_type=MESH)

    def d2d(ins, outs, send_sems, recv_sems, a, j, landed=False):
        mx, my, mc = _my_pos()
        dx, dy = CHIP_DELTAS[j]
        other = 2 * ((mx + dx) % 2) + (my + dy) % 2
        half = 1 - mc if landed else mc
        region = layouts[a].region(outs[a], other, half)
        return pltpu.make_async_remote_copy(
            src_ref=region, dst_ref=region, send_sem=send_sems.at[a, 3 + j], recv_sem=recv_sems.at[a, 3 + j],
            device_id=(mx, my, 1 - mc), device_id_type=MESH)

    pairs = [(a, j) for a in range(n) for j in range(3)]

    def start(*r):
        for a, j in pairs:
            ici(*r, a, j).start()

    def forward(*r):
        for a, j in pairs:
            ici(*r, a, j, landed=True).wait_recv()
            d2d(*r, a, j).start()

    def drain(*r):
        for a, j in pairs:
            d2d(*r, a, j, landed=True).wait_recv()
        for a, j in pairs:
            ici(*r, a, j).wait_send()
            d2d(*r, a, j).wait_send()

    def finish(*r):
        forward(*r)
        drain(*r)

    out_shapes = [jax.ShapeDtypeStruct(lay.full, BF16) for lay in layouts]
    if forward_half_way:
        return _Carried(fulls, out_shapes, {a: a for a in range(n)}, (n, 6), start, drain, middle=forward)
    return _Carried(fulls, out_shapes, {a: a for a in range(n)}, (n, 6), start, finish)


def _pair_exchange(grads, layouts, name):
    n = len(grads)

    def body(*refs):
        ins, outs = refs[:n], refs[n:2 * n]
        send_sems, recv_sems = refs[2 * n:]
        mx, my, mc = _my_pos()
        copies = []
        for a, lay in enumerate(layouts):
            cp = pltpu.make_async_remote_copy(
                src_ref=lay.halves_of_full(ins[a], 1 - mc), dst_ref=outs[a],
                send_sem=send_sems.at[a], recv_sem=recv_sems.at[a],
                device_id=(mx, my, 1 - mc), device_id_type=MESH)
            cp.start()
            copies.append(cp)
        for cp in copies:
            cp.wait()

    return pl.pallas_call(
        body, name=name,
        out_shape=[jax.ShapeDtypeStruct(lay.halves, g.dtype) for lay, g in zip(layouts, grads)],
        in_specs=[ANY] * n, out_specs=[ANY] * n,
        scratch_shapes=[pltpu.SemaphoreType.DMA((n,)), pltpu.SemaphoreType.DMA((n,))],
    )(*grads)


def _pair_sum(g, theirs, lay, core_arr, name):
    r, c = lay.halves
    tr = _row_tile(r, c * 4)
    nt = r // tr

    def body(core_ref, g_ref, t_ref, o_ref):
        o_ref[...] = (g_ref[...].astype(F32) + t_ref[...].astype(F32)).astype(BF16)

    if lay.by_cols:
        g_map = lambda i, core_ref: (core_ref[0] * nt + i, 0)
    else:
        g_map = lambda i, core_ref: (i, core_ref[0])
    plain = pl.BlockSpec((tr, c), lambda i, core_ref: (i, 0))
    return pl.pallas_call(
        body, name=name,
        grid_spec=pltpu.PrefetchScalarGridSpec(
            num_scalar_prefetch=1, grid=(nt,), in_specs=[pl.BlockSpec((tr, c), g_map), plain], out_specs=plain),
        out_shape=jax.ShapeDtypeStruct((r, c), BF16), compiler_params=_cp(("parallel",)))(core_arr, g, theirs)


def _scatter_comm(pair_sums, layouts):
    n = len(pair_sums)

    def copy(ins, outs, send_sems, recv_sems, a, j):
        mx, my, mc = _my_pos()
        dx, dy = CHIP_DELTAS[j]
        px, py = (mx + dx) % 2, (my + dy) % 2
        return pltpu.make_async_remote_copy(
            src_ref=layouts[a].region_in_halves(ins[a], 2 * px + py), dst_ref=outs[a].at[j],
            send_sem=send_sems.at[a, j], recv_sem=recv_sems.at[a, j], device_id=(px, py, mc), device_id_type=MESH)

    pairs = [(a, j) for a in range(n) for j in range(3)]

    def start(*r):
        for a, j in pairs:
            copy(*r, a, j).start()

    def finish(*r):
        for a, j in pairs:
            copy(*r, a, j).wait()

    return _Carried(pair_sums, [jax.ShapeDtypeStruct((3,) + lay.half, BF16) for lay in layouts], {}, (n, 3),
                    start, finish)


def _chip_sum(pair_sum, slots, lay, chip_arr, core_arr, name):
    r, c = lay.half
    tr = _row_tile(r, c * 4)
    nt = r // tr

    def body(chip_ref, core_ref, s_ref, slot_ref, o_ref):
        acc = s_ref[...].astype(F32)
        for j in range(3):
            acc = acc + slot_ref[j].astype(F32)
        o_ref[...] = acc

    if lay.by_cols:
        s_map = lambda i, chip_ref, core_ref: (i, chip_ref[0])
        o_map = lambda i, chip_ref, core_ref: (core_ref[0] * nt + i, 0)
    else:
        s_map = lambda i, chip_ref, core_ref: (chip_ref[0] * nt + i, 0)
        o_map = lambda i, chip_ref, core_ref: (i, core_ref[0])
    return pl.pallas_call(
        body, name=name,
        grid_spec=pltpu.PrefetchScalarGridSpec(
            num_scalar_prefetch=2, grid=(nt,),
            in_specs=[pl.BlockSpec((tr, c), s_map),
                      pl.BlockSpec((3, tr, c), lambda i, chip_ref, core_ref: (0, i, 0))],
            out_specs=pl.BlockSpec((tr, c), o_map)),
        out_shape=jax.ShapeDtypeStruct(lay.shard, F32), compiler_params=_cp(("parallel",)))(
            chip_arr, core_arr, pair_sum, slots)


def _share_comm(bufs, layouts):
    n = len(bufs)

    def copy(ins, outs, send_sems, recv_sems, a, landed=False):
        mx, my, mc = _my_pos()
        half = 1 - mc if landed else mc
        return pltpu.make_async_remote_copy(
            src_ref=layouts[a].half_of_shard(ins[a], half), dst_ref=layouts[a].half_of_shard(outs[a], half),
            send_sem=send_sems.at[a], recv_sem=recv_sems.at[a], device_id=(mx, my, 1 - mc), device_id_type=MESH)

    def start(*r):
        for a in range(n):
            copy(*r, a).start()

    def finish(*r):
        for a in range(n):
            copy(*r, a, landed=True).wait_recv()
        for a in range(n):
            copy(*r, a).wait_send()

    return _Carried(bufs, [jax.ShapeDtypeStruct(lay.shard, F32) for lay in layouts], {a: a for a in range(n)},
                    (n,), start, finish)


def _allgather_comm(x):
    deltas = [(dx, dy, dc) for dx in (0, 1) for dy in (0, 1) for dc in (0, 1) if (dx, dy, dc) != (0, 0, 0)]
    local = len(deltas)

    def remote(ins, outs, send_sems, recv_sems, k, landed=False):
        mx, my, mc = _my_pos()
        dx, dy, dc = deltas[k]
        px, py, pc = (mx + dx) % 2, (my + dy) % 2, (mc + dc) % 2
        slot = 4 * px + 2 * py + pc if landed else 4 * mx + 2 * my + mc
        return pltpu.make_async_remote_copy(
            src_ref=ins[0], dst_ref=outs[0].at[slot], send_sem=send_sems.at[k], recv_sem=recv_sems.at[k],
            device_id=(px, py, pc), device_id_type=MESH)

    def mine(ins, outs, send_sems, recv_sems):
        mx, my, mc = _my_pos()
        return pltpu.make_async_copy(ins[0], outs[0].at[4 * mx + 2 * my + mc], send_sems.at[local])

    def start(*r):
        mine(*r).start()
        for k in range(len(deltas)):
            remote(*r, k).start()

    def finish(*r):
        for k in range(len(deltas)):
            remote(*r, k, landed=True).wait_recv()
        for k in range(len(deltas)):
            remote(*r, k).wait_send()
        mine(*r).wait()

    return _Carried([x], [jax.ShapeDtypeStruct((N_DEV,) + x.shape, x.dtype)], {}, (N_DEV,), start, finish)


def _sum_slots(x, name):
    s, r, c = x.shape
    tr = _pick(r, (256, 128, 64, 32, 16, 8))

    def body(x_ref, o_ref):
        acc = x_ref[0]
        for k in range(1, s):
            acc = acc + x_ref[k]
        o_ref[...] = acc

    return pl.pallas_call(body, name=name, grid=(r // tr,),
                          in_specs=[pl.BlockSpec((s, tr, c), lambda i: (0, i, 0))],
                          out_specs=pl.BlockSpec((tr, c), lambda i: (i, 0)),
                          out_shape=jax.ShapeDtypeStruct((r, c), F32), compiler_params=_cp(("parallel",)))(x)


def _adamw(w, g, m, v, name, copy_grad=False):
    r, c = w.shape
    tr = _pick(r, (256, 128, 64, 32, 16, 8))
    bc1 = 1.0 - ADAM_B1 ** ADAM_STEP
    bc2 = 1.0 - ADAM_B2 ** ADAM_STEP
    n_out = 4 if copy_grad else 3

    def body(w_ref, g_ref, m_ref, v_ref, d_ref, nm_ref, nv_ref, *g_copy):
        gv = g_ref[...]
        nm = ADAM_B1 * m_ref[...] + (1.0 - ADAM_B1) * gv
        nv = ADAM_B2 * v_ref[...] + (1.0 - ADAM_B2) * (gv * gv)
        d_ref[...] = -ADAM_LR * ((nm / bc1) / (jnp.sqrt(nv / bc2) + ADAM_EPS) + ADAM_WD * w_ref[...])
        nm_ref[...] = nm
        nv_ref[...] = nv
        if copy_grad:
            g_copy[0][...] = gv

    spec = pl.BlockSpec((tr, c), lambda i: (i, 0))
    shp = jax.ShapeDtypeStruct((r, c), F32)
    return _call(body, name=name, grid=(r // tr,), in_specs=[spec] * 4, out_specs=[spec] * n_out,
                 out_shape=[shp] * n_out, args=(w, g, m, v), sem=("parallel",))


def _cctx_grad(parts, c_ctx2d):
    def body(p_ref, c_ref, o_ref):
        tot = ((p_ref[0] + p_ref[2]) + p_ref[4]) + p_ref[6]
        o_ref[...] = tot * _dsilu(c_ref[...])

    return pl.pallas_call(body, name="cctx_grad", in_specs=[VMEM_SPEC, VMEM_SPEC], out_specs=VMEM_SPEC,
                          out_shape=jax.ShapeDtypeStruct(c_ctx2d.shape, F32))(parts, c_ctx2d)


def _mod_fwd(c_rows_t, ada_w, ada_b_shard):
    nl, d, w = ada_w.shape
    td = _pick(d, (256, 128))
    nd = d // td

    def body(ct_ref, w_ref, b_ref, o_ref):
        i = pl.program_id(1)

        @pl.when(i == 0)
        def _():
            o_ref[0] = jnp.broadcast_to(b_ref[0], (MOD_ROWS, w))

        st = _silu(ct_ref[...])
        wv = w_ref[0]
        used = N_DEV + 1
        rows = [jnp.sum(st[:, r:r + 1] * wv, axis=0, keepdims=True) for r in range(used)]
        rows.append(jnp.zeros((MOD_ROWS - used, w), F32))
        o_ref[0] += jnp.concatenate(rows, axis=0)

    return pl.pallas_call(
        body, name="mod_fwd", grid=(nl, nd),
        in_specs=[pl.BlockSpec((td, MOD_ROWS), lambda l, i: (i, 0)),
                  pl.BlockSpec((1, td, w), lambda l, i: (l, i, 0)),
                  pl.BlockSpec((1, 1, w), lambda l, i: (l, 0, 0))],
        out_specs=pl.BlockSpec((1, MOD_ROWS, w), lambda l, i: (l, 0, 0)),
        out_shape=jax.ShapeDtypeStruct((nl, MOD_ROWS, w), F32),
        compiler_params=_cp(("parallel", "arbitrary")),
    )(c_rows_t, ada_w, ada_b_shard)


def _mod_bwd(c_rows_t, dmod, ada_w):
    nl, d, w = ada_w.shape
    td = _pick(d, (256, 128))
    ctx_row = N_DEV

    def body(ct_ref, dm_ref, w_ref, gw_ref, ds_ref):
        st = _silu(ct_ref[...])
        dm = dm_ref[0]
        acc = st[:, 0:1] * dm[0:1, :]
        for r in range(1, ctx_row + 1):
            acc = acc + st[:, r:r + 1] * dm[r:r + 1, :]
        gw_ref[0] = acc
        ds_ref[0] = jnp.sum(w_ref[0] * dm[ctx_row:ctx_row + 1, :], axis=1, keepdims=True)

    return pl.pallas_call(
        body, name="mod_bwd", grid=(nl, d // td),
        in_specs=[pl.BlockSpec((td, MOD_ROWS), lambda l, i: (i, 0)),
                  pl.BlockSpec((1, MOD_ROWS, w), lambda l, i: (l, 0, 0)),
                  pl.BlockSpec((1, td, w), lambda l, i: (l, i, 0))],
        out_specs=[pl.BlockSpec((1, td, w), lambda l, i: (l, i, 0)),
                   pl.BlockSpec((1, td, 1), lambda l, i: (l, i, 0))],
        out_shape=[jax.ShapeDtypeStruct((nl, d, w), F32), jax.ShapeDtypeStruct((nl, d, 1), F32)],
        compiler_params=_cp(("parallel", "parallel")),
    )(c_rows_t, dmod, ada_w)


def _adaln_fwd(x, g, shift, scale, tr, name, comm=None, out_rows=None, into=None, row0=0):
    r, d = x.shape
    rb0 = row0 // tr
    total = into.shape[0] if into is not None else (out_rows or r)

    def body(x_ref, g_ref, sh_ref, sc_ref, *rest):
        o_ref = rest[-1]
        xv = x_ref[...]
        rs = lax.rsqrt(jnp.mean(xv * xv, axis=-1, keepdims=True) + EPS)
        o_ref[...] = ((xv * rs * g_ref[...]) * (1.0 + sc_ref[...]) + sh_ref[...]).astype(BF16)

    spec = pl.BlockSpec((tr, d), lambda i: (i, 0))
    in_specs, args, aliases = [spec, _vec(d), _vec(d), _vec(d)], [x, g, shift, scale], None
    if into is not None:
        in_specs.append(ANY)
        args.append(into)
        aliases = {4: 0}
    outs = _call(body, name=name, grid=(r // tr,), in_specs=in_specs,
                 out_specs=[pl.BlockSpec((tr, d), lambda i: (rb0 + i, 0))],
                 out_shape=[jax.ShapeDtypeStruct((total, d), BF16)], args=args, sem=("parallel",),
                 comm=comm, aliases=aliases)
    return outs if comm else outs[0]


def _adaln_bwd(xin, dh, row0, g, scale, dg_init, tr, name, dres=None, o_prev=None, gate_prev=None, comm=None):
    r, d = xin.shape
    assert row0 % tr == 0
    rb0 = row0 // tr
    want_dx = dres is not None
    want_prev = o_prev is not None
    assert want_dx or not want_prev

    def body(*refs):
        it = iter(refs)
        x_ref, dh_ref, g_ref, sc_ref, dgi_ref = next(it), next(it), next(it), next(it), next(it)
        dres_ref = next(it) if want_dx else None
        o_ref, gp_ref = (next(it), next(it)) if want_prev else (None, None)
        dx_ref = next(it) if want_dx else None
        do_ref = next(it) if want_prev else None
        dsh_ref, dsc_ref, dg_ref = next(it), next(it), next(it)
        dgp_ref = next(it) if want_prev else None
        i = pl.program_id(0)

        @pl.when(i == 0)
        def _():
            dsh_ref[...] = jnp.zeros_like(dsh_ref)
            dsc_ref[...] = jnp.zeros_like(dsc_ref)
            dg_ref[...] = dgi_ref[...]
            if want_prev:
                dgp_ref[...] = jnp.zeros_like(dgp_ref)

        xv = x_ref[...]
        dhv = dh_ref[...].astype(F32)
        gv = g_ref[...]
        rs = lax.rsqrt(jnp.mean(xv * xv, axis=-1, keepdims=True) + EPS)
        xn = xv * rs
        dsh_ref[...] += jnp.sum(dhv, axis=0, keepdims=True)
        dsc_ref[...] += jnp.sum(dhv * (xn * gv), axis=0, keepdims=True)
        dr = dhv * (1.0 + sc_ref[...])
        dg_ref[...] += jnp.sum(dr * xn, axis=0, keepdims=True)
        if want_dx:
            gy = dr * gv
            dx = dres_ref[...] + rs * (gy - xn * jnp.mean(gy * xn, axis=-1, keepdims=True))
            dx_ref[...] = dx
            if want_prev:
                do_ref[...] = (gp_ref[...] * dx).astype(BF16)
                dgp_ref[...] += jnp.sum(dx * o_ref[...].astype(F32), axis=0, keepdims=True)

    row = pl.BlockSpec((tr, d), lambda i: (i, 0))
    in_specs = [row, pl.BlockSpec((tr, d), lambda i: (rb0 + i, 0)), _vec(d), _vec(d), _vec(d)]
    args = [xin, dh, g, scale, dg_init]
    out_specs, out_shape, names = [], [], []
    if want_dx:
        in_specs.append(row)
        args.append(dres)
    if want_prev:
        in_specs += [row, _vec(d)]
        args += [o_prev, gate_prev]
    if want_dx:
        out_specs.append(row)
        out_shape.append(jax.ShapeDtypeStruct((r, d), F32))
        names.append("dx")
    if want_prev:
        out_specs.append(row)
        out_shape.append(jax.ShapeDtypeStruct((r, d), BF16))
        names.append("do_prev")
    for nm in ("dshift", "dscale", "dg") + (("dgate_prev",) if want_prev else ()):
        out_specs.append(_vec(d))
        out_shape.append(jax.ShapeDtypeStruct((1, d), F32))
        names.append(nm)
    outs = _call(body, name=name, grid=(r // tr,), in_specs=in_specs, out_specs=out_specs, out_shape=out_shape,
                 args=args, sem=("arbitrary",), comm=comm)
    res = dict(zip(names, outs))
    if comm:
        res["carried"] = outs[len(names):]
    return res


def _mm(a, b, *, name, tm, tn, tk, ta=False, tb=False, out_dtype=F32, res=None, gate=None, comm=None):
    if ta:
        kd, m = a.shape
    else:
        m, kd = a.shape
    if tb:
        n, kd2 = b.shape
    else:
        kd2, n = b.shape
    assert kd == kd2 and m % tm == 0 and n % tn == 0 and kd % tk == 0, (a.shape, b.shape, tm, tn, tk)
    ni, nj, nk = m // tm, n // tn, kd // tk
    dn = (((0 if ta else 1,), (1 if tb else 0,)), ((), ()))
    with_res = res is not None
    n_in = 4 if with_res else 2
    n_out = 2 if with_res else 1
    n_cin = len(comm.ins) if comm else 0
    n_cout = len(comm.out_shapes) if comm else 0

    def body(*refs):
        in_refs = refs[:n_in + n_cin]
        out_refs = refs[n_in + n_cin:n_in + n_cin + n_out + n_cout]
        scratch = refs[n_in + n_cin + n_out + n_cout:]
        a_ref, b_ref = in_refs[0], in_refs[1]
        o_ref = out_refs[0]
        i, j, k = pl.program_id(0), pl.program_id(1), pl.program_id(2)
        if comm:
            carried = comm.split(in_refs, out_refs, scratch)

            @pl.when(jnp.logical_and(jnp.logical_and(i == 0, j == 0), k == 0))
            def _():
                comm.start(*carried)

            if comm.middle is not None:
                @pl.when(jnp.logical_and(jnp.logical_and(i == ni // 2, j == 0), k == 0))
                def _():
                    comm.middle(*carried)

        def emit(acc):
            o_ref[...] = acc.astype(o_ref.dtype)
            if with_res:
                out_refs[1][...] = in_refs[2][...] + in_refs[3][...] * acc

        if nk == 1:
            emit(lax.dot_general(a_ref[...], b_ref[...], dn, preferred_element_type=F32))
        else:
            acc_ref = scratch[0]

            @pl.when(k == 0)
            def _():
                acc_ref[...] = jnp.zeros_like(acc_ref)

            acc_ref[...] += lax.dot_general(a_ref[...], b_ref[...], dn, preferred_element_type=F32)

            @pl.when(k == nk - 1)
            def _():
                emit(acc_ref[...])

        if comm:
            @pl.when(jnp.logical_and(jnp.logical_and(i == ni - 1, j == nj - 1), k == nk - 1))
            def _():
                comm.finish(*carried)

    a_spec = pl.BlockSpec((tk, tm), lambda i, j, k: (k, i)) if ta else pl.BlockSpec((tm, tk), lambda i, j, k: (i, k))
    b_spec = pl.BlockSpec((tn, tk), lambda i, j, k: (j, k)) if tb else pl.BlockSpec((tk, tn), lambda i, j, k: (k, j))
    o_spec = pl.BlockSpec((tm, tn), lambda i, j, k: (i, j))
    in_specs, args = [a_spec, b_spec], [a, b]
    out_specs, out_shape = [o_spec], [jax.ShapeDtypeStruct((m, n), out_dtype)]
    if with_res:
        in_specs += [o_spec, pl.BlockSpec((1, tn), lambda i, j, k: (0, j))]
        args += [res, gate]
        out_specs.append(o_spec)
        out_shape.append(jax.ShapeDtypeStruct((m, n), F32))
    scratch_shapes = [pltpu.VMEM((tm, tn), F32)] if nk > 1 else []
    aliases = {}
    sem = ("parallel", "parallel", "arbitrary")
    if comm:
        in_specs += [ANY] * n_cin
        args += comm.ins
        out_specs += [ANY] * n_cout
        out_shape += comm.out_shapes
        scratch_shapes += comm.scratch()
        aliases = {n_in + s: n_out + d for s, d in comm.aliases.items()}
        sem = ("arbitrary", "arbitrary", "arbitrary")
    outs = pl.pallas_call(body, name=name, grid=(ni, nj, nk), in_specs=in_specs, out_specs=out_specs,
                          out_shape=out_shape, scratch_shapes=scratch_shapes, input_output_aliases=aliases,
                          compiler_params=_cp(sem))(*args)
    return outs if (with_res or comm) else outs[0]


def _swap_pairs(x):
    lane = lax.broadcasted_iota(jnp.int32, x.shape, 1)
    return jnp.where(lane % 2 == 0, pltpu.roll(x, HEAD_DIM - 1, 1), pltpu.roll(x, 1, 1))


def _qk_prep(p0, cos_i, sin_s, q_norm, k_norm, dims, tr):
    rows = p0.shape[0]
    kvw, aw, cb = dims["kv_w"], dims["attn_w"], dims["cb"]
    nkv, nq = kvw // HEAD_DIM, aw // HEAD_DIM
    scale = HEAD_DIM ** -0.5
    n_kv_specs, n_q_specs = (2 * kvw) // cb, aw // cb

    def body(*refs):
        kv_refs = refs[:n_kv_specs]
        q_refs = refs[n_kv_specs:n_kv_specs + n_q_specs]
        cos_ref, sin_ref, qn_ref, kn_ref, qo_ref, ko_ref, vo_ref = refs[n_kv_specs + n_q_specs:]
        kv = _cat(kv_refs)
        qv = _cat(q_refs)
        cs, sn = cos_ref[...], sin_ref[...]

        def norm_rope(xh, gvec):
            rs = lax.rsqrt(jnp.mean(xh * xh, axis=-1, keepdims=True) + EPS)
            xn = xh * rs * gvec
            return xn * cs + _swap_pairs(xn) * sn

        for h in range(nkv):
            sl = slice(h * HEAD_DIM, (h + 1) * HEAD_DIM)
            ko_ref[:, sl] = norm_rope(kv[:, sl], kn_ref[...]).astype(BF16)
        vo_ref[...] = kv[:, kvw:].astype(BF16)
        for h in range(nq):
            sl = slice(h * HEAD_DIM, (h + 1) * HEAD_DIM)
            qo_ref[:, sl] = (norm_rope(qv[:, sl], qn_ref[...]) * scale).astype(BF16)

    rm = lambda i: i
    in_specs = (_col_specs(tr, 0, 2 * kvw, cb, rm) + _col_specs(tr, 2 * kvw, aw, cb, rm)
                + [pl.BlockSpec((tr, HEAD_DIM), lambda i: (i, 0))] * 2 + [_vec(HEAD_DIM)] * 2)
    args = [p0] * (n_kv_specs + n_q_specs) + [cos_i, sin_s, q_norm, k_norm]
    return pl.pallas_call(
        body, name="qk_prep", grid=(rows // tr,), in_specs=in_specs,
        out_specs=[pl.BlockSpec((tr, aw), lambda i: (i, 0)), pl.BlockSpec((tr, kvw), lambda i: (i, 0)),
                   pl.BlockSpec((tr, kvw), lambda i: (i, 0))],
        out_shape=[jax.ShapeDtypeStruct((rows, aw), BF16), jax.ShapeDtypeStruct((rows, kvw), BF16),
                   jax.ShapeDtypeStruct((rows, kvw), BF16)],
        compiler_params=_cp(("parallel",)))(*args)


def _qk_prep_bwd(dq_hat, dk_hat, dv, p0, cos_i, sin_s, q_norm, k_norm, dims, tr, n_lat):
    rows = p0.shape[0]
    kvw, aw, cb = dims["kv_w"], dims["attn_w"], dims["cb"]
    nkv, nq = kvw // HEAD_DIM, aw // HEAD_DIM
    scale = HEAD_DIM ** -0.5
    n_kv_specs, n_q_specs = (2 * kvw) // cb, aw // cb
    lat_tiles = n_lat // tr

    def body(*refs):
        kv_refs = refs[:n_kv_specs]
        q_refs = refs[n_kv_specs:n_kv_specs + n_q_specs]
        (dq_ref, dk_ref, dv_ref, cos_ref, sin_ref, qn_ref, kn_ref,
         out_ref, dqn_ref, dkn_ref) = refs[n_kv_specs + n_q_specs:]
        i = pl.program_id(0)

        @pl.when(i == 0)
        def _():
            dqn_ref[...] = jnp.zeros_like(dqn_ref)
            dkn_ref[...] = jnp.zeros_like(dkn_ref)

        kv = _cat(kv_refs)
        qv = _cat(q_refs)
        cs, sn = cos_ref[...], sin_ref[...]
        is_lat = (i < lat_tiles).astype(F32)

        def head_bwd(xh, dhat, gvec):
            dn = dhat * cs + _swap_pairs(dhat * sn)
            rs = lax.rsqrt(jnp.mean(xh * xh, axis=-1, keepdims=True) + EPS)
            xn = xh * rs
            gy = dn * gvec
            dx = rs * (gy - xn * jnp.mean(gy * xn, axis=-1, keepdims=True))
            return dx, jnp.sum(dn * xn, axis=0, keepdims=True)

        dkn = jnp.zeros((1, HEAD_DIM), F32)
        for h in range(nkv):
            sl = slice(h * HEAD_DIM, (h + 1) * HEAD_DIM)
            dx, dgv = head_bwd(kv[:, sl], dk_ref[:, sl], kn_ref[...])
            out_ref[:, sl] = dx.astype(BF16)
            dkn = dkn + dgv
        dkn_ref[...] += dkn
        out_ref[:, kvw:2 * kvw] = dv_ref[...].astype(BF16)
        dqn = jnp.zeros((1, HEAD_DIM), F32)
        for h in range(nq):
            sl = slice(h * HEAD_DIM, (h + 1) * HEAD_DIM)
            dx, dgv = head_bwd(qv[:, sl], dq_ref[:, sl] * (scale * is_lat), qn_ref[...])
            out_ref[:, 2 * kvw + h * HEAD_DIM:2 * kvw + (h + 1) * HEAD_DIM] = dx.astype(BF16)
            dqn = dqn + dgv
        dqn_ref[...] += dqn

    rm = lambda i: i
    wout = 2 * kvw + aw
    in_specs = (_col_specs(tr, 0, 2 * kvw, cb, rm) + _col_specs(tr, 2 * kvw, aw, cb, rm)
                + [pl.BlockSpec((tr, aw), lambda i: (jnp.minimum(i, lat_tiles - 1), 0)),
                   pl.BlockSpec((tr, kvw), lambda i: (i, 0)), pl.BlockSpec((tr, kvw), lambda i: (i, 0)),
                   pl.BlockSpec((tr, HEAD_DIM), lambda i: (i, 0)), pl.BlockSpec((tr, HEAD_DIM), lambda i: (i, 0)),
                   _vec(HEAD_DIM), _vec(HEAD_DIM)])
    args = [p0] * (n_kv_specs + n_q_specs) + [dq_hat, dk_hat, dv, cos_i, sin_s, q_norm, k_norm]
    return pl.pallas_call(
        body, name="qk_prep_bwd", grid=(rows // tr,), in_specs=in_specs,
        out_specs=[pl.BlockSpec((tr, wout), lambda i: (i, 0)), _vec(HEAD_DIM), _vec(HEAD_DIM)],
        out_shape=[jax.ShapeDtypeStruct((rows, wout), BF16), jax.ShapeDtypeStruct((1, HEAD_DIM), F32),
                   jax.ShapeDtypeStruct((1, HEAD_DIM), F32)],
        compiler_params=_cp(("arbitrary",)))(*args)


def _stack_heads(x, g):
    return jnp.concatenate([x[:, h * HEAD_DIM:(h + 1) * HEAD_DIM] for h in range(g)], axis=0)


def _flash_fwd(q_hat, k_all, v_all, n_lat, dims, tq, tk, comm=None):
    kvw, aw = dims["kv_w"], dims["attn_w"]
    nkv = kvw // HEAD_DIM
    g = aw // kvw
    gw = g * HEAD_DIM
    n_keys = k_all.shape[0]
    ni, nj = n_lat // tq, n_keys // tk
    dn_nt = (((1,), (1,)), ((), ()))

    def body(q_ref, k_ref, v_ref, o_ref, lse_ref):
        qs = _stack_heads(q_ref[...], g)
        m = jnp.full((g * tq, 1), -1e30, F32)
        l = jnp.zeros((g * tq, 1), F32)
        acc = jnp.zeros((g * tq, HEAD_DIM), F32)
        for j in range(nj):
            kb = k_ref[pl.ds(j * tk, tk), :]
            vb = v_ref[pl.ds(j * tk, tk), :]
            s = lax.dot_general(qs, kb, dn_nt, preferred_element_type=F32)
            m_new = jnp.maximum(m, jnp.max(s, axis=-1, keepdims=True))
            alpha = jnp.exp(m - m_new)
            p = jnp.exp(s - m_new)
            l = alpha * l + jnp.sum(p, axis=-1, keepdims=True)
            acc = alpha * acc + jnp.dot(p.astype(BF16), vb, preferred_element_type=F32)
            m = m_new
        o = acc / l
        for h in range(g):
            o_ref[:, h * HEAD_DIM:(h + 1) * HEAD_DIM] = o[h * tq:(h + 1) * tq]
        lse_ref[...] = m + jnp.log(l)

    return _call(
        body, name="flash_fwd", grid=(nkv, ni),
        in_specs=[pl.BlockSpec((tq, gw), lambda h, i: (i, h)),
                  pl.BlockSpec((n_keys, HEAD_DIM), lambda h, i: (0, h)),
                  pl.BlockSpec((n_keys, HEAD_DIM), lambda h, i: (0, h))],
        out_specs=[pl.BlockSpec((tq, gw), lambda h, i: (i, h)),
                   pl.BlockSpec((g * tq, 1), lambda h, i: (h * ni + i, 0))],
        out_shape=[jax.ShapeDtypeStruct((n_lat, aw), F32), jax.ShapeDtypeStruct((nkv * ni * g * tq, 1), F32)],
        args=(q_hat, k_all, v_all), sem=("parallel", "parallel"), comm=comm)


def _flash_bwd(q_hat, k_all, v_all, do, o, lse, n_lat, dims, tq, tk):
    kvw, aw = dims["kv_w"], dims["attn_w"]
    nkv = kvw // HEAD_DIM
    g = aw // kvw
    gw = g * HEAD_DIM
    n_keys = k_all.shape[0]
    ni, nj = n_lat // tq, n_keys // tk
    dn_nt = (((1,), (1,)), ((), ()))
    dn_tn = (((0,), (0,)), ((), ()))

    def body(q_ref, k_ref, v_ref, do_ref, o_ref, lse_ref, dq_ref, dk_ref, dv_ref):
        i = pl.program_id(1)

        @pl.when(i == 0)
        def _():
            dk_ref[...] = jnp.zeros_like(dk_ref)
            dv_ref[...] = jnp.zeros_like(dv_ref)

        dos = _stack_heads(do_ref[...], g)
        qs = _stack_heads(q_ref[...], g)
        delta = jnp.sum(dos.astype(F32) * _stack_heads(o_ref[...], g), axis=-1, keepdims=True)
        lse_v = lse_ref[...]
        dq = jnp.zeros((g * tq, HEAD_DIM), F32)
        for j in range(nj):
            rows = pl.ds(j * tk, tk)
            kb, vb = k_ref[rows, :], v_ref[rows, :]
            s = lax.dot_general(qs, kb, dn_nt, preferred_element_type=F32)
            p = jnp.exp(s - lse_v)
            dp = lax.dot_general(dos, vb, dn_nt, preferred_element_type=F32)
            ds = (p * (dp - delta)).astype(BF16)
            dv_ref[rows, :] += lax.dot_general(p.astype(BF16), dos, dn_tn, preferred_element_type=F32)
            dk_ref[rows, :] += lax.dot_general(ds, qs, dn_tn, preferred_element_type=F32)
            dq = dq + jnp.dot(ds, kb, preferred_element_type=F32)
        for h in range(g):
            dq_ref[:, h * HEAD_DIM:(h + 1) * HEAD_DIM] = dq[h * tq:(h + 1) * tq]

    qspec = pl.BlockSpec((tq, gw), lambda h, i: (i, h))
    full_k = pl.BlockSpec((n_keys, HEAD_DIM), lambda h, i: (0, h))
    return pl.pallas_call(
        body, name="flash_bwd", grid=(nkv, ni),
        in_specs=[qspec, full_k, full_k, qspec, qspec, pl.BlockSpec((g * tq, 1), lambda h, i: (h * ni + i, 0))],
        out_specs=[qspec, full_k, full_k],
        out_shape=[jax.ShapeDtypeStruct((n_lat, aw), F32), jax.ShapeDtypeStruct((n_keys, kvw), F32),
                   jax.ShapeDtypeStruct((n_keys, kvw), F32)],
        compiler_params=_cp(("parallel", "arbitrary")))(q_hat, k_all, v_all, do, o, lse)


def _shifted_copies(pad_ref, sh_ref, tr):
    rows = tr + 2 * CONV_HALO - SUBLANES
    for r in range(SUBLANES):
        sh_ref[r] = pad_ref[pl.ds(r, rows), :]


def _stencil(sh_ref, w_ref, out_ref, offsets, tr, cc, init_ref=None):
    for c0 in range(0, cc, LANES):
        lanes = pl.ds(c0, LANES)
        wv = [jnp.broadcast_to(w_ref[pl.ds(k, 1), lanes], (STENCIL_ROWS, LANES)) for k in range(len(offsets))]
        if init_ref is None:
            init = jnp.zeros((STENCIL_ROWS, LANES), F32)
        else:
            init = jnp.broadcast_to(init_ref[:, lanes], (STENCIL_ROWS, LANES))

        def block(rb, carry, lanes=lanes, wv=wv, init=init):
            r0 = rb * STENCIL_ROWS
            parts = [init] + [None] * (STENCIL_CHAINS - 1)
            for k, o in enumerate(offsets):
                rows = pl.ds(SUBLANES * (o // SUBLANES) + r0, STENCIL_ROWS)
                term = sh_ref[o % SUBLANES, rows, lanes] * wv[k]
                q = k % STENCIL_CHAINS
                parts[q] = term if parts[q] is None else parts[q] + term
            acc = parts[0]
            for p_ in parts[1:]:
                acc = acc + p_
            out_ref[pl.ds(r0, STENCIL_ROWS), lanes] = acc
            return carry

        for rb in range(tr // STENCIL_ROWS):
            block(rb, 0)


def _stencil_weight_grad(sh_ref, d_ref, d_row0, dw_ref, offsets, tr, cc):
    rows_per = REDUCE_ROWS
    for c0 in range(0, cc, LANES):
        lanes = pl.ds(c0, LANES)

        def block(rb, accs, lanes=lanes):
            r0 = rb * rows_per
            dblk = d_ref[pl.ds(d_row0 + r0, rows_per), lanes]
            out = []
            for k, o in enumerate(offsets):
                prod = dblk * sh_ref[o % SUBLANES, pl.ds(SUBLANES * (o // SUBLANES) + r0, rows_per), lanes]
                part = prod[0:SUBLANES]
                for q in range(1, rows_per // SUBLANES):
                    part = part + prod[q * SUBLANES:(q + 1) * SUBLANES]
                out.append(accs[k] + part)
            return tuple(out)

        zero = jnp.zeros((SUBLANES, LANES), F32)
        accs = tuple(zero for _ in offsets)
        for rb in range(tr // rows_per):
            accs = block(rb, accs)
        for k in range(len(offsets)):
            dw_ref[pl.ds(k, 1), lanes] += jnp.sum(accs[k], axis=0, keepdims=True)


def _halo_maps(tr, n_tiles):
    per = tr // CONV_HALO
    prev = lambda i: jnp.maximum(i * per - 1, 0)
    nxt = lambda i: (i + 1) * per
    return prev, nxt


def _mix_fwd(attn, p0, dw_w, dw_b, ln_g, ln_b, dims, tr):
    n_lat, aw = attn.shape
    cc, cb, cw = dims["conv_ch"], dims["cb"], dims["conv_w"]
    off = dims["off"]
    n_tiles = n_lat // tr
    pad = cw // 2
    na, nc = aw // cb, cc // cb
    prev_map, next_map = _halo_maps(tr, n_tiles)

    def body(*refs):
        it = iter(refs)
        attn_ref = next(it)
        za = [next(it) for _ in range(na)]
        a_c = [next(it) for _ in range(nc)]
        b_c = [next(it) for _ in range(nc)]
        zb = [next(it) for _ in range(nc)]
        a_p = [next(it) for _ in range(nc)]
        b_p = [next(it) for _ in range(nc)]
        a_n = [next(it) for _ in range(nc)]
        b_n = [next(it) for _ in range(nc)]
        w_ref, db_ref, g_ref, bb_ref, mix_ref, yc_ref, ypad, ysh = (next(it) for _ in range(8))
        i = pl.program_id(0)
        ypad[pl.ds(0, CONV_HALO), :] = _cat(a_p) * _sigmoid(_cat(b_p)) * (i > 0).astype(F32)
        ypad[pl.ds(CONV_HALO, tr), :] = _cat(a_c) * _sigmoid(_cat(b_c))
        ypad[pl.ds(CONV_HALO + tr, CONV_HALO), :] = _cat(a_n) * _sigmoid(_cat(b_n)) * (i < n_tiles - 1).astype(F32)
        _shifted_copies(ypad, ysh, tr)
        _stencil(ysh, w_ref, yc_ref, [CONV_HALO - pad + k for k in range(cw)], tr, cc, init_ref=db_ref)
        acc = yc_ref[...]
        mu = jnp.mean(acc, axis=-1, keepdims=True)
        xc = acc - mu
        rs = lax.rsqrt(jnp.mean(xc * xc, axis=-1, keepdims=True) + EPS)
        nv = xc * rs * g_ref[...] + bb_ref[...]
        mix_ref[:, :aw] = (attn_ref[...] * _silu(_cat(za))).astype(BF16)
        mix_ref[:, aw:] = (_silu(nv) * _silu(_cat(zb))).astype(BF16)

    rm = lambda i: i
    in_specs = ([pl.BlockSpec((tr, aw), lambda i: (i, 0))]
                + _col_specs(tr, off["za"], aw, cb, rm) + _col_specs(tr, off["a"], cc, cb, rm)
                + _col_specs(tr, off["b"], cc, cb, rm) + _col_specs(tr, off["zb"], cc, cb, rm)
                + _col_specs(CONV_HALO, off["a"], cc, cb, prev_map) + _col_specs(CONV_HALO, off["b"], cc, cb, prev_map)
                + _col_specs(CONV_HALO, off["a"], cc, cb, next_map) + _col_specs(CONV_HALO, off["b"], cc, cb, next_map)
                + [pl.BlockSpec(dw_w.shape, lambda i: (0, 0)), _vec(cc), _vec(cc), _vec(cc)])
    args = [attn] + [p0] * (na + 7 * nc) + [dw_w, dw_b, ln_g, ln_b]
    return pl.pallas_call(
        body, name="mix_fwd", grid=(n_tiles,), in_specs=in_specs,
        out_specs=[pl.BlockSpec((tr, aw + cc), lambda i: (i, 0)), pl.BlockSpec((tr, cc), lambda i: (i, 0))],
        out_shape=[jax.ShapeDtypeStruct((n_lat, aw + cc), BF16), jax.ShapeDtypeStruct((n_lat, cc), F32)],
        scratch_shapes=[pltpu.VMEM((tr + 2 * CONV_HALO, cc), F32),
                        pltpu.VMEM((SUBLANES, tr + 2 * CONV_HALO - SUBLANES, cc), F32)],
        compiler_params=_cp(("parallel",)))(*args)


def _mix_bwd_pointwise(dmix, attn, p0, yc, ln_g, ln_b, dims, tr, n_ext):
    n_lat, aw = attn.shape
    cc, cb, off = dims["conv_ch"], dims["cb"], dims["off"]
    na, nc = aw // cb, cc // cb
    lat_tiles = n_lat // tr

    def body(*refs):
        it = iter(refs)
        dmix_ref, attn_ref = next(it), next(it)
        za = [next(it) for _ in range(na)]
        zb = [next(it) for _ in range(nc)]
        yc_ref, g_ref, bb_ref = next(it), next(it), next(it)
        dattn_ref, dza_ref, dzb_ref, dyc_ref, dg_ref, dbb_ref, ddb_ref = (next(it) for _ in range(7))
        i = pl.program_id(0)

        @pl.when(i == 0)
        def _():
            dg_ref[...] = jnp.zeros_like(dg_ref)
            dbb_ref[...] = jnp.zeros_like(dbb_ref)
            ddb_ref[...] = jnp.zeros_like(ddb_ref)

        lat = (i < lat_tiles).astype(F32)
        dm = dmix_ref[...].astype(F32)
        dma, dmb = dm[:, :aw], dm[:, aw:]
        zav, zbv = _cat(za), _cat(zb)
        sa, dsa = _silu_and_grad(zav)
        sb, dsb = _silu_and_grad(zbv)
        dattn_ref[...] = (dma * sa).astype(BF16)
        dza_ref[...] = (dma * attn_ref[...] * dsa * lat).astype(BF16)
        ycv = yc_ref[...]
        mu = jnp.mean(ycv, axis=-1, keepdims=True)
        xc = ycv - mu
        rs = lax.rsqrt(jnp.mean(xc * xc, axis=-1, keepdims=True) + EPS)
        xh = xc * rs
        nv = xh * g_ref[...] + bb_ref[...]
        sn, dsn = _silu_and_grad(nv)
        dzb_ref[...] = (dmb * sn * dsb * lat).astype(BF16)
        dn = dmb * sb * dsn
        dg_ref[...] += lat * jnp.sum(dn * xh, axis=0, keepdims=True)
        dbb_ref[...] += lat * jnp.sum(dn, axis=0, keepdims=True)
        dxh = dn * g_ref[...]
        dyc = rs * (dxh - jnp.mean(dxh, axis=-1, keepdims=True) - xh * jnp.mean(dxh * xh, axis=-1, keepdims=True))
        dyc_ref[...] = dyc
        ddb_ref[...] += lat * jnp.sum(dyc, axis=0, keepdims=True)

    rm = lambda i: jnp.minimum(i, lat_tiles - 1)
    row = lambda w: pl.BlockSpec((tr, w), lambda i: (rm(i), 0))
    ext = lambda w: pl.BlockSpec((tr, w), lambda i: (i, 0))
    in_specs = ([row(aw + cc), row(aw)] + _col_specs(tr, off["za"], aw, cb, rm)
                + _col_specs(tr, off["zb"], cc, cb, rm) + [row(cc), _vec(cc), _vec(cc)])
    args = [dmix, attn] + [p0] * (na + nc) + [yc, ln_g, ln_b]
    return pl.pallas_call(
        body, name="mix_bwd_pointwise", grid=(n_ext // tr,), in_specs=in_specs,
        out_specs=[row(aw), ext(aw), ext(cc), row(cc), _vec(cc), _vec(cc), _vec(cc)],
        out_shape=[jax.ShapeDtypeStruct((n_lat, aw), BF16), jax.ShapeDtypeStruct((n_ext, aw), BF16),
                   jax.ShapeDtypeStruct((n_ext, cc), BF16), jax.ShapeDtypeStruct((n_lat, cc), F32)]
                  + [jax.ShapeDtypeStruct((1, cc), F32)] * 3,
        compiler_params=_cp(("arbitrary",)))(*args)


def _conv_bwd(dyc, p0, dw_w, dims, tr, n_ext):
    n_lat, cc = dyc.shape
    cb, cw, off = dims["cb"], dims["conv_w"], dims["off"]
    nc = cc // cb
    n_tiles = n_lat // tr
    pad = cw // 2
    prev_lat, next_lat = _halo_maps(tr, n_tiles)
    rm = lambda i: jnp.minimum(i, n_tiles - 1)
    prev_map = lambda i: prev_lat(rm(i))
    next_map = lambda i: next_lat(rm(i))

    def body(*refs):
        it = iter(refs)
        a_c = [next(it) for _ in range(nc)]
        b_c = [next(it) for _ in range(nc)]
        a_p = [next(it) for _ in range(nc)]
        b_p = [next(it) for _ in range(nc)]
        a_n = [next(it) for _ in range(nc)]
        b_n = [next(it) for _ in range(nc)]
        d_c, d_p, d_n, w_ref, dab_ref, dw_ref, ypad, dpad, ysh, dsh, dy_scr = (next(it) for _ in range(11))
        i = pl.program_id(0)

        @pl.when(i == 0)
        def _():
            dw_ref[...] = jnp.zeros_like(dw_ref)

        @pl.when(i >= n_tiles)
        def _():
            dab_ref[...] = jnp.zeros_like(dab_ref)

        @pl.when(i < n_tiles)
        def _():
            first, last = (i > 0).astype(F32), (i < n_tiles - 1).astype(F32)
            av, bv = _cat(a_c), _cat(b_c)
            sg = _sigmoid(bv)
            ypad[pl.ds(0, CONV_HALO), :] = _cat(a_p) * _sigmoid(_cat(b_p)) * first
            ypad[pl.ds(CONV_HALO, tr), :] = av * sg
            ypad[pl.ds(CONV_HALO + tr, CONV_HALO), :] = _cat(a_n) * _sigmoid(_cat(b_n)) * last
            dpad[pl.ds(0, CONV_HALO), :] = d_p[...] * first
            dpad[pl.ds(CONV_HALO, tr), :] = d_c[...]
            dpad[pl.ds(CONV_HALO + tr, CONV_HALO), :] = d_n[...] * last
            _shifted_copies(ypad, ysh, tr)
            _shifted_copies(dpad, dsh, tr)
            _stencil(dsh, w_ref, dy_scr, [CONV_HALO + pad - k for k in range(cw)], tr, cc)
            _stencil_weight_grad(ysh, dpad, CONV_HALO, dw_ref, [CONV_HALO - pad + k for k in range(cw)], tr, cc)
            dy = dy_scr[...]
            dab_ref[:, :cc] = (dy * sg).astype(BF16)
            dab_ref[:, cc:] = (dy * av * sg * (1.0 - sg)).astype(BF16)

    in_specs = (_col_specs(tr, off["a"], cc, cb, rm) + _col_specs(tr, off["b"], cc, cb, rm)
                + _col_specs(CONV_HALO, off["a"], cc, cb, prev_map) + _col_specs(CONV_HALO, off["b"], cc, cb, prev_map)
                + _col_specs(CONV_HALO, off["a"], cc, cb, next_map) + _col_specs(CONV_HALO, off["b"], cc, cb, next_map)
                + [pl.BlockSpec((tr, cc), lambda i: (rm(i), 0)),
                   pl.BlockSpec((CONV_HALO, cc), lambda i: (prev_map(i), 0)),
                   pl.BlockSpec((CONV_HALO, cc), lambda i: (jnp.minimum(next_map(i), n_lat // CONV_HALO - 1), 0)),
                   pl.BlockSpec(dw_w.shape, lambda i: (0, 0))])
    args = [p0] * (6 * nc) + [dyc, dyc, dyc, dw_w]
    return pl.pallas_call(
        body, name="conv_bwd", grid=(n_ext // tr,), in_specs=in_specs,
        out_specs=[pl.BlockSpec((tr, 2 * cc), lambda i: (i, 0)), pl.BlockSpec(dw_w.shape, lambda i: (0, 0))],
        out_shape=[jax.ShapeDtypeStruct((n_ext, 2 * cc), BF16), jax.ShapeDtypeStruct(dw_w.shape, F32)],
        scratch_shapes=[pltpu.VMEM((tr + 2 * CONV_HALO, cc), F32), pltpu.VMEM((tr + 2 * CONV_HALO, cc), F32),
                        pltpu.VMEM((SUBLANES, tr + 2 * CONV_HALO - SUBLANES, cc), F32),
                        pltpu.VMEM((SUBLANES, tr + 2 * CONV_HALO - SUBLANES, cc), F32),
                        pltpu.VMEM((tr, cc), F32)],
        compiler_params=_cp(("arbitrary",)))(*args)


def _sgu_parts(u, v, ln_g, ln_b):
    mu = jnp.mean(v, axis=-1, keepdims=True)
    xc = v - mu
    rs = lax.rsqrt(jnp.mean(xc * xc, axis=-1, keepdims=True) + EPS)
    xh = xc * rs
    return u, xh, rs, xh * ln_g + ln_b


def _sgu_fwd(p1, ln_g, ln_b, ws, bs_t, tr):
    n_lat, w3 = p1.shape
    w = w3 // 3
    ng, ch = ws.shape[0], ws.shape[1]
    gwid = w // ng
    n_ch = tr // ch

    def body(pu_ref, pv_ref, pg_ref, g_ref, b_ref, ws_ref, bs_ref, o_ref):
        u, _, _, vln = _sgu_parts(_gelu(pu_ref[...].astype(F32)), _gelu(pv_ref[...].astype(F32)),
                                  g_ref[...], b_ref[...])
        gate = _silu(pg_ref[...].astype(F32))
        vb = vln.astype(BF16)
        for c in range(n_ch):
            rs_ = slice(c * ch, (c + 1) * ch)
            for gi in range(ng):
                cs_ = slice(gi * gwid, (gi + 1) * gwid)
                mixed = jnp.dot(ws_ref[gi], vb[rs_, cs_], preferred_element_type=F32) + bs_ref[:, gi:gi + 1]
                o_ref[rs_, cs_] = (u[rs_, cs_] * mixed * gate[rs_, cs_]).astype(BF16)

    col = lambda t: pl.BlockSpec((tr, w), lambda i, _t=t: (i, _t))
    return pl.pallas_call(
        body, name="sgu_fwd", grid=(n_lat // tr,),
        in_specs=[col(0), col(1), col(2), _vec(w), _vec(w),
                  pl.BlockSpec(ws.shape, lambda i: (0, 0, 0)), pl.BlockSpec(bs_t.shape, lambda i: (0, 0))],
        out_specs=pl.BlockSpec((tr, w), lambda i: (i, 0)),
        out_shape=jax.ShapeDtypeStruct((n_lat, w), BF16),
        compiler_params=_cp(("parallel",)))(p1, p1, p1, ln_g, ln_b, ws, bs_t)


def _sgu_bwd(dm, p1, ln_g, ln_b, ws, ws_t, bs_t, tr):
    n_lat, w3 = p1.shape
    w = w3 // 3
    ng, ch = ws.shape[0], ws.shape[1]
    gwid = w // ng
    n_ch = tr // ch
    dn_nt = (((1,), (1,)), ((), ()))

    def body(dm_ref, pu_ref, pv_ref, pg_ref, g_ref, b_ref, ws_ref, wst_ref, bs_ref,
             dp_ref, dws_ref, dbs_ref, dg_ref, dbb_ref, dvln_scr):
        i = pl.program_id(0)

        @pl.when(i == 0)
        def _():
            dws_ref[...] = jnp.zeros_like(dws_ref)
            dbs_ref[...] = jnp.zeros_like(dbs_ref)
            dg_ref[...] = jnp.zeros_like(dg_ref)
            dbb_ref[...] = jnp.zeros_like(dbb_ref)

        puv, pvv, pgv = pu_ref[...].astype(F32), pv_ref[...].astype(F32), pg_ref[...].astype(F32)
        gu, dgu = _gelu_and_grad(puv)
        gv, dgv = _gelu_and_grad(pvv)
        gate, dgate = _silu_and_grad(pgv)
        u, xh, rs, vln = _sgu_parts(gu, gv, g_ref[...], b_ref[...])
        dmv = dm_ref[...].astype(F32)
        vb = vln.astype(BF16)
        dmu = dmv * u
        du_pre = dmv * gate * dgu
        dg_pre = dmu * dgate
        dmix_all = dmu * gate
        dbs_cols = [jnp.zeros((ch, 1), F32) for _ in range(ng)]
        for c in range(n_ch):
            rs_ = slice(c * ch, (c + 1) * ch)
            for gi in range(ng):
                cs_ = slice(gi * gwid, (gi + 1) * gwid)
                mixed = jnp.dot(ws_ref[gi], vb[rs_, cs_], preferred_element_type=F32) + bs_ref[:, gi:gi + 1]
                dmixed = dmix_all[rs_, cs_]
                dmb = dmixed.astype(BF16)
                dp_ref[rs_, gi * gwid:(gi + 1) * gwid] = (du_pre[rs_, cs_] * mixed).astype(BF16)
                dp_ref[rs_, 2 * w + gi * gwid:2 * w + (gi + 1) * gwid] = (dg_pre[rs_, cs_] * mixed).astype(BF16)
                dvln_scr[rs_, cs_] = jnp.dot(wst_ref[gi], dmb, preferred_element_type=F32)
                dws_ref[gi] += lax.dot_general(dmb, vb[rs_, cs_], dn_nt, preferred_element_type=F32)
                dbs_cols[gi] = dbs_cols[gi] + jnp.sum(dmixed, axis=-1, keepdims=True)
        dbs_ref[...] += jnp.concatenate(dbs_cols, axis=1)
        dvln = dvln_scr[...]
        dg_ref[...] += jnp.sum(dvln * xh, axis=0, keepdims=True)
        dbb_ref[...] += jnp.sum(dvln, axis=0, keepdims=True)
        dxh = dvln * g_ref[...]
        dv = rs * (dxh - jnp.mean(dxh, axis=-1, keepdims=True) - xh * jnp.mean(dxh * xh, axis=-1, keepdims=True))
        dp_ref[:, w:2 * w] = (dv * dgv).astype(BF16)

    col = lambda t: pl.BlockSpec((tr, w), lambda i, _t=t: (i, _t))
    return pl.pallas_call(
        body, name="sgu_bwd", grid=(n_lat // tr,),
        in_specs=[pl.BlockSpec((tr, w), lambda i: (i, 0)), col(0), col(1), col(2), _vec(w), _vec(w),
                  pl.BlockSpec(ws.shape, lambda i: (0, 0, 0)), pl.BlockSpec(ws.shape, lambda i: (0, 0, 0)),
                  pl.BlockSpec(bs_t.shape, lambda i: (0, 0))],
        out_specs=[pl.BlockSpec((tr, w3), lambda i: (i, 0)), pl.BlockSpec(ws.shape, lambda i: (0, 0, 0)),
                   pl.BlockSpec(bs_t.shape, lambda i: (0, 0)), _vec(w), _vec(w)],
        out_shape=[jax.ShapeDtypeStruct((n_lat, w3), BF16), jax.ShapeDtypeStruct(ws.shape, F32),
                   jax.ShapeDtypeStruct(bs_t.shape, F32), jax.ShapeDtypeStruct((1, w), F32),
                   jax.ShapeDtypeStruct((1, w), F32)],
        scratch_shapes=[pltpu.VMEM((tr, w), F32)],
        compiler_params=_cp(("arbitrary",)))(dm, p1, p1, p1, ln_g, ln_b, ws, ws_t, bs_t)


def _final_loss(x2, target, final_g, o_prev, gate_prev, tr):
    n_lat, d = x2.shape

    def body(x_ref, t_ref, g_ref, o_ref, gp_ref, dx_ref, do_ref, ls_ref, dg_ref, dgp_ref):
        i = pl.program_id(0)

        @pl.when(i == 0)
        def _():
            ls_ref[...] = jnp.zeros_like(ls_ref)
            dg_ref[...] = jnp.zeros_like(dg_ref)
            dgp_ref[...] = jnp.zeros_like(dgp_ref)

        xv = x_ref[...]
        gv = g_ref[...]
        rs = lax.rsqrt(jnp.mean(xv * xv, axis=-1, keepdims=True) + EPS)
        xn = xv * rs
        err = xn * gv - t_ref[...]
        ls_ref[...] += jnp.sum(err * err, axis=0, keepdims=True)
        dy = err * (1.0 / d)
        dg_ref[...] += jnp.sum(dy * xn, axis=0, keepdims=True)
        gy = dy * gv
        dx = rs * (gy - xn * jnp.mean(gy * xn, axis=-1, keepdims=True))
        dx_ref[...] = dx
        do_ref[...] = (gp_ref[...] * dx).astype(BF16)
        dgp_ref[...] += jnp.sum(dx * o_ref[...].astype(F32), axis=0, keepdims=True)

    row = pl.BlockSpec((tr, d), lambda i: (i, 0))
    return pl.pallas_call(
        body, name="final_loss", grid=(n_lat // tr,), in_specs=[row, row, _vec(d), row, _vec(d)],
        out_specs=[row, row, _vec(d), _vec(d), _vec(d)],
        out_shape=[jax.ShapeDtypeStruct((n_lat, d), F32), jax.ShapeDtypeStruct((n_lat, d), BF16)]
                  + [jax.ShapeDtypeStruct((1, d), F32)] * 3,
        compiler_params=_cp(("arbitrary",)))(x2, target, final_g, o_prev, gate_prev)


def _pack(arrays):
    flat = jnp.concatenate([a.reshape(-1).astype(F32) for a in arrays])
    n = flat.shape[0]
    rows = -(-n // LANES)
    rows = -(-rows // PACK_ROWS) * PACK_ROWS
    return jnp.pad(flat, (0, rows * LANES - n)).reshape(rows, LANES)


def _unpack(buf, shapes):
    flat = buf.reshape(buf.shape[:-2] + (-1,))
    out, pos = [], 0
    for shp in shapes:
        n = math.prod(shp)
        out.append(flat[..., pos:pos + n].reshape(buf.shape[:-2] + tuple(shp)))
        pos += n
    return out


def _rope_tables(n_lat, n_ctx):
    rows = n_lat // GRID_W
    row = jnp.repeat(jnp.arange(rows, dtype=F32), GRID_W)
    col = jnp.tile(jnp.arange(GRID_W, dtype=F32), rows)
    n_freq, axis_dim = HEAD_DIM // 4, HEAD_DIM // 2
    inv = jnp.power(ROPE_THETA, jnp.arange(n_freq, dtype=F32) * (-2.0 / axis_dim))
    ang = jnp.concatenate([row[:, None] * inv, col[:, None] * inv], axis=-1)
    cos, sin = jnp.cos(ang), jnp.sin(ang)
    cos_i = jnp.repeat(cos, 2, axis=-1)
    sin_s = jnp.stack([-sin, sin], axis=-1).reshape(n_lat, HEAD_DIM)
    cos_i = jnp.concatenate([cos_i, jnp.ones((n_ctx, HEAD_DIM), F32)], axis=0)
    sin_s = jnp.concatenate([sin_s, jnp.zeros((n_ctx, HEAD_DIM), F32)], axis=0)
    return cos_i, sin_s


def kernel(x, c, ctx, c_ctx, ada_w, ada_b, norm_g, ev_w_in, ev_q_norm, ev_k_norm, ev_dw_w, ev_dw_b, ev_ln_g, ev_ln_b, ev_w_out, od_w_in, od_ln_g, od_ln_b, od_ws, od_bs, od_w_out, final_g, loss_target, m_c_ctx, m_ada_w, m_ada_b, m_norm_g, m_ev_w_in, m_ev_q_norm, m_ev_k_norm, m_ev_dw_w, m_ev_dw_b, m_ev_ln_g, m_ev_ln_b, m_ev_w_out, m_od_w_in, m_od_ln_g, m_od_ln_b, m_od_ws, m_od_bs, m_od_w_out, m_final_g, v_c_ctx, v_ada_w, v_ada_b, v_norm_g, v_ev_w_in, v_ev_q_norm, v_ev_k_norm, v_ev_dw_w, v_ev_dw_b, v_ev_ln_g, v_ev_ln_b, v_ev_w_out, v_od_w_in, v_od_ln_g, v_od_ln_b, v_od_ws, v_od_bs, v_od_w_out, v_final_g):
    weights = dict(c_ctx=c_ctx, ada_w=ada_w, ada_b=ada_b, norm_g=norm_g, ev_w_in=ev_w_in, ev_q_norm=ev_q_norm,
                   ev_k_norm=ev_k_norm, ev_dw_w=ev_dw_w, ev_dw_b=ev_dw_b, ev_ln_g=ev_ln_g, ev_ln_b=ev_ln_b,
                   ev_w_out=ev_w_out, od_w_in=od_w_in, od_ln_g=od_ln_g, od_ln_b=od_ln_b, od_ws=od_ws, od_bs=od_bs,
                   od_w_out=od_w_out, final_g=final_g)
    mom_m = dict(c_ctx=m_c_ctx, ada_w=m_ada_w, ada_b=m_ada_b, norm_g=m_norm_g, ev_w_in=m_ev_w_in,
                 ev_q_norm=m_ev_q_norm, ev_k_norm=m_ev_k_norm, ev_dw_w=m_ev_dw_w, ev_dw_b=m_ev_dw_b,
                 ev_ln_g=m_ev_ln_g, ev_ln_b=m_ev_ln_b, ev_w_out=m_ev_w_out, od_w_in=m_od_w_in, od_ln_g=m_od_ln_g,
                 od_ln_b=m_od_ln_b, od_ws=m_od_ws, od_bs=m_od_bs, od_w_out=m_od_w_out, final_g=m_final_g)
    mom_v = dict(c_ctx=v_c_ctx, ada_w=v_ada_w, ada_b=v_ada_b, norm_g=v_norm_g, ev_w_in=v_ev_w_in,
                 ev_q_norm=v_ev_q_norm, ev_k_norm=v_ev_k_norm, ev_dw_w=v_ev_dw_w, ev_dw_b=v_ev_dw_b,
                 ev_ln_g=v_ev_ln_g, ev_ln_b=v_ev_ln_b, ev_w_out=v_ev_w_out, od_w_in=v_od_w_in, od_ln_g=v_od_ln_g,
                 od_ln_b=v_od_ln_b, od_ws=v_od_ws, od_bs=v_od_bs, od_w_out=v_od_w_out, final_g=v_final_g)
    order = list(weights)

    _, n_lat, d = x.shape
    n_ctx = ctx.shape[1]
    n_ext = n_lat + n_ctx
    ev_in = ev_w_in.shape[-1] * N_CHIP
    ev_mix = ev_w_out.shape[1] * N_CHIP
    conv_ch = ev_dw_b.shape[-1]
    conv_w = ev_dw_w.shape[1]
    attn_w = ev_mix - conv_ch
    kv_w = N_KV_HEADS * HEAD_DIM
    assert ev_in == 2 * kv_w + 2 * attn_w + 3 * conv_ch and conv_w // 2 < CONV_HALO
    sgu_w = od_w_out.shape[1] * N_CHIP
    wa = ada_w.shape[-1]
    cb = math.gcd(2 * kv_w, attn_w, conv_ch)
    off = dict(k=0, v=kv_w, q=2 * kv_w, za=2 * kv_w + attn_w, a=2 * kv_w + 2 * attn_w,
               b=2 * kv_w + 2 * attn_w + conv_ch, zb=2 * kv_w + 2 * attn_w + 2 * conv_ch)
    dims = dict(kv_w=kv_w, attn_w=attn_w, conv_ch=conv_ch, conv_w=conv_w, cb=cb, off=off)
    tr = 256 if (n_lat % 256 == 0 and n_ctx % 256 == 0) else 128

    mx, my, mc = lax.axis_index("x"), lax.axis_index("y"), lax.axis_index("c")
    me = 4 * mx + 2 * my + mc
    chip = 2 * mx + my

    x2d, tgt2d, ctx2d = x[0], loss_target[0], ctx[0]
    ev_dw_w_l = ev_dw_w[0]
    dwc = ev_dw_w_l.shape[1]
    lnc = od_ln_g.shape[1]

    g_c = _allgather_small(jnp.broadcast_to(c, (8, d)), "gather_cond")[:, 0, :]
    c_rows = jnp.concatenate([g_c, c_ctx[None, :], jnp.zeros((MOD_ROWS - N_DEV - 1, d), F32)], axis=0)
    c_rows_t = c_rows.T
    ada_b_shard = lax.dynamic_slice_in_dim(ada_b, chip * wa, wa, axis=1)[:, None, :]
    mod_part = _mod_fwd(c_rows_t, ada_w, ada_b_shard)
    part_shapes = [(2, MOD_ROWS, wa), (conv_w, dwc), (1, lnc), (1, lnc)]
    g_parts = _allgather_small(_pack([mod_part, ev_dw_w_l, od_ln_g, od_ln_b]), "gather_mod")
    per_chip = [_unpack(g_parts[2 * s], part_shapes) for s in range(N_CHIP)]
    mod_all = jnp.concatenate([p[0] for p in per_chip], axis=-1)
    dw_w_full = jnp.concatenate([p[1] for p in per_chip], axis=-1)
    od_ln_g_full = jnp.concatenate([p[2] for p in per_chip], axis=-1)
    od_ln_b_full = jnp.concatenate([p[3] for p in per_chip], axis=-1)
    dw_w_pad = jnp.pad(dw_w_full, ((0, 2 * CONV_HALO - conv_w), (0, 0)))
    mod_me = lax.dynamic_slice_in_dim(mod_all, me, 1, axis=1)
    shift0, scale0, gate0 = mod_me[0, :, :d], mod_me[0, :, d:2 * d], mod_me[0, :, 2 * d:]
    shift1, scale1, gate1 = mod_me[1, :, :d], mod_me[1, :, d:2 * d], mod_me[1, :, 2 * d:]
    shift_c, scale_c = mod_all[0, N_DEV:N_DEV + 1, :d], mod_all[0, N_DEV:N_DEV + 1, d:2 * d]
    g0, g1 = norm_g[0:1], norm_g[1:2]

    lay = dict(ev_w_in=_Sharded((d, ev_in), True), ev_w_out=_Sharded((ev_mix, d), False),
               od_w_in=_Sharded((d, 3 * sgu_w), True), od_w_out=_Sharded((sgu_w, d), False))
    big = list(lay)
    chip_arr = jnp.reshape(chip, (1,)).astype(jnp.int32)
    core_arr = jnp.reshape(mc, (1,)).astype(jnp.int32)
    own = {n: _cast_into_full(weights[n][0], lay[n], chip_arr, f"cast_{n}") for n in big}
    gather_ev_out = _gather_comm([own["ev_w_out"]], [lay["ev_w_out"]], forward_half_way=True)
    layer1 = ["od_w_in", "od_w_out"]
    gather_layer1 = _gather_comm([own[n] for n in layer1], [lay[n] for n in layer1], forward_half_way=True)

    def reduce_start(g, n):
        theirs = _pair_exchange([g], [lay[n]], f"pair_exchange_{n}")[0]
        psum = _pair_sum(g, theirs, lay[n], core_arr, f"pair_sum_{n}")
        return psum, _scatter_comm([psum], [lay[n]])

    h0, w_ev_in = _adaln_fwd(x2d, g0, shift0, scale0, tr, "adaln0_fwd",
                             comm=_gather_comm([own["ev_w_in"]], [lay["ev_w_in"]]), out_rows=n_ext)
    w_full = {"ev_w_in": w_ev_in}
    h0e = _adaln_fwd(ctx2d, g0, shift_c, scale_c, tr, "adaln0_ctx_fwd", into=h0, row0=n_lat)
    tm_e = _pick(n_ext, (1408, 768, 640, 512, 256, 128))
    tk_e = _pick(n_ext, (768, 640, 512, 256, 128))
    tm_l = _pick(n_lat, (1024, 512, 256, 128))
    p0, *gathered = _mm(h0e, w_full["ev_w_in"], name="mm_ev_in", tm=tm_e, tn=_pick(ev_in, (1408, 512, 256, 128)), tk=d,
                        out_dtype=BF16, comm=gather_ev_out)
    w_full["ev_w_out"] = gathered[0]
    cos_i, sin_s = _rope_tables(n_lat, n_ctx)
    q_hat, k_all, v_all = _qk_prep(p0, cos_i, sin_s, ev_q_norm, ev_k_norm, dims, tr)
    tq = _pick(n_lat, (256, 128))
    tkk = _pick(n_ext, (1408, 640, 512, 256, 128))
    attn, lse, *gathered = _flash_fwd(q_hat, k_all, v_all, n_lat, dims, tq, tkk, comm=gather_layer1)
    w_full.update(zip(layer1, gathered))
    mix, yc = _mix_fwd(attn, p0, dw_w_pad, ev_dw_b, ev_ln_g, ev_ln_b, dims, tr)
    o0, x1 = _mm(mix, w_full["ev_w_out"], name="mm_ev_out", tm=tm_l, tn=_pick(d, (1024, 512, 256)),
                 tk=ev_mix, out_dtype=BF16, res=x2d, gate=gate0)

    h1 = _adaln_fwd(x1, g1, shift1, scale1, tr, "adaln1_fwd")
    p1 = _mm(h1, w_full["od_w_in"], name="mm_od_in", tm=tm_l, tn=_pick(3 * sgu_w, (1536, 512, 256, 128)), tk=d,
             out_dtype=BF16)
    ws_b = od_ws[0].astype(BF16)
    ws_t_b = jnp.swapaxes(od_ws[0], 1, 2).astype(BF16)
    bs_t = od_bs[0].T
    m1 = _sgu_fwd(p1, od_ln_g_full, od_ln_b_full, ws_b, bs_t, tr)
    o1, x2 = _mm(m1, w_full["od_w_out"], name="mm_od_out", tm=_pick(n_lat, (512, 256, 128)),
                 tn=_pick(d, (2048, 1024, 512, 256)),
                 tk=sgu_w, out_dtype=BF16, res=x1, gate=gate1)

    dx2, do1, loss_cols, d_final_g, dgate1 = _final_loss(x2, tgt2d, final_g[None, :], o1, gate1, tr)
    loss = lax.psum(0.5 / d * jnp.sum(loss_cols), ("x", "y", "c"))

    tk_l = _pick(n_lat, (1024, 512, 256, 128))
    psums, slots = {}, {}
    tk_nt = (1408, 1024, 768, 512, 256, 128)
    g_od_w_out = _mm(m1, do1, name="mm_od_out_dw", ta=True, tm=_pick(sgu_w, (2048, 1024, 512, 256)),
                     tn=_pick(d, (1024, 512, 256)), tk=tk_l, out_dtype=BF16)
    psums["od_w_out"], sc = reduce_start(g_od_w_out, "od_w_out")
    dm1, slots["od_w_out"] = _mm(do1, w_full["od_w_out"], name="mm_od_out_dx", tb=True, tm=tm_l,
                                 tn=_pick(sgu_w, (2048, 1024, 512, 256)), tk=d, out_dtype=BF16, comm=sc)
    dp1, d_ws, d_bs_t, d_od_ln_g, d_od_ln_b = _sgu_bwd(dm1, p1, od_ln_g_full, od_ln_b_full, ws_b, ws_t_b, bs_t, tr)
    small1 = [dgate1, d_final_g, d_od_ln_g, d_od_ln_b, d_ws, d_bs_t.T]
    g_od_w_in, g_small1 = _mm(h1, dp1, name="mm_od_in_dw", ta=True, tm=_pick(d, (2048, 1024, 512, 256)),
                              tn=_pick(3 * sgu_w, (1536, 768, 512, 384, 256, 128)), tk=tk_l, out_dtype=BF16,
                              comm=_allgather_comm(_pack(small1)))
    psums["od_w_in"], sc = reduce_start(g_od_w_in, "od_w_in")
    dh1, slots["od_w_in"] = _mm(dp1, w_full["od_w_in"], name="mm_od_in_dx", tb=True, tm=tm_l,
                                tn=_pick(d, (2048, 1024, 512, 256)), tk=_pick(3 * sgu_w, tk_nt), out_dtype=BF16, comm=sc)
    zero_d = jnp.zeros((1, d), F32)
    b1 = _adaln_bwd(x1, dh1, 0, g1, scale1, zero_d, tr, "adaln1_bwd", dres=dx2, o_prev=o0, gate_prev=gate0)
    dx1, do0, dgate0 = b1["dx"], b1["do_prev"], b1["dgate_prev"]

    g_ev_w_out = _mm(mix, do0, name="mm_ev_out_dw", ta=True, tm=_pick(ev_mix, (2048, 1024, 512, 256)),
                     tn=_pick(d, (1024, 512, 256)), tk=tk_l, out_dtype=BF16)
    psums["ev_w_out"], sc = reduce_start(g_ev_w_out, "ev_w_out")
    dmix, slots["ev_w_out"] = _mm(do0, w_full["ev_w_out"], name="mm_ev_out_dx", tb=True, tm=tm_l,
                                  tn=_pick(ev_mix, (2048, 1024, 512, 256)), tk=d, out_dtype=BF16, comm=sc)
    dattn, dza, dzb, dyc, d_ev_ln_g, d_ev_ln_b, d_dw_b = _mix_bwd_pointwise(
        dmix, attn, p0, yc, ev_ln_g, ev_ln_b, dims, tr, n_ext)
    dab, d_dw_w_pad = _conv_bwd(dyc, p0, dw_w_pad, dims, tr, n_ext)
    dq_hat, dk_hat, dv_all = _flash_bwd(q_hat, k_all, v_all, dattn, attn, lse, n_lat, dims, tq, tkk)
    dkvq, d_q_norm, d_k_norm = _qk_prep_bwd(dq_hat, dk_hat, dv_all, p0, cos_i, sin_s, ev_q_norm, ev_k_norm,
                                            dims, tr, n_lat)
    dp0 = jnp.concatenate([dkvq, dza, dab, dzb], axis=1)
    small2 = [b1["dshift"], b1["dscale"], dgate0, b1["dg"], d_q_norm, d_k_norm, d_dw_w_pad[:conv_w], d_dw_b,
              d_ev_ln_g, d_ev_ln_b]
    g_ev_w_in, g_small2 = _mm(h0e, dp0, name="mm_ev_in_dw", ta=True, tm=_pick(d, (2048, 1024, 512, 256)),
                              tn=_pick(ev_in, (1408, 768, 512, 256, 128)), tk=tk_e, out_dtype=BF16,
                              comm=_allgather_comm(_pack(small2)))
    psums["ev_w_in"], sc = reduce_start(g_ev_w_in, "ev_w_in")
    dh0, slots["ev_w_in"] = _mm(dp0, w_full["ev_w_in"], name="mm_ev_in_dx", tb=True, tm=tm_e,
                                tn=_pick(d, (2048, 1024, 512, 256)), tk=_pick(ev_in, tk_nt), out_dtype=BF16, comm=sc)
    lays = [lay[n] for n in big]
    halves = [_chip_sum(psums[n], slots[n], lay[n], chip_arr, core_arr, f"chip_sum_{n}") for n in big]
    bc = _adaln_bwd(ctx2d, dh0, n_lat, g0, scale_c, zero_d, tr, "adaln0_ctx_bwd", comm=_share_comm(halves, lays))
    g_big = dict(zip(big, bc["carried"]))
    b0 = _adaln_bwd(x2d, dh0, 0, g0, scale0, bc["dg"], tr, "adaln0_bwd", dres=dx1)
    grad_x = b0["dx"]

    small3 = [b0["dshift"], b0["dscale"], bc["dshift"], bc["dscale"], b0["dg"]]
    g_small3 = _allgather_small(_pack(small3), "gather_small_grads")

    def totals_and_rows(gathered, parts, n_rows, name):
        shapes = [a.shape for a in parts]
        tot = _unpack(_sum_slots(gathered, name), shapes)
        return tot, [r[:, 0, :] for r in _unpack(gathered, shapes[:n_rows])]

    (t_dgate1, t_fg, t_oln_g, t_oln_b, t_ws, t_bs), (dgate1_rows,) = totals_and_rows(
        g_small1, small1, 1, "sum_small_grads1")
    ((t_dshift1, t_dscale1, t_dgate0, t_g1, t_qn, t_kn, t_dw_w, t_dw_b, t_eln_g, t_eln_b),
     (dshift1_rows, dscale1_rows, dgate0_rows)) = totals_and_rows(g_small2, small2, 3, "sum_small_grads2")
    (t_dshift0, t_dscale0, t_dshift_c, t_dscale_c, t_g0), (dshift0_rows, dscale0_rows) = totals_and_rows(
        g_small3, small3, 2, "sum_small_grads3")
    zeros_d = jnp.zeros((1, d), F32)
    t_dmodc = jnp.concatenate([t_dshift_c, t_dscale_c, zeros_d], axis=1)
    t_dmod0 = jnp.concatenate([t_dshift0, t_dscale0, t_dgate0], axis=1)
    t_dmod1 = jnp.concatenate([t_dshift1, t_dscale1, t_dgate1], axis=1)
    dmod0_rows = jnp.concatenate([dshift0_rows, dscale0_rows, dgate0_rows], axis=1)
    dmod1_rows = jnp.concatenate([dshift1_rows, dscale1_rows, dgate1_rows], axis=1)
    pad_rows = jnp.zeros((MOD_ROWS - N_DEV - 1, 3 * d), F32)
    dm_l0 = jnp.concatenate([dmod0_rows, t_dmodc, pad_rows], axis=0)
    dm_l1 = jnp.concatenate([dmod1_rows, jnp.zeros((MOD_ROWS - N_DEV, 3 * d), F32)], axis=0)
    dm_shard = lax.dynamic_slice_in_dim(jnp.stack([dm_l0, dm_l1]), chip * wa, wa, axis=2)
    g_ada_w, dsc = _mod_bwd(c_rows_t, dm_shard, ada_w)
    g_dsc = _allgather_small(_pack([dsc[0]]), "gather_cctx")
    g_c_ctx = _cctx_grad(g_dsc, _pack([c_ctx])).reshape(-1)[:d]
    g_ada_b = jnp.stack([t_dmod0[0] + t_dmodc[0], t_dmod1[0]])

    grads = dict(
        c_ctx=g_c_ctx, ada_w=g_ada_w, ada_b=g_ada_b, norm_g=jnp.concatenate([t_g0, t_g1], axis=0),
        ev_w_in=g_big["ev_w_in"][None], ev_q_norm=t_qn, ev_k_norm=t_kn,
        ev_dw_w=lax.dynamic_slice_in_dim(t_dw_w, chip * dwc, dwc, axis=1)[None], ev_dw_b=t_dw_b,
        ev_ln_g=t_eln_g, ev_ln_b=t_eln_b, ev_w_out=g_big["ev_w_out"][None], od_w_in=g_big["od_w_in"][None],
        od_ln_g=lax.dynamic_slice_in_dim(t_oln_g, chip * lnc, lnc, axis=1),
        od_ln_b=lax.dynamic_slice_in_dim(t_oln_b, chip * lnc, lnc, axis=1),
        od_ws=t_ws[None], od_bs=t_bs[None], od_w_out=g_big["od_w_out"][None], final_g=t_fg[0])
    grads = {n: grads[n].reshape(weights[n].shape) for n in order}

    delta, new_m, new_v = {}, {}, {}
    large = ("ada_w", "ev_w_in", "ev_w_out", "od_w_in", "od_w_out")
    for n in large:
        shp = weights[n].shape
        as2d = lambda a: a.reshape(-1, shp[-1])
        outs = _adamw(as2d(weights[n]), as2d(grads[n]), as2d(mom_m[n]), as2d(mom_v[n]), f"adamw_{n}",
                      copy_grad=n in big)
        delta[n], new_m[n], new_v[n] = (o.reshape(shp) for o in outs[:3])
        if n in big:
            grads[n] = outs[3].reshape(shp)
    rest_names = [n for n in order if n not in large]
    rest_shapes = [weights[n].shape for n in rest_names]
    dl, nm, nv = _adamw(_pack([weights[n] for n in rest_names]), _pack([grads[n] for n in rest_names]),
                        _pack([mom_m[n] for n in rest_names]), _pack([mom_v[n] for n in rest_names]), "adamw_small")
    for n, a, b_, c_ in zip(rest_names, _unpack(dl, rest_shapes), _unpack(nm, rest_shapes), _unpack(nv, rest_shapes)):
        delta[n], new_m[n], new_v[n] = a, b_, c_

    return (loss, grad_x[None], *[grads[n] for n in order], *[delta[n] for n in order],
            *[new_m[n] for n in order], *[new_v[n] for n in order])
```

```python
import math

import jax
import jax.numpy as jnp
from jax import lax
from jax.experimental import pallas as pl
from jax.experimental.pallas import tpu as pltpu

F32 = jnp.float32
BF16 = jnp.bfloat16
EPS = 1e-6
GRID_W = 64
ROPE_THETA = 10000.0
HEAD_DIM = 128
N_KV_HEADS = 2
CONV_HALO = 16
LANES = 128
SUBLANES = 8
STENCIL_ROWS = 32
STENCIL_CHAINS = 4
REDUCE_ROWS = 32
N_DEV = 8
N_CHIP = 4
MOD_ROWS = 16
PACK_ROWS = 64
ADAM_LR, ADAM_B1, ADAM_B2, ADAM_EPS, ADAM_WD, ADAM_STEP = 0.001, 0.9, 0.999, 1e-08, 0.01, 10
VMEM_LIMIT = 56 * 1024 * 1024
MESH = pl.DeviceIdType.MESH
ANY = pl.BlockSpec(memory_space=pl.ANY)
VMEM_SPEC = pl.BlockSpec(memory_space=pltpu.VMEM)
CHIP_DELTAS = ((1, 0), (0, 1), (1, 1))


def _cp(sem=None):
    return pltpu.CompilerParams(dimension_semantics=sem, vmem_limit_bytes=VMEM_LIMIT)


def _pick(n, cands):
    for c in cands:
        if n % c == 0:
            return c
    raise ValueError(f"no tile for {n} in {cands}")


def _sigmoid(x):
    return 1.0 / (1.0 + jnp.exp(-x))


def _silu(x):
    return x * _sigmoid(x)


def _silu_and_grad(x):
    s = _sigmoid(x)
    y = x * s
    return y, s + y * (1.0 - s)


def _dsilu(x):
    return _silu_and_grad(x)[1]


_GELU_C = math.sqrt(2.0 / math.pi)
_GELU_A = 0.044715


def _gelu_and_grad(x):
    x2 = x * x
    t = jnp.tanh(x * (_GELU_C + (_GELU_C * _GELU_A) * x2))
    h = 0.5 + 0.5 * t
    return x * h, h + x * (1.0 - t * t) * (0.5 * _GELU_C + (1.5 * _GELU_C * _GELU_A) * x2)


def _gelu(x):
    x2 = x * x
    return x * (0.5 + 0.5 * jnp.tanh(x * (_GELU_C + (_GELU_C * _GELU_A) * x2)))


def _vec(d):
    return pl.BlockSpec((1, d), lambda *_: (0, 0))


def _cat(refs):
    parts = [r[...].astype(F32) for r in refs]
    return parts[0] if len(parts) == 1 else jnp.concatenate(parts, axis=1)


def _col_specs(rows, off, width, cb, row_map):
    assert off % cb == 0 and width % cb == 0
    return [pl.BlockSpec((rows, cb), (lambda *g, _c=off // cb + t: (row_map(*g), _c))) for t in range(width // cb)]


def _my_pos():
    return lax.axis_index("x"), lax.axis_index("y"), lax.axis_index("c")


def _allgather_small(x, name):
    r, c = x.shape

    def body(x_ref, out_ref, send_sems, recv_sems, local_sem):
        mx, my, mc = _my_pos()
        me = 4 * mx + 2 * my + mc
        mine = pltpu.make_async_copy(x_ref, out_ref.at[me], local_sem)
        mine.start()
        deltas = [(dx, dy, dc) for dx in (0, 1) for dy in (0, 1) for dc in (0, 1) if (dx, dy, dc) != (0, 0, 0)]
        sends = []
        for k, (dx, dy, dc) in enumerate(deltas):
            px, py, pc = (mx + dx) % 2, (my + dy) % 2, (mc + dc) % 2
            cp = pltpu.make_async_remote_copy(
                src_ref=x_ref, dst_ref=out_ref.at[me], send_sem=send_sems.at[k], recv_sem=recv_sems.at[k],
                device_id=(px, py, pc), device_id_type=MESH)
            cp.start()
            sends.append(cp)
        for k, (dx, dy, dc) in enumerate(deltas):
            px, py, pc = (mx + dx) % 2, (my + dy) % 2, (mc + dc) % 2
            peer = 4 * px + 2 * py + pc
            pltpu.make_async_remote_copy(
                src_ref=x_ref, dst_ref=out_ref.at[peer], send_sem=send_sems.at[k], recv_sem=recv_sems.at[k],
                device_id=(px, py, pc), device_id_type=MESH).wait_recv()
        for cp in sends:
            cp.wait_send()
        mine.wait()

    return pl.pallas_call(
        body, name=name,
        out_shape=jax.ShapeDtypeStruct((N_DEV, r, c), x.dtype),
        in_specs=[VMEM_SPEC], out_specs=VMEM_SPEC,
        scratch_shapes=[pltpu.SemaphoreType.DMA((N_DEV - 1,)), pltpu.SemaphoreType.DMA((N_DEV - 1,)),
                        pltpu.SemaphoreType.DMA],
        compiler_params=pltpu.CompilerParams(vmem_limit_bytes=VMEM_LIMIT),
    )(x)


class _Sharded:
    def __init__(self, full_shape, by_cols):
        self.full = full_shape
        self.by_cols = by_cols
        rows, cols = full_shape
        if by_cols:
            self.shard, self.half, self.halves = (rows, cols // N_CHIP), (rows // 2, cols // N_CHIP), (rows // 2, cols)
        else:
            self.shard, self.half, self.halves = (rows // N_CHIP, cols), (rows // N_CHIP, cols // 2), (rows, cols // 2)

    def region(self, ref, s, h):
        if self.by_cols:
            return ref.at[pl.ds(h * self.half[0], self.half[0]), pl.ds(s * self.shard[1], self.shard[1])]
        return ref.at[pl.ds(s * self.shard[0], self.shard[0]), pl.ds(h * self.half[1], self.half[1])]

    def halves_of_full(self, ref, h):
        if self.by_cols:
            return ref.at[pl.ds(h * self.halves[0], self.halves[0]), :]
        return ref.at[:, pl.ds(h * self.halves[1], self.halves[1])]

    def region_in_halves(self, ref, s):
        if self.by_cols:
            return ref.at[:, pl.ds(s * self.shard[1], self.shard[1])]
        return ref.at[pl.ds(s * self.shard[0], self.shard[0]), :]

    def half_of_shard(self, ref, h):
        if self.by_cols:
            return ref.at[pl.ds(h * self.half[0], self.half[0]), :]
        return ref.at[:, pl.ds(h * self.half[1], self.half[1])]


def _row_tile(rows, row_bytes):
    for t in (512, 256, 128, 64, 32, 16):
        if rows % t == 0 and t * row_bytes <= 2 * 1024 * 1024:
            return t
    return 16


def _cast_into_full(w_shard, lay, chip_arr, name):
    r, c = lay.shard
    tr = _row_tile(r, c * 4)
    nt = r // tr

    def body(chip_ref, w_ref, o_ref):
        o_ref[...] = w_ref[...].astype(BF16)

    if lay.by_cols:
        out_map = lambda i, chip_ref: (i, chip_ref[0])
    else:
        out_map = lambda i, chip_ref: (chip_ref[0] * nt + i, 0)
    return pl.pallas_call(
        body, name=name,
        grid_spec=pltpu.PrefetchScalarGridSpec(
            num_scalar_prefetch=1, grid=(nt,),
            in_specs=[pl.BlockSpec((tr, c), lambda i, chip_ref: (i, 0))],
            out_specs=pl.BlockSpec((tr, c), out_map)),
        out_shape=jax.ShapeDtypeStruct(lay.full, BF16), compiler_params=_cp(("parallel",)))(chip_arr, w_shard)


class _Carried:
    def __init__(self, ins, out_shapes, aliases, sem_shape, start, finish, middle=None):
        self.ins, self.out_shapes, self.aliases, self.sem_shape = list(ins), list(out_shapes), dict(aliases), sem_shape
        self.start, self.finish, self.middle = start, finish, middle

    def scratch(self):
        return [pltpu.SemaphoreType.DMA(self.sem_shape), pltpu.SemaphoreType.DMA(self.sem_shape)]

    def split(self, in_refs, out_refs, scratch_refs):
        ni, no = len(self.ins), len(self.out_shapes)
        return in_refs[len(in_refs) - ni:], out_refs[len(out_refs) - no:], scratch_refs[-2], scratch_refs[-1]


def _call(body, *, name, grid, in_specs, out_specs, out_shape, args, sem, comm=None, aliases=None):
    if comm is None:
        return pl.pallas_call(body, name=name, grid=grid, in_specs=in_specs, out_specs=out_specs,
                              out_shape=out_shape, input_output_aliases=aliases or {},
                              compiler_params=_cp(sem))(*args)
    n_in, n_out, n_ci, n_co = len(in_specs), len(out_specs), len(comm.ins), len(comm.out_shapes)

    def carrying(*refs):
        in_refs, out_refs = refs[:n_in + n_ci], refs[n_in + n_ci:n_in + n_ci + n_out + n_co]
        carried = comm.split(in_refs, out_refs, refs[n_in + n_ci + n_out + n_co:])
        first, last = None, None
        for axis, extent in enumerate(grid):
            at0, at1 = pl.program_id(axis) == 0, pl.program_id(axis) == extent - 1
            first = at0 if first is None else jnp.logical_and(first, at0)
            last = at1 if last is None else jnp.logical_and(last, at1)

        @pl.when(first)
        def _():
            comm.start(*carried)

        if comm.middle is not None:
            half = pl.program_id(0) == grid[0] // 2
            for axis in range(1, len(grid)):
                half = jnp.logical_and(half, pl.program_id(axis) == 0)

            @pl.when(half)
            def _():
                comm.middle(*carried)

        body(*in_refs[:n_in], *out_refs[:n_out])

        @pl.when(last)
        def _():
            comm.finish(*carried)

    return pl.pallas_call(
        carrying, name=name, grid=grid, in_specs=list(in_specs) + [ANY] * n_ci,
        out_specs=list(out_specs) + [ANY] * n_co, out_shape=list(out_shape) + comm.out_shapes,
        scratch_shapes=comm.scratch(),
        input_output_aliases={**(aliases or {}), **{n_in + s: n_out + d for s, d in comm.aliases.items()}},
        compiler_params=_cp(("arbitrary",) * len(grid)))(*args, *comm.ins)


def _gather_comm(fulls, layouts, forward_half_way=False):
    n = len(fulls)

    def ici(ins, outs, send_sems, recv_sems, a, j, landed=False):
        mx, my, mc = _my_pos()
        dx, dy = CHIP_DELTAS[j]
        px, py = (mx + dx) % 2, (my + dy) % 2
        src_chip = 2 * px + py if landed else 2 * mx + my
        return pltpu.make_async_remote_copy(
            src_ref=layouts[a].region(ins[a], src_chip, mc), dst_ref=layouts[a].region(outs[a], src_chip, mc),
            send_sem=send_sems.at[a, j], recv_sem=recv_sems.at[a, j], device_id=(px, py, mc), device_id_type=MESH)

    def d2d(ins, outs, send_sems, recv_sems, a, j, landed=False):
        mx, my, mc = _my_pos()
        dx, dy = CHIP_DELTAS[j]
        other = 2 * ((mx + dx) % 2) + (my + dy) % 2
        half = 1 - mc if landed else mc
        region = layouts[a].region(outs[a], other, half)
        return pltpu.make_async_remote_copy(
            src_ref=region, dst_ref=region, send_sem=send_sems.at[a, 3 + j], recv_sem=recv_sems.at[a, 3 + j],
            device_id=(mx, my, 1 - mc), device_id_type=MESH)

    pairs = [(a, j) for a in range(n) for j in range(3)]

    def start(*r):
        for a, j in pairs:
            ici(*r, a, j).start()

    def forward(*r):
        for a, j in pairs:
            ici(*r, a, j, landed=True).wait_recv()
            d2d(*r, a, j).start()

    def drain(*r):
        for a, j in pairs:
            d2d(*r, a, j, landed=True).wait_recv()
        for a, j in pairs:
            ici(*r, a, j).wait_send()
            d2d(*r, a, j).wait_send()

    def finish(*r):
        forward(*r)
        drain(*r)

    out_shapes = [jax.ShapeDtypeStruct(lay.full, BF16) for lay in layouts]
    if forward_half_way:
        return _Carried(fulls, out_shapes, {a: a for a in range(n)}, (n, 6), start, drain, middle=forward)
    return _Carried(fulls, out_shapes, {a: a for a in range(n)}, (n, 6), start, finish)


def _pair_exchange(grads, layouts, name):
    n = len(grads)

    def body(*refs):
        ins, outs = refs[:n], refs[n:2 * n]
        send_sems, recv_sems = refs[2 * n:]
        mx, my, mc = _my_pos()
        copies = []
        for a, lay in enumerate(layouts):
            cp = pltpu.make_async_remote_copy(
                src_ref=lay.halves_of_full(ins[a], 1 - mc), dst_ref=outs[a],
                send_sem=send_sems.at[a], recv_sem=recv_sems.at[a],
                device_id=(mx, my, 1 - mc), device_id_type=MESH)
            cp.start()
            copies.append(cp)
        for cp in copies:
            cp.wait()

    return pl.pallas_call(
        body, name=name,
        out_shape=[jax.ShapeDtypeStruct(lay.halves, g.dtype) for lay, g in zip(layouts, grads)],
        in_specs=[ANY] * n, out_specs=[ANY] * n,
        scratch_shapes=[pltpu.SemaphoreType.DMA((n,)), pltpu.SemaphoreType.DMA((n,))],
    )(*grads)


def _pair_sum(g, theirs, lay, core_arr, name):
    r, c = lay.halves
    tr = _row_tile(r, c * 4)
    nt = r // tr

    def body(core_ref, g_ref, t_ref, o_ref):
        o_ref[...] = (g_ref[...].astype(F32) + t_ref[...].astype(F32)).astype(BF16)

    if lay.by_cols:
        g_map = lambda i, core_ref: (core_ref[0] * nt + i, 0)
    else:
        g_map = lambda i, core_ref: (i, core_ref[0])
    plain = pl.BlockSpec((tr, c), lambda i, core_ref: (i, 0))
    return pl.pallas_call(
        body, name=name,
        grid_spec=pltpu.PrefetchScalarGridSpec(
            num_scalar_prefetch=1, grid=(nt,), in_specs=[pl.BlockSpec((tr, c), g_map), plain], out_specs=plain),
        out_shape=jax.ShapeDtypeStruct((r, c), BF16), compiler_params=_cp(("parallel",)))(core_arr, g, theirs)


def _scatter_comm(pair_sums, layouts):
    n = len(pair_sums)

    def copy(ins, outs, send_sems, recv_sems, a, j):
        mx, my, mc = _my_pos()
        dx, dy = CHIP_DELTAS[j]
        px, py = (mx + dx) % 2, (my + dy) % 2
        return pltpu.make_async_remote_copy(
            src_ref=layouts[a].region_in_halves(ins[a], 2 * px + py), dst_ref=outs[a].at[j],
            send_sem=send_sems.at[a, j], recv_sem=recv_sems.at[a, j], device_id=(px, py, mc), device_id_type=MESH)

    pairs = [(a, j) for a in range(n) for j in range(3)]

    def start(*r):
        for a, j in pairs:
            copy(*r, a, j).start()

    def finish(*r):
        for a, j in pairs:
            copy(*r, a, j).wait()

    return _Carried(pair_sums, [jax.ShapeDtypeStruct((3,) + lay.half, BF16) for lay in layouts], {}, (n, 3),
                    start, finish)


def _chip_sum(pair_sum, slots, lay, chip_arr, core_arr, name):
    r, c = lay.half
    tr = _row_tile(r, c * 4)
    nt = r // tr

    def body(chip_ref, core_ref, s_ref, slot_ref, o_ref):
        acc = s_ref[...].astype(F32)
        for j in range(3):
            acc = acc + slot_ref[j].astype(F32)
        o_ref[...] = acc

    if lay.by_cols:
        s_map = lambda i, chip_ref, core_ref: (i, chip_ref[0])
        o_map = lambda i, chip_ref, core_ref: (core_ref[0] * nt + i, 0)
    else:
        s_map = lambda i, chip_ref, core_ref: (chip_ref[0] * nt + i, 0)
        o_map = lambda i, chip_ref, core_ref: (i, core_ref[0])
    return pl.pallas_call(
        body, name=name,
        grid_spec=pltpu.PrefetchScalarGridSpec(
            num_scalar_prefetch=2, grid=(nt,),
            in_specs=[pl.BlockSpec((tr, c), s_map),
                      pl.BlockSpec((3, tr, c), lambda i, chip_ref, core_ref: (0, i, 0))],
            out_specs=pl.BlockSpec((tr, c), o_map)),
        out_shape=jax.ShapeDtypeStruct(lay.shard, F32), compiler_params=_cp(("parallel",)))(
            chip_arr, core_arr, pair_sum, slots)


def _share_comm(bufs, layouts):
    n = len(bufs)

    def copy(ins, outs, send_sems, recv_sems, a, landed=False):
        mx, my, mc = _my_pos()
        half = 1 - mc if landed else mc
        return pltpu.make_async_remote_copy(
            src_ref=layouts[a].half_of_shard(ins[a], half), dst_ref=layouts[a].half_of_shard(outs[a], half),
            send_sem=send_sems.at[a], recv_sem=recv_sems.at[a], device_id=(mx, my, 1 - mc), device_id_type=MESH)

    def start(*r):
        for a in range(n):
            copy(*r, a).start()

    def finish(*r):
        for a in range(n):
            copy(*r, a, landed=True).wait_recv()
        for a in range(n):
            copy(*r, a).wait_send()

    return _Carried(bufs, [jax.ShapeDtypeStruct(lay.shard, F32) for lay in layouts], {a: a for a in range(n)},
                    (n,), start, finish)


def _allgather_comm(x):
    deltas = [(dx, dy, dc) for dx in (0, 1) for dy in (0, 1) for dc in (0, 1) if (dx, dy, dc) != (0, 0, 0)]
    local = len(deltas)

    def remote(ins, outs, send_sems, recv_sems, k, landed=False):
        mx, my, mc = _my_pos()
        dx, dy, dc = deltas[k]
        px, py, pc = (mx + dx) % 2, (my + dy) % 2, (mc + dc) % 2
        slot = 4 * px + 2 * py + pc if landed else 4 * mx + 2 * my + mc
        return pltpu.make_async_remote_copy(
            src_ref=ins[0], dst_ref=outs[0].at[slot], send_sem=send_sems.at[k], recv_sem=recv_sems.at[k],
            device_id=(px, py, pc), device_id_type=MESH)

    def mine(ins, outs, send_sems, recv_sems):
        mx, my, mc = _my_pos()
        return pltpu.make_async_copy(ins[0], outs[0].at[4 * mx + 2 * my + mc], send_sems.at[local])

    def start(*r):
        mine(*r).start()
        for k in range(len(deltas)):
            remote(*r, k).start()

    def finish(*r):
        for k in range(len(deltas)):
            remote(*r, k, landed=True).wait_recv()
        for k in range(len(deltas)):
            remote(*r, k).wait_send()
        mine(*r).wait()

    return _Carried([x], [jax.ShapeDtypeStruct((N_DEV,) + x.shape, x.dtype)], {}, (N_DEV,), start, finish)


def _sum_slots(x, name):
    s, r, c = x.shape
    tr = _pick(r, (256, 128, 64, 32, 16, 8))

    def body(x_ref, o_ref):
        acc = x_ref[0]
        for k in range(1, s):
            acc = acc + x_ref[k]
        o_ref[...] = acc

    return pl.pallas_call(body, name=name, grid=(r // tr,),
                          in_specs=[pl.BlockSpec((s, tr, c), lambda i: (0, i, 0))],
                          out_specs=pl.BlockSpec((tr, c), lambda i: (i, 0)),
                          out_shape=jax.ShapeDtypeStruct((r, c), F32), compiler_params=_cp(("parallel",)))(x)


def _adamw(w, g, m, v, name, copy_grad=False):
    r, c = w.shape
    tr = _pick(r, (256, 128, 64, 32, 16, 8))
    bc1 = 1.0 - ADAM_B1 ** ADAM_STEP
    bc2 = 1.0 - ADAM_B2 ** ADAM_STEP
    n_out = 4 if copy_grad else 3

    def body(w_ref, g_ref, m_ref, v_ref, d_ref, nm_ref, nv_ref, *g_copy):
        gv = g_ref[...]
        nm = ADAM_B1 * m_ref[...] + (1.0 - ADAM_B1) * gv
        nv = ADAM_B2 * v_ref[...] + (1.0 - ADAM_B2) * (gv * gv)
        d_ref[...] = -ADAM_LR * ((nm / bc1) / (jnp.sqrt(nv / bc2) + ADAM_EPS) + ADAM_WD * w_ref[...])
        nm_ref[...] = nm
        nv_ref[...] = nv
        if copy_grad:
            g_copy[0][...] = gv

    spec = pl.BlockSpec((tr, c), lambda i: (i, 0))
    shp = jax.ShapeDtypeStruct((r, c), F32)
    return _call(body, name=name, grid=(r // tr,), in_specs=[spec] * 4, out_specs=[spec] * n_out,
                 out_shape=[shp] * n_out, args=(w, g, m, v), sem=("parallel",))


def _cctx_grad(parts, c_ctx2d):
    def body(p_ref, c_ref, o_ref):
        tot = ((p_ref[0] + p_ref[2]) + p_ref[4]) + p_ref[6]
        o_ref[...] = tot * _dsilu(c_ref[...])

    return pl.pallas_call(body, name="cctx_grad", in_specs=[VMEM_SPEC, VMEM_SPEC], out_specs=VMEM_SPEC,
                          out_shape=jax.ShapeDtypeStruct(c_ctx2d.shape, F32))(parts, c_ctx2d)


def _mod_fwd(c_rows_t, ada_w, ada_b_shard):
    nl, d, w = ada_w.shape
    td = _pick(d, (256, 128))
    nd = d // td

    def body(ct_ref, w_ref, b_ref, o_ref):
        i = pl.program_id(1)

        @pl.when(i == 0)
        def _():
            o_ref[0] = jnp.broadcast_to(b_ref[0], (MOD_ROWS, w))

        st = _silu(ct_ref[...])
        wv = w_ref[0]
        used = N_DEV + 1
        rows = [jnp.sum(st[:, r:r + 1] * wv, axis=0, keepdims=True) for r in range(used)]
        rows.append(jnp.zeros((MOD_ROWS - used, w), F32))
        o_ref[0] += jnp.concatenate(rows, axis=0)

    return pl.pallas_call(
        body, name="mod_fwd", grid=(nl, nd),
        in_specs=[pl.BlockSpec((td, MOD_ROWS), lambda l, i: (i, 0)),
                  pl.BlockSpec((1, td, w), lambda l, i: (l, i, 0)),
                  pl.BlockSpec((1, 1, w), lambda l, i: (l, 0, 0))],
        out_specs=pl.BlockSpec((1, MOD_ROWS, w), lambda l, i: (l, 0, 0)),
        out_shape=jax.ShapeDtypeStruct((nl, MOD_ROWS, w), F32),
        compiler_params=_cp(("parallel", "arbitrary")),
    )(c_rows_t, ada_w, ada_b_shard)


def _mod_bwd(c_rows_t, dmod, ada_w):
    nl, d, w = ada_w.shape
    td = _pick(d, (256, 128))
    ctx_row = N_DEV

    def body(ct_ref, dm_ref, w_ref, gw_ref, ds_ref):
        st = _silu(ct_ref[...])
        dm = dm_ref[0]
        acc = st[:, 0:1] * dm[0:1, :]
        for r in range(1, ctx_row + 1):
            acc = acc + st[:, r:r + 1] * dm[r:r + 1, :]
        gw_ref[0] = acc
        ds_ref[0] = jnp.sum(w_ref[0] * dm[ctx_row:ctx_row + 1, :], axis=1, keepdims=True)

    return pl.pallas_call(
        body, name="mod_bwd", grid=(nl, d // td),
        in_specs=[pl.BlockSpec((td, MOD_ROWS), lambda l, i: (i, 0)),
                  pl.BlockSpec((1, MOD_ROWS, w), lambda l, i: (l, 0, 0)),
                  pl.BlockSpec((1, td, w), lambda l, i: (l, i, 0))],
        out_specs=[pl.BlockSpec((1, td, w), lambda l, i: (l, i, 0)),
                   pl.BlockSpec((1, td, 1), lambda l, i: (l, i, 0))],
        out_shape=[jax.ShapeDtypeStruct((nl, d, w), F32), jax.ShapeDtypeStruct((nl, d, 1), F32)],
        compiler_params=_cp(("parallel", "parallel")),
    )(c_rows_t, dmod, ada_w)


def _adaln_fwd(x, g, shift, scale, tr, name, comm=None, out_rows=None, into=None, row0=0):
    r, d = x.shape
    rb0 = row0 // tr
    total = into.shape[0] if into is not None else (out_rows or r)

    def body(x_ref, g_ref, sh_ref, sc_ref, *rest):
        o_ref = rest[-1]
        xv = x_ref[...]
        rs = lax.rsqrt(jnp.mean(xv * xv, axis=-1, keepdims=True) + EPS)
        o_ref[...] = ((xv * rs * g_ref[...]) * (1.0 + sc_ref[...]) + sh_ref[...]).astype(BF16)

    spec = pl.BlockSpec((tr, d), lambda i: (i, 0))
    in_specs, args, aliases = [spec, _vec(d), _vec(d), _vec(d)], [x, g, shift, scale], None
    if into is not None:
        in_specs.append(ANY)
        args.append(into)
        aliases = {4: 0}
    outs = _call(body, name=name, grid=(r // tr,), in_specs=in_specs,
                 out_specs=[pl.BlockSpec((tr, d), lambda i: (rb0 + i, 0))],
                 out_shape=[jax.ShapeDtypeStruct((total, d), BF16)], args=args, sem=("parallel",),
                 comm=comm, aliases=aliases)
    return outs if comm else outs[0]


def _adaln_bwd(xin, dh, row0, g, scale, dg_init, tr, name, dres=None, o_prev=None, gate_prev=None, comm=None):
    r, d = xin.shape
    assert row0 % tr == 0
    rb0 = row0 // tr
    want_dx = dres is not None
    want_prev = o_prev is not None
    assert want_dx or not want_prev

    def body(*refs):
        it = iter(refs)
        x_ref, dh_ref, g_ref, sc_ref, dgi_ref = next(it), next(it), next(it), next(it), next(it)
        dres_ref = next(it) if want_dx else None
        o_ref, gp_ref = (next(it), next(it)) if want_prev else (None, None)
        dx_ref = next(it) if want_dx else None
        do_ref = next(it) if want_prev else None
        dsh_ref, dsc_ref, dg_ref = next(it), next(it), next(it)
        dgp_ref = next(it) if want_prev else None
        i = pl.program_id(0)

        @pl.when(i == 0)
        def _():
            dsh_ref[...] = jnp.zeros_like(dsh_ref)
            dsc_ref[...] = jnp.zeros_like(dsc_ref)
            dg_ref[...] = dgi_ref[...]
            if want_prev:
                dgp_ref[...] = jnp.zeros_like(dgp_ref)

        xv = x_ref[...]
        dhv = dh_ref[...].astype(F32)
        gv = g_ref[...]
        rs = lax.rsqrt(jnp.mean(xv * xv, axis=-1, keepdims=True) + EPS)
        xn = xv * rs
        dsh_ref[...] += jnp.sum(dhv, axis=0, keepdims=True)
        dsc_ref[...] += jnp.sum(dhv * (xn * gv), axis=0, keepdims=True)
        dr = dhv * (1.0 + sc_ref[...])
        dg_ref[...] += jnp.sum(dr * xn, axis=0, keepdims=True)
        if want_dx:
            gy = dr * gv
            dx = dres_ref[...] + rs * (gy - xn * jnp.mean(gy * xn, axis=-1, keepdims=True))
            dx_ref[...] = dx
            if want_prev:
                do_ref[...] = (gp_ref[...] * dx).astype(BF16)
                dgp_ref[...] += jnp.sum(dx * o_ref[...].astype(F32), axis=0, keepdims=True)

    row = pl.BlockSpec((tr, d), lambda i: (i, 0))
    in_specs = [row, pl.BlockSpec((tr, d), lambda i: (rb0 + i, 0)), _vec(d), _vec(d), _vec(d)]
    args = [xin, dh, g, scale, dg_init]
    out_specs, out_shape, names = [], [], []
    if want_dx:
        in_specs.append(row)
        args.append(dres)
    if want_prev:
        in_specs += [row, _vec(d)]
        args += [o_prev, gate_prev]
    if want_dx:
        out_specs.append(row)
        out_shape.append(jax.ShapeDtypeStruct((r, d), F32))
        names.append("dx")
    if want_prev:
        out_specs.append(row)
        out_shape.append(jax.ShapeDtypeStruct((r, d), BF16))
        names.append("do_prev")
    for nm in ("dshift", "dscale", "dg") + (("dgate_prev",) if want_prev else ()):
        out_specs.append(_vec(d))
        out_shape.append(jax.ShapeDtypeStruct((1, d), F32))
        names.append(nm)
    outs = _call(body, name=name, grid=(r // tr,), in_specs=in_specs, out_specs=out_specs, out_shape=out_shape,
                 args=args, sem=("arbitrary",), comm=comm)
    res = dict(zip(names, outs))
    if comm:
        res["carried"] = outs[len(names):]
    return res


def _mm(a, b, *, name, tm, tn, tk, ta=False, tb=False, out_dtype=F32, res=None, gate=None, comm=None):
    if ta:
        kd, m = a.shape
    else:
        m, kd = a.shape
    if tb:
        n, kd2 = b.shape
    else:
        kd2, n = b.shape
    assert kd == kd2 and m % tm == 0 and n % tn == 0 and kd % tk == 0, (a.shape, b.shape, tm, tn, tk)
    ni, nj, nk = m // tm, n // tn, kd // tk
    dn = (((0 if ta else 1,), (1 if tb else 0,)), ((), ()))
    with_res = res is not None
    n_in = 4 if with_res else 2
    n_out = 2 if with_res else 1
    n_cin = len(comm.ins) if comm else 0
    n_cout = len(comm.out_shapes) if comm else 0

    def body(*refs):
        in_refs = refs[:n_in + n_cin]
        out_refs = refs[n_in + n_cin:n_in + n_cin + n_out + n_cout]
        scratch = refs[n_in + n_cin + n_out + n_cout:]
        a_ref, b_ref = in_refs[0], in_refs[1]
        o_ref = out_refs[0]
        i, j, k = pl.program_id(0), pl.program_id(1), pl.program_id(2)
        if comm:
            carried = comm.split(in_refs, out_refs, scratch)

            @pl.when(jnp.logical_and(jnp.logical_and(i == 0, j == 0), k == 0))
            def _():
                comm.start(*carried)

            if comm.middle is not None:
                @pl.when(jnp.logical_and(jnp.logical_and(i == ni // 2, j == 0), k == 0))
                def _():
                    comm.middle(*carried)

        def emit(acc):
            o_ref[...] = acc.astype(o_ref.dtype)
            if with_res:
                out_refs[1][...] = in_refs[2][...] + in_refs[3][...] * acc

        if nk == 1:
            emit(lax.dot_general(a_ref[...], b_ref[...], dn, preferred_element_type=F32))
        else:
            acc_ref = scratch[0]

            @pl.when(k == 0)
            def _():
                acc_ref[...] = jnp.zeros_like(acc_ref)

            acc_ref[...] += lax.dot_general(a_ref[...], b_ref[...], dn, preferred_element_type=F32)

            @pl.when(k == nk - 1)
            def _():
                emit(acc_ref[...])

        if comm:
            @pl.when(jnp.logical_and(jnp.logical_and(i == ni - 1, j == nj - 1), k == nk - 1))
            def _():
                comm.finish(*carried)

    a_spec = pl.BlockSpec((tk, tm), lambda i, j, k: (k, i)) if ta else pl.BlockSpec((tm, tk), lambda i, j, k: (i, k))
    b_spec = pl.BlockSpec((tn, tk), lambda i, j, k: (j, k)) if tb else pl.BlockSpec((tk, tn), lambda i, j, k: (k, j))
    o_spec = pl.BlockSpec((tm, tn), lambda i, j, k: (i, j))
    in_specs, args = [a_spec, b_spec], [a, b]
    out_specs, out_shape = [o_spec], [jax.ShapeDtypeStruct((m, n), out_dtype)]
    if with_res:
        in_specs += [o_spec, pl.BlockSpec((1, tn), lambda i, j, k: (0, j))]
        args += [res, gate]
        out_specs.append(o_spec)
        out_shape.append(jax.ShapeDtypeStruct((m, n), F32))
    scratch_shapes = [pltpu.VMEM((tm, tn), F32)] if nk > 1 else []
    aliases = {}
    sem = ("parallel", "parallel", "arbitrary")
    if comm:
        in_specs += [ANY] * n_cin
        args += comm.ins
        out_specs += [ANY] * n_cout
        out_shape += comm.out_shapes
        scratch_shapes += comm.scratch()
        aliases = {n_in + s: n_out + d for s, d in comm.aliases.items()}
        sem = ("arbitrary", "arbitrary", "arbitrary")
    outs = pl.pallas_call(body, name=name, grid=(ni, nj, nk), in_specs=in_specs, out_specs=out_specs,
                          out_shape=out_shape, scratch_shapes=scratch_shapes, input_output_aliases=aliases,
                          compiler_params=_cp(sem))(*args)
    return outs if (with_res or comm) else outs[0]


def _swap_pairs(x):
    lane = lax.broadcasted_iota(jnp.int32, x.shape, 1)
    return jnp.where(lane % 2 == 0, pltpu.roll(x, HEAD_DIM - 1, 1), pltpu.roll(x, 1, 1))


def _qk_prep(p0, cos_i, sin_s, q_norm, k_norm, dims, tr):
    rows = p0.shape[0]
    kvw, aw, cb = dims["kv_w"], dims["attn_w"], dims["cb"]
    nkv, nq = kvw // HEAD_DIM, aw // HEAD_DIM
    scale = HEAD_DIM ** -0.5
    n_kv_specs, n_q_specs = (2 * kvw) // cb, aw // cb

    def body(*refs):
        kv_refs = refs[:n_kv_specs]
        q_refs = refs[n_kv_specs:n_kv_specs + n_q_specs]
        cos_ref, sin_ref, qn_ref, kn_ref, qo_ref, ko_ref, vo_ref = refs[n_kv_specs + n_q_specs:]
        kv = _cat(kv_refs)
        qv = _cat(q_refs)
        cs, sn = cos_ref[...], sin_ref[...]

        def norm_rope(xh, gvec):
            rs = lax.rsqrt(jnp.mean(xh * xh, axis=-1, keepdims=True) + EPS)
            xn = xh * rs * gvec
            return xn * cs + _swap_pairs(xn) * sn

        for h in range(nkv):
            sl = slice(h * HEAD_DIM, (h + 1) * HEAD_DIM)
            ko_ref[:, sl] = norm_rope(kv[:, sl], kn_ref[...]).astype(BF16)
        vo_ref[...] = kv[:, kvw:].astype(BF16)
        for h in range(nq):
            sl = slice(h * HEAD_DIM, (h + 1) * HEAD_DIM)
            qo_ref[:, sl] = (norm_rope(qv[:, sl], qn_ref[...]) * scale).astype(BF16)

    rm = lambda i: i
    in_specs = (_col_specs(tr, 0, 2 * kvw, cb, rm) + _col_specs(tr, 2 * kvw, aw, cb, rm)
                + [pl.BlockSpec((tr, HEAD_DIM), lambda i: (i, 0))] * 2 + [_vec(HEAD_DIM)] * 2)
    args = [p0] * (n_kv_specs + n_q_specs) + [cos_i, sin_s, q_norm, k_norm]
    return pl.pallas_call(
        body, name="qk_prep", grid=(rows // tr,), in_specs=in_specs,
        out_specs=[pl.BlockSpec((tr, aw), lambda i: (i, 0)), pl.BlockSpec((tr, kvw), lambda i: (i, 0)),
                   pl.BlockSpec((tr, kvw), lambda i: (i, 0))],
        out_shape=[jax.ShapeDtypeStruct((rows, aw), BF16), jax.ShapeDtypeStruct((rows, kvw), BF16),
                   jax.ShapeDtypeStruct((rows, kvw), BF16)],
        compiler_params=_cp(("parallel",)))(*args)


def _qk_prep_bwd(dq_hat, dk_hat, dv, p0, cos_i, sin_s, q_norm, k_norm, dims, tr, n_lat):
    rows = p0.shape[0]
    kvw, aw, cb = dims["kv_w"], dims["attn_w"], dims["cb"]
    nkv, nq = kvw // HEAD_DIM, aw // HEAD_DIM
    scale = HEAD_DIM ** -0.5
    n_kv_specs, n_q_specs = (2 * kvw) // cb, aw // cb
    lat_tiles = n_lat // tr

    def body(*refs):
        kv_refs = refs[:n_kv_specs]
        q_refs = refs[n_kv_specs:n_kv_specs + n_q_specs]
        (dq_ref, dk_ref, dv_ref, cos_ref, sin_ref, qn_ref, kn_ref,
         out_ref, dqn_ref, dkn_ref) = refs[n_kv_specs + n_q_specs:]
        i = pl.program_id(0)

        @pl.when(i == 0)
        def _():
            dqn_ref[...] = jnp.zeros_like(dqn_ref)
            dkn_ref[...] = jnp.zeros_like(dkn_ref)

        kv = _cat(kv_refs)
        qv = _cat(q_refs)
        cs, sn = cos_ref[...], sin_ref[...]
        is_lat = (i < lat_tiles).astype(F32)

        def head_bwd(xh, dhat, gvec):
            dn = dhat * cs + _swap_pairs(dhat * sn)
            rs = lax.rsqrt(jnp.mean(xh * xh, axis=-1, keepdims=True) + EPS)
            xn = xh * rs
            gy = dn * gvec
            dx = rs * (gy - xn * jnp.mean(gy * xn, axis=-1, keepdims=True))
            return dx, jnp.sum(dn * xn, axis=0, keepdims=True)

        dkn = jnp.zeros((1, HEAD_DIM), F32)
        for h in range(nkv):
            sl = slice(h * HEAD_DIM, (h + 1) * HEAD_DIM)
            dx, dgv = head_bwd(kv[:, sl], dk_ref[:, sl], kn_ref[...])
            out_ref[:, sl] = dx.astype(BF16)
            dkn = dkn + dgv
        dkn_ref[...] += dkn
        out_ref[:, kvw:2 * kvw] = dv_ref[...].astype(BF16)
        dqn = jnp.zeros((1, HEAD_DIM), F32)
        for h in range(nq):
            sl = slice(h * HEAD_DIM, (h + 1) * HEAD_DIM)
            dx, dgv = head_bwd(qv[:, sl], dq_ref[:, sl] * (scale * is_lat), qn_ref[...])
            out_ref[:, 2 * kvw + h * HEAD_DIM:2 * kvw + (h + 1) * HEAD_DIM] = dx.astype(BF16)
            dqn = dqn + dgv
        dqn_ref[...] += dqn

    rm = lambda i: i
    wout = 2 * kvw + aw
    in_specs = (_col_specs(tr, 0, 2 * kvw, cb, rm) + _col_specs(tr, 2 * kvw, aw, cb, rm)
                + [pl.BlockSpec((tr, aw), lambda i: (jnp.minimum(i, lat_tiles - 1), 0)),
                   pl.BlockSpec((tr, kvw), lambda i: (i, 0)), pl.BlockSpec((tr, kvw), lambda i: (i, 0)),
                   pl.BlockSpec((tr, HEAD_DIM), lambda i: (i, 0)), pl.BlockSpec((tr, HEAD_DIM), lambda i: (i, 0)),
                   _vec(HEAD_DIM), _vec(HEAD_DIM)])
    args = [p0] * (n_kv_specs + n_q_specs) + [dq_hat, dk_hat, dv, cos_i, sin_s, q_norm, k_norm]
    return pl.pallas_call(
        body, name="qk_prep_bwd", grid=(rows // tr,), in_specs=in_specs,
        out_specs=[pl.BlockSpec((tr, wout), lambda i: (i, 0)), _vec(HEAD_DIM), _vec(HEAD_DIM)],
        out_shape=[jax.ShapeDtypeStruct((rows, wout), BF16), jax.ShapeDtypeStruct((1, HEAD_DIM), F32),
                   jax.ShapeDtypeStruct((1, HEAD_DIM), F32)],
        compiler_params=_cp(("arbitrary",)))(*args)


def _stack_heads(x, g):
    return jnp.concatenate([x[:, h * HEAD_DIM:(h + 1) * HEAD_DIM] for h in range(g)], axis=0)


def _flash_fwd(q_hat, k_all, v_all, n_lat, dims, tq, tk, comm=None):
    kvw, aw = dims["kv_w"], dims["attn_w"]
    nkv = kvw // HEAD_DIM
    g = aw // kvw
    gw = g * HEAD_DIM
    n_keys = k_all.shape[0]
    ni, nj = n_lat // tq, n_keys // tk
    dn_nt = (((1,), (1,)), ((), ()))

    def body(q_ref, k_ref, v_ref, o_ref, lse_ref):
        qs = _stack_heads(q_ref[...], g)
        m = jnp.full((g * tq, 1), -1e30, F32)
        l = jnp.zeros((g * tq, 1), F32)
        acc = jnp.zeros((g * tq, HEAD_DIM), F32)
        for j in range(nj):
            kb = k_ref[pl.ds(j * tk, tk), :]
            vb = v_ref[pl.ds(j * tk, tk), :]
            s = lax.dot_general(qs, kb, dn_nt, preferred_element_type=F32)
            m_new = jnp.maximum(m, jnp.max(s, axis=-1, keepdims=True))
            alpha = jnp.exp(m - m_new)
            p = jnp.exp(s - m_new)
            l = alpha * l + jnp.sum(p, axis=-1, keepdims=True)
            acc = alpha * acc + jnp.dot(p.astype(BF16), vb, preferred_element_type=F32)
            m = m_new
        o = acc / l
        for h in range(g):
            o_ref[:, h * HEAD_DIM:(h + 1) * HEAD_DIM] = o[h * tq:(h + 1) * tq]
        lse_ref[...] = m + jnp.log(l)

    return _call(
        body, name="flash_fwd", grid=(nkv, ni),
        in_specs=[pl.BlockSpec((tq, gw), lambda h, i: (i, h)),
                  pl.BlockSpec((n_keys, HEAD_DIM), lambda h, i: (0, h)),
                  pl.BlockSpec((n_keys, HEAD_DIM), lambda h, i: (0, h))],
        out_specs=[pl.BlockSpec((tq, gw), lambda h, i: (i, h)),
                   pl.BlockSpec((g * tq, 1), lambda h, i: (h * ni + i, 0))],
        out_shape=[jax.ShapeDtypeStruct((n_lat, aw), F32), jax.ShapeDtypeStruct((nkv * ni * g * tq, 1), F32)],
        args=(q_hat, k_all, v_all), sem=("parallel", "parallel"), comm=comm)


def _flash_bwd(q_hat, k_all, v_all, do, o, lse, n_lat, dims, tq, tk):
    kvw, aw = dims["kv_w"], dims["attn_w"]
    nkv = kvw // HEAD_DIM
    g = aw // kvw
    gw = g * HEAD_DIM
    n_keys = k_all.shape[0]
    ni, nj = n_lat // tq, n_keys // tk
    dn_nt = (((1,), (1,)), ((), ()))
    dn_tn = (((0,), (0,)), ((), ()))

    def body(q_ref, k_ref, v_ref, do_ref, o_ref, lse_ref, dq_ref, dk_ref, dv_ref):
        i = pl.program_id(1)

        @pl.when(i == 0)
        def _():
            dk_ref[...] = jnp.zeros_like(dk_ref)
            dv_ref[...] = jnp.zeros_like(dv_ref)

        dos = _stack_heads(do_ref[...], g)
        qs = _stack_heads(q_ref[...], g)
        delta = jnp.sum(dos.astype(F32) * _stack_heads(o_ref[...], g), axis=-1, keepdims=True)
        lse_v = lse_ref[...]
        dq = jnp.zeros((g * tq, HEAD_DIM), F32)
        for j in range(nj):
            rows = pl.ds(j * tk, tk)
            kb, vb = k_ref[rows, :], v_ref[rows, :]
            s = lax.dot_general(qs, kb, dn_nt, preferred_element_type=F32)
            p = jnp.exp(s - lse_v)
            dp = lax.dot_general(dos, vb, dn_nt, preferred_element_type=F32)
            ds = (p * (dp - delta)).astype(BF16)
            dv_ref[rows, :] += lax.dot_general(p.astype(BF16), dos, dn_tn, preferred_element_type=F32)
            dk_ref[rows, :] += lax.dot_general(ds, qs, dn_tn, preferred_element_type=F32)
            dq = dq + jnp.dot(ds, kb, preferred_element_type=F32)
        for h in range(g):
            dq_ref[:, h * HEAD_DIM:(h + 1) * HEAD_DIM] = dq[h * tq:(h + 1) * tq]

    qspec = pl.BlockSpec((tq, gw), lambda h, i: (i, h))
    full_k = pl.BlockSpec((n_keys, HEAD_DIM), lambda h, i: (0, h))
    return pl.pallas_call(
        body, name="flash_bwd", grid=(nkv, ni),
        in_specs=[qspec, full_k, full_k, qspec, qspec, pl.BlockSpec((g * tq, 1), lambda h, i: (h * ni + i, 0))],
        out_specs=[qspec, full_k, full_k],
        out_shape=[jax.ShapeDtypeStruct((n_lat, aw), F32), jax.ShapeDtypeStruct((n_keys, kvw), F32),
                   jax.ShapeDtypeStruct((n_keys, kvw), F32)],
        compiler_params=_cp(("parallel", "arbitrary")))(q_hat, k_all, v_all, do, o, lse)


def _shifted_copies(pad_ref, sh_ref, tr):
    rows = tr + 2 * CONV_HALO - SUBLANES
    for r in range(SUBLANES):
        sh_ref[r] = pad_ref[pl.ds(r, rows), :]


def _stencil(sh_ref, w_ref, out_ref, offsets, tr, cc, init_ref=None):
    for c0 in range(0, cc, LANES):
        lanes = pl.ds(c0, LANES)
        wv = [jnp.broadcast_to(w_ref[pl.ds(k, 1), lanes], (STENCIL_ROWS, LANES)) for k in range(len(offsets))]
        if init_ref is None:
            init = jnp.zeros((STENCIL_ROWS, LANES), F32)
        else:
            init = jnp.broadcast_to(init_ref[:, lanes], (STENCIL_ROWS, LANES))

        def block(rb, carry, lanes=lanes, wv=wv, init=init):
            r0 = rb * STENCIL_ROWS
            parts = [init] + [None] * (STENCIL_CHAINS - 1)
            for k, o in enumerate(offsets):
                rows = pl.ds(SUBLANES * (o // SUBLANES) + r0, STENCIL_ROWS)
                term = sh_ref[o % SUBLANES, rows, lanes] * wv[k]
                q = k % STENCIL_CHAINS
                parts[q] = term if parts[q] is None else parts[q] + term
            acc = parts[0]
            for p_ in parts[1:]:
                acc = acc + p_
            out_ref[pl.ds(r0, STENCIL_ROWS), lanes] = acc
            return carry

        for rb in range(tr // STENCIL_ROWS):
            block(rb, 0)


def _stencil_weight_grad(sh_ref, d_ref, d_row0, dw_ref, offsets, tr, cc):
    rows_per = REDUCE_ROWS
    for c0 in range(0, cc, LANES):
        lanes = pl.ds(c0, LANES)

        def block(rb, accs, lanes=lanes):
            r0 = rb * rows_per
            dblk = d_ref[pl.ds(d_row0 + r0, rows_per), lanes]
            out = []
            for k, o in enumerate(offsets):
                prod = dblk * sh_ref[o % SUBLANES, pl.ds(SUBLANES * (o // SUBLANES) + r0, rows_per), lanes]
                part = prod[0:SUBLANES]
                for q in range(1, rows_per // SUBLANES):
                    part = part + prod[q * SUBLANES:(q + 1) * SUBLANES]
                out.append(accs[k] + part)
            return tuple(out)

        zero = jnp.zeros((SUBLANES, LANES), F32)
        accs = tuple(zero for _ in offsets)
        for rb in range(tr // rows_per):
            accs = block(rb, accs)
        for k in range(len(offsets)):
            dw_ref[pl.ds(k, 1), lanes] += jnp.sum(accs[k], axis=0, keepdims=True)


def _halo_maps(tr, n_tiles):
    per = tr // CONV_HALO
    prev = lambda i: jnp.maximum(i * per - 1, 0)
    nxt = lambda i: (i + 1) * per
    return prev, nxt


def _mix_fwd(attn, p0, dw_w, dw_b, ln_g, ln_b, dims, tr):
    n_lat, aw = attn.shape
    cc, cb, cw = dims["conv_ch"], dims["cb"], dims["conv_w"]
    off = dims["off"]
    n_tiles = n_lat // tr
    pad = cw // 2
    na, nc = aw // cb, cc // cb
    prev_map, next_map = _halo_maps(tr, n_tiles)

    def body(*refs):
        it = iter(refs)
        attn_ref = next(it)
        za = [next(it) for _ in range(na)]
        a_c = [next(it) for _ in range(nc)]
        b_c = [next(it) for _ in range(nc)]
        zb = [next(it) for _ in range(nc)]
        a_p = [next(it) for _ in range(nc)]
        b_p = [next(it) for _ in range(nc)]
        a_n = [next(it) for _ in range(nc)]
        b_n = [next(it) for _ in range(nc)]
        w_ref, db_ref, g_ref, bb_ref, mix_ref, yc_ref, ypad, ysh = (next(it) for _ in range(8))
        i = pl.program_id(0)
        ypad[pl.ds(0, CONV_HALO), :] = _cat(a_p) * _sigmoid(_cat(b_p)) * (i > 0).astype(F32)
        ypad[pl.ds(CONV_HALO, tr), :] = _cat(a_c) * _sigmoid(_cat(b_c))
        ypad[pl.ds(CONV_HALO + tr, CONV_HALO), :] = _cat(a_n) * _sigmoid(_cat(b_n)) * (i < n_tiles - 1).astype(F32)
        _shifted_copies(ypad, ysh, tr)
        _stencil(ysh, w_ref, yc_ref, [CONV_HALO - pad + k for k in range(cw)], tr, cc, init_ref=db_ref)
        acc = yc_ref[...]
        mu = jnp.mean(acc, axis=-1, keepdims=True)
        xc = acc - mu
        rs = lax.rsqrt(jnp.mean(xc * xc, axis=-1, keepdims=True) + EPS)
        nv = xc * rs * g_ref[...] + bb_ref[...]
        mix_ref[:, :aw] = (attn_ref[...] * _silu(_cat(za))).astype(BF16)
        mix_ref[:, aw:] = (_silu(nv) * _silu(_cat(zb))).astype(BF16)

    rm = lambda i: i
    in_specs = ([pl.BlockSpec((tr, aw), lambda i: (i, 0))]
                + _col_specs(tr, off["za"], aw, cb, rm) + _col_specs(tr, off["a"], cc, cb, rm)
                + _col_specs(tr, off["b"], cc, cb, rm) + _col_specs(tr, off["zb"], cc, cb, rm)
                + _col_specs(CONV_HALO, off["a"], cc, cb, prev_map) + _col_specs(CONV_HALO, off["b"], cc, cb, prev_map)
                + _col_specs(CONV_HALO, off["a"], cc, cb, next_map) + _col_specs(CONV_HALO, off["b"], cc, cb, next_map)
                + [pl.BlockSpec(dw_w.shape, lambda i: (0, 0)), _vec(cc), _vec(cc), _vec(cc)])
    args = [attn] + [p0] * (na + 7 * nc) + [dw_w, dw_b, ln_g, ln_b]
    return pl.pallas_call(
        body, name="mix_fwd", grid=(n_tiles,), in_specs=in_specs,
        out_specs=[pl.BlockSpec((tr, aw + cc), lambda i: (i, 0)), pl.BlockSpec((tr, cc), lambda i: (i, 0))],
        out_shape=[jax.ShapeDtypeStruct((n_lat, aw + cc), BF16), jax.ShapeDtypeStruct((n_lat, cc), F32)],
        scratch_shapes=[pltpu.VMEM((tr + 2 * CONV_HALO, cc), F32),
                        pltpu.VMEM((SUBLANES, tr + 2 * CONV_HALO - SUBLANES, cc), F32)],
        compiler_params=_cp(("parallel",)))(*args)


def _mix_bwd_pointwise(dmix, attn, p0, yc, ln_g, ln_b, dims, tr, n_ext):
    n_lat, aw = attn.shape
    cc, cb, off = dims["conv_ch"], dims["cb"], dims["off"]
    na, nc = aw // cb, cc // cb
    lat_tiles = n_lat // tr

    def body(*refs):
        it = iter(refs)
        dmix_ref, attn_ref = next(it), next(it)
        za = [next(it) for _ in range(na)]
        zb = [next(it) for _ in range(nc)]
        yc_ref, g_ref, bb_ref = next(it), next(it), next(it)
        dattn_ref, dza_ref, dzb_ref, dyc_ref, dg_ref, dbb_ref, ddb_ref = (next(it) for _ in range(7))
        i = pl.program_id(0)

        @pl.when(i == 0)
        def _():
            dg_ref[...] = jnp.zeros_like(dg_ref)
            dbb_ref[...] = jnp.zeros_like(dbb_ref)
            ddb_ref[...] = jnp.zeros_like(ddb_ref)

        lat = (i < lat_tiles).astype(F32)
        dm = dmix_ref[...].astype(F32)
        dma, dmb = dm[:, :aw], dm[:, aw:]
        zav, zbv = _cat(za), _cat(zb)
        sa, dsa = _silu_and_grad(zav)
        sb, dsb = _silu_and_grad(zbv)
        dattn_ref[...] = (dma * sa).astype(BF16)
        dza_ref[...] = (dma * attn_ref[...] * dsa * lat).astype(BF16)
        ycv = yc_ref[...]
        mu = jnp.mean(ycv, axis=-1, keepdims=True)
        xc = ycv - mu
        rs = lax.rsqrt(jnp.mean(xc * xc, axis=-1, keepdims=True) + EPS)
        xh = xc * rs
        nv = xh * g_ref[...] + bb_ref[...]
        sn, dsn = _silu_and_grad(nv)
        dzb_ref[...] = (dmb * sn * dsb * lat).astype(BF16)
        dn = dmb * sb * dsn
        dg_ref[...] += lat * jnp.sum(dn * xh, axis=0, keepdims=True)
        dbb_ref[...] += lat * jnp.sum(dn, axis=0, keepdims=True)
        dxh = dn * g_ref[...]
        dyc = rs * (dxh - jnp.mean(dxh, axis=-1, keepdims=True) - xh * jnp.mean(dxh * xh, axis=-1, keepdims=True))
        dyc_ref[...] = dyc
        ddb_ref[...] += lat * jnp.sum(dyc, axis=0, keepdims=True)

    rm = lambda i: jnp.minimum(i, lat_tiles - 1)
    row = lambda w: pl.BlockSpec((tr, w), lambda i: (rm(i), 0))
    ext = lambda w: pl.BlockSpec((tr, w), lambda i: (i, 0))
    in_specs = ([row(aw + cc), row(aw)] + _col_specs(tr, off["za"], aw, cb, rm)
                + _col_specs(tr, off["zb"], cc, cb, rm) + [row(cc), _vec(cc), _vec(cc)])
    args = [dmix, attn] + [p0] * (na + nc) + [yc, ln_g, ln_b]
    return pl.pallas_call(
        body, name="mix_bwd_pointwise", grid=(n_ext // tr,), in_specs=in_specs,
        out_specs=[row(aw), ext(aw), ext(cc), row(cc), _vec(cc), _vec(cc), _vec(cc)],
        out_shape=[jax.ShapeDtypeStruct((n_lat, aw), BF16), jax.ShapeDtypeStruct((n_ext, aw), BF16),
                   jax.ShapeDtypeStruct((n_ext, cc), BF16), jax.ShapeDtypeStruct((n_lat, cc), F32)]
                  + [jax.ShapeDtypeStruct((1, cc), F32)] * 3,
        compiler_params=_cp(("arbitrary",)))(*args)


def _conv_bwd(dyc, p0, dw_w, dims, tr, n_ext):
    n_lat, cc = dyc.shape
    cb, cw, off = dims["cb"], dims["conv_w"], dims["off"]
    nc = cc // cb
    n_tiles = n_lat // tr
    pad = cw // 2
    prev_lat, next_lat = _halo_maps(tr, n_tiles)
    rm = lambda i: jnp.minimum(i, n_tiles - 1)
    prev_map = lambda i: prev_lat(rm(i))
    next_map = lambda i: next_lat(rm(i))

    def body(*refs):
        it = iter(refs)
        a_c = [next(it) for _ in range(nc)]
        b_c = [next(it) for _ in range(nc)]
        a_p = [next(it) for _ in range(nc)]
        b_p = [next(it) for _ in range(nc)]
        a_n = [next(it) for _ in range(nc)]
        b_n = [next(it) for _ in range(nc)]
        d_c, d_p, d_n, w_ref, dab_ref, dw_ref, ypad, dpad, ysh, dsh, dy_scr = (next(it) for _ in range(11))
        i = pl.program_id(0)

        @pl.when(i == 0)
        def _():
            dw_ref[...] = jnp.zeros_like(dw_ref)

        @pl.when(i >= n_tiles)
        def _():
            dab_ref[...] = jnp.zeros_like(dab_ref)

        @pl.when(i < n_tiles)
        def _():
            first, last = (i > 0).astype(F32), (i < n_tiles - 1).astype(F32)
            av, bv = _cat(a_c), _cat(b_c)
            sg = _sigmoid(bv)
            ypad[pl.ds(0, CONV_HALO), :] = _cat(a_p) * _sigmoid(_cat(b_p)) * first
            ypad[pl.ds(CONV_HALO, tr), :] = av * sg
            ypad[pl.ds(CONV_HALO + tr, CONV_HALO), :] = _cat(a_n) * _sigmoid(_cat(b_n)) * last
            dpad[pl.ds(0, CONV_HALO), :] = d_p[...] * first
            dpad[pl.ds(CONV_HALO, tr), :] = d_c[...]
            dpad[pl.ds(CONV_HALO + tr, CONV_HALO), :] = d_n[...] * last
            _shifted_copies(ypad, ysh, tr)
            _shifted_copies(dpad, dsh, tr)
            _stencil(dsh, w_ref, dy_scr, [CONV_HALO + pad - k for k in range(cw)], tr, cc)
            _stencil_weight_grad(ysh, dpad, CONV_HALO, dw_ref, [CONV_HALO - pad + k for k in range(cw)], tr, cc)
            dy = dy_scr[...]
            dab_ref[:, :cc] = (dy * sg).astype(BF16)
            dab_ref[:, cc:] = (dy * av * sg * (1.0 - sg)).astype(BF16)

    in_specs = (_col_specs(tr, off["a"], cc, cb, rm) + _col_specs(tr, off["b"], cc, cb, rm)
                + _col_specs(CONV_HALO, off["a"], cc, cb, prev_map) + _col_specs(CONV_HALO, off["b"], cc, cb, prev_map)
                + _col_specs(CONV_HALO, off["a"], cc, cb, next_map) + _col_specs(CONV_HALO, off["b"], cc, cb, next_map)
                + [pl.BlockSpec((tr, cc), lambda i: (rm(i), 0)),
                   pl.BlockSpec((CONV_HALO, cc), lambda i: (prev_map(i), 0)),
                   pl.BlockSpec((CONV_HALO, cc), lambda i: (jnp.minimum(next_map(i), n_lat // CONV_HALO - 1), 0)),
                   pl.BlockSpec(dw_w.shape, lambda i: (0, 0))])
    args = [p0] * (6 * nc) + [dyc, dyc, dyc, dw_w]
    return pl.pallas_call(
        body, name="conv_bwd", grid=(n_ext // tr,), in_specs=in_specs,
        out_specs=[pl.BlockSpec((tr, 2 * cc), lambda i: (i, 0)), pl.BlockSpec(dw_w.shape, lambda i: (0, 0))],
        out_shape=[jax.ShapeDtypeStruct((n_ext, 2 * cc), BF16), jax.ShapeDtypeStruct(dw_w.shape, F32)],
        scratch_shapes=[pltpu.VMEM((tr + 2 * CONV_HALO, cc), F32), pltpu.VMEM((tr + 2 * CONV_HALO, cc), F32),
                        pltpu.VMEM((SUBLANES, tr + 2 * CONV_HALO - SUBLANES, cc), F32),
                        pltpu.VMEM((SUBLANES, tr + 2 * CONV_HALO - SUBLANES, cc), F32),
                        pltpu.VMEM((tr, cc), F32)],
        compiler_params=_cp(("arbitrary",)))(*args)


def _sgu_parts(u, v, ln_g, ln_b):
    mu = jnp.mean(v, axis=-1, keepdims=True)
    xc = v - mu
    rs = lax.rsqrt(jnp.mean(xc * xc, axis=-1, keepdims=True) + EPS)
    xh = xc * rs
    return u, xh, rs, xh * ln_g + ln_b


def _sgu_fwd(p1, ln_g, ln_b, ws, bs_t, tr):
    n_lat, w3 = p1.shape
    w = w3 // 3
    ng, ch = ws.shape[0], ws.shape[1]
    gwid = w // ng
    n_ch = tr // ch

    def body(pu_ref, pv_ref, pg_ref, g_ref, b_ref, ws_ref, bs_ref, o_ref):
        u, _, _, vln = _sgu_parts(_gelu(pu_ref[...].astype(F32)), _gelu(pv_ref[...].astype(F32)),
                                  g_ref[...], b_ref[...])
        gate = _silu(pg_ref[...].astype(F32))
        vb = vln.astype(BF16)
        for c in range(n_ch):
            rs_ = slice(c * ch, (c + 1) * ch)
            for gi in range(ng):
                cs_ = slice(gi * gwid, (gi + 1) * gwid)
                mixed = jnp.dot(ws_ref[gi], vb[rs_, cs_], preferred_element_type=F32) + bs_ref[:, gi:gi + 1]
                o_ref[rs_, cs_] = (u[rs_, cs_] * mixed * gate[rs_, cs_]).astype(BF16)

    col = lambda t: pl.BlockSpec((tr, w), lambda i, _t=t: (i, _t))
    return pl.pallas_call(
        body, name="sgu_fwd", grid=(n_lat // tr,),
        in_specs=[col(0), col(1), col(2), _vec(w), _vec(w),
                  pl.BlockSpec(ws.shape, lambda i: (0, 0, 0)), pl.BlockSpec(bs_t.shape, lambda i: (0, 0))],
        out_specs=pl.BlockSpec((tr, w), lambda i: (i, 0)),
        out_shape=jax.ShapeDtypeStruct((n_lat, w), BF16),
        compiler_params=_cp(("parallel",)))(p1, p1, p1, ln_g, ln_b, ws, bs_t)


def _sgu_bwd(dm, p1, ln_g, ln_b, ws, ws_t, bs_t, tr):
    n_lat, w3 = p1.shape
    w = w3 // 3
    ng, ch = ws.shape[0], ws.shape[1]
    gwid = w // ng
    n_ch = tr // ch
    dn_nt = (((1,), (1,)), ((), ()))

    def body(dm_ref, pu_ref, pv_ref, pg_ref, g_ref, b_ref, ws_ref, wst_ref, bs_ref,
             dp_ref, dws_ref, dbs_ref, dg_ref, dbb_ref, dvln_scr):
        i = pl.program_id(0)

        @pl.when(i == 0)
        def _():
            dws_ref[...] = jnp.zeros_like(dws_ref)
            dbs_ref[...] = jnp.zeros_like(dbs_ref)
            dg_ref[...] = jnp.zeros_like(dg_ref)
            dbb_ref[...] = jnp.zeros_like(dbb_ref)

        puv, pvv, pgv = pu_ref[...].astype(F32), pv_ref[...].astype(F32), pg_ref[...].astype(F32)
        gu, dgu = _gelu_and_grad(puv)
        gv, dgv = _gelu_and_grad(pvv)
        gate, dgate = _silu_and_grad(pgv)
        u, xh, rs, vln = _sgu_parts(gu, gv, g_ref[...], b_ref[...])
        dmv = dm_ref[...].astype(F32)
        vb = vln.astype(BF16)
        dmu = dmv * u
        du_pre = dmv * gate * dgu
        dg_pre = dmu * dgate
        dmix_all = dmu * gate
        dbs_cols = [jnp.zeros((ch, 1), F32) for _ in range(ng)]
        for c in range(n_ch):
            rs_ = slice(c * ch, (c + 1) * ch)
            for gi in range(ng):
                cs_ = slice(gi * gwid, (gi + 1) * gwid)
                mixed = jnp.dot(ws_ref[gi], vb[rs_, cs_], preferred_element_type=F32) + bs_ref[:, gi:gi + 1]
                dmixed = dmix_all[rs_, cs_]
                dmb = dmixed.astype(BF16)
                dp_ref[rs_, gi * gwid:(gi + 1) * gwid] = (du_pre[rs_, cs_] * mixed).astype(BF16)
                dp_ref[rs_, 2 * w + gi * gwid:2 * w + (gi + 1) * gwid] = (dg_pre[rs_, cs_] * mixed).astype(BF16)
                dvln_scr[rs_, cs_] = jnp.dot(wst_ref[gi], dmb, preferred_element_type=F32)
                dws_ref[gi] += lax.dot_general(dmb, vb[rs_, cs_], dn_nt, preferred_element_type=F32)
                dbs_cols[gi] = dbs_cols[gi] + jnp.sum(dmixed, axis=-1, keepdims=True)
        dbs_ref[...] += jnp.concatenate(dbs_cols, axis=1)
        dvln = dvln_scr[...]
        dg_ref[...] += jnp.sum(dvln * xh, axis=0, keepdims=True)
        dbb_ref[...] += jnp.sum(dvln, axis=0, keepdims=True)
        dxh = dvln * g_ref[...]
        dv = rs * (dxh - jnp.mean(dxh, axis=-1, keepdims=True) - xh * jnp.mean(dxh * xh, axis=-1, keepdims=True))
        dp_ref[:, w:2 * w] = (dv * dgv).astype(BF16)

    col = lambda t: pl.BlockSpec((tr, w), lambda i, _t=t: (i, _t))
    return pl.pallas_call(
        body, name="sgu_bwd", grid=(n_lat // tr,),
        in_specs=[pl.BlockSpec((tr, w), lambda i: (i, 0)), col(0), col(1), col(2), _vec(w), _vec(w),
                  pl.BlockSpec(ws.shape, lambda i: (0, 0, 0)), pl.BlockSpec(ws.shape, lambda i: (0, 0, 0)),
                  pl.BlockSpec(bs_t.shape, lambda i: (0, 0))],
        out_specs=[pl.BlockSpec((tr, w3), lambda i: (i, 0)), pl.BlockSpec(ws.shape, lambda i: (0, 0, 0)),
                   pl.BlockSpec(bs_t.shape, lambda i: (0, 0)), _vec(w), _vec(w)],
        out_shape=[jax.ShapeDtypeStruct((n_lat, w3), BF16), jax.ShapeDtypeStruct(ws.shape, F32),
                   jax.ShapeDtypeStruct(bs_t.shape, F32), jax.ShapeDtypeStruct((1, w), F32),
                   jax.ShapeDtypeStruct((1, w), F32)],
        scratch_shapes=[pltpu.VMEM((tr, w), F32)],
        compiler_params=_cp(("arbitrary",)))(dm, p1, p1, p1, ln_g, ln_b, ws, ws_t, bs_t)


def _final_loss(x2, target, final_g, o_prev, gate_prev, tr):
    n_lat, d = x2.shape

    def body(x_ref, t_ref, g_ref, o_ref, gp_ref, dx_ref, do_ref, ls_ref, dg_ref, dgp_ref):
        i = pl.program_id(0)

        @pl.when(i == 0)
        def _():
            ls_ref[...] = jnp.zeros_like(ls_ref)
            dg_ref[...] = jnp.zeros_like(dg_ref)
            dgp_ref[...] = jnp.zeros_like(dgp_ref)

        xv = x_ref[...]
        gv = g_ref[...]
        rs = lax.rsqrt(jnp.mean(xv * xv, axis=-1, keepdims=True) + EPS)
        xn = xv * rs
        err = xn * gv - t_ref[...]
        ls_ref[...] += jnp.sum(err * err, axis=0, keepdims=True)
        dy = err * (1.0 / d)
        dg_ref[...] += jnp.sum(dy * xn, axis=0, keepdims=True)
        gy = dy * gv
        dx = rs * (gy - xn * jnp.mean(gy * xn, axis=-1, keepdims=True))
        dx_ref[...] = dx
        do_ref[...] = (gp_ref[...] * dx).astype(BF16)
        dgp_ref[...] += jnp.sum(dx * o_ref[...].astype(F32), axis=0, keepdims=True)

    row = pl.BlockSpec((tr, d), lambda i: (i, 0))
    return pl.pallas_call(
        body, name="final_loss", grid=(n_lat // tr,), in_specs=[row, row, _vec(d), row, _vec(d)],
        out_specs=[row, row, _vec(d), _vec(d), _vec(d)],
        out_shape=[jax.ShapeDtypeStruct((n_lat, d), F32), jax.ShapeDtypeStruct((n_lat, d), BF16)]
                  + [jax.ShapeDtypeStruct((1, d), F32)] * 3,
        compiler_params=_cp(("arbitrary",)))(x2, target, final_g, o_prev, gate_prev)


def _pack(arrays):
    flat = jnp.concatenate([a.reshape(-1).astype(F32) for a in arrays])
    n = flat.shape[0]
    rows = -(-n // LANES)
    rows = -(-rows // PACK_ROWS) * PACK_ROWS
    return jnp.pad(flat, (0, rows * LANES - n)).reshape(rows, LANES)


def _unpack(buf, shapes):
    flat = buf.reshape(buf.shape[:-2] + (-1,))
    out, pos = [], 0
    for shp in shapes:
        n = math.prod(shp)
        out.append(flat[..., pos:pos + n].reshape(buf.shape[:-2] + tuple(shp)))
        pos += n
    return out


def _rope_tables(n_lat, n_ctx):
    rows = n_lat // GRID_W
    row = jnp.repeat(jnp.arange(rows, dtype=F32), GRID_W)
    col = jnp.tile(jnp.arange(GRID_W, dtype=F32), rows)
    n_freq, axis_dim = HEAD_DIM // 4, HEAD_DIM // 2
    inv = jnp.power(ROPE_THETA, jnp.arange(n_freq, dtype=F32) * (-2.0 / axis_dim))
    ang = jnp.concatenate([row[:, None] * inv, col[:, None] * inv], axis=-1)
    cos, sin = jnp.cos(ang), jnp.sin(ang)
    cos_i = jnp.repeat(cos, 2, axis=-1)
    sin_s = jnp.stack([-sin, sin], axis=-1).reshape(n_lat, HEAD_DIM)
    cos_i = jnp.concatenate([cos_i, jnp.ones((n_ctx, HEAD_DIM), F32)], axis=0)
    sin_s = jnp.concatenate([sin_s, jnp.zeros((n_ctx, HEAD_DIM), F32)], axis=0)
    return cos_i, sin_s


def kernel(x, c, ctx, c_ctx, ada_w, ada_b, norm_g, ev_w_in, ev_q_norm, ev_k_norm, ev_dw_w, ev_dw_b, ev_ln_g, ev_ln_b, ev_w_out, od_w_in, od_ln_g, od_ln_b, od_ws, od_bs, od_w_out, final_g, loss_target, m_c_ctx, m_ada_w, m_ada_b, m_norm_g, m_ev_w_in, m_ev_q_norm, m_ev_k_norm, m_ev_dw_w, m_ev_dw_b, m_ev_ln_g, m_ev_ln_b, m_ev_w_out, m_od_w_in, m_od_ln_g, m_od_ln_b, m_od_ws, m_od_bs, m_od_w_out, m_final_g, v_c_ctx, v_ada_w, v_ada_b, v_norm_g, v_ev_w_in, v_ev_q_norm, v_ev_k_norm, v_ev_dw_w, v_ev_dw_b, v_ev_ln_g, v_ev_ln_b, v_ev_w_out, v_od_w_in, v_od_ln_g, v_od_ln_b, v_od_ws, v_od_bs, v_od_w_out, v_final_g):
    weights = dict(c_ctx=c_ctx, ada_w=ada_w, ada_b=ada_b, norm_g=norm_g, ev_w_in=ev_w_in, ev_q_norm=ev_q_norm,
                   ev_k_norm=ev_k_norm, ev_dw_w=ev_dw_w, ev_dw_b=ev_dw_b, ev_ln_g=ev_ln_g, ev_ln_b=ev_ln_b,
                   ev_w_out=ev_w_out, od_w_in=od_w_in, od_ln_g=od_ln_g, od_ln_b=od_ln_b, od_ws=od_ws, od_bs=od_bs,
                   od_w_out=od_w_out, final_g=final_g)
    mom_m = dict(c_ctx=m_c_ctx, ada_w=m_ada_w, ada_b=m_ada_b, norm_g=m_norm_g, ev_w_in=m_ev_w_in,
                 ev_q_norm=m_ev_q_norm, ev_k_norm=m_ev_k_norm, ev_dw_w=m_ev_dw_w, ev_dw_b=m_ev_dw_b,
                 ev_ln_g=m_ev_ln_g, ev_ln_b=m_ev_ln_b, ev_w_out=m_ev_w_out, od_w_in=m_od_w_in, od_ln_g=m_od_ln_g,
                 od_ln_b=m_od_ln_b, od_ws=m_od_ws, od_bs=m_od_bs, od_w_out=m_od_w_out, final_g=m_final_g)
    mom_v = dict(c_ctx=v_c_ctx, ada_w=v_ada_w, ada_b=v_ada_b, norm_g=v_norm_g, ev_w_in=v_ev_w_in,
                 ev_q_norm=v_ev_q_norm, ev_k_norm=v_ev_k_norm, ev_dw_w=v_ev_dw_w, ev_dw_b=v_ev_dw_b,
                 ev_ln_g=v_ev_ln_g, ev_ln_b=v_ev_ln_b, ev_w_out=v_ev_w_out, od_w_in=v_od_w_in, od_ln_g=v_od_ln_g,
                 od_ln_b=v_od_ln_b, od_ws=v_od_ws, od_bs=v_od_bs, od_w_out=v_od_w_out, final_g=v_final_g)
    order = list(weights)

    _, n_lat, d = x.shape
    n_ctx = ctx.shape[1]
    n_ext = n_lat + n_ctx
    ev_in = ev_w_in.shape[-1] * N_CHIP
    ev_mix = ev_w_out.shape[1] * N_CHIP
    conv_ch = ev_dw_b.shape[-1]
    conv_w = ev_dw_w.shape[1]
    attn_w = ev_mix - conv_ch
    kv_w = N_KV_HEADS * HEAD_DIM
    assert ev_in == 2 * kv_w + 2 * attn_w + 3 * conv_ch and conv_w // 2 < CONV_HALO
    sgu_w = od_w_out.shape[1] * N_CHIP
    wa = ada_w.shape[-1]
    cb = math.gcd(2 * kv_w, attn_w, conv_ch)
    off = dict(k=0, v=kv_w, q=2 * kv_w, za=2 * kv_w + attn_w, a=2 * kv_w + 2 * attn_w,
               b=2 * kv_w + 2 * attn_w + conv_ch, zb=2 * kv_w + 2 * attn_w + 2 * conv_ch)
    dims = dict(kv_w=kv_w, attn_w=attn_w, conv_ch=conv_ch, conv_w=conv_w, cb=cb, off=off)
    tr = 256 if (n_lat % 256 == 0 and n_ctx % 256 == 0) else 128

    mx, my, mc = lax.axis_index("x"), lax.axis_index("y"), lax.axis_index("c")
    me = 4 * mx + 2 * my + mc
    chip = 2 * mx + my

    x2d, tgt2d, ctx2d = x[0], loss_target[0], ctx[0]
    ev_dw_w_l = ev_dw_w[0]
    dwc = ev_dw_w_l.shape[1]
    lnc = od_ln_g.shape[1]

    g_c = _allgather_small(jnp.broadcast_to(c, (8, d)), "gather_cond")[:, 0, :]
    c_rows = jnp.concatenate([g_c, c_ctx[None, :], jnp.zeros((MOD_ROWS - N_DEV - 1, d), F32)], axis=0)
    c_rows_t = c_rows.T
    ada_b_shard = lax.dynamic_slice_in_dim(ada_b, chip * wa, wa, axis=1)[:, None, :]
    mod_part = _mod_fwd(c_rows_t, ada_w, ada_b_shard)
    part_shapes = [(2, MOD_ROWS, wa), (conv_w, dwc), (1, lnc), (1, lnc)]
    g_parts = _allgather_small(_pack([mod_part, ev_dw_w_l, od_ln_g, od_ln_b]), "gather_mod")
    per_chip = [_unpack(g_parts[2 * s], part_shapes) for s in range(N_CHIP)]
    mod_all = jnp.concatenate([p[0] for p in per_chip], axis=-1)
    dw_w_full = jnp.concatenate([p[1] for p in per_chip], axis=-1)
    od_ln_g_full = jnp.concatenate([p[2] for p in per_chip], axis=-1)
    od_ln_b_full = jnp.concatenate([p[3] for p in per_chip], axis=-1)
    dw_w_pad = jnp.pad(dw_w_full, ((0, 2 * CONV_HALO - conv_w), (0, 0)))
    mod_me = lax.dynamic_slice_in_dim(mod_all, me, 1, axis=1)
    shift0, scale0, gate0 = mod_me[0, :, :d], mod_me[0, :, d:2 * d], mod_me[0, :, 2 * d:]
    shift1, scale1, gate1 = mod_me[1, :, :d], mod_me[1, :, d:2 * d], mod_me[1, :, 2 * d:]
    shift_c, scale_c = mod_all[0, N_DEV:N_DEV + 1, :d], mod_all[0, N_DEV:N_DEV + 1, d:2 * d]
    g0, g1 = norm_g[0:1], norm_g[1:2]

    lay = dict(ev_w_in=_Sharded((d, ev_in), True), ev_w_out=_Sharded((ev_mix, d), False),
               od_w_in=_Sharded((d, 3 * sgu_w), True), od_w_out=_Sharded((sgu_w, d), False))
    big = list(lay)
    chip_arr = jnp.reshape(chip, (1,)).astype(jnp.int32)
    core_arr = jnp.reshape(mc, (1,)).astype(jnp.int32)
    own = {n: _cast_into_full(weights[n][0], lay[n], chip_arr, f"cast_{n}") for n in big}
    gather_ev_out = _gather_comm([own["ev_w_out"]], [lay["ev_w_out"]], forward_half_way=True)
    layer1 = ["od_w_in", "od_w_out"]
    gather_layer1 = _gather_comm([own[n] for n in layer1], [lay[n] for n in layer1], forward_half_way=True)

    def reduce_start(g, n):
        theirs = _pair_exchange([g], [lay[n]], f"pair_exchange_{n}")[0]
        psum = _pair_sum(g, theirs, lay[n], core_arr, f"pair_sum_{n}")
        return psum, _scatter_comm([psum], [lay[n]])

    h0, w_ev_in = _adaln_fwd(x2d, g0, shift0, scale0, tr, "adaln0_fwd",
                             comm=_gather_comm([own["ev_w_in"]], [lay["ev_w_in"]]), out_rows=n_ext)
    w_full = {"ev_w_in": w_ev_in}
    h0e = _adaln_fwd(ctx2d, g0, shift_c, scale_c, tr, "adaln0_ctx_fwd", into=h0, row0=n_lat)
    tm_e = _pick(n_ext, (1408, 768, 640, 512, 256, 128))
    tk_e = _pick(n_ext, (768, 640, 512, 256, 128))
    tm_l = _pick(n_lat, (1024, 512, 256, 128))
    p0, *gathered = _mm(h0e, w_full["ev_w_in"], name="mm_ev_in", tm=tm_e, tn=_pick(ev_in, (1408, 512, 256, 128)), tk=d,
                        out_dtype=BF16, comm=gather_ev_out)
    w_full["ev_w_out"] = gathered[0]
    cos_i, sin_s = _rope_tables(n_lat, n_ctx)
    q_hat, k_all, v_all = _qk_prep(p0, cos_i, sin_s, ev_q_norm, ev_k_norm, dims, tr)
    tq = _pick(n_lat, (256, 128))
    tkk = _pick(n_ext, (1408, 640, 512, 256, 128))
    attn, lse, *gathered = _flash_fwd(q_hat, k_all, v_all, n_lat, dims, tq, tkk, comm=gather_layer1)
    w_full.update(zip(layer1, gathered))
    mix, yc = _mix_fwd(attn, p0, dw_w_pad, ev_dw_b, ev_ln_g, ev_ln_b, dims, tr)
    o0, x1 = _mm(mix, w_full["ev_w_out"], name="mm_ev_out", tm=_pick(n_lat, (512, 256, 128)),
                 tn=_pick(d, (2048, 1024, 512, 256)),
                 tk=ev_mix, out_dtype=BF16, res=x2d, gate=gate0)

    h1 = _adaln_fwd(x1, g1, shift1, scale1, tr, "adaln1_fwd")
    p1 = _mm(h1, w_full["od_w_in"], name="mm_od_in", tm=tm_l, tn=_pick(3 * sgu_w, (1536, 512, 256, 128)), tk=d,
             out_dtype=BF16)
    ws_b = od_ws[0].astype(BF16)
    ws_t_b = jnp.swapaxes(od_ws[0], 1, 2).astype(BF16)
    bs_t = od_bs[0].T
    m1 = _sgu_fwd(p1, od_ln_g_full, od_ln_b_full, ws_b, bs_t, tr)
    o1, x2 = _mm(m1, w_full["od_w_out"], name="mm_od_out", tm=_pick(n_lat, (512, 256, 128)),
                 tn=_pick(d, (2048, 1024, 512, 256)),
                 tk=sgu_w, out_dtype=BF16, res=x1, gate=gate1)

    dx2, do1, loss_cols, d_final_g, dgate1 = _final_loss(x2, tgt2d, final_g[None, :], o1, gate1, tr)
    loss = lax.psum(0.5 / d * jnp.sum(loss_cols), ("x", "y", "c"))

    tk_l = _pick(n_lat, (1024, 512, 256, 128))
    psums, slots = {}, {}
    tk_nt = (1408, 1024, 768, 512, 256, 128)
    g_od_w_out = _mm(m1, do1, name="mm_od_out_dw", ta=True, tm=_pick(sgu_w, (2048, 1024, 512, 256)),
                     tn=_pick(d, (1024, 512, 256)), tk=tk_l, out_dtype=BF16)
    psums["od_w_out"], sc = reduce_start(g_od_w_out, "od_w_out")
    dm1, slots["od_w_out"] = _mm(do1, w_full["od_w_out"], name="mm_od_out_dx", tb=True, tm=tm_l,
                                 tn=_pick(sgu_w, (2048, 1024, 512, 256)), tk=d, out_dtype=BF16, comm=sc)
    dp1, d_ws, d_bs_t, d_od_ln_g, d_od_ln_b = _sgu_bwd(dm1, p1, od_ln_g_full, od_ln_b_full, ws_b, ws_t_b, bs_t, tr)
    small1 = [dgate1, d_final_g, d_od_ln_g, d_od_ln_b, d_ws, d_bs_t.T]
    g_od_w_in, g_small1 = _mm(h1, dp1, name="mm_od_in_dw", ta=True, tm=_pick(d, (2048, 1024, 512, 256)),
                              tn=_pick(3 * sgu_w, (1536, 768, 512, 384, 256, 128)), tk=tk_l, out_dtype=BF16,
                              comm=_allgather_comm(_pack(small1)))
    psums["od_w_in"], sc = reduce_start(g_od_w_in, "od_w_in")
    dh1, slots["od_w_in"] = _mm(dp1, w_full["od_w_in"], name="mm_od_in_dx", tb=True, tm=tm_l,
                                tn=_pick(d, (2048, 1024, 512, 256)), tk=_pick(3 * sgu_w, tk_nt), out_dtype=BF16, comm=sc)
    zero_d = jnp.zeros((1, d), F32)
    b1 = _adaln_bwd(x1, dh1, 0, g1, scale1, zero_d, tr, "adaln1_bwd", dres=dx2, o_prev=o0, gate_prev=gate0)
    dx1, do0, dgate0 = b1["dx"], b1["do_prev"], b1["dgate_prev"]

    g_ev_w_out = _mm(mix, do0, name="mm_ev_out_dw", ta=True, tm=_pick(ev_mix, (2048, 1024, 512, 256)),
                     tn=_pick(d, (1024, 512, 256)), tk=tk_l, out_dtype=BF16)
    psums["ev_w_out"], sc = reduce_start(g_ev_w_out, "ev_w_out")
    dmix, slots["ev_w_out"] = _mm(do0, w_full["ev_w_out"], name="mm_ev_out_dx", tb=True, tm=tm_l,
                                  tn=_pick(ev_mix, (2048, 1024, 512, 256)), tk=d, out_dtype=BF16, comm=sc)
    dattn, dza, dzb, dyc, d_ev_ln_g, d_ev_ln_b, d_dw_b = _mix_bwd_pointwise(
        dmix, attn, p0, yc, ev_ln_g, ev_ln_b, dims, tr, n_ext)
    dab, d_dw_w_pad = _conv_bwd(dyc, p0, dw_w_pad, dims, tr, n_ext)
    dq_hat, dk_hat, dv_all = _flash_bwd(q_hat, k_all, v_all, dattn, attn, lse, n_lat, dims, tq, tkk)
    dkvq, d_q_norm, d_k_norm = _qk_prep_bwd(dq_hat, dk_hat, dv_all, p0, cos_i, sin_s, ev_q_norm, ev_k_norm,
                                            dims, tr, n_lat)
    dp0 = jnp.concatenate([dkvq, dza, dab, dzb], axis=1)
    small2 = [b1["dshift"], b1["dscale"], dgate0, b1["dg"], d_q_norm, d_k_norm, d_dw_w_pad[:conv_w], d_dw_b,
              d_ev_ln_g, d_ev_ln_b]
    g_ev_w_in, g_small2 = _mm(h0e, dp0, name="mm_ev_in_dw", ta=True, tm=_pick(d, (2048, 1024, 512, 256)),
                              tn=_pick(ev_in, (1408, 768, 512, 256, 128)), tk=tk_e, out_dtype=BF16,
                              comm=_allgather_comm(_pack(small2)))
    psums["ev_w_in"], sc = reduce_start(g_ev_w_in, "ev_w_in")
    dh0, slots["ev_w_in"] = _mm(dp0, w_full["ev_w_in"], name="mm_ev_in_dx", tb=True, tm=tm_e,
                                tn=_pick(d, (2048, 1024, 512, 256)), tk=_pick(ev_in, tk_nt), out_dtype=BF16, comm=sc)
    lays = [lay[n] for n in big]
    halves = [_chip_sum(psums[n], slots[n], lay[n], chip_arr, core_arr, f"chip_sum_{n}") for n in big]
    bc = _adaln_bwd(ctx2d, dh0, n_lat, g0, scale_c, zero_d, tr, "adaln0_ctx_bwd", comm=_share_comm(halves, lays))
    g_big = dict(zip(big, bc["carried"]))
    b0 = _adaln_bwd(x2d, dh0, 0, g0, scale0, bc["dg"], tr, "adaln0_bwd", dres=dx1)
    grad_x = b0["dx"]

    small3 = [b0["dshift"], b0["dscale"], bc["dshift"], bc["dscale"], b0["dg"]]
    g_small3 = _allgather_small(_pack(small3), "gather_small_grads")

    def totals_and_rows(gathered, parts, n_rows, name):
        shapes = [a.shape for a in parts]
        tot = _unpack(_sum_slots(gathered, name), shapes)
        return tot, [r[:, 0, :] for r in _unpack(gathered, shapes[:n_rows])]

    (t_dgate1, t_fg, t_oln_g, t_oln_b, t_ws, t_bs), (dgate1_rows,) = totals_and_rows(
        g_small1, small1, 1, "sum_small_grads1")
    ((t_dshift1, t_dscale1, t_dgate0, t_g1, t_qn, t_kn, t_dw_w, t_dw_b, t_eln_g, t_eln_b),
     (dshift1_rows, dscale1_rows, dgate0_rows)) = totals_and_rows(g_small2, small2, 3, "sum_small_grads2")
    (t_dshift0, t_dscale0, t_dshift_c, t_dscale_c, t_g0), (dshift0_rows, dscale0_rows) = totals_and_rows(
        g_small3, small3, 2, "sum_small_grads3")
    zeros_d = jnp.zeros((1, d), F32)
    t_dmodc = jnp.concatenate([t_dshift_c, t_dscale_c, zeros_d], axis=1)
    t_dmod0 = jnp.concatenate([t_dshift0, t_dscale0, t_dgate0], axis=1)
    t_dmod1 = jnp.concatenate([t_dshift1, t_dscale1, t_dgate1], axis=1)
    dmod0_rows = jnp.concatenate([dshift0_rows, dscale0_rows, dgate0_rows], axis=1)
    dmod1_rows = jnp.concatenate([dshift1_rows, dscale1_rows, dgate1_rows], axis=1)
    pad_rows = jnp.zeros((MOD_ROWS - N_DEV - 1, 3 * d), F32)
    dm_l0 = jnp.concatenate([dmod0_rows, t_dmodc, pad_rows], axis=0)
    dm_l1 = jnp.concatenate([dmod1_rows, jnp.zeros((MOD_ROWS - N_DEV, 3 * d), F32)], axis=0)
    dm_shard = lax.dynamic_slice_in_dim(jnp.stack([dm_l0, dm_l1]), chip * wa, wa, axis=2)
    g_ada_w, dsc = _mod_bwd(c_rows_t, dm_shard, ada_w)
    g_dsc = _allgather_small(_pack([dsc[0]]), "gather_cctx")
    g_c_ctx = _cctx_grad(g_dsc, _pack([c_ctx])).reshape(-1)[:d]
    g_ada_b = jnp.stack([t_dmod0[0] + t_dmodc[0], t_dmod1[0]])

    grads = dict(
        c_ctx=g_c_ctx, ada_w=g_ada_w, ada_b=g_ada_b, norm_g=jnp.concatenate([t_g0, t_g1], axis=0),
        ev_w_in=g_big["ev_w_in"][None], ev_q_norm=t_qn, ev_k_norm=t_kn,
        ev_dw_w=lax.dynamic_slice_in_dim(t_dw_w, chip * dwc, dwc, axis=1)[None], ev_dw_b=t_dw_b,
        ev_ln_g=t_eln_g, ev_ln_b=t_eln_b, ev_w_out=g_big["ev_w_out"][None], od_w_in=g_big["od_w_in"][None],
        od_ln_g=lax.dynamic_slice_in_dim(t_oln_g, chip * lnc, lnc, axis=1),
        od_ln_b=lax.dynamic_slice_in_dim(t_oln_b, chip * lnc, lnc, axis=1),
        od_ws=t_ws[None], od_bs=t_bs[None], od_w_out=g_big["od_w_out"][None], final_g=t_fg[0])
    grads = {n: grads[n].reshape(weights[n].shape) for n in order}

    delta, new_m, new_v = {}, {}, {}
    large = ("ada_w", "ev_w_in", "ev_w_out", "od_w_in", "od_w_out")
    for n in large:
        shp = weights[n].shape
        as2d = lambda a: a.reshape(-1, shp[-1])
        outs = _adamw(as2d(weights[n]), as2d(grads[n]), as2d(mom_m[n]), as2d(mom_v[n]), f"adamw_{n}",
                      copy_grad=n in big)
        delta[n], new_m[n], new_v[n] = (o.reshape(shp) for o in outs[:3])
        if n in big:
            grads[n] = outs[3].reshape(shp)
    rest_names = [n for n in order if n not in large]
    rest_shapes = [weights[n].shape for n in rest_names]
    dl, nm, nv = _adamw(_pack([weights[n] for n in rest_names]), _pack([grads[n] for n in rest_names]),
                        _pack([mom_m[n] for n in rest_names]), _pack([mom_v[n] for n in rest_names]), "adamw_small")
    for n, a, b_, c_ in zip(rest_names, _unpack(dl, rest_shapes), _unpack(nm, rest_shapes), _unpack(nv, rest_shapes)):
        delta[n], new_m[n], new_v[n] = a, b_, c_

    return (loss, grad_x[None], *[grads[n] for n in order], *[delta[n] for n in order],
            *[new_m[n] for n in order], *[new_v[n] for n in order])
```

```python
import math

import jax
import jax.numpy as jnp
from jax import lax
from jax.experimental import pallas as pl
from jax.experimental.pallas import tpu as pltpu

F32 = jnp.float32
BF16 = jnp.bfloat16
EPS = 1e-6
GRID_W = 64
ROPE_THETA = 10000.0
HEAD_DIM = 128
N_KV_HEADS = 2
CONV_HALO = 16
LANES = 128
SUBLANES = 8
STENCIL_ROWS = 32
STENCIL_CHAINS = 4
REDUCE_ROWS = 32
N_DEV = 8
N_CHIP = 4
MOD_ROWS = 16
PACK_ROWS = 64
ADAM_LR, ADAM_B1, ADAM_B2, ADAM_EPS, ADAM_WD, ADAM_STEP = 0.001, 0.9, 0.999, 1e-08, 0.01, 10
VMEM_LIMIT = 56 * 1024 * 1024
MESH = pl.DeviceIdType.MESH
ANY = pl.BlockSpec(memory_space=pl.ANY)
VMEM_SPEC = pl.BlockSpec(memory_space=pltpu.VMEM)
CHIP_DELTAS = ((1, 0), (0, 1), (1, 1))


def _cp(sem=None):
    return pltpu.CompilerParams(dimension_semantics=sem, vmem_limit_bytes=VMEM_LIMIT)


def _pick(n, cands):
    for c in cands:
        if n % c == 0:
            return c
    raise ValueError(f"no tile for {n} in {cands}")


def _sigmoid(x):
    return 1.0 / (1.0 + jnp.exp(-x))


def _silu(x):
    return x * _sigmoid(x)


def _silu_and_grad(x):
    s = _sigmoid(x)
    y = x * s
    return y, s + y * (1.0 - s)


def _dsilu(x):
    return _silu_and_grad(x)[1]


_GELU_C = math.sqrt(2.0 / math.pi)
_GELU_A = 0.044715


def _gelu_and_grad(x):
    x2 = x * x
    t = jnp.tanh(x * (_GELU_C + (_GELU_C * _GELU_A) * x2))
    h = 0.5 + 0.5 * t
    return x * h, h + x * (1.0 - t * t) * (0.5 * _GELU_C + (1.5 * _GELU_C * _GELU_A) * x2)


def _gelu(x):
    x2 = x * x
    return x * (0.5 + 0.5 * jnp.tanh(x * (_GELU_C + (_GELU_C * _GELU_A) * x2)))


def _vec(d):
    return pl.BlockSpec((1, d), lambda *_: (0, 0))


def _cat(refs):
    parts = [r[...].astype(F32) for r in refs]
    return parts[0] if len(parts) == 1 else jnp.concatenate(parts, axis=1)


def _col_specs(rows, off, width, cb, row_map):
    assert off % cb == 0 and width % cb == 0
    return [pl.BlockSpec((rows, cb), (lambda *g, _c=off // cb + t: (row_map(*g), _c))) for t in range(width // cb)]


def _my_pos():
    return lax.axis_index("x"), lax.axis_index("y"), lax.axis_index("c")


def _allgather_small(x, name):
    r, c = x.shape

    def body(x_ref, out_ref, send_sems, recv_sems, local_sem):
        mx, my, mc = _my_pos()
        me = 4 * mx + 2 * my + mc
        mine = pltpu.make_async_copy(x_ref, out_ref.at[me], local_sem)
        mine.start()
        deltas = [(dx, dy, dc) for dx in (0, 1) for dy in (0, 1) for dc in (0, 1) if (dx, dy, dc) != (0, 0, 0)]
        sends = []
        for k, (dx, dy, dc) in enumerate(deltas):
            px, py, pc = (mx + dx) % 2, (my + dy) % 2, (mc + dc) % 2
            cp = pltpu.make_async_remote_copy(
                src_ref=x_ref, dst_ref=out_ref.at[me], send_sem=send_sems.at[k], recv_sem=recv_sems.at[k],
                device_id=(px, py, pc), device_id_type=MESH)
            cp.start()
            sends.append(cp)
        for k, (dx, dy, dc) in enumerate(deltas):
            px, py, pc = (mx + dx) % 2, (my + dy) % 2, (mc + dc) % 2
            peer = 4 * px + 2 * py + pc
            pltpu.make_async_remote_copy(
                src_ref=x_ref, dst_ref=out_ref.at[peer], send_sem=send_sems.at[k], recv_sem=recv_sems.at[k],
                device_id=(px, py, pc), device_id_type=MESH).wait_recv()
        for cp in sends:
            cp.wait_send()
        mine.wait()

    return pl.pallas_call(
        body, name=name,
        out_shape=jax.ShapeDtypeStruct((N_DEV, r, c), x.dtype),
        in_specs=[VMEM_SPEC], out_specs=VMEM_SPEC,
        scratch_shapes=[pltpu.SemaphoreType.DMA((N_DEV - 1,)), pltpu.SemaphoreType.DMA((N_DEV - 1,)),
                        pltpu.SemaphoreType.DMA],
        compiler_params=pltpu.CompilerParams(vmem_limit_bytes=VMEM_LIMIT),
    )(x)


class _Sharded:
    def __init__(self, full_shape, by_cols):
        self.full = full_shape
        self.by_cols = by_cols
        rows, cols = full_shape
        if by_cols:
            self.shard, self.half, self.halves = (rows, cols // N_CHIP), (rows // 2, cols // N_CHIP), (rows // 2, cols)
        else:
            self.shard, self.half, self.halves = (rows // N_CHIP, cols), (rows // N_CHIP, cols // 2), (rows, cols // 2)

    def region(self, ref, s, h):
        if self.by_cols:
            return ref.at[pl.ds(h * self.half[0], self.half[0]), pl.ds(s * self.shard[1], self.shard[1])]
        return ref.at[pl.ds(s * self.shard[0], self.shard[0]), pl.ds(h * self.half[1], self.half[1])]

    def halves_of_full(self, ref, h):
        if self.by_cols:
            return ref.at[pl.ds(h * self.halves[0], self.halves[0]), :]
        return ref.at[:, pl.ds(h * self.halves[1], self.halves[1])]

    def region_in_halves(self, ref, s):
        if self.by_cols:
            return ref.at[:, pl.ds(s * self.shard[1], self.shard[1])]
        return ref.at[pl.ds(s * self.shard[0], self.shard[0]), :]

    def half_of_shard(self, ref, h):
        if self.by_cols:
            return ref.at[pl.ds(h * self.half[0], self.half[0]), :]
        return ref.at[:, pl.ds(h * self.half[1], self.half[1])]


def _row_tile(rows, row_bytes):
    for t in (512, 256, 128, 64, 32, 16):
        if rows % t == 0 and t * row_bytes <= 2 * 1024 * 1024:
            return t
    return 16


def _cast_into_full(w_shard, lay, chip_arr, name):
    r, c = lay.shard
    tr = _row_tile(r, c * 4)
    nt = r // tr

    def body(chip_ref, w_ref, o_ref):
        o_ref[...] = w_ref[...].astype(BF16)

    if lay.by_cols:
        out_map = lambda i, chip_ref: (i, chip_ref[0])
    else:
        out_map = lambda i, chip_ref: (chip_ref[0] * nt + i, 0)
    return pl.pallas_call(
        body, name=name,
        grid_spec=pltpu.PrefetchScalarGridSpec(
            num_scalar_prefetch=1, grid=(nt,),
            in_specs=[pl.BlockSpec((tr, c), lambda i, chip_ref: (i, 0))],
            out_specs=pl.BlockSpec((tr, c), out_map)),
        out_shape=jax.ShapeDtypeStruct(lay.full, BF16), compiler_params=_cp(("parallel",)))(chip_arr, w_shard)


class _Carried:
    def __init__(self, ins, out_shapes, aliases, sem_shape, start, finish, middle=None):
        self.ins, self.out_shapes, self.aliases, self.sem_shape = list(ins), list(out_shapes), dict(aliases), sem_shape
        self.start, self.finish, self.middle = start, finish, middle

    def scratch(self):
        return [pltpu.SemaphoreType.DMA(self.sem_shape), pltpu.SemaphoreType.DMA(self.sem_shape)]

    def split(self, in_refs, out_refs, scratch_refs):
        ni, no = len(self.ins), len(self.out_shapes)
        return in_refs[len(in_refs) - ni:], out_refs[len(out_refs) - no:], scratch_refs[-2], scratch_refs[-1]


def _call(body, *, name, grid, in_specs, out_specs, out_shape, args, sem, comm=None, aliases=None):
    if comm is None:
        return pl.pallas_call(body, name=name, grid=grid, in_specs=in_specs, out_specs=out_specs,
                              out_shape=out_shape, input_output_aliases=aliases or {},
                              compiler_params=_cp(sem))(*args)
    n_in, n_out, n_ci, n_co = len(in_specs), len(out_specs), len(comm.ins), len(comm.out_shapes)

    def carrying(*refs):
        in_refs, out_refs = refs[:n_in + n_ci], refs[n_in + n_ci:n_in + n_ci + n_out + n_co]
        carried = comm.split(in_refs, out_refs, refs[n_in + n_ci + n_out + n_co:])
        first, last = None, None
        for axis, extent in enumerate(grid):
            at0, at1 = pl.program_id(axis) == 0, pl.program_id(axis) == extent - 1
            first = at0 if first is None else jnp.logical_and(first, at0)
            last = at1 if last is None else jnp.logical_and(last, at1)

        @pl.when(first)
        def _():
            comm.start(*carried)

        if comm.middle is not None:
            half = pl.program_id(0) == grid[0] // 2
            for axis in range(1, len(grid)):
                half = jnp.logical_and(half, pl.program_id(axis) == 0)

            @pl.when(half)
            def _():
                comm.middle(*carried)

        body(*in_refs[:n_in], *out_refs[:n_out])

        @pl.when(last)
        def _():
            comm.finish(*carried)

    return pl.pallas_call(
        carrying, name=name, grid=grid, in_specs=list(in_specs) + [ANY] * n_ci,
        out_specs=list(out_specs) + [ANY] * n_co, out_shape=list(out_shape) + comm.out_shapes,
        scratch_shapes=comm.scratch(),
        input_output_aliases={**(aliases or {}), **{n_in + s: n_out + d for s, d in comm.aliases.items()}},
        compiler_params=_cp(("arbitrary",) * len(grid)))(*args, *comm.ins)


def _gather_comm(fulls, layouts, forward_half_way=False):
    n = len(fulls)

    def ici(ins, outs, send_sems, recv_sems, a, j, landed=False):
        mx, my, mc = _my_pos()
        dx, dy = CHIP_DELTAS[j]
        px, py = (mx + dx) % 2, (my + dy) % 2
        src_chip = 2 * px + py if landed else 2 * mx + my
        return pltpu.make_async_remote_copy(
            src_ref=layouts[a].region(ins[a], src_chip, mc), dst_ref=layouts[a].region(outs[a], src_chip, mc),
            send_sem=send_sems.at[a, j], recv_sem=recv_sems.at[a, j], device_id=(px, py, mc), device_id_type=MESH)

    def d2d(ins, outs, send_sems, recv_sems, a, j, landed=False):
        mx, my, mc = _my_pos()
        dx, dy = CHIP_DELTAS[j]
        other = 2 * ((mx + dx) % 2) + (my + dy) % 2
        half = 1 - mc if landed else mc
        region = layouts[a].region(outs[a], other, half)
        return pltpu.make_async_remote_copy(
            src_ref=region, dst_ref=region, send_sem=send_sems.at[a, 3 + j], recv_sem=recv_sems.at[a, 3 + j],
            device_id=(mx, my, 1 - mc), device_id_type=MESH)

    pairs = [(a, j) for a in range(n) for j in range(3)]

    def start(*r):
        for a, j in pairs:
            ici(*r, a, j).start()

    def forward(*r):
        for a, j in pairs:
            ici(*r, a, j, landed=True).wait_recv()
            d2d(*r, a, j).start()

    def drain(*r):
        for a, j in pairs:
            d2d(*r, a, j, landed=True).wait_recv()
        for a, j in pairs:
            ici(*r, a, j).wait_send()
            d2d(*r, a, j).wait_send()

    def finish(*r):
        forward(*r)
        drain(*r)

    out_shapes = [jax.ShapeDtypeStruct(lay.full, BF16) for lay in layouts]
    if forward_half_way:
        return _Carried(fulls, out_shapes, {a: a for a in range(n)}, (n, 6), start, drain, middle=forward)
    return _Carried(fulls, out_shapes, {a: a for a in range(n)}, (n, 6), start, finish)


def _pair_exchange(grads, layouts, name):
    n = len(grads)

    def body(*refs):
        ins, outs = refs[:n], refs[n:2 * n]
        send_sems, recv_sems = refs[2 * n:]
        mx, my, mc = _my_pos()
        copies = []
        for a, lay in enumerate(layouts):
            cp = pltpu.make_async_remote_copy(
                src_ref=lay.halves_of_full(ins[a], 1 - mc), dst_ref=outs[a],
                send_sem=send_sems.at[a], recv_sem=recv_sems.at[a],
                device_id=(mx, my, 1 - mc), device_id_type=MESH)
            cp.start()
            copies.append(cp)
        for cp in copies:
            cp.wait()

    return pl.pallas_call(
        body, name=name,
        out_shape=[jax.ShapeDtypeStruct(lay.halves, g.dtype) for lay, g in zip(layouts, grads)],
        in_specs=[ANY] * n, out_specs=[ANY] * n,
        scratch_shapes=[pltpu.SemaphoreType.DMA((n,)), pltpu.SemaphoreType.DMA((n,))],
    )(*grads)


def _pair_sum(g, theirs, lay, core_arr, name):
    r, c = lay.halves
    tr = _row_tile(r, c * 4)
    nt = r // tr

    def body(core_ref, g_ref, t_ref, o_ref):
        o_ref[...] = (g_ref[...].astype(F32) + t_ref[...].astype(F32)).astype(BF16)

    if lay.by_cols:
        g_map = lambda i, core_ref: (core_ref[0] * nt + i, 0)
    else:
        g_map = lambda i, core_ref: (i, core_ref[0])
    plain = pl.BlockSpec((tr, c), lambda i, core_ref: (i, 0))
    return pl.pallas_call(
        body, name=name,
        grid_spec=pltpu.PrefetchScalarGridSpec(
            num_scalar_prefetch=1, grid=(nt,), in_specs=[pl.BlockSpec((tr, c), g_map), plain], out_specs=plain),
        out_shape=jax.ShapeDtypeStruct((r, c), BF16), compiler_params=_cp(("parallel",)))(core_arr, g, theirs)


def _scatter_comm(pair_sums, layouts):
    n = len(pair_sums)

    def copy(ins, outs, send_sems, recv_sems, a, j):
        mx, my, mc = _my_pos()
        dx, dy = CHIP_DELTAS[j]
        px, py = (mx + dx) % 2, (my + dy) % 2
        return pltpu.make_async_remote_copy(
            src_ref=layouts[a].region_in_halves(ins[a], 2 * px + py), dst_ref=outs[a].at[j],
            send_sem=send_sems.at[a, j], recv_sem=recv_sems.at[a, j], device_id=(px, py, mc), device_id_type=MESH)

    pairs = [(a, j) for a in range(n) for j in range(3)]

    def start(*r):
        for a, j in pairs:
            copy(*r, a, j).start()

    def finish(*r):
        for a, j in pairs:
            copy(*r, a, j).wait()

    return _Carried(pair_sums, [jax.ShapeDtypeStruct((3,) + lay.half, BF16) for lay in layouts], {}, (n, 3),
                    start, finish)


def _chip_sum(pair_sum, slots, lay, chip_arr, core_arr, name):
    r, c = lay.half
    tr = _row_tile(r, c * 4)
    nt = r // tr

    def body(chip_ref, core_ref, s_ref, slot_ref, o_ref):
        acc = s_ref[...].astype(F32)
        for j in range(3):
            acc = acc + slot_ref[j].astype(F32)
        o_ref[...] = acc

    if lay.by_cols:
        s_map = lambda i, chip_ref, core_ref: (i, chip_ref[0])
        o_map = lambda i, chip_ref, core_ref: (core_ref[0] * nt + i, 0)
    else:
        s_map = lambda i, chip_ref, core_ref: (chip_ref[0] * nt + i, 0)
        o_map = lambda i, chip_ref, core_ref: (i, core_ref[0])
    return pl.pallas_call(
        body, name=name,
        grid_spec=pltpu.PrefetchScalarGridSpec(
            num_scalar_prefetch=2, grid=(nt,),
            in_specs=[pl.BlockSpec((tr, c), s_map),
                      pl.BlockSpec((3, tr, c), lambda i, chip_ref, core_ref: (0, i, 0))],
            out_specs=pl.BlockSpec((tr, c), o_map)),
        out_shape=jax.ShapeDtypeStruct(lay.shard, F32), compiler_params=_cp(("parallel",)))(
            chip_arr, core_arr, pair_sum, slots)


def _share_comm(bufs, layouts):
    n = len(bufs)

    def copy(ins, outs, send_sems, recv_sems, a, landed=False):
        mx, my, mc = _my_pos()
        half = 1 - mc if landed else mc
        return pltpu.make_async_remote_copy(
            src_ref=layouts[a].half_of_shard(ins[a], half), dst_ref=layouts[a].half_of_shard(outs[a], half),
            send_sem=send_sems.at[a], recv_sem=recv_sems.at[a], device_id=(mx, my, 1 - mc), device_id_type=MESH)

    def start(*r):
        for a in range(n):
            copy(*r, a).start()

    def finish(*r):
        for a in range(n):
            copy(*r, a, landed=True).wait_recv()
        for a in range(n):
            copy(*r, a).wait_send()

    return _Carried(bufs, [jax.ShapeDtypeStruct(lay.shard, F32) for lay in layouts], {a: a for a in range(n)},
                    (n,), start, finish)


def _allgather_comm(x):
    deltas = [(dx, dy, dc) for dx in (0, 1) for dy in (0, 1) for dc in (0, 1) if (dx, dy, dc) != (0, 0, 0)]
    local = len(deltas)

    def remote(ins, outs, send_sems, recv_sems, k, landed=False):
        mx, my, mc = _my_pos()
        dx, dy, dc = deltas[k]
        px, py, pc = (mx + dx) % 2, (my + dy) % 2, (mc + dc) % 2
        slot = 4 * px + 2 * py + pc if landed else 4 * mx + 2 * my + mc
        return pltpu.make_async_remote_copy(
            src_ref=ins[0], dst_ref=outs[0].at[slot], send_sem=send_sems.at[k], recv_sem=recv_sems.at[k],
            device_id=(px, py, pc), device_id_type=MESH)

    def mine(ins, outs, send_sems, recv_sems):
        mx, my, mc = _my_pos()
        return pltpu.make_async_copy(ins[0], outs[0].at[4 * mx + 2 * my + mc], send_sems.at[local])

    def start(*r):
        mine(*r).start()
        for k in range(len(deltas)):
            remote(*r, k).start()

    def finish(*r):
        for k in range(len(deltas)):
            remote(*r, k, landed=True).wait_recv()
        for k in range(len(deltas)):
            remote(*r, k).wait_send()
        mine(*r).wait()

    return _Carried([x], [jax.ShapeDtypeStruct((N_DEV,) + x.shape, x.dtype)], {}, (N_DEV,), start, finish)


def _sum_slots(x, name):
    s, r, c = x.shape
    tr = _pick(r, (256, 128, 64, 32, 16, 8))

    def body(x_ref, o_ref):
        acc = x_ref[0]
        for k in range(1, s):
            acc = acc + x_ref[k]
        o_ref[...] = acc

    return pl.pallas_call(body, name=name, grid=(r // tr,),
                          in_specs=[pl.BlockSpec((s, tr, c), lambda i: (0, i, 0))],
                          out_specs=pl.BlockSpec((tr, c), lambda i: (i, 0)),
                          out_shape=jax.ShapeDtypeStruct((r, c), F32), compiler_params=_cp(("parallel",)))(x)


def _adamw(w, g, m, v, name, copy_grad=False):
    r, c = w.shape
    tr = _pick(r, (256, 128, 64, 32, 16, 8))
    bc1 = 1.0 - ADAM_B1 ** ADAM_STEP
    bc2 = 1.0 - ADAM_B2 ** ADAM_STEP
    n_out = 4 if copy_grad else 3

    def body(w_ref, g_ref, m_ref, v_ref, d_ref, nm_ref, nv_ref, *g_copy):
        gv = g_ref[...]
        nm = ADAM_B1 * m_ref[...] + (1.0 - ADAM_B1) * gv
        nv = ADAM_B2 * v_ref[...] + (1.0 - ADAM_B2) * (gv * gv)
        d_ref[...] = -ADAM_LR * ((nm / bc1) / (jnp.sqrt(nv / bc2) + ADAM_EPS) + ADAM_WD * w_ref[...])
        nm_ref[...] = nm
        nv_ref[...] = nv
        if copy_grad:
            g_copy[0][...] = gv

    spec = pl.BlockSpec((tr, c), lambda i: (i, 0))
    shp = jax.ShapeDtypeStruct((r, c), F32)
    return _call(body, name=name, grid=(r // tr,), in_specs=[spec] * 4, out_specs=[spec] * n_out,
                 out_shape=[shp] * n_out, args=(w, g, m, v), sem=("parallel",))


def _cctx_grad(parts, c_ctx2d):
    def body(p_ref, c_ref, o_ref):
        tot = ((p_ref[0] + p_ref[2]) + p_ref[4]) + p_ref[6]
        o_ref[...] = tot * _dsilu(c_ref[...])

    return pl.pallas_call(body, name="cctx_grad", in_specs=[VMEM_SPEC, VMEM_SPEC], out_specs=VMEM_SPEC,
                          out_shape=jax.ShapeDtypeStruct(c_ctx2d.shape, F32))(parts, c_ctx2d)


def _mod_fwd(c_rows_t, ada_w, ada_b_shard):
    nl, d, w = ada_w.shape
    td = _pick(d, (256, 128))
    nd = d // td

    def body(ct_ref, w_ref, b_ref, o_ref):
        i = pl.program_id(1)

        @pl.when(i == 0)
        def _():
            o_ref[0] = jnp.broadcast_to(b_ref[0], (MOD_ROWS, w))

        st = _silu(ct_ref[...])
        wv = w_ref[0]
        used = N_DEV + 1
        rows = [jnp.sum(st[:, r:r + 1] * wv, axis=0, keepdims=True) for r in range(used)]
        rows.append(jnp.zeros((MOD_ROWS - used, w), F32))
        o_ref[0] += jnp.concatenate(rows, axis=0)

    return pl.pallas_call(
        body, name="mod_fwd", grid=(nl, nd),
        in_specs=[pl.BlockSpec((td, MOD_ROWS), lambda l, i: (i, 0)),
                  pl.BlockSpec((1, td, w), lambda l, i: (l, i, 0)),
                  pl.BlockSpec((1, 1, w), lambda l, i: (l, 0, 0))],
        out_specs=pl.BlockSpec((1, MOD_ROWS, w), lambda l, i: (l, 0, 0)),
        out_shape=jax.ShapeDtypeStruct((nl, MOD_ROWS, w), F32),
        compiler_params=_cp(("parallel", "arbitrary")),
    )(c_rows_t, ada_w, ada_b_shard)


def _mod_bwd(c_rows_t, dmod, ada_w):
    nl, d, w = ada_w.shape
    td = _pick(d, (256, 128))
    ctx_row = N_DEV

    def body(ct_ref, dm_ref, w_ref, gw_ref, ds_ref):
        st = _silu(ct_ref[...])
        dm = dm_ref[0]
        acc = st[:, 0:1] * dm[0:1, :]
        for r in range(1, ctx_row + 1):
            acc = acc + st[:, r:r + 1] * dm[r:r + 1, :]
        gw_ref[0] = acc
        ds_ref[0] = jnp.sum(w_ref[0] * dm[ctx_row:ctx_row + 1, :], axis=1, keepdims=True)

    return pl.pallas_call(
        body, name="mod_bwd", grid=(nl, d // td),
        in_specs=[pl.BlockSpec((td, MOD_ROWS), lambda l, i: (i, 0)),
                  pl.BlockSpec((1, MOD_ROWS, w), lambda l, i: (l, 0, 0)),
                  pl.BlockSpec((1, td, w), lambda l, i: (l, i, 0))],
        out_specs=[pl.BlockSpec((1, td, w), lambda l, i: (l, i, 0)),
                   pl.BlockSpec((1, td, 1), lambda l, i: (l, i, 0))],
        out_shape=[jax.ShapeDtypeStruct((nl, d, w), F32), jax.ShapeDtypeStruct((nl, d, 1), F32)],
        compiler_params=_cp(("parallel", "parallel")),
    )(c_rows_t, dmod, ada_w)


def _adaln_fwd(x, g, shift, scale, tr, name, comm=None, out_rows=None, into=None, row0=0):
    r, d = x.shape
    rb0 = row0 // tr
    total = into.shape[0] if into is not None else (out_rows or r)

    def body(x_ref, g_ref, sh_ref, sc_ref, *rest):
        o_ref = rest[-1]
        xv = x_ref[...]
        rs = lax.rsqrt(jnp.mean(xv * xv, axis=-1, keepdims=True) + EPS)
        o_ref[...] = ((xv * rs * g_ref[...]) * (1.0 + sc_ref[...]) + sh_ref[...]).astype(BF16)

    spec = pl.BlockSpec((tr, d), lambda i: (i, 0))
    in_specs, args, aliases = [spec, _vec(d), _vec(d), _vec(d)], [x, g, shift, scale], None
    if into is not None:
        in_specs.append(ANY)
        args.append(into)
        aliases = {4: 0}
    outs = _call(body, name=name, grid=(r // tr,), in_specs=in_specs,
                 out_specs=[pl.BlockSpec((tr, d), lambda i: (rb0 + i, 0))],
                 out_shape=[jax.ShapeDtypeStruct((total, d), BF16)], args=args, sem=("parallel",),
                 comm=comm, aliases=aliases)
    return outs if comm else outs[0]


def _adaln_bwd(xin, dh, row0, g, scale, dg_init, tr, name, dres=None, o_prev=None, gate_prev=None, comm=None):
    r, d = xin.shape
    assert row0 % tr == 0
    rb0 = row0 // tr
    want_dx = dres is not None
    want_prev = o_prev is not None
    assert want_dx or not want_prev

    def body(*refs):
        it = iter(refs)
        x_ref, dh_ref, g_ref, sc_ref, dgi_ref = next(it), next(it), next(it), next(it), next(it)
        dres_ref = next(it) if want_dx else None
        o_ref, gp_ref = (next(it), next(it)) if want_prev else (None, None)
        dx_ref = next(it) if want_dx else None
        do_ref = next(it) if want_prev else None
        dsh_ref, dsc_ref, dg_ref = next(it), next(it), next(it)
        dgp_ref = next(it) if want_prev else None
        i = pl.program_id(0)

        @pl.when(i == 0)
        def _():
            dsh_ref[...] = jnp.zeros_like(dsh_ref)
            dsc_ref[...] = jnp.zeros_like(dsc_ref)
            dg_ref[...] = dgi_ref[...]
            if want_prev:
                dgp_ref[...] = jnp.zeros_like(dgp_ref)

        xv = x_ref[...]
        dhv = dh_ref[...].astype(F32)
        gv = g_ref[...]
        rs = lax.rsqrt(jnp.mean(xv * xv, axis=-1, keepdims=True) + EPS)
        xn = xv * rs
        dsh_ref[...] += jnp.sum(dhv, axis=0, keepdims=True)
        dsc_ref[...] += jnp.sum(dhv * (xn * gv), axis=0, keepdims=True)
        dr = dhv * (1.0 + sc_ref[...])
        dg_ref[...] += jnp.sum(dr * xn, axis=0, keepdims=True)
        if want_dx:
            gy = dr * gv
            dx = dres_ref[...] + rs * (gy - xn * jnp.mean(gy * xn, axis=-1, keepdims=True))
            dx_ref[...] = dx
            if want_prev:
                do_ref[...] = (gp_ref[...] * dx).astype(BF16)
                dgp_ref[...] += jnp.sum(dx * o_ref[...].astype(F32), axis=0, keepdims=True)

    row = pl.BlockSpec((tr, d), lambda i: (i, 0))
    in_specs = [row, pl.BlockSpec((tr, d), lambda i: (rb0 + i, 0)), _vec(d), _vec(d), _vec(d)]
    args = [xin, dh, g, scale, dg_init]
    out_specs, out_shape, names = [], [], []
    if want_dx:
        in_specs.append(row)
        args.append(dres)
    if want_prev:
        in_specs += [row, _vec(d)]
        args += [o_prev, gate_prev]
    if want_dx:
        out_specs.append(row)
        out_shape.append(jax.ShapeDtypeStruct((r, d), F32))
        names.append("dx")
    if want_prev:
        out_specs.append(row)
        out_shape.append(jax.ShapeDtypeStruct((r, d), BF16))
        names.append("do_prev")
    for nm in ("dshift", "dscale", "dg") + (("dgate_prev",) if want_prev else ()):
        out_specs.append(_vec(d))
        out_shape.append(jax.ShapeDtypeStruct((1, d), F32))
        names.append(nm)
    outs = _call(body, name=name, grid=(r // tr,), in_specs=in_specs, out_specs=out_specs, out_shape=out_shape,
                 args=args, sem=("arbitrary",), comm=comm)
    res = dict(zip(names, outs))
    if comm:
        res["carried"] = outs[len(names):]
    return res


def _mm(a, b, *, name, tm, tn, tk, ta=False, tb=False, out_dtype=F32, res=None, gate=None, comm=None):
    if ta:
        kd, m = a.shape
    else:
        m, kd = a.shape
    if tb:
        n, kd2 = b.shape
    else:
        kd2, n = b.shape
    assert kd == kd2 and m % tm == 0 and n % tn == 0 and kd % tk == 0, (a.shape, b.shape, tm, tn, tk)
    ni, nj, nk = m // tm, n // tn, kd // tk
    dn = (((0 if ta else 1,), (1 if tb else 0,)), ((), ()))
    with_res = res is not None
    n_in = 4 if with_res else 2
    n_out = 2 if with_res else 1
    n_cin = len(comm.ins) if comm else 0
    n_cout = len(comm.out_shapes) if comm else 0

    def body(*refs):
        in_refs = refs[:n_in + n_cin]
        out_refs = refs[n_in + n_cin:n_in + n_cin + n_out + n_cout]
        scratch = refs[n_in + n_cin + n_out + n_cout:]
        a_ref, b_ref = in_refs[0], in_refs[1]
        o_ref = out_refs[0]
        i, j, k = pl.program_id(0), pl.program_id(1), pl.program_id(2)
        if comm:
            carried = comm.split(in_refs, out_refs, scratch)

            @pl.when(jnp.logical_and(jnp.logical_and(i == 0, j == 0), k == 0))
            def _():
                comm.start(*carried)

            if comm.middle is not None:
                @pl.when(jnp.logical_and(jnp.logical_and(i == ni // 2, j == 0), k == 0))
                def _():
                    comm.middle(*carried)

        def emit(acc):
            o_ref[...] = acc.astype(o_ref.dtype)
            if with_res:
                out_refs[1][...] = in_refs[2][...] + in_refs[3][...] * acc

        if nk == 1:
            emit(lax.dot_general(a_ref[...], b_ref[...], dn, preferred_element_type=F32))
        else:
            acc_ref = scratch[0]

            @pl.when(k == 0)
            def _():
                acc_ref[...] = jnp.zeros_like(acc_ref)

            acc_ref[...] += lax.dot_general(a_ref[...], b_ref[...], dn, preferred_element_type=F32)

            @pl.when(k == nk - 1)
            def _():
                emit(acc_ref[...])

        if comm:
            @pl.when(jnp.logical_and(jnp.logical_and(i == ni - 1, j == nj - 1), k == nk - 1))
            def _():
                comm.finish(*carried)

    a_spec = pl.BlockSpec((tk, tm), lambda i, j, k: (k, i)) if ta else pl.BlockSpec((tm, tk), lambda i, j, k: (i, k))
    b_spec = pl.BlockSpec((tn, tk), lambda i, j, k: (j, k)) if tb else pl.BlockSpec((tk, tn), lambda i, j, k: (k, j))
    o_spec = pl.BlockSpec((tm, tn), lambda i, j, k: (i, j))
    in_specs, args = [a_spec, b_spec], [a, b]
    out_specs, out_shape = [o_spec], [jax.ShapeDtypeStruct((m, n), out_dtype)]
    if with_res:
        in_specs += [o_spec, pl.BlockSpec((1, tn), lambda i, j, k: (0, j))]
        args += [res, gate]
        out_specs.append(o_spec)
        out_shape.append(jax.ShapeDtypeStruct((m, n), F32))
    scratch_shapes = [pltpu.VMEM((tm, tn), F32)] if nk > 1 else []
    aliases = {}
    sem = ("parallel", "parallel", "arbitrary")
    if comm:
        in_specs += [ANY] * n_cin
        args += comm.ins
        out_specs += [ANY] * n_cout
        out_shape += comm.out_shapes
        scratch_shapes += comm.scratch()
        aliases = {n_in + s: n_out + d for s, d in comm.aliases.items()}
        sem = ("arbitrary", "arbitrary", "arbitrary")
    outs = pl.pallas_call(body, name=name, grid=(ni, nj, nk), in_specs=in_specs, out_specs=out_specs,
                          out_shape=out_shape, scratch_shapes=scratch_shapes, input_output_aliases=aliases,
                          compiler_params=_cp(sem))(*args)
    return outs if (with_res or comm) else outs[0]


def _swap_pairs(x):
    lane = lax.broadcasted_iota(jnp.int32, x.shape, 1)
    return jnp.where(lane % 2 == 0, pltpu.roll(x, HEAD_DIM - 1, 1), pltpu.roll(x, 1, 1))


def _qk_prep(p0, cos_i, sin_s, q_norm, k_norm, dims, tr):
    rows = p0.shape[0]
    kvw, aw, cb = dims["kv_w"], dims["attn_w"], dims["cb"]
    nkv, nq = kvw // HEAD_DIM, aw // HEAD_DIM
    scale = HEAD_DIM ** -0.5
    n_kv_specs, n_q_specs = (2 * kvw) // cb, aw // cb

    def body(*refs):
        kv_refs = refs[:n_kv_specs]
        q_refs = refs[n_kv_specs:n_kv_specs + n_q_specs]
        cos_ref, sin_ref, qn_ref, kn_ref, qo_ref, ko_ref, vo_ref = refs[n_kv_specs + n_q_specs:]
        kv = _cat(kv_refs)
        qv = _cat(q_refs)
        cs, sn = cos_ref[...], sin_ref[...]

        def norm_rope(xh, gvec):
            rs = lax.rsqrt(jnp.mean(xh * xh, axis=-1, keepdims=True) + EPS)
            xn = xh * rs * gvec
            return xn * cs + _swap_pairs(xn) * sn

        for h in range(nkv):
            sl = slice(h * HEAD_DIM, (h + 1) * HEAD_DIM)
            ko_ref[:, sl] = norm_rope(kv[:, sl], kn_ref[...]).astype(BF16)
        vo_ref[...] = kv[:, kvw:].astype(BF16)
        for h in range(nq):
            sl = slice(h * HEAD_DIM, (h + 1) * HEAD_DIM)
            qo_ref[:, sl] = (norm_rope(qv[:, sl], qn_ref[...]) * scale).astype(BF16)

    rm = lambda i: i
    in_specs = (_col_specs(tr, 0, 2 * kvw, cb, rm) + _col_specs(tr, 2 * kvw, aw, cb, rm)
                + [pl.BlockSpec((tr, HEAD_DIM), lambda i: (i, 0))] * 2 + [_vec(HEAD_DIM)] * 2)
    args = [p0] * (n_kv_specs + n_q_specs) + [cos_i, sin_s, q_norm, k_norm]
    return pl.pallas_call(
        body, name="qk_prep", grid=(rows // tr,), in_specs=in_specs,
        out_specs=[pl.BlockSpec((tr, aw), lambda i: (i, 0)), pl.BlockSpec((tr, kvw), lambda i: (i, 0)),
                   pl.BlockSpec((tr, kvw), lambda i: (i, 0))],
        out_shape=[jax.ShapeDtypeStruct((rows, aw), BF16), jax.ShapeDtypeStruct((rows, kvw), BF16),
                   jax.ShapeDtypeStruct((rows, kvw), BF16)],
        compiler_params=_cp(("parallel",)))(*args)


def _qk_prep_bwd(dq_hat, dk_hat, dv, p0, cos_i, sin_s, q_norm, k_norm, dims, tr, n_lat):
    rows = p0.shape[0]
    kvw, aw, cb = dims["kv_w"], dims["attn_w"], dims["cb"]
    nkv, nq = kvw // HEAD_DIM, aw // HEAD_DIM
    scale = HEAD_DIM ** -0.5
    n_kv_specs, n_q_specs = (2 * kvw) // cb, aw // cb
    lat_tiles = n_lat // tr

    def body(*refs):
        kv_refs = refs[:n_kv_specs]
        q_refs = refs[n_kv_specs:n_kv_specs + n_q_specs]
        (dq_ref, dk_ref, dv_ref, cos_ref, sin_ref, qn_ref, kn_ref,
         out_ref, dqn_ref, dkn_ref) = refs[n_kv_specs + n_q_specs:]
        i = pl.program_id(0)

        @pl.when(i == 0)
        def _():
            dqn_ref[...] = jnp.zeros_like(dqn_ref)
            dkn_ref[...] = jnp.zeros_like(dkn_ref)

        kv = _cat(kv_refs)
        qv = _cat(q_refs)
        cs, sn = cos_ref[...], sin_ref[...]
        is_lat = (i < lat_tiles).astype(F32)

        def head_bwd(xh, dhat, gvec):
            dn = dhat * cs + _swap_pairs(dhat * sn)
            rs = lax.rsqrt(jnp.mean(xh * xh, axis=-1, keepdims=True) + EPS)
            xn = xh * rs
            gy = dn * gvec
            dx = rs * (gy - xn * jnp.mean(gy * xn, axis=-1, keepdims=True))
            return dx, jnp.sum(dn * xn, axis=0, keepdims=True)

        dkn = jnp.zeros((1, HEAD_DIM), F32)
        for h in range(nkv):
            sl = slice(h * HEAD_DIM, (h + 1) * HEAD_DIM)
            dx, dgv = head_bwd(kv[:, sl], dk_ref[:, sl], kn_ref[...])
            out_ref[:, sl] = dx.astype(BF16)
            dkn = dkn + dgv
        dkn_ref[...] += dkn
        out_ref[:, kvw:2 * kvw] = dv_ref[...].astype(BF16)
        dqn = jnp.zeros((1, HEAD_DIM), F32)
        for h in range(nq):
            sl = slice(h * HEAD_DIM, (h + 1) * HEAD_DIM)
            dx, dgv = head_bwd(qv[:, sl], dq_ref[:, sl] * (scale * is_lat), qn_ref[...])
            out_ref[:, 2 * kvw + h * HEAD_DIM:2 * kvw + (h + 1) * HEAD_DIM] = dx.astype(BF16)
            dqn = dqn + dgv
        dqn_ref[...] += dqn

    rm = lambda i: i
    wout = 2 * kvw + aw
    in_specs = (_col_specs(tr, 0, 2 * kvw, cb, rm) + _col_specs(tr, 2 * kvw, aw, cb, rm)
                + [pl.BlockSpec((tr, aw), lambda i: (jnp.minimum(i, lat_tiles - 1), 0)),
                   pl.BlockSpec((tr, kvw), lambda i: (i, 0)), pl.BlockSpec((tr, kvw), lambda i: (i, 0)),
                   pl.BlockSpec((tr, HEAD_DIM), lambda i: (i, 0)), pl.BlockSpec((tr, HEAD_DIM), lambda i: (i, 0)),
                   _vec(HEAD_DIM), _vec(HEAD_DIM)])
    args = [p0] * (n_kv_specs + n_q_specs) + [dq_hat, dk_hat, dv, cos_i, sin_s, q_norm, k_norm]
    return pl.pallas_call(
        body, name="qk_prep_bwd", grid=(rows // tr,), in_specs=in_specs,
        out_specs=[pl.BlockSpec((tr, wout), lambda i: (i, 0)), _vec(HEAD_DIM), _vec(HEAD_DIM)],
        out_shape=[jax.ShapeDtypeStruct((rows, wout), BF16), jax.ShapeDtypeStruct((1, HEAD_DIM), F32),
                   jax.ShapeDtypeStruct((1, HEAD_DIM), F32)],
        compiler_params=_cp(("arbitrary",)))(*args)


def _stack_heads(x, g):
    return jnp.concatenate([x[:, h * HEAD_DIM:(h + 1) * HEAD_DIM] for h in range(g)], axis=0)


def _flash_fwd(q_hat, k_all, v_all, n_lat, dims, tq, tk, comm=None):
    kvw, aw = dims["kv_w"], dims["attn_w"]
    nkv = kvw // HEAD_DIM
    g = aw // kvw
    gw = g * HEAD_DIM
    n_keys = k_all.shape[0]
    ni, nj = n_lat // tq, n_keys // tk
    dn_nt = (((1,), (1,)), ((), ()))

    def body(q_ref, k_ref, v_ref, o_ref, lse_ref):
        qs = _stack_heads(q_ref[...], g)
        m = jnp.full((g * tq, 1), -1e30, F32)
        l = jnp.zeros((g * tq, 1), F32)
        acc = jnp.zeros((g * tq, HEAD_DIM), F32)
        for j in range(nj):
            kb = k_ref[pl.ds(j * tk, tk), :]
            vb = v_ref[pl.ds(j * tk, tk), :]
            s = lax.dot_general(qs, kb, dn_nt, preferred_element_type=F32)
            m_new = jnp.maximum(m, jnp.max(s, axis=-1, keepdims=True))
            alpha = jnp.exp(m - m_new)
            p = jnp.exp(s - m_new)
            l = alpha * l + jnp.sum(p, axis=-1, keepdims=True)
            acc = alpha * acc + jnp.dot(p.astype(BF16), vb, preferred_element_type=F32)
            m = m_new
        o = acc / l
        for h in range(g):
            o_ref[:, h * HEAD_DIM:(h + 1) * HEAD_DIM] = o[h * tq:(h + 1) * tq]
        lse_ref[...] = m + jnp.log(l)

    return _call(
        body, name="flash_fwd", grid=(nkv, ni),
        in_specs=[pl.BlockSpec((tq, gw), lambda h, i: (i, h)),
                  pl.BlockSpec((n_keys, HEAD_DIM), lambda h, i: (0, h)),
                  pl.BlockSpec((n_keys, HEAD_DIM), lambda h, i: (0, h))],
        out_specs=[pl.BlockSpec((tq, gw), lambda h, i: (i, h)),
                   pl.BlockSpec((g * tq, 1), lambda h, i: (h * ni + i, 0))],
        out_shape=[jax.ShapeDtypeStruct((n_lat, aw), F32), jax.ShapeDtypeStruct((nkv * ni * g * tq, 1), F32)],
        args=(q_hat, k_all, v_all), sem=("parallel", "parallel"), comm=comm)


def _flash_bwd(q_hat, k_all, v_all, do, o, lse, n_lat, dims, tq, tk):
    kvw, aw = dims["kv_w"], dims["attn_w"]
    nkv = kvw // HEAD_DIM
    g = aw // kvw
    gw = g * HEAD_DIM
    n_keys = k_all.shape[0]
    ni, nj = n_lat // tq, n_keys // tk
    dn_nt = (((1,), (1,)), ((), ()))
    dn_tn = (((0,), (0,)), ((), ()))

    def body(q_ref, k_ref, v_ref, do_ref, o_ref, lse_ref, dq_ref, dk_ref, dv_ref):
        i = pl.program_id(1)

        @pl.when(i == 0)
        def _():
            dk_ref[...] = jnp.zeros_like(dk_ref)
            dv_ref[...] = jnp.zeros_like(dv_ref)

        dos = _stack_heads(do_ref[...], g)
        qs = _stack_heads(q_ref[...], g)
        delta = jnp.sum(dos.astype(F32) * _stack_heads(o_ref[...], g), axis=-1, keepdims=True)
        lse_v = lse_ref[...]
        dq = jnp.zeros((g * tq, HEAD_DIM), F32)
        for j in range(nj):
            rows = pl.ds(j * tk, tk)
            kb, vb = k_ref[rows, :], v_ref[rows, :]
            s = lax.dot_general(qs, kb, dn_nt, preferred_element_type=F32)
            p = jnp.exp(s - lse_v)
            dp = lax.dot_general(dos, vb, dn_nt, preferred_element_type=F32)
            ds = (p * (dp - delta)).astype(BF16)
            dv_ref[rows, :] += lax.dot_general(p.astype(BF16), dos, dn_tn, preferred_element_type=F32)
            dk_ref[rows, :] += lax.dot_general(ds, qs, dn_tn, preferred_element_type=F32)
            dq = dq + jnp.dot(ds, kb, preferred_element_type=F32)
        for h in range(g):
            dq_ref[:, h * HEAD_DIM:(h + 1) * HEAD_DIM] = dq[h * tq:(h + 1) * tq]

    qspec = pl.BlockSpec((tq, gw), lambda h, i: (i, h))
    full_k = pl.BlockSpec((n_keys, HEAD_DIM), lambda h, i: (0, h))
    return pl.pallas_call(
        body, name="flash_bwd", grid=(nkv, ni),
        in_specs=[qspec, full_k, full_k, qspec, qspec, pl.BlockSpec((g * tq, 1), lambda h, i: (h * ni + i, 0))],
        out_specs=[qspec, full_k, full_k],
        out_shape=[jax.ShapeDtypeStruct((n_lat, aw), F32), jax.ShapeDtypeStruct((n_keys, kvw), F32),
                   jax.ShapeDtypeStruct((n_keys, kvw), F32)],
        compiler_params=_cp(("parallel", "arbitrary")))(q_hat, k_all, v_all, do, o, lse)


def _shifted_copies(pad_ref, sh_ref, tr):
    rows = tr + 2 * CONV_HALO - SUBLANES
    for r in range(SUBLANES):
        sh_ref[r] = pad_ref[pl.ds(r, rows), :]


def _stencil(sh_ref, w_ref, out_ref, offsets, tr, cc, init_ref=None):
    for c0 in range(0, cc, LANES):
        lanes = pl.ds(c0, LANES)
        wv = [jnp.broadcast_to(w_ref[pl.ds(k, 1), lanes], (STENCIL_ROWS, LANES)) for k in range(len(offsets))]
        if init_ref is None:
            init = jnp.zeros((STENCIL_ROWS, LANES), F32)
        else:
            init = jnp.broadcast_to(init_ref[:, lanes], (STENCIL_ROWS, LANES))

        def block(rb, carry, lanes=lanes, wv=wv, init=init):
            r0 = rb * STENCIL_ROWS
            parts = [init] + [None] * (STENCIL_CHAINS - 1)
            for k, o in enumerate(offsets):
                rows = pl.ds(SUBLANES * (o // SUBLANES) + r0, STENCIL_ROWS)
                term = sh_ref[o % SUBLANES, rows, lanes] * wv[k]
                q = k % STENCIL_CHAINS
                parts[q] = term if parts[q] is None else parts[q] + term
            acc = parts[0]
            for p_ in parts[1:]:
                acc = acc + p_
            out_ref[pl.ds(r0, STENCIL_ROWS), lanes] = acc
            return carry

        for rb in range(tr // STENCIL_ROWS):
            block(rb, 0)


def _stencil_weight_grad(sh_ref, d_ref, d_row0, dw_ref, offsets, tr, cc):
    rows_per = REDUCE_ROWS
    for c0 in range(0, cc, LANES):
        lanes = pl.ds(c0, LANES)

        def block(rb, accs, lanes=lanes):
            r0 = rb * rows_per
            dblk = d_ref[pl.ds(d_row0 + r0, rows_per), lanes]
            out = []
            for k, o in enumerate(offsets):
                prod = dblk * sh_ref[o % SUBLANES, pl.ds(SUBLANES * (o // SUBLANES) + r0, rows_per), lanes]
                part = prod[0:SUBLANES]
                for q in range(1, rows_per // SUBLANES):
                    part = part + prod[q * SUBLANES:(q + 1) * SUBLANES]
                out.append(accs[k] + part)
            return tuple(out)

        zero = jnp.zeros((SUBLANES, LANES), F32)
        accs = tuple(zero for _ in offsets)
        for rb in range(tr // rows_per):
            accs = block(rb, accs)
        for k in range(len(offsets)):
            dw_ref[pl.ds(k, 1), lanes] += jnp.sum(accs[k], axis=0, keepdims=True)


def _halo_maps(tr, n_tiles):
    per = tr // CONV_HALO
    prev = lambda i: jnp.maximum(i * per - 1, 0)
    nxt = lambda i: (i + 1) * per
    return prev, nxt


def _mix_fwd(attn, p0, dw_w, dw_b, ln_g, ln_b, dims, tr):
    n_lat, aw = attn.shape
    cc, cb, cw = dims["conv_ch"], dims["cb"], dims["conv_w"]
    off = dims["off"]
    n_tiles = n_lat // tr
    pad = cw // 2
    na, nc = aw // cb, cc // cb
    prev_map, next_map = _halo_maps(tr, n_tiles)

    def body(*refs):
        it = iter(refs)
        attn_ref = next(it)
        za = [next(it) for _ in range(na)]
        a_c = [next(it) for _ in range(nc)]
        b_c = [next(it) for _ in range(nc)]
        zb = [next(it) for _ in range(nc)]
        a_p = [next(it) for _ in range(nc)]
        b_p = [next(it) for _ in range(nc)]
        a_n = [next(it) for _ in range(nc)]
        b_n = [next(it) for _ in range(nc)]
        w_ref, db_ref, g_ref, bb_ref, mix_ref, yc_ref, ypad, ysh = (next(it) for _ in range(8))
        i = pl.program_id(0)
        ypad[pl.ds(0, CONV_HALO), :] = _cat(a_p) * _sigmoid(_cat(b_p)) * (i > 0).astype(F32)
        ypad[pl.ds(CONV_HALO, tr), :] = _cat(a_c) * _sigmoid(_cat(b_c))
        ypad[pl.ds(CONV_HALO + tr, CONV_HALO), :] = _cat(a_n) * _sigmoid(_cat(b_n)) * (i < n_tiles - 1).astype(F32)
        _shifted_copies(ypad, ysh, tr)
        _stencil(ysh, w_ref, yc_ref, [CONV_HALO - pad + k for k in range(cw)], tr, cc, init_ref=db_ref)
        acc = yc_ref[...]
        mu = jnp.mean(acc, axis=-1, keepdims=True)
        xc = acc - mu
        rs = lax.rsqrt(jnp.mean(xc * xc, axis=-1, keepdims=True) + EPS)
        nv = xc * rs * g_ref[...] + bb_ref[...]
        mix_ref[:, :aw] = (attn_ref[...] * _silu(_cat(za))).astype(BF16)
        mix_ref[:, aw:] = (_silu(nv) * _silu(_cat(zb))).astype(BF16)

    rm = lambda i: i
    in_specs = ([pl.BlockSpec((tr, aw), lambda i: (i, 0))]
                + _col_specs(tr, off["za"], aw, cb, rm) + _col_specs(tr, off["a"], cc, cb, rm)
                + _col_specs(tr, off["b"], cc, cb, rm) + _col_specs(tr, off["zb"], cc, cb, rm)
                + _col_specs(CONV_HALO, off["a"], cc, cb, prev_map) + _col_specs(CONV_HALO, off["b"], cc, cb, prev_map)
                + _col_specs(CONV_HALO, off["a"], cc, cb, next_map) + _col_specs(CONV_HALO, off["b"], cc, cb, next_map)
                + [pl.BlockSpec(dw_w.shape, lambda i: (0, 0)), _vec(cc), _vec(cc), _vec(cc)])
    args = [attn] + [p0] * (na + 7 * nc) + [dw_w, dw_b, ln_g, ln_b]
    return pl.pallas_call(
        body, name="mix_fwd", grid=(n_tiles,), in_specs=in_specs,
        out_specs=[pl.BlockSpec((tr, aw + cc), lambda i: (i, 0)), pl.BlockSpec((tr, cc), lambda i: (i, 0))],
        out_shape=[jax.ShapeDtypeStruct((n_lat, aw + cc), BF16), jax.ShapeDtypeStruct((n_lat, cc), F32)],
        scratch_shapes=[pltpu.VMEM((tr + 2 * CONV_HALO, cc), F32),
                        pltpu.VMEM((SUBLANES, tr + 2 * CONV_HALO - SUBLANES, cc), F32)],
        compiler_params=_cp(("parallel",)))(*args)


def _mix_bwd_pointwise(dmix, attn, p0, yc, ln_g, ln_b, dims, tr, n_ext):
    n_lat, aw = attn.shape
    cc, cb, off = dims["conv_ch"], dims["cb"], dims["off"]
    na, nc = aw // cb, cc // cb
    lat_tiles = n_lat // tr

    def body(*refs):
        it = iter(refs)
        dmix_ref, attn_ref = next(it), next(it)
        za = [next(it) for _ in range(na)]
        zb = [next(it) for _ in range(nc)]
        yc_ref, g_ref, bb_ref = next(it), next(it), next(it)
        dattn_ref, dza_ref, dzb_ref, dyc_ref, dg_ref, dbb_ref, ddb_ref = (next(it) for _ in range(7))
        i = pl.program_id(0)

        @pl.when(i == 0)
        def _():
            dg_ref[...] = jnp.zeros_like(dg_ref)
            dbb_ref[...] = jnp.zeros_like(dbb_ref)
            ddb_ref[...] = jnp.zeros_like(ddb_ref)

        lat = (i < lat_tiles).astype(F32)
        dm = dmix_ref[...].astype(F32)
        dma, dmb = dm[:, :aw], dm[:, aw:]
        zav, zbv = _cat(za), _cat(zb)
        sa, dsa = _silu_and_grad(zav)
        sb, dsb = _silu_and_grad(zbv)
        dattn_ref[...] = (dma * sa).astype(BF16)
        dza_ref[...] = (dma * attn_ref[...] * dsa * lat).astype(BF16)
        ycv = yc_ref[...]
        mu = jnp.mean(ycv, axis=-1, keepdims=True)
        xc = ycv - mu
        rs = lax.rsqrt(jnp.mean(xc * xc, axis=-1, keepdims=True) + EPS)
        xh = xc * rs
        nv = xh * g_ref[...] + bb_ref[...]
        sn, dsn = _silu_and_grad(nv)
        dzb_ref[...] = (dmb * sn * dsb * lat).astype(BF16)
        dn = dmb * sb * dsn
        dg_ref[...] += lat * jnp.sum(dn * xh, axis=0, keepdims=True)
        dbb_ref[...] += lat * jnp.sum(dn, axis=0, keepdims=True)
        dxh = dn * g_ref[...]
        dyc = rs * (dxh - jnp.mean(dxh, axis=-1, keepdims=True) - xh * jnp.mean(dxh * xh, axis=-1, keepdims=True))
        dyc_ref[...] = dyc
        ddb_ref[...] += lat * jnp.sum(dyc, axis=0, keepdims=True)

    rm = lambda i: jnp.minimum(i, lat_tiles - 1)
    row = lambda w: pl.BlockSpec((tr, w), lambda i: (rm(i), 0))
    ext = lambda w: pl.BlockSpec((tr, w), lambda i: (i, 0))
    in_specs = ([row(aw + cc), row(aw)] + _col_specs(tr, off["za"], aw, cb, rm)
                + _col_specs(tr, off["zb"], cc, cb, rm) + [row(cc), _vec(cc), _vec(cc)])
    args = [dmix, attn] + [p0] * (na + nc) + [yc, ln_g, ln_b]
    return pl.pallas_call(
        body, name="mix_bwd_pointwise", grid=(n_ext // tr,), in_specs=in_specs,
        out_specs=[row(aw), ext(aw), ext(cc), row(cc), _vec(cc), _vec(cc), _vec(cc)],
        out_shape=[jax.ShapeDtypeStruct((n_lat, aw), BF16), jax.ShapeDtypeStruct((n_ext, aw), BF16),
                   jax.ShapeDtypeStruct((n_ext, cc), BF16), jax.ShapeDtypeStruct((n_lat, cc), F32)]
                  + [jax.ShapeDtypeStruct((1, cc), F32)] * 3,
        compiler_params=_cp(("arbitrary",)))(*args)


def _conv_bwd(dyc, p0, dw_w, dims, tr, n_ext):
    n_lat, cc = dyc.shape
    cb, cw, off = dims["cb"], dims["conv_w"], dims["off"]
    nc = cc // cb
    n_tiles = n_lat // tr
    pad = cw // 2
    prev_lat, next_lat = _halo_maps(tr, n_tiles)
    rm = lambda i: jnp.minimum(i, n_tiles - 1)
    prev_map = lambda i: prev_lat(rm(i))
    next_map = lambda i: next_lat(rm(i))

    def body(*refs):
        it = iter(refs)
        a_c = [next(it) for _ in range(nc)]
        b_c = [next(it) for _ in range(nc)]
        a_p = [next(it) for _ in range(nc)]
        b_p = [next(it) for _ in range(nc)]
        a_n = [next(it) for _ in range(nc)]
        b_n = [next(it) for _ in range(nc)]
        d_c, d_p, d_n, w_ref, dab_ref, dw_ref, ypad, dpad, ysh, dsh, dy_scr = (next(it) for _ in range(11))
        i = pl.program_id(0)

        @pl.when(i == 0)
        def _():
            dw_ref[...] = jnp.zeros_like(dw_ref)

        @pl.when(i >= n_tiles)
        def _():
            dab_ref[...] = jnp.zeros_like(dab_ref)

        @pl.when(i < n_tiles)
        def _():
            first, last = (i > 0).astype(F32), (i < n_tiles - 1).astype(F32)
            av, bv = _cat(a_c), _cat(b_c)
            sg = _sigmoid(bv)
            ypad[pl.ds(0, CONV_HALO), :] = _cat(a_p) * _sigmoid(_cat(b_p)) * first
            ypad[pl.ds(CONV_HALO, tr), :] = av * sg
            ypad[pl.ds(CONV_HALO + tr, CONV_HALO), :] = _cat(a_n) * _sigmoid(_cat(b_n)) * last
            dpad[pl.ds(0, CONV_HALO), :] = d_p[...] * first
            dpad[pl.ds(CONV_HALO, tr), :] = d_c[...]
            dpad[pl.ds(CONV_HALO + tr, CONV_HALO), :] = d_n[...] * last
            _shifted_copies(ypad, ysh, tr)
            _shifted_copies(dpad, dsh, tr)
            _stencil(dsh, w_ref, dy_scr, [CONV_HALO + pad - k for k in range(cw)], tr, cc)
            _stencil_weight_grad(ysh, dpad, CONV_HALO, dw_ref, [CONV_HALO - pad + k for k in range(cw)], tr, cc)
            dy = dy_scr[...]
            dab_ref[:, :cc] = (dy * sg).astype(BF16)
            dab_ref[:, cc:] = (dy * av * sg * (1.0 - sg)).astype(BF16)

    in_specs = (_col_specs(tr, off["a"], cc, cb, rm) + _col_specs(tr, off["b"], cc, cb, rm)
                + _col_specs(CONV_HALO, off["a"], cc, cb, prev_map) + _col_specs(CONV_HALO, off["b"], cc, cb, prev_map)
                + _col_specs(CONV_HALO, off["a"], cc, cb, next_map) + _col_specs(CONV_HALO, off["b"], cc, cb, next_map)
                + [pl.BlockSpec((tr, cc), lambda i: (rm(i), 0)),
                   pl.BlockSpec((CONV_HALO, cc), lambda i: (prev_map(i), 0)),
                   pl.BlockSpec((CONV_HALO, cc), lambda i: (jnp.minimum(next_map(i), n_lat // CONV_HALO - 1), 0)),
                   pl.BlockSpec(dw_w.shape, lambda i: (0, 0))])
    args = [p0] * (6 * nc) + [dyc, dyc, dyc, dw_w]
    return pl.pallas_call(
        body, name="conv_bwd", grid=(n_ext // tr,), in_specs=in_specs,
        out_specs=[pl.BlockSpec((tr, 2 * cc), lambda i: (i, 0)), pl.BlockSpec(dw_w.shape, lambda i: (0, 0))],
        out_shape=[jax.ShapeDtypeStruct((n_ext, 2 * cc), BF16), jax.ShapeDtypeStruct(dw_w.shape, F32)],
        scratch_shapes=[pltpu.VMEM((tr + 2 * CONV_HALO, cc), F32), pltpu.VMEM((tr + 2 * CONV_HALO, cc), F32),
                        pltpu.VMEM((SUBLANES, tr + 2 * CONV_HALO - SUBLANES, cc), F32),
                        pltpu.VMEM((SUBLANES, tr + 2 * CONV_HALO - SUBLANES, cc), F32),
                        pltpu.VMEM((tr, cc), F32)],
        compiler_params=_cp(("arbitrary",)))(*args)


def _sgu_parts(u, v, ln_g, ln_b):
    mu = jnp.mean(v, axis=-1, keepdims=True)
    xc = v - mu
    rs = lax.rsqrt(jnp.mean(xc * xc, axis=-1, keepdims=True) + EPS)
    xh = xc * rs
    return u, xh, rs, xh * ln_g + ln_b


def _sgu_fwd(p1, ln_g, ln_b, ws, bs_t, tr):
    n_lat, w3 = p1.shape
    w = w3 // 3
    ng, ch = ws.shape[0], ws.shape[1]
    gwid = w // ng
    n_ch = tr // ch

    def body(pu_ref, pv_ref, pg_ref, g_ref, b_ref, ws_ref, bs_ref, o_ref):
        u, _, _, vln = _sgu_parts(_gelu(pu_ref[...].astype(F32)), _gelu(pv_ref[...].astype(F32)),
                                  g_ref[...], b_ref[...])
        gate = _silu(pg_ref[...].astype(F32))
        vb = vln.astype(BF16)
        for c in range(n_ch):
            rs_ = slice(c * ch, (c + 1) * ch)
            for gi in range(ng):
                cs_ = slice(gi * gwid, (gi + 1) * gwid)
                mixed = jnp.dot(ws_ref[gi], vb[rs_, cs_], preferred_element_type=F32) + bs_ref[:, gi:gi + 1]
                o_ref[rs_, cs_] = (u[rs_, cs_] * mixed * gate[rs_, cs_]).astype(BF16)

    col = lambda t: pl.BlockSpec((tr, w), lambda i, _t=t: (i, _t))
    return pl.pallas_call(
        body, name="sgu_fwd", grid=(n_lat // tr,),
        in_specs=[col(0), col(1), col(2), _vec(w), _vec(w),
                  pl.BlockSpec(ws.shape, lambda i: (0, 0, 0)), pl.BlockSpec(bs_t.shape, lambda i: (0, 0))],
        out_specs=pl.BlockSpec((tr, w), lambda i: (i, 0)),
        out_shape=jax.ShapeDtypeStruct((n_lat, w), BF16),
        compiler_params=_cp(("parallel",)))(p1, p1, p1, ln_g, ln_b, ws, bs_t)


def _sgu_bwd(dm, p1, ln_g, ln_b, ws, ws_t, bs_t, tr):
    n_lat, w3 = p1.shape
    w = w3 // 3
    ng, ch = ws.shape[0], ws.shape[1]
    gwid = w // ng
    n_ch = tr // ch
    dn_nt = (((1,), (1,)), ((), ()))

    def body(dm_ref, pu_ref, pv_ref, pg_ref, g_ref, b_ref, ws_ref, wst_ref, bs_ref,
             dp_ref, dws_ref, dbs_ref, dg_ref, dbb_ref, dvln_scr):
        i = pl.program_id(0)

        @pl.when(i == 0)
        def _():
            dws_ref[...] = jnp.zeros_like(dws_ref)
            dbs_ref[...] = jnp.zeros_like(dbs_ref)
            dg_ref[...] = jnp.zeros_like(dg_ref)
            dbb_ref[...] = jnp.zeros_like(dbb_ref)

        puv, pvv, pgv = pu_ref[...].astype(F32), pv_ref[...].astype(F32), pg_ref[...].astype(F32)
        gu, dgu = _gelu_and_grad(puv)
        gv, dgv = _gelu_and_grad(pvv)
        gate, dgate = _silu_and_grad(pgv)
        u, xh, rs, vln = _sgu_parts(gu, gv, g_ref[...], b_ref[...])
        dmv = dm_ref[...].astype(F32)
        vb = vln.astype(BF16)
        dmu = dmv * u
        du_pre = dmv * gate * dgu
        dg_pre = dmu * dgate
        dmix_all = dmu * gate
        dbs_cols = [jnp.zeros((ch, 1), F32) for _ in range(ng)]
        for c in range(n_ch):
            rs_ = slice(c * ch, (c + 1) * ch)
            for gi in range(ng):
                cs_ = slice(gi * gwid, (gi + 1) * gwid)
                mixed = jnp.dot(ws_ref[gi], vb[rs_, cs_], preferred_element_type=F32) + bs_ref[:, gi:gi + 1]
                dmixed = dmix_all[rs_, cs_]
                dmb = dmixed.astype(BF16)
                dp_ref[rs_, gi * gwid:(gi + 1) * gwid] = (du_pre[rs_, cs_] * mixed).astype(BF16)
                dp_ref[rs_, 2 * w + gi * gwid:2 * w + (gi + 1) * gwid] = (dg_pre[rs_, cs_] * mixed).astype(BF16)
                dvln_scr[rs_, cs_] = jnp.dot(wst_ref[gi], dmb, preferred_element_type=F32)
                dws_ref[gi] += lax.dot_general(dmb, vb[rs_, cs_], dn_nt, preferred_element_type=F32)
                dbs_cols[gi] = dbs_cols[gi] + jnp.sum(dmixed, axis=-1, keepdims=True)
        dbs_ref[...] += jnp.concatenate(dbs_cols, axis=1)
        dvln = dvln_scr[...]
        dg_ref[...] += jnp.sum(dvln * xh, axis=0, keepdims=True)
        dbb_ref[...] += jnp.sum(dvln, axis=0, keepdims=True)
        dxh = dvln * g_ref[...]
        dv = rs * (dxh - jnp.mean(dxh, axis=-1, keepdims=True) - xh * jnp.mean(dxh * xh, axis=-1, keepdims=True))
        dp_ref[:, w:2 * w] = (dv * dgv).astype(BF16)

    col = lambda t: pl.BlockSpec((tr, w), lambda i, _t=t: (i, _t))
    return pl.pallas_call(
        body, name="sgu_bwd", grid=(n_lat // tr,),
        in_specs=[pl.BlockSpec((tr, w), lambda i: (i, 0)), col(0), col(1), col(2), _vec(w), _vec(w),
                  pl.BlockSpec(ws.shape, lambda i: (0, 0, 0)), pl.BlockSpec(ws.shape, lambda i: (0, 0, 0)),
                  pl.BlockSpec(bs_t.shape, lambda i: (0, 0))],
        out_specs=[pl.BlockSpec((tr, w3), lambda i: (i, 0)), pl.BlockSpec(ws.shape, lambda i: (0, 0, 0)),
                   pl.BlockSpec(bs_t.shape, lambda i: (0, 0)), _vec(w), _vec(w)],
        out_shape=[jax.ShapeDtypeStruct((n_lat, w3), BF16), jax.ShapeDtypeStruct(ws.shape, F32),
                   jax.ShapeDtypeStruct(bs_t.shape, F32), jax.ShapeDtypeStruct((1, w), F32),
                   jax.ShapeDtypeStruct((1, w), F32)],
        scratch_shapes=[pltpu.VMEM((tr, w), F32)],
        compiler_params=_cp(("arbitrary",)))(dm, p1, p1, p1, ln_g, ln_b, ws, ws_t, bs_t)


def _final_loss(x2, target, final_g, o_prev, gate_prev, tr):
    n_lat, d = x2.shape

    def body(x_ref, t_ref, g_ref, o_ref, gp_ref, dx_ref, do_ref, ls_ref, dg_ref, dgp_ref):
        i = pl.program_id(0)

        @pl.when(i == 0)
        def _():
            ls_ref[...] = jnp.zeros_like(ls_ref)
            dg_ref[...] = jnp.zeros_like(dg_ref)
            dgp_ref[...] = jnp.zeros_like(dgp_ref)

        xv = x_ref[...]
        gv = g_ref[...]
        rs = lax.rsqrt(jnp.mean(xv * xv, axis=-1, keepdims=True) + EPS)
        xn = xv * rs
        err = xn * gv - t_ref[...]
        ls_ref[...] += jnp.sum(err * err, axis=0, keepdims=True)
        dy = err * (1.0 / d)
        dg_ref[...] += jnp.sum(dy * xn, axis=0, keepdims=True)
        gy = dy * gv
        dx = rs * (gy - xn * jnp.mean(gy * xn, axis=-1, keepdims=True))
        dx_ref[...] = dx
        do_ref[...] = (gp_ref[...] * dx).astype(BF16)
        dgp_ref[...] += jnp.sum(dx * o_ref[...].astype(F32), axis=0, keepdims=True)

    row = pl.BlockSpec((tr, d), lambda i: (i, 0))
    return pl.pallas_call(
        body, name="final_loss", grid=(n_lat // tr,), in_specs=[row, row, _vec(d), row, _vec(d)],
        out_specs=[row, row, _vec(d), _vec(d), _vec(d)],
        out_shape=[jax.ShapeDtypeStruct((n_lat, d), F32), jax.ShapeDtypeStruct((n_lat, d), BF16)]
                  + [jax.ShapeDtypeStruct((1, d), F32)] * 3,
        compiler_params=_cp(("arbitrary",)))(x2, target, final_g, o_prev, gate_prev)


def _pack(arrays):
    flat = jnp.concatenate([a.reshape(-1).astype(F32) for a in arrays])
    n = flat.shape[0]
    rows = -(-n // LANES)
    rows = -(-rows // PACK_ROWS) * PACK_ROWS
    return jnp.pad(flat, (0, rows * LANES - n)).reshape(rows, LANES)


def _unpack(buf, shapes):
    flat = buf.reshape(buf.shape[:-2] + (-1,))
    out, pos = [], 0
    for shp in shapes:
        n = math.prod(shp)
        out.append(flat[..., pos:pos + n].reshape(buf.shape[:-2] + tuple(shp)))
        pos += n
    return out


def _rope_tables(n_lat, n_ctx):
    rows = n_lat // GRID_W
    row = jnp.repeat(jnp.arange(rows, dtype=F32), GRID_W)
    col = jnp.tile(jnp.arange(GRID_W, dtype=F32), rows)
    n_freq, axis_dim = HEAD_DIM // 4, HEAD_DIM // 2
    inv = jnp.power(ROPE_THETA, jnp.arange(n_freq, dtype=F32) * (-2.0 / axis_dim))
    ang = jnp.concatenate([row[:, None] * inv, col[:, None] * inv], axis=-1)
    cos, sin = jnp.cos(ang), jnp.sin(ang)
    cos_i = jnp.repeat(cos, 2, axis=-1)
    sin_s = jnp.stack([-sin, sin], axis=-1).reshape(n_lat, HEAD_DIM)
    cos_i = jnp.concatenate([cos_i, jnp.ones((n_ctx, HEAD_DIM), F32)], axis=0)
    sin_s = jnp.concatenate([sin_s, jnp.zeros((n_ctx, HEAD_DIM), F32)], axis=0)
    return cos_i, sin_s


def kernel(x, c, ctx, c_ctx, ada_w, ada_b, norm_g, ev_w_in, ev_q_norm, ev_k_norm, ev_dw_w, ev_dw_b, ev_ln_g, ev_ln_b, ev_w_out, od_w_in, od_ln_g, od_ln_b, od_ws, od_bs, od_w_out, final_g, loss_target, m_c_ctx, m_ada_w, m_ada_b, m_norm_g, m_ev_w_in, m_ev_q_norm, m_ev_k_norm, m_ev_dw_w, m_ev_dw_b, m_ev_ln_g, m_ev_ln_b, m_ev_w_out, m_od_w_in, m_od_ln_g, m_od_ln_b, m_od_ws, m_od_bs, m_od_w_out, m_final_g, v_c_ctx, v_ada_w, v_ada_b, v_norm_g, v_ev_w_in, v_ev_q_norm, v_ev_k_norm, v_ev_dw_w, v_ev_dw_b, v_ev_ln_g, v_ev_ln_b, v_ev_w_out, v_od_w_in, v_od_ln_g, v_od_ln_b, v_od_ws, v_od_bs, v_od_w_out, v_final_g):
    weights = dict(c_ctx=c_ctx, ada_w=ada_w, ada_b=ada_b, norm_g=norm_g, ev_w_in=ev_w_in, ev_q_norm=ev_q_norm,
                   ev_k_norm=ev_k_norm, ev_dw_w=ev_dw_w, ev_dw_b=ev_dw_b, ev_ln_g=ev_ln_g, ev_ln_b=ev_ln_b,
                   ev_w_out=ev_w_out, od_w_in=od_w_in, od_ln_g=od_ln_g, od_ln_b=od_ln_b, od_ws=od_ws, od_bs=od_bs,
                   od_w_out=od_w_out, final_g=final_g)
    mom_m = dict(c_ctx=m_c_ctx, ada_w=m_ada_w, ada_b=m_ada_b, norm_g=m_norm_g, ev_w_in=m_ev_w_in,
                 ev_q_norm=m_ev_q_norm, ev_k_norm=m_ev_k_norm, ev_dw_w=m_ev_dw_w, ev_dw_b=m_ev_dw_b,
                 ev_ln_g=m_ev_ln_g, ev_ln_b=m_ev_ln_b, ev_w_out=m_ev_w_out, od_w_in=m_od_w_in, od_ln_g=m_od_ln_g,
                 od_ln_b=m_od_ln_b, od_ws=m_od_ws, od_bs=m_od_bs, od_w_out=m_od_w_out, final_g=m_final_g)
    mom_v = dict(c_ctx=v_c_ctx, ada_w=v_ada_w, ada_b=v_ada_b, norm_g=v_norm_g, ev_w_in=v_ev_w_in,
                 ev_q_norm=v_ev_q_norm, ev_k_norm=v_ev_k_norm, ev_dw_w=v_ev_dw_w, ev_dw_b=v_ev_dw_b,
                 ev_ln_g=v_ev_ln_g, ev_ln_b=v_ev_ln_b, ev_w_out=v_ev_w_out, od_w_in=v_od_w_in, od_ln_g=v_od_ln_g,
                 od_ln_b=v_od_ln_b, od_ws=v_od_ws, od_bs=v_od_bs, od_w_out=v_od_w_out, final_g=v_final_g)
    order = list(weights)

    _, n_lat, d = x.shape
    n_ctx = ctx.shape[1]
    n_ext = n_lat + n_ctx
    ev_in = ev_w_in.shape[-1] * N_CHIP
    ev_mix = ev_w_out.shape[1] * N_CHIP
    conv_ch = ev_dw_b.shape[-1]
    conv_w = ev_dw_w.shape[1]
    attn_w = ev_mix - conv_ch
    kv_w = N_KV_HEADS * HEAD_DIM
    assert ev_in == 2 * kv_w + 2 * attn_w + 3 * conv_ch and conv_w // 2 < CONV_HALO
    sgu_w = od_w_out.shape[1] * N_CHIP
    wa = ada_w.shape[-1]
    cb = math.gcd(2 * kv_w, attn_w, conv_ch)
    off = dict(k=0, v=kv_w, q=2 * kv_w, za=2 * kv_w + attn_w, a=2 * kv_w + 2 * attn_w,
               b=2 * kv_w + 2 * attn_w + conv_ch, zb=2 * kv_w + 2 * attn_w + 2 * conv_ch)
    dims = dict(kv_w=kv_w, attn_w=attn_w, conv_ch=conv_ch, conv_w=conv_w, cb=cb, off=off)
    tr = 256 if (n_lat % 256 == 0 and n_ctx % 256 == 0) else 128

    mx, my, mc = lax.axis_index("x"), lax.axis_index("y"), lax.axis_index("c")
    me = 4 * mx + 2 * my + mc
    chip = 2 * mx + my

    x2d, tgt2d, ctx2d = x[0], loss_target[0], ctx[0]
    ev_dw_w_l = ev_dw_w[0]
    dwc = ev_dw_w_l.shape[1]
    lnc = od_ln_g.shape[1]

    g_c = _allgather_small(jnp.broadcast_to(c, (8, d)), "gather_cond")[:, 0, :]
    c_rows = jnp.concatenate([g_c, c_ctx[None, :], jnp.zeros((MOD_ROWS - N_DEV - 1, d), F32)], axis=0)
    c_rows_t = c_rows.T
    ada_b_shard = lax.dynamic_slice_in_dim(ada_b, chip * wa, wa, axis=1)[:, None, :]
    mod_part = _mod_fwd(c_rows_t, ada_w, ada_b_shard)
    part_shapes = [(2, MOD_ROWS, wa), (conv_w, dwc), (1, lnc), (1, lnc)]
    g_parts = _allgather_small(_pack([mod_part, ev_dw_w_l, od_ln_g, od_ln_b]), "gather_mod")
    per_chip = [_unpack(g_parts[2 * s], part_shapes) for s in range(N_CHIP)]
    mod_all = jnp.concatenate([p[0] for p in per_chip], axis=-1)
    dw_w_full = jnp.concatenate([p[1] for p in per_chip], axis=-1)
    od_ln_g_full = jnp.concatenate([p[2] for p in per_chip], axis=-1)
    od_ln_b_full = jnp.concatenate([p[3] for p in per_chip], axis=-1)
    dw_w_pad = jnp.pad(dw_w_full, ((0, 2 * CONV_HALO - conv_w), (0, 0)))
    mod_me = lax.dynamic_slice_in_dim(mod_all, me, 1, axis=1)
    shift0, scale0, gate0 = mod_me[0, :, :d], mod_me[0, :, d:2 * d], mod_me[0, :, 2 * d:]
    shift1, scale1, gate1 = mod_me[1, :, :d], mod_me[1, :, d:2 * d], mod_me[1, :, 2 * d:]
    shift_c, scale_c = mod_all[0, N_DEV:N_DEV + 1, :d], mod_all[0, N_DEV:N_DEV + 1, d:2 * d]
    g0, g1 = norm_g[0:1], norm_g[1:2]

    lay = dict(ev_w_in=_Sharded((d, ev_in), True), ev_w_out=_Sharded((ev_mix, d), False),
               od_w_in=_Sharded((d, 3 * sgu_w), True), od_w_out=_Sharded((sgu_w, d), False))
    big = list(lay)
    chip_arr = jnp.reshape(chip, (1,)).astype(jnp.int32)
    core_arr = jnp.reshape(mc, (1,)).astype(jnp.int32)
    own = {n: _cast_into_full(weights[n][0], lay[n], chip_arr, f"cast_{n}") for n in big}
    gather_ev_out = _gather_comm([own["ev_w_out"]], [lay["ev_w_out"]], forward_half_way=True)
    layer1 = ["od_w_in", "od_w_out"]
    gather_layer1 = _gather_comm([own[n] for n in layer1], [lay[n] for n in layer1], forward_half_way=True)

    def reduce_start(g, n):
        theirs = _pair_exchange([g], [lay[n]], f"pair_exchange_{n}")[0]
        psum = _pair_sum(g, theirs, lay[n], core_arr, f"pair_sum_{n}")
        return psum, _scatter_comm([psum], [lay[n]])

    h0, w_ev_in = _adaln_fwd(x2d, g0, shift0, scale0, tr, "adaln0_fwd",
                             comm=_gather_comm([own["ev_w_in"]], [lay["ev_w_in"]]), out_rows=n_ext)
    w_full = {"ev_w_in": w_ev_in}
    h0e = _adaln_fwd(ctx2d, g0, shift_c, scale_c, tr, "adaln0_ctx_fwd", into=h0, row0=n_lat)
    tm_e = _pick(n_ext, (1408, 768, 640, 512, 256, 128))
    tk_e = _pick(n_ext, (768, 640, 512, 256, 128))
    tm_l = _pick(n_lat, (1024, 512, 256, 128))
    p0, *gathered = _mm(h0e, w_full["ev_w_in"], name="mm_ev_in", tm=tm_e, tn=_pick(ev_in, (1408, 512, 256, 128)), tk=d,
                        out_dtype=BF16, comm=gather_ev_out)
    w_full["ev_w_out"] = gathered[0]
    cos_i, sin_s = _rope_tables(n_lat, n_ctx)
    q_hat, k_all, v_all = _qk_prep(p0, cos_i, sin_s, ev_q_norm, ev_k_norm, dims, tr)
    tq = _pick(n_lat, (256, 128))
    tkk = _pick(n_ext, (1408, 640, 512, 256, 128))
    attn, lse, *gathered = _flash_fwd(q_hat, k_all, v_all, n_lat, dims, tq, tkk, comm=gather_layer1)
    w_full.update(zip(layer1, gathered))
    mix, yc = _mix_fwd(attn, p0, dw_w_pad, ev_dw_b, ev_ln_g, ev_ln_b, dims, tr)
    o0, x1 = _mm(mix, w_full["ev_w_out"], name="mm_ev_out", tm=_pick(n_lat, (512, 256, 128)),
                 tn=_pick(d, (2048, 1024, 512, 256)),
                 tk=ev_mix, out_dtype=BF16, res=x2d, gate=gate0)

    h1 = _adaln_fwd(x1, g1, shift1, scale1, tr, "adaln1_fwd")
    p1 = _mm(h1, w_full["od_w_in"], name="mm_od_in", tm=tm_l, tn=_pick(3 * sgu_w, (1536, 512, 256, 128)), tk=d,
             out_dtype=BF16)
    ws_b = od_ws[0].astype(BF16)
    ws_t_b = jnp.swapaxes(od_ws[0], 1, 2).astype(BF16)
    bs_t = od_bs[0].T
    m1 = _sgu_fwd(p1, od_ln_g_full, od_ln_b_full, ws_b, bs_t, tr)
    o1, x2 = _mm(m1, w_full["od_w_out"], name="mm_od_out", tm=_pick(n_lat, (512, 256, 128)),
                 tn=_pick(d, (2048, 1024, 512, 256)),
                 tk=sgu_w, out_dtype=BF16, res=x1, gate=gate1)

    dx2, do1, loss_cols, d_final_g, dgate1 = _final_loss(x2, tgt2d, final_g[None, :], o1, gate1, tr)
    loss = lax.psum(0.5 / d * jnp.sum(loss_cols), ("x", "y", "c"))

    tk_l = _pick(n_lat, (1024, 512, 256, 128))
    psums, slots = {}, {}
    tk_nt = (1408, 1024, 768, 512, 256, 128)
    g_od_w_out = _mm(m1, do1, name="mm_od_out_dw", ta=True, tm=_pick(sgu_w, (2048, 1024, 512, 256)),
                     tn=_pick(d, (1024, 512, 256)), tk=_pick(n_lat, (2048, 1024, 512, 256, 128)), out_dtype=BF16)
    psums["od_w_out"], sc = reduce_start(g_od_w_out, "od_w_out")
    dm1, slots["od_w_out"] = _mm(do1, w_full["od_w_out"], name="mm_od_out_dx", tb=True, tm=tm_l,
                                 tn=_pick(sgu_w, (2048, 1024, 512, 256)), tk=d, out_dtype=BF16, comm=sc)
    dp1, d_ws, d_bs_t, d_od_ln_g, d_od_ln_b = _sgu_bwd(dm1, p1, od_ln_g_full, od_ln_b_full, ws_b, ws_t_b, bs_t, tr)
    small1 = [dgate1, d_final_g, d_od_ln_g, d_od_ln_b, d_ws, d_bs_t.T]
    g_od_w_in, g_small1 = _mm(h1, dp1, name="mm_od_in_dw", ta=True, tm=_pick(d, (2048, 1024, 512, 256)),
                              tn=_pick(3 * sgu_w, (1536, 768, 512, 384, 256, 128)), tk=tk_l, out_dtype=BF16,
                              comm=_allgather_comm(_pack(small1)))
    psums["od_w_in"], sc = reduce_start(g_od_w_in, "od_w_in")
    dh1, slots["od_w_in"] = _mm(dp1, w_full["od_w_in"], name="mm_od_in_dx", tb=True, tm=tm_l,
                                tn=_pick(d, (2048, 1024, 512, 256)), tk=_pick(3 * sgu_w, (2048,) + tk_nt),
                                out_dtype=BF16, comm=sc)
    zero_d = jnp.zeros((1, d), F32)
    b1 = _adaln_bwd(x1, dh1, 0, g1, scale1, zero_d, tr, "adaln1_bwd", dres=dx2, o_prev=o0, gate_prev=gate0)
    dx1, do0, dgate0 = b1["dx"], b1["do_prev"], b1["dgate_prev"]

    g_ev_w_out = _mm(mix, do0, name="mm_ev_out_dw", ta=True, tm=_pick(ev_mix, (2048, 1024, 512, 256)),
                     tn=_pick(d, (1024, 512, 256)), tk=_pick(n_lat, (2048, 1024, 512, 256, 128)), out_dtype=BF16)
    psums["ev_w_out"], sc = reduce_start(g_ev_w_out, "ev_w_out")
    dmix, slots["ev_w_out"] = _mm(do0, w_full["ev_w_out"], name="mm_ev_out_dx", tb=True, tm=tm_l,
                                  tn=_pick(ev_mix, (2048, 1024, 512, 256)), tk=d, out_dtype=BF16, comm=sc)
    dattn, dza, dzb, dyc, d_ev_ln_g, d_ev_ln_b, d_dw_b = _mix_bwd_pointwise(
        dmix, attn, p0, yc, ev_ln_g, ev_ln_b, dims, tr, n_ext)
    dab, d_dw_w_pad = _conv_bwd(dyc, p0, dw_w_pad, dims, tr, n_ext)
    dq_hat, dk_hat, dv_all = _flash_bwd(q_hat, k_all, v_all, dattn, attn, lse, n_lat, dims, tq, tkk)
    dkvq, d_q_norm, d_k_norm = _qk_prep_bwd(dq_hat, dk_hat, dv_all, p0, cos_i, sin_s, ev_q_norm, ev_k_norm,
                                            dims, tr, n_lat)
    dp0 = jnp.concatenate([dkvq, dza, dab, dzb], axis=1)
    small2 = [b1["dshift"], b1["dscale"], dgate0, b1["dg"], d_q_norm, d_k_norm, d_dw_w_pad[:conv_w], d_dw_b,
              d_ev_ln_g, d_ev_ln_b]
    g_ev_w_in, g_small2 = _mm(h0e, dp0, name="mm_ev_in_dw", ta=True, tm=_pick(d, (2048, 1024, 512, 256)),
                              tn=_pick(ev_in, (1408, 768, 512, 256, 128)),
                              tk=tk_e, out_dtype=BF16,
                              comm=_allgather_comm(_pack(small2)))
    psums["ev_w_in"], sc = reduce_start(g_ev_w_in, "ev_w_in")
    dh0, slots["ev_w_in"] = _mm(dp0, w_full["ev_w_in"], name="mm_ev_in_dx", tb=True, tm=tm_e,
                                tn=_pick(d, (2048, 1024, 512, 256)), tk=_pick(ev_in, tk_nt), out_dtype=BF16, comm=sc)
    lays = [lay[n] for n in big]
    halves = [_chip_sum(psums[n], slots[n], lay[n], chip_arr, core_arr, f"chip_sum_{n}") for n in big]
    bc = _adaln_bwd(ctx2d, dh0, n_lat, g0, scale_c, zero_d, tr, "adaln0_ctx_bwd", comm=_share_comm(halves, lays))
    g_big = dict(zip(big, bc["carried"]))
    b0 = _adaln_bwd(x2d, dh0, 0, g0, scale0, bc["dg"], tr, "adaln0_bwd", dres=dx1)
    grad_x = b0["dx"]

    small3 = [b0["dshift"], b0["dscale"], bc["dshift"], bc["dscale"], b0["dg"]]
    g_small3 = _allgather_small(_pack(small3), "gather_small_grads")

    def totals_and_rows(gathered, parts, n_rows, name):
        shapes = [a.shape for a in parts]
        tot = _unpack(_sum_slots(gathered, name), shapes)
        return tot, [r[:, 0, :] for r in _unpack(gathered, shapes[:n_rows])]

    (t_dgate1, t_fg, t_oln_g, t_oln_b, t_ws, t_bs), (dgate1_rows,) = totals_and_rows(
        g_small1, small1, 1, "sum_small_grads1")
    ((t_dshift1, t_dscale1, t_dgate0, t_g1, t_qn, t_kn, t_dw_w, t_dw_b, t_eln_g, t_eln_b),
     (dshift1_rows, dscale1_rows, dgate0_rows)) = totals_and_rows(g_small2, small2, 3, "sum_small_grads2")
    (t_dshift0, t_dscale0, t_dshift_c, t_dscale_c, t_g0), (dshift0_rows, dscale0_rows) = totals_and_rows(
        g_small3, small3, 2, "sum_small_grads3")
    zeros_d = jnp.zeros((1, d), F32)
    t_dmodc = jnp.concatenate([t_dshift_c, t_dscale_c, zeros_d], axis=1)
    t_dmod0 = jnp.concatenate([t_dshift0, t_dscale0, t_dgate0], axis=1)
    t_dmod1 = jnp.concatenate([t_dshift1, t_dscale1, t_dgate1], axis=1)
    dmod0_rows = jnp.concatenate([dshift0_rows, dscale0_rows, dgate0_rows], axis=1)
    dmod1_rows = jnp.concatenate([dshift1_rows, dscale1_rows, dgate1_rows], axis=1)
    pad_rows = jnp.zeros((MOD_ROWS - N_DEV - 1, 3 * d), F32)
    dm_l0 = jnp.concatenate([dmod0_rows, t_dmodc, pad_rows], axis=0)
    dm_l1 = jnp.concatenate([dmod1_rows, jnp.zeros((MOD_ROWS - N_DEV, 3 * d), F32)], axis=0)
    dm_shard = lax.dynamic_slice_in_dim(jnp.stack([dm_l0, dm_l1]), chip * wa, wa, axis=2)
    g_ada_w, dsc = _mod_bwd(c_rows_t, dm_shard, ada_w)
    g_dsc = _allgather_small(_pack([dsc[0]]), "gather_cctx")
    g_c_ctx = _cctx_grad(g_dsc, _pack([c_ctx])).reshape(-1)[:d]
    g_ada_b = jnp.stack([t_dmod0[0] + t_dmodc[0], t_dmod1[0]])

    grads = dict(
        c_ctx=g_c_ctx, ada_w=g_ada_w, ada_b=g_ada_b, norm_g=jnp.concatenate([t_g0, t_g1], axis=0),
        ev_w_in=g_big["ev_w_in"][None], ev_q_norm=t_qn, ev_k_norm=t_kn,
        ev_dw_w=lax.dynamic_slice_in_dim(t_dw_w, chip * dwc, dwc, axis=1)[None], ev_dw_b=t_dw_b,
        ev_ln_g=t_eln_g, ev_ln_b=t_eln_b, ev_w_out=g_big["ev_w_out"][None], od_w_in=g_big["od_w_in"][None],
        od_ln_g=lax.dynamic_slice_in_dim(t_oln_g, chip * lnc, lnc, axis=1),
        od_ln_b=lax.dynamic_slice_in_dim(t_oln_b, chip * lnc, lnc, axis=1),
        od_ws=t_ws[None], od_bs=t_bs[None], od_w_out=g_big["od_w_out"][None], final_g=t_fg[0])
    grads = {n: grads[n].reshape(weights[n].shape) for n in order}

    delta, new_m, new_v = {}, {}, {}
    large = ("ada_w", "ev_w_in", "ev_w_out", "od_w_in", "od_w_out")
    for n in large:
        shp = weights[n].shape
        as2d = lambda a: a.reshape(-1, shp[-1])
        outs = _adamw(as2d(weights[n]), as2d(grads[n]), as2d(mom_m[n]), as2d(mom_v[n]), f"adamw_{n}",
                      copy_grad=n in big)
        delta[n], new_m[n], new_v[n] = (o.reshape(shp) for o in outs[:3])
        if n in big:
            grads[n] = outs[3].reshape(shp)
    rest_names = [n for n in order if n not in large]
    rest_shapes = [weights[n].shape for n in rest_names]
    dl, nm, nv = _adamw(_pack([weights[n] for n in rest_names]), _pack([grads[n] for n in rest_names]),
                        _pack([mom_m[n] for n in rest_names]), _pack([mom_v[n] for n in rest_names]), "adamw_small")
    for n, a, b_, c_ in zip(rest_names, _unpack(dl, rest_shapes), _unpack(nm, rest_shapes), _unpack(nv, rest_shapes)):
        delta[n], new_m[n], new_v[n] = a, b_, c_

    return (loss, grad_x[None], *[grads[n] for n in order], *[delta[n] for n in order],
            *[new_m[n] for n in order], *[new_v[n] for n in order])
```

```python
import math

import jax
import jax.numpy as jnp
from jax import lax
from jax.experimental import pallas as pl
from jax.experimental.pallas import tpu as pltpu

F32 = jnp.float32
BF16 = jnp.bfloat16
EPS = 1e-6
GRID_W = 64
ROPE_THETA = 10000.0
HEAD_DIM = 128
LOG2E = math.log2(math.e)
N_KV_HEADS = 2
CONV_HALO = 16
LANES = 128
SUBLANES = 8
STENCIL_ROWS = 32
STENCIL_CHAINS = 4
REDUCE_ROWS = 32
N_DEV = 8
N_CHIP = 4
MOD_ROWS = 16
PACK_ROWS = 64
ADAM_LR, ADAM_B1, ADAM_B2, ADAM_EPS, ADAM_WD, ADAM_STEP = 0.001, 0.9, 0.999, 1e-08, 0.01, 10
VMEM_LIMIT = 56 * 1024 * 1024
MESH = pl.DeviceIdType.MESH
ANY = pl.BlockSpec(memory_space=pl.ANY)
VMEM_SPEC = pl.BlockSpec(memory_space=pltpu.VMEM)
CHIP_DELTAS = ((1, 0), (0, 1), (1, 1))


def _cp(sem=None):
    return pltpu.CompilerParams(dimension_semantics=sem, vmem_limit_bytes=VMEM_LIMIT)


def _pick(n, cands):
    for c in cands:
        if n % c == 0:
            return c
    raise ValueError(f"no tile for {n} in {cands}")


def _sigmoid(x):
    return 1.0 / (1.0 + jnp.exp(-x))


def _silu(x):
    return x * _sigmoid(x)


def _silu_and_grad(x):
    s = _sigmoid(x)
    y = x * s
    return y, s + y * (1.0 - s)


def _dsilu(x):
    return _silu_and_grad(x)[1]


_GELU_C = math.sqrt(2.0 / math.pi)
_GELU_A = 0.044715


def _gelu_and_grad(x):
    x2 = x * x
    t = jnp.tanh(x * (_GELU_C + (_GELU_C * _GELU_A) * x2))
    h = 0.5 + 0.5 * t
    return x * h, h + x * (1.0 - t * t) * (0.5 * _GELU_C + (1.5 * _GELU_C * _GELU_A) * x2)


def _gelu(x):
    x2 = x * x
    return x * (0.5 + 0.5 * jnp.tanh(x * (_GELU_C + (_GELU_C * _GELU_A) * x2)))


def _vec(d):
    return pl.BlockSpec((1, d), lambda *_: (0, 0))


def _cat(refs):
    parts = [r[...].astype(F32) for r in refs]
    return parts[0] if len(parts) == 1 else jnp.concatenate(parts, axis=1)


def _col_specs(rows, off, width, cb, row_map):
    assert off % cb == 0 and width % cb == 0
    return [pl.BlockSpec((rows, cb), (lambda *g, _c=off // cb + t: (row_map(*g), _c))) for t in range(width // cb)]


def _my_pos():
    return lax.axis_index("x"), lax.axis_index("y"), lax.axis_index("c")


def _allgather_small(x, name):
    r, c = x.shape

    def body(x_ref, out_ref, send_sems, recv_sems, local_sem):
        mx, my, mc = _my_pos()
        me = 4 * mx + 2 * my + mc
        mine = pltpu.make_async_copy(x_ref, out_ref.at[me], local_sem)
        mine.start()
        deltas = [(dx, dy, dc) for dx in (0, 1) for dy in (0, 1) for dc in (0, 1) if (dx, dy, dc) != (0, 0, 0)]
        sends = []
        for k, (dx, dy, dc) in enumerate(deltas):
            px, py, pc = (mx + dx) % 2, (my + dy) % 2, (mc + dc) % 2
            cp = pltpu.make_async_remote_copy(
                src_ref=x_ref, dst_ref=out_ref.at[me], send_sem=send_sems.at[k], recv_sem=recv_sems.at[k],
                device_id=(px, py, pc), device_id_type=MESH)
            cp.start()
            sends.append(cp)
        for k, (dx, dy, dc) in enumerate(deltas):
            px, py, pc = (mx + dx) % 2, (my + dy) % 2, (mc + dc) % 2
            peer = 4 * px + 2 * py + pc
            pltpu.make_async_remote_copy(
                src_ref=x_ref, dst_ref=out_ref.at[peer], send_sem=send_sems.at[k], recv_sem=recv_sems.at[k],
                device_id=(px, py, pc), device_id_type=MESH).wait_recv()
        for cp in sends:
            cp.wait_send()
        mine.wait()

    return pl.pallas_call(
        body, name=name,
        out_shape=jax.ShapeDtypeStruct((N_DEV, r, c), x.dtype),
        in_specs=[VMEM_SPEC], out_specs=VMEM_SPEC,
        scratch_shapes=[pltpu.SemaphoreType.DMA((N_DEV - 1,)), pltpu.SemaphoreType.DMA((N_DEV - 1,)),
                        pltpu.SemaphoreType.DMA],
        compiler_params=pltpu.CompilerParams(vmem_limit_bytes=VMEM_LIMIT),
    )(x)


class _Sharded:
    def __init__(self, full_shape, by_cols):
        self.full = full_shape
        self.by_cols = by_cols
        rows, cols = full_shape
        if by_cols:
            self.shard, self.half, self.halves = (rows, cols // N_CHIP), (rows // 2, cols // N_CHIP), (rows // 2, cols)
        else:
            self.shard, self.half, self.halves = (rows // N_CHIP, cols), (rows // N_CHIP, cols // 2), (rows, cols // 2)

    def region(self, ref, s, h):
        if self.by_cols:
            return ref.at[pl.ds(h * self.half[0], self.half[0]), pl.ds(s * self.shard[1], self.shard[1])]
        return ref.at[pl.ds(s * self.shard[0], self.shard[0]), pl.ds(h * self.half[1], self.half[1])]

    def halves_of_full(self, ref, h):
        if self.by_cols:
            return ref.at[pl.ds(h * self.halves[0], self.halves[0]), :]
        return ref.at[:, pl.ds(h * self.halves[1], self.halves[1])]

    def region_in_halves(self, ref, s):
        if self.by_cols:
            return ref.at[:, pl.ds(s * self.shard[1], self.shard[1])]
        return ref.at[pl.ds(s * self.shard[0], self.shard[0]), :]

    def half_of_shard(self, ref, h):
        if self.by_cols:
            return ref.at[pl.ds(h * self.half[0], self.half[0]), :]
        return ref.at[:, pl.ds(h * self.half[1], self.half[1])]


def _row_tile(rows, row_bytes):
    for t in (512, 256, 128, 64, 32, 16):
        if rows % t == 0 and t * row_bytes <= 2 * 1024 * 1024:
            return t
    return 16


def _cast_into_full(w_shard, lay, chip_arr, name):
    r, c = lay.shard
    tr = _row_tile(r, c * 4)
    nt = r // tr

    def body(chip_ref, w_ref, o_ref):
        o_ref[...] = w_ref[...].astype(BF16)

    if lay.by_cols:
        out_map = lambda i, chip_ref: (i, chip_ref[0])
    else:
        out_map = lambda i, chip_ref: (chip_ref[0] * nt + i, 0)
    return pl.pallas_call(
        body, name=name,
        grid_spec=pltpu.PrefetchScalarGridSpec(
            num_scalar_prefetch=1, grid=(nt,),
            in_specs=[pl.BlockSpec((tr, c), lambda i, chip_ref: (i, 0))],
            out_specs=pl.BlockSpec((tr, c), out_map)),
        out_shape=jax.ShapeDtypeStruct(lay.full, BF16), compiler_params=_cp(("parallel",)))(chip_arr, w_shard)


class _Carried:
    def __init__(self, ins, out_shapes, aliases, sem_shape, start, finish, middle=None):
        self.ins, self.out_shapes, self.aliases, self.sem_shape = list(ins), list(out_shapes), dict(aliases), sem_shape
        self.start, self.finish, self.middle = start, finish, middle

    def scratch(self):
        return [pltpu.SemaphoreType.DMA(self.sem_shape), pltpu.SemaphoreType.DMA(self.sem_shape)]

    def split(self, in_refs, out_refs, scratch_refs):
        ni, no = len(self.ins), len(self.out_shapes)
        return in_refs[len(in_refs) - ni:], out_refs[len(out_refs) - no:], scratch_refs[-2], scratch_refs[-1]


def _call(body, *, name, grid, in_specs, out_specs, out_shape, args, sem, comm=None, aliases=None):
    if comm is None:
        return pl.pallas_call(body, name=name, grid=grid, in_specs=in_specs, out_specs=out_specs,
                              out_shape=out_shape, input_output_aliases=aliases or {},
                              compiler_params=_cp(sem))(*args)
    n_in, n_out, n_ci, n_co = len(in_specs), len(out_specs), len(comm.ins), len(comm.out_shapes)

    def carrying(*refs):
        in_refs, out_refs = refs[:n_in + n_ci], refs[n_in + n_ci:n_in + n_ci + n_out + n_co]
        carried = comm.split(in_refs, out_refs, refs[n_in + n_ci + n_out + n_co:])
        first, last = None, None
        for axis, extent in enumerate(grid):
            at0, at1 = pl.program_id(axis) == 0, pl.program_id(axis) == extent - 1
            first = at0 if first is None else jnp.logical_and(first, at0)
            last = at1 if last is None else jnp.logical_and(last, at1)

        @pl.when(first)
        def _():
            comm.start(*carried)

        if comm.middle is not None:
            half = pl.program_id(0) == grid[0] // 2
            for axis in range(1, len(grid)):
                half = jnp.logical_and(half, pl.program_id(axis) == 0)

            @pl.when(half)
            def _():
                comm.middle(*carried)

        body(*in_refs[:n_in], *out_refs[:n_out])

        @pl.when(last)
        def _():
            comm.finish(*carried)

    return pl.pallas_call(
        carrying, name=name, grid=grid, in_specs=list(in_specs) + [ANY] * n_ci,
        out_specs=list(out_specs) + [ANY] * n_co, out_shape=list(out_shape) + comm.out_shapes,
        scratch_shapes=comm.scratch(),
        input_output_aliases={**(aliases or {}), **{n_in + s: n_out + d for s, d in comm.aliases.items()}},
        compiler_params=_cp(("arbitrary",) * len(grid)))(*args, *comm.ins)


def _gather_comm(fulls, layouts, forward_half_way=False):
    n = len(fulls)

    def ici(ins, outs, send_sems, recv_sems, a, j, landed=False):
        mx, my, mc = _my_pos()
        dx, dy = CHIP_DELTAS[j]
        px, py = (mx + dx) % 2, (my + dy) % 2
        src_chip = 2 * px + py if landed else 2 * mx + my
        return pltpu.make_async_remote_copy(
            src_ref=layouts[a].region(ins[a], src_chip, mc), dst_ref=layouts[a].region(outs[a], src_chip, mc),
            send_sem=send_sems.at[a, j], recv_sem=recv_sems.at[a, j], device_id=(px, py, mc), device_id_type=MESH)

    def d2d(ins, outs, send_sems, recv_sems, a, j, landed=False):
        mx, my, mc = _my_pos()
        dx, dy = CHIP_DELTAS[j]
        other = 2 * ((mx + dx) % 2) + (my + dy) % 2
        half = 1 - mc if landed else mc
        region = layouts[a].region(outs[a], other, half)
        return pltpu.make_async_remote_copy(
            src_ref=region, dst_ref=region, send_sem=send_sems.at[a, 3 + j], recv_sem=recv_sems.at[a, 3 + j],
            device_id=(mx, my, 1 - mc), device_id_type=MESH)

    pairs = [(a, j) for a in range(n) for j in range(3)]

    def start(*r):
        for a, j in pairs:
            ici(*r, a, j).start()

    def forward(*r):
        for a, j in pairs:
            ici(*r, a, j, landed=True).wait_recv()
            d2d(*r, a, j).start()

    def drain(*r):
        for a, j in pairs:
            d2d(*r, a, j, landed=True).wait_recv()
        for a, j in pairs:
            ici(*r, a, j).wait_send()
            d2d(*r, a, j).wait_send()

    def finish(*r):
        forward(*r)
        drain(*r)

    out_shapes = [jax.ShapeDtypeStruct(lay.full, BF16) for lay in layouts]
    if forward_half_way:
        return _Carried(fulls, out_shapes, {a: a for a in range(n)}, (n, 6), start, drain, middle=forward)
    return _Carried(fulls, out_shapes, {a: a for a in range(n)}, (n, 6), start, finish)


def _pair_exchange(grads, layouts, name):
    n = len(grads)

    def body(*refs):
        ins, outs = refs[:n], refs[n:2 * n]
        send_sems, recv_sems = refs[2 * n:]
        mx, my, mc = _my_pos()
        copies = []
        for a, lay in enumerate(layouts):
            cp = pltpu.make_async_remote_copy(
                src_ref=lay.halves_of_full(ins[a], 1 - mc), dst_ref=outs[a],
                send_sem=send_sems.at[a], recv_sem=recv_sems.at[a],
                device_id=(mx, my, 1 - mc), device_id_type=MESH)
            cp.start()
            copies.append(cp)
        for cp in copies:
            cp.wait()

    return pl.pallas_call(
        body, name=name,
        out_shape=[jax.ShapeDtypeStruct(lay.halves, g.dtype) for lay, g in zip(layouts, grads)],
        in_specs=[ANY] * n, out_specs=[ANY] * n,
        scratch_shapes=[pltpu.SemaphoreType.DMA((n,)), pltpu.SemaphoreType.DMA((n,))],
    )(*grads)


def _pair_sum(g, theirs, lay, core_arr, name):
    r, c = lay.halves
    tr = _row_tile(r, c * 4)
    nt = r // tr

    def body(core_ref, g_ref, t_ref, o_ref):
        o_ref[...] = (g_ref[...].astype(F32) + t_ref[...].astype(F32)).astype(BF16)

    if lay.by_cols:
        g_map = lambda i, core_ref: (core_ref[0] * nt + i, 0)
    else:
        g_map = lambda i, core_ref: (i, core_ref[0])
    plain = pl.BlockSpec((tr, c), lambda i, core_ref: (i, 0))
    return pl.pallas_call(
        body, name=name,
        grid_spec=pltpu.PrefetchScalarGridSpec(
            num_scalar_prefetch=1, grid=(nt,), in_specs=[pl.BlockSpec((tr, c), g_map), plain], out_specs=plain),
        out_shape=jax.ShapeDtypeStruct((r, c), BF16), compiler_params=_cp(("parallel",)))(core_arr, g, theirs)


def _scatter_comm(pair_sums, layouts):
    n = len(pair_sums)

    def copy(ins, outs, send_sems, recv_sems, a, j):
        mx, my, mc = _my_pos()
        dx, dy = CHIP_DELTAS[j]
        px, py = (mx + dx) % 2, (my + dy) % 2
        return pltpu.make_async_remote_copy(
            src_ref=layouts[a].region_in_halves(ins[a], 2 * px + py), dst_ref=outs[a].at[j],
            send_sem=send_sems.at[a, j], recv_sem=recv_sems.at[a, j], device_id=(px, py, mc), device_id_type=MESH)

    pairs = [(a, j) for a in range(n) for j in range(3)]

    def start(*r):
        for a, j in pairs:
            copy(*r, a, j).start()

    def finish(*r):
        for a, j in pairs:
            copy(*r, a, j).wait()

    return _Carried(pair_sums, [jax.ShapeDtypeStruct((3,) + lay.half, BF16) for lay in layouts], {}, (n, 3),
                    start, finish)


def _chip_sum(pair_sum, slots, lay, chip_arr, core_arr, name):
    r, c = lay.half
    tr = _row_tile(r, c * 4)
    nt = r // tr

    def body(chip_ref, core_ref, s_ref, slot_ref, o_ref):
        acc = s_ref[...].astype(F32)
        for j in range(3):
            acc = acc + slot_ref[j].astype(F32)
        o_ref[...] = acc

    if lay.by_cols:
        s_map = lambda i, chip_ref, core_ref: (i, chip_ref[0])
        o_map = lambda i, chip_ref, core_ref: (core_ref[0] * nt + i, 0)
    else:
        s_map = lambda i, chip_ref, core_ref: (chip_ref[0] * nt + i, 0)
        o_map = lambda i, chip_ref, core_ref: (i, core_ref[0])
    return pl.pallas_call(
        body, name=name,
        grid_spec=pltpu.PrefetchScalarGridSpec(
            num_scalar_prefetch=2, grid=(nt,),
            in_specs=[pl.BlockSpec((tr, c), s_map),
                      pl.BlockSpec((3, tr, c), lambda i, chip_ref, core_ref: (0, i, 0))],
            out_specs=pl.BlockSpec((tr, c), o_map)),
        out_shape=jax.ShapeDtypeStruct(lay.shard, F32), compiler_params=_cp(("parallel",)))(
            chip_arr, core_arr, pair_sum, slots)


def _share_comm(bufs, layouts):
    n = len(bufs)

    def copy(ins, outs, send_sems, recv_sems, a, landed=False):
        mx, my, mc = _my_pos()
        half = 1 - mc if landed else mc
        return pltpu.make_async_remote_copy(
            src_ref=layouts[a].half_of_shard(ins[a], half), dst_ref=layouts[a].half_of_shard(outs[a], half),
            send_sem=send_sems.at[a], recv_sem=recv_sems.at[a], device_id=(mx, my, 1 - mc), device_id_type=MESH)

    def start(*r):
        for a in range(n):
            copy(*r, a).start()

    def finish(*r):
        for a in range(n):
            copy(*r, a, landed=True).wait_recv()
        for a in range(n):
            copy(*r, a).wait_send()

    return _Carried(bufs, [jax.ShapeDtypeStruct(lay.shard, F32) for lay in layouts], {a: a for a in range(n)},
                    (n,), start, finish)


def _allgather_comm(x):
    deltas = [(dx, dy, dc) for dx in (0, 1) for dy in (0, 1) for dc in (0, 1) if (dx, dy, dc) != (0, 0, 0)]
    local = len(deltas)

    def remote(ins, outs, send_sems, recv_sems, k, landed=False):
        mx, my, mc = _my_pos()
        dx, dy, dc = deltas[k]
        px, py, pc = (mx + dx) % 2, (my + dy) % 2, (mc + dc) % 2
        slot = 4 * px + 2 * py + pc if landed else 4 * mx + 2 * my + mc
        return pltpu.make_async_remote_copy(
            src_ref=ins[0], dst_ref=outs[0].at[slot], send_sem=send_sems.at[k], recv_sem=recv_sems.at[k],
            device_id=(px, py, pc), device_id_type=MESH)

    def mine(ins, outs, send_sems, recv_sems):
        mx, my, mc = _my_pos()
        return pltpu.make_async_copy(ins[0], outs[0].at[4 * mx + 2 * my + mc], send_sems.at[local])

    def start(*r):
        mine(*r).start()
        for k in range(len(deltas)):
            remote(*r, k).start()

    def finish(*r):
        for k in range(len(deltas)):
            remote(*r, k, landed=True).wait_recv()
        for k in range(len(deltas)):
            remote(*r, k).wait_send()
        mine(*r).wait()

    return _Carried([x], [jax.ShapeDtypeStruct((N_DEV,) + x.shape, x.dtype)], {}, (N_DEV,), start, finish)


def _sum_slots(x, name):
    s, r, c = x.shape
    tr = _pick(r, (256, 128, 64, 32, 16, 8))

    def body(x_ref, o_ref):
        acc = x_ref[0]
        for k in range(1, s):
            acc = acc + x_ref[k]
        o_ref[...] = acc

    return pl.pallas_call(body, name=name, grid=(r // tr,),
                          in_specs=[pl.BlockSpec((s, tr, c), lambda i: (0, i, 0))],
                          out_specs=pl.BlockSpec((tr, c), lambda i: (i, 0)),
                          out_shape=jax.ShapeDtypeStruct((r, c), F32), compiler_params=_cp(("parallel",)))(x)


def _adamw(w, g, m, v, name, copy_grad=False):
    r, c = w.shape
    tr = _pick(r, (256, 128, 64, 32, 16, 8))
    bc1 = 1.0 - ADAM_B1 ** ADAM_STEP
    bc2 = 1.0 - ADAM_B2 ** ADAM_STEP
    n_out = 4 if copy_grad else 3

    def body(w_ref, g_ref, m_ref, v_ref, d_ref, nm_ref, nv_ref, *g_copy):
        gv = g_ref[...]
        nm = ADAM_B1 * m_ref[...] + (1.0 - ADAM_B1) * gv
        nv = ADAM_B2 * v_ref[...] + (1.0 - ADAM_B2) * (gv * gv)
        d_ref[...] = -ADAM_LR * ((nm / bc1) / (jnp.sqrt(nv / bc2) + ADAM_EPS) + ADAM_WD * w_ref[...])
        nm_ref[...] = nm
        nv_ref[...] = nv
        if copy_grad:
            g_copy[0][...] = gv

    spec = pl.BlockSpec((tr, c), lambda i: (i, 0))
    shp = jax.ShapeDtypeStruct((r, c), F32)
    return _call(body, name=name, grid=(r // tr,), in_specs=[spec] * 4, out_specs=[spec] * n_out,
                 out_shape=[shp] * n_out, args=(w, g, m, v), sem=("parallel",))


def _cctx_grad(parts, c_ctx2d):
    def body(p_ref, c_ref, o_ref):
        tot = ((p_ref[0] + p_ref[2]) + p_ref[4]) + p_ref[6]
        o_ref[...] = tot * _dsilu(c_ref[...])

    return pl.pallas_call(body, name="cctx_grad", in_specs=[VMEM_SPEC, VMEM_SPEC], out_specs=VMEM_SPEC,
                          out_shape=jax.ShapeDtypeStruct(c_ctx2d.shape, F32))(parts, c_ctx2d)


def _mod_fwd(c_rows_t, ada_w, ada_b_shard):
    nl, d, w = ada_w.shape
    td = _pick(d, (256, 128))
    nd = d // td

    def body(ct_ref, w_ref, b_ref, o_ref):
        i = pl.program_id(1)

        @pl.when(i == 0)
        def _():
            o_ref[0] = jnp.broadcast_to(b_ref[0], (MOD_ROWS, w))

        st = _silu(ct_ref[...])
        wv = w_ref[0]
        used = N_DEV + 1
        rows = [jnp.sum(st[:, r:r + 1] * wv, axis=0, keepdims=True) for r in range(used)]
        rows.append(jnp.zeros((MOD_ROWS - used, w), F32))
        o_ref[0] += jnp.concatenate(rows, axis=0)

    return pl.pallas_call(
        body, name="mod_fwd", grid=(nl, nd),
        in_specs=[pl.BlockSpec((td, MOD_ROWS), lambda l, i: (i, 0)),
                  pl.BlockSpec((1, td, w), lambda l, i: (l, i, 0)),
                  pl.BlockSpec((1, 1, w), lambda l, i: (l, 0, 0))],
        out_specs=pl.BlockSpec((1, MOD_ROWS, w), lambda l, i: (l, 0, 0)),
        out_shape=jax.ShapeDtypeStruct((nl, MOD_ROWS, w), F32),
        compiler_params=_cp(("parallel", "arbitrary")),
    )(c_rows_t, ada_w, ada_b_shard)


def _mod_bwd(c_rows_t, dmod, ada_w):
    nl, d, w = ada_w.shape
    td = _pick(d, (256, 128))
    ctx_row = N_DEV

    def body(ct_ref, dm_ref, w_ref, gw_ref, ds_ref):
        st = _silu(ct_ref[...])
        dm = dm_ref[0]
        acc = st[:, 0:1] * dm[0:1, :]
        for r in range(1, ctx_row + 1):
            acc = acc + st[:, r:r + 1] * dm[r:r + 1, :]
        gw_ref[0] = acc
        ds_ref[0] = jnp.sum(w_ref[0] * dm[ctx_row:ctx_row + 1, :], axis=1, keepdims=True)

    return pl.pallas_call(
        body, name="mod_bwd", grid=(nl, d // td),
        in_specs=[pl.BlockSpec((td, MOD_ROWS), lambda l, i: (i, 0)),
                  pl.BlockSpec((1, MOD_ROWS, w), lambda l, i: (l, 0, 0)),
                  pl.BlockSpec((1, td, w), lambda l, i: (l, i, 0))],
        out_specs=[pl.BlockSpec((1, td, w), lambda l, i: (l, i, 0)),
                   pl.BlockSpec((1, td, 1), lambda l, i: (l, i, 0))],
        out_shape=[jax.ShapeDtypeStruct((nl, d, w), F32), jax.ShapeDtypeStruct((nl, d, 1), F32)],
        compiler_params=_cp(("parallel", "parallel")),
    )(c_rows_t, dmod, ada_w)


def _adaln_fwd(x, g, shift, scale, tr, name, comm=None, out_rows=None, into=None, row0=0):
    r, d = x.shape
    rb0 = row0 // tr
    total = into.shape[0] if into is not None else (out_rows or r)

    def body(x_ref, g_ref, sh_ref, sc_ref, *rest):
        o_ref = rest[-1]
        xv = x_ref[...]
        rs = lax.rsqrt(jnp.mean(xv * xv, axis=-1, keepdims=True) + EPS)
        o_ref[...] = ((xv * rs * g_ref[...]) * (1.0 + sc_ref[...]) + sh_ref[...]).astype(BF16)

    spec = pl.BlockSpec((tr, d), lambda i: (i, 0))
    in_specs, args, aliases = [spec, _vec(d), _vec(d), _vec(d)], [x, g, shift, scale], None
    if into is not None:
        in_specs.append(ANY)
        args.append(into)
        aliases = {4: 0}
    outs = _call(body, name=name, grid=(r // tr,), in_specs=in_specs,
                 out_specs=[pl.BlockSpec((tr, d), lambda i: (rb0 + i, 0))],
                 out_shape=[jax.ShapeDtypeStruct((total, d), BF16)], args=args, sem=("parallel",),
                 comm=comm, aliases=aliases)
    return outs if comm else outs[0]


def _adaln_bwd(xin, dh, row0, g, scale, dg_init, tr, name, dres=None, o_prev=None, gate_prev=None, comm=None):
    r, d = xin.shape
    assert row0 % tr == 0
    rb0 = row0 // tr
    want_dx = dres is not None
    want_prev = o_prev is not None
    assert want_dx or not want_prev

    def body(*refs):
        it = iter(refs)
        x_ref, dh_ref, g_ref, sc_ref, dgi_ref = next(it), next(it), next(it), next(it), next(it)
        dres_ref = next(it) if want_dx else None
        o_ref, gp_ref = (next(it), next(it)) if want_prev else (None, None)
        dx_ref = next(it) if want_dx else None
        do_ref = next(it) if want_prev else None
        dsh_ref, dsc_ref, dg_ref = next(it), next(it), next(it)
        dgp_ref = next(it) if want_prev else None
        i = pl.program_id(0)

        @pl.when(i == 0)
        def _():
            dsh_ref[...] = jnp.zeros_like(dsh_ref)
            dsc_ref[...] = jnp.zeros_like(dsc_ref)
            dg_ref[...] = dgi_ref[...]
            if want_prev:
                dgp_ref[...] = jnp.zeros_like(dgp_ref)

        xv = x_ref[...]
        dhv = dh_ref[...].astype(F32)
        gv = g_ref[...]
        rs = lax.rsqrt(jnp.mean(xv * xv, axis=-1, keepdims=True) + EPS)
        xn = xv * rs
        dsh_ref[...] += jnp.sum(dhv, axis=0, keepdims=True)
        dsc_ref[...] += jnp.sum(dhv * (xn * gv), axis=0, keepdims=True)
        dr = dhv * (1.0 + sc_ref[...])
        dg_ref[...] += jnp.sum(dr * xn, axis=0, keepdims=True)
        if want_dx:
            gy = dr * gv
            dx = dres_ref[...] + rs * (gy - xn * jnp.mean(gy * xn, axis=-1, keepdims=True))
            dx_ref[...] = dx
            if want_prev:
                do_ref[...] = (gp_ref[...] * dx).astype(BF16)
                dgp_ref[...] += jnp.sum(dx * o_ref[...].astype(F32), axis=0, keepdims=True)

    row = pl.BlockSpec((tr, d), lambda i: (i, 0))
    in_specs = [row, pl.BlockSpec((tr, d), lambda i: (rb0 + i, 0)), _vec(d), _vec(d), _vec(d)]
    args = [xin, dh, g, scale, dg_init]
    out_specs, out_shape, names = [], [], []
    if want_dx:
        in_specs.append(row)
        args.append(dres)
    if want_prev:
        in_specs += [row, _vec(d)]
        args += [o_prev, gate_prev]
    if want_dx:
        out_specs.append(row)
        out_shape.append(jax.ShapeDtypeStruct((r, d), F32))
        names.append("dx")
    if want_prev:
        out_specs.append(row)
        out_shape.append(jax.ShapeDtypeStruct((r, d), BF16))
        names.append("do_prev")
    for nm in ("dshift", "dscale", "dg") + (("dgate_prev",) if want_prev else ()):
        out_specs.append(_vec(d))
        out_shape.append(jax.ShapeDtypeStruct((1, d), F32))
        names.append(nm)
    outs = _call(body, name=name, grid=(r // tr,), in_specs=in_specs, out_specs=out_specs, out_shape=out_shape,
                 args=args, sem=("arbitrary",), comm=comm)
    res = dict(zip(names, outs))
    if comm:
        res["carried"] = outs[len(names):]
    return res


def _mm(a, b, *, name, tm, tn, tk, ta=False, tb=False, out_dtype=F32, res=None, gate=None, comm=None):
    if ta:
        kd, m = a.shape
    else:
        m, kd = a.shape
    if tb:
        n, kd2 = b.shape
    else:
        kd2, n = b.shape
    assert kd == kd2 and m % tm == 0 and n % tn == 0 and kd % tk == 0, (a.shape, b.shape, tm, tn, tk)
    ni, nj, nk = m // tm, n // tn, kd // tk
    dn = (((0 if ta else 1,), (1 if tb else 0,)), ((), ()))
    with_res = res is not None
    n_in = 4 if with_res else 2
    n_out = 2 if with_res else 1
    n_cin = len(comm.ins) if comm else 0
    n_cout = len(comm.out_shapes) if comm else 0

    def body(*refs):
        in_refs = refs[:n_in + n_cin]
        out_refs = refs[n_in + n_cin:n_in + n_cin + n_out + n_cout]
        scratch = refs[n_in + n_cin + n_out + n_cout:]
        a_ref, b_ref = in_refs[0], in_refs[1]
        o_ref = out_refs[0]
        i, j, k = pl.program_id(0), pl.program_id(1), pl.program_id(2)
        if comm:
            carried = comm.split(in_refs, out_refs, scratch)

            @pl.when(jnp.logical_and(jnp.logical_and(i == 0, j == 0), k == 0))
            def _():
                comm.start(*carried)

            if comm.middle is not None:
                @pl.when(jnp.logical_and(jnp.logical_and(i == ni // 2, j == 0), k == 0))
                def _():
                    comm.middle(*carried)

        def emit(acc):
            o_ref[...] = acc.astype(o_ref.dtype)
            if with_res:
                out_refs[1][...] = in_refs[2][...] + in_refs[3][...] * acc

        if nk == 1:
            emit(lax.dot_general(a_ref[...], b_ref[...], dn, preferred_element_type=F32))
        else:
            acc_ref = scratch[0]

            @pl.when(k == 0)
            def _():
                acc_ref[...] = jnp.zeros_like(acc_ref)

            acc_ref[...] += lax.dot_general(a_ref[...], b_ref[...], dn, preferred_element_type=F32)

            @pl.when(k == nk - 1)
            def _():
                emit(acc_ref[...])

        if comm:
            @pl.when(jnp.logical_and(jnp.logical_and(i == ni - 1, j == nj - 1), k == nk - 1))
            def _():
                comm.finish(*carried)

    a_spec = pl.BlockSpec((tk, tm), lambda i, j, k: (k, i)) if ta else pl.BlockSpec((tm, tk), lambda i, j, k: (i, k))
    b_spec = pl.BlockSpec((tn, tk), lambda i, j, k: (j, k)) if tb else pl.BlockSpec((tk, tn), lambda i, j, k: (k, j))
    o_spec = pl.BlockSpec((tm, tn), lambda i, j, k: (i, j))
    in_specs, args = [a_spec, b_spec], [a, b]
    out_specs, out_shape = [o_spec], [jax.ShapeDtypeStruct((m, n), out_dtype)]
    if with_res:
        in_specs += [o_spec, pl.BlockSpec((1, tn), lambda i, j, k: (0, j))]
        args += [res, gate]
        out_specs.append(o_spec)
        out_shape.append(jax.ShapeDtypeStruct((m, n), F32))
    scratch_shapes = [pltpu.VMEM((tm, tn), F32)] if nk > 1 else []
    aliases = {}
    sem = ("parallel", "parallel", "arbitrary")
    if comm:
        in_specs += [ANY] * n_cin
        args += comm.ins
        out_specs += [ANY] * n_cout
        out_shape += comm.out_shapes
        scratch_shapes += comm.scratch()
        aliases = {n_in + s: n_out + d for s, d in comm.aliases.items()}
        sem = ("arbitrary", "arbitrary", "arbitrary")
    outs = pl.pallas_call(body, name=name, grid=(ni, nj, nk), in_specs=in_specs, out_specs=out_specs,
                          out_shape=out_shape, scratch_shapes=scratch_shapes, input_output_aliases=aliases,
                          compiler_params=_cp(sem))(*args)
    return outs if (with_res or comm) else outs[0]


def _swap_pairs(x):
    lane = lax.broadcasted_iota(jnp.int32, x.shape, 1)
    return jnp.where(lane % 2 == 0, pltpu.roll(x, HEAD_DIM - 1, 1), pltpu.roll(x, 1, 1))


def _qk_prep(p0, cos_i, sin_s, q_norm, k_norm, dims, tr):
    rows = p0.shape[0]
    kvw, aw, cb = dims["kv_w"], dims["attn_w"], dims["cb"]
    nkv, nq = kvw // HEAD_DIM, aw // HEAD_DIM
    scale = HEAD_DIM ** -0.5 * LOG2E
    n_kv_specs, n_q_specs = (2 * kvw) // cb, aw // cb

    def body(*refs):
        kv_refs = refs[:n_kv_specs]
        q_refs = refs[n_kv_specs:n_kv_specs + n_q_specs]
        cos_ref, sin_ref, qn_ref, kn_ref, qo_ref, ko_ref, vo_ref = refs[n_kv_specs + n_q_specs:]
        kv = _cat(kv_refs)
        qv = _cat(q_refs)
        cs, sn = cos_ref[...], sin_ref[...]

        def norm_rope(xh, gvec):
            rs = lax.rsqrt(jnp.mean(xh * xh, axis=-1, keepdims=True) + EPS)
            xn = xh * rs * gvec
            return xn * cs + _swap_pairs(xn) * sn

        for h in range(nkv):
            sl = slice(h * HEAD_DIM, (h + 1) * HEAD_DIM)
            ko_ref[:, sl] = norm_rope(kv[:, sl], kn_ref[...]).astype(BF16)
        vo_ref[...] = kv[:, kvw:].astype(BF16)
        for h in range(nq):
            sl = slice(h * HEAD_DIM, (h + 1) * HEAD_DIM)
            qo_ref[:, sl] = (norm_rope(qv[:, sl], qn_ref[...]) * scale).astype(BF16)

    rm = lambda i: i
    in_specs = (_col_specs(tr, 0, 2 * kvw, cb, rm) + _col_specs(tr, 2 * kvw, aw, cb, rm)
                + [pl.BlockSpec((tr, HEAD_DIM), lambda i: (i, 0))] * 2 + [_vec(HEAD_DIM)] * 2)
    args = [p0] * (n_kv_specs + n_q_specs) + [cos_i, sin_s, q_norm, k_norm]
    return pl.pallas_call(
        body, name="qk_prep", grid=(rows // tr,), in_specs=in_specs,
        out_specs=[pl.BlockSpec((tr, aw), lambda i: (i, 0)), pl.BlockSpec((tr, kvw), lambda i: (i, 0)),
                   pl.BlockSpec((tr, kvw), lambda i: (i, 0))],
        out_shape=[jax.ShapeDtypeStruct((rows, aw), BF16), jax.ShapeDtypeStruct((rows, kvw), BF16),
                   jax.ShapeDtypeStruct((rows, kvw), BF16)],
        compiler_params=_cp(("parallel",)))(*args)


def _qk_prep_bwd(dq_hat, dk_hat, dv, p0, cos_i, sin_s, q_norm, k_norm, dims, tr, n_lat):
    rows = p0.shape[0]
    kvw, aw, cb = dims["kv_w"], dims["attn_w"], dims["cb"]
    nkv, nq = kvw // HEAD_DIM, aw // HEAD_DIM
    scale = HEAD_DIM ** -0.5
    n_kv_specs, n_q_specs = (2 * kvw) // cb, aw // cb
    lat_tiles = n_lat // tr

    def body(*refs):
        kv_refs = refs[:n_kv_specs]
        q_refs = refs[n_kv_specs:n_kv_specs + n_q_specs]
        (dq_ref, dk_ref, dv_ref, cos_ref, sin_ref, qn_ref, kn_ref,
         out_ref, dqn_ref, dkn_ref) = refs[n_kv_specs + n_q_specs:]
        i = pl.program_id(0)

        @pl.when(i == 0)
        def _():
            dqn_ref[...] = jnp.zeros_like(dqn_ref)
            dkn_ref[...] = jnp.zeros_like(dkn_ref)

        kv = _cat(kv_refs)
        qv = _cat(q_refs)
        cs, sn = cos_ref[...], sin_ref[...]
        is_lat = (i < lat_tiles).astype(F32)

        def head_bwd(xh, dhat, gvec):
            dn = dhat * cs + _swap_pairs(dhat * sn)
            rs = lax.rsqrt(jnp.mean(xh * xh, axis=-1, keepdims=True) + EPS)
            xn = xh * rs
            gy = dn * gvec
            dx = rs * (gy - xn * jnp.mean(gy * xn, axis=-1, keepdims=True))
            return dx, jnp.sum(dn * xn, axis=0, keepdims=True)

        dkn = jnp.zeros((1, HEAD_DIM), F32)
        for h in range(nkv):
            sl = slice(h * HEAD_DIM, (h + 1) * HEAD_DIM)
            dx, dgv = head_bwd(kv[:, sl], dk_ref[:, sl], kn_ref[...])
            out_ref[:, sl] = dx.astype(BF16)
            dkn = dkn + dgv
        dkn_ref[...] += dkn
        out_ref[:, kvw:2 * kvw] = dv_ref[...].astype(BF16)
        dqn = jnp.zeros((1, HEAD_DIM), F32)
        for h in range(nq):
            sl = slice(h * HEAD_DIM, (h + 1) * HEAD_DIM)
            dx, dgv = head_bwd(qv[:, sl], dq_ref[:, sl] * (scale * is_lat), qn_ref[...])
            out_ref[:, 2 * kvw + h * HEAD_DIM:2 * kvw + (h + 1) * HEAD_DIM] = dx.astype(BF16)
            dqn = dqn + dgv
        dqn_ref[...] += dqn

    rm = lambda i: i
    wout = 2 * kvw + aw
    in_specs = (_col_specs(tr, 0, 2 * kvw, cb, rm) + _col_specs(tr, 2 * kvw, aw, cb, rm)
                + [pl.BlockSpec((tr, aw), lambda i: (jnp.minimum(i, lat_tiles - 1), 0)),
                   pl.BlockSpec((tr, kvw), lambda i: (i, 0)), pl.BlockSpec((tr, kvw), lambda i: (i, 0)),
                   pl.BlockSpec((tr, HEAD_DIM), lambda i: (i, 0)), pl.BlockSpec((tr, HEAD_DIM), lambda i: (i, 0)),
                   _vec(HEAD_DIM), _vec(HEAD_DIM)])
    args = [p0] * (n_kv_specs + n_q_specs) + [dq_hat, dk_hat, dv, cos_i, sin_s, q_norm, k_norm]
    return pl.pallas_call(
        body, name="qk_prep_bwd", grid=(rows // tr,), in_specs=in_specs,
        out_specs=[pl.BlockSpec((tr, wout), lambda i: (i, 0)), _vec(HEAD_DIM), _vec(HEAD_DIM)],
        out_shape=[jax.ShapeDtypeStruct((rows, wout), BF16), jax.ShapeDtypeStruct((1, HEAD_DIM), F32),
                   jax.ShapeDtypeStruct((1, HEAD_DIM), F32)],
        compiler_params=_cp(("arbitrary",)))(*args)


def _stack_heads(x, g):
    return jnp.concatenate([x[:, h * HEAD_DIM:(h + 1) * HEAD_DIM] for h in range(g)], axis=0)


def _flash_fwd(q_hat, k_all, v_all, n_lat, dims, tq, tk, comm=None):
    kvw, aw = dims["kv_w"], dims["attn_w"]
    nkv = kvw // HEAD_DIM
    g = aw // kvw
    gw = g * HEAD_DIM
    n_keys = k_all.shape[0]
    ni, nj = n_lat // tq, n_keys // tk
    dn_nt = (((1,), (1,)), ((), ()))

    def body(q_ref, k_ref, v_ref, o_ref, lse_ref):
        qs = _stack_heads(q_ref[...], g)
        m = jnp.full((g * tq, 1), -1e30, F32)
        l = jnp.zeros((g * tq, 1), F32)
        acc = jnp.zeros((g * tq, HEAD_DIM), F32)
        for j in range(nj):
            kb = k_ref[pl.ds(j * tk, tk), :]
            vb = v_ref[pl.ds(j * tk, tk), :]
            s = lax.dot_general(qs, kb, dn_nt, preferred_element_type=F32)
            m_new = jnp.maximum(m, jnp.max(s, axis=-1, keepdims=True))
            alpha = jnp.exp2(m - m_new)
            p = jnp.exp2(s - m_new)
            l = alpha * l + jnp.sum(p, axis=-1, keepdims=True)
            acc = alpha * acc + jnp.dot(p.astype(BF16), vb, preferred_element_type=F32)
            m = m_new
        o = acc / l
        for h in range(g):
            o_ref[:, h * HEAD_DIM:(h + 1) * HEAD_DIM] = o[h * tq:(h + 1) * tq]
        lse_ref[...] = m + jnp.log2(l)

    return _call(
        body, name="flash_fwd", grid=(nkv, ni),
        in_specs=[pl.BlockSpec((tq, gw), lambda h, i: (i, h)),
                  pl.BlockSpec((n_keys, HEAD_DIM), lambda h, i: (0, h)),
                  pl.BlockSpec((n_keys, HEAD_DIM), lambda h, i: (0, h))],
        out_specs=[pl.BlockSpec((tq, gw), lambda h, i: (i, h)),
                   pl.BlockSpec((g * tq, 1), lambda h, i: (h * ni + i, 0))],
        out_shape=[jax.ShapeDtypeStruct((n_lat, aw), F32), jax.ShapeDtypeStruct((nkv * ni * g * tq, 1), F32)],
        args=(q_hat, k_all, v_all), sem=("parallel", "parallel"), comm=comm)


def _flash_bwd(q_hat, k_all, v_all, do, o, lse, n_lat, dims, tq, tk):
    kvw, aw = dims["kv_w"], dims["attn_w"]
    nkv = kvw // HEAD_DIM
    g = aw // kvw
    gw = g * HEAD_DIM
    n_keys = k_all.shape[0]
    ni, nj = n_lat // tq, n_keys // tk
    dn_nt = (((1,), (1,)), ((), ()))
    dn_tn = (((0,), (0,)), ((), ()))

    def body(q_ref, k_ref, v_ref, do_ref, o_ref, lse_ref, dq_ref, dk_ref, dv_ref):
        i = pl.program_id(1)

        @pl.when(i == 0)
        def _():
            dk_ref[...] = jnp.zeros_like(dk_ref)
            dv_ref[...] = jnp.zeros_like(dv_ref)

        dos = _stack_heads(do_ref[...], g)
        qs = _stack_heads(q_ref[...], g)
        delta = jnp.sum(dos.astype(F32) * _stack_heads(o_ref[...], g), axis=-1, keepdims=True)
        lse_v = lse_ref[...]
        dq = jnp.zeros((g * tq, HEAD_DIM), F32)
        for j in range(nj):
            rows = pl.ds(j * tk, tk)
            kb, vb = k_ref[rows, :], v_ref[rows, :]
            s = lax.dot_general(qs, kb, dn_nt, preferred_element_type=F32)
            p = jnp.exp2(s - lse_v)
            dp = lax.dot_general(dos, vb, dn_nt, preferred_element_type=F32)
            ds = (p * (dp - delta)).astype(BF16)
            dv_ref[rows, :] += lax.dot_general(p.astype(BF16), dos, dn_tn, preferred_element_type=F32)
            dk_ref[rows, :] += lax.dot_general(ds, qs, dn_tn, preferred_element_type=F32) * (1.0 / LOG2E)
            dq = dq + jnp.dot(ds, kb, preferred_element_type=F32)
        for h in range(g):
            dq_ref[:, h * HEAD_DIM:(h + 1) * HEAD_DIM] = dq[h * tq:(h + 1) * tq]

    qspec = pl.BlockSpec((tq, gw), lambda h, i: (i, h))
    full_k = pl.BlockSpec((n_keys, HEAD_DIM), lambda h, i: (0, h))
    return pl.pallas_call(
        body, name="flash_bwd", grid=(nkv, ni),
        in_specs=[qspec, full_k, full_k, qspec, qspec, pl.BlockSpec((g * tq, 1), lambda h, i: (h * ni + i, 0))],
        out_specs=[qspec, full_k, full_k],
        out_shape=[jax.ShapeDtypeStruct((n_lat, aw), F32), jax.ShapeDtypeStruct((n_keys, kvw), F32),
                   jax.ShapeDtypeStruct((n_keys, kvw), F32)],
        compiler_params=_cp(("parallel", "arbitrary")))(q_hat, k_all, v_all, do, o, lse)


def _shifted_copies(pad_ref, sh_ref, tr):
    rows = tr + 2 * CONV_HALO - SUBLANES
    for r in range(SUBLANES):
        sh_ref[r] = pad_ref[pl.ds(r, rows), :]


def _stencil(sh_ref, w_ref, out_ref, offsets, tr, cc, init_ref=None):
    for c0 in range(0, cc, LANES):
        lanes = pl.ds(c0, LANES)
        wv = [jnp.broadcast_to(w_ref[pl.ds(k, 1), lanes], (STENCIL_ROWS, LANES)) for k in range(len(offsets))]
        if init_ref is None:
            init = jnp.zeros((STENCIL_ROWS, LANES), F32)
        else:
            init = jnp.broadcast_to(init_ref[:, lanes], (STENCIL_ROWS, LANES))

        def block(rb, carry, lanes=lanes, wv=wv, init=init):
            r0 = rb * STENCIL_ROWS
            parts = [init] + [None] * (STENCIL_CHAINS - 1)
            for k, o in enumerate(offsets):
                rows = pl.ds(SUBLANES * (o // SUBLANES) + r0, STENCIL_ROWS)
                term = sh_ref[o % SUBLANES, rows, lanes] * wv[k]
                q = k % STENCIL_CHAINS
                parts[q] = term if parts[q] is None else parts[q] + term
            acc = parts[0]
            for p_ in parts[1:]:
                acc = acc + p_
            out_ref[pl.ds(r0, STENCIL_ROWS), lanes] = acc
            return carry

        for rb in range(tr // STENCIL_ROWS):
            block(rb, 0)


def _stencil_weight_grad(sh_ref, d_ref, d_row0, dw_ref, offsets, tr, cc):
    rows_per = REDUCE_ROWS
    for c0 in range(0, cc, LANES):
        lanes = pl.ds(c0, LANES)

        def block(rb, accs, lanes=lanes):
            r0 = rb * rows_per
            dblk = d_ref[pl.ds(d_row0 + r0, rows_per), lanes]
            out = []
            for k, o in enumerate(offsets):
                prod = dblk * sh_ref[o % SUBLANES, pl.ds(SUBLANES * (o // SUBLANES) + r0, rows_per), lanes]
                part = prod[0:SUBLANES]
                for q in range(1, rows_per // SUBLANES):
                    part = part + prod[q * SUBLANES:(q + 1) * SUBLANES]
                out.append(accs[k] + part)
            return tuple(out)

        zero = jnp.zeros((SUBLANES, LANES), F32)
        accs = tuple(zero for _ in offsets)
        for rb in range(tr // rows_per):
            accs = block(rb, accs)
        for k in range(len(offsets)):
            dw_ref[pl.ds(k, 1), lanes] += jnp.sum(accs[k], axis=0, keepdims=True)


def _halo_maps(tr, n_tiles):
    per = tr // CONV_HALO
    prev = lambda i: jnp.maximum(i * per - 1, 0)
    nxt = lambda i: (i + 1) * per
    return prev, nxt


def _mix_fwd(attn, p0, dw_w, dw_b, ln_g, ln_b, dims, tr):
    n_lat, aw = attn.shape
    cc, cb, cw = dims["conv_ch"], dims["cb"], dims["conv_w"]
    off = dims["off"]
    n_tiles = n_lat // tr
    pad = cw // 2
    na, nc = aw // cb, cc // cb
    prev_map, next_map = _halo_maps(tr, n_tiles)

    def body(*refs):
        it = iter(refs)
        attn_ref = next(it)
        za = [next(it) for _ in range(na)]
        a_c = [next(it) for _ in range(nc)]
        b_c = [next(it) for _ in range(nc)]
        zb = [next(it) for _ in range(nc)]
        a_p = [next(it) for _ in range(nc)]
        b_p = [next(it) for _ in range(nc)]
        a_n = [next(it) for _ in range(nc)]
        b_n = [next(it) for _ in range(nc)]
        w_ref, db_ref, g_ref, bb_ref, mix_ref, yc_ref, ypad, ysh = (next(it) for _ in range(8))
        i = pl.program_id(0)
        ypad[pl.ds(0, CONV_HALO), :] = _cat(a_p) * _sigmoid(_cat(b_p)) * (i > 0).astype(F32)
        ypad[pl.ds(CONV_HALO, tr), :] = _cat(a_c) * _sigmoid(_cat(b_c))
        ypad[pl.ds(CONV_HALO + tr, CONV_HALO), :] = _cat(a_n) * _sigmoid(_cat(b_n)) * (i < n_tiles - 1).astype(F32)
        _shifted_copies(ypad, ysh, tr)
        _stencil(ysh, w_ref, yc_ref, [CONV_HALO - pad + k for k in range(cw)], tr, cc, init_ref=db_ref)
        acc = yc_ref[...]
        mu = jnp.mean(acc, axis=-1, keepdims=True)
        xc = acc - mu
        rs = lax.rsqrt(jnp.mean(xc * xc, axis=-1, keepdims=True) + EPS)
        nv = xc * rs * g_ref[...] + bb_ref[...]
        mix_ref[:, :aw] = (attn_ref[...] * _silu(_cat(za))).astype(BF16)
        mix_ref[:, aw:] = (_silu(nv) * _silu(_cat(zb))).astype(BF16)

    rm = lambda i: i
    in_specs = ([pl.BlockSpec((tr, aw), lambda i: (i, 0))]
                + _col_specs(tr, off["za"], aw, cb, rm) + _col_specs(tr, off["a"], cc, cb, rm)
                + _col_specs(tr, off["b"], cc, cb, rm) + _col_specs(tr, off["zb"], cc, cb, rm)
                + _col_specs(CONV_HALO, off["a"], cc, cb, prev_map) + _col_specs(CONV_HALO, off["b"], cc, cb, prev_map)
                + _col_specs(CONV_HALO, off["a"], cc, cb, next_map) + _col_specs(CONV_HALO, off["b"], cc, cb, next_map)
                + [pl.BlockSpec(dw_w.shape, lambda i: (0, 0)), _vec(cc), _vec(cc), _vec(cc)])
    args = [attn] + [p0] * (na + 7 * nc) + [dw_w, dw_b, ln_g, ln_b]
    return pl.pallas_call(
        body, name="mix_fwd", grid=(n_tiles,), in_specs=in_specs,
        out_specs=[pl.BlockSpec((tr, aw + cc), lambda i: (i, 0)), pl.BlockSpec((tr, cc), lambda i: (i, 0))],
        out_shape=[jax.ShapeDtypeStruct((n_lat, aw + cc), BF16), jax.ShapeDtypeStruct((n_lat, cc), F32)],
        scratch_shapes=[pltpu.VMEM((tr + 2 * CONV_HALO, cc), F32),
                        pltpu.VMEM((SUBLANES, tr + 2 * CONV_HALO - SUBLANES, cc), F32)],
        compiler_params=_cp(("parallel",)))(*args)


def _mix_bwd_pointwise(dmix, attn, p0, yc, ln_g, ln_b, dims, tr, n_ext):
    n_lat, aw = attn.shape
    cc, cb, off = dims["conv_ch"], dims["cb"], dims["off"]
    na, nc = aw // cb, cc // cb
    lat_tiles = n_lat // tr

    def body(*refs):
        it = iter(refs)
        dmix_ref, attn_ref = next(it), next(it)
        za = [next(it) for _ in range(na)]
        zb = [next(it) for _ in range(nc)]
        yc_ref, g_ref, bb_ref = next(it), next(it), next(it)
        dattn_ref, dza_ref, dzb_ref, dyc_ref, dg_ref, dbb_ref, ddb_ref = (next(it) for _ in range(7))
        i = pl.program_id(0)

        @pl.when(i == 0)
        def _():
            dg_ref[...] = jnp.zeros_like(dg_ref)
            dbb_ref[...] = jnp.zeros_like(dbb_ref)
            ddb_ref[...] = jnp.zeros_like(ddb_ref)

        lat = (i < lat_tiles).astype(F32)
        dm = dmix_ref[...].astype(F32)
        dma, dmb = dm[:, :aw], dm[:, aw:]
        zav, zbv = _cat(za), _cat(zb)
        sa, dsa = _silu_and_grad(zav)
        sb, dsb = _silu_and_grad(zbv)
        dattn_ref[...] = (dma * sa).astype(BF16)
        dza_ref[...] = (dma * attn_ref[...] * dsa * lat).astype(BF16)
        ycv = yc_ref[...]
        mu = jnp.mean(ycv, axis=-1, keepdims=True)
        xc = ycv - mu
        rs = lax.rsqrt(jnp.mean(xc * xc, axis=-1, keepdims=True) + EPS)
        xh = xc * rs
        nv = xh * g_ref[...] + bb_ref[...]
        sn, dsn = _silu_and_grad(nv)
        dzb_ref[...] = (dmb * sn * dsb * lat).astype(BF16)
        dn = dmb * sb * dsn
        dg_ref[...] += lat * jnp.sum(dn * xh, axis=0, keepdims=True)
        dbb_ref[...] += lat * jnp.sum(dn, axis=0, keepdims=True)
        dxh = dn * g_ref[...]
        dyc = rs * (dxh - jnp.mean(dxh, axis=-1, keepdims=True) - xh * jnp.mean(dxh * xh, axis=-1, keepdims=True))
        dyc_ref[...] = dyc
        ddb_ref[...] += lat * jnp.sum(dyc, axis=0, keepdims=True)

    rm = lambda i: jnp.minimum(i, lat_tiles - 1)
    row = lambda w: pl.BlockSpec((tr, w), lambda i: (rm(i), 0))
    ext = lambda w: pl.BlockSpec((tr, w), lambda i: (i, 0))
    in_specs = ([row(aw + cc), row(aw)] + _col_specs(tr, off["za"], aw, cb, rm)
                + _col_specs(tr, off["zb"], cc, cb, rm) + [row(cc), _vec(cc), _vec(cc)])
    args = [dmix, attn] + [p0] * (na + nc) + [yc, ln_g, ln_b]
    return pl.pallas_call(
        body, name="mix_bwd_pointwise", grid=(n_ext // tr,), in_specs=in_specs,
        out_specs=[row(aw), ext(aw), ext(cc), row(cc), _vec(cc), _vec(cc), _vec(cc)],
        out_shape=[jax.ShapeDtypeStruct((n_lat, aw), BF16), jax.ShapeDtypeStruct((n_ext, aw), BF16),
                   jax.ShapeDtypeStruct((n_ext, cc), BF16), jax.ShapeDtypeStruct((n_lat, cc), F32)]
                  + [jax.ShapeDtypeStruct((1, cc), F32)] * 3,
        compiler_params=_cp(("arbitrary",)))(*args)


def _conv_bwd(dyc, p0, dw_w, dims, tr, n_ext):
    n_lat, cc = dyc.shape
    cb, cw, off = dims["cb"], dims["conv_w"], dims["off"]
    nc = cc // cb
    n_tiles = n_lat // tr
    pad = cw // 2
    prev_lat, next_lat = _halo_maps(tr, n_tiles)
    rm = lambda i: jnp.minimum(i, n_tiles - 1)
    prev_map = lambda i: prev_lat(rm(i))
    next_map = lambda i: next_lat(rm(i))

    def body(*refs):
        it = iter(refs)
        a_c = [next(it) for _ in range(nc)]
        b_c = [next(it) for _ in range(nc)]
        a_p = [next(it) for _ in range(nc)]
        b_p = [next(it) for _ in range(nc)]
        a_n = [next(it) for _ in range(nc)]
        b_n = [next(it) for _ in range(nc)]
        d_c, d_p, d_n, w_ref, dab_ref, dw_ref, ypad, dpad, ysh, dsh, dy_scr = (next(it) for _ in range(11))
        i = pl.program_id(0)

        @pl.when(i == 0)
        def _():
            dw_ref[...] = jnp.zeros_like(dw_ref)

        @pl.when(i >= n_tiles)
        def _():
            dab_ref[...] = jnp.zeros_like(dab_ref)

        @pl.when(i < n_tiles)
        def _():
            first, last = (i > 0).astype(F32), (i < n_tiles - 1).astype(F32)
            av, bv = _cat(a_c), _cat(b_c)
            sg = _sigmoid(bv)
            ypad[pl.ds(0, CONV_HALO), :] = _cat(a_p) * _sigmoid(_cat(b_p)) * first
            ypad[pl.ds(CONV_HALO, tr), :] = av * sg
            ypad[pl.ds(CONV_HALO + tr, CONV_HALO), :] = _cat(a_n) * _sigmoid(_cat(b_n)) * last
            dpad[pl.ds(0, CONV_HALO), :] = d_p[...] * first
            dpad[pl.ds(CONV_HALO, tr), :] = d_c[...]
            dpad[pl.ds(CONV_HALO + tr, CONV_HALO), :] = d_n[...] * last
            _shifted_copies(ypad, ysh, tr)
            _shifted_copies(dpad, dsh, tr)
            _stencil(dsh, w_ref, dy_scr, [CONV_HALO + pad - k for k in range(cw)], tr, cc)
            _stencil_weight_grad(ysh, dpad, CONV_HALO, dw_ref, [CONV_HALO - pad + k for k in range(cw)], tr, cc)
            dy = dy_scr[...]
            dab_ref[:, :cc] = (dy * sg).astype(BF16)
            dab_ref[:, cc:] = (dy * av * sg * (1.0 - sg)).astype(BF16)

    in_specs = (_col_specs(tr, off["a"], cc, cb, rm) + _col_specs(tr, off["b"], cc, cb, rm)
                + _col_specs(CONV_HALO, off["a"], cc, cb, prev_map) + _col_specs(CONV_HALO, off["b"], cc, cb, prev_map)
                + _col_specs(CONV_HALO, off["a"], cc, cb, next_map) + _col_specs(CONV_HALO, off["b"], cc, cb, next_map)
                + [pl.BlockSpec((tr, cc), lambda i: (rm(i), 0)),
                   pl.BlockSpec((CONV_HALO, cc), lambda i: (prev_map(i), 0)),
                   pl.BlockSpec((CONV_HALO, cc), lambda i: (jnp.minimum(next_map(i), n_lat // CONV_HALO - 1), 0)),
                   pl.BlockSpec(dw_w.shape, lambda i: (0, 0))])
    args = [p0] * (6 * nc) + [dyc, dyc, dyc, dw_w]
    return pl.pallas_call(
        body, name="conv_bwd", grid=(n_ext // tr,), in_specs=in_specs,
        out_specs=[pl.BlockSpec((tr, 2 * cc), lambda i: (i, 0)), pl.BlockSpec(dw_w.shape, lambda i: (0, 0))],
        out_shape=[jax.ShapeDtypeStruct((n_ext, 2 * cc), BF16), jax.ShapeDtypeStruct(dw_w.shape, F32)],
        scratch_shapes=[pltpu.VMEM((tr + 2 * CONV_HALO, cc), F32), pltpu.VMEM((tr + 2 * CONV_HALO, cc), F32),
                        pltpu.VMEM((SUBLANES, tr + 2 * CONV_HALO - SUBLANES, cc), F32),
                        pltpu.VMEM((SUBLANES, tr + 2 * CONV_HALO - SUBLANES, cc), F32),
                        pltpu.VMEM((tr, cc), F32)],
        compiler_params=_cp(("arbitrary",)))(*args)


def _sgu_parts(u, v, ln_g, ln_b):
    mu = jnp.mean(v, axis=-1, keepdims=True)
    xc = v - mu
    rs = lax.rsqrt(jnp.mean(xc * xc, axis=-1, keepdims=True) + EPS)
    xh = xc * rs
    return u, xh, rs, xh * ln_g + ln_b


def _sgu_fwd(p1, ln_g, ln_b, ws, bs_t, tr):
    n_lat, w3 = p1.shape
    w = w3 // 3
    ng, ch = ws.shape[0], ws.shape[1]
    gwid = w // ng
    n_ch = tr // ch

    def body(pu_ref, pv_ref, pg_ref, g_ref, b_ref, ws_ref, bs_ref, o_ref):
        u, _, _, vln = _sgu_parts(_gelu(pu_ref[...].astype(F32)), _gelu(pv_ref[...].astype(F32)),
                                  g_ref[...], b_ref[...])
        gate = _silu(pg_ref[...].astype(F32))
        vb = vln.astype(BF16)
        for c in range(n_ch):
            rs_ = slice(c * ch, (c + 1) * ch)
            for gi in range(ng):
                cs_ = slice(gi * gwid, (gi + 1) * gwid)
                mixed = jnp.dot(ws_ref[gi], vb[rs_, cs_], preferred_element_type=F32) + bs_ref[:, gi:gi + 1]
                o_ref[rs_, cs_] = (u[rs_, cs_] * mixed * gate[rs_, cs_]).astype(BF16)

    col = lambda t: pl.BlockSpec((tr, w), lambda i, _t=t: (i, _t))
    return pl.pallas_call(
        body, name="sgu_fwd", grid=(n_lat // tr,),
        in_specs=[col(0), col(1), col(2), _vec(w), _vec(w),
                  pl.BlockSpec(ws.shape, lambda i: (0, 0, 0)), pl.BlockSpec(bs_t.shape, lambda i: (0, 0))],
        out_specs=pl.BlockSpec((tr, w), lambda i: (i, 0)),
        out_shape=jax.ShapeDtypeStruct((n_lat, w), BF16),
        compiler_params=_cp(("parallel",)))(p1, p1, p1, ln_g, ln_b, ws, bs_t)


def _sgu_bwd(dm, p1, ln_g, ln_b, ws, ws_t, bs_t, tr):
    n_lat, w3 = p1.shape
    w = w3 // 3
    ng, ch = ws.shape[0], ws.shape[1]
    gwid = w // ng
    n_ch = tr // ch
    dn_nt = (((1,), (1,)), ((), ()))

    def body(dm_ref, pu_ref, pv_ref, pg_ref, g_ref, b_ref, ws_ref, wst_ref, bs_ref,
             dp_ref, dws_ref, dbs_ref, dg_ref, dbb_ref, dvln_scr):
        i = pl.program_id(0)

        @pl.when(i == 0)
        def _():
            dws_ref[...] = jnp.zeros_like(dws_ref)
            dbs_ref[...] = jnp.zeros_like(dbs_ref)
            dg_ref[...] = jnp.zeros_like(dg_ref)
            dbb_ref[...] = jnp.zeros_like(dbb_ref)

        puv, pvv, pgv = pu_ref[...].astype(F32), pv_ref[...].astype(F32), pg_ref[...].astype(F32)
        gu, dgu = _gelu_and_grad(puv)
        gv, dgv = _gelu_and_grad(pvv)
        gate, dgate = _silu_and_grad(pgv)
        u, xh, rs, vln = _sgu_parts(gu, gv, g_ref[...], b_ref[...])
        dmv = dm_ref[...].astype(F32)
        vb = vln.astype(BF16)
        dmu = dmv * u
        du_pre = dmv * gate * dgu
        dg_pre = dmu * dgate
        dmix_all = dmu * gate
        dbs_cols = [jnp.zeros((ch, 1), F32) for _ in range(ng)]
        for c in range(n_ch):
            rs_ = slice(c * ch, (c + 1) * ch)
            for gi in range(ng):
                cs_ = slice(gi * gwid, (gi + 1) * gwid)
                mixed = jnp.dot(ws_ref[gi], vb[rs_, cs_], preferred_element_type=F32) + bs_ref[:, gi:gi + 1]
                dmixed = dmix_all[rs_, cs_]
                dmb = dmixed.astype(BF16)
                dp_ref[rs_, gi * gwid:(gi + 1) * gwid] = (du_pre[rs_, cs_] * mixed).astype(BF16)
                dp_ref[rs_, 2 * w + gi * gwid:2 * w + (gi + 1) * gwid] = (dg_pre[rs_, cs_] * mixed).astype(BF16)
                dvln_scr[rs_, cs_] = jnp.dot(wst_ref[gi], dmb, preferred_element_type=F32)
                dws_ref[gi] += lax.dot_general(dmb, vb[rs_, cs_], dn_nt, preferred_element_type=F32)
                dbs_cols[gi] = dbs_cols[gi] + jnp.sum(dmixed, axis=-1, keepdims=True)
        dbs_ref[...] += jnp.concatenate(dbs_cols, axis=1)
        dvln = dvln_scr[...]
        dg_ref[...] += jnp.sum(dvln * xh, axis=0, keepdims=True)
        dbb_ref[...] += jnp.sum(dvln, axis=0, keepdims=True)
        dxh = dvln * g_ref[...]
        dv = rs * (dxh - jnp.mean(dxh, axis=-1, keepdims=True) - xh * jnp.mean(dxh * xh, axis=-1, keepdims=True))
        dp_ref[:, w:2 * w] = (dv * dgv).astype(BF16)

    col = lambda t: pl.BlockSpec((tr, w), lambda i, _t=t: (i, _t))
    return pl.pallas_call(
        body, name="sgu_bwd", grid=(n_lat // tr,),
        in_specs=[pl.BlockSpec((tr, w), lambda i: (i, 0)), col(0), col(1), col(2), _vec(w), _vec(w),
                  pl.BlockSpec(ws.shape, lambda i: (0, 0, 0)), pl.BlockSpec(ws.shape, lambda i: (0, 0, 0)),
                  pl.BlockSpec(bs_t.shape, lambda i: (0, 0))],
        out_specs=[pl.BlockSpec((tr, w3), lambda i: (i, 0)), pl.BlockSpec(ws.shape, lambda i: (0, 0, 0)),
                   pl.BlockSpec(bs_t.shape, lambda i: (0, 0)), _vec(w), _vec(w)],
        out_shape=[jax.ShapeDtypeStruct((n_lat, w3), BF16), jax.ShapeDtypeStruct(ws.shape, F32),
                   jax.ShapeDtypeStruct(bs_t.shape, F32), jax.ShapeDtypeStruct((1, w), F32),
                   jax.ShapeDtypeStruct((1, w), F32)],
        scratch_shapes=[pltpu.VMEM((tr, w), F32)],
        compiler_params=_cp(("arbitrary",)))(dm, p1, p1, p1, ln_g, ln_b, ws, ws_t, bs_t)


def _final_loss(x2, target, final_g, o_prev, gate_prev, tr):
    n_lat, d = x2.shape

    def body(x_ref, t_ref, g_ref, o_ref, gp_ref, dx_ref, do_ref, ls_ref, dg_ref, dgp_ref):
        i = pl.program_id(0)

        @pl.when(i == 0)
        def _():
            ls_ref[...] = jnp.zeros_like(ls_ref)
            dg_ref[...] = jnp.zeros_like(dg_ref)
            dgp_ref[...] = jnp.zeros_like(dgp_ref)

        xv = x_ref[...]
        gv = g_ref[...]
        rs = lax.rsqrt(jnp.mean(xv * xv, axis=-1, keepdims=True) + EPS)
        xn = xv * rs
        err = xn * gv - t_ref[...]
        ls_ref[...] += jnp.sum(err * err, axis=0, keepdims=True)
        dy = err * (1.0 / d)
        dg_ref[...] += jnp.sum(dy * xn, axis=0, keepdims=True)
        gy = dy * gv
        dx = rs * (gy - xn * jnp.mean(gy * xn, axis=-1, keepdims=True))
        dx_ref[...] = dx
        do_ref[...] = (gp_ref[...] * dx).astype(BF16)
        dgp_ref[...] += jnp.sum(dx * o_ref[...].astype(F32), axis=0, keepdims=True)

    row = pl.BlockSpec((tr, d), lambda i: (i, 0))
    return pl.pallas_call(
        body, name="final_loss", grid=(n_lat // tr,), in_specs=[row, row, _vec(d), row, _vec(d)],
        out_specs=[row, row, _vec(d), _vec(d), _vec(d)],
        out_shape=[jax.ShapeDtypeStruct((n_lat, d), F32), jax.ShapeDtypeStruct((n_lat, d), BF16)]
                  + [jax.ShapeDtypeStruct((1, d), F32)] * 3,
        compiler_params=_cp(("arbitrary",)))(x2, target, final_g, o_prev, gate_prev)


def _pack(arrays):
    flat = jnp.concatenate([a.reshape(-1).astype(F32) for a in arrays])
    n = flat.shape[0]
    rows = -(-n // LANES)
    rows = -(-rows // PACK_ROWS) * PACK_ROWS
    return jnp.pad(flat, (0, rows * LANES - n)).reshape(rows, LANES)


def _unpack(buf, shapes):
    flat = buf.reshape(buf.shape[:-2] + (-1,))
    out, pos = [], 0
    for shp in shapes:
        n = math.prod(shp)
        out.append(flat[..., pos:pos + n].reshape(buf.shape[:-2] + tuple(shp)))
        pos += n
    return out


def _rope_tables(n_lat, n_ctx):
    rows = n_lat // GRID_W
    row = jnp.repeat(jnp.arange(rows, dtype=F32), GRID_W)
    col = jnp.tile(jnp.arange(GRID_W, dtype=F32), rows)
    n_freq, axis_dim = HEAD_DIM // 4, HEAD_DIM // 2
    inv = jnp.power(ROPE_THETA, jnp.arange(n_freq, dtype=F32) * (-2.0 / axis_dim))
    ang = jnp.concatenate([row[:, None] * inv, col[:, None] * inv], axis=-1)
    cos, sin = jnp.cos(ang), jnp.sin(ang)
    cos_i = jnp.repeat(cos, 2, axis=-1)
    sin_s = jnp.stack([-sin, sin], axis=-1).reshape(n_lat, HEAD_DIM)
    cos_i = jnp.concatenate([cos_i, jnp.ones((n_ctx, HEAD_DIM), F32)], axis=0)
    sin_s = jnp.concatenate([sin_s, jnp.zeros((n_ctx, HEAD_DIM), F32)], axis=0)
    return cos_i, sin_s


def kernel(x, c, ctx, c_ctx, ada_w, ada_b, norm_g, ev_w_in, ev_q_norm, ev_k_norm, ev_dw_w, ev_dw_b, ev_ln_g, ev_ln_b, ev_w_out, od_w_in, od_ln_g, od_ln_b, od_ws, od_bs, od_w_out, final_g, loss_target, m_c_ctx, m_ada_w, m_ada_b, m_norm_g, m_ev_w_in, m_ev_q_norm, m_ev_k_norm, m_ev_dw_w, m_ev_dw_b, m_ev_ln_g, m_ev_ln_b, m_ev_w_out, m_od_w_in, m_od_ln_g, m_od_ln_b, m_od_ws, m_od_bs, m_od_w_out, m_final_g, v_c_ctx, v_ada_w, v_ada_b, v_norm_g, v_ev_w_in, v_ev_q_norm, v_ev_k_norm, v_ev_dw_w, v_ev_dw_b, v_ev_ln_g, v_ev_ln_b, v_ev_w_out, v_od_w_in, v_od_ln_g, v_od_ln_b, v_od_ws, v_od_bs, v_od_w_out, v_final_g):
    weights = dict(c_ctx=c_ctx, ada_w=ada_w, ada_b=ada_b, norm_g=norm_g, ev_w_in=ev_w_in, ev_q_norm=ev_q_norm,
                   ev_k_norm=ev_k_norm, ev_dw_w=ev_dw_w, ev_dw_b=ev_dw_b, ev_ln_g=ev_ln_g, ev_ln_b=ev_ln_b,
                   ev_w_out=ev_w_out, od_w_in=od_w_in, od_ln_g=od_ln_g, od_ln_b=od_ln_b, od_ws=od_ws, od_bs=od_bs,
                   od_w_out=od_w_out, final_g=final_g)
    mom_m = dict(c_ctx=m_c_ctx, ada_w=m_ada_w, ada_b=m_ada_b, norm_g=m_norm_g, ev_w_in=m_ev_w_in,
                 ev_q_norm=m_ev_q_norm, ev_k_norm=m_ev_k_norm, ev_dw_w=m_ev_dw_w, ev_dw_b=m_ev_dw_b,
                 ev_ln_g=m_ev_ln_g, ev_ln_b=m_ev_ln_b, ev_w_out=m_ev_w_out, od_w_in=m_od_w_in, od_ln_g=m_od_ln_g,
                 od_ln_b=m_od_ln_b, od_ws=m_od_ws, od_bs=m_od_bs, od_w_out=m_od_w_out, final_g=m_final_g)
    mom_v = dict(c_ctx=v_c_ctx, ada_w=v_ada_w, ada_b=v_ada_b, norm_g=v_norm_g, ev_w_in=v_ev_w_in,
                 ev_q_norm=v_ev_q_norm, ev_k_norm=v_ev_k_norm, ev_dw_w=v_ev_dw_w, ev_dw_b=v_ev_dw_b,
                 ev_ln_g=v_ev_ln_g, ev_ln_b=v_ev_ln_b, ev_w_out=v_ev_w_out, od_w_in=v_od_w_in, od_ln_g=v_od_ln_g,
                 od_ln_b=v_od_ln_b, od_ws=v_od_ws, od_bs=v_od_bs, od_w_out=v_od_w_out, final_g=v_final_g)
    order = list(weights)

    _, n_lat, d = x.shape
    n_ctx = ctx.shape[1]
    n_ext = n_lat + n_ctx
    ev_in = ev_w_in.shape[-1] * N_CHIP
    ev_mix = ev_w_out.shape[1] * N_CHIP
    conv_ch = ev_dw_b.shape[-1]
    conv_w = ev_dw_w.shape[1]
    attn_w = ev_mix - conv_ch
    kv_w = N_KV_HEADS * HEAD_DIM
    assert ev_in == 2 * kv_w + 2 * attn_w + 3 * conv_ch and conv_w // 2 < CONV_HALO
    sgu_w = od_w_out.shape[1] * N_CHIP
    wa = ada_w.shape[-1]
    cb = math.gcd(2 * kv_w, attn_w, conv_ch)
    off = dict(k=0, v=kv_w, q=2 * kv_w, za=2 * kv_w + attn_w, a=2 * kv_w + 2 * attn_w,
               b=2 * kv_w + 2 * attn_w + conv_ch, zb=2 * kv_w + 2 * attn_w + 2 * conv_ch)
    dims = dict(kv_w=kv_w, attn_w=attn_w, conv_ch=conv_ch, conv_w=conv_w, cb=cb, off=off)
    tr = 256 if (n_lat % 256 == 0 and n_ctx % 256 == 0) else 128

    mx, my, mc = lax.axis_index("x"), lax.axis_index("y"), lax.axis_index("c")
    me = 4 * mx + 2 * my + mc
    chip = 2 * mx + my

    x2d, tgt2d, ctx2d = x[0], loss_target[0], ctx[0]
    ev_dw_w_l = ev_dw_w[0]
    dwc = ev_dw_w_l.shape[1]
    lnc = od_ln_g.shape[1]

    g_c = _allgather_small(jnp.broadcast_to(c, (8, d)), "gather_cond")[:, 0, :]
    c_rows = jnp.concatenate([g_c, c_ctx[None, :], jnp.zeros((MOD_ROWS - N_DEV - 1, d), F32)], axis=0)
    c_rows_t = c_rows.T
    ada_b_shard = lax.dynamic_slice_in_dim(ada_b, chip * wa, wa, axis=1)[:, None, :]
    mod_part = _mod_fwd(c_rows_t, ada_w, ada_b_shard)
    part_shapes = [(2, MOD_ROWS, wa), (conv_w, dwc), (1, lnc), (1, lnc)]
    g_parts = _allgather_small(_pack([mod_part, ev_dw_w_l, od_ln_g, od_ln_b]), "gather_mod")
    per_chip = [_unpack(g_parts[2 * s], part_shapes) for s in range(N_CHIP)]
    mod_all = jnp.concatenate([p[0] for p in per_chip], axis=-1)
    dw_w_full = jnp.concatenate([p[1] for p in per_chip], axis=-1)
    od_ln_g_full = jnp.concatenate([p[2] for p in per_chip], axis=-1)
    od_ln_b_full = jnp.concatenate([p[3] for p in per_chip], axis=-1)
    dw_w_pad = jnp.pad(dw_w_full, ((0, 2 * CONV_HALO - conv_w), (0, 0)))
    mod_me = lax.dynamic_slice_in_dim(mod_all, me, 1, axis=1)
    shift0, scale0, gate0 = mod_me[0, :, :d], mod_me[0, :, d:2 * d], mod_me[0, :, 2 * d:]
    shift1, scale1, gate1 = mod_me[1, :, :d], mod_me[1, :, d:2 * d], mod_me[1, :, 2 * d:]
    shift_c, scale_c = mod_all[0, N_DEV:N_DEV + 1, :d], mod_all[0, N_DEV:N_DEV + 1, d:2 * d]
    g0, g1 = norm_g[0:1], norm_g[1:2]

    lay = dict(ev_w_in=_Sharded((d, ev_in), True), ev_w_out=_Sharded((ev_mix, d), False),
               od_w_in=_Sharded((d, 3 * sgu_w), True), od_w_out=_Sharded((sgu_w, d), False))
    big = list(lay)
    chip_arr = jnp.reshape(chip, (1,)).astype(jnp.int32)
    core_arr = jnp.reshape(mc, (1,)).astype(jnp.int32)
    own = {n: _cast_into_full(weights[n][0], lay[n], chip_arr, f"cast_{n}") for n in big}
    gather_ev_out = _gather_comm([own["ev_w_out"]], [lay["ev_w_out"]], forward_half_way=True)
    layer1 = ["od_w_in", "od_w_out"]
    gather_layer1 = _gather_comm([own[n] for n in layer1], [lay[n] for n in layer1], forward_half_way=True)

    def reduce_start(g, n):
        theirs = _pair_exchange([g], [lay[n]], f"pair_exchange_{n}")[0]
        psum = _pair_sum(g, theirs, lay[n], core_arr, f"pair_sum_{n}")
        return psum, _scatter_comm([psum], [lay[n]])

    h0, w_ev_in = _adaln_fwd(x2d, g0, shift0, scale0, tr, "adaln0_fwd",
                             comm=_gather_comm([own["ev_w_in"]], [lay["ev_w_in"]]), out_rows=n_ext)
    w_full = {"ev_w_in": w_ev_in}
    h0e = _adaln_fwd(ctx2d, g0, shift_c, scale_c, tr, "adaln0_ctx_fwd", into=h0, row0=n_lat)
    tm_e = _pick(n_ext, (1408, 768, 640, 512, 256, 128))
    tk_e = _pick(n_ext, (768, 640, 512, 256, 128))
    tm_l = _pick(n_lat, (1024, 512, 256, 128))
    p0, *gathered = _mm(h0e, w_full["ev_w_in"], name="mm_ev_in", tm=tm_e, tn=_pick(ev_in, (1408, 512, 256, 128)), tk=d,
                        out_dtype=BF16, comm=gather_ev_out)
    w_full["ev_w_out"] = gathered[0]
    cos_i, sin_s = _rope_tables(n_lat, n_ctx)
    q_hat, k_all, v_all = _qk_prep(p0, cos_i, sin_s, ev_q_norm, ev_k_norm, dims, tr)
    tq = _pick(n_lat, (256, 128))
    tkk = _pick(n_ext, (1408, 640, 512, 256, 128))
    attn, lse, *gathered = _flash_fwd(q_hat, k_all, v_all, n_lat, dims, tq, tkk, comm=gather_layer1)
    w_full.update(zip(layer1, gathered))
    mix, yc = _mix_fwd(attn, p0, dw_w_pad, ev_dw_b, ev_ln_g, ev_ln_b, dims, tr)
    o0, x1 = _mm(mix, w_full["ev_w_out"], name="mm_ev_out", tm=_pick(n_lat, (512, 256, 128)),
                 tn=_pick(d, (2048, 1024, 512, 256)),
                 tk=ev_mix, out_dtype=BF16, res=x2d, gate=gate0)

    h1 = _adaln_fwd(x1, g1, shift1, scale1, tr, "adaln1_fwd")
    p1 = _mm(h1, w_full["od_w_in"], name="mm_od_in", tm=tm_l, tn=_pick(3 * sgu_w, (1536, 512, 256, 128)), tk=d,
             out_dtype=BF16)
    ws_b = od_ws[0].astype(BF16)
    ws_t_b = jnp.swapaxes(od_ws[0], 1, 2).astype(BF16)
    bs_t = od_bs[0].T
    m1 = _sgu_fwd(p1, od_ln_g_full, od_ln_b_full, ws_b, bs_t, tr)
    o1, x2 = _mm(m1, w_full["od_w_out"], name="mm_od_out", tm=_pick(n_lat, (512, 256, 128)),
                 tn=_pick(d, (2048, 1024, 512, 256)),
                 tk=sgu_w, out_dtype=BF16, res=x1, gate=gate1)

    dx2, do1, loss_cols, d_final_g, dgate1 = _final_loss(x2, tgt2d, final_g[None, :], o1, gate1, tr)
    loss = lax.psum(0.5 / d * jnp.sum(loss_cols), ("x", "y", "c"))

    tk_l = _pick(n_lat, (1024, 512, 256, 128))
    psums, slots = {}, {}
    tk_nt = (1408, 1024, 768, 512, 256, 128)
    g_od_w_out = _mm(m1, do1, name="mm_od_out_dw", ta=True, tm=_pick(sgu_w, (2048, 1024, 512, 256)),
                     tn=_pick(d, (1024, 512, 256)), tk=_pick(n_lat, (2048, 1024, 512, 256, 128)), out_dtype=BF16)
    psums["od_w_out"], sc = reduce_start(g_od_w_out, "od_w_out")
    dm1, slots["od_w_out"] = _mm(do1, w_full["od_w_out"], name="mm_od_out_dx", tb=True, tm=tm_l,
                                 tn=_pick(sgu_w, (2048, 1024, 512, 256)), tk=d, out_dtype=BF16, comm=sc)
    dp1, d_ws, d_bs_t, d_od_ln_g, d_od_ln_b = _sgu_bwd(dm1, p1, od_ln_g_full, od_ln_b_full, ws_b, ws_t_b, bs_t, tr)
    small1 = [dgate1, d_final_g, d_od_ln_g, d_od_ln_b, d_ws, d_bs_t.T]
    g_od_w_in, g_small1 = _mm(h1, dp1, name="mm_od_in_dw", ta=True, tm=_pick(d, (2048, 1024, 512, 256)),
                              tn=_pick(3 * sgu_w, (1536, 768, 512, 384, 256, 128)), tk=tk_l, out_dtype=BF16,
                              comm=_allgather_comm(_pack(small1)))
    psums["od_w_in"], sc = reduce_start(g_od_w_in, "od_w_in")
    dh1, slots["od_w_in"] = _mm(dp1, w_full["od_w_in"], name="mm_od_in_dx", tb=True, tm=tm_l,
                                tn=_pick(d, (2048, 1024, 512, 256)), tk=_pick(3 * sgu_w, (2048,) + tk_nt),
                                out_dtype=BF16, comm=sc)
    zero_d = jnp.zeros((1, d), F32)
    b1 = _adaln_bwd(x1, dh1, 0, g1, scale1, zero_d, tr, "adaln1_bwd", dres=dx2, o_prev=o0, gate_prev=gate0)
    dx1, do0, dgate0 = b1["dx"], b1["do_prev"], b1["dgate_prev"]

    g_ev_w_out = _mm(mix, do0, name="mm_ev_out_dw", ta=True, tm=_pick(ev_mix, (2048, 1024, 512, 256)),
                     tn=_pick(d, (1024, 512, 256)), tk=_pick(n_lat, (2048, 1024, 512, 256, 128)), out_dtype=BF16)
    psums["ev_w_out"], sc = reduce_start(g_ev_w_out, "ev_w_out")
    dmix, slots["ev_w_out"] = _mm(do0, w_full["ev_w_out"], name="mm_ev_out_dx", tb=True, tm=tm_l,
                                  tn=_pick(ev_mix, (2048, 1024, 512, 256)), tk=d, out_dtype=BF16, comm=sc)
    dattn, dza, dzb, dyc, d_ev_ln_g, d_ev_ln_b, d_dw_b = _mix_bwd_pointwise(
        dmix, attn, p0, yc, ev_ln_g, ev_ln_b, dims, tr, n_ext)
    dab, d_dw_w_pad = _conv_bwd(dyc, p0, dw_w_pad, dims, tr, n_ext)
    dq_hat, dk_hat, dv_all = _flash_bwd(q_hat, k_all, v_all, dattn, attn, lse, n_lat, dims, tq, tkk)
    dkvq, d_q_norm, d_k_norm = _qk_prep_bwd(dq_hat, dk_hat, dv_all, p0, cos_i, sin_s, ev_q_norm, ev_k_norm,
                                            dims, tr, n_lat)
    dp0 = jnp.concatenate([dkvq, dza, dab, dzb], axis=1)
    small2 = [b1["dshift"], b1["dscale"], dgate0, b1["dg"], d_q_norm, d_k_norm, d_dw_w_pad[:conv_w], d_dw_b,
              d_ev_ln_g, d_ev_ln_b]
    g_ev_w_in, g_small2 = _mm(h0e, dp0, name="mm_ev_in_dw", ta=True, tm=_pick(d, (2048, 1024, 512, 256)),
                              tn=_pick(ev_in, (1408, 768, 512, 256, 128)),
                              tk=tk_e, out_dtype=BF16,
                              comm=_allgather_comm(_pack(small2)))
    psums["ev_w_in"], sc = reduce_start(g_ev_w_in, "ev_w_in")
    dh0, slots["ev_w_in"] = _mm(dp0, w_full["ev_w_in"], name="mm_ev_in_dx", tb=True, tm=tm_e,
                                tn=_pick(d, (2048, 1024, 512, 256)), tk=_pick(ev_in, tk_nt), out_dtype=BF16, comm=sc)
    lays = [lay[n] for n in big]
    halves = [_chip_sum(psums[n], slots[n], lay[n], chip_arr, core_arr, f"chip_sum_{n}") for n in big]
    bc = _adaln_bwd(ctx2d, dh0, n_lat, g0, scale_c, zero_d, tr, "adaln0_ctx_bwd", comm=_share_comm(halves, lays))
    g_big = dict(zip(big, bc["carried"]))
    b0 = _adaln_bwd(x2d, dh0, 0, g0, scale0, bc["dg"], tr, "adaln0_bwd", dres=dx1)
    grad_x = b0["dx"]

    small3 = [b0["dshift"], b0["dscale"], bc["dshift"], bc["dscale"], b0["dg"]]
    g_small3 = _allgather_small(_pack(small3), "gather_small_grads")

    def totals_and_rows(gathered, parts, n_rows, name):
        shapes = [a.shape for a in parts]
        tot = _unpack(_sum_slots(gathered, name), shapes)
        return tot, [r[:, 0, :] for r in _unpack(gathered, shapes[:n_rows])]

    (t_dgate1, t_fg, t_oln_g, t_oln_b, t_ws, t_bs), (dgate1_rows,) = totals_and_rows(
        g_small1, small1, 1, "sum_small_grads1")
    ((t_dshift1, t_dscale1, t_dgate0, t_g1, t_qn, t_kn, t_dw_w, t_dw_b, t_eln_g, t_eln_b),
     (dshift1_rows, dscale1_rows, dgate0_rows)) = totals_and_rows(g_small2, small2, 3, "sum_small_grads2")
    (t_dshift0, t_dscale0, t_dshift_c, t_dscale_c, t_g0), (dshift0_rows, dscale0_rows) = totals_and_rows(
        g_small3, small3, 2, "sum_small_grads3")
    zeros_d = jnp.zeros((1, d), F32)
    t_dmodc = jnp.concatenate([t_dshift_c, t_dscale_c, zeros_d], axis=1)
    t_dmod0 = jnp.concatenate([t_dshift0, t_dscale0, t_dgate0], axis=1)
    t_dmod1 = jnp.concatenate([t_dshift1, t_dscale1, t_dgate1], axis=1)
    dmod0_rows = jnp.concatenate([dshift0_rows, dscale0_rows, dgate0_rows], axis=1)
    dmod1_rows = jnp.concatenate([dshift1_rows, dscale1_rows, dgate1_rows], axis=1)
    pad_rows = jnp.zeros((MOD_ROWS - N_DEV - 1, 3 * d), F32)
    dm_l0 = jnp.concatenate([dmod0_rows, t_dmodc, pad_rows], axis=0)
    dm_l1 = jnp.concatenate([dmod1_rows, jnp.zeros((MOD_ROWS - N_DEV, 3 * d), F32)], axis=0)
    dm_shard = lax.dynamic_slice_in_dim(jnp.stack([dm_l0, dm_l1]), chip * wa, wa, axis=2)
    g_ada_w, dsc = _mod_bwd(c_rows_t, dm_shard, ada_w)
    g_dsc = _allgather_small(_pack([dsc[0]]), "gather_cctx")
    g_c_ctx = _cctx_grad(g_dsc, _pack([c_ctx])).reshape(-1)[:d]
    g_ada_b = jnp.stack([t_dmod0[0] + t_dmodc[0], t_dmod1[0]])

    grads = dict(
        c_ctx=g_c_ctx, ada_w=g_ada_w, ada_b=g_ada_b, norm_g=jnp.concatenate([t_g0, t_g1], axis=0),
        ev_w_in=g_big["ev_w_in"][None], ev_q_norm=t_qn, ev_k_norm=t_kn,
        ev_dw_w=lax.dynamic_slice_in_dim(t_dw_w, chip * dwc, dwc, axis=1)[None], ev_dw_b=t_dw_b,
        ev_ln_g=t_eln_g, ev_ln_b=t_eln_b, ev_w_out=g_big["ev_w_out"][None], od_w_in=g_big["od_w_in"][None],
        od_ln_g=lax.dynamic_slice_in_dim(t_oln_g, chip * lnc, lnc, axis=1),
        od_ln_b=lax.dynamic_slice_in_dim(t_oln_b, chip * lnc, lnc, axis=1),
        od_ws=t_ws[None], od_bs=t_bs[None], od_w_out=g_big["od_w_out"][None], final_g=t_fg[0])
    grads = {n: grads[n].reshape(weights[n].shape) for n in order}

    delta, new_m, new_v = {}, {}, {}
    large = ("ada_w", "ev_w_in", "ev_w_out", "od_w_in", "od_w_out")
    for n in large:
        shp = weights[n].shape
        as2d = lambda a: a.reshape(-1, shp[-1])
        outs = _adamw(as2d(weights[n]), as2d(grads[n]), as2d(mom_m[n]), as2d(mom_v[n]), f"adamw_{n}",
                      copy_grad=n in big)
        delta[n], new_m[n], new_v[n] = (o.reshape(shp) for o in outs[:3])
        if n in big:
            grads[n] = outs[3].reshape(shp)
    rest_names = [n for n in order if n not in large]
    rest_shapes = [weights[n].shape for n in rest_names]
    dl, nm, nv = _adamw(_pack([weights[n] for n in rest_names]), _pack([grads[n] for n in rest_names]),
                        _pack([mom_m[n] for n in rest_names]), _pack([mom_v[n] for n in rest_names]), "adamw_small")
    for n, a, b_, c_ in zip(rest_names, _unpack(dl, rest_shapes), _unpack(nm, rest_shapes), _unpack(nv, rest_shapes)):
        delta[n], new_m[n], new_v[n] = a, b_, c_

    return (loss, grad_x[None], *[grads[n] for n in order], *[delta[n] for n in order],
            *[new_m[n] for n in order], *[new_v[n] for n in order])
```
